```python
import math
import jax, jax.numpy as jnp
from jax import lax
import numpy as np

D_MODEL = 1024
BATCH = 16
SEQ = 2048
DEPTH = 2

N_A_LAYERS = DEPTH // 2
N_B_LAYERS = DEPTH - N_A_LAYERS

RET_HEADS = 4
RET_QK_DIM = D_MODEL // RET_HEADS
RET_V_DIM = 2 * RET_QK_DIM
RET_CHUNK = 128
ROPE_BASE = 10000.0

DIL_WINDOWS = (128, 512, 2048)
DIL_RATES = (1, 4, 16)
N_GROUPS = 3
DIL_HEADS = 16
DIL_HEAD_DIM = D_MODEL // DIL_HEADS
SUB_BLOCK = 128

NUM_BUCKETS = 32
MAX_DISTANCE = 2048

MOE_GROUPS = 4
EXPERTS_PER_GROUP = 8
N_EXPERTS = MOE_GROUPS * EXPERTS_PER_GROUP
TOP_K_INNER = 2
EXPERT_FF = D_MODEL // 2
MOE_BLOCK = 256

EPS = 1e-6
NEG_INF = -1e30

kernel_name = "yoco_retention_dilated_hmoe"


def rmsnorm(x, g):
    xf = x.astype(jnp.float32)
    y = xf * lax.rsqrt(jnp.mean(xf * xf, axis=-1, keepdims=True) + EPS)
    return (y * g.astype(jnp.float32)).astype(x.dtype)


def head_rmsnorm(t, g):
    tf = t.astype(jnp.float32)
    y = tf * lax.rsqrt(jnp.mean(tf * tf, axis=-1, keepdims=True) + EPS)
    return (y * g.astype(jnp.float32)).astype(t.dtype)


def rotary(t, pos):
    half = t.shape[-1] // 2
    inv = 1.0 / (ROPE_BASE ** jnp.linspace(0.0, 1.0, half, dtype=jnp.float32))
    ang = pos[:, None] * inv[None, :]
    cos = jnp.cos(ang)[None, :, None, :]
    sin = jnp.sin(ang)[None, :, None, :]
    tf = t.astype(jnp.float32)
    t1, t2 = tf[..., :half], tf[..., half:]
    return jnp.concatenate([t1 * cos - t2 * sin, t1 * sin + t2 * cos], axis=-1).astype(t.dtype)


def retention(xn, w_in, w_out):
    B, S, D = xn.shape
    H, dk, dv, C = RET_HEADS, RET_QK_DIM, RET_V_DIM, RET_CHUNK
    nC = S // C
    proj = xn @ w_in
    q, k, v, gate = jnp.split(proj, [H * dk, 2 * H * dk, 2 * H * dk + H * dv], axis=-1)
    pos = jnp.arange(S, dtype=jnp.float32)
    q = rotary(q.reshape(B, S, H, dk), pos)
    k = rotary(k.reshape(B, S, H, dk), pos) * (dk ** -0.5)
    v = v.reshape(B, S, H, dv)

    log_g = jnp.log(1.0 - 2.0 ** (-5.0 - jnp.arange(H, dtype=jnp.float32)))
    idx = jnp.arange(C, dtype=jnp.float32)
    diff = idx[:, None] - idx[None, :]
    d_in = jnp.where(diff >= 0, jnp.exp(log_g[:, None, None] * jnp.maximum(diff, 0.0)), 0.0)
    xi = jnp.exp(log_g[:, None] * (idx + 1.0))
    zeta = jnp.exp(log_g[:, None] * (C - 1.0 - idx))
    chunk_decay = jnp.exp(log_g * C)

    def chunks(t):
        return t.reshape(B, nC, C, H, -1).transpose(1, 0, 3, 2, 4).astype(jnp.float32)

    def step(state, qkv):
        qc, kc, vc = qkv
        inner = jnp.einsum('bhnm,bhmv->bhnv', jnp.einsum('bhnd,bhmd->bhnm', qc, kc) * d_in, vc)
        cross = jnp.einsum('bhnd,bhdv->bhnv', qc, state) * xi[None, :, :, None]
        state = state * chunk_decay[None, :, None, None] + jnp.einsum(
            'bhmd,bhmv->bhdv', kc * zeta[None, :, :, None], vc)
        return state, inner + cross

    state0 = jnp.zeros((B, H, dk, dv), jnp.float32)
    _, o = lax.scan(step, state0, (chunks(q), chunks(k), chunks(v)))
    o = o.transpose(1, 0, 3, 2, 4).reshape(B, S, H, dv)
    o = o * lax.rsqrt(jnp.mean(o * o, axis=-1, keepdims=True) + EPS)
    y = jax.nn.silu(gate.astype(jnp.float32)) * o.reshape(B, S, H * dv)
    return y.astype(xn.dtype) @ w_out


def t5_bucket(n):
    max_exact = NUM_BUCKETS // 2
    nf = jnp.maximum(n, max_exact).astype(jnp.float32)
    large = max_exact + (jnp.log(nf / max_exact) / math.log(MAX_DISTANCE / max_exact)
                         * (NUM_BUCKETS - max_exact)).astype(jnp.int32)
    large = jnp.minimum(large, NUM_BUCKETS - 1)
    return jnp.where(n < max_exact, n, large)


def dilated_group(q, k, v, bias_table, rate, n_steps):
    B, S, H, dh = q.shape
    Bk = SUB_BLOCK
    L = S // rate
    nb = -(-L // Bk)
    Lp = nb * Bk

    def to_blocks(t):
        t = t.reshape(B, L, rate, H, dh).transpose(0, 2, 1, 3, 4)
        t = jnp.pad(t, ((0, 0), (0, 0), (0, Lp - L), (0, 0), (0, 0)))
        return t.reshape(B, rate, nb, Bk, H, dh)

    def with_prev(t):
        prev = jnp.pad(t[:, :, :-1], ((0, 0), (0, 0), (1, 0), (0, 0), (0, 0), (0, 0)))
        return jnp.concatenate([prev, t], axis=3)

    qb = to_blocks(q)
    kk = with_prev(to_blocks(k))
    vv = with_prev(to_blocks(v))

    ql = jnp.arange(Bk, dtype=jnp.int32)[:, None]
    kl = jnp.arange(2 * Bk, dtype=jnp.int32)[None, :]
    steps = ql + Bk - kl
    bucket = t5_bucket(jnp.maximum(steps, 0) * rate)
    bias = bias_table[bucket].transpose(2, 0, 1).astype(jnp.float32)
    band = (steps >= 0) & (steps <= n_steps)
    first = (jnp.arange(nb) == 0)[:, None, None]
    valid = band[None] & ~(first & (kl < Bk)[None])

    s = jnp.einsum('brnqhd,brnkhd->brnhqk', qb, kk,
                   preferred_element_type=jnp.float32) * (dh ** -0.5) + bias[None, None, None]
    s = jnp.where(valid[None, None, :, None], s, NEG_INF)
    m = jnp.max(s, axis=-1)
    p = jnp.exp(s - m[..., None])
    l = jnp.sum(p, axis=-1)
    o = jnp.einsum('brnhqk,brnkhd->brnqhd', p, vv.astype(jnp.float32))

    def from_blocks(t):
        t = t.reshape((B, rate, Lp) + t.shape[4:])[:, :, :L]
        return jnp.moveaxis(t, 1, 2).reshape((B, S) + t.shape[3:])

    return (from_blocks(o), from_blocks(jnp.swapaxes(m, 3, 4)),
            from_blocks(jnp.swapaxes(l, 3, 4)))


def shared_kv(h, kv_norm, w_kv, k_norm):
    B, S, _ = h.shape
    kv = (rmsnorm(h, kv_norm) @ w_kv).reshape(B, S, 2, N_GROUPS, DIL_HEADS, DIL_HEAD_DIM)
    k = head_rmsnorm(kv[:, :, 0], k_norm[:, None, :])
    return k, kv[:, :, 1]


def dilated_attention(xn, k, v, w_q, q_norm, w_o, rel_bias):
    B, S, _ = xn.shape
    H, dh = DIL_HEADS, DIL_HEAD_DIM
    q = head_rmsnorm((xn @ w_q).reshape(B, S, N_GROUPS, H, dh), q_norm[:, None, :])
    outs, maxs, dens = [], [], []
    for g in range(N_GROUPS):
        o_g, m_g, l_g = dilated_group(q[:, :, g], k[:, :, g], v[:, :, g],
                                      rel_bias[:, g * H:(g + 1) * H],
                                      DIL_RATES[g], DIL_WINDOWS[g] // DIL_RATES[g])
        outs.append(o_g); maxs.append(m_g); dens.append(l_g)
    ms = jnp.stack(maxs)
    w = jnp.exp(ms - jnp.max(ms, axis=0, keepdims=True))
    num = jnp.sum(w[..., None] * jnp.stack(outs), axis=0)
    den = jnp.sum(w * jnp.stack(dens), axis=0)
    out = (num / den[..., None]).reshape(B, S, H * dh).astype(xn.dtype)
    return out @ w_o


def hier_moe(xt, w_grp, b_grp, w_exp, b_exp, w_gate, w_up, w_down):
    T, D = xt.shape
    grp_logits = (xt @ w_grp).astype(jnp.float32) + b_grp.astype(jnp.float32)
    grp = jnp.argmax(grp_logits, axis=-1)
    p_grp = jnp.take_along_axis(jax.nn.softmax(grp_logits, axis=-1), grp[:, None], axis=1)[:, 0]
    exp_logits = ((xt @ w_exp).astype(jnp.float32) + b_exp.astype(jnp.float32)).reshape(
        T, MOE_GROUPS, EXPERTS_PER_GROUP)
    in_grp = jnp.take_along_axis(exp_logits, grp[:, None, None], axis=1)[:, 0]
    top_v, top_i = lax.top_k(in_grp, TOP_K_INNER)
    gate = jax.nn.softmax(top_v, axis=-1) * p_grp[:, None]

    A = T * TOP_K_INNER
    eid = (grp[:, None] * EXPERTS_PER_GROUP + top_i).reshape(-1).astype(jnp.int32)
    tok = jnp.repeat(jnp.arange(T, dtype=jnp.int32), TOP_K_INNER)
    gw = gate.reshape(-1)
    order = jnp.argsort(eid)
    se = eid[order]
    counts = jnp.bincount(eid, length=N_EXPERTS).astype(jnp.int32)
    starts = jnp.cumsum(counts) - counts
    padded = (counts + MOE_BLOCK - 1) // MOE_BLOCK * MOE_BLOCK
    pends = jnp.cumsum(padded)
    pstarts = pends - padded
    dest = pstarts[se] + jnp.arange(A, dtype=jnp.int32) - starts[se]
    nblk = -(-A // MOE_BLOCK) + N_EXPERTS
    P = nblk * MOE_BLOCK
    rows_tok = jnp.full((P,), T, jnp.int32).at[dest].set(tok[order])
    rows_gate = jnp.zeros((P,), jnp.float32).at[dest].set(gw[order])
    blk_exp = jnp.minimum(jnp.searchsorted(pends, jnp.arange(nblk, dtype=jnp.int32) * MOE_BLOCK,
                                           side='right'), N_EXPERTS - 1)
    x_pad = jnp.concatenate([xt, jnp.zeros((1, D), xt.dtype)], axis=0)
    xb = x_pad[rows_tok].reshape(nblk, MOE_BLOCK, D)

    def expert_block(args):
        xk, e = args
        hid = jax.nn.silu(xk @ w_gate[e]) * (xk @ w_up[e])
        return hid @ w_down[e]

    yb = lax.map(expert_block, (xb, blk_exp)).reshape(P, D)
    y = jax.ops.segment_sum(yb * rows_gate[:, None].astype(yb.dtype), rows_tok,
                            num_segments=T + 1)
    return y[:T]


def setup_inputs(seed: int = 0) -> dict:
    key = jax.random.key(seed)
    ks = jax.random.split(key, 24)
    D = D_MODEL
    f32 = jnp.float32

    def w(k, shape, fan_in):
        return jax.random.normal(k, shape, f32) * (fan_in ** -0.5)

    def gain(k, shape):
        return 1.0 + 0.02 * jax.random.normal(k, shape, f32)

    ret_in_cols = 2 * RET_HEADS * RET_QK_DIM + 2 * RET_HEADS * RET_V_DIM
    q_cols = N_GROUPS * DIL_HEADS * DIL_HEAD_DIM
    return {
        "x": jax.random.normal(ks[0], (BATCH, SEQ, D), f32),
        "ret_w_in": w(ks[1], (N_A_LAYERS, D, ret_in_cols), D),
        "ret_w_out": w(ks[2], (N_A_LAYERS, RET_HEADS * RET_V_DIM, D), RET_HEADS * RET_V_DIM),
        "kv_norm": gain(ks[3], (D,)),
        "w_kv": w(ks[4], (D, 2 * q_cols), D),
        "k_norm": gain(ks[5], (N_GROUPS, DIL_HEAD_DIM)),
        "dil_wq": w(ks[6], (N_B_LAYERS, D, q_cols), D),
        "q_norm": gain(ks[7], (N_B_LAYERS, N_GROUPS, DIL_HEAD_DIM)),
        "dil_wo": w(ks[8], (N_B_LAYERS, DIL_HEADS * DIL_HEAD_DIM, D), DIL_HEADS * DIL_HEAD_DIM),
        "rel_bias": 0.2 * jax.random.normal(ks[9], (NUM_BUCKETS, N_GROUPS * DIL_HEADS), f32),
        "mixer_norm": gain(ks[10], (DEPTH, D)),
        "ffn_norm": gain(ks[11], (DEPTH, D)),
        "router_grp": w(ks[12], (DEPTH, D, MOE_GROUPS), D),
        "router_grp_b": 0.01 * jax.random.normal(ks[13], (DEPTH, MOE_GROUPS), f32),
        "router_exp": w(ks[14], (DEPTH, D, N_EXPERTS), D),
        "router_exp_b": 0.01 * jax.random.normal(ks[15], (DEPTH, N_EXPERTS), f32),
        "exp_gate": w(ks[16], (DEPTH, N_EXPERTS, D, EXPERT_FF), D),
        "exp_up": w(ks[17], (DEPTH, N_EXPERTS, D, EXPERT_FF), D),
        "exp_down": w(ks[18], (DEPTH, N_EXPERTS, EXPERT_FF, D), EXPERT_FF),
    }


def reference(x, ret_w_in, ret_w_out, kv_norm, w_kv, k_norm, dil_wq, q_norm, dil_wo, rel_bias,
              mixer_norm, ffn_norm, router_grp, router_grp_b, router_exp, router_exp_b,
              exp_gate, exp_up, exp_down):
    B, S, D = x.shape
    h = x
    k_sh, v_sh = None, None
    for layer in range(DEPTH):
        xn = rmsnorm(h, mixer_norm[layer])
        if layer < N_A_LAYERS:
            mix = retention(xn, ret_w_in[layer], ret_w_out[layer])
        else:
            j = layer - N_A_LAYERS
            if j == 0:
                k_sh, v_sh = shared_kv(h, kv_norm, w_kv, k_norm)
            mix = dilated_attention(xn, k_sh, v_sh, dil_wq[j], q_norm[j], dil_wo[j], rel_bias)
        h = h + mix
        hn = rmsnorm(h, ffn_norm[layer]).reshape(B * S, D)
        h = h + hier_moe(hn, router_grp[layer], router_grp_b[layer], router_exp[layer],
                         router_exp_b[layer], exp_gate[layer], exp_up[layer],
                         exp_down[layer]).reshape(B, S, D)
    return h
```

```python
import functools
import math

import jax
import jax.numpy as jnp
from jax import lax
from jax.experimental import pallas as pl
from jax.experimental.pallas import tpu as pltpu

F32 = jnp.float32
BF16 = jnp.bfloat16

EPS = 1e-6
NEG_INF = -1e30

RET_HEADS = 4
RET_CHUNK = 128
ROPE_BASE = 10000.0

DIL_WINDOWS = (128, 512, 2048)
DIL_RATES = (1, 4, 16)
DIL_HEADS = 16
DIL_HEAD_DIM = 64
SUB_BLOCK = 128
NUM_BUCKETS = 32
MAX_DISTANCE = 2048

MOE_GROUPS = 4
EXPERTS_PER_GROUP = 8
N_EXPERTS = MOE_GROUPS * EXPERTS_PER_GROUP
MOE_BLOCK = 256

LANES = 128
TOKEN_TILE = 512
VMEM_LIMIT = 56 * 1024 * 1024


def _params(*sem):
    return pltpu.CompilerParams(dimension_semantics=sem, vmem_limit_bytes=VMEM_LIMIT)


def _resident(shape, index_map):
    return pl.BlockSpec(shape, index_map, pipeline_mode=pl.Buffered(1))


def _rms(x):
    return x * lax.rsqrt(jnp.mean(x * x, axis=-1, keepdims=True) + EPS)


def _norm_proj_kernel(x_ref, g_ref, w_ref, o_ref, *, col_tile):
    xn = (_rms(x_ref[...]) * g_ref[...]).astype(BF16)
    for j in range(w_ref.shape[1] // col_tile):
        cols = slice(j * col_tile, (j + 1) * col_tile)
        o_ref[:, cols] = jnp.dot(xn, w_ref[:, cols], preferred_element_type=F32).astype(o_ref.dtype)


def _norm_proj(x, gain, w):
    T, D = x.shape
    N = w.shape[1]
    tm = min(TOKEN_TILE, T)
    return pl.pallas_call(
        functools.partial(_norm_proj_kernel, col_tile=512),
        grid=(T // tm,),
        in_specs=[pl.BlockSpec((tm, D), lambda i: (i, 0)),
                  _resident((1, D), lambda i: (0, 0)),
                  _resident((D, N), lambda i: (0, 0))],
        out_specs=pl.BlockSpec((tm, N), lambda i: (i, 0)),
        out_shape=jax.ShapeDtypeStruct((T, N), BF16),
        compiler_params=_params("parallel"),
        name="norm_proj",
    )(x, gain.reshape(1, D), w)


def _retention_kernel(q_ref, k_ref, v_ref, gate_ref, cos_ref, sin_ref, din_ref, xi_ref, zeta_ref,
                      cd_ref, o_ref, state_ref, *, n_chunks, k_scale):
    C = RET_CHUNK
    half = q_ref.shape[2] // 2
    state_ref[...] = jnp.zeros_like(state_ref)
    d_in = din_ref[0]
    xi = xi_ref[0]
    zeta = zeta_ref[0]
    cd = cd_ref[0]

    def rot(t, cos, sin):
        t1, t2 = t[:, :half], t[:, half:]
        return jnp.concatenate([t1 * cos - t2 * sin, t1 * sin + t2 * cos], axis=-1)

    def body(c, carry):
        rows = pl.ds(pl.multiple_of(c * C, C), C)
        cos = cos_ref[rows, :]
        sin = sin_ref[rows, :]
        q = rot(q_ref[0, rows, :].astype(F32), cos, sin)
        k = rot(k_ref[0, rows, :].astype(F32), cos, sin) * k_scale
        v = v_ref[0, rows, :]
        qb = q.astype(BF16)
        kb = k.astype(BF16)
        s = lax.dot_general(qb, kb, (((1,), (1,)), ((), ())), preferred_element_type=F32) * d_in
        inner = jnp.dot(s.astype(BF16), v, preferred_element_type=F32)
        state = state_ref[...]
        cross = jnp.dot(qb, state.astype(BF16), preferred_element_type=F32) * xi
        kz_t = (k * zeta).T.astype(BF16)
        state_ref[...] = state * cd + jnp.dot(kz_t, v, preferred_element_type=F32)
        o = _rms(inner + cross)
        g = gate_ref[0, rows, :].astype(F32)
        o_ref[0, rows, :] = (g * jax.nn.sigmoid(g) * o).astype(o_ref.dtype)
        return carry

    lax.fori_loop(0, n_chunks, body, 0)


def _retention(proj, B, S):
    H, C = RET_HEADS, RET_CHUNK
    D = proj.shape[1] // 6
    dk, dv = D // H, 2 * D // H
    half = dk // 2
    proj = proj.reshape(B, S, 6 * D)

    pos = jnp.arange(S, dtype=F32)
    inv = 1.0 / (ROPE_BASE ** jnp.linspace(0.0, 1.0, half, dtype=F32))
    ang = pos[:, None] * inv[None, :]
    cos, sin = jnp.cos(ang), jnp.sin(ang)
    log_g = jnp.log(1.0 - 2.0 ** (-5.0 - jnp.arange(H, dtype=F32)))
    idx = jnp.arange(C, dtype=F32)
    diff = idx[:, None] - idx[None, :]
    d_in = jnp.where(diff >= 0, jnp.exp(log_g[:, None, None] * jnp.maximum(diff, 0.0)), 0.0)
    xi = jnp.exp(log_g[:, None] * (idx + 1.0))[:, :, None]
    zeta = jnp.exp(log_g[:, None] * (C - 1.0 - idx))[:, :, None]
    chunk_decay = jnp.exp(log_g * C)[:, None, None]

    kb, vb, gb = H * dk // dk, 2 * H * dk // dv, (2 * H * dk + H * dv) // dv
    out = pl.pallas_call(
        functools.partial(_retention_kernel, n_chunks=S // C, k_scale=dk ** -0.5),
        grid=(B, H),
        in_specs=[pl.BlockSpec((1, S, dk), lambda b, h: (b, 0, h)),
                  pl.BlockSpec((1, S, dk), lambda b, h: (b, 0, kb + h)),
                  pl.BlockSpec((1, S, dv), lambda b, h: (b, 0, vb + h)),
                  pl.BlockSpec((1, S, dv), lambda b, h: (b, 0, gb + h)),
                  _resident((S, half), lambda b, h: (0, 0)),
                  _resident((S, half), lambda b, h: (0, 0)),
                  pl.BlockSpec((1, C, C), lambda b, h: (h, 0, 0)),
                  pl.BlockSpec((1, C, 1), lambda b, h: (h, 0, 0)),
                  pl.BlockSpec((1, C, 1), lambda b, h: (h, 0, 0)),
                  pl.BlockSpec((1, 1, 1), lambda b, h: (h, 0, 0))],
        out_specs=pl.BlockSpec((1, S, dv), lambda b, h: (b, 0, h)),
        out_shape=jax.ShapeDtypeStruct((B, S, H * dv), BF16),
        scratch_shapes=[pltpu.VMEM((dk, dv), F32)],
        compiler_params=_params("parallel", "parallel"),
        name="retention",
    )(proj, proj, proj, proj, cos, sin, d_in, xi, zeta, chunk_decay)
    return out.reshape(B * S, H * dv)


ROUTE_E1, ROUTE_E2, ROUTE_G1, ROUTE_G2, ROUTE_R1, ROUTE_R2 = range(6)
ROUTER_EXP_LANE0 = MOE_GROUPS


def _post_kernel(a_ref, w_ref, h_ref, g_ref, wr_ref, br_ref, tri_ref,
                 h1_ref, hn_ref, route_ref, cnt_ref, carry_ref):
    @pl.when(pl.program_id(0) == 0)
    def _():
        carry_ref[...] = jnp.zeros_like(carry_ref)

    h1 = h_ref[...] + jnp.dot(a_ref[...], w_ref[...], preferred_element_type=F32)
    h1_ref[...] = h1
    hn = _rms(h1) * g_ref[...]
    hn_ref[...] = hn
    logits = jnp.dot(hn, wr_ref[...], preferred_element_type=F32,
                     precision=lax.Precision.HIGHEST) + br_ref[...]
    lane = lax.broadcasted_iota(jnp.int32, logits.shape, 1).astype(F32)
    ninf = -jnp.inf

    def first_argmax(vals):
        top = jnp.max(vals, axis=1, keepdims=True)
        where = jnp.min(jnp.where(vals == top, lane, float(LANES)), axis=1, keepdims=True)
        return top, where

    is_grp = lane < MOE_GROUPS
    lg = jnp.where(is_grp, logits, ninf)
    mg, grp = first_argmax(lg)
    p_grp = 1.0 / jnp.sum(jnp.where(is_grp, jnp.exp(lg - mg), 0.0), axis=1, keepdims=True)

    e_lane = lane - ROUTER_EXP_LANE0
    in_grp = (e_lane < N_EXPERTS) & (jnp.floor(e_lane * (1.0 / EXPERTS_PER_GROUP)) == grp)
    le = jnp.where(in_grp, logits, ninf)
    v1, i1 = first_argmax(le)
    le2 = jnp.where(lane == i1, ninf, le)
    v2, i2 = first_argmax(le2)
    e = jnp.exp(v2 - v1)
    g1 = p_grp / (1.0 + e)
    g2 = p_grp * e / (1.0 + e)
    e1 = i1 - ROUTER_EXP_LANE0
    e2 = i2 - ROUTER_EXP_LANE0

    hit1 = lane == e1
    hit2 = lane == e2
    onehot = jnp.where(hit1 | hit2, 1.0, 0.0)
    before = carry_ref[...] + jnp.dot(tri_ref[...], onehot.astype(BF16), preferred_element_type=F32)
    r1 = jnp.sum(jnp.where(hit1, before, 0.0), axis=1, keepdims=True)
    r2 = jnp.sum(jnp.where(hit2, before, 0.0), axis=1, keepdims=True)
    carry_ref[...] += jnp.sum(onehot, axis=0, keepdims=True)
    cnt_ref[...] = carry_ref[...]

    route = jnp.zeros(logits.shape, F32)
    for slot, val in ((ROUTE_E1, e1), (ROUTE_E2, e2), (ROUTE_G1, g1),
                      (ROUTE_G2, g2), (ROUTE_R1, r1), (ROUTE_R2, r2)):
        route = jnp.where(lane == slot, val, route)
    route_ref[...] = route


def _post(a, w, h, gain, w_grp, b_grp, w_exp, b_exp):
    T, D = h.shape
    K = a.shape[1]
    tm = min(TOKEN_TILE, T)
    n_r = MOE_GROUPS + N_EXPERTS
    wr = jnp.zeros((D, LANES), F32).at[:, :n_r].set(jnp.concatenate([w_grp, w_exp], axis=1))
    br = jnp.zeros((1, LANES), F32).at[0, :n_r].set(jnp.concatenate([b_grp, b_exp]))
    tri = jnp.tril(jnp.ones((tm, tm), BF16), k=-1)
    return pl.pallas_call(
        _post_kernel,
        grid=(T // tm,),
        in_specs=[pl.BlockSpec((tm, K), lambda i: (i, 0)),
                  _resident((K, D), lambda i: (0, 0)),
                  pl.BlockSpec((tm, D), lambda i: (i, 0)),
                  _resident((1, D), lambda i: (0, 0)),
                  _resident((D, LANES), lambda i: (0, 0)),
                  _resident((1, LANES), lambda i: (0, 0)),
                  _resident((tm, tm), lambda i: (0, 0))],
        out_specs=[pl.BlockSpec((tm, D), lambda i: (i, 0)),
                   pl.BlockSpec((tm, D), lambda i: (i, 0)),
                   pl.BlockSpec((tm, LANES), lambda i: (i, 0)),
                   pl.BlockSpec((1, LANES), lambda i: (0, 0))],
        out_shape=[jax.ShapeDtypeStruct((T, D), F32),
                   jax.ShapeDtypeStruct((T, D), F32),
                   jax.ShapeDtypeStruct((T, LANES), F32),
                   jax.ShapeDtypeStruct((1, LANES), F32)],
        scratch_shapes=[pltpu.VMEM((1, LANES), F32)],
        compiler_params=_params("arbitrary"),
        name="post_mixer",
    )(a, w, h, gain.reshape(1, D), wr, br, tri)


def _dispatch_kernel(dest_ref, x_ref, buf_ref, o_ref, sem, *, tm):
    del buf_ref

    def row_copy(j, d):
        return pltpu.make_async_copy(x_ref.at[pl.ds(j, 1)], o_ref.at[pl.ds(d, 1)], sem)

    def issue(j, carry):
        row_copy(j, dest_ref[0, 0, 2 * j]).start()
        row_copy(j, dest_ref[0, 0, 2 * j + 1]).start()
        return carry

    lax.fori_loop(0, tm, issue, 0)
    for _ in range(2):
        pltpu.make_async_copy(x_ref, o_ref.at[pl.ds(0, tm)], sem).wait()


def _dispatch(x, dest, P):
    T, D = x.shape
    tm = min(TOKEN_TILE, T)
    nt = T // tm
    return pl.pallas_call(
        functools.partial(_dispatch_kernel, tm=tm),
        grid=(nt,),
        in_specs=[pl.BlockSpec((1, 1, 2 * tm), lambda i: (i, 0, 0), memory_space=pltpu.SMEM),
                  pl.BlockSpec((tm, D), lambda i: (i, 0)),
                  pl.BlockSpec(memory_space=pl.ANY)],
        out_specs=pl.BlockSpec(memory_space=pl.ANY),
        out_shape=jax.ShapeDtypeStruct((P, D), x.dtype),
        scratch_shapes=[pltpu.SemaphoreType.DMA(())],
        input_output_aliases={2: 0},
        compiler_params=_params("arbitrary"),
        name="dispatch",
    )(dest.reshape(nt, 1, 2 * tm), x, jnp.zeros((P, D), x.dtype))


def _expert_kernel(blk_exp_ref, n_used_ref, x_ref, wg_ref, wu_ref, wd_ref, o_ref):
    del blk_exp_ref
    used = pl.program_id(0) < n_used_ref[0]

    @pl.when(used)
    def _():
        x = x_ref[...].astype(BF16)
        g = jnp.dot(x, wg_ref[0], preferred_element_type=F32)
        u = jnp.dot(x, wu_ref[0], preferred_element_type=F32)
        hid = (g * jax.nn.sigmoid(g) * u).astype(BF16)
        o_ref[...] = jnp.dot(hid, wd_ref[0], preferred_element_type=F32)

    @pl.when(jnp.logical_not(used))
    def _():
        o_ref[...] = jnp.zeros_like(o_ref)


def _experts(xs, blk_exp, n_used, w_gate, w_up, w_down):
    P, D = xs.shape
    FF = w_gate.shape[2]
    nblk = P // MOE_BLOCK

    def x_map(i, be, nu):
        return (jnp.minimum(i, nu[0] - 1), 0)

    def w_map(i, be, nu):
        return (be[jnp.minimum(i, nu[0] - 1)], 0, 0)

    return pl.pallas_call(
        _expert_kernel,
        grid_spec=pltpu.PrefetchScalarGridSpec(
            num_scalar_prefetch=2,
            grid=(nblk,),
            in_specs=[pl.BlockSpec((MOE_BLOCK, D), x_map),
                      pl.BlockSpec((1, D, FF), w_map),
                      pl.BlockSpec((1, D, FF), w_map),
                      pl.BlockSpec((1, FF, D), w_map)],
            out_specs=pl.BlockSpec((MOE_BLOCK, D), lambda i, be, nu: (i, 0))),
        out_shape=jax.ShapeDtypeStruct((P, D), F32),
        compiler_params=_params("arbitrary"),
        name="experts",
    )(blk_exp, n_used, xs, w_gate, w_up, w_down)


def _combine_kernel(dest_ref, h_ref, route_ref, y_ref, o_ref, ya_ref, yb_ref, sem, *, tm):
    def issue(j, carry):
        pltpu.make_async_copy(y_ref.at[pl.ds(dest_ref[0, 0, 2 * j], 1)], ya_ref.at[pl.ds(j, 1)], sem).start()
        pltpu.make_async_copy(y_ref.at[pl.ds(dest_ref[0, 0, 2 * j + 1], 1)], yb_ref.at[pl.ds(j, 1)], sem).start()
        return carry

    lax.fori_loop(0, tm, issue, 0)
    for buf in (ya_ref, yb_ref):
        pltpu.make_async_copy(y_ref.at[pl.ds(0, tm)], buf, sem).wait()
    route = route_ref[...]
    g1 = route[:, ROUTE_G1:ROUTE_G1 + 1]
    g2 = route[:, ROUTE_G2:ROUTE_G2 + 1]
    o_ref[...] = h_ref[...] + (g1 * ya_ref[...] + g2 * yb_ref[...])


def _combine(h, route, ys, dest):
    T, D = h.shape
    tm = min(TOKEN_TILE, T)
    nt = T // tm
    return pl.pallas_call(
        functools.partial(_combine_kernel, tm=tm),
        grid=(nt,),
        in_specs=[pl.BlockSpec((1, 1, 2 * tm), lambda i: (i, 0, 0), memory_space=pltpu.SMEM),
                  pl.BlockSpec((tm, D), lambda i: (i, 0)),
                  pl.BlockSpec((tm, LANES), lambda i: (i, 0)),
                  pl.BlockSpec(memory_space=pl.ANY)],
        out_specs=pl.BlockSpec((tm, D), lambda i: (i, 0)),
        out_shape=jax.ShapeDtypeStruct((T, D), F32),
        scratch_shapes=[pltpu.VMEM((tm, D), F32), pltpu.VMEM((tm, D), F32),
                        pltpu.SemaphoreType.DMA(())],
        compiler_params=_params("arbitrary"),
        name="combine",
    )(dest.reshape(nt, 1, 2 * tm), h, route, ys)


def _moe(h1, hn, route, counts, w_gate, w_up, w_down):
    T, D = h1.shape
    A = 2 * T
    nblk = -(-A // MOE_BLOCK) + N_EXPERTS
    P = nblk * MOE_BLOCK
    eid = route[:, ROUTE_E1:ROUTE_E2 + 1].astype(jnp.int32)
    rank = route[:, ROUTE_R1:ROUTE_R2 + 1].astype(jnp.int32)
    cnt = counts[0, :N_EXPERTS].astype(jnp.int32)
    padded = (cnt + MOE_BLOCK - 1) // MOE_BLOCK * MOE_BLOCK
    pends = jnp.cumsum(padded)
    pstarts = pends - padded
    dest = pstarts[eid] + rank
    blk_exp = jnp.minimum(
        jnp.searchsorted(pends, jnp.arange(nblk, dtype=jnp.int32) * MOE_BLOCK, side='right'),
        N_EXPERTS - 1).astype(jnp.int32)
    n_used = (pends[-1:] // MOE_BLOCK).astype(jnp.int32)
    xs = _dispatch(hn, dest, P)
    ys = _experts(xs, blk_exp, n_used, w_gate.astype(BF16), w_up.astype(BF16), w_down.astype(BF16))
    return _combine(h1, route, ys, dest)


def _qkv_kernel(x_ref, gq_ref, gkv_ref, wq_ref, wk_ref, wv_ref, qn_ref, kn_ref, seg_ref,
                q_ref, k_ref, v_ref, *, n_res):
    D = wq_ref.shape[0]
    seg = seg_ref[...]
    width = seg.shape[0]

    def head_norm(t, gain):
        cols = []
        for j in range(t.shape[1] // width):
            tj = t[:, j * width:(j + 1) * width]
            ms = jnp.dot((tj * tj).astype(BF16), seg, preferred_element_type=F32)
            cols.append(tj * lax.rsqrt(ms + EPS))
        return jnp.concatenate(cols, axis=1) * gain

    for c in range(n_res):
        y = _rms(x_ref[0, :, c * D:(c + 1) * D])
        xq = (y * gq_ref[...]).astype(BF16)
        xkv = (y * gkv_ref[...]).astype(BF16)
        q = jnp.dot(xq, wq_ref[...], preferred_element_type=F32)
        q_ref[c] = head_norm(q, qn_ref[...]).astype(q_ref.dtype)
        k = jnp.dot(xkv, wk_ref[...], preferred_element_type=F32)
        k_ref[c] = head_norm(k, kn_ref[...]).astype(k_ref.dtype)
        v_ref[c] = jnp.dot(xkv, wv_ref[...], preferred_element_type=F32).astype(v_ref.dtype)


def _qkv(h, B, S, rate, gq, gkv, wq, wk, wv, qn, kn):
    D = h.shape[1]
    L = S // rate
    tl = min(TOKEN_TILE, L)
    n_res = max(1, min(rate, TOKEN_TILE // L))
    hd = DIL_HEAD_DIM
    width = 2 * LANES
    ii = jnp.arange(width)
    seg = jnp.where((ii[:, None] // hd) == (ii[None, :] // hd), 1.0 / hd, 0.0).astype(BF16)
    hv = h.reshape(B, L, rate * D)
    row = lambda g: jnp.tile(g, D // hd).reshape(1, D)
    out_spec = pl.BlockSpec((n_res, tl, D), lambda b, c, i: (b * (rate // n_res) + c, i, 0))
    out_shape = jax.ShapeDtypeStruct((B * rate, L, D), BF16)
    return pl.pallas_call(
        functools.partial(_qkv_kernel, n_res=n_res),
        grid=(B, rate // n_res, L // tl),
        in_specs=[pl.BlockSpec((1, tl, n_res * D), lambda b, c, i: (b, i, c)),
                  _resident((1, D), lambda b, c, i: (0, 0)),
                  _resident((1, D), lambda b, c, i: (0, 0)),
                  _resident((D, D), lambda b, c, i: (0, 0)),
                  _resident((D, D), lambda b, c, i: (0, 0)),
                  _resident((D, D), lambda b, c, i: (0, 0)),
                  _resident((1, D), lambda b, c, i: (0, 0)),
                  _resident((1, D), lambda b, c, i: (0, 0)),
                  _resident((width, width), lambda b, c, i: (0, 0))],
        out_specs=[out_spec, out_spec, out_spec],
        out_shape=[out_shape, out_shape, out_shape],
        compiler_params=_params("parallel", "parallel", "parallel"),
        name="qkv_rate%d" % rate,
    )(hv, gq.reshape(1, D), gkv.reshape(1, D), wq, wk, wv, row(qn), row(kn), seg)


def _attn_kernel(q_ref, kp_ref, kc_ref, vp_ref, vc_ref, bias_ref, o_ref, lse_ref, *, n_steps, with_prev):
    Bk = SUB_BLOCK
    hd = DIL_HEAD_DIM
    scale = hd ** -0.5
    n = pl.program_id(1)
    ql = lax.broadcasted_iota(jnp.int32, (Bk, Bk), 0)
    kl = lax.broadcasted_iota(jnp.int32, (Bk, Bk), 1)
    steps_cur = ql - kl
    valid_cur = (steps_cur >= 0) & (steps_cur <= n_steps)
    steps_prev = steps_cur + Bk
    valid_prev = (steps_prev <= n_steps) & (n > 0)
    nt = (((1,), (1,)), ((), ()))
    lse_ref[0] = jnp.zeros(lse_ref.shape[1:], F32)
    for hh in range(DIL_HEADS):
        cols = slice(hh * hd, (hh + 1) * hd)
        q = q_ref[0, :, cols]
        s_c = lax.dot_general(q, kc_ref[0, :, cols], nt, preferred_element_type=F32) * scale
        s_c = jnp.where(valid_cur, s_c + bias_ref[hh, :, Bk:], NEG_INF)
        m = jnp.max(s_c, axis=1, keepdims=True)
        if with_prev:
            s_p = lax.dot_general(q, kp_ref[0, :, cols], nt, preferred_element_type=F32) * scale
            s_p = jnp.where(valid_prev, s_p + bias_ref[hh, :, :Bk], NEG_INF)
            m = jnp.maximum(m, jnp.max(s_p, axis=1, keepdims=True))
        p_c = jnp.exp(s_c - m)
        l = jnp.sum(p_c, axis=1, keepdims=True)
        o = jnp.dot(p_c.astype(BF16), vc_ref[0, :, cols], preferred_element_type=F32)
        if with_prev:
            p_p = jnp.exp(s_p - m)
            l = l + jnp.sum(p_p, axis=1, keepdims=True)
            o = o + jnp.dot(p_p.astype(BF16), vp_ref[0, :, cols], preferred_element_type=F32)
        o_ref[0, :, cols] = (o / l).astype(o_ref.dtype)
        lse_ref[0, :, hh:hh + 1] = m + jnp.log(l)


def _t5_bucket(n):
    max_exact = NUM_BUCKETS // 2
    nf = jnp.maximum(n, max_exact).astype(F32)
    large = max_exact + (jnp.log(nf / max_exact) / math.log(MAX_DISTANCE / max_exact)
                         * (NUM_BUCKETS - max_exact)).astype(jnp.int32)
    large = jnp.minimum(large, NUM_BUCKETS - 1)
    return jnp.where(n < max_exact, n, large)


def _group_attention(q, k, v, bias_table, rate, n_steps):
    R, L, D = q.shape
    Bk = SUB_BLOCK
    nb = L // Bk
    with_prev = nb > 1
    ql = jnp.arange(Bk, dtype=jnp.int32)[:, None]
    kl = jnp.arange(2 * Bk, dtype=jnp.int32)[None, :]
    bucket = _t5_bucket(jnp.maximum(ql + Bk - kl, 0) * rate)
    bias = bias_table[bucket].transpose(2, 0, 1).astype(F32)
    cur = pl.BlockSpec((1, Bk, D), lambda r, n: (r, n, 0))
    prev = pl.BlockSpec((1, Bk, D), lambda r, n: (r, jnp.maximum(n - 1, 0), 0))
    return pl.pallas_call(
        functools.partial(_attn_kernel, n_steps=n_steps, with_prev=with_prev),
        grid=(R, nb),
        in_specs=[cur, prev, cur, prev, cur,
                  _resident((DIL_HEADS, Bk, 2 * Bk), lambda r, n: (0, 0, 0))],
        out_specs=[pl.BlockSpec((1, Bk, D), lambda r, n: (r, n, 0)),
                   pl.BlockSpec((1, Bk, LANES), lambda r, n: (r, n, 0))],
        out_shape=[jax.ShapeDtypeStruct((R, L, D), BF16),
                   jax.ShapeDtypeStruct((R, L, LANES), F32)],
        compiler_params=_params("parallel", "parallel"),
        name="attn_rate%d" % rate,
    )(q, k, k, v, v, bias)


def _merge_kernel(o0_ref, o1_ref, o2_ref, l0_ref, l1_ref, l2_ref, ex_ref, out_ref, *, n_res, D):
    ex = ex_ref[...]

    def expand(w):
        hi = w.astype(BF16)
        r1 = w - hi.astype(F32)
        mid = r1.astype(BF16)
        lo = (r1 - mid.astype(F32)).astype(BF16)
        return (jnp.dot(hi, ex, preferred_element_type=F32) + jnp.dot(mid, ex, preferred_element_type=F32)
                + jnp.dot(lo, ex, preferred_element_type=F32))

    for c in range(n_res):
        lanes = slice(c * LANES, (c + 1) * LANES)
        cols = slice(c * D, (c + 1) * D)
        l0, l1, l2 = l0_ref[0, :, lanes], l1_ref[0, c], l2_ref[0, 0, c]
        top = jnp.maximum(jnp.maximum(l0, l1), l2)
        w0, w1, w2 = jnp.exp(l0 - top), jnp.exp(l1 - top), jnp.exp(l2 - top)
        den = w0 + w1 + w2
        acc = expand(w0 / den) * o0_ref[0, :, cols].astype(F32)
        acc = acc + expand(w1 / den) * o1_ref[0, c].astype(F32)
        acc = acc + expand(w2 / den) * o2_ref[0, 0, c].astype(F32)
        out_ref[0, :, cols] = acc.astype(out_ref.dtype)


def _merge(outs, lses, B, S, D):
    r1, r2 = DIL_RATES[1], DIL_RATES[2]
    n_res = r2 // r1
    L2 = S // r2
    ex = jnp.where(jnp.arange(LANES)[:, None] == (jnp.arange(D)[None, :] // DIL_HEAD_DIM), 1.0, 0.0).astype(BF16)

    def views(t, W):
        return (t[0].reshape(B, L2, r2 * W),
                t[1].reshape(B, r1, L2, n_res * W),
                t[2].reshape(B, r2 // n_res, n_res, L2, W))

    o0, o1, o2 = views(outs, D)
    l0, l1, l2 = views(lses, LANES)

    def specs(W):
        return [pl.BlockSpec((1, L2, n_res * W), lambda b, j: (b, 0, j)),
                pl.BlockSpec((1, r1, L2, W), lambda b, j: (b, 0, 0, j)),
                pl.BlockSpec((1, 1, n_res, L2, W), lambda b, j: (b, j, 0, 0, 0))]

    out = pl.pallas_call(
        functools.partial(_merge_kernel, n_res=n_res, D=D),
        grid=(B, r2 // n_res),
        in_specs=specs(D) + specs(LANES) + [_resident((LANES, D), lambda b, j: (0, 0))],
        out_specs=pl.BlockSpec((1, L2, n_res * D), lambda b, j: (b, 0, j)),
        out_shape=jax.ShapeDtypeStruct((B, L2, r2 * D), BF16),
        compiler_params=_params("parallel", "parallel"),
        name="merge_groups",
    )(o0, o1, o2, l0, l1, l2, ex)
    return out.reshape(B * S, D)


def kernel(x, ret_w_in, ret_w_out, kv_norm, w_kv, k_norm, dil_wq, q_norm, dil_wo, rel_bias,
           mixer_norm, ffn_norm, router_grp, router_grp_b, router_exp, router_exp_b,
           exp_gate, exp_up, exp_down):
    B, S, D = x.shape
    h = x.reshape(B * S, D)

    def moe_layer(layer, a, w_out, h):
        h1, hn, route, counts = _post(a, w_out.astype(BF16), h, ffn_norm[layer], router_grp[layer],
                                      router_grp_b[layer], router_exp[layer], router_exp_b[layer])
        return _moe(h1, hn, route, counts, exp_gate[layer], exp_up[layer], exp_down[layer])

    proj = _norm_proj(h, mixer_norm[0], ret_w_in[0].astype(BF16))
    y = _retention(proj, B, S)
    h = moe_layer(0, y, ret_w_out[0], h)

    G = len(DIL_RATES)
    gd = DIL_HEADS * DIL_HEAD_DIM
    outs, lses = [], []
    for g in range(G):
        cq = slice(g * gd, (g + 1) * gd)
        ck = cq
        cv = slice(G * gd + g * gd, G * gd + (g + 1) * gd)
        q, k, v = _qkv(h, B, S, DIL_RATES[g], mixer_norm[1], kv_norm,
                       dil_wq[0][:, cq].astype(BF16), w_kv[:, ck].astype(BF16), w_kv[:, cv].astype(BF16),
                       q_norm[0][g], k_norm[g])
        o, lse = _group_attention(q, k, v, rel_bias[:, g * DIL_HEADS:(g + 1) * DIL_HEADS],
                                  DIL_RATES[g], DIL_WINDOWS[g] // DIL_RATES[g])
        outs.append(o)
        lses.append(lse)
    att = _merge(outs, lses, B, S, D)
    h = moe_layer(1, att, dil_wo[0], h)
    return h.reshape(B, S, D)
```

```python
import functools
import math

import jax
import jax.numpy as jnp
from jax import lax
from jax.experimental import pallas as pl
from jax.experimental.pallas import tpu as pltpu

F32 = jnp.float32
BF16 = jnp.bfloat16

EPS = 1e-6
NEG_INF = -1e30

RET_HEADS = 4
RET_CHUNK = 128
ROPE_BASE = 10000.0

DIL_WINDOWS = (128, 512, 2048)
DIL_RATES = (1, 4, 16)
DIL_HEADS = 16
DIL_HEAD_DIM = 64
SUB_BLOCK = 128
NUM_BUCKETS = 32
MAX_DISTANCE = 2048

MOE_GROUPS = 4
EXPERTS_PER_GROUP = 8
N_EXPERTS = MOE_GROUPS * EXPERTS_PER_GROUP
MOE_BLOCK = 256

LANES = 128
TOKEN_TILE = 512
VMEM_LIMIT = 56 * 1024 * 1024

NT_DIMS = (((1,), (1,)), ((), ()))


def _params(*sem):
    return pltpu.CompilerParams(dimension_semantics=sem, vmem_limit_bytes=VMEM_LIMIT)


def _resident(shape, index_map):
    return pl.BlockSpec(shape, index_map, pipeline_mode=pl.Buffered(1))


def _rms(x):
    return x * lax.rsqrt(jnp.mean(x * x, axis=-1, keepdims=True) + EPS)


def _norm_proj_kernel(x_ref, g_ref, w_ref, o_ref, *, col_tile):
    xn = (_rms(x_ref[...]) * g_ref[...]).astype(BF16)
    for j in range(w_ref.shape[1] // col_tile):
        cols = slice(j * col_tile, (j + 1) * col_tile)
        o_ref[:, cols] = jnp.dot(xn, w_ref[:, cols], preferred_element_type=F32).astype(o_ref.dtype)


def _norm_proj(x, gain, w):
    T, D = x.shape
    N = w.shape[1]
    tm = min(TOKEN_TILE, T)
    return pl.pallas_call(
        functools.partial(_norm_proj_kernel, col_tile=512),
        grid=(T // tm,),
        in_specs=[pl.BlockSpec((tm, D), lambda i: (i, 0)),
                  _resident((1, D), lambda i: (0, 0)),
                  _resident((D, N), lambda i: (0, 0))],
        out_specs=pl.BlockSpec((tm, N), lambda i: (i, 0)),
        out_shape=jax.ShapeDtypeStruct((T, N), BF16),
        compiler_params=_params("parallel"),
        name="norm_proj",
    )(x, gain.reshape(1, D), w)


def _retention_kernel(q_ref, k_ref, v_ref, gate_ref, cos_ref, sin_ref, din_ref, xi_ref, zeta_ref,
                      cd_ref, o_ref, state_ref, *, n_chunks, k_scale):
    C = RET_CHUNK
    half = q_ref.shape[2] // 2
    state_ref[...] = jnp.zeros_like(state_ref)
    d_in = din_ref[0]
    xi = xi_ref[0]
    zeta = zeta_ref[0]
    cd = cd_ref[0]

    def rot(t, cos, sin):
        t1, t2 = t[:, :half], t[:, half:]
        return jnp.concatenate([t1 * cos - t2 * sin, t1 * sin + t2 * cos], axis=-1)

    def body(c, carry):
        rows = pl.ds(pl.multiple_of(c * C, C), C)
        cos = cos_ref[rows, :]
        sin = sin_ref[rows, :]
        q = rot(q_ref[0, rows, :].astype(F32), cos, sin)
        k = rot(k_ref[0, rows, :].astype(F32), cos, sin) * k_scale
        v = v_ref[0, rows, :]
        qb = q.astype(BF16)
        kb = k.astype(BF16)
        s = lax.dot_general(qb, kb, NT_DIMS, preferred_element_type=F32) * d_in
        inner = jnp.dot(s.astype(BF16), v, preferred_element_type=F32)
        state = state_ref[...]
        cross = jnp.dot(qb, state.astype(BF16), preferred_element_type=F32) * xi
        kz_t = (k * zeta).T.astype(BF16)
        state_ref[...] = state * cd + jnp.dot(kz_t, v, preferred_element_type=F32)
        o = _rms(inner + cross)
        g = gate_ref[0, rows, :].astype(F32)
        o_ref[0, rows, :] = (g * jax.nn.sigmoid(g) * o).astype(o_ref.dtype)
        return carry

    lax.fori_loop(0, n_chunks, body, 0)


def _retention(proj, B, S):
    H, C = RET_HEADS, RET_CHUNK
    D = proj.shape[1] // 6
    dk, dv = D // H, 2 * D // H
    half = dk // 2
    proj = proj.reshape(B, S, 6 * D)

    pos = jnp.arange(S, dtype=F32)
    inv = 1.0 / (ROPE_BASE ** jnp.linspace(0.0, 1.0, half, dtype=F32))
    ang = pos[:, None] * inv[None, :]
    cos, sin = jnp.cos(ang), jnp.sin(ang)
    log_g = jnp.log(1.0 - 2.0 ** (-5.0 - jnp.arange(H, dtype=F32)))
    idx = jnp.arange(C, dtype=F32)
    diff = idx[:, None] - idx[None, :]
    d_in = jnp.where(diff >= 0, jnp.exp(log_g[:, None, None] * jnp.maximum(diff, 0.0)), 0.0)
    xi = jnp.exp(log_g[:, None] * (idx + 1.0))[:, :, None]
    zeta = jnp.exp(log_g[:, None] * (C - 1.0 - idx))[:, :, None]
    chunk_decay = jnp.exp(log_g * C)[:, None, None]

    kb, vb, gb = H * dk // dk, 2 * H * dk // dv, (2 * H * dk + H * dv) // dv
    out = pl.pallas_call(
        functools.partial(_retention_kernel, n_chunks=S // C, k_scale=dk ** -0.5),
        grid=(B, H),
        in_specs=[pl.BlockSpec((1, S, dk), lambda b, h: (b, 0, h)),
                  pl.BlockSpec((1, S, dk), lambda b, h: (b, 0, kb + h)),
                  pl.BlockSpec((1, S, dv), lambda b, h: (b, 0, vb + h)),
                  pl.BlockSpec((1, S, dv), lambda b, h: (b, 0, gb + h)),
                  _resident((S, half), lambda b, h: (0, 0)),
                  _resident((S, half), lambda b, h: (0, 0)),
                  pl.BlockSpec((1, C, C), lambda b, h: (h, 0, 0)),
                  pl.BlockSpec((1, C, 1), lambda b, h: (h, 0, 0)),
                  pl.BlockSpec((1, C, 1), lambda b, h: (h, 0, 0)),
                  pl.BlockSpec((1, 1, 1), lambda b, h: (h, 0, 0))],
        out_specs=pl.BlockSpec((1, S, dv), lambda b, h: (b, 0, h)),
        out_shape=jax.ShapeDtypeStruct((B, S, H * dv), BF16),
        scratch_shapes=[pltpu.VMEM((dk, dv), F32)],
        compiler_params=_params("parallel", "parallel"),
        name="retention",
    )(proj, proj, proj, proj, cos, sin, d_in, xi, zeta, chunk_decay)
    return out.reshape(B * S, H * dv)


ROUTE_E1, ROUTE_E2, ROUTE_G1, ROUTE_G2, ROUTE_R1, ROUTE_R2 = range(6)
ROUTE_ROWS = 8
ROUTER_EXP_LANE0 = MOE_GROUPS


def _post_kernel(a_ref, w_ref, h_ref, g_ref, wr_ref, br_ref, tri_ref,
                 h1_ref, hn_ref, route_ref, route_t_ref, cnt_ref, carry_ref):
    @pl.when(pl.program_id(0) == 0)
    def _():
        carry_ref[...] = jnp.zeros_like(carry_ref)

    h1 = h_ref[...] + jnp.dot(a_ref[...], w_ref[...], preferred_element_type=F32)
    h1_ref[...] = h1
    hn = _rms(h1) * g_ref[...]
    hn_ref[...] = hn
    logits = jnp.dot(hn, wr_ref[...], preferred_element_type=F32,
                     precision=lax.Precision.HIGHEST) + br_ref[...]
    lane = lax.broadcasted_iota(jnp.int32, logits.shape, 1).astype(F32)
    ninf = -jnp.inf

    def first_argmax(vals):
        top = jnp.max(vals, axis=1, keepdims=True)
        where = jnp.min(jnp.where(vals == top, lane, float(LANES)), axis=1, keepdims=True)
        return top, where

    is_grp = lane < MOE_GROUPS
    lg = jnp.where(is_grp, logits, ninf)
    mg, grp = first_argmax(lg)
    p_grp = 1.0 / jnp.sum(jnp.where(is_grp, jnp.exp(lg - mg), 0.0), axis=1, keepdims=True)

    e_lane = lane - ROUTER_EXP_LANE0
    in_grp = (e_lane < N_EXPERTS) & (jnp.floor(e_lane * (1.0 / EXPERTS_PER_GROUP)) == grp)
    le = jnp.where(in_grp, logits, ninf)
    v1, i1 = first_argmax(le)
    le2 = jnp.where(lane == i1, ninf, le)
    v2, i2 = first_argmax(le2)
    e = jnp.exp(v2 - v1)
    g1 = p_grp / (1.0 + e)
    g2 = p_grp * e / (1.0 + e)
    e1 = i1 - ROUTER_EXP_LANE0
    e2 = i2 - ROUTER_EXP_LANE0

    hit1 = lane == e1
    hit2 = lane == e2
    onehot = jnp.where(hit1 | hit2, 1.0, 0.0)
    before = carry_ref[...] + jnp.dot(tri_ref[...], onehot.astype(BF16), preferred_element_type=F32)
    r1 = jnp.sum(jnp.where(hit1, before, 0.0), axis=1, keepdims=True)
    r2 = jnp.sum(jnp.where(hit2, before, 0.0), axis=1, keepdims=True)
    carry_ref[...] += jnp.sum(onehot, axis=0, keepdims=True)
    cnt_ref[...] = carry_ref[...]

    route = jnp.zeros(logits.shape, F32)
    for slot, val in ((ROUTE_E1, e1), (ROUTE_E2, e2), (ROUTE_G1, g1),
                      (ROUTE_G2, g2), (ROUTE_R1, r1), (ROUTE_R2, r2)):
        route = jnp.where(lane == slot, val, route)
    route_ref[...] = route
    route_t_ref[...] = route.T[:ROUTE_ROWS]


def _post(a, w, h, gain, w_grp, b_grp, w_exp, b_exp):
    T, D = h.shape
    K = a.shape[1]
    tm = min(TOKEN_TILE, T)
    n_r = MOE_GROUPS + N_EXPERTS
    wr = jnp.zeros((D, LANES), F32).at[:, :n_r].set(jnp.concatenate([w_grp, w_exp], axis=1))
    br = jnp.zeros((1, LANES), F32).at[0, :n_r].set(jnp.concatenate([b_grp, b_exp]))
    tri = jnp.tril(jnp.ones((tm, tm), BF16), k=-1)
    return pl.pallas_call(
        _post_kernel,
        grid=(T // tm,),
        in_specs=[pl.BlockSpec((tm, K), lambda i: (i, 0)),
                  _resident((K, D), lambda i: (0, 0)),
                  pl.BlockSpec((tm, D), lambda i: (i, 0)),
                  _resident((1, D), lambda i: (0, 0)),
                  _resident((D, LANES), lambda i: (0, 0)),
                  _resident((1, LANES), lambda i: (0, 0)),
                  _resident((tm, tm), lambda i: (0, 0))],
        out_specs=[pl.BlockSpec((tm, D), lambda i: (i, 0)),
                   pl.BlockSpec((tm, D), lambda i: (i, 0)),
                   pl.BlockSpec((tm, LANES), lambda i: (i, 0)),
                   pl.BlockSpec((ROUTE_ROWS, tm), lambda i: (0, i)),
                   pl.BlockSpec((1, LANES), lambda i: (0, 0))],
        out_shape=[jax.ShapeDtypeStruct((T, D), F32),
                   jax.ShapeDtypeStruct((T, D), F32),
                   jax.ShapeDtypeStruct((T, LANES), F32),
                   jax.ShapeDtypeStruct((ROUTE_ROWS, T), F32),
                   jax.ShapeDtypeStruct((1, LANES), F32)],
        scratch_shapes=[pltpu.VMEM((1, LANES), F32)],
        compiler_params=_params("arbitrary"),
        name="post_mixer",
    )(a, w, h, gain.reshape(1, D), wr, br, tri)


def _dispatch_kernel(dest_ref, x_ref, buf_ref, o_ref, sem, *, tm):
    del buf_ref

    def row_copy(j, d):
        return pltpu.make_async_copy(x_ref.at[pl.ds(j, 1)], o_ref.at[pl.ds(d, 1)], sem)

    def issue(j, carry):
        row_copy(j, dest_ref[0, 0, j]).start()
        row_copy(j, dest_ref[0, 0, tm + j]).start()
        return carry

    lax.fori_loop(0, tm, issue, 0)
    for _ in range(2):
        pltpu.make_async_copy(x_ref, o_ref.at[pl.ds(0, tm)], sem).wait()


def _dispatch(x, dest, P):
    T, D = x.shape
    tm = dest.shape[2] // 2
    nt = T // tm
    return pl.pallas_call(
        functools.partial(_dispatch_kernel, tm=tm),
        grid=(nt,),
        in_specs=[pl.BlockSpec((1, 1, 2 * tm), lambda i: (i, 0, 0), memory_space=pltpu.SMEM),
                  pl.BlockSpec((tm, D), lambda i: (i, 0)),
                  pl.BlockSpec(memory_space=pl.ANY)],
        out_specs=pl.BlockSpec(memory_space=pl.ANY),
        out_shape=jax.ShapeDtypeStruct((P, D), x.dtype),
        scratch_shapes=[pltpu.SemaphoreType.DMA(())],
        input_output_aliases={2: 0},
        compiler_params=_params("arbitrary"),
        name="dispatch",
    )(dest, x, jnp.zeros((P, D), x.dtype))


def _expert_kernel(blk_exp_ref, n_used_ref, x_ref, wg_ref, wu_ref, wd_ref, o_ref,
                   wg_s, wu_s, wd_s):
    i = pl.program_id(0)
    used = i < n_used_ref[0]
    new_expert = (i == 0) | (blk_exp_ref[i] != blk_exp_ref[jnp.maximum(i - 1, 0)])

    @pl.when(used & new_expert)
    def _():
        wg_s[...] = wg_ref[0].astype(BF16)
        wu_s[...] = wu_ref[0].astype(BF16)
        wd_s[...] = wd_ref[0].astype(BF16)

    @pl.when(used)
    def _():
        x = x_ref[...].astype(BF16)
        g = jnp.dot(x, wg_s[...], preferred_element_type=F32)
        u = jnp.dot(x, wu_s[...], preferred_element_type=F32)
        hid = (g * jax.nn.sigmoid(g) * u).astype(BF16)
        o_ref[...] = jnp.dot(hid, wd_s[...], preferred_element_type=F32)

    @pl.when(jnp.logical_not(used))
    def _():
        o_ref[...] = jnp.zeros_like(o_ref)


def _experts(xs, blk_exp, n_used, w_gate, w_up, w_down):
    P, D = xs.shape
    FF = w_gate.shape[2]
    nblk = P // MOE_BLOCK

    def x_map(i, be, nu):
        return (jnp.minimum(i, nu[0] - 1), 0)

    def w_map(i, be, nu):
        return (be[jnp.minimum(i, nu[0] - 1)], 0, 0)

    return pl.pallas_call(
        _expert_kernel,
        grid_spec=pltpu.PrefetchScalarGridSpec(
            num_scalar_prefetch=2,
            grid=(nblk,),
            in_specs=[pl.BlockSpec((MOE_BLOCK, D), x_map),
                      pl.BlockSpec((1, D, FF), w_map),
                      pl.BlockSpec((1, D, FF), w_map),
                      pl.BlockSpec((1, FF, D), w_map)],
            out_specs=pl.BlockSpec((MOE_BLOCK, D), lambda i, be, nu: (i, 0)),
            scratch_shapes=[pltpu.VMEM((D, FF), BF16), pltpu.VMEM((D, FF), BF16),
                            pltpu.VMEM((FF, D), BF16)]),
        out_shape=jax.ShapeDtypeStruct((P, D), F32),
        compiler_params=_params("arbitrary"),
        name="experts",
    )(blk_exp, n_used, xs, w_gate, w_up, w_down)


def _combine_kernel(dest_ref, h_ref, route_ref, y_ref, o_ref, ya_ref, yb_ref, sem, *, tm):
    def issue(j, carry):
        pltpu.make_async_copy(y_ref.at[pl.ds(dest_ref[0, 0, j], 1)], ya_ref.at[pl.ds(j, 1)], sem).start()
        pltpu.make_async_copy(y_ref.at[pl.ds(dest_ref[0, 0, tm + j], 1)], yb_ref.at[pl.ds(j, 1)], sem).start()
        return carry

    lax.fori_loop(0, tm, issue, 0)
    for buf in (ya_ref, yb_ref):
        pltpu.make_async_copy(y_ref.at[pl.ds(0, tm)], buf, sem).wait()
    route = route_ref[...]
    g1 = route[:, ROUTE_G1:ROUTE_G1 + 1]
    g2 = route[:, ROUTE_G2:ROUTE_G2 + 1]
    o_ref[...] = h_ref[...] + (g1 * ya_ref[...] + g2 * yb_ref[...])


def _combine(h, route, ys, dest):
    T, D = h.shape
    tm = dest.shape[2] // 2
    nt = T // tm
    return pl.pallas_call(
        functools.partial(_combine_kernel, tm=tm),
        grid=(nt,),
        in_specs=[pl.BlockSpec((1, 1, 2 * tm), lambda i: (i, 0, 0), memory_space=pltpu.SMEM),
                  pl.BlockSpec((tm, D), lambda i: (i, 0)),
                  pl.BlockSpec((tm, LANES), lambda i: (i, 0)),
                  pl.BlockSpec(memory_space=pl.ANY)],
        out_specs=pl.BlockSpec((tm, D), lambda i: (i, 0)),
        out_shape=jax.ShapeDtypeStruct((T, D), F32),
        scratch_shapes=[pltpu.VMEM((tm, D), F32), pltpu.VMEM((tm, D), F32),
                        pltpu.SemaphoreType.DMA(())],
        compiler_params=_params("arbitrary"),
        name="combine",
    )(dest, h, route, ys)


def _moe(h1, hn, route, route_t, counts, w_gate, w_up, w_down):
    T, D = h1.shape
    tm = min(TOKEN_TILE, T)
    nt = T // tm
    A = 2 * T
    nblk = -(-A // MOE_BLOCK) + N_EXPERTS
    P = nblk * MOE_BLOCK
    eid = route_t[ROUTE_E1:ROUTE_E2 + 1].astype(jnp.int32)
    rank = route_t[ROUTE_R1:ROUTE_R2 + 1].astype(jnp.int32)
    cnt = counts[0, :N_EXPERTS].astype(jnp.int32)
    padded = (cnt + MOE_BLOCK - 1) // MOE_BLOCK * MOE_BLOCK
    pends = jnp.cumsum(padded)
    pstarts = pends - padded
    experts = jnp.arange(N_EXPERTS, dtype=jnp.int32)
    start_of = jnp.sum(jnp.where(eid[:, None, :] == experts[None, :, None], pstarts[None, :, None], 0), axis=1)
    dest = start_of + rank
    dest = dest.reshape(2, nt, tm).transpose(1, 0, 2).reshape(nt, 1, 2 * tm)
    blk_start = jnp.arange(nblk, dtype=jnp.int32) * MOE_BLOCK
    blk_exp = jnp.minimum(jnp.sum((pends[None, :] <= blk_start[:, None]).astype(jnp.int32), axis=1),
                          N_EXPERTS - 1)
    n_used = pends[-1:] // MOE_BLOCK
    xs = _dispatch(hn, dest, P)
    ys = _experts(xs, blk_exp, n_used, w_gate, w_up, w_down)
    return _combine(h1, route, ys, dest)


def _qkv_kernel(x_ref, gq_ref, gkv_ref, wq_ref, wk_ref, wvt_ref, qn_ref, kn_ref, seg_ref,
                q_ref, k_ref, vt_ref, xs_ref, *, rate, n, res_per_chunk):
    seg = seg_ref[...]
    width = seg.shape[0]

    def head_norm(t, gain):
        cols = []
        for j in range(t.shape[1] // width):
            tj = t[:, j * width:(j + 1) * width]
            ms = jnp.dot((tj * tj).astype(BF16), seg, preferred_element_type=F32)
            cols.append(tj * lax.rsqrt(ms + EPS))
        return jnp.concatenate(cols, axis=1) * gain

    def lane_chunk(j):
        return slice(j * LANES, (j + 1) * LANES)

    if rate > 1:
        for j in range(xs_ref.shape[0]):
            xs_ref[j] = x_ref[:, lane_chunk(j)]

    for c0 in range(0, rate, res_per_chunk):
        residues = range(c0, c0 + res_per_chunk)
        if rate == 1:
            x = x_ref[...]
        else:
            x = jnp.concatenate(
                [jnp.concatenate([xs_ref[j, pl.ds(c, n, stride=rate), :] for j in range(xs_ref.shape[0])], axis=1)
                 for c in residues], axis=0)
        y = _rms(x)
        xq = (y * gq_ref[...]).astype(BF16)
        xkv = (y * gkv_ref[...]).astype(BF16)
        q = head_norm(jnp.dot(xq, wq_ref[...], preferred_element_type=F32), qn_ref[...]).astype(q_ref.dtype)
        k = head_norm(jnp.dot(xkv, wk_ref[...], preferred_element_type=F32), kn_ref[...]).astype(k_ref.dtype)
        vt = lax.dot_general(wvt_ref[...], xkv, NT_DIMS, preferred_element_type=F32).astype(vt_ref.dtype)
        for j, c in enumerate(residues):
            q_ref[0, c] = q[j * n:(j + 1) * n]
            k_ref[0, c] = k[j * n:(j + 1) * n]
            vt_ref[0, c] = vt[:, j * n:(j + 1) * n]


def _qkv(h, B, S, rate, gq, gkv, wq, wk, wv, qn, kn):
    D = h.shape[1]
    L = S // rate
    hd = DIL_HEAD_DIM
    n = max(SUB_BLOCK, TOKEN_TILE // rate)
    tm = n * rate
    res_per_chunk = max(1, TOKEN_TILE // n)
    col_split = max(1, tm // (2 * TOKEN_TILE))
    dc = D // col_split
    width = 2 * LANES
    ii = jnp.arange(width)
    seg = jnp.where((ii[:, None] // hd) == (ii[None, :] // hd), 1.0 / hd, 0.0).astype(BF16)
    row = lambda g: jnp.tile(g, D // hd).reshape(1, D)
    qk_spec = pl.BlockSpec((1, rate, n, dc), lambda b, i, s: (b, 0, i, s))
    qk_shape = jax.ShapeDtypeStruct((B, rate, L, D), BF16)
    return pl.pallas_call(
        functools.partial(_qkv_kernel, rate=rate, n=n, res_per_chunk=res_per_chunk),
        grid=(B, S // tm, col_split),
        in_specs=[pl.BlockSpec((tm, D), lambda b, i, s: (b * (S // tm) + i, 0)),
                  _resident((1, D), lambda b, i, s: (0, 0)),
                  _resident((1, D), lambda b, i, s: (0, 0)),
                  pl.BlockSpec((D, dc), lambda b, i, s: (0, s)),
                  pl.BlockSpec((D, dc), lambda b, i, s: (0, s)),
                  pl.BlockSpec((dc, D), lambda b, i, s: (s, 0)),
                  pl.BlockSpec((1, dc), lambda b, i, s: (0, s)),
                  pl.BlockSpec((1, dc), lambda b, i, s: (0, s)),
                  _resident((width, width), lambda b, i, s: (0, 0))],
        out_specs=[qk_spec, qk_spec,
                   pl.BlockSpec((1, rate, dc, n), lambda b, i, s: (b, 0, s, i))],
        out_shape=[qk_shape, qk_shape, jax.ShapeDtypeStruct((B, rate, D, L), BF16)],
        scratch_shapes=[pltpu.VMEM((D // LANES, tm if rate > 1 else 8, LANES), F32)],
        compiler_params=_params("parallel", "parallel", "arbitrary"),
        name="qkv_rate%d" % rate,
    )(h, gq.reshape(1, D), gkv.reshape(1, D), wq, wk, wv.T, row(qn) * hd ** -0.5, row(kn), seg)


def _attn_kernel(*refs, rate, n_steps, with_prev):
    if with_prev:
        q_ref, kc_ref, kp_ref, vc_ref, vp_ref, bias_ref, o_ref, lse_ref, os_ref = refs
    else:
        q_ref, kc_ref, vc_ref, bias_ref, o_ref, lse_ref, os_ref = refs
    Bk = SUB_BLOCK
    hd = DIL_HEAD_DIM
    n_pairs = DIL_HEADS // 2
    nk = 2 * Bk if with_prev else Bk
    tile = pl.program_id(1)

    lane = lax.broadcasted_iota(jnp.int32, (Bk, LANES), 1)
    kk = lax.broadcasted_iota(jnp.int32, (nk, 2 * Bk), 0)
    qq = lax.broadcasted_iota(jnp.int32, (nk, 2 * Bk), 1) & (Bk - 1)
    steps = qq - kk + (nk - Bk)
    band = (steps >= 0) & (steps <= n_steps)

    def block(load_q, load_k, load_vt, has_prev, store):
        valid = band if has_prev is None else band & ((kk >= Bk) | has_prev)
        out_t, lse_t = [], []
        for p in range(n_pairs):
            qp = load_q(p)
            zero = jnp.zeros_like(qp)
            q2 = jnp.concatenate([jnp.where(lane < hd, qp, zero), jnp.where(lane >= hd, qp, zero)], axis=0)
            s = lax.dot_general(load_k(p), q2, NT_DIMS, preferred_element_type=F32)
            s = jnp.where(valid, s + bias_ref[p], NEG_INF)
            m = jnp.max(s, axis=0, keepdims=True)
            pr = jnp.exp(s - m)
            l = jnp.sum(pr, axis=0, keepdims=True)
            pb = pr.astype(BF16)
            vt = load_vt(p)
            out_t.append(jnp.dot(vt[:hd], pb[:, :Bk], preferred_element_type=F32) / l[:, :Bk])
            out_t.append(jnp.dot(vt[hd:], pb[:, Bk:], preferred_element_type=F32) / l[:, Bk:])
            lse = m + jnp.log(l)
            lse_t += [lse[:, :Bk], lse[:, Bk:]]
        lse_t.append(jnp.zeros((LANES - DIL_HEADS, Bk), F32))
        store(jnp.concatenate(out_t, axis=0).T, jnp.concatenate(lse_t, axis=0).T)

    def pair_cols(p):
        return slice(p * LANES, (p + 1) * LANES)

    if rate == 1:
        for j in range(q_ref.shape[2] // Bk):
            rows = slice(j * Bk, (j + 1) * Bk)
            if j == 0:
                load_k = lambda p: jnp.concatenate([kp_ref[0, 0, :, pair_cols(p)],
                                                    kc_ref[0, 0, :Bk, pair_cols(p)]], axis=0)
                load_vt = lambda p: jnp.concatenate([vp_ref[0, 0, pair_cols(p), :],
                                                     vc_ref[0, 0, pair_cols(p), :Bk]], axis=1)
                has_prev = tile > 0
            else:
                krows = slice((j - 1) * Bk, (j + 1) * Bk)
                load_k = lambda p, krows=krows: kc_ref[0, 0, krows, pair_cols(p)]
                load_vt = lambda p, krows=krows: vc_ref[0, 0, pair_cols(p), krows]
                has_prev = None

            def store(o, lse, rows=rows):
                o_ref[rows, :] = o.astype(o_ref.dtype)
                lse_ref[rows, :] = lse

            block(lambda p, rows=rows: q_ref[0, 0, rows, pair_cols(p)], load_k, load_vt, has_prev, store)
    else:
        for c in range(q_ref.shape[1]):
            if with_prev:
                load_k = lambda p, c=c: jnp.concatenate([kp_ref[0, c, :, pair_cols(p)],
                                                         kc_ref[0, c, :, pair_cols(p)]], axis=0)
                load_vt = lambda p, c=c: jnp.concatenate([vp_ref[0, c, pair_cols(p), :],
                                                          vc_ref[0, c, pair_cols(p), :]], axis=1)
                has_prev = tile > 0
            else:
                load_k = lambda p, c=c: kc_ref[0, c, :, pair_cols(p)]
                load_vt = lambda p, c=c: vc_ref[0, c, pair_cols(p), :]
                has_prev = None

            def store(o, lse, c=c):
                for j in range(os_ref.shape[0]):
                    os_ref[j, pl.ds(c, Bk, stride=rate), :] = o[:, pair_cols(j)]
                lse_ref[pl.ds(c, Bk, stride=rate), :] = lse

            block(lambda p, c=c: q_ref[0, c, :, pair_cols(p)], load_k, load_vt, has_prev, store)
        for j in range(os_ref.shape[0]):
            o_ref[:, pair_cols(j)] = os_ref[j].astype(o_ref.dtype)


def _t5_bucket(n):
    max_exact = NUM_BUCKETS // 2
    nf = jnp.maximum(n, max_exact).astype(F32)
    large = max_exact + (jnp.log(nf / max_exact) / math.log(MAX_DISTANCE / max_exact)
                         * (NUM_BUCKETS - max_exact)).astype(jnp.int32)
    large = jnp.minimum(large, NUM_BUCKETS - 1)
    return jnp.where(n < max_exact, n, large)


def _group_attention(q, k, vt, bias_table, rate, n_steps):
    B, _, L, D = q.shape
    S = L * rate
    Bk = SUB_BLOCK
    with_prev = L > Bk
    nk = 2 * Bk if with_prev else Bk
    n = max(Bk, TOKEN_TILE // rate)
    tm = n * rate
    nt = S // tm
    ql = jnp.arange(Bk, dtype=jnp.int32)[:, None]
    kl = jnp.arange(2 * Bk, dtype=jnp.int32)[None, :]
    bucket = _t5_bucket(jnp.maximum(ql + Bk - kl, 0) * rate)
    bias = bias_table[bucket].astype(F32)
    bias = bias[:, 2 * Bk - nk:, :].transpose(1, 2, 0)
    bias = bias.reshape(nk, DIL_HEADS // 2, 2 * Bk).transpose(1, 0, 2)

    cur_qk = pl.BlockSpec((1, rate, n, D), lambda b, i: (b, 0, i, 0))
    cur_vt = pl.BlockSpec((1, rate, D, n), lambda b, i: (b, 0, 0, i))
    per_n = n // Bk
    prev_qk = pl.BlockSpec((1, rate, Bk, D), lambda b, i: (b, 0, jnp.maximum(i * per_n - 1, 0), 0))
    prev_vt = pl.BlockSpec((1, rate, D, Bk), lambda b, i: (b, 0, 0, jnp.maximum(i * per_n - 1, 0)))
    bias_spec = _resident((DIL_HEADS // 2, nk, 2 * Bk), lambda b, i: (0, 0, 0))
    if with_prev:
        in_specs = [cur_qk, cur_qk, prev_qk, cur_vt, prev_vt, bias_spec]
        args = (q, k, k, vt, vt, bias)
    else:
        in_specs = [cur_qk, cur_qk, cur_vt, bias_spec]
        args = (q, k, vt, bias)
    return pl.pallas_call(
        functools.partial(_attn_kernel, rate=rate, n_steps=n_steps, with_prev=with_prev),
        grid=(B, nt),
        in_specs=in_specs,
        out_specs=[pl.BlockSpec((tm, D), lambda b, i: (b * nt + i, 0)),
                   pl.BlockSpec((tm, LANES), lambda b, i: (b * nt + i, 0))],
        out_shape=[jax.ShapeDtypeStruct((B * S, D), BF16),
                   jax.ShapeDtypeStruct((B * S, LANES), F32)],
        scratch_shapes=[pltpu.VMEM((D // LANES, tm if rate > 1 else 8, LANES), F32)],
        compiler_params=_params("parallel", "parallel"),
        name="attn_rate%d" % rate,
    )(*args)


def _merge_kernel(o0_ref, o1_ref, o2_ref, l0_ref, l1_ref, l2_ref, ex_ref, out_ref):
    ex = ex_ref[...]

    def expand(w):
        hi = w.astype(BF16)
        r1 = w - hi.astype(F32)
        mid = r1.astype(BF16)
        lo = (r1 - mid.astype(F32)).astype(BF16)
        return (jnp.dot(hi, ex, preferred_element_type=F32) + jnp.dot(mid, ex, preferred_element_type=F32)
                + jnp.dot(lo, ex, preferred_element_type=F32))

    l0, l1, l2 = l0_ref[...], l1_ref[...], l2_ref[...]
    top = jnp.maximum(jnp.maximum(l0, l1), l2)
    w0, w1, w2 = jnp.exp(l0 - top), jnp.exp(l1 - top), jnp.exp(l2 - top)
    den = w0 + w1 + w2
    acc = expand(w0 / den) * o0_ref[...].astype(F32)
    acc = acc + expand(w1 / den) * o1_ref[...].astype(F32)
    acc = acc + expand(w2 / den) * o2_ref[...].astype(F32)
    out_ref[...] = acc.astype(out_ref.dtype)


def _merge(outs, lses):
    T, D = outs[0].shape
    tm = min(TOKEN_TILE, T)
    ex = jnp.where(jnp.arange(LANES)[:, None] == (jnp.arange(D)[None, :] // DIL_HEAD_DIM), 1.0, 0.0).astype(BF16)
    o_spec = pl.BlockSpec((tm, D), lambda i: (i, 0))
    l_spec = pl.BlockSpec((tm, LANES), lambda i: (i, 0))
    return pl.pallas_call(
        _merge_kernel,
        grid=(T // tm,),
        in_specs=[o_spec] * 3 + [l_spec] * 3 + [_resident((LANES, D), lambda i: (0, 0))],
        out_specs=o_spec,
        out_shape=jax.ShapeDtypeStruct((T, D), BF16),
        compiler_params=_params("parallel"),
        name="merge_groups",
    )(*outs, *lses, ex)


def kernel(x, ret_w_in, ret_w_out, kv_norm, w_kv, k_norm, dil_wq, q_norm, dil_wo, rel_bias,
           mixer_norm, ffn_norm, router_grp, router_grp_b, router_exp, router_exp_b,
           exp_gate, exp_up, exp_down):
    B, S, D = x.shape
    h = x.reshape(B * S, D)

    def moe_layer(layer, a, w_out, h):
        h1, hn, route, route_t, counts = _post(
            a, w_out.astype(BF16), h, ffn_norm[layer], router_grp[layer], router_grp_b[layer],
            router_exp[layer], router_exp_b[layer])
        return _moe(h1, hn, route, route_t, counts, exp_gate[layer], exp_up[layer], exp_down[layer])

    proj = _norm_proj(h, mixer_norm[0], ret_w_in[0].astype(BF16))
    y = _retention(proj, B, S)
    h = moe_layer(0, y, ret_w_out[0], h)

    G = len(DIL_RATES)
    gd = DIL_HEADS * DIL_HEAD_DIM
    outs, lses = [], []
    for g in range(G):
        cq = slice(g * gd, (g + 1) * gd)
        cv = slice(G * gd + g * gd, G * gd + (g + 1) * gd)
        q, k, vt = _qkv(h, B, S, DIL_RATES[g], mixer_norm[1], kv_norm,
                        dil_wq[0][:, cq].astype(BF16), w_kv[:, cq].astype(BF16), w_kv[:, cv].astype(BF16),
                        q_norm[0][g], k_norm[g])
        o, lse = _group_attention(q, k, vt, rel_bias[:, g * DIL_HEADS:(g + 1) * DIL_HEADS],
                                  DIL_RATES[g], DIL_WINDOWS[g] // DIL_RATES[g])
        outs.append(o)
        lses.append(lse)
    att = _merge(outs, lses)
    h = moe_layer(1, att, dil_wo[0], h)
    return h.reshape(B, S, D)
```

```python
import functools
import math

import jax
import jax.numpy as jnp
from jax import lax
from jax.experimental import pallas as pl
from jax.experimental.pallas import tpu as pltpu

F32 = jnp.float32
BF16 = jnp.bfloat16

EPS = 1e-6
NEG_INF = -1e30

RET_HEADS = 4
RET_CHUNK = 128
ROPE_BASE = 10000.0

DIL_WINDOWS = (128, 512, 2048)
DIL_RATES = (1, 4, 16)
DIL_HEADS = 16
DIL_HEAD_DIM = 64
SUB_BLOCK = 128
NUM_BUCKETS = 32
MAX_DISTANCE = 2048

MOE_GROUPS = 4
EXPERTS_PER_GROUP = 8
N_EXPERTS = MOE_GROUPS * EXPERTS_PER_GROUP
MOE_BLOCK = 256

LANES = 128
TOKEN_TILE = 512
VMEM_LIMIT = 56 * 1024 * 1024

NT_DIMS = (((1,), (1,)), ((), ()))


def _params(*sem):
    return pltpu.CompilerParams(dimension_semantics=sem, vmem_limit_bytes=VMEM_LIMIT)


def _resident(shape, index_map):
    return pl.BlockSpec(shape, index_map, pipeline_mode=pl.Buffered(1))


def _rms(x):
    return x * lax.rsqrt(jnp.mean(x * x, axis=-1, keepdims=True) + EPS)


def _norm_proj_kernel(x_ref, g_ref, w_ref, o_ref, *, col_tile):
    xn = (_rms(x_ref[...]) * g_ref[...]).astype(BF16)
    for j in range(w_ref.shape[1] // col_tile):
        cols = slice(j * col_tile, (j + 1) * col_tile)
        o_ref[:, cols] = jnp.dot(xn, w_ref[:, cols], preferred_element_type=F32).astype(o_ref.dtype)


def _norm_proj(x, gain, w):
    T, D = x.shape
    N = w.shape[1]
    tm = min(TOKEN_TILE, T)
    return pl.pallas_call(
        functools.partial(_norm_proj_kernel, col_tile=512),
        grid=(T // tm,),
        in_specs=[pl.BlockSpec((tm, D), lambda i: (i, 0)),
                  _resident((1, D), lambda i: (0, 0)),
                  _resident((D, N), lambda i: (0, 0))],
        out_specs=pl.BlockSpec((tm, N), lambda i: (i, 0)),
        out_shape=jax.ShapeDtypeStruct((T, N), BF16),
        compiler_params=_params("parallel"),
        name="norm_proj",
    )(x, gain.reshape(1, D), w)


def _retention_kernel(q_ref, k_ref, v_ref, gate_ref, cos_ref, sin_ref, din_ref, xi_ref, zeta_ref,
                      cd_ref, o_ref, state_ref, *, n_chunks, k_scale):
    C = RET_CHUNK
    half = q_ref.shape[2] // 2
    state_ref[...] = jnp.zeros_like(state_ref)
    d_in = din_ref[0]
    xi = xi_ref[0]
    zeta = zeta_ref[0]
    cd = cd_ref[0]

    def rot(t, cos, sin):
        t1, t2 = t[:, :half], t[:, half:]
        return jnp.concatenate([t1 * cos - t2 * sin, t1 * sin + t2 * cos], axis=-1)

    def body(c, carry):
        rows = pl.ds(pl.multiple_of(c * C, C), C)
        cos = cos_ref[rows, :]
        sin = sin_ref[rows, :]
        q = rot(q_ref[0, rows, :].astype(F32), cos, sin)
        k = rot(k_ref[0, rows, :].astype(F32), cos, sin) * k_scale
        v = v_ref[0, rows, :]
        qb = q.astype(BF16)
        kb = k.astype(BF16)
        s = lax.dot_general(qb, kb, NT_DIMS, preferred_element_type=F32) * d_in
        inner = jnp.dot(s.astype(BF16), v, preferred_element_type=F32)
        state = state_ref[...]
        cross = jnp.dot(qb, state.astype(BF16), preferred_element_type=F32) * xi
        kz_t = (k * zeta).T.astype(BF16)
        state_ref[...] = state * cd + jnp.dot(kz_t, v, preferred_element_type=F32)
        o = _rms(inner + cross)
        g = gate_ref[0, rows, :].astype(F32)
        o_ref[0, rows, :] = (g * jax.nn.sigmoid(g) * o).astype(o_ref.dtype)
        return carry

    lax.fori_loop(0, n_chunks, body, 0)


def _retention(proj, B, S):
    H, C = RET_HEADS, RET_CHUNK
    D = proj.shape[1] // 6
    dk, dv = D // H, 2 * D // H
    half = dk // 2
    proj = proj.reshape(B, S, 6 * D)

    pos = jnp.arange(S, dtype=F32)
    inv = 1.0 / (ROPE_BASE ** jnp.linspace(0.0, 1.0, half, dtype=F32))
    ang = pos[:, None] * inv[None, :]
    cos, sin = jnp.cos(ang), jnp.sin(ang)
    log_g = jnp.log(1.0 - 2.0 ** (-5.0 - jnp.arange(H, dtype=F32)))
    idx = jnp.arange(C, dtype=F32)
    diff = idx[:, None] - idx[None, :]
    d_in = jnp.where(diff >= 0, jnp.exp(log_g[:, None, None] * jnp.maximum(diff, 0.0)), 0.0)
    xi = jnp.exp(log_g[:, None] * (idx + 1.0))[:, :, None]
    zeta = jnp.exp(log_g[:, None] * (C - 1.0 - idx))[:, :, None]
    chunk_decay = jnp.exp(log_g * C)[:, None, None]

    kb, vb, gb = H * dk // dk, 2 * H * dk // dv, (2 * H * dk + H * dv) // dv
    out = pl.pallas_call(
        functools.partial(_retention_kernel, n_chunks=S // C, k_scale=dk ** -0.5),
        grid=(B, H),
        in_specs=[pl.BlockSpec((1, S, dk), lambda b, h: (b, 0, h)),
                  pl.BlockSpec((1, S, dk), lambda b, h: (b, 0, kb + h)),
                  pl.BlockSpec((1, S, dv), lambda b, h: (b, 0, vb + h)),
                  pl.BlockSpec((1, S, dv), lambda b, h: (b, 0, gb + h)),
                  _resident((S, half), lambda b, h: (0, 0)),
                  _resident((S, half), lambda b, h: (0, 0)),
                  pl.BlockSpec((1, C, C), lambda b, h: (h, 0, 0)),
                  pl.BlockSpec((1, C, 1), lambda b, h: (h, 0, 0)),
                  pl.BlockSpec((1, C, 1), lambda b, h: (h, 0, 0)),
                  pl.BlockSpec((1, 1, 1), lambda b, h: (h, 0, 0))],
        out_specs=pl.BlockSpec((1, S, dv), lambda b, h: (b, 0, h)),
        out_shape=jax.ShapeDtypeStruct((B, S, H * dv), BF16),
        scratch_shapes=[pltpu.VMEM((dk, dv), F32)],
        compiler_params=_params("parallel", "parallel"),
        name="retention",
    )(proj, proj, proj, proj, cos, sin, d_in, xi, zeta, chunk_decay)
    return out.reshape(B * S, H * dv)


ROUTE_E1, ROUTE_E2, ROUTE_G1, ROUTE_G2, ROUTE_R1, ROUTE_R2 = range(6)
ROUTE_ROWS = 8
ROUTER_EXP_LANE0 = MOE_GROUPS


def _post_kernel(a_ref, w_ref, h_ref, g_ref, wr_ref, br_ref, tri_ref,
                 h1_ref, hn_ref, route_ref, route_t_ref, cnt_ref, carry_ref):
    @pl.when(pl.program_id(0) == 0)
    def _():
        carry_ref[...] = jnp.zeros_like(carry_ref)

    h1 = h_ref[...] + jnp.dot(a_ref[...], w_ref[...], preferred_element_type=F32)
    h1_ref[...] = h1
    hn = _rms(h1) * g_ref[...]
    hn_ref[...] = hn
    logits = jnp.dot(hn, wr_ref[...], preferred_element_type=F32,
                     precision=lax.Precision.HIGHEST) + br_ref[...]
    lane = lax.broadcasted_iota(jnp.int32, logits.shape, 1).astype(F32)
    ninf = -jnp.inf

    def first_argmax(vals):
        top = jnp.max(vals, axis=1, keepdims=True)
        where = jnp.min(jnp.where(vals == top, lane, float(LANES)), axis=1, keepdims=True)
        return top, where

    is_grp = lane < MOE_GROUPS
    lg = jnp.where(is_grp, logits, ninf)
    mg, grp = first_argmax(lg)
    p_grp = 1.0 / jnp.sum(jnp.where(is_grp, jnp.exp(lg - mg), 0.0), axis=1, keepdims=True)

    e_lane = lane - ROUTER_EXP_LANE0
    in_grp = (e_lane < N_EXPERTS) & (jnp.floor(e_lane * (1.0 / EXPERTS_PER_GROUP)) == grp)
    le = jnp.where(in_grp, logits, ninf)
    v1, i1 = first_argmax(le)
    le2 = jnp.where(lane == i1, ninf, le)
    v2, i2 = first_argmax(le2)
    e = jnp.exp(v2 - v1)
    g1 = p_grp / (1.0 + e)
    g2 = p_grp * e / (1.0 + e)
    e1 = i1 - ROUTER_EXP_LANE0
    e2 = i2 - ROUTER_EXP_LANE0

    hit1 = lane == e1
    hit2 = lane == e2
    onehot = jnp.where(hit1 | hit2, 1.0, 0.0)
    before = carry_ref[...] + jnp.dot(tri_ref[...], onehot.astype(BF16), preferred_element_type=F32)
    r1 = jnp.sum(jnp.where(hit1, before, 0.0), axis=1, keepdims=True)
    r2 = jnp.sum(jnp.where(hit2, before, 0.0), axis=1, keepdims=True)
    carry_ref[...] += jnp.sum(onehot, axis=0, keepdims=True)
    cnt_ref[...] = carry_ref[...]

    route = jnp.zeros(logits.shape, F32)
    for slot, val in ((ROUTE_E1, e1), (ROUTE_E2, e2), (ROUTE_G1, g1),
                      (ROUTE_G2, g2), (ROUTE_R1, r1), (ROUTE_R2, r2)):
        route = jnp.where(lane == slot, val, route)
    route_ref[...] = route
    route_t_ref[...] = route.T[:ROUTE_ROWS]


def _post(a, w, h, gain, w_grp, b_grp, w_exp, b_exp):
    T, D = h.shape
    K = a.shape[1]
    tm = min(TOKEN_TILE, T)
    n_r = MOE_GROUPS + N_EXPERTS
    wr = jnp.zeros((D, LANES), F32).at[:, :n_r].set(jnp.concatenate([w_grp, w_exp], axis=1))
    br = jnp.zeros((1, LANES), F32).at[0, :n_r].set(jnp.concatenate([b_grp, b_exp]))
    tri = jnp.tril(jnp.ones((tm, tm), BF16), k=-1)
    return pl.pallas_call(
        _post_kernel,
        grid=(T // tm,),
        in_specs=[pl.BlockSpec((tm, K), lambda i: (i, 0)),
                  _resident((K, D), lambda i: (0, 0)),
                  pl.BlockSpec((tm, D), lambda i: (i, 0)),
                  _resident((1, D), lambda i: (0, 0)),
                  _resident((D, LANES), lambda i: (0, 0)),
                  _resident((1, LANES), lambda i: (0, 0)),
                  _resident((tm, tm), lambda i: (0, 0))],
        out_specs=[pl.BlockSpec((tm, D), lambda i: (i, 0)),
                   pl.BlockSpec((tm, D), lambda i: (i, 0)),
                   pl.BlockSpec((tm, LANES), lambda i: (i, 0)),
                   pl.BlockSpec((ROUTE_ROWS, tm), lambda i: (0, i)),
                   pl.BlockSpec((1, LANES), lambda i: (0, 0))],
        out_shape=[jax.ShapeDtypeStruct((T, D), F32),
                   jax.ShapeDtypeStruct((T, D), F32),
                   jax.ShapeDtypeStruct((T, LANES), F32),
                   jax.ShapeDtypeStruct((ROUTE_ROWS, T), F32),
                   jax.ShapeDtypeStruct((1, LANES), F32)],
        scratch_shapes=[pltpu.VMEM((1, LANES), F32)],
        compiler_params=_params("arbitrary"),
        name="post_mixer",
    )(a, w, h, gain.reshape(1, D), wr, br, tri)


def _dispatch_kernel(pends_ref, dest_ref, x_ref, o_ref, zero_ref, sem, zero_sem, *, tm):
    @pl.when(pl.program_id(0) == 0)
    def _():
        zero_ref[...] = jnp.zeros_like(zero_ref)
        fills = []
        for e in range(N_EXPERTS):
            end = pends_ref[e]
            nonempty = end > (pends_ref[e - 1] if e else 0)
            start = pl.multiple_of(jnp.maximum(end - MOE_BLOCK, 0), MOE_BLOCK)
            fills.append((nonempty, pltpu.make_async_copy(zero_ref, o_ref.at[pl.ds(start, MOE_BLOCK)], zero_sem)))
        for t in range(N_EXPERTS):
            start = pends_ref[N_EXPERTS - 1] + t * MOE_BLOCK
            in_range = start < o_ref.shape[0]
            start = pl.multiple_of(jnp.minimum(start, o_ref.shape[0] - MOE_BLOCK), MOE_BLOCK)
            fills.append((in_range, pltpu.make_async_copy(zero_ref, o_ref.at[pl.ds(start, MOE_BLOCK)], zero_sem)))
        for nonempty, fill in fills:
            pl.when(nonempty)(fill.start)
        for nonempty, fill in fills:
            pl.when(nonempty)(fill.wait)

    def row_copy(j, d):
        return pltpu.make_async_copy(x_ref.at[pl.ds(j, 1)], o_ref.at[pl.ds(d, 1)], sem)

    def issue(j, carry):
        row_copy(j, dest_ref[0, 0, j]).start(priority=0)
        row_copy(j, dest_ref[0, 0, tm + j]).start(priority=1)
        return carry

    lax.fori_loop(0, tm, issue, 0)
    for _ in range(2):
        pltpu.make_async_copy(x_ref, o_ref.at[pl.ds(0, tm)], sem).wait()


def _dispatch(x, dest, pends, P):
    T, D = x.shape
    tm = dest.shape[2] // 2
    nt = T // tm
    return pl.pallas_call(
        functools.partial(_dispatch_kernel, tm=tm),
        grid_spec=pltpu.PrefetchScalarGridSpec(
            num_scalar_prefetch=1,
            grid=(nt,),
            in_specs=[pl.BlockSpec((1, 1, 2 * tm), lambda i, pe: (i, 0, 0), memory_space=pltpu.SMEM),
                      pl.BlockSpec((tm, D), lambda i, pe: (i, 0))],
            out_specs=pl.BlockSpec(memory_space=pl.ANY),
            scratch_shapes=[pltpu.VMEM((MOE_BLOCK, D), x.dtype), pltpu.SemaphoreType.DMA(()),
                            pltpu.SemaphoreType.DMA(())]),
        out_shape=jax.ShapeDtypeStruct((P, D), x.dtype),
        compiler_params=_params("arbitrary"),
        name="dispatch",
    )(pends, dest, x)


def _expert_kernel(blk_exp_ref, n_used_ref, x_ref, wg_ref, wu_ref, wd_ref, o_ref,
                   wg_s, wu_s, wd_s):
    i = pl.program_id(0)
    used = i < n_used_ref[0]
    new_expert = (i == 0) | (blk_exp_ref[i] != blk_exp_ref[jnp.maximum(i - 1, 0)])

    @pl.when(used & new_expert)
    def _():
        wg_s[...] = wg_ref[0, 0].astype(BF16)
        wu_s[...] = wu_ref[0, 0].astype(BF16)
        wd_s[...] = wd_ref[0, 0].astype(BF16)

    @pl.when(used)
    def _():
        x = x_ref[...].astype(BF16)
        g = jnp.dot(x, wg_s[...], preferred_element_type=F32)
        u = jnp.dot(x, wu_s[...], preferred_element_type=F32)
        hid = (g * jax.nn.sigmoid(g) * u).astype(BF16)
        o_ref[...] = jnp.dot(hid, wd_s[...], preferred_element_type=F32)

    @pl.when(jnp.logical_not(used))
    def _():
        o_ref[...] = jnp.zeros_like(o_ref)


def _experts(xs, blk_exp, n_used, layer, w_gate, w_up, w_down):
    P, D = xs.shape
    FF = w_gate.shape[3]
    nblk = P // MOE_BLOCK

    def x_map(i, be, nu):
        return (jnp.minimum(i, nu[0] - 1), 0)

    def w_map(i, be, nu):
        return (layer, be[jnp.minimum(i, nu[0] - 1)], 0, 0)

    return pl.pallas_call(
        _expert_kernel,
        grid_spec=pltpu.PrefetchScalarGridSpec(
            num_scalar_prefetch=2,
            grid=(nblk,),
            in_specs=[pl.BlockSpec((MOE_BLOCK, D), x_map),
                      pl.BlockSpec((1, 1, D, FF), w_map),
                      pl.BlockSpec((1, 1, D, FF), w_map),
                      pl.BlockSpec((1, 1, FF, D), w_map)],
            out_specs=pl.BlockSpec((MOE_BLOCK, D), lambda i, be, nu: (i, 0)),
            scratch_shapes=[pltpu.VMEM((D, FF), BF16), pltpu.VMEM((D, FF), BF16),
                            pltpu.VMEM((FF, D), BF16)]),
        out_shape=jax.ShapeDtypeStruct((P, D), F32),
        compiler_params=_params("arbitrary"),
        name="experts",
    )(blk_exp, n_used, xs, w_gate, w_up, w_down)


def _combine_kernel(dest_ref, h_ref, route_ref, y_ref, o_ref, ya_ref, yb_ref, sem, *, tm):
    def issue(j, carry):
        pltpu.make_async_copy(y_ref.at[pl.ds(dest_ref[0, 0, j], 1)], ya_ref.at[pl.ds(j, 1)],
                              sem).start(priority=0)
        pltpu.make_async_copy(y_ref.at[pl.ds(dest_ref[0, 0, tm + j], 1)], yb_ref.at[pl.ds(j, 1)],
                              sem).start(priority=1)
        return carry

    lax.fori_loop(0, tm, issue, 0)
    for buf in (ya_ref, yb_ref):
        pltpu.make_async_copy(y_ref.at[pl.ds(0, tm)], buf, sem).wait()
    route = route_ref[...]
    g1 = route[:, ROUTE_G1:ROUTE_G1 + 1]
    g2 = route[:, ROUTE_G2:ROUTE_G2 + 1]
    o_ref[...] = h_ref[...] + (g1 * ya_ref[...] + g2 * yb_ref[...])


def _combine(h, route, ys, dest):
    T, D = h.shape
    tm = dest.shape[2] // 2
    nt = T // tm
    return pl.pallas_call(
        functools.partial(_combine_kernel, tm=tm),
        grid=(nt,),
        in_specs=[pl.BlockSpec((1, 1, 2 * tm), lambda i: (i, 0, 0), memory_space=pltpu.SMEM),
                  pl.BlockSpec((tm, D), lambda i: (i, 0)),
                  pl.BlockSpec((tm, LANES), lambda i: (i, 0)),
                  pl.BlockSpec(memory_space=pl.ANY)],
        out_specs=pl.BlockSpec((tm, D), lambda i: (i, 0)),
        out_shape=jax.ShapeDtypeStruct((T, D), F32),
        scratch_shapes=[pltpu.VMEM((tm, D), F32), pltpu.VMEM((tm, D), F32),
                        pltpu.SemaphoreType.DMA(())],
        compiler_params=_params("arbitrary"),
        name="combine",
    )(dest, h, route, ys)


def _moe(h1, hn, route, route_t, counts, layer, w_gate, w_up, w_down):
    T, D = h1.shape
    tm = min(TOKEN_TILE, T)
    nt = T // tm
    A = 2 * T
    nblk = -(-A // MOE_BLOCK) + N_EXPERTS
    P = nblk * MOE_BLOCK
    eid = route_t[ROUTE_E1:ROUTE_E2 + 1].astype(jnp.int32)
    rank = route_t[ROUTE_R1:ROUTE_R2 + 1].astype(jnp.int32)
    cnt = counts[0, :N_EXPERTS].astype(jnp.int32)
    padded = (cnt + MOE_BLOCK - 1) // MOE_BLOCK * MOE_BLOCK
    pends = jnp.cumsum(padded)
    pstarts = pends - padded
    experts = jnp.arange(N_EXPERTS, dtype=jnp.int32)
    start_of = jnp.sum(jnp.where(eid[:, None, :] == experts[None, :, None], pstarts[None, :, None], 0), axis=1)
    dest = start_of + rank
    dest = dest.reshape(2, nt, tm).transpose(1, 0, 2).reshape(nt, 1, 2 * tm)
    blk_start = jnp.arange(nblk, dtype=jnp.int32) * MOE_BLOCK
    blk_exp = jnp.minimum(jnp.sum((pends[None, :] <= blk_start[:, None]).astype(jnp.int32), axis=1),
                          N_EXPERTS - 1)
    n_used = pends[-1:] // MOE_BLOCK
    xs = _dispatch(hn, dest, pends, P)
    ys = _experts(xs, blk_exp, n_used, layer, w_gate, w_up, w_down)
    return _combine(h1, route, ys, dest)


def _qkv_kernel(x_ref, gq_ref, gkv_ref, wq_ref, wk_ref, wvt_ref, qn_ref, kn_ref, seg_ref,
                q_ref, k_ref, vt_ref, xs_ref, *, rate, n, res_per_chunk):
    seg = seg_ref[...]
    width = seg.shape[0]

    def head_norm(t, gain):
        cols = []
        for j in range(t.shape[1] // width):
            tj = t[:, j * width:(j + 1) * width]
            ms = jnp.dot((tj * tj).astype(BF16), seg, preferred_element_type=F32)
            cols.append(tj * lax.rsqrt(ms + EPS))
        return jnp.concatenate(cols, axis=1) * gain

    def lane_chunk(j):
        return slice(j * LANES, (j + 1) * LANES)

    if rate > 1:
        for j in range(xs_ref.shape[0]):
            xs_ref[j] = x_ref[:, lane_chunk(j)]

    for c0 in range(0, rate, res_per_chunk):
        residues = range(c0, c0 + res_per_chunk)
        if rate == 1:
            x = x_ref[...]
        else:
            x = jnp.concatenate(
                [jnp.concatenate([xs_ref[j, pl.ds(c, n, stride=rate), :] for j in range(xs_ref.shape[0])], axis=1)
                 for c in residues], axis=0)
        y = _rms(x)
        xq = (y * gq_ref[...]).astype(BF16)
        xkv = (y * gkv_ref[...]).astype(BF16)
        q = head_norm(jnp.dot(xq, wq_ref[...], preferred_element_type=F32), qn_ref[...]).astype(q_ref.dtype)
        k = head_norm(jnp.dot(xkv, wk_ref[...], preferred_element_type=F32), kn_ref[...]).astype(k_ref.dtype)
        vt = lax.dot_general(wvt_ref[...], xkv, NT_DIMS, preferred_element_type=F32).astype(vt_ref.dtype)
        for j, c in enumerate(residues):
            q_ref[0, c] = q[j * n:(j + 1) * n]
            k_ref[0, c] = k[j * n:(j + 1) * n]
            vt_ref[0, c] = vt[:, j * n:(j + 1) * n]


def _qkv(h, B, S, rate, gq, gkv, wq, wk, wv, qn, kn):
    D = h.shape[1]
    L = S // rate
    hd = DIL_HEAD_DIM
    n = max(SUB_BLOCK, TOKEN_TILE // rate)
    tm = n * rate
    res_per_chunk = max(1, TOKEN_TILE // n)
    col_split = max(1, tm // (2 * TOKEN_TILE))
    dc = D // col_split
    width = 2 * LANES
    ii = jnp.arange(width)
    seg = jnp.where((ii[:, None] // hd) == (ii[None, :] // hd), 1.0 / hd, 0.0).astype(BF16)
    row = lambda g: jnp.tile(g, D // hd).reshape(1, D)
    qk_spec = pl.BlockSpec((1, rate, n, dc), lambda b, i, s: (b, 0, i, s))
    qk_shape = jax.ShapeDtypeStruct((B, rate, L, D), BF16)
    return pl.pallas_call(
        functools.partial(_qkv_kernel, rate=rate, n=n, res_per_chunk=res_per_chunk),
        grid=(B, S // tm, col_split),
        in_specs=[pl.BlockSpec((tm, D), lambda b, i, s: (b * (S // tm) + i, 0)),
                  _resident((1, D), lambda b, i, s: (0, 0)),
                  _resident((1, D), lambda b, i, s: (0, 0)),
                  pl.BlockSpec((D, dc), lambda b, i, s: (0, s)),
                  pl.BlockSpec((D, dc), lambda b, i, s: (0, s)),
                  pl.BlockSpec((dc, D), lambda b, i, s: (s, 0)),
                  pl.BlockSpec((1, dc), lambda b, i, s: (0, s)),
                  pl.BlockSpec((1, dc), lambda b, i, s: (0, s)),
                  _resident((width, width), lambda b, i, s: (0, 0))],
        out_specs=[qk_spec, qk_spec,
                   pl.BlockSpec((1, rate, dc, n), lambda b, i, s: (b, 0, s, i))],
        out_shape=[qk_shape, qk_shape, jax.ShapeDtypeStruct((B, rate, D, L), BF16)],
        scratch_shapes=[pltpu.VMEM((D // LANES, tm if rate > 1 else 8, LANES), F32)],
        compiler_params=_params("parallel", "parallel", "arbitrary"),
        name="qkv_rate%d" % rate,
    )(h, gq.reshape(1, D), gkv.reshape(1, D), wq, wk, wv.T, row(qn) * hd ** -0.5, row(kn), seg)


def _attn_kernel(*refs, rate, n_steps, with_prev):
    if with_prev:
        q_ref, kc_ref, kp_ref, vc_ref, vp_ref, bias_ref, o_ref, lse_ref, os_ref = refs
    else:
        q_ref, kc_ref, vc_ref, bias_ref, o_ref, lse_ref, os_ref = refs
    Bk = SUB_BLOCK
    hd = DIL_HEAD_DIM
    n_pairs = DIL_HEADS // 2
    nk = 2 * Bk if with_prev else Bk
    tile = pl.program_id(1)

    lane = lax.broadcasted_iota(jnp.int32, (Bk, LANES), 1)
    kk = lax.broadcasted_iota(jnp.int32, (nk, 2 * Bk), 0)
    qq = lax.broadcasted_iota(jnp.int32, (nk, 2 * Bk), 1) & (Bk - 1)
    steps = qq - kk + (nk - Bk)
    band = (steps >= 0) & (steps <= n_steps)

    def block(load_q, load_k, load_vt, has_prev, store):
        valid = band if has_prev is None else band & ((kk >= Bk) | has_prev)
        out_t, lse_t = [], []
        for p in range(n_pairs):
            qp = load_q(p)
            zero = jnp.zeros_like(qp)
            q2 = jnp.concatenate([jnp.where(lane < hd, qp, zero), jnp.where(lane >= hd, qp, zero)], axis=0)
            s = lax.dot_general(load_k(p), q2, NT_DIMS, preferred_element_type=F32)
            s = jnp.where(valid, s + bias_ref[p], NEG_INF)
            m = jnp.max(s, axis=0, keepdims=True)
            pr = jnp.exp(s - m)
            l = jnp.sum(pr, axis=0, keepdims=True)
            pb = pr.astype(BF16)
            vt = load_vt(p)
            out_t.append(jnp.dot(vt[:hd], pb[:, :Bk], preferred_element_type=F32) / l[:, :Bk])
            out_t.append(jnp.dot(vt[hd:], pb[:, Bk:], preferred_element_type=F32) / l[:, Bk:])
            lse = m + jnp.log(l)
            lse_t += [lse[:, :Bk], lse[:, Bk:]]
        lse_t.append(jnp.zeros((LANES - DIL_HEADS, Bk), F32))
        store(jnp.concatenate(out_t, axis=0).T, jnp.concatenate(lse_t, axis=0).T)

    def pair_cols(p):
        return slice(p * LANES, (p + 1) * LANES)

    if rate == 1:
        for j in range(q_ref.shape[2] // Bk):
            rows = slice(j * Bk, (j + 1) * Bk)
            if j == 0:
                load_k = lambda p: jnp.concatenate([kp_ref[0, 0, :, pair_cols(p)],
                                                    kc_ref[0, 0, :Bk, pair_cols(p)]], axis=0)
                load_vt = lambda p: jnp.concatenate([vp_ref[0, 0, pair_cols(p), :],
                                                     vc_ref[0, 0, pair_cols(p), :Bk]], axis=1)
                has_prev = tile > 0
            else:
                krows = slice((j - 1) * Bk, (j + 1) * Bk)
                load_k = lambda p, krows=krows: kc_ref[0, 0, krows, pair_cols(p)]
                load_vt = lambda p, krows=krows: vc_ref[0, 0, pair_cols(p), krows]
                has_prev = None

            def store(o, lse, rows=rows):
                o_ref[rows, :] = o.astype(o_ref.dtype)
                lse_ref[rows, :] = lse

            block(lambda p, rows=rows: q_ref[0, 0, rows, pair_cols(p)], load_k, load_vt, has_prev, store)
    else:
        for c in range(q_ref.shape[1]):
            if with_prev:
                load_k = lambda p, c=c: jnp.concatenate([kp_ref[0, c, :, pair_cols(p)],
                                                         kc_ref[0, c, :, pair_cols(p)]], axis=0)
                load_vt = lambda p, c=c: jnp.concatenate([vp_ref[0, c, pair_cols(p), :],
                                                          vc_ref[0, c, pair_cols(p), :]], axis=1)
                has_prev = tile > 0
            else:
                load_k = lambda p, c=c: kc_ref[0, c, :, pair_cols(p)]
                load_vt = lambda p, c=c: vc_ref[0, c, pair_cols(p), :]
                has_prev = None

            def store(o, lse, c=c):
                for j in range(os_ref.shape[0]):
                    os_ref[j, pl.ds(c, Bk, stride=rate), :] = o[:, pair_cols(j)]
                lse_ref[pl.ds(c, Bk, stride=rate), :] = lse

            block(lambda p, c=c: q_ref[0, c, :, pair_cols(p)], load_k, load_vt, has_prev, store)
        for j in range(os_ref.shape[0]):
            o_ref[:, pair_cols(j)] = os_ref[j].astype(o_ref.dtype)


def _t5_bucket(n):
    max_exact = NUM_BUCKETS // 2
    nf = jnp.maximum(n, max_exact).astype(F32)
    large = max_exact + (jnp.log(nf / max_exact) / math.log(MAX_DISTANCE / max_exact)
                         * (NUM_BUCKETS - max_exact)).astype(jnp.int32)
    large = jnp.minimum(large, NUM_BUCKETS - 1)
    return jnp.where(n < max_exact, n, large)


def _group_attention(q, k, vt, bias_table, rate, n_steps):
    B, _, L, D = q.shape
    S = L * rate
    Bk = SUB_BLOCK
    with_prev = L > Bk
    nk = 2 * Bk if with_prev else Bk
    n = max(Bk, TOKEN_TILE // rate)
    tm = n * rate
    nt = S // tm
    ql = jnp.arange(Bk, dtype=jnp.int32)[:, None]
    kl = jnp.arange(2 * Bk, dtype=jnp.int32)[None, :]
    bucket = _t5_bucket(jnp.maximum(ql + Bk - kl, 0) * rate)
    buckets = jnp.arange(NUM_BUCKETS, dtype=jnp.int32)
    bias = jnp.sum(jnp.where(bucket[None, :, :, None] == buckets[:, None, None, None],
                             bias_table.astype(F32)[:, None, None, :], 0.0), axis=0)
    bias = bias[:, 2 * Bk - nk:, :].transpose(1, 2, 0)
    bias = bias.reshape(nk, DIL_HEADS // 2, 2 * Bk).transpose(1, 0, 2)

    cur_qk = pl.BlockSpec((1, rate, n, D), lambda b, i: (b, 0, i, 0))
    cur_vt = pl.BlockSpec((1, rate, D, n), lambda b, i: (b, 0, 0, i))
    per_n = n // Bk
    prev_qk = pl.BlockSpec((1, rate, Bk, D), lambda b, i: (b, 0, jnp.maximum(i * per_n - 1, 0), 0))
    prev_vt = pl.BlockSpec((1, rate, D, Bk), lambda b, i: (b, 0, 0, jnp.maximum(i * per_n - 1, 0)))
    bias_spec = _resident((DIL_HEADS // 2, nk, 2 * Bk), lambda b, i: (0, 0, 0))
    if with_prev:
        in_specs = [cur_qk, cur_qk, prev_qk, cur_vt, prev_vt, bias_spec]
        args = (q, k, k, vt, vt, bias)
    else:
        in_specs = [cur_qk, cur_qk, cur_vt, bias_spec]
        args = (q, k, vt, bias)
    return pl.pallas_call(
        functools.partial(_attn_kernel, rate=rate, n_steps=n_steps, with_prev=with_prev),
        grid=(B, nt),
        in_specs=in_specs,
        out_specs=[pl.BlockSpec((tm, D), lambda b, i: (b * nt + i, 0)),
                   pl.BlockSpec((tm, LANES), lambda b, i: (b * nt + i, 0))],
        out_shape=[jax.ShapeDtypeStruct((B * S, D), BF16),
                   jax.ShapeDtypeStruct((B * S, LANES), F32)],
        scratch_shapes=[pltpu.VMEM((D // LANES, tm if rate > 1 else 8, LANES), F32)],
        compiler_params=_params("parallel", "parallel"),
        name="attn_rate%d" % rate,
    )(*args)


def _merge_kernel(o0_ref, o1_ref, o2_ref, l0_ref, l1_ref, l2_ref, ex_ref, out_ref):
    ex = ex_ref[...]

    def expand(w):
        hi = w.astype(BF16)
        r1 = w - hi.astype(F32)
        mid = r1.astype(BF16)
        lo = (r1 - mid.astype(F32)).astype(BF16)
        return (jnp.dot(hi, ex, preferred_element_type=F32) + jnp.dot(mid, ex, preferred_element_type=F32)
                + jnp.dot(lo, ex, preferred_element_type=F32))

    l0, l1, l2 = l0_ref[...], l1_ref[...], l2_ref[...]
    top = jnp.maximum(jnp.maximum(l0, l1), l2)
    w0, w1, w2 = jnp.exp(l0 - top), jnp.exp(l1 - top), jnp.exp(l2 - top)
    den = w0 + w1 + w2
    acc = expand(w0 / den) * o0_ref[...].astype(F32)
    acc = acc + expand(w1 / den) * o1_ref[...].astype(F32)
    acc = acc + expand(w2 / den) * o2_ref[...].astype(F32)
    out_ref[...] = acc.astype(out_ref.dtype)


def _merge(outs, lses):
    T, D = outs[0].shape
    tm = min(TOKEN_TILE, T)
    ex = jnp.where(jnp.arange(LANES)[:, None] == (jnp.arange(D)[None, :] // DIL_HEAD_DIM), 1.0, 0.0).astype(BF16)
    o_spec = pl.BlockSpec((tm, D), lambda i: (i, 0))
    l_spec = pl.BlockSpec((tm, LANES), lambda i: (i, 0))
    return pl.pallas_call(
        _merge_kernel,
        grid=(T // tm,),
        in_specs=[o_spec] * 3 + [l_spec] * 3 + [_resident((LANES, D), lambda i: (0, 0))],
        out_specs=o_spec,
        out_shape=jax.ShapeDtypeStruct((T, D), BF16),
        compiler_params=_params("parallel"),
        name="merge_groups",
    )(*outs, *lses, ex)


def kernel(x, ret_w_in, ret_w_out, kv_norm, w_kv, k_norm, dil_wq, q_norm, dil_wo, rel_bias,
           mixer_norm, ffn_norm, router_grp, router_grp_b, router_exp, router_exp_b,
           exp_gate, exp_up, exp_down):
    B, S, D = x.shape
    h = x.reshape(B * S, D)

    def moe_layer(layer, a, w_out, h):
        h1, hn, route, route_t, counts = _post(
            a, w_out.astype(BF16), h, ffn_norm[layer], router_grp[layer], router_grp_b[layer],
            router_exp[layer], router_exp_b[layer])
        return _moe(h1, hn, route, route_t, counts, layer, exp_gate, exp_up, exp_down)

    proj = _norm_proj(h, mixer_norm[0], ret_w_in[0].astype(BF16))
    y = _retention(proj, B, S)
    h = moe_layer(0, y, ret_w_out[0], h)

    G = len(DIL_RATES)
    gd = DIL_HEADS * DIL_HEAD_DIM
    outs, lses = [], []
    for g in range(G):
        cq = slice(g * gd, (g + 1) * gd)
        cv = slice(G * gd + g * gd, G * gd + (g + 1) * gd)
        q, k, vt = _qkv(h, B, S, DIL_RATES[g], mixer_norm[1], kv_norm,
                        dil_wq[0][:, cq].astype(BF16), w_kv[:, cq].astype(BF16), w_kv[:, cv].astype(BF16),
                        q_norm[0][g], k_norm[g])
        o, lse = _group_attention(q, k, vt, rel_bias[:, g * DIL_HEADS:(g + 1) * DIL_HEADS],
                                  DIL_RATES[g], DIL_WINDOWS[g] // DIL_RATES[g])
        outs.append(o)
        lses.append(lse)
    att = _merge(outs, lses)
    h = moe_layer(1, att, dil_wo[0], h)
    return h.reshape(B, S, D)
```

```python
import functools
import math

import jax
import jax.numpy as jnp
from jax import lax
from jax.experimental import pallas as pl
from jax.experimental.pallas import tpu as pltpu

F32 = jnp.float32
BF16 = jnp.bfloat16

EPS = 1e-6
NEG_INF = -1e30

RET_HEADS = 4
RET_CHUNK = 128
ROPE_BASE = 10000.0

DIL_WINDOWS = (128, 512, 2048)
DIL_RATES = (1, 4, 16)
DIL_HEADS = 16
DIL_HEAD_DIM = 64
SUB_BLOCK = 128
NUM_BUCKETS = 32
MAX_DISTANCE = 2048

MOE_GROUPS = 4
EXPERTS_PER_GROUP = 8
N_EXPERTS = MOE_GROUPS * EXPERTS_PER_GROUP
MOE_BLOCK = 256

LANES = 128
TOKEN_TILE = 512
VMEM_LIMIT = 56 * 1024 * 1024

NT_DIMS = (((1,), (1,)), ((), ()))
LOG2E = math.log2(math.e)
LN2 = math.log(2.0)


def _params(*sem):
    return pltpu.CompilerParams(dimension_semantics=sem, vmem_limit_bytes=VMEM_LIMIT)


def _resident(shape, index_map):
    return pl.BlockSpec(shape, index_map, pipeline_mode=pl.Buffered(1))


def _rms(x):
    return x * lax.rsqrt(jnp.mean(x * x, axis=-1, keepdims=True) + EPS)


def _norm_proj_kernel(x_ref, g_ref, w_ref, o_ref, *, col_tile):
    xn = (_rms(x_ref[...]) * g_ref[...]).astype(BF16)
    for j in range(w_ref.shape[1] // col_tile):
        cols = slice(j * col_tile, (j + 1) * col_tile)
        o_ref[:, cols] = jnp.dot(xn, w_ref[:, cols], preferred_element_type=F32).astype(o_ref.dtype)


def _norm_proj(x, gain, w):
    T, D = x.shape
    N = w.shape[1]
    tm = min(TOKEN_TILE, T)
    return pl.pallas_call(
        functools.partial(_norm_proj_kernel, col_tile=512),
        grid=(T // tm,),
        in_specs=[pl.BlockSpec((tm, D), lambda i: (i, 0)),
                  _resident((1, D), lambda i: (0, 0)),
                  _resident((D, N), lambda i: (0, 0))],
        out_specs=pl.BlockSpec((tm, N), lambda i: (i, 0)),
        out_shape=jax.ShapeDtypeStruct((T, N), BF16),
        compiler_params=_params("parallel"),
        name="norm_proj",
    )(x, gain.reshape(1, D), w)


def _retention_kernel(q_ref, k_ref, v_ref, gate_ref, cos_ref, sin_ref, din_ref, xi_ref, zeta_ref,
                      cd_ref, o_ref, state_ref, *, n_chunks, k_scale, heads):
    C = RET_CHUNK
    dk = q_ref.shape[2] // heads
    dv = v_ref.shape[2] // heads
    half = dk // 2
    state_ref[...] = jnp.zeros_like(state_ref)

    def rot(t, cos, sin):
        t1, t2 = t[:, :half], t[:, half:]
        return jnp.concatenate([t1 * cos - t2 * sin, t1 * sin + t2 * cos], axis=-1)

    def body(c, carry):
        rows = pl.ds(pl.multiple_of(c * C, C), C)
        cos = cos_ref[rows, :]
        sin = sin_ref[rows, :]
        for hh in range(heads):
            qk_cols = slice(hh * dk, (hh + 1) * dk)
            v_cols = slice(hh * dv, (hh + 1) * dv)
            q = rot(q_ref[0, rows, qk_cols].astype(F32), cos, sin)
            k = rot(k_ref[0, rows, qk_cols].astype(F32), cos, sin) * k_scale
            v = v_ref[0, rows, v_cols]
            qb = q.astype(BF16)
            kb = k.astype(BF16)
            s = lax.dot_general(qb, kb, NT_DIMS, preferred_element_type=F32) * din_ref[hh]
            inner = jnp.dot(s.astype(BF16), v, preferred_element_type=F32)
            state = state_ref[hh]
            cross = jnp.dot(qb, state.astype(BF16), preferred_element_type=F32) * xi_ref[hh]
            kz_t = (k * zeta_ref[hh]).T.astype(BF16)
            state_ref[hh] = state * cd_ref[hh] + jnp.dot(kz_t, v, preferred_element_type=F32)
            o = _rms(inner + cross)
            g = gate_ref[0, rows, v_cols].astype(F32)
            o_ref[0, rows, v_cols] = (g * jax.nn.sigmoid(g) * o).astype(o_ref.dtype)
        return carry

    lax.fori_loop(0, n_chunks, body, 0)


def _retention(proj, B, S):
    H, C = RET_HEADS, RET_CHUNK
    D = proj.shape[1] // 6
    dk, dv = D // H, 2 * D // H
    half = dk // 2
    heads = 2
    proj = proj.reshape(B, S, 6 * D)

    pos = jnp.arange(S, dtype=F32)
    inv = 1.0 / (ROPE_BASE ** jnp.linspace(0.0, 1.0, half, dtype=F32))
    ang = pos[:, None] * inv[None, :]
    cos, sin = jnp.cos(ang), jnp.sin(ang)
    log_g = jnp.log(1.0 - 2.0 ** (-5.0 - jnp.arange(H, dtype=F32)))
    idx = jnp.arange(C, dtype=F32)
    diff = idx[:, None] - idx[None, :]
    d_in = jnp.where(diff >= 0, jnp.exp(log_g[:, None, None] * jnp.maximum(diff, 0.0)), 0.0)
    xi = jnp.exp(log_g[:, None] * (idx + 1.0))[:, :, None]
    zeta = jnp.exp(log_g[:, None] * (C - 1.0 - idx))[:, :, None]
    chunk_decay = jnp.exp(log_g * C)[:, None, None]

    wq, wv = heads * dk, heads * dv
    kb, vb, gb = H * dk // wq, 2 * H * dk // wv, (2 * H * dk + H * dv) // wv
    per_head = lambda shape: pl.BlockSpec((heads,) + shape, lambda b, h: (h, 0, 0))
    out = pl.pallas_call(
        functools.partial(_retention_kernel, n_chunks=S // C, k_scale=dk ** -0.5, heads=heads),
        grid=(B, H // heads),
        in_specs=[pl.BlockSpec((1, S, wq), lambda b, h: (b, 0, h)),
                  pl.BlockSpec((1, S, wq), lambda b, h: (b, 0, kb + h)),
                  pl.BlockSpec((1, S, wv), lambda b, h: (b, 0, vb + h)),
                  pl.BlockSpec((1, S, wv), lambda b, h: (b, 0, gb + h)),
                  _resident((S, half), lambda b, h: (0, 0)),
                  _resident((S, half), lambda b, h: (0, 0)),
                  per_head((C, C)), per_head((C, 1)), per_head((C, 1)), per_head((1, 1))],
        out_specs=pl.BlockSpec((1, S, wv), lambda b, h: (b, 0, h)),
        out_shape=jax.ShapeDtypeStruct((B, S, H * dv), BF16),
        scratch_shapes=[pltpu.VMEM((heads, dk, dv), F32)],
        compiler_params=_params("parallel", "parallel"),
        name="retention",
    )(proj, proj, proj, proj, cos, sin, d_in, xi, zeta, chunk_decay)
    return out.reshape(B * S, H * dv)


ROUTE_E1, ROUTE_E2, ROUTE_G1, ROUTE_G2, ROUTE_R1, ROUTE_R2 = range(6)
ROUTE_ROWS = 8
ROUTER_EXP_LANE0 = MOE_GROUPS
HI16 = 0xFFFF0000


def _pack_bf16_pair(a, b):
    ua = lax.bitcast_convert_type(a.astype(BF16).astype(F32), jnp.uint32)
    ub = lax.bitcast_convert_type(b.astype(BF16).astype(F32), jnp.uint32)
    return ua | (ub >> 16)


def _unpack_bf16_pair(w):
    a = lax.bitcast_convert_type(w & jnp.uint32(HI16), F32)
    b = lax.bitcast_convert_type(w << 16, F32)
    return a, b


def _post_kernel(a_ref, w_ref, h_ref, g_ref, wr_ref, br_ref, tri_ref,
                 h1_ref, hn_ref, route_ref, route_t_ref, cnt_ref, carry_ref):
    @pl.when(pl.program_id(0) == 0)
    def _():
        carry_ref[...] = jnp.zeros_like(carry_ref)

    half = h_ref.shape[0] // 2
    D = h_ref.shape[1]
    lane = lax.broadcasted_iota(jnp.int32, (half, LANES), 1).astype(F32)
    ninf = -jnp.inf

    def first_argmax(vals):
        top = jnp.max(vals, axis=1, keepdims=True)
        where = jnp.min(jnp.where(vals == top, lane, float(LANES)), axis=1, keepdims=True)
        return top, where

    def route_half(rows):
        h1 = h_ref[rows, :] + jnp.dot(a_ref[rows, :], w_ref[...], preferred_element_type=F32)
        h1_ref[rows, :] = h1
        hn = _rms(h1) * g_ref[...]
        hn_ref[rows, :] = _pack_bf16_pair(hn[:, :D // 2], hn[:, D // 2:])
        hi = hn.astype(BF16)
        lo = (hn - hi.astype(F32)).astype(BF16)
        logits = jnp.dot(jnp.concatenate([hi, hi, lo], axis=1), wr_ref[...],
                         preferred_element_type=F32) + br_ref[...]

        is_grp = lane < MOE_GROUPS
        lg = jnp.where(is_grp, logits, ninf)
        mg, grp = first_argmax(lg)
        p_grp = 1.0 / jnp.sum(jnp.where(is_grp, jnp.exp(lg - mg), 0.0), axis=1, keepdims=True)

        e_lane = lane - ROUTER_EXP_LANE0
        in_grp = (e_lane < N_EXPERTS) & (jnp.floor(e_lane * (1.0 / EXPERTS_PER_GROUP)) == grp)
        le = jnp.where(in_grp, logits, ninf)
        v1, i1 = first_argmax(le)
        le2 = jnp.where(lane == i1, ninf, le)
        v2, i2 = first_argmax(le2)
        e = jnp.exp(v2 - v1)
        hit1 = lane == (i1 - ROUTER_EXP_LANE0)
        hit2 = lane == (i2 - ROUTER_EXP_LANE0)
        onehot = jnp.where(hit1 | hit2, 1.0, 0.0)
        earlier = jnp.dot(tri_ref[...], onehot.astype(BF16), preferred_element_type=F32)
        return dict(e1=i1 - ROUTER_EXP_LANE0, e2=i2 - ROUTER_EXP_LANE0, g1=p_grp / (1.0 + e),
                    g2=p_grp * e / (1.0 + e), hit1=hit1, hit2=hit2, earlier=earlier,
                    count=jnp.sum(onehot, axis=0, keepdims=True))

    halves = [route_half(slice(0, half)), route_half(slice(half, 2 * half))]
    carry = carry_ref[...]
    for j, r in enumerate(halves):
        before = carry + r["earlier"]
        r1 = jnp.sum(jnp.where(r["hit1"], before, 0.0), axis=1, keepdims=True)
        r2 = jnp.sum(jnp.where(r["hit2"], before, 0.0), axis=1, keepdims=True)
        carry = carry + r["count"]
        route = jnp.zeros((half, LANES), F32)
        for slot, val in ((ROUTE_E1, r["e1"]), (ROUTE_E2, r["e2"]), (ROUTE_G1, r["g1"]),
                          (ROUTE_G2, r["g2"]), (ROUTE_R1, r1), (ROUTE_R2, r2)):
            route = jnp.where(lane == slot, val, route)
        route_ref[j * half:(j + 1) * half, :] = route
        route_t_ref[:, j * half:(j + 1) * half] = route.T[:ROUTE_ROWS]
    carry_ref[...] = carry
    cnt_ref[...] = carry


def _post(a, w, h, gain, w_grp, b_grp, w_exp, b_exp):
    T, D = h.shape
    K = a.shape[1]
    tm = min(TOKEN_TILE, T)
    n_r = MOE_GROUPS + N_EXPERTS
    wr = jnp.zeros((D, LANES), F32).at[:, :n_r].set(jnp.concatenate([w_grp, w_exp], axis=1))
    wr_hi = wr.astype(BF16)
    wr_lo = (wr - wr_hi.astype(F32)).astype(BF16)
    wr3 = jnp.concatenate([wr_hi, wr_lo, wr_hi], axis=0)
    br = jnp.zeros((1, LANES), F32).at[0, :n_r].set(jnp.concatenate([b_grp, b_exp]))
    tri = jnp.tril(jnp.ones((tm // 2, tm // 2), BF16), k=-1)
    return pl.pallas_call(
        _post_kernel,
        grid=(T // tm,),
        in_specs=[pl.BlockSpec((tm, K), lambda i: (i, 0)),
                  _resident((K, D), lambda i: (0, 0)),
                  pl.BlockSpec((tm, D), lambda i: (i, 0)),
                  _resident((1, D), lambda i: (0, 0)),
                  _resident((3 * D, LANES), lambda i: (0, 0)),
                  _resident((1, LANES), lambda i: (0, 0)),
                  _resident((tm // 2, tm // 2), lambda i: (0, 0))],
        out_specs=[pl.BlockSpec((tm, D), lambda i: (i, 0)),
                   pl.BlockSpec((tm, D // 2), lambda i: (i, 0)),
                   pl.BlockSpec((tm, LANES), lambda i: (i, 0)),
                   pl.BlockSpec((ROUTE_ROWS, tm), lambda i: (0, i)),
                   pl.BlockSpec((1, LANES), lambda i: (0, 0))],
        out_shape=[jax.ShapeDtypeStruct((T, D), F32),
                   jax.ShapeDtypeStruct((T, D // 2), jnp.uint32),
                   jax.ShapeDtypeStruct((T, LANES), F32),
                   jax.ShapeDtypeStruct((ROUTE_ROWS, T), F32),
                   jax.ShapeDtypeStruct((1, LANES), F32)],
        scratch_shapes=[pltpu.VMEM((1, LANES), F32)],
        compiler_params=_params("arbitrary"),
        name="post_mixer",
    )(a, w, h, gain.reshape(1, D), wr3, br, tri)


def _dispatch_kernel(pends_ref, dest_ref, x_ref, o_ref, zero_ref, sem, zero_sem, *, tm):
    @pl.when(pl.program_id(0) == 0)
    def _():
        zero_ref[...] = jnp.zeros_like(zero_ref)
        fills = []
        for e in range(N_EXPERTS):
            end = pends_ref[e]
            nonempty = end > (pends_ref[e - 1] if e else 0)
            start = pl.multiple_of(jnp.maximum(end - MOE_BLOCK, 0), MOE_BLOCK)
            fills.append((nonempty, pltpu.make_async_copy(zero_ref, o_ref.at[pl.ds(start, MOE_BLOCK)], zero_sem)))
        for t in range(N_EXPERTS):
            start = pends_ref[N_EXPERTS - 1] + t * MOE_BLOCK
            in_range = start < o_ref.shape[0]
            start = pl.multiple_of(jnp.minimum(start, o_ref.shape[0] - MOE_BLOCK), MOE_BLOCK)
            fills.append((in_range, pltpu.make_async_copy(zero_ref, o_ref.at[pl.ds(start, MOE_BLOCK)], zero_sem)))
        for nonempty, fill in fills:
            pl.when(nonempty)(fill.start)
        for nonempty, fill in fills:
            pl.when(nonempty)(fill.wait)

    def row_copy(j, d):
        return pltpu.make_async_copy(x_ref.at[pl.ds(j, 1)], o_ref.at[pl.ds(d, 1)], sem)

    def issue(j, carry):
        row_copy(j, dest_ref[0, 0, j]).start()
        row_copy(j, dest_ref[0, 0, tm + j]).start()
        return carry

    lax.fori_loop(0, tm, issue, 0)
    for _ in range(2):
        pltpu.make_async_copy(x_ref, o_ref.at[pl.ds(0, tm)], sem).wait()


def _dispatch(x, dest, pends, P):
    T, W = x.shape
    tm = dest.shape[2] // 2
    nt = T // tm
    return pl.pallas_call(
        functools.partial(_dispatch_kernel, tm=tm),
        grid_spec=pltpu.PrefetchScalarGridSpec(
            num_scalar_prefetch=1,
            grid=(nt,),
            in_specs=[pl.BlockSpec((1, 1, 2 * tm), lambda i, pe: (i, 0, 0), memory_space=pltpu.SMEM),
                      pl.BlockSpec((tm, W), lambda i, pe: (i, 0))],
            out_specs=pl.BlockSpec(memory_space=pl.ANY),
            scratch_shapes=[pltpu.VMEM((MOE_BLOCK, W), x.dtype), pltpu.SemaphoreType.DMA(()),
                            pltpu.SemaphoreType.DMA(())]),
        out_shape=jax.ShapeDtypeStruct((P, W), x.dtype),
        compiler_params=_params("arbitrary"),
        name="dispatch",
    )(pends, dest, x)


def _expert_kernel(blk_exp_ref, n_used_ref, x_ref, wg_ref, wu_ref, wd_ref, o_ref,
                   wg_s, wu_s, wd_s):
    i = pl.program_id(0)
    used = i < n_used_ref[0]
    new_expert = (i == 0) | (blk_exp_ref[i] != blk_exp_ref[jnp.maximum(i - 1, 0)])
    W = x_ref.shape[1]

    @pl.when(used & new_expert)
    def _():
        wg_s[...] = wg_ref[0, 0].astype(BF16)
        wu_s[...] = wu_ref[0, 0].astype(BF16)
        wd_s[...] = wd_ref[0, 0].astype(BF16)

    @pl.when(used)
    def _():
        xa, xb = _unpack_bf16_pair(x_ref[...])
        xa, xb = xa.astype(BF16), xb.astype(BF16)

        def up(w_s):
            return (jnp.dot(xa, w_s[:W, :], preferred_element_type=F32)
                    + jnp.dot(xb, w_s[W:, :], preferred_element_type=F32))

        g = up(wg_s)
        hid = (g * jax.nn.sigmoid(g) * up(wu_s)).astype(BF16)
        y = jnp.dot(hid, wd_s[...], preferred_element_type=F32)
        o_ref[...] = _pack_bf16_pair(y[:, :W], y[:, W:])

    @pl.when(jnp.logical_not(used))
    def _():
        o_ref[...] = jnp.zeros_like(o_ref)


def _experts(xs, blk_exp, n_used, layer, w_gate, w_up, w_down):
    P, W = xs.shape
    D = 2 * W
    FF = w_gate.shape[3]
    nblk = P // MOE_BLOCK

    def x_map(i, be, nu):
        return (jnp.minimum(i, nu[0] - 1), 0)

    def w_map(i, be, nu):
        return (layer, be[jnp.minimum(i, nu[0] - 1)], 0, 0)

    return pl.pallas_call(
        _expert_kernel,
        grid_spec=pltpu.PrefetchScalarGridSpec(
            num_scalar_prefetch=2,
            grid=(nblk,),
            in_specs=[pl.BlockSpec((MOE_BLOCK, W), x_map),
                      pl.BlockSpec((1, 1, D, FF), w_map),
                      pl.BlockSpec((1, 1, D, FF), w_map),
                      pl.BlockSpec((1, 1, FF, D), w_map)],
            out_specs=pl.BlockSpec((MOE_BLOCK, W), lambda i, be, nu: (i, 0)),
            scratch_shapes=[pltpu.VMEM((D, FF), BF16), pltpu.VMEM((D, FF), BF16),
                            pltpu.VMEM((FF, D), BF16)]),
        out_shape=jax.ShapeDtypeStruct((P, W), jnp.uint32),
        compiler_params=_params("arbitrary"),
        name="experts",
    )(blk_exp, n_used, xs, w_gate, w_up, w_down)


def _combine_kernel(dest_ref, h_ref, route_ref, y_ref, o_ref, ya_ref, yb_ref, sem, *, tm):
    def issue(j, carry):
        pltpu.make_async_copy(y_ref.at[pl.ds(dest_ref[0, 0, j], 1)], ya_ref.at[pl.ds(j, 1)], sem).start()
        pltpu.make_async_copy(y_ref.at[pl.ds(dest_ref[0, 0, tm + j], 1)], yb_ref.at[pl.ds(j, 1)], sem).start()
        return carry

    lax.fori_loop(0, tm, issue, 0)
    for buf in (ya_ref, yb_ref):
        pltpu.make_async_copy(y_ref.at[pl.ds(0, tm)], buf, sem).wait()
    W = ya_ref.shape[1]
    route = route_ref[...]
    g1 = route[:, ROUTE_G1:ROUTE_G1 + 1]
    g2 = route[:, ROUTE_G2:ROUTE_G2 + 1]
    y1 = _unpack_bf16_pair(ya_ref[...])
    y2 = _unpack_bf16_pair(yb_ref[...])
    for part, cols in enumerate((slice(0, W), slice(W, 2 * W))):
        o_ref[:, cols] = h_ref[:, cols] + (g1 * y1[part] + g2 * y2[part])


def _combine(h, route, ys, dest):
    T, D = h.shape
    W = ys.shape[1]
    tm = dest.shape[2] // 2
    nt = T // tm
    return pl.pallas_call(
        functools.partial(_combine_kernel, tm=tm),
        grid=(nt,),
        in_specs=[pl.BlockSpec((1, 1, 2 * tm), lambda i: (i, 0, 0), memory_space=pltpu.SMEM),
                  pl.BlockSpec((tm, D), lambda i: (i, 0)),
                  pl.BlockSpec((tm, LANES), lambda i: (i, 0)),
                  pl.BlockSpec(memory_space=pl.ANY)],
        out_specs=pl.BlockSpec((tm, D), lambda i: (i, 0)),
        out_shape=jax.ShapeDtypeStruct((T, D), F32),
        scratch_shapes=[pltpu.VMEM((tm, W), ys.dtype), pltpu.VMEM((tm, W), ys.dtype),
                        pltpu.SemaphoreType.DMA(())],
        compiler_params=_params("arbitrary"),
        name="combine",
    )(dest, h, route, ys)


def _moe(h1, hn, route, route_t, counts, layer, w_gate, w_up, w_down):
    T, D = h1.shape
    tm = min(TOKEN_TILE, T)
    nt = T // tm
    A = 2 * T
    nblk = -(-A // MOE_BLOCK) + N_EXPERTS
    P = nblk * MOE_BLOCK
    eid = route_t[ROUTE_E1:ROUTE_E2 + 1].astype(jnp.int32)
    rank = route_t[ROUTE_R1:ROUTE_R2 + 1].astype(jnp.int32)
    cnt = counts[0, :N_EXPERTS].astype(jnp.int32)
    padded = (cnt + MOE_BLOCK - 1) // MOE_BLOCK * MOE_BLOCK
    pends = jnp.cumsum(padded)
    pstarts = pends - padded
    experts = jnp.arange(N_EXPERTS, dtype=jnp.int32)
    start_of = jnp.sum(jnp.where(eid[:, None, :] == experts[None, :, None], pstarts[None, :, None], 0), axis=1)
    dest = start_of + rank
    dest = dest.reshape(2, nt, tm).transpose(1, 0, 2).reshape(nt, 1, 2 * tm)
    blk_start = jnp.arange(nblk, dtype=jnp.int32) * MOE_BLOCK
    blk_exp = jnp.minimum(jnp.sum((pends[None, :] <= blk_start[:, None]).astype(jnp.int32), axis=1),
                          N_EXPERTS - 1)
    n_used = pends[-1:] // MOE_BLOCK
    xs = _dispatch(hn, dest, pends, P)
    ys = _experts(xs, blk_exp, n_used, layer, w_gate, w_up, w_down)
    return _combine(h1, route, ys, dest)


def _qkv_kernel(x_ref, gq_ref, gkv_ref, wq_ref, wk_ref, wvt_ref, qn_ref, kn_ref, seg_ref,
                q_ref, k_ref, vt_ref, xs_ref, *, rate, n, res_per_chunk):
    seg = seg_ref[...]
    width = seg.shape[0]

    def head_norm(t, gain):
        cols = []
        for j in range(t.shape[1] // width):
            tj = t[:, j * width:(j + 1) * width]
            ms = jnp.dot((tj * tj).astype(BF16), seg, preferred_element_type=F32)
            cols.append(tj * lax.rsqrt(ms + EPS))
        return jnp.concatenate(cols, axis=1) * gain

    def lane_chunk(j):
        return slice(j * LANES, (j + 1) * LANES)

    if rate > 1:
        for j in range(xs_ref.shape[0]):
            xs_ref[j] = x_ref[:, lane_chunk(j)]

    for c0 in range(0, rate, res_per_chunk):
        residues = range(c0, c0 + res_per_chunk)
        if rate == 1:
            x = x_ref[...]
        else:
            x = jnp.concatenate(
                [jnp.concatenate([xs_ref[j, pl.ds(c, n, stride=rate), :] for j in range(xs_ref.shape[0])], axis=1)
                 for c in residues], axis=0)
        y = _rms(x)
        xq = (y * gq_ref[...]).astype(BF16)
        xkv = (y * gkv_ref[...]).astype(BF16)
        q = head_norm(jnp.dot(xq, wq_ref[...], preferred_element_type=F32), qn_ref[...]).astype(q_ref.dtype)
        k = head_norm(jnp.dot(xkv, wk_ref[...], preferred_element_type=F32), kn_ref[...]).astype(k_ref.dtype)
        vt = lax.dot_general(wvt_ref[...], xkv, NT_DIMS, preferred_element_type=F32).astype(vt_ref.dtype)
        for j, c in enumerate(residues):
            q_ref[0, c] = q[j * n:(j + 1) * n]
            k_ref[0, c] = k[j * n:(j + 1) * n]
            vt_ref[0, c] = vt[:, j * n:(j + 1) * n]


def _qkv(h, B, S, rate, gq, gkv, wq, wk, wv, qn, kn):
    D = h.shape[1]
    L = S // rate
    hd = DIL_HEAD_DIM
    n = max(SUB_BLOCK, TOKEN_TILE // rate)
    tm = n * rate
    res_per_chunk = max(1, TOKEN_TILE // n)
    col_split = max(1, tm // (2 * TOKEN_TILE))
    dc = D // col_split
    width = 2 * LANES
    ii = jnp.arange(width)
    seg = jnp.where((ii[:, None] // hd) == (ii[None, :] // hd), 1.0 / hd, 0.0).astype(BF16)
    row = lambda g: jnp.tile(g, D // hd).reshape(1, D)
    qk_spec = pl.BlockSpec((1, rate, n, dc), lambda b, i, s: (b, 0, i, s))
    qk_shape = jax.ShapeDtypeStruct((B, rate, L, D), BF16)
    return pl.pallas_call(
        functools.partial(_qkv_kernel, rate=rate, n=n, res_per_chunk=res_per_chunk),
        grid=(B, S // tm, col_split),
        in_specs=[pl.BlockSpec((tm, D), lambda b, i, s: (b * (S // tm) + i, 0)),
                  _resident((1, D), lambda b, i, s: (0, 0)),
                  _resident((1, D), lambda b, i, s: (0, 0)),
                  pl.BlockSpec((D, dc), lambda b, i, s: (0, s)),
                  pl.BlockSpec((D, dc), lambda b, i, s: (0, s)),
                  pl.BlockSpec((dc, D), lambda b, i, s: (s, 0)),
                  pl.BlockSpec((1, dc), lambda b, i, s: (0, s)),
                  pl.BlockSpec((1, dc), lambda b, i, s: (0, s)),
                  _resident((width, width), lambda b, i, s: (0, 0))],
        out_specs=[qk_spec, qk_spec,
                   pl.BlockSpec((1, rate, dc, n), lambda b, i, s: (b, 0, s, i))],
        out_shape=[qk_shape, qk_shape, jax.ShapeDtypeStruct((B, rate, D, L), BF16)],
        scratch_shapes=[pltpu.VMEM((D // LANES, tm if rate > 1 else 8, LANES), F32)],
        compiler_params=_params("parallel", "parallel", "arbitrary"),
        name="qkv_rate%d" % rate,
    )(h, gq.reshape(1, D), gkv.reshape(1, D), wq, wk, wv.T, row(qn) * (hd ** -0.5 * LOG2E), row(kn), seg)


def _attn_kernel(*refs, rate, with_prev):
    if rate == 1:
        q_ref, kc_ref, kp_ref, vc_ref, vp_ref, edge_ref, bias_ref, o_ref, lse_ref, os_ref = refs
    elif with_prev:
        q_ref, kc_ref, kp_ref, vc_ref, vp_ref, edge_ref, o_ref, lse_ref, os_ref = refs
    else:
        q_ref, kc_ref, vc_ref, edge_ref, o_ref, lse_ref, os_ref = refs
    Bk = SUB_BLOCK
    hd = DIL_HEAD_DIM
    n_pairs = DIL_HEADS // 2
    lane = lax.broadcasted_iota(jnp.int32, (Bk, LANES), 1)

    def block(load_q, load_k, load_vt, load_bias, store):
        out_t, lse_t = [], []
        for p in range(n_pairs):
            qp = load_q(p)
            zero = jnp.zeros_like(qp)
            q2 = jnp.concatenate([jnp.where(lane < hd, qp, zero), jnp.where(lane >= hd, qp, zero)], axis=0)
            s = lax.dot_general(load_k(p), q2, NT_DIMS, preferred_element_type=F32) + load_bias(p)
            m = jnp.max(s, axis=0, keepdims=True)
            pr = jnp.exp2(s - m)
            l = jnp.sum(pr, axis=0, keepdims=True)
            pb = pr.astype(BF16)
            vt = load_vt(p)
            out_t.append(jnp.dot(vt[:hd], pb[:, :Bk], preferred_element_type=F32) / l[:, :Bk])
            out_t.append(jnp.dot(vt[hd:], pb[:, Bk:], preferred_element_type=F32) / l[:, Bk:])
            lse = (m + jnp.log2(l)) * LN2
            lse_t += [lse[:, :Bk], lse[:, Bk:]]
        lse_t.append(jnp.zeros((LANES - DIL_HEADS, Bk), F32))
        store(jnp.concatenate(out_t, axis=0).T, jnp.concatenate(lse_t, axis=0).T)

    def pair_cols(p):
        return slice(p * LANES, (p + 1) * LANES)

    edge_bias = lambda p: edge_ref[0, p]
    if rate == 1:
        for j in range(q_ref.shape[2] // Bk):
            rows = slice(j * Bk, (j + 1) * Bk)
            if j == 0:
                load_k = lambda p: jnp.concatenate([kp_ref[0, 0, :, pair_cols(p)],
                                                    kc_ref[0, 0, :Bk, pair_cols(p)]], axis=0)
                load_vt = lambda p: jnp.concatenate([vp_ref[0, 0, pair_cols(p), :],
                                                     vc_ref[0, 0, pair_cols(p), :Bk]], axis=1)
                load_bias = edge_bias
            else:
                krows = slice((j - 1) * Bk, (j + 1) * Bk)
                load_k = lambda p, krows=krows: kc_ref[0, 0, krows, pair_cols(p)]
                load_vt = lambda p, krows=krows: vc_ref[0, 0, pair_cols(p), krows]
                load_bias = lambda p: bias_ref[p]

            def store(o, lse, rows=rows):
                o_ref[rows, :] = o.astype(o_ref.dtype)
                lse_ref[rows, :] = lse

            block(lambda p, rows=rows: q_ref[0, 0, rows, pair_cols(p)], load_k, load_vt, load_bias, store)
    else:
        for c in range(q_ref.shape[1]):
            if with_prev:
                load_k = lambda p, c=c: jnp.concatenate([kp_ref[0, c, :, pair_cols(p)],
                                                         kc_ref[0, c, :, pair_cols(p)]], axis=0)
                load_vt = lambda p, c=c: jnp.concatenate([vp_ref[0, c, pair_cols(p), :],
                                                          vc_ref[0, c, pair_cols(p), :]], axis=1)
            else:
                load_k = lambda p, c=c: kc_ref[0, c, :, pair_cols(p)]
                load_vt = lambda p, c=c: vc_ref[0, c, pair_cols(p), :]

            def store(o, lse, c=c):
                for j in range(os_ref.shape[0]):
                    os_ref[j, pl.ds(c, Bk, stride=rate), :] = o[:, pair_cols(j)]
                lse_ref[pl.ds(c, Bk, stride=rate), :] = lse

            block(lambda p, c=c: q_ref[0, c, :, pair_cols(p)], load_k, load_vt, edge_bias, store)
        for j in range(os_ref.shape[0]):
            o_ref[:, pair_cols(j)] = os_ref[j].astype(o_ref.dtype)


def _t5_bucket(n):
    max_exact = NUM_BUCKETS // 2
    nf = jnp.maximum(n, max_exact).astype(F32)
    large = max_exact + (jnp.log(nf / max_exact) / math.log(MAX_DISTANCE / max_exact)
                         * (NUM_BUCKETS - max_exact)).astype(jnp.int32)
    large = jnp.minimum(large, NUM_BUCKETS - 1)
    return jnp.where(n < max_exact, n, large)


def _group_attention(q, k, vt, bias_table, rate, n_steps):
    B, _, L, D = q.shape
    S = L * rate
    Bk = SUB_BLOCK
    with_prev = L > Bk
    nk = 2 * Bk if with_prev else Bk
    n = max(Bk, TOKEN_TILE // rate)
    tm = n * rate
    nt = S // tm
    n_pairs = DIL_HEADS // 2
    ql = jnp.arange(Bk, dtype=jnp.int32)[:, None]
    kl = jnp.arange(2 * Bk, dtype=jnp.int32)[None, :]
    steps = ql + Bk - kl
    bucket = _t5_bucket(jnp.maximum(steps, 0) * rate)
    buckets = jnp.arange(NUM_BUCKETS, dtype=jnp.int32)
    bias = jnp.sum(jnp.where(bucket[None, :, :, None] == buckets[:, None, None, None],
                             bias_table.astype(F32)[:, None, None, :], 0.0), axis=0)
    band = ((steps >= 0) & (steps <= n_steps))[:, :, None]
    first = (kl < Bk)[:, :, None]

    def layout(t):
        t = t[:, 2 * Bk - nk:, :].transpose(1, 2, 0)
        return t.reshape(nk, n_pairs, 2 * Bk).transpose(1, 0, 2)

    bias_in = layout(jnp.where(band, bias * LOG2E, NEG_INF))
    bias_first = layout(jnp.where(band & ~first, bias * LOG2E, NEG_INF))
    edge = jnp.stack([bias_first, bias_in])

    cur_qk = pl.BlockSpec((1, rate, n, D), lambda b, i: (b, 0, i, 0))
    cur_vt = pl.BlockSpec((1, rate, D, n), lambda b, i: (b, 0, 0, i))
    per_n = n // Bk
    prev_qk = pl.BlockSpec((1, rate, Bk, D), lambda b, i: (b, 0, jnp.maximum(i * per_n - 1, 0), 0))
    prev_vt = pl.BlockSpec((1, rate, D, Bk), lambda b, i: (b, 0, 0, jnp.maximum(i * per_n - 1, 0)))
    edge_spec = pl.BlockSpec((1, n_pairs, nk, 2 * Bk), lambda b, i: (jnp.minimum(i, 1), 0, 0, 0))
    if with_prev:
        in_specs = [cur_qk, cur_qk, prev_qk, cur_vt, prev_vt, edge_spec]
        args = (q, k, k, vt, vt, edge)
    else:
        in_specs = [cur_qk, cur_qk, cur_vt, edge_spec]
        args = (q, k, vt, edge)
    if rate == 1:
        in_specs.append(_resident((n_pairs, nk, 2 * Bk), lambda b, i: (0, 0, 0)))
        args += (bias_in,)
    return pl.pallas_call(
        functools.partial(_attn_kernel, rate=rate, with_prev=with_prev),
        grid=(B, nt),
        in_specs=in_specs,
        out_specs=[pl.BlockSpec((tm, D), lambda b, i: (b * nt + i, 0)),
                   pl.BlockSpec((tm, LANES), lambda b, i: (b * nt + i, 0))],
        out_shape=[jax.ShapeDtypeStruct((B * S, D), BF16),
                   jax.ShapeDtypeStruct((B * S, LANES), F32)],
        scratch_shapes=[pltpu.VMEM((D // LANES, tm if rate > 1 else 8, LANES), F32)],
        compiler_params=_params("parallel", "parallel"),
        name="attn_rate%d" % rate,
    )(*args)


def _merge_kernel(o0_ref, o1_ref, o2_ref, l0_ref, l1_ref, l2_ref, ex_ref, out_ref):
    ex = ex_ref[...]

    def expand(w):
        hi = w.astype(BF16)
        r1 = w - hi.astype(F32)
        mid = r1.astype(BF16)
        lo = (r1 - mid.astype(F32)).astype(BF16)
        return (jnp.dot(hi, ex, preferred_element_type=F32) + jnp.dot(mid, ex, preferred_element_type=F32)
                + jnp.dot(lo, ex, preferred_element_type=F32))

    l0, l1, l2 = l0_ref[...], l1_ref[...], l2_ref[...]
    top = jnp.maximum(jnp.maximum(l0, l1), l2)
    w0, w1, w2 = jnp.exp(l0 - top), jnp.exp(l1 - top), jnp.exp(l2 - top)
    den = w0 + w1 + w2
    acc = expand(w0 / den) * o0_ref[...].astype(F32)
    acc = acc + expand(w1 / den) * o1_ref[...].astype(F32)
    acc = acc + expand(w2 / den) * o2_ref[...].astype(F32)
    out_ref[...] = acc.astype(out_ref.dtype)


def _merge(outs, lses):
    T, D = outs[0].shape
    tm = min(TOKEN_TILE, T)
    ex = jnp.where(jnp.arange(LANES)[:, None] == (jnp.arange(D)[None, :] // DIL_HEAD_DIM), 1.0, 0.0).astype(BF16)
    o_spec = pl.BlockSpec((tm, D), lambda i: (i, 0))
    l_spec = pl.BlockSpec((tm, LANES), lambda i: (i, 0))
    return pl.pallas_call(
        _merge_kernel,
        grid=(T // tm,),
        in_specs=[o_spec] * 3 + [l_spec] * 3 + [_resident((LANES, D), lambda i: (0, 0))],
        out_specs=o_spec,
        out_shape=jax.ShapeDtypeStruct((T, D), BF16),
        compiler_params=_params("parallel"),
        name="merge_groups",
    )(*outs, *lses, ex)


def kernel(x, ret_w_in, ret_w_out, kv_norm, w_kv, k_norm, dil_wq, q_norm, dil_wo, rel_bias,
           mixer_norm, ffn_norm, router_grp, router_grp_b, router_exp, router_exp_b,
           exp_gate, exp_up, exp_down):
    B, S, D = x.shape
    h = x.reshape(B * S, D)

    def moe_layer(layer, a, w_out, h):
        h1, hn, route, route_t, counts = _post(
            a, w_out.astype(BF16), h, ffn_norm[layer], router_grp[layer], router_grp_b[layer],
            router_exp[layer], router_exp_b[layer])
        return _moe(h1, hn, route, route_t, counts, layer, exp_gate, exp_up, exp_down)

    proj = _norm_proj(h, mixer_norm[0], ret_w_in[0].astype(BF16))
    y = _retention(proj, B, S)
    h = moe_layer(0, y, ret_w_out[0], h)

    G = len(DIL_RATES)
    gd = DIL_HEADS * DIL_HEAD_DIM
    outs, lses = [], []
    for g in range(G):
        cq = slice(g * gd, (g + 1) * gd)
        cv = slice(G * gd + g * gd, G * gd + (g + 1) * gd)
        q, k, vt = _qkv(h, B, S, DIL_RATES[g], mixer_norm[1], kv_norm,
                        dil_wq[0][:, cq].astype(BF16), w_kv[:, cq].astype(BF16), w_kv[:, cv].astype(BF16),
                        q_norm[0][g], k_norm[g])
        o, lse = _group_attention(q, k, vt, rel_bias[:, g * DIL_HEADS:(g + 1) * DIL_HEADS],
                                  DIL_RATES[g], DIL_WINDOWS[g] // DIL_RATES[g])
        outs.append(o)
        lses.append(lse)
    att = _merge(outs, lses)
    h = moe_layer(1, att, dil_wo[0], h)
    return h.reshape(B, S, D)
```

```python
import functools
import math

import jax
import jax.numpy as jnp
from jax import lax
from jax.experimental import pallas as pl
from jax.experimental.pallas import tpu as pltpu
from jax.experimental.pallas import tpu_sc as plsc

F32 = jnp.float32
BF16 = jnp.bfloat16

EPS = 1e-6
NEG_INF = -1e30

RET_HEADS = 4
RET_CHUNK = 128
ROPE_BASE = 10000.0

DIL_WINDOWS = (128, 512, 2048)
DIL_RATES = (1, 4, 16)
DIL_HEADS = 16
DIL_HEAD_DIM = 64
SUB_BLOCK = 128
NUM_BUCKETS = 32
MAX_DISTANCE = 2048

MOE_GROUPS = 4
EXPERTS_PER_GROUP = 8
N_EXPERTS = MOE_GROUPS * EXPERTS_PER_GROUP
MOE_BLOCK = 256

LANES = 128
TOKEN_TILE = 512
VMEM_LIMIT = 56 * 1024 * 1024

NT_DIMS = (((1,), (1,)), ((), ()))
LOG2E = math.log2(math.e)
LN2 = math.log(2.0)


def _params(*sem):
    return pltpu.CompilerParams(dimension_semantics=sem, vmem_limit_bytes=VMEM_LIMIT)


def _resident(shape, index_map):
    return pl.BlockSpec(shape, index_map, pipeline_mode=pl.Buffered(1))


def _rms(x):
    return x * lax.rsqrt(jnp.mean(x * x, axis=-1, keepdims=True) + EPS)


def _norm_proj_kernel(x_ref, g_ref, w_ref, o_ref, *, col_tile):
    xn = (_rms(x_ref[...]) * g_ref[...]).astype(BF16)
    for j in range(w_ref.shape[1] // col_tile):
        cols = slice(j * col_tile, (j + 1) * col_tile)
        o_ref[:, cols] = jnp.dot(xn, w_ref[:, cols], preferred_element_type=F32).astype(o_ref.dtype)


def _norm_proj(x, gain, w):
    T, D = x.shape
    N = w.shape[1]
    tm = min(TOKEN_TILE, T)
    return pl.pallas_call(
        functools.partial(_norm_proj_kernel, col_tile=512),
        grid=(T // tm,),
        in_specs=[pl.BlockSpec((tm, D), lambda i: (i, 0)),
                  _resident((1, D), lambda i: (0, 0)),
                  _resident((D, N), lambda i: (0, 0))],
        out_specs=pl.BlockSpec((tm, N), lambda i: (i, 0)),
        out_shape=jax.ShapeDtypeStruct((T, N), BF16),
        compiler_params=_params("parallel"),
        name="norm_proj",
    )(x, gain.reshape(1, D), w)


def _retention_kernel(q_ref, k_ref, v_ref, gate_ref, cos_ref, sin_ref, din_ref, xi_ref, zeta_ref,
                      cd_ref, o_ref, state_ref, *, n_chunks, k_scale, heads):
    C = RET_CHUNK
    dk = q_ref.shape[2] // heads
    dv = v_ref.shape[2] // heads
    half = dk // 2
    state_ref[...] = jnp.zeros_like(state_ref)

    def rot(t, cos, sin):
        t1, t2 = t[:, :half], t[:, half:]
        return jnp.concatenate([t1 * cos - t2 * sin, t1 * sin + t2 * cos], axis=-1)

    def body(c, carry):
        rows = pl.ds(pl.multiple_of(c * C, C), C)
        cos = cos_ref[rows, :]
        sin = sin_ref[rows, :]
        for hh in range(heads):
            qk_cols = slice(hh * dk, (hh + 1) * dk)
            v_cols = slice(hh * dv, (hh + 1) * dv)
            q = rot(q_ref[0, rows, qk_cols].astype(F32), cos, sin)
            k = rot(k_ref[0, rows, qk_cols].astype(F32), cos, sin) * k_scale
            v = v_ref[0, rows, v_cols]
            qb = q.astype(BF16)
            kb = k.astype(BF16)
            s = lax.dot_general(qb, kb, NT_DIMS, preferred_element_type=F32) * din_ref[hh]
            inner = jnp.dot(s.astype(BF16), v, preferred_element_type=F32)
            state = state_ref[hh]
            cross = jnp.dot(qb, state.astype(BF16), preferred_element_type=F32) * xi_ref[hh]
            kz_t = (k * zeta_ref[hh]).T.astype(BF16)
            state_ref[hh] = state * cd_ref[hh] + jnp.dot(kz_t, v, preferred_element_type=F32)
            o = _rms(inner + cross)
            g = gate_ref[0, rows, v_cols].astype(F32)
            o_ref[0, rows, v_cols] = (g * jax.nn.sigmoid(g) * o).astype(o_ref.dtype)
        return carry

    lax.fori_loop(0, n_chunks, body, 0)


def _retention(proj, B, S):
    H, C = RET_HEADS, RET_CHUNK
    D = proj.shape[1] // 6
    dk, dv = D // H, 2 * D // H
    half = dk // 2
    heads = 2
    proj = proj.reshape(B, S, 6 * D)

    pos = jnp.arange(S, dtype=F32)
    inv = 1.0 / (ROPE_BASE ** jnp.linspace(0.0, 1.0, half, dtype=F32))
    ang = pos[:, None] * inv[None, :]
    cos, sin = jnp.cos(ang), jnp.sin(ang)
    log_g = jnp.log(1.0 - 2.0 ** (-5.0 - jnp.arange(H, dtype=F32)))
    idx = jnp.arange(C, dtype=F32)
    diff = idx[:, None] - idx[None, :]
    d_in = jnp.where(diff >= 0, jnp.exp(log_g[:, None, None] * jnp.maximum(diff, 0.0)), 0.0)
    xi = jnp.exp(log_g[:, None] * (idx + 1.0))[:, :, None]
    zeta = jnp.exp(log_g[:, None] * (C - 1.0 - idx))[:, :, None]
    chunk_decay = jnp.exp(log_g * C)[:, None, None]

    wq, wv = heads * dk, heads * dv
    kb, vb, gb = H * dk // wq, 2 * H * dk // wv, (2 * H * dk + H * dv) // wv
    per_head = lambda shape: pl.BlockSpec((heads,) + shape, lambda b, h: (h, 0, 0))
    out = pl.pallas_call(
        functools.partial(_retention_kernel, n_chunks=S // C, k_scale=dk ** -0.5, heads=heads),
        grid=(B, H // heads),
        in_specs=[pl.BlockSpec((1, S, wq), lambda b, h: (b, 0, h)),
                  pl.BlockSpec((1, S, wq), lambda b, h: (b, 0, kb + h)),
                  pl.BlockSpec((1, S, wv), lambda b, h: (b, 0, vb + h)),
                  pl.BlockSpec((1, S, wv), lambda b, h: (b, 0, gb + h)),
                  _resident((S, half), lambda b, h: (0, 0)),
                  _resident((S, half), lambda b, h: (0, 0)),
                  per_head((C, C)), per_head((C, 1)), per_head((C, 1)), per_head((1, 1))],
        out_specs=pl.BlockSpec((1, S, wv), lambda b, h: (b, 0, h)),
        out_shape=jax.ShapeDtypeStruct((B, S, H * dv), BF16),
        scratch_shapes=[pltpu.VMEM((heads, dk, dv), F32)],
        compiler_params=_params("parallel", "parallel"),
        name="retention",
    )(proj, proj, proj, proj, cos, sin, d_in, xi, zeta, chunk_decay)
    return out.reshape(B * S, H * dv)


ROUTE_E1, ROUTE_E2, ROUTE_G1, ROUTE_G2, ROUTE_R1, ROUTE_R2 = range(6)
ROUTE_ROWS = 8
ROUTER_EXP_LANE0 = MOE_GROUPS
HI16 = 0xFFFF0000


def _pack_bf16_pair(a, b):
    ua = lax.bitcast_convert_type(a.astype(BF16).astype(F32), jnp.uint32)
    ub = lax.bitcast_convert_type(b.astype(BF16).astype(F32), jnp.uint32)
    return ua | (ub >> 16)


def _unpack_bf16_pair(w):
    a = lax.bitcast_convert_type(w & jnp.uint32(HI16), F32)
    b = lax.bitcast_convert_type(w << 16, F32)
    return a, b


ROW_PARTS = 2


def _part_cols(D, part):
    w = D // 2 // ROW_PARTS
    return slice(part * w, (part + 1) * w), slice(D // 2 + part * w, D // 2 + (part + 1) * w)


def _pack_part(x, part):
    hi, lo = _part_cols(x.shape[1], part)
    return _pack_bf16_pair(x[:, hi], x[:, lo])


def _post_kernel(a_ref, w_ref, h_ref, g_ref, wr_ref, br_ref, tri_ref,
                 h1_ref, hn0_ref, hn1_ref, route_ref, route_t_ref, cnt_ref, carry_ref):
    hn_refs = (hn0_ref, hn1_ref)
    @pl.when(pl.program_id(0) == 0)
    def _():
        carry_ref[...] = jnp.zeros_like(carry_ref)

    half = h_ref.shape[0] // 2
    D = h_ref.shape[1]
    lane = lax.broadcasted_iota(jnp.int32, (half, LANES), 1).astype(F32)
    ninf = -jnp.inf

    def first_argmax(vals):
        top = jnp.max(vals, axis=1, keepdims=True)
        where = jnp.min(jnp.where(vals == top, lane, float(LANES)), axis=1, keepdims=True)
        return top, where

    def route_half(rows):
        h1 = h_ref[rows, :] + jnp.dot(a_ref[rows, :], w_ref[...], preferred_element_type=F32)
        h1_ref[rows, :] = h1
        hn = _rms(h1) * g_ref[...]
        for part, ref in enumerate(hn_refs):
            ref[rows, :] = _pack_part(hn, part)
        hi = hn.astype(BF16)
        lo = (hn - hi.astype(F32)).astype(BF16)
        logits = jnp.dot(jnp.concatenate([hi, hi, lo], axis=1), wr_ref[...],
                         preferred_element_type=F32) + br_ref[...]

        is_grp = lane < MOE_GROUPS
        lg = jnp.where(is_grp, logits, ninf)
        mg, grp = first_argmax(lg)
        p_grp = 1.0 / jnp.sum(jnp.where(is_grp, jnp.exp(lg - mg), 0.0), axis=1, keepdims=True)

        e_lane = lane - ROUTER_EXP_LANE0
        in_grp = (e_lane < N_EXPERTS) & (jnp.floor(e_lane * (1.0 / EXPERTS_PER_GROUP)) == grp)
        le = jnp.where(in_grp, logits, ninf)
        v1, i1 = first_argmax(le)
        le2 = jnp.where(lane == i1, ninf, le)
        v2, i2 = first_argmax(le2)
        e = jnp.exp(v2 - v1)
        hit1 = lane == (i1 - ROUTER_EXP_LANE0)
        hit2 = lane == (i2 - ROUTER_EXP_LANE0)
        onehot = jnp.where(hit1 | hit2, 1.0, 0.0)
        earlier = jnp.dot(tri_ref[...], onehot.astype(BF16), preferred_element_type=F32)
        return dict(e1=i1 - ROUTER_EXP_LANE0, e2=i2 - ROUTER_EXP_LANE0, g1=p_grp / (1.0 + e),
                    g2=p_grp * e / (1.0 + e), hit1=hit1, hit2=hit2, earlier=earlier,
                    count=jnp.sum(onehot, axis=0, keepdims=True))

    halves = [route_half(slice(0, half)), route_half(slice(half, 2 * half))]
    carry = carry_ref[...]
    for j, r in enumerate(halves):
        before = carry + r["earlier"]
        r1 = jnp.sum(jnp.where(r["hit1"], before, 0.0), axis=1, keepdims=True)
        r2 = jnp.sum(jnp.where(r["hit2"], before, 0.0), axis=1, keepdims=True)
        carry = carry + r["count"]
        route = jnp.zeros((half, LANES), F32)
        for slot, val in ((ROUTE_E1, r["e1"]), (ROUTE_E2, r["e2"]), (ROUTE_G1, r["g1"]),
                          (ROUTE_G2, r["g2"]), (ROUTE_R1, r1), (ROUTE_R2, r2)):
            route = jnp.where(lane == slot, val, route)
        route_ref[j * half:(j + 1) * half, :] = route
        route_t_ref[:, j * half:(j + 1) * half] = route.T[:ROUTE_ROWS]
    carry_ref[...] = carry
    cnt_ref[...] = carry


def _post(a, w, h, gain, w_grp, b_grp, w_exp, b_exp):
    T, D = h.shape
    K = a.shape[1]
    tm = min(TOKEN_TILE, T)
    n_r = MOE_GROUPS + N_EXPERTS
    wr = jnp.zeros((D, LANES), F32).at[:, :n_r].set(jnp.concatenate([w_grp, w_exp], axis=1))
    wr_hi = wr.astype(BF16)
    wr_lo = (wr - wr_hi.astype(F32)).astype(BF16)
    wr3 = jnp.concatenate([wr_hi, wr_lo, wr_hi], axis=0)
    br = jnp.zeros((1, LANES), F32).at[0, :n_r].set(jnp.concatenate([b_grp, b_exp]))
    tri = jnp.tril(jnp.ones((tm // 2, tm // 2), BF16), k=-1)
    return pl.pallas_call(
        _post_kernel,
        grid=(T // tm,),
        in_specs=[pl.BlockSpec((tm, K), lambda i: (i, 0)),
                  _resident((K, D), lambda i: (0, 0)),
                  pl.BlockSpec((tm, D), lambda i: (i, 0)),
                  _resident((1, D), lambda i: (0, 0)),
                  _resident((3 * D, LANES), lambda i: (0, 0)),
                  _resident((1, LANES), lambda i: (0, 0)),
                  _resident((tm // 2, tm // 2), lambda i: (0, 0))],
        out_specs=[pl.BlockSpec((tm, D), lambda i: (i, 0))]
                  + [pl.BlockSpec((tm, D // 2 // ROW_PARTS), lambda i: (i, 0))] * ROW_PARTS
                  + [pl.BlockSpec((tm, LANES), lambda i: (i, 0)),
                   pl.BlockSpec((ROUTE_ROWS, tm), lambda i: (0, i)),
                   pl.BlockSpec((1, LANES), lambda i: (0, 0))],
        out_shape=[jax.ShapeDtypeStruct((T, D), F32)]
                  + [jax.ShapeDtypeStruct((T, D // 2 // ROW_PARTS), jnp.uint32)] * ROW_PARTS
                  + [jax.ShapeDtypeStruct((T, LANES), F32),
                   jax.ShapeDtypeStruct((ROUTE_ROWS, T), F32),
                   jax.ShapeDtypeStruct((1, LANES), F32)],
        scratch_shapes=[pltpu.VMEM((1, LANES), F32)],
        compiler_params=_params("arbitrary"),
        name="post_mixer",
    )(a, w, h, gain.reshape(1, D), wr3, br, tri)


SC_WINDOW = 128


def _sc_mesh():
    return plsc.VectorSubcoreMesh(core_axis_name="core", subcore_axis_name="subcore")


def _sc_window_specs(W):
    rows = pl.BlockSpec((SC_WINDOW, W), lambda i: (i, 0))
    idx = pl.BlockSpec((1, SC_WINDOW), lambda i: (0, i))
    return rows, idx


def _sc_pipeline(body, n_rows, in_specs, out_specs):
    return pltpu.emit_pipeline(body, grid=(n_rows // SC_WINDOW,), in_specs=in_specs, out_specs=out_specs,
                               core_axis_name=("core", "subcore"), dimension_semantics=(pltpu.PARALLEL,))


def _dispatch(xs, dest, pad_idx, P):
    T, W = xs[0].shape
    n_pad = pad_idx.shape[1]
    n = len(xs)
    rows, idx = _sc_window_specs(W)
    out = jax.ShapeDtypeStruct((P, W), xs[0].dtype)

    @functools.partial(pl.kernel, out_type=(out,) * n, mesh=_sc_mesh(), scratch_types=[], name="dispatch")
    def scatter(*refs):
        x_hbm, (d0_hbm, d1_hbm, z_hbm, p_hbm), o_hbm = refs[:n], refs[n:n + 4], refs[n + 4:]
        for x, o in zip(x_hbm, o_hbm):
            def put(x_vmem, i_vmem, o=o):
                pltpu.sync_copy(x_vmem, o.at[i_vmem.at[0]])

            _sc_pipeline(put, T, [rows, idx], [])(x, d0_hbm)
            _sc_pipeline(put, T, [rows, idx], [])(x, d1_hbm)
            _sc_pipeline(put, n_pad, [rows, idx], [])(z_hbm, p_hbm)

    return scatter(*xs, dest[0:1], dest[1:2], jnp.zeros((n_pad, W), xs[0].dtype), pad_idx)


def _expert_kernel(blk_exp_ref, n_used_ref, *refs):
    x_refs = refs[:ROW_PARTS]
    wg_ref, wu_ref, wd_ref = refs[ROW_PARTS:ROW_PARTS + 3]
    o_refs = refs[ROW_PARTS + 3:2 * ROW_PARTS + 3]
    wg_s, wu_s, wd_s = refs[2 * ROW_PARTS + 3:]
    i = pl.program_id(0)
    used = i < n_used_ref[0]
    new_expert = (i == 0) | (blk_exp_ref[i] != blk_exp_ref[jnp.maximum(i - 1, 0)])
    D = wg_s.shape[0]

    @pl.when(used & new_expert)
    def _():
        wg_s[...] = wg_ref[0, 0].astype(BF16)
        wu_s[...] = wu_ref[0, 0].astype(BF16)
        wd_s[...] = wd_ref[0, 0].astype(BF16)

    @pl.when(used)
    def _():
        pieces = []
        for part, x_ref in enumerate(x_refs):
            for cols, val in zip(_part_cols(D, part), _unpack_bf16_pair(x_ref[...])):
                pieces.append((cols, val.astype(BF16)))

        def up(w_s):
            return sum(jnp.dot(val, w_s[cols, :], preferred_element_type=F32) for cols, val in pieces)

        g = up(wg_s)
        hid = (g * jax.nn.sigmoid(g) * up(wu_s)).astype(BF16)
        y = jnp.dot(hid, wd_s[...], preferred_element_type=F32)
        for part, o_ref in enumerate(o_refs):
            o_ref[...] = _pack_part(y, part)

    @pl.when(jnp.logical_not(used))
    def _():
        for o_ref in o_refs:
            o_ref[...] = jnp.zeros_like(o_ref)


def _experts(xs, blk_exp, n_used, layer, w_gate, w_up, w_down):
    P, W = xs[0].shape
    D = 2 * W * ROW_PARTS
    FF = w_gate.shape[3]
    nblk = P // MOE_BLOCK

    def x_map(i, be, nu):
        return (jnp.minimum(i, nu[0] - 1), 0)

    def w_map(i, be, nu):
        return (layer, be[jnp.minimum(i, nu[0] - 1)], 0, 0)

    return pl.pallas_call(
        _expert_kernel,
        grid_spec=pltpu.PrefetchScalarGridSpec(
            num_scalar_prefetch=2,
            grid=(nblk,),
            in_specs=[pl.BlockSpec((MOE_BLOCK, W), x_map)] * ROW_PARTS
                     + [pl.BlockSpec((1, 1, D, FF), w_map),
                        pl.BlockSpec((1, 1, D, FF), w_map),
                        pl.BlockSpec((1, 1, FF, D), w_map)],
            out_specs=[pl.BlockSpec((MOE_BLOCK, W), lambda i, be, nu: (i, 0))] * ROW_PARTS,
            scratch_shapes=[pltpu.VMEM((D, FF), BF16), pltpu.VMEM((D, FF), BF16),
                            pltpu.VMEM((FF, D), BF16)]),
        out_shape=[jax.ShapeDtypeStruct((P, W), jnp.uint32)] * ROW_PARTS,
        compiler_params=_params("arbitrary"),
        name="experts",
    )(blk_exp, n_used, *xs, w_gate, w_up, w_down)


def _gather_pairs(ys, dest):
    W = ys[0].shape[1]
    T = dest.shape[1]
    n = len(ys)
    rows, idx = _sc_window_specs(W)
    out = jax.ShapeDtypeStruct((T, W), ys[0].dtype)

    @functools.partial(pl.kernel, out_type=(out,) * (2 * n), mesh=_sc_mesh(), scratch_types=[],
                       name="gather_pairs")
    def gather(*refs):
        y_hbm, d_hbm, o_hbm = refs[:n], refs[n:n + 2], refs[n + 2:]
        for slot, d in enumerate(d_hbm):
            for y, o in zip(y_hbm, o_hbm[slot * n:(slot + 1) * n]):
                def get(i_vmem, o_vmem, y=y):
                    pltpu.sync_copy(y.at[i_vmem.at[0]], o_vmem)

                _sc_pipeline(get, T, [idx], [rows])(d, o)

    return gather(*ys, dest[0:1], dest[1:2])


def _combine_kernel(h_ref, route_ref, *refs):
    y_refs, o_ref = refs[:-1], refs[-1]
    D = h_ref.shape[1]
    route = route_ref[...]
    gates = (route[:, ROUTE_G1:ROUTE_G1 + 1], route[:, ROUTE_G2:ROUTE_G2 + 1])
    for part in range(ROW_PARTS):
        slots = [_unpack_bf16_pair(y_refs[slot * ROW_PARTS + part][...]) for slot in range(2)]
        for half, cols in enumerate(_part_cols(D, part)):
            o_ref[:, cols] = h_ref[:, cols] + (gates[0] * slots[0][half] + gates[1] * slots[1][half])


def _combine(h, route, pairs):
    T, D = h.shape
    W = pairs[0].shape[1]
    tm = min(TOKEN_TILE, T)
    return pl.pallas_call(
        _combine_kernel,
        grid=(T // tm,),
        in_specs=[pl.BlockSpec((tm, D), lambda i: (i, 0)), pl.BlockSpec((tm, LANES), lambda i: (i, 0))]
                 + [pl.BlockSpec((tm, W), lambda i: (i, 0))] * len(pairs),
        out_specs=pl.BlockSpec((tm, D), lambda i: (i, 0)),
        out_shape=jax.ShapeDtypeStruct((T, D), F32),
        compiler_params=_params("parallel"),
        name="combine",
    )(h, route, *pairs)


def _moe(h1, hn, route, route_t, counts, layer, w_gate, w_up, w_down):
    T, D = h1.shape
    A = 2 * T
    nblk = -(-A // MOE_BLOCK) + N_EXPERTS
    P = nblk * MOE_BLOCK
    eid = route_t[ROUTE_E1:ROUTE_E2 + 1].astype(jnp.int32)
    rank = route_t[ROUTE_R1:ROUTE_R2 + 1].astype(jnp.int32)
    cnt = counts[0, :N_EXPERTS].astype(jnp.int32)
    padded = (cnt + MOE_BLOCK - 1) // MOE_BLOCK * MOE_BLOCK
    pends = jnp.cumsum(padded)
    pstarts = pends - padded
    experts = jnp.arange(N_EXPERTS, dtype=jnp.int32)
    start_of = jnp.sum(jnp.where(eid[:, None, :] == experts[None, :, None], pstarts[None, :, None], 0), axis=1)
    dest = start_of + rank
    blk_start = jnp.arange(nblk, dtype=jnp.int32) * MOE_BLOCK
    blk_exp = jnp.minimum(jnp.sum((pends[None, :] <= blk_start[:, None]).astype(jnp.int32), axis=1),
                          N_EXPERTS - 1)
    n_used = pends[-1:] // MOE_BLOCK

    gap_start = jnp.concatenate([pstarts + cnt, pends[-1:]])
    gap_len = jnp.concatenate([padded - cnt, P - pends[-1:]])
    gap_end = jnp.cumsum(gap_len)
    j = jnp.arange(P - A, dtype=jnp.int32)
    gap_of = jnp.sum((gap_end[None, :] <= j[:, None]).astype(jnp.int32), axis=1)
    sel = gap_of[:, None] == jnp.arange(N_EXPERTS + 1, dtype=jnp.int32)[None, :]
    pad_idx = jnp.sum(jnp.where(sel, (gap_start - (gap_end - gap_len))[None, :] + j[:, None], 0), axis=1)

    xs = _dispatch(hn, dest, pad_idx.reshape(1, P - A), P)
    ys = _experts(xs, blk_exp, n_used, layer, w_gate, w_up, w_down)
    return _combine(h1, route, _gather_pairs(ys, dest))


def _qkv_kernel(x_ref, gq_ref, gkv_ref, wq_ref, wk_ref, wvt_ref, qn_ref, kn_ref, seg_ref,
                q_ref, k_ref, vt_ref, xs_ref, *, rate, n, res_per_chunk):
    seg = seg_ref[...]
    width = seg.shape[0]

    def head_norm(t, gain):
        cols = []
        for j in range(t.shape[1] // width):
            tj = t[:, j * width:(j + 1) * width]
            ms = jnp.dot((tj * tj).astype(BF16), seg, preferred_element_type=F32)
            cols.append(tj * lax.rsqrt(ms + EPS))
        return jnp.concatenate(cols, axis=1) * gain

    def lane_chunk(j):
        return slice(j * LANES, (j + 1) * LANES)

    if rate > 1:
        for j in range(xs_ref.shape[0]):
            xs_ref[j] = x_ref[:, lane_chunk(j)]

    for c0 in range(0, rate, res_per_chunk):
        residues = range(c0, c0 + res_per_chunk)
        if rate == 1:
            x = x_ref[...]
        else:
            x = jnp.concatenate(
                [jnp.concatenate([xs_ref[j, pl.ds(c, n, stride=rate), :] for j in range(xs_ref.shape[0])], axis=1)
                 for c in residues], axis=0)
        y = _rms(x)
        xq = (y * gq_ref[...]).astype(BF16)
        xkv = (y * gkv_ref[...]).astype(BF16)
        q = head_norm(jnp.dot(xq, wq_ref[...], preferred_element_type=F32), qn_ref[...]).astype(q_ref.dtype)
        k = head_norm(jnp.dot(xkv, wk_ref[...], preferred_element_type=F32), kn_ref[...]).astype(k_ref.dtype)
        vt = lax.dot_general(wvt_ref[...], xkv, NT_DIMS, preferred_element_type=F32).astype(vt_ref.dtype)
        for j, c in enumerate(residues):
            q_ref[0, c] = q[j * n:(j + 1) * n]
            k_ref[0, c] = k[j * n:(j + 1) * n]
            vt_ref[0, c] = vt[:, j * n:(j + 1) * n]


def _qkv(h, B, S, rate, gq, gkv, wq, wk, wv, qn, kn):
    D = h.shape[1]
    L = S // rate
    hd = DIL_HEAD_DIM
    n = max(SUB_BLOCK, TOKEN_TILE // rate)
    tm = n * rate
    res_per_chunk = max(1, TOKEN_TILE // n)
    col_split = max(1, tm // (2 * TOKEN_TILE))
    dc = D // col_split
    width = 2 * LANES
    ii = jnp.arange(width)
    seg = jnp.where((ii[:, None] // hd) == (ii[None, :] // hd), 1.0 / hd, 0.0).astype(BF16)
    row = lambda g: jnp.tile(g, D // hd).reshape(1, D)
    qk_spec = pl.BlockSpec((1, rate, n, dc), lambda b, i, s: (b, 0, i, s))
    qk_shape = jax.ShapeDtypeStruct((B, rate, L, D), BF16)
    return pl.pallas_call(
        functools.partial(_qkv_kernel, rate=rate, n=n, res_per_chunk=res_per_chunk),
        grid=(B, S // tm, col_split),
        in_specs=[pl.BlockSpec((tm, D), lambda b, i, s: (b * (S // tm) + i, 0)),
                  _resident((1, D), lambda b, i, s: (0, 0)),
                  _resident((1, D), lambda b, i, s: (0, 0)),
                  pl.BlockSpec((D, dc), lambda b, i, s: (0, s)),
                  pl.BlockSpec((D, dc), lambda b, i, s: (0, s)),
                  pl.BlockSpec((dc, D), lambda b, i, s: (s, 0)),
                  pl.BlockSpec((1, dc), lambda b, i, s: (0, s)),
                  pl.BlockSpec((1, dc), lambda b, i, s: (0, s)),
                  _resident((width, width), lambda b, i, s: (0, 0))],
        out_specs=[qk_spec, qk_spec,
                   pl.BlockSpec((1, rate, dc, n), lambda b, i, s: (b, 0, s, i))],
        out_shape=[qk_shape, qk_shape, jax.ShapeDtypeStruct((B, rate, D, L), BF16)],
        scratch_shapes=[pltpu.VMEM((D // LANES, tm if rate > 1 else 8, LANES), F32)],
        compiler_params=_params("parallel", "parallel", "arbitrary"),
        name="qkv_rate%d" % rate,
    )(h, gq.reshape(1, D), gkv.reshape(1, D), wq, wk, wv.T, row(qn) * (hd ** -0.5 * LOG2E), row(kn), seg)


def _attn_kernel(*refs, rate, with_prev):
    if rate == 1:
        q_ref, kc_ref, kp_ref, vc_ref, vp_ref, edge_ref, bias_ref, o_ref, lse_ref, os_ref = refs
    elif with_prev:
        q_ref, kc_ref, kp_ref, vc_ref, vp_ref, edge_ref, o_ref, lse_ref, os_ref = refs
    else:
        q_ref, kc_ref, vc_ref, edge_ref, o_ref, lse_ref, os_ref = refs
    Bk = SUB_BLOCK
    hd = DIL_HEAD_DIM
    n_pairs = DIL_HEADS // 2
    lane = lax.broadcasted_iota(jnp.int32, (Bk, LANES), 1)

    def block(load_q, load_k, load_vt, load_bias, store):
        out_t, lse_t = [], []
        for p in range(n_pairs):
            qp = load_q(p)
            zero = jnp.zeros_like(qp)
            q2 = jnp.concatenate([jnp.where(lane < hd, qp, zero), jnp.where(lane >= hd, qp, zero)], axis=0)
            s = lax.dot_general(load_k(p), q2, NT_DIMS, preferred_element_type=F32) + load_bias(p)
            m = jnp.max(s, axis=0, keepdims=True)
            pr = jnp.exp2(s - m)
            l = jnp.sum(pr, axis=0, keepdims=True)
            pb = pr.astype(BF16)
            vt = load_vt(p)
            out_t.append(jnp.dot(vt[:hd], pb[:, :Bk], preferred_element_type=F32) / l[:, :Bk])
            out_t.append(jnp.dot(vt[hd:], pb[:, Bk:], preferred_element_type=F32) / l[:, Bk:])
            lse = (m + jnp.log2(l)) * LN2
            lse_t += [lse[:, :Bk], lse[:, Bk:]]
        lse_t.append(jnp.zeros((LANES - DIL_HEADS, Bk), F32))
        store(jnp.concatenate(out_t, axis=0).T, jnp.concatenate(lse_t, axis=0).T)

    def pair_cols(p):
        return slice(p * LANES, (p + 1) * LANES)

    edge_bias = lambda p: edge_ref[0, p]
    if rate == 1:
        for j in range(q_ref.shape[2] // Bk):
            rows = slice(j * Bk, (j + 1) * Bk)
            if j == 0:
                load_k = lambda p: jnp.concatenate([kp_ref[0, 0, :, pair_cols(p)],
                                                    kc_ref[0, 0, :Bk, pair_cols(p)]], axis=0)
                load_vt = lambda p: jnp.concatenate([vp_ref[0, 0, pair_cols(p), :],
                                                     vc_ref[0, 0, pair_cols(p), :Bk]], axis=1)
                load_bias = edge_bias
            else:
                krows = slice((j - 1) * Bk, (j + 1) * Bk)
                load_k = lambda p, krows=krows: kc_ref[0, 0, krows, pair_cols(p)]
                load_vt = lambda p, krows=krows: vc_ref[0, 0, pair_cols(p), krows]
                load_bias = lambda p: bias_ref[p]

            def store(o, lse, rows=rows):
                o_ref[rows, :] = o.astype(o_ref.dtype)
                lse_ref[rows, :] = lse

            block(lambda p, rows=rows: q_ref[0, 0, rows, pair_cols(p)], load_k, load_vt, load_bias, store)
    else:
        for c in range(q_ref.shape[1]):
            if with_prev:
                load_k = lambda p, c=c: jnp.concatenate([kp_ref[0, c, :, pair_cols(p)],
                                                         kc_ref[0, c, :, pair_cols(p)]], axis=0)
                load_vt = lambda p, c=c: jnp.concatenate([vp_ref[0, c, pair_cols(p), :],
                                                          vc_ref[0, c, pair_cols(p), :]], axis=1)
            else:
                load_k = lambda p, c=c: kc_ref[0, c, :, pair_cols(p)]
                load_vt = lambda p, c=c: vc_ref[0, c, pair_cols(p), :]

            def store(o, lse, c=c):
                for j in range(os_ref.shape[0]):
                    os_ref[j, pl.ds(c, Bk, stride=rate), :] = o[:, pair_cols(j)]
                lse_ref[pl.ds(c, Bk, stride=rate), :] = lse

            block(lambda p, c=c: q_ref[0, c, :, pair_cols(p)], load_k, load_vt, edge_bias, store)
        for j in range(os_ref.shape[0]):
            o_ref[:, pair_cols(j)] = os_ref[j].astype(o_ref.dtype)


def _t5_bucket(n):
    max_exact = NUM_BUCKETS // 2
    nf = jnp.maximum(n, max_exact).astype(F32)
    large = max_exact + (jnp.log(nf / max_exact) / math.log(MAX_DISTANCE / max_exact)
                         * (NUM_BUCKETS - max_exact)).astype(jnp.int32)
    large = jnp.minimum(large, NUM_BUCKETS - 1)
    return jnp.where(n < max_exact, n, large)


def _group_attention(q, k, vt, bias_table, rate, n_steps):
    B, _, L, D = q.shape
    S = L * rate
    Bk = SUB_BLOCK
    with_prev = L > Bk
    nk = 2 * Bk if with_prev else Bk
    n = max(Bk, TOKEN_TILE // rate)
    tm = n * rate
    nt = S // tm
    n_pairs = DIL_HEADS // 2
    ql = jnp.arange(Bk, dtype=jnp.int32)[:, None]
    kl = jnp.arange(2 * Bk, dtype=jnp.int32)[None, :]
    steps = ql + Bk - kl
    bucket = _t5_bucket(jnp.maximum(steps, 0) * rate)
    buckets = jnp.arange(NUM_BUCKETS, dtype=jnp.int32)
    bias = jnp.sum(jnp.where(bucket[None, :, :, None] == buckets[:, None, None, None],
                             bias_table.astype(F32)[:, None, None, :], 0.0), axis=0)
    band = ((steps >= 0) & (steps <= n_steps))[:, :, None]
    first = (kl < Bk)[:, :, None]

    def layout(t):
        t = t[:, 2 * Bk - nk:, :].transpose(1, 2, 0)
        return t.reshape(nk, n_pairs, 2 * Bk).transpose(1, 0, 2)

    bias_in = layout(jnp.where(band, bias * LOG2E, NEG_INF))
    bias_first = layout(jnp.where(band & ~first, bias * LOG2E, NEG_INF))
    edge = jnp.stack([bias_first, bias_in])

    cur_qk = pl.BlockSpec((1, rate, n, D), lambda b, i: (b, 0, i, 0))
    cur_vt = pl.BlockSpec((1, rate, D, n), lambda b, i: (b, 0, 0, i))
    per_n = n // Bk
    prev_qk = pl.BlockSpec((1, rate, Bk, D), lambda b, i: (b, 0, jnp.maximum(i * per_n - 1, 0), 0))
    prev_vt = pl.BlockSpec((1, rate, D, Bk), lambda b, i: (b, 0, 0, jnp.maximum(i * per_n - 1, 0)))
    edge_spec = pl.BlockSpec((1, n_pairs, nk, 2 * Bk), lambda b, i: (jnp.minimum(i, 1), 0, 0, 0))
    if with_prev:
        in_specs = [cur_qk, cur_qk, prev_qk, cur_vt, prev_vt, edge_spec]
        args = (q, k, k, vt, vt, edge)
    else:
        in_specs = [cur_qk, cur_qk, cur_vt, edge_spec]
        args = (q, k, vt, edge)
    if rate == 1:
        in_specs.append(_resident((n_pairs, nk, 2 * Bk), lambda b, i: (0, 0, 0)))
        args += (bias_in,)
    return pl.pallas_call(
        functools.partial(_attn_kernel, rate=rate, with_prev=with_prev),
        grid=(B, nt),
        in_specs=in_specs,
        out_specs=[pl.BlockSpec((tm, D), lambda b, i: (b * nt + i, 0)),
                   pl.BlockSpec((tm, LANES), lambda b, i: (b * nt + i, 0))],
        out_shape=[jax.ShapeDtypeStruct((B * S, D), BF16),
                   jax.ShapeDtypeStruct((B * S, LANES), F32)],
        scratch_shapes=[pltpu.VMEM((D // LANES, tm if rate > 1 else 8, LANES), F32)],
        compiler_params=_params("parallel", "parallel"),
        name="attn_rate%d" % rate,
    )(*args)


def _merge_kernel(o0_ref, o1_ref, o2_ref, l0_ref, l1_ref, l2_ref, ex_ref, out_ref):
    ex = ex_ref[...]

    def expand(w):
        hi = w.astype(BF16)
        r1 = w - hi.astype(F32)
        mid = r1.astype(BF16)
        lo = (r1 - mid.astype(F32)).astype(BF16)
        return (jnp.dot(hi, ex, preferred_element_type=F32) + jnp.dot(mid, ex, preferred_element_type=F32)
                + jnp.dot(lo, ex, preferred_element_type=F32))

    l0, l1, l2 = l0_ref[...], l1_ref[...], l2_ref[...]
    top = jnp.maximum(jnp.maximum(l0, l1), l2)
    w0, w1, w2 = jnp.exp(l0 - top), jnp.exp(l1 - top), jnp.exp(l2 - top)
    den = w0 + w1 + w2
    acc = expand(w0 / den) * o0_ref[...].astype(F32)
    acc = acc + expand(w1 / den) * o1_ref[...].astype(F32)
    acc = acc + expand(w2 / den) * o2_ref[...].astype(F32)
    out_ref[...] = acc.astype(out_ref.dtype)


def _merge(outs, lses):
    T, D = outs[0].shape
    tm = min(TOKEN_TILE, T)
    ex = jnp.where(jnp.arange(LANES)[:, None] == (jnp.arange(D)[None, :] // DIL_HEAD_DIM), 1.0, 0.0).astype(BF16)
    o_spec = pl.BlockSpec((tm, D), lambda i: (i, 0))
    l_spec = pl.BlockSpec((tm, LANES), lambda i: (i, 0))
    return pl.pallas_call(
        _merge_kernel,
        grid=(T // tm,),
        in_specs=[o_spec] * 3 + [l_spec] * 3 + [_resident((LANES, D), lambda i: (0, 0))],
        out_specs=o_spec,
        out_shape=jax.ShapeDtypeStruct((T, D), BF16),
        compiler_params=_params("parallel"),
        name="merge_groups",
    )(*outs, *lses, ex)


def kernel(x, ret_w_in, ret_w_out, kv_norm, w_kv, k_norm, dil_wq, q_norm, dil_wo, rel_bias,
           mixer_norm, ffn_norm, router_grp, router_grp_b, router_exp, router_exp_b,
           exp_gate, exp_up, exp_down):
    B, S, D = x.shape
    h = x.reshape(B * S, D)

    def moe_layer(layer, a, w_out, h):
        h1, *hn, route, route_t, counts = _post(
            a, w_out.astype(BF16), h, ffn_norm[layer], router_grp[layer], router_grp_b[layer],
            router_exp[layer], router_exp_b[layer])
        return _moe(h1, hn, route, route_t, counts, layer, exp_gate, exp_up, exp_down)

    proj = _norm_proj(h, mixer_norm[0], ret_w_in[0].astype(BF16))
    y = _retention(proj, B, S)
    h = moe_layer(0, y, ret_w_out[0], h)

    G = len(DIL_RATES)
    gd = DIL_HEADS * DIL_HEAD_DIM
    outs, lses = [], []
    for g in range(G):
        cq = slice(g * gd, (g + 1) * gd)
        cv = slice(G * gd + g * gd, G * gd + (g + 1) * gd)
        q, k, vt = _qkv(h, B, S, DIL_RATES[g], mixer_norm[1], kv_norm,
                        dil_wq[0][:, cq].astype(BF16), w_kv[:, cq].astype(BF16), w_kv[:, cv].astype(BF16),
                        q_norm[0][g], k_norm[g])
        o, lse = _group_attention(q, k, vt, rel_bias[:, g * DIL_HEADS:(g + 1) * DIL_HEADS],
                                  DIL_RATES[g], DIL_WINDOWS[g] // DIL_RATES[g])
        outs.append(o)
        lses.append(lse)
    att = _merge(outs, lses)
    h = moe_layer(1, att, dil_wo[0], h)
    return h.reshape(B, S, D)
```

```python
import functools
import math

import jax
import jax.numpy as jnp
from jax import lax
from jax.experimental import pallas as pl
from jax.experimental.pallas import tpu as pltpu
from jax.experimental.pallas import tpu_sc as plsc

F32 = jnp.float32
BF16 = jnp.bfloat16

EPS = 1e-6
NEG_INF = -1e30

RET_HEADS = 4
RET_CHUNK = 128
ROPE_BASE = 10000.0

DIL_WINDOWS = (128, 512, 2048)
DIL_RATES = (1, 4, 16)
DIL_HEADS = 16
DIL_HEAD_DIM = 64
SUB_BLOCK = 128
NUM_BUCKETS = 32
MAX_DISTANCE = 2048

MOE_GROUPS = 4
EXPERTS_PER_GROUP = 8
N_EXPERTS = MOE_GROUPS * EXPERTS_PER_GROUP
MOE_BLOCK = 256

LANES = 128
TOKEN_TILE = 512
VMEM_LIMIT = 56 * 1024 * 1024

NT_DIMS = (((1,), (1,)), ((), ()))
LOG2E = math.log2(math.e)
LN2 = math.log(2.0)


def _params(*sem):
    return pltpu.CompilerParams(dimension_semantics=sem, vmem_limit_bytes=VMEM_LIMIT)


def _resident(shape, index_map):
    return pl.BlockSpec(shape, index_map, pipeline_mode=pl.Buffered(1))


def _rms(x):
    return x * lax.rsqrt(jnp.mean(x * x, axis=-1, keepdims=True) + EPS)


def _norm_proj_kernel(x_ref, g_ref, w_ref, o_ref, *, col_tile):
    xn = (_rms(x_ref[...]) * g_ref[...]).astype(BF16)
    for j in range(w_ref.shape[1] // col_tile):
        cols = slice(j * col_tile, (j + 1) * col_tile)
        o_ref[:, cols] = jnp.dot(xn, w_ref[:, cols], preferred_element_type=F32).astype(o_ref.dtype)


def _norm_proj(x, gain, w):
    T, D = x.shape
    N = w.shape[1]
    tm = min(TOKEN_TILE, T)
    return pl.pallas_call(
        functools.partial(_norm_proj_kernel, col_tile=512),
        grid=(T // tm,),
        in_specs=[pl.BlockSpec((tm, D), lambda i: (i, 0)),
                  _resident((1, D), lambda i: (0, 0)),
                  _resident((D, N), lambda i: (0, 0))],
        out_specs=pl.BlockSpec((tm, N), lambda i: (i, 0)),
        out_shape=jax.ShapeDtypeStruct((T, N), BF16),
        compiler_params=_params("parallel"),
        name="norm_proj",
    )(x, gain.reshape(1, D), w)


def _retention_kernel(q_ref, k_ref, v_ref, gate_ref, cos_ref, sin_ref, din_ref, xi_ref, zeta_ref,
                      cd_ref, o_ref, state_ref, *, k_scale):
    C = RET_CHUNK
    heads = state_ref.shape[0]
    dk = q_ref.shape[2] // heads
    dv = v_ref.shape[2] // heads
    half = dk // 2
    ts = q_ref.shape[1]
    tile = pl.program_id(1)

    @pl.when(tile == 0)
    def _():
        state_ref[...] = jnp.zeros_like(state_ref)

    def rot(t, cos, sin):
        t1, t2 = t[:, :half], t[:, half:]
        return jnp.concatenate([t1 * cos - t2 * sin, t1 * sin + t2 * cos], axis=-1)

    for c in range(ts // C):
        rows = slice(c * C, (c + 1) * C)
        pos = pl.ds(pl.multiple_of(tile * ts + c * C, C), C)
        cos = cos_ref[pos, :]
        sin = sin_ref[pos, :]
        for hh in range(heads):
            qk_cols = slice(hh * dk, (hh + 1) * dk)
            v_cols = slice(hh * dv, (hh + 1) * dv)
            q = rot(q_ref[0, rows, qk_cols].astype(F32), cos, sin)
            k = rot(k_ref[0, rows, qk_cols].astype(F32), cos, sin) * k_scale
            v = v_ref[0, rows, v_cols]
            qb = q.astype(BF16)
            kb = k.astype(BF16)
            s = lax.dot_general(qb, kb, NT_DIMS, preferred_element_type=F32) * din_ref[hh]
            inner = jnp.dot(s.astype(BF16), v, preferred_element_type=F32)
            state = state_ref[hh]
            cross = jnp.dot(qb, state.astype(BF16), preferred_element_type=F32) * xi_ref[hh]
            kz_t = (k * zeta_ref[hh]).T.astype(BF16)
            state_ref[hh] = state * cd_ref[hh] + jnp.dot(kz_t, v, preferred_element_type=F32)
            o = _rms(inner + cross)
            g = gate_ref[0, rows, v_cols].astype(F32)
            o_ref[0, rows, v_cols] = (g * jax.nn.sigmoid(g) * o).astype(o_ref.dtype)


def _retention(proj, B, S):
    H, C = RET_HEADS, RET_CHUNK
    D = proj.shape[1] // 6
    dk, dv = D // H, 2 * D // H
    half = dk // 2
    ts = min(TOKEN_TILE, S)
    proj = proj.reshape(B, S, 6 * D)

    pos = jnp.arange(S, dtype=F32)
    inv = 1.0 / (ROPE_BASE ** jnp.linspace(0.0, 1.0, half, dtype=F32))
    ang = pos[:, None] * inv[None, :]
    cos, sin = jnp.cos(ang), jnp.sin(ang)
    log_g = jnp.log(1.0 - 2.0 ** (-5.0 - jnp.arange(H, dtype=F32)))
    idx = jnp.arange(C, dtype=F32)
    diff = idx[:, None] - idx[None, :]
    d_in = jnp.where(diff >= 0, jnp.exp(log_g[:, None, None] * jnp.maximum(diff, 0.0)), 0.0)
    xi = jnp.exp(log_g[:, None] * (idx + 1.0))[:, :, None]
    zeta = jnp.exp(log_g[:, None] * (C - 1.0 - idx))[:, :, None]
    chunk_decay = jnp.exp(log_g * C)[:, None, None]

    wq, wv = H * dk, H * dv
    per_head = lambda shape: _resident((H,) + shape, lambda b, i: (0, 0, 0))
    out = pl.pallas_call(
        functools.partial(_retention_kernel, k_scale=dk ** -0.5),
        grid=(B, S // ts),
        in_specs=[pl.BlockSpec((1, ts, wq), lambda b, i: (b, i, 0)),
                  pl.BlockSpec((1, ts, wq), lambda b, i: (b, i, 1)),
                  pl.BlockSpec((1, ts, wv), lambda b, i: (b, i, 1)),
                  pl.BlockSpec((1, ts, wv), lambda b, i: (b, i, 2)),
                  _resident((S, half), lambda b, i: (0, 0)),
                  _resident((S, half), lambda b, i: (0, 0)),
                  per_head((C, C)), per_head((C, 1)), per_head((C, 1)), per_head((1, 1))],
        out_specs=pl.BlockSpec((1, ts, wv), lambda b, i: (b, i, 0)),
        out_shape=jax.ShapeDtypeStruct((B, S, H * dv), BF16),
        scratch_shapes=[pltpu.VMEM((H, dk, dv), F32)],
        compiler_params=_params("parallel", "arbitrary"),
        name="retention",
    )(proj, proj, proj, proj, cos, sin, d_in, xi, zeta, chunk_decay)
    return out.reshape(B * S, H * dv)


ROUTE_E1, ROUTE_E2, ROUTE_G1, ROUTE_G2, ROUTE_R1, ROUTE_R2 = range(6)
ROUTE_ROWS = 8
ROUTER_EXP_LANE0 = MOE_GROUPS
HI16 = 0xFFFF0000


def _pack_bf16_pair(a, b):
    ua = lax.bitcast_convert_type(a.astype(BF16).astype(F32), jnp.uint32)
    ub = lax.bitcast_convert_type(b.astype(BF16).astype(F32), jnp.uint32)
    return ua | (ub >> 16)


def _unpack_bf16_pair(w):
    a = lax.bitcast_convert_type(w & jnp.uint32(HI16), F32)
    b = lax.bitcast_convert_type(w << 16, F32)
    return a, b


ROW_PARTS = 2


def _part_cols(D, part):
    w = D // 2 // ROW_PARTS
    return slice(part * w, (part + 1) * w), slice(D // 2 + part * w, D // 2 + (part + 1) * w)


def _pack_part(x, part):
    hi, lo = _part_cols(x.shape[1], part)
    return _pack_bf16_pair(x[:, hi], x[:, lo])


def _post_kernel(a_ref, w_ref, h_ref, g_ref, wr_ref, br_ref, tri_ref,
                 h1_ref, hn0_ref, hn1_ref, route_ref, route_t_ref, cnt_ref, carry_ref):
    hn_refs = (hn0_ref, hn1_ref)
    @pl.when(pl.program_id(0) == 0)
    def _():
        carry_ref[...] = jnp.zeros_like(carry_ref)

    half = h_ref.shape[0] // 2
    D = h_ref.shape[1]
    lane = lax.broadcasted_iota(jnp.int32, (half, LANES), 1).astype(F32)
    ninf = -jnp.inf

    def first_argmax(vals):
        top = jnp.max(vals, axis=1, keepdims=True)
        where = jnp.min(jnp.where(vals == top, lane, float(LANES)), axis=1, keepdims=True)
        return top, where

    def route_half(rows):
        h1 = h_ref[rows, :] + jnp.dot(a_ref[rows, :], w_ref[...], preferred_element_type=F32)
        h1_ref[rows, :] = h1
        hn = _rms(h1) * g_ref[...]
        for part, ref in enumerate(hn_refs):
            ref[rows, :] = _pack_part(hn, part)
        hi = hn.astype(BF16)
        lo = (hn - hi.astype(F32)).astype(BF16)
        logits = jnp.dot(jnp.concatenate([hi, hi, lo], axis=1), wr_ref[...],
                         preferred_element_type=F32) + br_ref[...]

        is_grp = lane < MOE_GROUPS
        lg = jnp.where(is_grp, logits, ninf)
        mg, grp = first_argmax(lg)
        p_grp = 1.0 / jnp.sum(jnp.where(is_grp, jnp.exp(lg - mg), 0.0), axis=1, keepdims=True)

        e_lane = lane - ROUTER_EXP_LANE0
        in_grp = (e_lane < N_EXPERTS) & (jnp.floor(e_lane * (1.0 / EXPERTS_PER_GROUP)) == grp)
        le = jnp.where(in_grp, logits, ninf)
        v1, i1 = first_argmax(le)
        le2 = jnp.where(lane == i1, ninf, le)
        v2, i2 = first_argmax(le2)
        e = jnp.exp(v2 - v1)
        hit1 = lane == (i1 - ROUTER_EXP_LANE0)
        hit2 = lane == (i2 - ROUTER_EXP_LANE0)
        onehot = jnp.where(hit1 | hit2, 1.0, 0.0)
        earlier = jnp.dot(tri_ref[...], onehot.astype(BF16), preferred_element_type=F32)
        return dict(e1=i1 - ROUTER_EXP_LANE0, e2=i2 - ROUTER_EXP_LANE0, g1=p_grp / (1.0 + e),
                    g2=p_grp * e / (1.0 + e), hit1=hit1, hit2=hit2, earlier=earlier,
                    count=jnp.sum(onehot, axis=0, keepdims=True))

    halves = [route_half(slice(0, half)), route_half(slice(half, 2 * half))]
    carry = carry_ref[...]
    for j, r in enumerate(halves):
        before = carry + r["earlier"]
        r1 = jnp.sum(jnp.where(r["hit1"], before, 0.0), axis=1, keepdims=True)
        r2 = jnp.sum(jnp.where(r["hit2"], before, 0.0), axis=1, keepdims=True)
        carry = carry + r["count"]
        route = jnp.zeros((half, LANES), F32)
        for slot, val in ((ROUTE_E1, r["e1"]), (ROUTE_E2, r["e2"]), (ROUTE_G1, r["g1"]),
                          (ROUTE_G2, r["g2"]), (ROUTE_R1, r1), (ROUTE_R2, r2)):
            route = jnp.where(lane == slot, val, route)
        route_ref[j * half:(j + 1) * half, :] = route
        route_t_ref[:, j * half:(j + 1) * half] = route.T[:ROUTE_ROWS]
    carry_ref[...] = carry
    cnt_ref[...] = carry


def _post(a, w, h, gain, w_grp, b_grp, w_exp, b_exp):
    T, D = h.shape
    K = a.shape[1]
    tm = min(TOKEN_TILE, T)
    n_r = MOE_GROUPS + N_EXPERTS
    wr = jnp.zeros((D, LANES), F32).at[:, :n_r].set(jnp.concatenate([w_grp, w_exp], axis=1))
    wr_hi = wr.astype(BF16)
    wr_lo = (wr - wr_hi.astype(F32)).astype(BF16)
    wr3 = jnp.concatenate([wr_hi, wr_lo, wr_hi], axis=0)
    br = jnp.zeros((1, LANES), F32).at[0, :n_r].set(jnp.concatenate([b_grp, b_exp]))
    tri = jnp.tril(jnp.ones((tm // 2, tm // 2), BF16), k=-1)
    return pl.pallas_call(
        _post_kernel,
        grid=(T // tm,),
        in_specs=[pl.BlockSpec((tm, K), lambda i: (i, 0)),
                  _resident((K, D), lambda i: (0, 0)),
                  pl.BlockSpec((tm, D), lambda i: (i, 0)),
                  _resident((1, D), lambda i: (0, 0)),
                  _resident((3 * D, LANES), lambda i: (0, 0)),
                  _resident((1, LANES), lambda i: (0, 0)),
                  _resident((tm // 2, tm // 2), lambda i: (0, 0))],
        out_specs=[pl.BlockSpec((tm, D), lambda i: (i, 0))]
                  + [pl.BlockSpec((tm, D // 2 // ROW_PARTS), lambda i: (i, 0))] * ROW_PARTS
                  + [pl.BlockSpec((tm, LANES), lambda i: (i, 0)),
                   pl.BlockSpec((ROUTE_ROWS, tm), lambda i: (0, i)),
                   pl.BlockSpec((1, LANES), lambda i: (0, 0))],
        out_shape=[jax.ShapeDtypeStruct((T, D), F32)]
                  + [jax.ShapeDtypeStruct((T, D // 2 // ROW_PARTS), jnp.uint32)] * ROW_PARTS
                  + [jax.ShapeDtypeStruct((T, LANES), F32),
                   jax.ShapeDtypeStruct((ROUTE_ROWS, T), F32),
                   jax.ShapeDtypeStruct((1, LANES), F32)],
        scratch_shapes=[pltpu.VMEM((1, LANES), F32)],
        compiler_params=_params("arbitrary"),
        name="post_mixer",
    )(a, w, h, gain.reshape(1, D), wr3, br, tri)


SC_WINDOW = 128


def _sc_mesh():
    return plsc.VectorSubcoreMesh(core_axis_name="core", subcore_axis_name="subcore")


def _sc_window_specs(W):
    rows = pl.BlockSpec((SC_WINDOW, W), lambda i: (i, 0))
    idx = pl.BlockSpec((1, SC_WINDOW), lambda i: (0, i))
    return rows, idx


def _sc_pipeline(body, n_rows, in_specs, out_specs):
    return pltpu.emit_pipeline(body, grid=(n_rows // SC_WINDOW,), in_specs=in_specs, out_specs=out_specs,
                               core_axis_name=("core", "subcore"), dimension_semantics=(pltpu.PARALLEL,))


def _dispatch(xs, dest, pad_idx, P):
    T, W = xs[0].shape
    n_pad = pad_idx.shape[1]
    n = len(xs)
    rows, idx = _sc_window_specs(W)
    out = jax.ShapeDtypeStruct((P, W), xs[0].dtype)

    @functools.partial(pl.kernel, out_type=(out,) * n, mesh=_sc_mesh(), scratch_types=[], name="dispatch")
    def scatter(*refs):
        x_hbm, (d0_hbm, d1_hbm, z_hbm, p_hbm), o_hbm = refs[:n], refs[n:n + 4], refs[n + 4:]
        for x, o in zip(x_hbm, o_hbm):
            def put(x_vmem, i_vmem, o=o):
                pltpu.sync_copy(x_vmem, o.at[i_vmem.at[0]])

            _sc_pipeline(put, T, [rows, idx], [])(x, d0_hbm)
            _sc_pipeline(put, T, [rows, idx], [])(x, d1_hbm)
            _sc_pipeline(put, n_pad, [rows, idx], [])(z_hbm, p_hbm)

    return scatter(*xs, dest[0:1], dest[1:2], jnp.zeros((n_pad, W), xs[0].dtype), pad_idx)


def _expert_kernel(blk_exp_ref, n_used_ref, *refs):
    x_refs = refs[:ROW_PARTS]
    wg_ref, wu_ref, wd_ref = refs[ROW_PARTS:ROW_PARTS + 3]
    o_refs = refs[ROW_PARTS + 3:2 * ROW_PARTS + 3]
    wg_s, wu_s, wd_s = refs[2 * ROW_PARTS + 3:]
    i = pl.program_id(0)
    used = i < n_used_ref[0]
    new_expert = (i == 0) | (blk_exp_ref[i] != blk_exp_ref[jnp.maximum(i - 1, 0)])
    D = wg_s.shape[0]

    @pl.when(used & new_expert)
    def _():
        wg_s[...] = wg_ref[0, 0].astype(BF16)
        wu_s[...] = wu_ref[0, 0].astype(BF16)
        wd_s[...] = wd_ref[0, 0].astype(BF16)

    @pl.when(used)
    def _():
        pieces = []
        for part, x_ref in enumerate(x_refs):
            for cols, val in zip(_part_cols(D, part), _unpack_bf16_pair(x_ref[...])):
                pieces.append((cols, val.astype(BF16)))

        def up(w_s):
            return sum(jnp.dot(val, w_s[cols, :], preferred_element_type=F32) for cols, val in pieces)

        g = up(wg_s)
        hid = (g * jax.nn.sigmoid(g) * up(wu_s)).astype(BF16)
        y = jnp.dot(hid, wd_s[...], preferred_element_type=F32)
        for part, o_ref in enumerate(o_refs):
            o_ref[...] = _pack_part(y, part)

    @pl.when(jnp.logical_not(used))
    def _():
        for o_ref in o_refs:
            o_ref[...] = jnp.zeros_like(o_ref)


def _experts(xs, blk_exp, n_used, layer, w_gate, w_up, w_down):
    P, W = xs[0].shape
    D = 2 * W * ROW_PARTS
    FF = w_gate.shape[3]
    nblk = P // MOE_BLOCK

    def x_map(i, be, nu):
        return (jnp.minimum(i, nu[0] - 1), 0)

    def w_map(i, be, nu):
        return (layer, be[jnp.minimum(i, nu[0] - 1)], 0, 0)

    return pl.pallas_call(
        _expert_kernel,
        grid_spec=pltpu.PrefetchScalarGridSpec(
            num_scalar_prefetch=2,
            grid=(nblk,),
            in_specs=[pl.BlockSpec((MOE_BLOCK, W), x_map)] * ROW_PARTS
                     + [pl.BlockSpec((1, 1, D, FF), w_map),
                        pl.BlockSpec((1, 1, D, FF), w_map),
                        pl.BlockSpec((1, 1, FF, D), w_map)],
            out_specs=[pl.BlockSpec((MOE_BLOCK, W), lambda i, be, nu: (i, 0))] * ROW_PARTS,
            scratch_shapes=[pltpu.VMEM((D, FF), BF16), pltpu.VMEM((D, FF), BF16),
                            pltpu.VMEM((FF, D), BF16)]),
        out_shape=[jax.ShapeDtypeStruct((P, W), jnp.uint32)] * ROW_PARTS,
        compiler_params=_params("arbitrary"),
        name="experts",
    )(blk_exp, n_used, *xs, w_gate, w_up, w_down)


def _gather_pairs(ys, dest):
    W = ys[0].shape[1]
    T = dest.shape[1]
    n = len(ys)
    rows, idx = _sc_window_specs(W)
    out = jax.ShapeDtypeStruct((T, W), ys[0].dtype)

    @functools.partial(pl.kernel, out_type=(out,) * (2 * n), mesh=_sc_mesh(), scratch_types=[],
                       name="gather_pairs")
    def gather(*refs):
        y_hbm, d_hbm, o_hbm = refs[:n], refs[n:n + 2], refs[n + 2:]
        for slot, d in enumerate(d_hbm):
            for y, o in zip(y_hbm, o_hbm[slot * n:(slot + 1) * n]):
                def get(i_vmem, o_vmem, y=y):
                    pltpu.sync_copy(y.at[i_vmem.at[0]], o_vmem)

                _sc_pipeline(get, T, [idx], [rows])(d, o)

    return gather(*ys, dest[0:1], dest[1:2])


def _combine_kernel(h_ref, route_ref, *refs):
    y_refs, o_ref = refs[:-1], refs[-1]
    D = h_ref.shape[1]
    route = route_ref[...]
    gates = (route[:, ROUTE_G1:ROUTE_G1 + 1], route[:, ROUTE_G2:ROUTE_G2 + 1])
    for part in range(ROW_PARTS):
        slots = [_unpack_bf16_pair(y_refs[slot * ROW_PARTS + part][...]) for slot in range(2)]
        for half, cols in enumerate(_part_cols(D, part)):
            o_ref[:, cols] = h_ref[:, cols] + (gates[0] * slots[0][half] + gates[1] * slots[1][half])


def _combine(h, route, pairs):
    T, D = h.shape
    W = pairs[0].shape[1]
    tm = min(TOKEN_TILE, T)
    return pl.pallas_call(
        _combine_kernel,
        grid=(T // tm,),
        in_specs=[pl.BlockSpec((tm, D), lambda i: (i, 0)), pl.BlockSpec((tm, LANES), lambda i: (i, 0))]
                 + [pl.BlockSpec((tm, W), lambda i: (i, 0))] * len(pairs),
        out_specs=pl.BlockSpec((tm, D), lambda i: (i, 0)),
        out_shape=jax.ShapeDtypeStruct((T, D), F32),
        compiler_params=_params("parallel"),
        name="combine",
    )(h, route, *pairs)


def _moe(h1, hn, route, route_t, counts, layer, w_gate, w_up, w_down):
    T, D = h1.shape
    A = 2 * T
    nblk = -(-A // MOE_BLOCK) + N_EXPERTS
    P = nblk * MOE_BLOCK
    eid = route_t[ROUTE_E1:ROUTE_E2 + 1].astype(jnp.int32)
    rank = route_t[ROUTE_R1:ROUTE_R2 + 1].astype(jnp.int32)
    cnt = counts[0, :N_EXPERTS].astype(jnp.int32)
    padded = (cnt + MOE_BLOCK - 1) // MOE_BLOCK * MOE_BLOCK
    pends = jnp.cumsum(padded)
    pstarts = pends - padded
    experts = jnp.arange(N_EXPERTS, dtype=jnp.int32)
    start_of = jnp.sum(jnp.where(eid[:, None, :] == experts[None, :, None], pstarts[None, :, None], 0), axis=1)
    dest = start_of + rank
    blk_start = jnp.arange(nblk, dtype=jnp.int32) * MOE_BLOCK
    blk_exp = jnp.minimum(jnp.sum((pends[None, :] <= blk_start[:, None]).astype(jnp.int32), axis=1),
                          N_EXPERTS - 1)
    n_used = pends[-1:] // MOE_BLOCK

    gap_start = jnp.concatenate([pstarts + cnt, pends[-1:]])
    gap_len = jnp.concatenate([padded - cnt, P - pends[-1:]])
    gap_end = jnp.cumsum(gap_len)
    j = jnp.arange(P - A, dtype=jnp.int32)
    gap_of = jnp.sum((gap_end[None, :] <= j[:, None]).astype(jnp.int32), axis=1)
    sel = gap_of[:, None] == jnp.arange(N_EXPERTS + 1, dtype=jnp.int32)[None, :]
    pad_idx = jnp.sum(jnp.where(sel, (gap_start - (gap_end - gap_len))[None, :] + j[:, None], 0), axis=1)

    xs = _dispatch(hn, dest, pad_idx.reshape(1, P - A), P)
    ys = _experts(xs, blk_exp, n_used, layer, w_gate, w_up, w_down)
    return _combine(h1, route, _gather_pairs(ys, dest))


def _qkv_kernel(x_ref, gq_ref, gkv_ref, wq_ref, wk_ref, wvt_ref, qn_ref, kn_ref, seg_ref,
                q_ref, k_ref, vt_ref, xs_ref, *, rate, n, n_chunks):
    seg = seg_ref[...]
    width = seg.shape[0]
    res_per_chunk = q_ref.shape[1]
    chunk = pl.program_id(2)

    def head_norm(t, gain):
        cols = []
        for j in range(t.shape[1] // width):
            tj = t[:, j * width:(j + 1) * width]
            ms = jnp.dot((tj * tj).astype(BF16), seg, preferred_element_type=F32)
            cols.append(tj * lax.rsqrt(ms + EPS))
        return jnp.concatenate(cols, axis=1) * gain

    def project(x):
        y = _rms(x)
        xq = (y * gq_ref[...]).astype(BF16)
        xkv = (y * gkv_ref[...]).astype(BF16)
        q = head_norm(jnp.dot(xq, wq_ref[...], preferred_element_type=F32), qn_ref[...]).astype(q_ref.dtype)
        k = head_norm(jnp.dot(xkv, wk_ref[...], preferred_element_type=F32), kn_ref[...]).astype(k_ref.dtype)
        vt = lax.dot_general(wvt_ref[...], xkv, NT_DIMS, preferred_element_type=F32).astype(vt_ref.dtype)
        for j in range(res_per_chunk):
            q_ref[0, j] = q[j * n:(j + 1) * n]
            k_ref[0, j] = k[j * n:(j + 1) * n]
            vt_ref[0, j] = vt[:, j * n:(j + 1) * n]

    if rate == 1:
        project(x_ref[...])
        return

    @pl.when(chunk == 0)
    def _():
        for j in range(xs_ref.shape[0]):
            xs_ref[j] = x_ref[:, j * LANES:(j + 1) * LANES]

    for ch in range(n_chunks):
        @pl.when(chunk == ch)
        def _(ch=ch):
            residues = range(ch * res_per_chunk, (ch + 1) * res_per_chunk)
            project(jnp.concatenate(
                [jnp.concatenate([xs_ref[j, pl.ds(c, n, stride=rate), :] for j in range(xs_ref.shape[0])], axis=1)
                 for c in residues], axis=0))


def _qkv(h, B, S, rate, gq, gkv, wq, wk, wv, qn, kn):
    D = h.shape[1]
    L = S // rate
    hd = DIL_HEAD_DIM
    n = max(SUB_BLOCK, TOKEN_TILE // rate)
    tm = n * rate
    res_per_chunk = max(1, TOKEN_TILE // n)
    n_chunks = rate // res_per_chunk
    width = 2 * LANES
    ii = jnp.arange(width)
    seg = jnp.where((ii[:, None] // hd) == (ii[None, :] // hd), 1.0 / hd, 0.0).astype(BF16)
    row = lambda g: jnp.tile(g, D // hd).reshape(1, D)
    const = lambda shape: _resident(shape, lambda b, i, c: (0,) * len(shape))
    qk_spec = pl.BlockSpec((1, res_per_chunk, n, D), lambda b, i, c: (b, c, i, 0))
    qk_shape = jax.ShapeDtypeStruct((B, rate, L, D), BF16)
    return pl.pallas_call(
        functools.partial(_qkv_kernel, rate=rate, n=n, n_chunks=n_chunks),
        grid=(B, S // tm, n_chunks),
        in_specs=[pl.BlockSpec((tm, D), lambda b, i, c: (b * (S // tm) + i, 0)),
                  const((1, D)), const((1, D)), const((D, D)), const((D, D)), const((D, D)),
                  const((1, D)), const((1, D)), const((width, width))],
        out_specs=[qk_spec, qk_spec,
                   pl.BlockSpec((1, res_per_chunk, D, n), lambda b, i, c: (b, c, 0, i))],
        out_shape=[qk_shape, qk_shape, jax.ShapeDtypeStruct((B, rate, D, L), BF16)],
        scratch_shapes=[pltpu.VMEM((D // LANES, tm if rate > 1 else 8, LANES), F32)],
        compiler_params=_params("parallel", "parallel", "arbitrary"),
        name="qkv_rate%d" % rate,
    )(h, gq.reshape(1, D), gkv.reshape(1, D), wq, wk, wv.T, row(qn) * (hd ** -0.5 * LOG2E), row(kn), seg)


def _attn_kernel(*refs, rate, with_prev):
    if rate == 1:
        q_ref, kc_ref, kp_ref, vc_ref, vp_ref, edge_ref, bias_ref, o_ref, lse_ref, os_ref = refs
    elif with_prev:
        q_ref, kc_ref, kp_ref, vc_ref, vp_ref, edge_ref, o_ref, lse_ref, os_ref = refs
    else:
        q_ref, kc_ref, vc_ref, edge_ref, o_ref, lse_ref, os_ref = refs
    Bk = SUB_BLOCK
    hd = DIL_HEAD_DIM
    n_pairs = DIL_HEADS // 2
    lane = lax.broadcasted_iota(jnp.int32, (Bk, LANES), 1)

    def block(load_q, load_k, load_vt, load_bias, store):
        out_t, lse_t = [], []
        for p in range(n_pairs):
            qp = load_q(p)
            zero = jnp.zeros_like(qp)
            q2 = jnp.concatenate([jnp.where(lane < hd, qp, zero), jnp.where(lane >= hd, qp, zero)], axis=0)
            s = lax.dot_general(load_k(p), q2, NT_DIMS, preferred_element_type=F32) + load_bias(p)
            m = jnp.max(s, axis=0, keepdims=True)
            pr = jnp.exp2(s - m)
            l = jnp.sum(pr, axis=0, keepdims=True)
            pb = pr.astype(BF16)
            vt = load_vt(p)
            out_t.append(jnp.dot(vt[:hd], pb[:, :Bk], preferred_element_type=F32) / l[:, :Bk])
            out_t.append(jnp.dot(vt[hd:], pb[:, Bk:], preferred_element_type=F32) / l[:, Bk:])
            lse = (m + jnp.log2(l)) * LN2
            lse_t += [lse[:, :Bk], lse[:, Bk:]]
        lse_t.append(jnp.zeros((LANES - DIL_HEADS, Bk), F32))
        store(jnp.concatenate(out_t, axis=0).T, jnp.concatenate(lse_t, axis=0).T)

    def pair_cols(p):
        return slice(p * LANES, (p + 1) * LANES)

    edge_bias = lambda p: edge_ref[0, p]
    if rate == 1:
        for j in range(q_ref.shape[2] // Bk):
            rows = slice(j * Bk, (j + 1) * Bk)
            if j == 0:
                load_k = lambda p: jnp.concatenate([kp_ref[0, 0, :, pair_cols(p)],
                                                    kc_ref[0, 0, :Bk, pair_cols(p)]], axis=0)
                load_vt = lambda p: jnp.concatenate([vp_ref[0, 0, pair_cols(p), :],
                                                     vc_ref[0, 0, pair_cols(p), :Bk]], axis=1)
                load_bias = edge_bias
            else:
                krows = slice((j - 1) * Bk, (j + 1) * Bk)
                load_k = lambda p, krows=krows: kc_ref[0, 0, krows, pair_cols(p)]
                load_vt = lambda p, krows=krows: vc_ref[0, 0, pair_cols(p), krows]
                load_bias = lambda p: bias_ref[p]

            def store(o, lse, rows=rows):
                o_ref[rows, :] = o.astype(o_ref.dtype)
                lse_ref[rows, :] = lse

            block(lambda p, rows=rows: q_ref[0, 0, rows, pair_cols(p)], load_k, load_vt, load_bias, store)
    else:
        for c in range(q_ref.shape[1]):
            if with_prev:
                load_k = lambda p, c=c: jnp.concatenate([kp_ref[0, c, :, pair_cols(p)],
                                                         kc_ref[0, c, :, pair_cols(p)]], axis=0)
                load_vt = lambda p, c=c: jnp.concatenate([vp_ref[0, c, pair_cols(p), :],
                                                          vc_ref[0, c, pair_cols(p), :]], axis=1)
            else:
                load_k = lambda p, c=c: kc_ref[0, c, :, pair_cols(p)]
                load_vt = lambda p, c=c: vc_ref[0, c, pair_cols(p), :]

            def store(o, lse, c=c):
                for j in range(os_ref.shape[0]):
                    os_ref[j, pl.ds(c, Bk, stride=rate), :] = o[:, pair_cols(j)]
                lse_ref[pl.ds(c, Bk, stride=rate), :] = lse

            block(lambda p, c=c: q_ref[0, c, :, pair_cols(p)], load_k, load_vt, edge_bias, store)
        for j in range(os_ref.shape[0]):
            o_ref[:, pair_cols(j)] = os_ref[j].astype(o_ref.dtype)


def _t5_bucket(n):
    max_exact = NUM_BUCKETS // 2
    nf = jnp.maximum(n, max_exact).astype(F32)
    large = max_exact + (jnp.log(nf / max_exact) / math.log(MAX_DISTANCE / max_exact)
                         * (NUM_BUCKETS - max_exact)).astype(jnp.int32)
    large = jnp.minimum(large, NUM_BUCKETS - 1)
    return jnp.where(n < max_exact, n, large)


def _group_attention(q, k, vt, bias_table, rate, n_steps):
    B, _, L, D = q.shape
    S = L * rate
    Bk = SUB_BLOCK
    with_prev = L > Bk
    nk = 2 * Bk if with_prev else Bk
    n = max(Bk, TOKEN_TILE // rate)
    tm = n * rate
    nt = S // tm
    n_pairs = DIL_HEADS // 2
    ql = jnp.arange(Bk, dtype=jnp.int32)[:, None]
    kl = jnp.arange(2 * Bk, dtype=jnp.int32)[None, :]
    steps = ql + Bk - kl
    bucket = _t5_bucket(jnp.maximum(steps, 0) * rate)
    buckets = jnp.arange(NUM_BUCKETS, dtype=jnp.int32)
    bias = jnp.sum(jnp.where(bucket[None, :, :, None] == buckets[:, None, None, None],
                             bias_table.astype(F32)[:, None, None, :], 0.0), axis=0)
    band = ((steps >= 0) & (steps <= n_steps))[:, :, None]
    first = (kl < Bk)[:, :, None]

    def layout(t):
        t = t[:, 2 * Bk - nk:, :].transpose(1, 2, 0)
        return t.reshape(nk, n_pairs, 2 * Bk).transpose(1, 0, 2)

    bias_in = layout(jnp.where(band, bias * LOG2E, NEG_INF))
    bias_first = layout(jnp.where(band & ~first, bias * LOG2E, NEG_INF))
    edge = jnp.stack([bias_first, bias_in])

    cur_qk = pl.BlockSpec((1, rate, n, D), lambda b, i: (b, 0, i, 0))
    cur_vt = pl.BlockSpec((1, rate, D, n), lambda b, i: (b, 0, 0, i))
    per_n = n // Bk
    prev_qk = pl.BlockSpec((1, rate, Bk, D), lambda b, i: (b, 0, jnp.maximum(i * per_n - 1, 0), 0))
    prev_vt = pl.BlockSpec((1, rate, D, Bk), lambda b, i: (b, 0, 0, jnp.maximum(i * per_n - 1, 0)))
    edge_spec = pl.BlockSpec((1, n_pairs, nk, 2 * Bk), lambda b, i: (jnp.minimum(i, 1), 0, 0, 0))
    if with_prev:
        in_specs = [cur_qk, cur_qk, prev_qk, cur_vt, prev_vt, edge_spec]
        args = (q, k, k, vt, vt, edge)
    else:
        in_specs = [cur_qk, cur_qk, cur_vt, edge_spec]
        args = (q, k, vt, edge)
    if rate == 1:
        in_specs.append(_resident((n_pairs, nk, 2 * Bk), lambda b, i: (0, 0, 0)))
        args += (bias_in,)
    return pl.pallas_call(
        functools.partial(_attn_kernel, rate=rate, with_prev=with_prev),
        grid=(B, nt),
        in_specs=in_specs,
        out_specs=[pl.BlockSpec((tm, D), lambda b, i: (b * nt + i, 0)),
                   pl.BlockSpec((tm, LANES), lambda b, i: (b * nt + i, 0))],
        out_shape=[jax.ShapeDtypeStruct((B * S, D), BF16),
                   jax.ShapeDtypeStruct((B * S, LANES), F32)],
        scratch_shapes=[pltpu.VMEM((D // LANES, tm if rate > 1 else 8, LANES), F32)],
        compiler_params=_params("parallel", "parallel"),
        name="attn_rate%d" % rate,
    )(*args)


def _merge_kernel(o0_ref, o1_ref, o2_ref, l0_ref, l1_ref, l2_ref, ex_ref, out_ref):
    ex = ex_ref[...]

    def expand(w):
        hi = w.astype(BF16)
        r1 = w - hi.astype(F32)
        mid = r1.astype(BF16)
        lo = (r1 - mid.astype(F32)).astype(BF16)
        return (jnp.dot(hi, ex, preferred_element_type=F32) + jnp.dot(mid, ex, preferred_element_type=F32)
                + jnp.dot(lo, ex, preferred_element_type=F32))

    l0, l1, l2 = l0_ref[...], l1_ref[...], l2_ref[...]
    top = jnp.maximum(jnp.maximum(l0, l1), l2)
    w0, w1, w2 = jnp.exp(l0 - top), jnp.exp(l1 - top), jnp.exp(l2 - top)
    den = w0 + w1 + w2
    acc = expand(w0 / den) * o0_ref[...].astype(F32)
    acc = acc + expand(w1 / den) * o1_ref[...].astype(F32)
    acc = acc + expand(w2 / den) * o2_ref[...].astype(F32)
    out_ref[...] = acc.astype(out_ref.dtype)


def _merge(outs, lses):
    T, D = outs[0].shape
    tm = min(TOKEN_TILE, T)
    ex = jnp.where(jnp.arange(LANES)[:, None] == (jnp.arange(D)[None, :] // DIL_HEAD_DIM), 1.0, 0.0).astype(BF16)
    o_spec = pl.BlockSpec((tm, D), lambda i: (i, 0))
    l_spec = pl.BlockSpec((tm, LANES), lambda i: (i, 0))
    return pl.pallas_call(
        _merge_kernel,
        grid=(T // tm,),
        in_specs=[o_spec] * 3 + [l_spec] * 3 + [_resident((LANES, D), lambda i: (0, 0))],
        out_specs=o_spec,
        out_shape=jax.ShapeDtypeStruct((T, D), BF16),
        compiler_params=_params("parallel"),
        name="merge_groups",
    )(*outs, *lses, ex)


def kernel(x, ret_w_in, ret_w_out, kv_norm, w_kv, k_norm, dil_wq, q_norm, dil_wo, rel_bias,
           mixer_norm, ffn_norm, router_grp, router_grp_b, router_exp, router_exp_b,
           exp_gate, exp_up, exp_down):
    B, S, D = x.shape
    h = x.reshape(B * S, D)

    def moe_layer(layer, a, w_out, h):
        h1, *hn, route, route_t, counts = _post(
            a, w_out.astype(BF16), h, ffn_norm[layer], router_grp[layer], router_grp_b[layer],
            router_exp[layer], router_exp_b[layer])
        return _moe(h1, hn, route, route_t, counts, layer, exp_gate, exp_up, exp_down)

    proj = _norm_proj(h, mixer_norm[0], ret_w_in[0].astype(BF16))
    y = _retention(proj, B, S)
    h = moe_layer(0, y, ret_w_out[0], h)

    G = len(DIL_RATES)
    gd = DIL_HEADS * DIL_HEAD_DIM
    outs, lses = [], []
    for g in range(G):
        cq = slice(g * gd, (g + 1) * gd)
        cv = slice(G * gd + g * gd, G * gd + (g + 1) * gd)
        q, k, vt = _qkv(h, B, S, DIL_RATES[g], mixer_norm[1], kv_norm,
                        dil_wq[0][:, cq].astype(BF16), w_kv[:, cq].astype(BF16), w_kv[:, cv].astype(BF16),
                        q_norm[0][g], k_norm[g])
        o, lse = _group_attention(q, k, vt, rel_bias[:, g * DIL_HEADS:(g + 1) * DIL_HEADS],
                                  DIL_RATES[g], DIL_WINDOWS[g] // DIL_RATES[g])
        outs.append(o)
        lses.append(lse)
    att = _merge(outs, lses)
    h = moe_layer(1, att, dil_wo[0], h)
    return h.reshape(B, S, D)
```

```python
import functools
import math

import jax
import jax.numpy as jnp
from jax import lax
from jax.experimental import pallas as pl
from jax.experimental.pallas import tpu as pltpu
from jax.experimental.pallas import tpu_sc as plsc

F32 = jnp.float32
BF16 = jnp.bfloat16

EPS = 1e-6
NEG_INF = -1e30

RET_HEADS = 4
RET_CHUNK = 128
ROPE_BASE = 10000.0

DIL_WINDOWS = (128, 512, 2048)
DIL_RATES = (1, 4, 16)
DIL_HEADS = 16
DIL_HEAD_DIM = 64
SUB_BLOCK = 128
NUM_BUCKETS = 32
MAX_DISTANCE = 2048

MOE_GROUPS = 4
EXPERTS_PER_GROUP = 8
N_EXPERTS = MOE_GROUPS * EXPERTS_PER_GROUP
MOE_BLOCK = 256

LANES = 128
TOKEN_TILE = 512
VMEM_LIMIT = 56 * 1024 * 1024

NT_DIMS = (((1,), (1,)), ((), ()))
LOG2E = math.log2(math.e)
LN2 = math.log(2.0)


def _params(*sem):
    return pltpu.CompilerParams(dimension_semantics=sem, vmem_limit_bytes=VMEM_LIMIT)


def _resident(shape, index_map):
    return pl.BlockSpec(shape, index_map, pipeline_mode=pl.Buffered(1))


def _rms(x):
    return x * lax.rsqrt(jnp.mean(x * x, axis=-1, keepdims=True) + EPS)


def _norm_proj_kernel(x_ref, g_ref, w_ref, o_ref, *, col_tile):
    xn = (_rms(x_ref[...]) * g_ref[...]).astype(BF16)
    for j in range(w_ref.shape[1] // col_tile):
        cols = slice(j * col_tile, (j + 1) * col_tile)
        o_ref[:, cols] = jnp.dot(xn, w_ref[:, cols], preferred_element_type=F32).astype(o_ref.dtype)


def _norm_proj(x, gain, w):
    T, D = x.shape
    N = w.shape[1]
    tm = min(TOKEN_TILE, T)
    return pl.pallas_call(
        functools.partial(_norm_proj_kernel, col_tile=512),
        grid=(T // tm,),
        in_specs=[pl.BlockSpec((tm, D), lambda i: (i, 0)),
                  _resident((1, D), lambda i: (0, 0)),
                  _resident((D, N), lambda i: (0, 0))],
        out_specs=pl.BlockSpec((tm, N), lambda i: (i, 0)),
        out_shape=jax.ShapeDtypeStruct((T, N), BF16),
        compiler_params=_params("parallel"),
        name="norm_proj",
    )(x, gain.reshape(1, D), w)


def _retention_kernel(q_ref, k_ref, v_ref, gate_ref, cos_ref, sin_ref, din_ref, xi_ref, zeta_ref,
                      cd_ref, o_ref, state_ref, *, k_scale):
    C = RET_CHUNK
    heads = state_ref.shape[0]
    dk = q_ref.shape[2] // heads
    dv = v_ref.shape[2] // heads
    half = dk // 2
    ts = q_ref.shape[1]
    tile = pl.program_id(1)

    @pl.when(tile == 0)
    def _():
        state_ref[...] = jnp.zeros_like(state_ref)

    def rot(t, cos, sin):
        t1, t2 = t[:, :half], t[:, half:]
        return jnp.concatenate([t1 * cos - t2 * sin, t1 * sin + t2 * cos], axis=-1)

    for c in range(ts // C):
        rows = slice(c * C, (c + 1) * C)
        pos = pl.ds(pl.multiple_of(tile * ts + c * C, C), C)
        cos = cos_ref[pos, :]
        sin = sin_ref[pos, :]
        for hh in range(heads):
            qk_cols = slice(hh * dk, (hh + 1) * dk)
            v_cols = slice(hh * dv, (hh + 1) * dv)
            q = rot(q_ref[0, rows, qk_cols].astype(F32), cos, sin)
            k = rot(k_ref[0, rows, qk_cols].astype(F32), cos, sin) * k_scale
            v = v_ref[0, rows, v_cols]
            qb = q.astype(BF16)
            kb = k.astype(BF16)
            s = lax.dot_general(qb, kb, NT_DIMS, preferred_element_type=F32) * din_ref[hh]
            inner = jnp.dot(s.astype(BF16), v, preferred_element_type=F32)
            state = state_ref[hh]
            cross = jnp.dot(qb, state.astype(BF16), preferred_element_type=F32) * xi_ref[hh]
            kz_t = (k * zeta_ref[hh]).T.astype(BF16)
            state_ref[hh] = state * cd_ref[hh] + jnp.dot(kz_t, v, preferred_element_type=F32)
            o = _rms(inner + cross)
            g = gate_ref[0, rows, v_cols].astype(F32)
            o_ref[0, rows, v_cols] = (g * jax.nn.sigmoid(g) * o).astype(o_ref.dtype)


def _retention(proj, B, S):
    H, C = RET_HEADS, RET_CHUNK
    D = proj.shape[1] // 6
    dk, dv = D // H, 2 * D // H
    half = dk // 2
    ts = min(TOKEN_TILE, S)
    proj = proj.reshape(B, S, 6 * D)

    pos = jnp.arange(S, dtype=F32)
    inv = 1.0 / (ROPE_BASE ** jnp.linspace(0.0, 1.0, half, dtype=F32))
    ang = pos[:, None] * inv[None, :]
    cos, sin = jnp.cos(ang), jnp.sin(ang)
    log_g = jnp.log(1.0 - 2.0 ** (-5.0 - jnp.arange(H, dtype=F32)))
    idx = jnp.arange(C, dtype=F32)
    diff = idx[:, None] - idx[None, :]
    d_in = jnp.where(diff >= 0, jnp.exp(log_g[:, None, None] * jnp.maximum(diff, 0.0)), 0.0)
    xi = jnp.exp(log_g[:, None] * (idx + 1.0))[:, :, None]
    zeta = jnp.exp(log_g[:, None] * (C - 1.0 - idx))[:, :, None]
    chunk_decay = jnp.exp(log_g * C)[:, None, None]

    wq, wv = H * dk, H * dv
    per_head = lambda shape: _resident((H,) + shape, lambda b, i: (0, 0, 0))
    out = pl.pallas_call(
        functools.partial(_retention_kernel, k_scale=dk ** -0.5),
        grid=(B, S // ts),
        in_specs=[pl.BlockSpec((1, ts, wq), lambda b, i: (b, i, 0)),
                  pl.BlockSpec((1, ts, wq), lambda b, i: (b, i, 1)),
                  pl.BlockSpec((1, ts, wv), lambda b, i: (b, i, 1)),
                  pl.BlockSpec((1, ts, wv), lambda b, i: (b, i, 2)),
                  _resident((S, half), lambda b, i: (0, 0)),
                  _resident((S, half), lambda b, i: (0, 0)),
                  per_head((C, C)), per_head((C, 1)), per_head((C, 1)), per_head((1, 1))],
        out_specs=pl.BlockSpec((1, ts, wv), lambda b, i: (b, i, 0)),
        out_shape=jax.ShapeDtypeStruct((B, S, H * dv), BF16),
        scratch_shapes=[pltpu.VMEM((H, dk, dv), F32)],
        compiler_params=_params("parallel", "arbitrary"),
        name="retention",
    )(proj, proj, proj, proj, cos, sin, d_in, xi, zeta, chunk_decay)
    return out.reshape(B * S, H * dv)


ROUTE_E1, ROUTE_E2, ROUTE_G1, ROUTE_G2, ROUTE_R1, ROUTE_R2 = range(6)
ROUTE_ROWS = 8
ROUTER_EXP_LANE0 = MOE_GROUPS
HI16 = 0xFFFF0000


def _pack_bf16_pair(a, b):
    ua = lax.bitcast_convert_type(a.astype(BF16).astype(F32), jnp.uint32)
    ub = lax.bitcast_convert_type(b.astype(BF16).astype(F32), jnp.uint32)
    return ua | (ub >> 16)


def _unpack_bf16_pair(w):
    a = lax.bitcast_convert_type(w & jnp.uint32(HI16), F32)
    b = lax.bitcast_convert_type(w << 16, F32)
    return a, b


ROW_PARTS = 2


def _part_cols(D, part):
    w = D // 2 // ROW_PARTS
    return slice(part * w, (part + 1) * w), slice(D // 2 + part * w, D // 2 + (part + 1) * w)


def _pack_part(x, part):
    hi, lo = _part_cols(x.shape[1], part)
    return _pack_bf16_pair(x[:, hi], x[:, lo])


def _post_kernel(a_ref, w_ref, h_ref, g_ref, wr_ref, br_ref, tri_ref,
                 h1_ref, hn0_ref, hn1_ref, route_ref, route_t_ref, cnt_ref, carry_ref):
    hn_refs = (hn0_ref, hn1_ref)
    @pl.when(pl.program_id(0) == 0)
    def _():
        carry_ref[...] = jnp.zeros_like(carry_ref)

    half = h_ref.shape[0] // 2
    D = h_ref.shape[1]
    lane = lax.broadcasted_iota(jnp.int32, (half, LANES), 1).astype(F32)
    ninf = -jnp.inf

    def first_argmax(vals):
        top = jnp.max(vals, axis=1, keepdims=True)
        where = jnp.min(jnp.where(vals == top, lane, float(LANES)), axis=1, keepdims=True)
        return top, where

    def route_half(rows):
        h1 = h_ref[rows, :] + jnp.dot(a_ref[rows, :], w_ref[...], preferred_element_type=F32)
        h1_ref[rows, :] = h1
        hn = _rms(h1) * g_ref[...]
        for part, ref in enumerate(hn_refs):
            ref[rows, :] = _pack_part(hn, part)
        hi = hn.astype(BF16)
        lo = (hn - hi.astype(F32)).astype(BF16)
        logits = jnp.dot(jnp.concatenate([hi, hi, lo], axis=1), wr_ref[...],
                         preferred_element_type=F32) + br_ref[...]

        is_grp = lane < MOE_GROUPS
        lg = jnp.where(is_grp, logits, ninf)
        mg, grp = first_argmax(lg)
        p_grp = 1.0 / jnp.sum(jnp.where(is_grp, jnp.exp(lg - mg), 0.0), axis=1, keepdims=True)

        e_lane = lane - ROUTER_EXP_LANE0
        in_grp = (e_lane < N_EXPERTS) & (jnp.floor(e_lane * (1.0 / EXPERTS_PER_GROUP)) == grp)
        le = jnp.where(in_grp, logits, ninf)
        v1, i1 = first_argmax(le)
        le2 = jnp.where(lane == i1, ninf, le)
        v2, i2 = first_argmax(le2)
        e = jnp.exp(v2 - v1)
        hit1 = lane == (i1 - ROUTER_EXP_LANE0)
        hit2 = lane == (i2 - ROUTER_EXP_LANE0)
        onehot = jnp.where(hit1 | hit2, 1.0, 0.0)
        earlier = jnp.dot(tri_ref[...], onehot.astype(BF16), preferred_element_type=F32)
        return dict(e1=i1 - ROUTER_EXP_LANE0, e2=i2 - ROUTER_EXP_LANE0, g1=p_grp / (1.0 + e),
                    g2=p_grp * e / (1.0 + e), hit1=hit1, hit2=hit2, earlier=earlier,
                    count=jnp.sum(onehot, axis=0, keepdims=True))

    halves = [route_half(slice(0, half)), route_half(slice(half, 2 * half))]
    carry = carry_ref[...]
    for j, r in enumerate(halves):
        before = carry + r["earlier"]
        r1 = jnp.sum(jnp.where(r["hit1"], before, 0.0), axis=1, keepdims=True)
        r2 = jnp.sum(jnp.where(r["hit2"], before, 0.0), axis=1, keepdims=True)
        carry = carry + r["count"]
        route = jnp.zeros((half, LANES), F32)
        for slot, val in ((ROUTE_E1, r["e1"]), (ROUTE_E2, r["e2"]), (ROUTE_G1, r["g1"]),
                          (ROUTE_G2, r["g2"]), (ROUTE_R1, r1), (ROUTE_R2, r2)):
            route = jnp.where(lane == slot, val, route)
        route_ref[j * half:(j + 1) * half, :] = route
        route_t_ref[:, j * half:(j + 1) * half] = route.T[:ROUTE_ROWS]
    carry_ref[...] = carry
    cnt_ref[...] = carry


def _post(a, w, h, gain, w_grp, b_grp, w_exp, b_exp):
    T, D = h.shape
    K = a.shape[1]
    tm = min(TOKEN_TILE, T)
    n_r = MOE_GROUPS + N_EXPERTS
    wr = jnp.zeros((D, LANES), F32).at[:, :n_r].set(jnp.concatenate([w_grp, w_exp], axis=1))
    wr_hi = wr.astype(BF16)
    wr_lo = (wr - wr_hi.astype(F32)).astype(BF16)
    wr3 = jnp.concatenate([wr_hi, wr_lo, wr_hi], axis=0)
    br = jnp.zeros((1, LANES), F32).at[0, :n_r].set(jnp.concatenate([b_grp, b_exp]))
    tri = jnp.tril(jnp.ones((tm // 2, tm // 2), BF16), k=-1)
    return pl.pallas_call(
        _post_kernel,
        grid=(T // tm,),
        in_specs=[pl.BlockSpec((tm, K), lambda i: (i, 0)),
                  _resident((K, D), lambda i: (0, 0)),
                  pl.BlockSpec((tm, D), lambda i: (i, 0)),
                  _resident((1, D), lambda i: (0, 0)),
                  _resident((3 * D, LANES), lambda i: (0, 0)),
                  _resident((1, LANES), lambda i: (0, 0)),
                  _resident((tm // 2, tm // 2), lambda i: (0, 0))],
        out_specs=[pl.BlockSpec((tm, D), lambda i: (i, 0))]
                  + [pl.BlockSpec((tm, D // 2 // ROW_PARTS), lambda i: (i, 0))] * ROW_PARTS
                  + [pl.BlockSpec((tm, LANES), lambda i: (i, 0)),
                   pl.BlockSpec((ROUTE_ROWS, tm), lambda i: (0, i)),
                   pl.BlockSpec((1, LANES), lambda i: (0, 0))],
        out_shape=[jax.ShapeDtypeStruct((T, D), F32)]
                  + [jax.ShapeDtypeStruct((T, D // 2 // ROW_PARTS), jnp.uint32)] * ROW_PARTS
                  + [jax.ShapeDtypeStruct((T, LANES), F32),
                   jax.ShapeDtypeStruct((ROUTE_ROWS, T), F32),
                   jax.ShapeDtypeStruct((1, LANES), F32)],
        scratch_shapes=[pltpu.VMEM((1, LANES), F32)],
        compiler_params=_params("arbitrary"),
        name="post_mixer",
    )(a, w, h, gain.reshape(1, D), wr3, br, tri)


SC_WINDOW = 128


def _sc_mesh():
    return plsc.VectorSubcoreMesh(core_axis_name="core", subcore_axis_name="subcore")


def _sc_window_specs(W):
    rows = pl.BlockSpec((SC_WINDOW, W), lambda i: (i, 0))
    idx = pl.BlockSpec((1, SC_WINDOW), lambda i: (0, i))
    return rows, idx


def _sc_pipeline(body, n_rows, in_specs, out_specs):
    return pltpu.emit_pipeline(body, grid=(n_rows // SC_WINDOW,), in_specs=in_specs, out_specs=out_specs,
                               core_axis_name=("core", "subcore"), dimension_semantics=(pltpu.PARALLEL,))


def _dispatch(xs, dest, pad_idx, P):
    T, W = xs[0].shape
    n_pad = pad_idx.shape[1]
    n = len(xs)
    rows, idx = _sc_window_specs(W)
    out = jax.ShapeDtypeStruct((P, W), xs[0].dtype)

    @functools.partial(pl.kernel, out_type=(out,) * n, mesh=_sc_mesh(), scratch_types=[], name="dispatch")
    def scatter(*refs):
        x_hbm, (d0_hbm, d1_hbm, z_hbm, p_hbm), o_hbm = refs[:n], refs[n:n + 4], refs[n + 4:]
        for x, o in zip(x_hbm, o_hbm):
            def put(x_vmem, i_vmem, o=o):
                pltpu.sync_copy(x_vmem, o.at[i_vmem.at[0]])

            _sc_pipeline(put, T, [rows, idx], [])(x, d0_hbm)
            _sc_pipeline(put, T, [rows, idx], [])(x, d1_hbm)
            _sc_pipeline(put, n_pad, [rows, idx], [])(z_hbm, p_hbm)

    return scatter(*xs, dest[0:1], dest[1:2], jnp.zeros((n_pad, W), xs[0].dtype), pad_idx)


def _expert_kernel(blk_exp_ref, n_used_ref, *refs):
    x_refs = refs[:ROW_PARTS]
    wg_ref, wu_ref, wd_ref = refs[ROW_PARTS:ROW_PARTS + 3]
    o_refs = refs[ROW_PARTS + 3:2 * ROW_PARTS + 3]
    wg_s, wu_s, wd_s = refs[2 * ROW_PARTS + 3:]
    i = pl.program_id(0)
    used = i < n_used_ref[0]
    new_expert = (i == 0) | (blk_exp_ref[i] != blk_exp_ref[jnp.maximum(i - 1, 0)])
    D = wg_s.shape[0]

    @pl.when(used & new_expert)
    def _():
        wg_s[...] = wg_ref[0, 0].astype(BF16)
        wu_s[...] = wu_ref[0, 0].astype(BF16)
        wd_s[...] = wd_ref[0, 0].astype(BF16)

    @pl.when(used)
    def _():
        pieces = []
        for part, x_ref in enumerate(x_refs):
            for cols, val in zip(_part_cols(D, part), _unpack_bf16_pair(x_ref[...])):
                pieces.append((cols, val.astype(BF16)))

        def up(w_s):
            return sum(jnp.dot(val, w_s[cols, :], preferred_element_type=F32) for cols, val in pieces)

        g = up(wg_s)
        hid = (g * jax.nn.sigmoid(g) * up(wu_s)).astype(BF16)
        y = jnp.dot(hid, wd_s[...], preferred_element_type=F32)
        for part, o_ref in enumerate(o_refs):
            o_ref[...] = _pack_part(y, part)

    @pl.when(jnp.logical_not(used))
    def _():
        for o_ref in o_refs:
            o_ref[...] = jnp.zeros_like(o_ref)


def _experts(xs, blk_exp, n_used, layer, w_gate, w_up, w_down):
    P, W = xs[0].shape
    D = 2 * W * ROW_PARTS
    FF = w_gate.shape[3]
    nblk = P // MOE_BLOCK

    def x_map(i, be, nu):
        return (jnp.minimum(i, nu[0] - 1), 0)

    def w_map(i, be, nu):
        return (layer, be[jnp.minimum(i, nu[0] - 1)], 0, 0)

    return pl.pallas_call(
        _expert_kernel,
        grid_spec=pltpu.PrefetchScalarGridSpec(
            num_scalar_prefetch=2,
            grid=(nblk,),
            in_specs=[pl.BlockSpec((MOE_BLOCK, W), x_map)] * ROW_PARTS
                     + [pl.BlockSpec((1, 1, D, FF), w_map),
                        pl.BlockSpec((1, 1, D, FF), w_map),
                        pl.BlockSpec((1, 1, FF, D), w_map)],
            out_specs=[pl.BlockSpec((MOE_BLOCK, W), lambda i, be, nu: (i, 0))] * ROW_PARTS,
            scratch_shapes=[pltpu.VMEM((D, FF), BF16), pltpu.VMEM((D, FF), BF16),
                            pltpu.VMEM((FF, D), BF16)]),
        out_shape=[jax.ShapeDtypeStruct((P, W), jnp.uint32)] * ROW_PARTS,
        compiler_params=_params("arbitrary"),
        name="experts",
    )(blk_exp, n_used, *xs, w_gate, w_up, w_down)


def _gather_pairs(ys, dest):
    W = ys[0].shape[1]
    T = dest.shape[1]
    n = len(ys)
    rows, idx = _sc_window_specs(W)
    out = jax.ShapeDtypeStruct((T, W), ys[0].dtype)

    @functools.partial(pl.kernel, out_type=(out,) * (2 * n), mesh=_sc_mesh(), scratch_types=[],
                       name="gather_pairs")
    def gather(*refs):
        y_hbm, d_hbm, o_hbm = refs[:n], refs[n:n + 2], refs[n + 2:]
        for slot, d in enumerate(d_hbm):
            for y, o in zip(y_hbm, o_hbm[slot * n:(slot + 1) * n]):
                def get(i_vmem, o_vmem, y=y):
                    pltpu.sync_copy(y.at[i_vmem.at[0]], o_vmem)

                _sc_pipeline(get, T, [idx], [rows])(d, o)

    return gather(*ys, dest[0:1], dest[1:2])


def _combine_kernel(h_ref, route_ref, *refs):
    y_refs, o_ref = refs[:-1], refs[-1]
    D = h_ref.shape[1]
    route = route_ref[...]
    gates = (route[:, ROUTE_G1:ROUTE_G1 + 1], route[:, ROUTE_G2:ROUTE_G2 + 1])
    for part in range(ROW_PARTS):
        slots = [_unpack_bf16_pair(y_refs[slot * ROW_PARTS + part][...]) for slot in range(2)]
        for half, cols in enumerate(_part_cols(D, part)):
            o_ref[:, cols] = h_ref[:, cols] + (gates[0] * slots[0][half] + gates[1] * slots[1][half])


def _combine(h, route, pairs):
    T, D = h.shape
    W = pairs[0].shape[1]
    tm = min(TOKEN_TILE, T)
    return pl.pallas_call(
        _combine_kernel,
        grid=(T // tm,),
        in_specs=[pl.BlockSpec((tm, D), lambda i: (i, 0)), pl.BlockSpec((tm, LANES), lambda i: (i, 0))]
                 + [pl.BlockSpec((tm, W), lambda i: (i, 0))] * len(pairs),
        out_specs=pl.BlockSpec((tm, D), lambda i: (i, 0)),
        out_shape=jax.ShapeDtypeStruct((T, D), F32),
        compiler_params=_params("parallel"),
        name="combine",
    )(h, route, *pairs)


def _moe(h1, hn, route, route_t, counts, layer, w_gate, w_up, w_down):
    T, D = h1.shape
    A = 2 * T
    nblk = -(-A // MOE_BLOCK) + N_EXPERTS
    P = nblk * MOE_BLOCK
    eid = route_t[ROUTE_E1:ROUTE_E2 + 1].astype(jnp.int32)
    rank = route_t[ROUTE_R1:ROUTE_R2 + 1].astype(jnp.int32)
    cnt = counts[0, :N_EXPERTS].astype(jnp.int32)
    padded = (cnt + MOE_BLOCK - 1) // MOE_BLOCK * MOE_BLOCK
    pends = jnp.cumsum(padded)
    pstarts = pends - padded
    experts = jnp.arange(N_EXPERTS, dtype=jnp.int32)
    start_of = jnp.sum(jnp.where(eid[:, None, :] == experts[None, :, None], pstarts[None, :, None], 0), axis=1)
    dest = start_of + rank
    blk_start = jnp.arange(nblk, dtype=jnp.int32) * MOE_BLOCK
    blk_exp = jnp.minimum(jnp.sum((pends[None, :] <= blk_start[:, None]).astype(jnp.int32), axis=1),
                          N_EXPERTS - 1)
    n_used = pends[-1:] // MOE_BLOCK

    gap_start = jnp.concatenate([pstarts + cnt, pends[-1:]])
    gap_len = jnp.concatenate([padded - cnt, P - pends[-1:]])
    gap_end = jnp.cumsum(gap_len)
    j = jnp.arange(P - A, dtype=jnp.int32)
    gap_of = jnp.sum((gap_end[None, :] <= j[:, None]).astype(jnp.int32), axis=1)
    sel = gap_of[:, None] == jnp.arange(N_EXPERTS + 1, dtype=jnp.int32)[None, :]
    pad_idx = jnp.sum(jnp.where(sel, (gap_start - (gap_end - gap_len))[None, :] + j[:, None], 0), axis=1)

    xs = _dispatch(hn, dest, pad_idx.reshape(1, P - A), P)
    ys = _experts(xs, blk_exp, n_used, layer, w_gate, w_up, w_down)
    return _combine(h1, route, _gather_pairs(ys, dest))


def _qkv_kernel(x_ref, gq_ref, gkv_ref, wq_ref, wk_ref, wvt_ref, qn_ref, kn_ref, seg_ref,
                q_ref, k_ref, vt_ref, xs_ref, *, rate, n, n_chunks):
    seg = seg_ref[...]
    width = seg.shape[0]
    res_per_chunk = q_ref.shape[1]
    chunk = pl.program_id(2)

    def head_norm(t, gain):
        cols = []
        for j in range(t.shape[1] // width):
            tj = t[:, j * width:(j + 1) * width]
            ms = jnp.dot((tj * tj).astype(BF16), seg, preferred_element_type=F32)
            cols.append(tj * lax.rsqrt(ms + EPS))
        return jnp.concatenate(cols, axis=1) * gain

    def project(x):
        y = _rms(x)
        xq = (y * gq_ref[...]).astype(BF16)
        xkv = (y * gkv_ref[...]).astype(BF16)
        q = head_norm(jnp.dot(xq, wq_ref[...], preferred_element_type=F32), qn_ref[...]).astype(q_ref.dtype)
        k = head_norm(jnp.dot(xkv, wk_ref[...], preferred_element_type=F32), kn_ref[...]).astype(k_ref.dtype)
        vt = lax.dot_general(wvt_ref[...], xkv, NT_DIMS, preferred_element_type=F32).astype(vt_ref.dtype)
        for j in range(res_per_chunk):
            q_ref[0, j] = q[j * n:(j + 1) * n]
            k_ref[0, j] = k[j * n:(j + 1) * n]
            vt_ref[0, j] = vt[:, j * n:(j + 1) * n]

    if rate == 1:
        project(x_ref[...])
        return

    @pl.when(chunk == 0)
    def _():
        for j in range(xs_ref.shape[0]):
            xs_ref[j] = x_ref[:, j * LANES:(j + 1) * LANES]

    for ch in range(n_chunks):
        @pl.when(chunk == ch)
        def _(ch=ch):
            residues = range(ch * res_per_chunk, (ch + 1) * res_per_chunk)
            project(jnp.concatenate(
                [jnp.concatenate([xs_ref[j, pl.ds(c, n, stride=rate), :] for j in range(xs_ref.shape[0])], axis=1)
                 for c in residues], axis=0))


def _qkv(h, B, S, rate, gq, gkv, wq, wk, wv, qn, kn):
    D = h.shape[1]
    L = S // rate
    hd = DIL_HEAD_DIM
    n = max(SUB_BLOCK, TOKEN_TILE // rate)
    tm = n * rate
    res_per_chunk = max(1, TOKEN_TILE // n)
    n_chunks = rate // res_per_chunk
    width = 2 * LANES
    ii = jnp.arange(width)
    seg = jnp.where((ii[:, None] // hd) == (ii[None, :] // hd), 1.0 / hd, 0.0).astype(BF16)
    row = lambda g: jnp.tile(g, D // hd).reshape(1, D)
    const = lambda shape: _resident(shape, lambda b, i, c: (0,) * len(shape))
    qk_spec = pl.BlockSpec((1, res_per_chunk, n, D), lambda b, i, c: (b, c, i, 0))
    qk_shape = jax.ShapeDtypeStruct((B, rate, L, D), BF16)
    return pl.pallas_call(
        functools.partial(_qkv_kernel, rate=rate, n=n, n_chunks=n_chunks),
        grid=(B, S // tm, n_chunks),
        in_specs=[pl.BlockSpec((tm, D), lambda b, i, c: (b * (S // tm) + i, 0)),
                  const((1, D)), const((1, D)), const((D, D)), const((D, D)), const((D, D)),
                  const((1, D)), const((1, D)), const((width, width))],
        out_specs=[qk_spec, qk_spec,
                   pl.BlockSpec((1, res_per_chunk, D, n), lambda b, i, c: (b, c, 0, i))],
        out_shape=[qk_shape, qk_shape, jax.ShapeDtypeStruct((B, rate, D, L), BF16)],
        scratch_shapes=[pltpu.VMEM((D // LANES, tm if rate > 1 else 8, LANES), F32)],
        compiler_params=_params("parallel", "parallel", "arbitrary"),
        name="qkv_rate%d" % rate,
    )(h, gq.reshape(1, D), gkv.reshape(1, D), wq, wk, wv.T, row(qn) * (hd ** -0.5 * LOG2E), row(kn), seg)


def _attn_kernel(*refs, rate, with_prev):
    if rate == 1:
        q_ref, kc_ref, kp_ref, vc_ref, vp_ref, edge_ref, bias_ref, o_ref, lse_ref, os_ref = refs
    elif with_prev:
        q_ref, kc_ref, kp_ref, vc_ref, vp_ref, edge_ref, o_ref, lse_ref, os_ref = refs
    else:
        q_ref, kc_ref, vc_ref, edge_ref, o_ref, lse_ref, os_ref = refs
    Bk = SUB_BLOCK
    hd = DIL_HEAD_DIM
    n_pairs = DIL_HEADS // 2
    lane = lax.broadcasted_iota(jnp.int32, (Bk, LANES), 1)

    def block(load_q, load_k, load_vt, load_bias, store):
        out_t, lse_t = [], []
        for p in range(n_pairs):
            qp = load_q(p)
            zero = jnp.zeros_like(qp)
            q2 = jnp.concatenate([jnp.where(lane < hd, qp, zero), jnp.where(lane >= hd, qp, zero)], axis=0)
            s = lax.dot_general(load_k(p), q2, NT_DIMS, preferred_element_type=F32) + load_bias(p)
            m = jnp.max(s, axis=0, keepdims=True)
            pr = jnp.exp2(s - m)
            l = jnp.sum(pr, axis=0, keepdims=True)
            pb = pr.astype(BF16)
            vt = load_vt(p)
            out_t.append(jnp.dot(vt[:hd], pb[:, :Bk], preferred_element_type=F32) / l[:, :Bk])
            out_t.append(jnp.dot(vt[hd:], pb[:, Bk:], preferred_element_type=F32) / l[:, Bk:])
            lse = (m + jnp.log2(l)) * LN2
            lse_t += [lse[:, :Bk], lse[:, Bk:]]
        lse_t.append(jnp.zeros((LANES - DIL_HEADS, Bk), F32))
        store(jnp.concatenate(out_t, axis=0).T, jnp.concatenate(lse_t, axis=0).T)

    def pair_cols(p):
        return slice(p * LANES, (p + 1) * LANES)

    edge_bias = lambda p: edge_ref[0, p]
    if rate == 1:
        for j in range(q_ref.shape[2] // Bk):
            rows = slice(j * Bk, (j + 1) * Bk)
            if j == 0:
                load_k = lambda p: jnp.concatenate([kp_ref[0, 0, :, pair_cols(p)],
                                                    kc_ref[0, 0, :Bk, pair_cols(p)]], axis=0)
                load_vt = lambda p: jnp.concatenate([vp_ref[0, 0, pair_cols(p), :],
                                                     vc_ref[0, 0, pair_cols(p), :Bk]], axis=1)
                load_bias = edge_bias
            else:
                krows = slice((j - 1) * Bk, (j + 1) * Bk)
                load_k = lambda p, krows=krows: kc_ref[0, 0, krows, pair_cols(p)]
                load_vt = lambda p, krows=krows: vc_ref[0, 0, pair_cols(p), krows]
                load_bias = lambda p: bias_ref[p]

            def store(o, lse, rows=rows):
                o_ref[rows, :] = o.astype(o_ref.dtype)
                lse_ref[rows, :] = lse

            block(lambda p, rows=rows: q_ref[0, 0, rows, pair_cols(p)], load_k, load_vt, load_bias, store)
    else:
        for c in range(q_ref.shape[1]):
            if with_prev:
                load_k = lambda p, c=c: jnp.concatenate([kp_ref[0, c, :, pair_cols(p)],
                                                         kc_ref[0, c, :, pair_cols(p)]], axis=0)
                load_vt = lambda p, c=c: jnp.concatenate([vp_ref[0, c, pair_cols(p), :],
                                                          vc_ref[0, c, pair_cols(p), :]], axis=1)
            else:
                load_k = lambda p, c=c: kc_ref[0, c, :, pair_cols(p)]
                load_vt = lambda p, c=c: vc_ref[0, c, pair_cols(p), :]

            def store(o, lse, c=c):
                for j in range(os_ref.shape[0]):
                    os_ref[j, pl.ds(c, Bk, stride=rate), :] = o[:, pair_cols(j)]
                lse_ref[pl.ds(c, Bk, stride=rate), :] = lse

            block(lambda p, c=c: q_ref[0, c, :, pair_cols(p)], load_k, load_vt, edge_bias, store)
        for j in range(os_ref.shape[0]):
            o_ref[:, pair_cols(j)] = os_ref[j].astype(o_ref.dtype)


def _t5_bucket(n):
    max_exact = NUM_BUCKETS // 2
    nf = jnp.maximum(n, max_exact).astype(F32)
    large = max_exact + (jnp.log(nf / max_exact) / math.log(MAX_DISTANCE / max_exact)
                         * (NUM_BUCKETS - max_exact)).astype(jnp.int32)
    large = jnp.minimum(large, NUM_BUCKETS - 1)
    return jnp.where(n < max_exact, n, large)


def _group_attention(q, k, vt, bias_table, rate, n_steps):
    B, _, L, D = q.shape
    S = L * rate
    Bk = SUB_BLOCK
    with_prev = L > Bk
    nk = 2 * Bk if with_prev else Bk
    n = max(Bk, TOKEN_TILE // rate)
    tm = n * rate
    nt = S // tm
    n_pairs = DIL_HEADS // 2
    ql = jnp.arange(Bk, dtype=jnp.int32)[:, None]
    kl = jnp.arange(2 * Bk, dtype=jnp.int32)[None, :]
    steps = ql + Bk - kl
    bucket = _t5_bucket(jnp.maximum(steps, 0) * rate)
    buckets = jnp.arange(NUM_BUCKETS, dtype=jnp.int32)
    bias = jnp.sum(jnp.where(bucket[None, :, :, None] == buckets[:, None, None, None],
                             bias_table.astype(F32)[:, None, None, :], 0.0), axis=0)
    band = ((steps >= 0) & (steps <= n_steps))[:, :, None]
    first = (kl < Bk)[:, :, None]

    def layout(t):
        t = t[:, 2 * Bk - nk:, :].transpose(1, 2, 0)
        return t.reshape(nk, n_pairs, 2 * Bk).transpose(1, 0, 2)

    bias_in = layout(jnp.where(band, bias * LOG2E, NEG_INF))
    bias_first = layout(jnp.where(band & ~first, bias * LOG2E, NEG_INF))
    edge = jnp.stack([bias_first, bias_in])

    cur_qk = pl.BlockSpec((1, rate, n, D), lambda b, i: (b, 0, i, 0))
    cur_vt = pl.BlockSpec((1, rate, D, n), lambda b, i: (b, 0, 0, i))
    per_n = n // Bk
    prev_qk = pl.BlockSpec((1, rate, Bk, D), lambda b, i: (b, 0, jnp.maximum(i * per_n - 1, 0), 0))
    prev_vt = pl.BlockSpec((1, rate, D, Bk), lambda b, i: (b, 0, 0, jnp.maximum(i * per_n - 1, 0)))
    edge_spec = pl.BlockSpec((1, n_pairs, nk, 2 * Bk), lambda b, i: (jnp.minimum(i, 1), 0, 0, 0))
    if with_prev:
        in_specs = [cur_qk, cur_qk, prev_qk, cur_vt, prev_vt, edge_spec]
        args = (q, k, k, vt, vt, edge)
    else:
        in_specs = [cur_qk, cur_qk, cur_vt, edge_spec]
        args = (q, k, vt, edge)
    if rate == 1:
        in_specs.append(_resident((n_pairs, nk, 2 * Bk), lambda b, i: (0, 0, 0)))
        args += (bias_in,)
    return pl.pallas_call(
        functools.partial(_attn_kernel, rate=rate, with_prev=with_prev),
        grid=(B, nt),
        in_specs=in_specs,
        out_specs=[pl.BlockSpec((tm, D), lambda b, i: (b * nt + i, 0)),
                   pl.BlockSpec((tm, LANES), lambda b, i: (b * nt + i, 0))],
        out_shape=[jax.ShapeDtypeStruct((B * S, D), BF16),
                   jax.ShapeDtypeStruct((B * S, LANES), F32)],
        scratch_shapes=[pltpu.VMEM((D // LANES, tm if rate > 1 else 8, LANES), F32)],
        compiler_params=_params("parallel", "parallel"),
        name="attn_rate%d" % rate,
    )(*args)


def _merge_kernel(o0_ref, o1_ref, o2_ref, l0_ref, l1_ref, l2_ref, ex_ref, out_ref):
    ex = ex_ref[...]

    def expand(w):
        hi = w.astype(BF16)
        r1 = w - hi.astype(F32)
        mid = r1.astype(BF16)
        lo = (r1 - mid.astype(F32)).astype(BF16)
        return (jnp.dot(hi, ex, preferred_element_type=F32) + jnp.dot(mid, ex, preferred_element_type=F32)
                + jnp.dot(lo, ex, preferred_element_type=F32))

    l0, l1, l2 = l0_ref[...], l1_ref[...], l2_ref[...]
    top = jnp.maximum(jnp.maximum(l0, l1), l2)
    w0, w1, w2 = jnp.exp(l0 - top), jnp.exp(l1 - top), jnp.exp(l2 - top)
    den = w0 + w1 + w2
    acc = expand(w0 / den) * o0_ref[...].astype(F32)
    acc = acc + expand(w1 / den) * o1_ref[...].astype(F32)
    acc = acc + expand(w2 / den) * o2_ref[...].astype(F32)
    out_ref[...] = acc.astype(out_ref.dtype)


def _merge(outs, lses):
    T, D = outs[0].shape
    tm = min(TOKEN_TILE, T)
    ex = jnp.where(jnp.arange(LANES)[:, None] == (jnp.arange(D)[None, :] // DIL_HEAD_DIM), 1.0, 0.0).astype(BF16)
    o_spec = pl.BlockSpec((tm, D), lambda i: (i, 0))
    l_spec = pl.BlockSpec((tm, LANES), lambda i: (i, 0))
    return pl.pallas_call(
        _merge_kernel,
        grid=(T // tm,),
        in_specs=[o_spec] * 3 + [l_spec] * 3 + [_resident((LANES, D), lambda i: (0, 0))],
        out_specs=o_spec,
        out_shape=jax.ShapeDtypeStruct((T, D), BF16),
        compiler_params=_params("parallel"),
        name="merge_groups",
    )(*outs, *lses, ex)


BATCH_SPLIT = 2


def kernel(x, ret_w_in, ret_w_out, kv_norm, w_kv, k_norm, dil_wq, q_norm, dil_wo, rel_bias,
           mixer_norm, ffn_norm, router_grp, router_grp_b, router_exp, router_exp_b,
           exp_gate, exp_up, exp_down):
    params = (ret_w_in, ret_w_out, kv_norm, w_kv, k_norm, dil_wq, q_norm, dil_wo, rel_bias,
              mixer_norm, ffn_norm, router_grp, router_grp_b, router_exp, router_exp_b,
              exp_gate, exp_up, exp_down)
    B = x.shape[0]
    nb = BATCH_SPLIT if B % BATCH_SPLIT == 0 else 1
    outs = [_trunk(x[i * (B // nb):(i + 1) * (B // nb)], *params) for i in range(nb)]
    return outs[0] if nb == 1 else jnp.concatenate(outs, axis=0)


def _trunk(x, ret_w_in, ret_w_out, kv_norm, w_kv, k_norm, dil_wq, q_norm, dil_wo, rel_bias,
           mixer_norm, ffn_norm, router_grp, router_grp_b, router_exp, router_exp_b,
           exp_gate, exp_up, exp_down):
    B, S, D = x.shape
    h = x.reshape(B * S, D)

    def moe_layer(layer, a, w_out, h):
        h1, *hn, route, route_t, counts = _post(
            a, w_out.astype(BF16), h, ffn_norm[layer], router_grp[layer], router_grp_b[layer],
            router_exp[layer], router_exp_b[layer])
        return _moe(h1, hn, route, route_t, counts, layer, exp_gate, exp_up, exp_down)

    proj = _norm_proj(h, mixer_norm[0], ret_w_in[0].astype(BF16))
    y = _retention(proj, B, S)
    h = moe_layer(0, y, ret_w_out[0], h)

    G = len(DIL_RATES)
    gd = DIL_HEADS * DIL_HEAD_DIM
    outs, lses = [], []
    for g in range(G):
        cq = slice(g * gd, (g + 1) * gd)
        cv = slice(G * gd + g * gd, G * gd + (g + 1) * gd)
        q, k, vt = _qkv(h, B, S, DIL_RATES[g], mixer_norm[1], kv_norm,
                        dil_wq[0][:, cq].astype(BF16), w_kv[:, cq].astype(BF16), w_kv[:, cv].astype(BF16),
                        q_norm[0][g], k_norm[g])
        o, lse = _group_attention(q, k, vt, rel_bias[:, g * DIL_HEADS:(g + 1) * DIL_HEADS],
                                  DIL_RATES[g], DIL_WINDOWS[g] // DIL_RATES[g])
        outs.append(o)
        lses.append(lse)
    att = _merge(outs, lses)
    h = moe_layer(1, att, dil_wo[0], h)
    return h.reshape(B, S, D)
```

```python
import functools
import math

import jax
import jax.numpy as jnp
from jax import lax
from jax.experimental import pallas as pl
from jax.experimental.pallas import tpu as pltpu
from jax.experimental.pallas import tpu_sc as plsc

F32 = jnp.float32
BF16 = jnp.bfloat16

EPS = 1e-6
NEG_INF = -1e30

RET_HEADS = 4
RET_CHUNK = 128
ROPE_BASE = 10000.0

DIL_WINDOWS = (128, 512, 2048)
DIL_RATES = (1, 4, 16)
DIL_HEADS = 16
DIL_HEAD_DIM = 64
SUB_BLOCK = 128
NUM_BUCKETS = 32
MAX_DISTANCE = 2048

MOE_GROUPS = 4
EXPERTS_PER_GROUP = 8
N_EXPERTS = MOE_GROUPS * EXPERTS_PER_GROUP
MOE_BLOCK = 256

LANES = 128
TOKEN_TILE = 512
VMEM_LIMIT = 56 * 1024 * 1024

NT_DIMS = (((1,), (1,)), ((), ()))
LOG2E = math.log2(math.e)
LN2 = math.log(2.0)


def _params(*sem):
    return pltpu.CompilerParams(dimension_semantics=sem, vmem_limit_bytes=VMEM_LIMIT)


def _resident(shape, index_map):
    return pl.BlockSpec(shape, index_map, pipeline_mode=pl.Buffered(1))


def _rms(x):
    return x * lax.rsqrt(jnp.mean(x * x, axis=-1, keepdims=True) + EPS)


def _norm_proj_kernel(x_ref, g_ref, w_ref, o_ref, *, col_tile):
    xn = (_rms(x_ref[...]) * g_ref[...]).astype(BF16)
    for j in range(w_ref.shape[1] // col_tile):
        cols = slice(j * col_tile, (j + 1) * col_tile)
        o_ref[:, cols] = jnp.dot(xn, w_ref[:, cols], preferred_element_type=F32).astype(o_ref.dtype)


def _norm_proj(x, gain, w):
    T, D = x.shape
    N = w.shape[1]
    tm = min(TOKEN_TILE, T)
    return pl.pallas_call(
        functools.partial(_norm_proj_kernel, col_tile=512),
        grid=(T // tm,),
        in_specs=[pl.BlockSpec((tm, D), lambda i: (i, 0)),
                  _resident((1, D), lambda i: (0, 0)),
                  _resident((D, N), lambda i: (0, 0))],
        out_specs=pl.BlockSpec((tm, N), lambda i: (i, 0)),
        out_shape=jax.ShapeDtypeStruct((T, N), BF16),
        compiler_params=_params("parallel"),
        name="norm_proj",
    )(x, gain.reshape(1, D), w)


def _retention_kernel(q_ref, k_ref, v_ref, gate_ref, cos_ref, sin_ref, din_ref, xi_ref, zeta_ref,
                      cd_ref, o_ref, state_ref, *, k_scale):
    C = RET_CHUNK
    heads = state_ref.shape[0]
    dk = q_ref.shape[2] // heads
    dv = v_ref.shape[2] // heads
    half = dk // 2
    ts = q_ref.shape[1]
    tile = pl.program_id(1)

    @pl.when(tile == 0)
    def _():
        state_ref[...] = jnp.zeros_like(state_ref)

    def rot(t, cos, sin):
        t1, t2 = t[:, :half], t[:, half:]
        return jnp.concatenate([t1 * cos - t2 * sin, t1 * sin + t2 * cos], axis=-1)

    for c in range(ts // C):
        rows = slice(c * C, (c + 1) * C)
        pos = pl.ds(pl.multiple_of(tile * ts + c * C, C), C)
        cos = cos_ref[pos, :]
        sin = sin_ref[pos, :]
        for hh in range(heads):
            qk_cols = slice(hh * dk, (hh + 1) * dk)
            v_cols = slice(hh * dv, (hh + 1) * dv)
            q = rot(q_ref[0, rows, qk_cols].astype(F32), cos, sin)
            k = rot(k_ref[0, rows, qk_cols].astype(F32), cos, sin) * k_scale
            v = v_ref[0, rows, v_cols]
            qb = q.astype(BF16)
            kb = k.astype(BF16)
            s = lax.dot_general(qb, kb, NT_DIMS, preferred_element_type=F32) * din_ref[hh]
            inner = jnp.dot(s.astype(BF16), v, preferred_element_type=F32)
            state = state_ref[hh]
            cross = jnp.dot(qb, state.astype(BF16), preferred_element_type=F32) * xi_ref[hh]
            kz_t = (k * zeta_ref[hh]).T.astype(BF16)
            state_ref[hh] = state * cd_ref[hh] + jnp.dot(kz_t, v, preferred_element_type=F32)
            o = _rms(inner + cross)
            g = gate_ref[0, rows, v_cols].astype(F32)
            o_ref[0, rows, v_cols] = (g * jax.nn.sigmoid(g) * o).astype(o_ref.dtype)


def _retention(proj, B, S):
    H, C = RET_HEADS, RET_CHUNK
    D = proj.shape[1] // 6
    dk, dv = D // H, 2 * D // H
    half = dk // 2
    ts = min(TOKEN_TILE, S)
    proj = proj.reshape(B, S, 6 * D)

    pos = jnp.arange(S, dtype=F32)
    inv = 1.0 / (ROPE_BASE ** jnp.linspace(0.0, 1.0, half, dtype=F32))
    ang = pos[:, None] * inv[None, :]
    cos, sin = jnp.cos(ang), jnp.sin(ang)
    log_g = jnp.log(1.0 - 2.0 ** (-5.0 - jnp.arange(H, dtype=F32)))
    idx = jnp.arange(C, dtype=F32)
    diff = idx[:, None] - idx[None, :]
    d_in = jnp.where(diff >= 0, jnp.exp(log_g[:, None, None] * jnp.maximum(diff, 0.0)), 0.0)
    xi = jnp.exp(log_g[:, None] * (idx + 1.0))[:, :, None]
    zeta = jnp.exp(log_g[:, None] * (C - 1.0 - idx))[:, :, None]
    chunk_decay = jnp.exp(log_g * C)[:, None, None]

    wq, wv = H * dk, H * dv
    per_head = lambda shape: _resident((H,) + shape, lambda b, i: (0, 0, 0))
    out = pl.pallas_call(
        functools.partial(_retention_kernel, k_scale=dk ** -0.5),
        grid=(B, S // ts),
        in_specs=[pl.BlockSpec((1, ts, wq), lambda b, i: (b, i, 0)),
                  pl.BlockSpec((1, ts, wq), lambda b, i: (b, i, 1)),
                  pl.BlockSpec((1, ts, wv), lambda b, i: (b, i, 1)),
                  pl.BlockSpec((1, ts, wv), lambda b, i: (b, i, 2)),
                  _resident((S, half), lambda b, i: (0, 0)),
                  _resident((S, half), lambda b, i: (0, 0)),
                  per_head((C, C)), per_head((C, 1)), per_head((C, 1)), per_head((1, 1))],
        out_specs=pl.BlockSpec((1, ts, wv), lambda b, i: (b, i, 0)),
        out_shape=jax.ShapeDtypeStruct((B, S, H * dv), BF16),
        scratch_shapes=[pltpu.VMEM((H, dk, dv), F32)],
        compiler_params=_params("parallel", "arbitrary"),
        name="retention",
    )(proj, proj, proj, proj, cos, sin, d_in, xi, zeta, chunk_decay)
    return out.reshape(B * S, H * dv)


ROUTE_E1, ROUTE_E2, ROUTE_G1, ROUTE_G2, ROUTE_R1, ROUTE_R2 = range(6)
ROUTE_ROWS = 8
ROUTER_EXP_LANE0 = MOE_GROUPS
HI16 = 0xFFFF0000


def _pack_bf16_pair(a, b):
    ua = lax.bitcast_convert_type(a.astype(BF16).astype(F32), jnp.uint32)
    ub = lax.bitcast_convert_type(b.astype(BF16).astype(F32), jnp.uint32)
    return ua | (ub >> 16)


def _unpack_bf16_pair(w):
    a = lax.bitcast_convert_type(w & jnp.uint32(HI16), F32)
    b = lax.bitcast_convert_type(w << 16, F32)
    return a, b


ROW_PARTS = 2


def _part_cols(D, part):
    w = D // 2 // ROW_PARTS
    return slice(part * w, (part + 1) * w), slice(D // 2 + part * w, D // 2 + (part + 1) * w)


def _pack_part(x, part):
    hi, lo = _part_cols(x.shape[1], part)
    return _pack_bf16_pair(x[:, hi], x[:, lo])


def _post_kernel(a_ref, w_ref, h_ref, g_ref, wr_ref, br_ref, tri_ref,
                 h1_ref, hn0_ref, hn1_ref, route_ref, route_t_ref, cnt_ref, carry_ref):
    hn_refs = (hn0_ref, hn1_ref)
    @pl.when(pl.program_id(0) == 0)
    def _():
        carry_ref[...] = jnp.zeros_like(carry_ref)

    half = h_ref.shape[0] // 2
    D = h_ref.shape[1]
    lane = lax.broadcasted_iota(jnp.int32, (half, LANES), 1).astype(F32)
    ninf = -jnp.inf

    def first_argmax(vals):
        top = jnp.max(vals, axis=1, keepdims=True)
        where = jnp.min(jnp.where(vals == top, lane, float(LANES)), axis=1, keepdims=True)
        return top, where

    def route_half(rows):
        h1 = h_ref[rows, :] + jnp.dot(a_ref[rows, :], w_ref[...], preferred_element_type=F32)
        h1_ref[rows, :] = h1
        hn = _rms(h1) * g_ref[...]
        for part, ref in enumerate(hn_refs):
            ref[rows, :] = _pack_part(hn, part)
        hi = hn.astype(BF16)
        lo = (hn - hi.astype(F32)).astype(BF16)
        logits = jnp.dot(jnp.concatenate([hi, hi, lo], axis=1), wr_ref[...],
                         preferred_element_type=F32) + br_ref[...]

        is_grp = lane < MOE_GROUPS
        lg = jnp.where(is_grp, logits, ninf)
        mg, grp = first_argmax(lg)
        p_grp = 1.0 / jnp.sum(jnp.where(is_grp, jnp.exp(lg - mg), 0.0), axis=1, keepdims=True)

        e_lane = lane - ROUTER_EXP_LANE0
        in_grp = (e_lane < N_EXPERTS) & (jnp.floor(e_lane * (1.0 / EXPERTS_PER_GROUP)) == grp)
        le = jnp.where(in_grp, logits, ninf)
        v1, i1 = first_argmax(le)
        le2 = jnp.where(lane == i1, ninf, le)
        v2, i2 = first_argmax(le2)
        e = jnp.exp(v2 - v1)
        hit1 = lane == (i1 - ROUTER_EXP_LANE0)
        hit2 = lane == (i2 - ROUTER_EXP_LANE0)
        onehot = jnp.where(hit1 | hit2, 1.0, 0.0)
        earlier = jnp.dot(tri_ref[...], onehot.astype(BF16), preferred_element_type=F32)
        return dict(e1=i1 - ROUTER_EXP_LANE0, e2=i2 - ROUTER_EXP_LANE0, g1=p_grp / (1.0 + e),
                    g2=p_grp * e / (1.0 + e), hit1=hit1, hit2=hit2, earlier=earlier,
                    count=jnp.sum(onehot, axis=0, keepdims=True))

    halves = [route_half(slice(0, half)), route_half(slice(half, 2 * half))]
    carry = carry_ref[...]
    for j, r in enumerate(halves):
        before = carry + r["earlier"]
        r1 = jnp.sum(jnp.where(r["hit1"], before, 0.0), axis=1, keepdims=True)
        r2 = jnp.sum(jnp.where(r["hit2"], before, 0.0), axis=1, keepdims=True)
        carry = carry + r["count"]
        route = jnp.zeros((half, LANES), F32)
        for slot, val in ((ROUTE_E1, r["e1"]), (ROUTE_E2, r["e2"]), (ROUTE_G1, r["g1"]),
                          (ROUTE_G2, r["g2"]), (ROUTE_R1, r1), (ROUTE_R2, r2)):
            route = jnp.where(lane == slot, val, route)
        route_ref[j * half:(j + 1) * half, :] = route
        route_t_ref[:, j * half:(j + 1) * half] = route.T[:ROUTE_ROWS]
    carry_ref[...] = carry
    cnt_ref[...] = carry


def _post(a, w, h, gain, w_grp, b_grp, w_exp, b_exp):
    T, D = h.shape
    K = a.shape[1]
    tm = min(TOKEN_TILE, T)
    n_r = MOE_GROUPS + N_EXPERTS
    wr = jnp.zeros((D, LANES), F32).at[:, :n_r].set(jnp.concatenate([w_grp, w_exp], axis=1))
    wr_hi = wr.astype(BF16)
    wr_lo = (wr - wr_hi.astype(F32)).astype(BF16)
    wr3 = jnp.concatenate([wr_hi, wr_lo, wr_hi], axis=0)
    br = jnp.zeros((1, LANES), F32).at[0, :n_r].set(jnp.concatenate([b_grp, b_exp]))
    tri = jnp.tril(jnp.ones((tm // 2, tm // 2), BF16), k=-1)
    return pl.pallas_call(
        _post_kernel,
        grid=(T // tm,),
        in_specs=[pl.BlockSpec((tm, K), lambda i: (i, 0)),
                  _resident((K, D), lambda i: (0, 0)),
                  pl.BlockSpec((tm, D), lambda i: (i, 0)),
                  _resident((1, D), lambda i: (0, 0)),
                  _resident((3 * D, LANES), lambda i: (0, 0)),
                  _resident((1, LANES), lambda i: (0, 0)),
                  _resident((tm // 2, tm // 2), lambda i: (0, 0))],
        out_specs=[pl.BlockSpec((tm, D), lambda i: (i, 0))]
                  + [pl.BlockSpec((tm, D // 2 // ROW_PARTS), lambda i: (i, 0))] * ROW_PARTS
                  + [pl.BlockSpec((tm, LANES), lambda i: (i, 0)),
                   pl.BlockSpec((ROUTE_ROWS, tm), lambda i: (0, i)),
                   pl.BlockSpec((1, LANES), lambda i: (0, 0))],
        out_shape=[jax.ShapeDtypeStruct((T, D), F32)]
                  + [jax.ShapeDtypeStruct((T, D // 2 // ROW_PARTS), jnp.uint32)] * ROW_PARTS
                  + [jax.ShapeDtypeStruct((T, LANES), F32),
                   jax.ShapeDtypeStruct((ROUTE_ROWS, T), F32),
                   jax.ShapeDtypeStruct((1, LANES), F32)],
        scratch_shapes=[pltpu.VMEM((1, LANES), F32)],
        compiler_params=_params("arbitrary"),
        name="post_mixer",
    )(a, w, h, gain.reshape(1, D), wr3, br, tri)


SC_WINDOW = 128


def _sc_mesh():
    return plsc.VectorSubcoreMesh(core_axis_name="core", subcore_axis_name="subcore")


def _sc_window_specs(W):
    rows = pl.BlockSpec((SC_WINDOW, W), lambda i: (i, 0))
    idx = pl.BlockSpec((1, SC_WINDOW), lambda i: (0, i))
    return rows, idx


def _sc_pipeline(body, n_rows, in_specs, out_specs):
    return pltpu.emit_pipeline(body, grid=(n_rows // SC_WINDOW,), in_specs=in_specs, out_specs=out_specs,
                               core_axis_name=("core", "subcore"), dimension_semantics=(pltpu.PARALLEL,))


def _dispatch(xs, dest, pad_idx, P):
    T, W = xs[0].shape
    n_pad = pad_idx.shape[1]
    n = len(xs)
    rows, idx = _sc_window_specs(W)
    zero_rows = pl.BlockSpec((SC_WINDOW, W), lambda i: (0, 0))
    out = jax.ShapeDtypeStruct((P, W), xs[0].dtype)

    @functools.partial(pl.kernel, out_type=(out,) * n, mesh=_sc_mesh(), scratch_types=[], name="dispatch")
    def scatter(*refs):
        x_hbm, (d0_hbm, d1_hbm, z_hbm, p_hbm), o_hbm = refs[:n], refs[n:n + 4], refs[n + 4:]
        for x, o in zip(x_hbm, o_hbm):
            def put_pair(x_vmem, i0_vmem, i1_vmem, o=o):
                pltpu.sync_copy(x_vmem, o.at[i0_vmem.at[0]])
                pltpu.sync_copy(x_vmem, o.at[i1_vmem.at[0]])

            def put(x_vmem, i_vmem, o=o):
                pltpu.sync_copy(x_vmem, o.at[i_vmem.at[0]])

            _sc_pipeline(put_pair, T, [rows, idx, idx], [])(x, d0_hbm, d1_hbm)
            _sc_pipeline(put, n_pad, [zero_rows, idx], [])(z_hbm, p_hbm)

    return scatter(*xs, dest[0:1], dest[1:2], jnp.zeros((SC_WINDOW, W), xs[0].dtype), pad_idx)


WEIGHT_LEADS = (3, 2, 1)
EXPERT_SLOTS = max(WEIGHT_LEADS) + 1


def _expert_kernel(blk_exp_ref, slot_ref, n_used_ref, *refs):
    x_refs = refs[:ROW_PARTS]
    w_refs = refs[ROW_PARTS:ROW_PARTS + 3]
    o_refs = refs[ROW_PARTS + 3:2 * ROW_PARTS + 3]
    w_slots = refs[2 * ROW_PARTS + 3:]
    lead = max(WEIGHT_LEADS)
    n_used = n_used_ref[0]
    i = pl.program_id(0) - lead
    D = w_slots[0].shape[1]

    for w_ref, w_s, ahead in zip(w_refs, w_slots, WEIGHT_LEADS):
        j = i + ahead
        jc = jnp.clip(j, 0, n_used - 1)
        arrived = (j >= 0) & (j < n_used) & ((j == 0) | (blk_exp_ref[jc] != blk_exp_ref[jnp.maximum(jc - 1, 0)]))

        @pl.when(arrived)
        def _(w_ref=w_ref, w_s=w_s, jc=jc):
            w_s[slot_ref[jc]] = w_ref[0, 0].astype(BF16)

    @pl.when((i >= 0) & (i < n_used))
    def _():
        slot = slot_ref[jnp.clip(i, 0, n_used - 1)]
        wg_s, wu_s, wd_s = w_slots
        pieces = []
        for part, x_ref in enumerate(x_refs):
            for cols, val in zip(_part_cols(D, part), _unpack_bf16_pair(x_ref[...])):
                pieces.append((cols, val.astype(BF16)))

        def up(w_s):
            return sum(jnp.dot(val, w_s[slot, cols, :], preferred_element_type=F32) for cols, val in pieces)

        g = up(wg_s)
        hid = (g * jax.nn.sigmoid(g) * up(wu_s)).astype(BF16)
        y = jnp.dot(hid, wd_s[slot], preferred_element_type=F32)
        for part, o_ref in enumerate(o_refs):
            o_ref[...] = _pack_part(y, part)

    @pl.when(i >= n_used)
    def _():
        for o_ref in o_refs:
            o_ref[...] = jnp.zeros_like(o_ref)


def _experts(xs, blk_exp, n_used, layer, w_gate, w_up, w_down):
    P, W = xs[0].shape
    D = 2 * W * ROW_PARTS
    FF = w_gate.shape[3]
    nblk = P // MOE_BLOCK
    lead = max(WEIGHT_LEADS)
    changes = jnp.concatenate([jnp.zeros((1,), jnp.int32), (blk_exp[1:] != blk_exp[:-1]).astype(jnp.int32)])
    slot = jnp.cumsum(changes) % EXPERT_SLOTS

    def x_map(g, be, sl, nu):
        return (jnp.clip(g - lead, 0, nu[0] - 1), 0)

    def o_map(g, be, sl, nu):
        return (jnp.maximum(g - lead, 0), 0)

    def w_map(ahead):
        return lambda g, be, sl, nu: (layer, be[jnp.clip(g - lead + ahead, 0, nu[0] - 1)], 0, 0)

    wg_spec, wu_spec, wd_spec = (pl.BlockSpec((1, 1) + shape, w_map(ahead))
                                 for shape, ahead in zip(((D, FF), (D, FF), (FF, D)), WEIGHT_LEADS))
    return pl.pallas_call(
        _expert_kernel,
        grid_spec=pltpu.PrefetchScalarGridSpec(
            num_scalar_prefetch=3,
            grid=(nblk + lead,),
            in_specs=[pl.BlockSpec((MOE_BLOCK, W), x_map)] * ROW_PARTS + [wg_spec, wu_spec, wd_spec],
            out_specs=[pl.BlockSpec((MOE_BLOCK, W), o_map)] * ROW_PARTS,
            scratch_shapes=[pltpu.VMEM((EXPERT_SLOTS, D, FF), BF16), pltpu.VMEM((EXPERT_SLOTS, D, FF), BF16),
                            pltpu.VMEM((EXPERT_SLOTS, FF, D), BF16)]),
        out_shape=[jax.ShapeDtypeStruct((P, W), jnp.uint32)] * ROW_PARTS,
        compiler_params=_params("arbitrary"),
        name="experts",
    )(blk_exp, slot, n_used, *xs, w_gate, w_up, w_down)


def _gather_pairs(ys, dest):
    W = ys[0].shape[1]
    T = dest.shape[1]
    n = len(ys)
    rows, idx = _sc_window_specs(W)
    out = jax.ShapeDtypeStruct((T, W), ys[0].dtype)

    @functools.partial(pl.kernel, out_type=(out,) * (2 * n), mesh=_sc_mesh(), scratch_types=[],
                       name="gather_pairs")
    def gather(*refs):
        y_hbm, d_hbm, o_hbm = refs[:n], refs[n:n + 2], refs[n + 2:]
        for slot, d in enumerate(d_hbm):
            for y, o in zip(y_hbm, o_hbm[slot * n:(slot + 1) * n]):
                def get(i_vmem, o_vmem, y=y):
                    pltpu.sync_copy(y.at[i_vmem.at[0]], o_vmem)

                _sc_pipeline(get, T, [idx], [rows])(d, o)

    return gather(*ys, dest[0:1], dest[1:2])


def _combine_kernel(h_ref, route_ref, *refs):
    y_refs, o_ref = refs[:-1], refs[-1]
    D = h_ref.shape[1]
    route = route_ref[...]
    gates = (route[:, ROUTE_G1:ROUTE_G1 + 1], route[:, ROUTE_G2:ROUTE_G2 + 1])
    for part in range(ROW_PARTS):
        slots = [_unpack_bf16_pair(y_refs[slot * ROW_PARTS + part][...]) for slot in range(2)]
        for half, cols in enumerate(_part_cols(D, part)):
            o_ref[:, cols] = h_ref[:, cols] + (gates[0] * slots[0][half] + gates[1] * slots[1][half])


def _combine(h, route, pairs):
    T, D = h.shape
    W = pairs[0].shape[1]
    tm = min(TOKEN_TILE, T)
    return pl.pallas_call(
        _combine_kernel,
        grid=(T // tm,),
        in_specs=[pl.BlockSpec((tm, D), lambda i: (i, 0)), pl.BlockSpec((tm, LANES), lambda i: (i, 0))]
                 + [pl.BlockSpec((tm, W), lambda i: (i, 0))] * len(pairs),
        out_specs=pl.BlockSpec((tm, D), lambda i: (i, 0)),
        out_shape=jax.ShapeDtypeStruct((T, D), F32),
        compiler_params=_params("parallel"),
        name="combine",
    )(h, route, *pairs)


def _moe(h1, hn, route, route_t, counts, layer, w_gate, w_up, w_down):
    T, D = h1.shape
    A = 2 * T
    nblk = -(-A // MOE_BLOCK) + N_EXPERTS
    P = nblk * MOE_BLOCK
    eid = route_t[ROUTE_E1:ROUTE_E2 + 1].astype(jnp.int32)
    rank = route_t[ROUTE_R1:ROUTE_R2 + 1].astype(jnp.int32)
    cnt = counts[0, :N_EXPERTS].astype(jnp.int32)
    padded = (cnt + MOE_BLOCK - 1) // MOE_BLOCK * MOE_BLOCK
    pends = jnp.cumsum(padded)
    pstarts = pends - padded
    experts = jnp.arange(N_EXPERTS, dtype=jnp.int32)
    start_of = jnp.sum(jnp.where(eid[:, None, :] == experts[None, :, None], pstarts[None, :, None], 0), axis=1)
    dest = start_of + rank
    blk_start = jnp.arange(nblk, dtype=jnp.int32) * MOE_BLOCK
    blk_exp = jnp.minimum(jnp.sum((pends[None, :] <= blk_start[:, None]).astype(jnp.int32), axis=1),
                          N_EXPERTS - 1)
    n_used = pends[-1:] // MOE_BLOCK

    gap_start = jnp.concatenate([pstarts + cnt, pends[-1:]])
    gap_len = jnp.concatenate([padded - cnt, P - pends[-1:]])
    gap_end = jnp.cumsum(gap_len)
    j = jnp.arange(P - A, dtype=jnp.int32)
    gap_of = jnp.sum((gap_end[None, :] <= j[:, None]).astype(jnp.int32), axis=1)
    sel = gap_of[:, None] == jnp.arange(N_EXPERTS + 1, dtype=jnp.int32)[None, :]
    pad_idx = jnp.sum(jnp.where(sel, (gap_start - (gap_end - gap_len))[None, :] + j[:, None], 0), axis=1)

    xs = _dispatch(hn, dest, pad_idx.reshape(1, P - A), P)
    ys = _experts(xs, blk_exp, n_used, layer, w_gate, w_up, w_down)
    return _combine(h1, route, _gather_pairs(ys, dest))


def _qkv_kernel(x_ref, gq_ref, gkv_ref, wq_ref, wk_ref, wvt_ref, qn_ref, kn_ref, seg_ref,
                q_ref, k_ref, vt_ref, xs_ref, *, rate, n, n_chunks):
    seg = seg_ref[...]
    width = seg.shape[0]
    res_per_chunk = q_ref.shape[1]
    chunk = pl.program_id(2)

    def head_norm(t, gain):
        cols = []
        for j in range(t.shape[1] // width):
            tj = t[:, j * width:(j + 1) * width]
            ms = jnp.dot((tj * tj).astype(BF16), seg, preferred_element_type=F32)
            cols.append(tj * lax.rsqrt(ms + EPS))
        return jnp.concatenate(cols, axis=1) * gain

    def project(x):
        y = _rms(x)
        xq = (y * gq_ref[...]).astype(BF16)
        xkv = (y * gkv_ref[...]).astype(BF16)
        q = head_norm(jnp.dot(xq, wq_ref[...], preferred_element_type=F32), qn_ref[...]).astype(q_ref.dtype)
        k = head_norm(jnp.dot(xkv, wk_ref[...], preferred_element_type=F32), kn_ref[...]).astype(k_ref.dtype)
        vt = lax.dot_general(wvt_ref[...], xkv, NT_DIMS, preferred_element_type=F32).astype(vt_ref.dtype)
        for j in range(res_per_chunk):
            q_ref[0, j] = q[j * n:(j + 1) * n]
            k_ref[0, j] = k[j * n:(j + 1) * n]
            vt_ref[0, j] = vt[:, j * n:(j + 1) * n]

    if rate == 1:
        project(x_ref[...])
        return

    @pl.when(chunk == 0)
    def _():
        for j in range(xs_ref.shape[0]):
            xs_ref[j] = x_ref[:, j * LANES:(j + 1) * LANES]

    for ch in range(n_chunks):
        @pl.when(chunk == ch)
        def _(ch=ch):
            residues = range(ch * res_per_chunk, (ch + 1) * res_per_chunk)
            project(jnp.concatenate(
                [jnp.concatenate([xs_ref[j, pl.ds(c, n, stride=rate), :] for j in range(xs_ref.shape[0])], axis=1)
                 for c in residues], axis=0))


def _qkv(h, B, S, rate, gq, gkv, wq, wk, wv, qn, kn):
    D = h.shape[1]
    L = S // rate
    hd = DIL_HEAD_DIM
    n = max(SUB_BLOCK, TOKEN_TILE // rate)
    tm = n * rate
    res_per_chunk = max(1, TOKEN_TILE // n)
    n_chunks = rate // res_per_chunk
    width = 2 * LANES
    ii = jnp.arange(width)
    seg = jnp.where((ii[:, None] // hd) == (ii[None, :] // hd), 1.0 / hd, 0.0).astype(BF16)
    row = lambda g: jnp.tile(g, D // hd).reshape(1, D)
    const = lambda shape: _resident(shape, lambda b, i, c: (0,) * len(shape))
    qk_spec = pl.BlockSpec((1, res_per_chunk, n, D), lambda b, i, c: (b, c, i, 0))
    qk_shape = jax.ShapeDtypeStruct((B, rate, L, D), BF16)
    return pl.pallas_call(
        functools.partial(_qkv_kernel, rate=rate, n=n, n_chunks=n_chunks),
        grid=(B, S // tm, n_chunks),
        in_specs=[pl.BlockSpec((tm, D), lambda b, i, c: (b * (S // tm) + i, 0)),
                  const((1, D)), const((1, D)), const((D, D)), const((D, D)), const((D, D)),
                  const((1, D)), const((1, D)), const((width, width))],
        out_specs=[qk_spec, qk_spec,
                   pl.BlockSpec((1, res_per_chunk, D, n), lambda b, i, c: (b, c, 0, i))],
        out_shape=[qk_shape, qk_shape, jax.ShapeDtypeStruct((B, rate, D, L), BF16)],
        scratch_shapes=[pltpu.VMEM((D // LANES, tm if rate > 1 else 8, LANES), F32)],
        compiler_params=_params("parallel", "parallel", "arbitrary"),
        name="qkv_rate%d" % rate,
    )(h, gq.reshape(1, D), gkv.reshape(1, D), wq, wk, wv.T, row(qn) * (hd ** -0.5 * LOG2E), row(kn), seg)


def _attn_kernel(*refs, rate, with_prev):
    if rate == 1:
        q_ref, kc_ref, kp_ref, vc_ref, vp_ref, edge_ref, bias_ref, o_ref, lse_ref, os_ref = refs
    elif with_prev:
        q_ref, kc_ref, kp_ref, vc_ref, vp_ref, edge_ref, o_ref, lse_ref, os_ref = refs
    else:
        q_ref, kc_ref, vc_ref, edge_ref, o_ref, lse_ref, os_ref = refs
    Bk = SUB_BLOCK
    hd = DIL_HEAD_DIM
    n_pairs = DIL_HEADS // 2
    lane = lax.broadcasted_iota(jnp.int32, (Bk, LANES), 1)

    def block(load_q, load_k, load_vt, load_bias, store):
        out_t, lse_t = [], []
        for p in range(n_pairs):
            qp = load_q(p)
            zero = jnp.zeros_like(qp)
            q2 = jnp.concatenate([jnp.where(lane < hd, qp, zero), jnp.where(lane >= hd, qp, zero)], axis=0)
            s = lax.dot_general(load_k(p), q2, NT_DIMS, preferred_element_type=F32) + load_bias(p)
            m = jnp.max(s, axis=0, keepdims=True)
            pr = jnp.exp2(s - m)
            l = jnp.sum(pr, axis=0, keepdims=True)
            pb = pr.astype(BF16)
            vt = load_vt(p)
            out_t.append(jnp.dot(vt[:hd], pb[:, :Bk], preferred_element_type=F32) / l[:, :Bk])
            out_t.append(jnp.dot(vt[hd:], pb[:, Bk:], preferred_element_type=F32) / l[:, Bk:])
            lse = (m + jnp.log2(l)) * LN2
            lse_t += [lse[:, :Bk], lse[:, Bk:]]
        lse_t.append(jnp.zeros((LANES - DIL_HEADS, Bk), F32))
        store(jnp.concatenate(out_t, axis=0).T, jnp.concatenate(lse_t, axis=0).T)

    def pair_cols(p):
        return slice(p * LANES, (p + 1) * LANES)

    edge_bias = lambda p: edge_ref[0, p]
    if rate == 1:
        for j in range(q_ref.shape[2] // Bk):
            rows = slice(j * Bk, (j + 1) * Bk)
            if j == 0:
                load_k = lambda p: jnp.concatenate([kp_ref[0, 0, :, pair_cols(p)],
                                                    kc_ref[0, 0, :Bk, pair_cols(p)]], axis=0)
                load_vt = lambda p: jnp.concatenate([vp_ref[0, 0, pair_cols(p), :],
                                                     vc_ref[0, 0, pair_cols(p), :Bk]], axis=1)
                load_bias = edge_bias
            else:
                krows = slice((j - 1) * Bk, (j + 1) * Bk)
                load_k = lambda p, krows=krows: kc_ref[0, 0, krows, pair_cols(p)]
                load_vt = lambda p, krows=krows: vc_ref[0, 0, pair_cols(p), krows]
                load_bias = lambda p: bias_ref[p]

            def store(o, lse, rows=rows):
                o_ref[rows, :] = o.astype(o_ref.dtype)
                lse_ref[rows, :] = lse

            block(lambda p, rows=rows: q_ref[0, 0, rows, pair_cols(p)], load_k, load_vt, load_bias, store)
    else:
        for c in range(q_ref.shape[1]):
            if with_prev:
                load_k = lambda p, c=c: jnp.concatenate([kp_ref[0, c, :, pair_cols(p)],
                                                         kc_ref[0, c, :, pair_cols(p)]], axis=0)
                load_vt = lambda p, c=c: jnp.concatenate([vp_ref[0, c, pair_cols(p), :],
                                                          vc_ref[0, c, pair_cols(p), :]], axis=1)
            else:
                load_k = lambda p, c=c: kc_ref[0, c, :, pair_cols(p)]
                load_vt = lambda p, c=c: vc_ref[0, c, pair_cols(p), :]

            def store(o, lse, c=c):
                for j in range(os_ref.shape[0]):
                    os_ref[j, pl.ds(c, Bk, stride=rate), :] = o[:, pair_cols(j)]
                lse_ref[pl.ds(c, Bk, stride=rate), :] = lse

            block(lambda p, c=c: q_ref[0, c, :, pair_cols(p)], load_k, load_vt, edge_bias, store)
        for j in range(os_ref.shape[0]):
            o_ref[:, pair_cols(j)] = os_ref[j].astype(o_ref.dtype)


def _t5_bucket(n):
    max_exact = NUM_BUCKETS // 2
    nf = jnp.maximum(n, max_exact).astype(F32)
    large = max_exact + (jnp.log(nf / max_exact) / math.log(MAX_DISTANCE / max_exact)
                         * (NUM_BUCKETS - max_exact)).astype(jnp.int32)
    large = jnp.minimum(large, NUM_BUCKETS - 1)
    return jnp.where(n < max_exact, n, large)


def _group_attention(q, k, vt, bias_table, rate, n_steps):
    B, _, L, D = q.shape
    S = L * rate
    Bk = SUB_BLOCK
    with_prev = L > Bk
    nk = 2 * Bk if with_prev else Bk
    n = max(Bk, TOKEN_TILE // rate)
    tm = n * rate
    nt = S // tm
    n_pairs = DIL_HEADS // 2
    ql = jnp.arange(Bk, dtype=jnp.int32)[:, None]
    kl = jnp.arange(2 * Bk, dtype=jnp.int32)[None, :]
    steps = ql + Bk - kl
    bucket = _t5_bucket(jnp.maximum(steps, 0) * rate)
    buckets = jnp.arange(NUM_BUCKETS, dtype=jnp.int32)
    bias = jnp.sum(jnp.where(bucket[None, :, :, None] == buckets[:, None, None, None],
                             bias_table.astype(F32)[:, None, None, :], 0.0), axis=0)
    band = ((steps >= 0) & (steps <= n_steps))[:, :, None]
    first = (kl < Bk)[:, :, None]

    def layout(t):
        t = t[:, 2 * Bk - nk:, :].transpose(1, 2, 0)
        return t.reshape(nk, n_pairs, 2 * Bk).transpose(1, 0, 2)

    bias_in = layout(jnp.where(band, bias * LOG2E, NEG_INF))
    bias_first = layout(jnp.where(band & ~first, bias * LOG2E, NEG_INF))
    edge = jnp.stack([bias_first, bias_in])

    cur_qk = pl.BlockSpec((1, rate, n, D), lambda b, i: (b, 0, i, 0))
    cur_vt = pl.BlockSpec((1, rate, D, n), lambda b, i: (b, 0, 0, i))
    per_n = n // Bk
    prev_qk = pl.BlockSpec((1, rate, Bk, D), lambda b, i: (b, 0, jnp.maximum(i * per_n - 1, 0), 0))
    prev_vt = pl.BlockSpec((1, rate, D, Bk), lambda b, i: (b, 0, 0, jnp.maximum(i * per_n - 1, 0)))
    edge_spec = pl.BlockSpec((1, n_pairs, nk, 2 * Bk), lambda b, i: (jnp.minimum(i, 1), 0, 0, 0))
    if with_prev:
        in_specs = [cur_qk, cur_qk, prev_qk, cur_vt, prev_vt, edge_spec]
        args = (q, k, k, vt, vt, edge)
    else:
        in_specs = [cur_qk, cur_qk, cur_vt, edge_spec]
        args = (q, k, vt, edge)
    if rate == 1:
        in_specs.append(_resident((n_pairs, nk, 2 * Bk), lambda b, i: (0, 0, 0)))
        args += (bias_in,)
    return pl.pallas_call(
        functools.partial(_attn_kernel, rate=rate, with_prev=with_prev),
        grid=(B, nt),
        in_specs=in_specs,
        out_specs=[pl.BlockSpec((tm, D), lambda b, i: (b * nt + i, 0)),
                   pl.BlockSpec((tm, LANES), lambda b, i: (b * nt + i, 0))],
        out_shape=[jax.ShapeDtypeStruct((B * S, D), BF16),
                   jax.ShapeDtypeStruct((B * S, LANES), F32)],
        scratch_shapes=[pltpu.VMEM((D // LANES, tm if rate > 1 else 8, LANES), F32)],
        compiler_params=_params("parallel", "parallel"),
        name="attn_rate%d" % rate,
    )(*args)


def _merge_kernel(o0_ref, o1_ref, o2_ref, l0_ref, l1_ref, l2_ref, ex_ref, out_ref):
    ex = ex_ref[...]

    def expand(w):
        hi = w.astype(BF16)
        r1 = w - hi.astype(F32)
        mid = r1.astype(BF16)
        lo = (r1 - mid.astype(F32)).astype(BF16)
        return (jnp.dot(hi, ex, preferred_element_type=F32) + jnp.dot(mid, ex, preferred_element_type=F32)
                + jnp.dot(lo, ex, preferred_element_type=F32))

    l0, l1, l2 = l0_ref[...], l1_ref[...], l2_ref[...]
    top = jnp.maximum(jnp.maximum(l0, l1), l2)
    w0, w1, w2 = jnp.exp(l0 - top), jnp.exp(l1 - top), jnp.exp(l2 - top)
    den = w0 + w1 + w2
    acc = expand(w0 / den) * o0_ref[...].astype(F32)
    acc = acc + expand(w1 / den) * o1_ref[...].astype(F32)
    acc = acc + expand(w2 / den) * o2_ref[...].astype(F32)
    out_ref[...] = acc.astype(out_ref.dtype)


def _merge(outs, lses):
    T, D = outs[0].shape
    tm = min(TOKEN_TILE, T)
    ex = jnp.where(jnp.arange(LANES)[:, None] == (jnp.arange(D)[None, :] // DIL_HEAD_DIM), 1.0, 0.0).astype(BF16)
    o_spec = pl.BlockSpec((tm, D), lambda i: (i, 0))
    l_spec = pl.BlockSpec((tm, LANES), lambda i: (i, 0))
    return pl.pallas_call(
        _merge_kernel,
        grid=(T // tm,),
        in_specs=[o_spec] * 3 + [l_spec] * 3 + [_resident((LANES, D), lambda i: (0, 0))],
        out_specs=o_spec,
        out_shape=jax.ShapeDtypeStruct((T, D), BF16),
        compiler_params=_params("parallel"),
        name="merge_groups",
    )(*outs, *lses, ex)


def kernel(x, ret_w_in, ret_w_out, kv_norm, w_kv, k_norm, dil_wq, q_norm, dil_wo, rel_bias,
           mixer_norm, ffn_norm, router_grp, router_grp_b, router_exp, router_exp_b,
           exp_gate, exp_up, exp_down):
    B, S, D = x.shape
    h = x.reshape(B * S, D)

    def moe_layer(layer, a, w_out, h):
        h1, *hn, route, route_t, counts = _post(
            a, w_out.astype(BF16), h, ffn_norm[layer], router_grp[layer], router_grp_b[layer],
            router_exp[layer], router_exp_b[layer])
        return _moe(h1, hn, route, route_t, counts, layer, exp_gate, exp_up, exp_down)

    proj = _norm_proj(h, mixer_norm[0], ret_w_in[0].astype(BF16))
    y = _retention(proj, B, S)
    h = moe_layer(0, y, ret_w_out[0], h)

    G = len(DIL_RATES)
    gd = DIL_HEADS * DIL_HEAD_DIM
    outs, lses = [], []
    for g in range(G):
        cq = slice(g * gd, (g + 1) * gd)
        cv = slice(G * gd + g * gd, G * gd + (g + 1) * gd)
        q, k, vt = _qkv(h, B, S, DIL_RATES[g], mixer_norm[1], kv_norm,
                        dil_wq[0][:, cq].astype(BF16), w_kv[:, cq].astype(BF16), w_kv[:, cv].astype(BF16),
                        q_norm[0][g], k_norm[g])
        o, lse = _group_attention(q, k, vt, rel_bias[:, g * DIL_HEADS:(g + 1) * DIL_HEADS],
                                  DIL_RATES[g], DIL_WINDOWS[g] // DIL_RATES[g])
        outs.append(o)
        lses.append(lse)
    att = _merge(outs, lses)
    h = moe_layer(1, att, dil_wo[0], h)
    return h.reshape(B, S, D)
```

```python
import functools
import math

import jax
import jax.numpy as jnp
from jax import lax
from jax.experimental import pallas as pl
from jax.experimental.pallas import tpu as pltpu
from jax.experimental.pallas import tpu_sc as plsc

F32 = jnp.float32
BF16 = jnp.bfloat16

EPS = 1e-6
NEG_INF = -1e30

RET_HEADS = 4
RET_CHUNK = 128
ROPE_BASE = 10000.0

DIL_WINDOWS = (128, 512, 2048)
DIL_RATES = (1, 4, 16)
DIL_HEADS = 16
DIL_HEAD_DIM = 64
SUB_BLOCK = 128
NUM_BUCKETS = 32
MAX_DISTANCE = 2048

MOE_GROUPS = 4
EXPERTS_PER_GROUP = 8
N_EXPERTS = MOE_GROUPS * EXPERTS_PER_GROUP
MOE_BLOCK = 512

LANES = 128
TOKEN_TILE = 512
VMEM_LIMIT = 56 * 1024 * 1024

NT_DIMS = (((1,), (1,)), ((), ()))
LOG2E = math.log2(math.e)
LN2 = math.log(2.0)


def _params(*sem):
    return pltpu.CompilerParams(dimension_semantics=sem, vmem_limit_bytes=VMEM_LIMIT)


def _resident(shape, index_map):
    return pl.BlockSpec(shape, index_map, pipeline_mode=pl.Buffered(1))


def _rms(x):
    return x * lax.rsqrt(jnp.mean(x * x, axis=-1, keepdims=True) + EPS)


def _norm_proj_kernel(x_ref, g_ref, w_ref, o_ref, *, col_tile):
    xn = (_rms(x_ref[...]) * g_ref[...]).astype(BF16)
    for j in range(w_ref.shape[1] // col_tile):
        cols = slice(j * col_tile, (j + 1) * col_tile)
        o_ref[:, cols] = jnp.dot(xn, w_ref[:, cols], preferred_element_type=F32).astype(o_ref.dtype)


def _norm_proj(x, gain, w):
    T, D = x.shape
    N = w.shape[1]
    tm = min(TOKEN_TILE, T)
    return pl.pallas_call(
        functools.partial(_norm_proj_kernel, col_tile=512),
        grid=(T // tm,),
        in_specs=[pl.BlockSpec((tm, D), lambda i: (i, 0)),
                  _resident((1, D), lambda i: (0, 0)),
                  _resident((D, N), lambda i: (0, 0))],
        out_specs=pl.BlockSpec((tm, N), lambda i: (i, 0)),
        out_shape=jax.ShapeDtypeStruct((T, N), BF16),
        compiler_params=_params("parallel"),
        name="norm_proj",
    )(x, gain.reshape(1, D), w)


def _retention_kernel(q_ref, k_ref, v_ref, gate_ref, cos_ref, sin_ref, din_ref, xi_ref, zeta_ref,
                      cd_ref, o_ref, state_ref, *, k_scale):
    C = RET_CHUNK
    heads = state_ref.shape[0]
    dk = q_ref.shape[2] // heads
    dv = v_ref.shape[2] // heads
    half = dk // 2
    ts = q_ref.shape[1]
    tile = pl.program_id(1)

    @pl.when(tile == 0)
    def _():
        state_ref[...] = jnp.zeros_like(state_ref)

    def rot(t, cos, sin):
        t1, t2 = t[:, :half], t[:, half:]
        return jnp.concatenate([t1 * cos - t2 * sin, t1 * sin + t2 * cos], axis=-1)

    for c in range(ts // C):
        rows = slice(c * C, (c + 1) * C)
        pos = pl.ds(pl.multiple_of(tile * ts + c * C, C), C)
        cos = cos_ref[pos, :]
        sin = sin_ref[pos, :]
        for hh in range(heads):
            qk_cols = slice(hh * dk, (hh + 1) * dk)
            v_cols = slice(hh * dv, (hh + 1) * dv)
            q = rot(q_ref[0, rows, qk_cols].astype(F32), cos, sin)
            k = rot(k_ref[0, rows, qk_cols].astype(F32), cos, sin) * k_scale
            v = v_ref[0, rows, v_cols]
            qb = q.astype(BF16)
            kb = k.astype(BF16)
            s = lax.dot_general(qb, kb, NT_DIMS, preferred_element_type=F32) * din_ref[hh]
            inner = jnp.dot(s.astype(BF16), v, preferred_element_type=F32)
            state = state_ref[hh]
            cross = jnp.dot(qb, state.astype(BF16), preferred_element_type=F32) * xi_ref[hh]
            kz_t = (k * zeta_ref[hh]).T.astype(BF16)
            state_ref[hh] = state * cd_ref[hh] + jnp.dot(kz_t, v, preferred_element_type=F32)
            o = _rms(inner + cross)
            g = gate_ref[0, rows, v_cols].astype(F32)
            o_ref[0, rows, v_cols] = (g * jax.nn.sigmoid(g) * o).astype(o_ref.dtype)


def _retention(proj, B, S):
    H, C = RET_HEADS, RET_CHUNK
    D = proj.shape[1] // 6
    dk, dv = D // H, 2 * D // H
    half = dk // 2
    ts = min(TOKEN_TILE, S)
    proj = proj.reshape(B, S, 6 * D)

    pos = jnp.arange(S, dtype=F32)
    inv = 1.0 / (ROPE_BASE ** jnp.linspace(0.0, 1.0, half, dtype=F32))
    ang = pos[:, None] * inv[None, :]
    cos, sin = jnp.cos(ang), jnp.sin(ang)
    log_g = jnp.log(1.0 - 2.0 ** (-5.0 - jnp.arange(H, dtype=F32)))
    idx = jnp.arange(C, dtype=F32)
    diff = idx[:, None] - idx[None, :]
    d_in = jnp.where(diff >= 0, jnp.exp(log_g[:, None, None] * jnp.maximum(diff, 0.0)), 0.0)
    xi = jnp.exp(log_g[:, None] * (idx + 1.0))[:, :, None]
    zeta = jnp.exp(log_g[:, None] * (C - 1.0 - idx))[:, :, None]
    chunk_decay = jnp.exp(log_g * C)[:, None, None]

    wq, wv = H * dk, H * dv
    per_head = lambda shape: _resident((H,) + shape, lambda b, i: (0, 0, 0))
    out = pl.pallas_call(
        functools.partial(_retention_kernel, k_scale=dk ** -0.5),
        grid=(B, S // ts),
        in_specs=[pl.BlockSpec((1, ts, wq), lambda b, i: (b, i, 0)),
                  pl.BlockSpec((1, ts, wq), lambda b, i: (b, i, 1)),
                  pl.BlockSpec((1, ts, wv), lambda b, i: (b, i, 1)),
                  pl.BlockSpec((1, ts, wv), lambda b, i: (b, i, 2)),
                  _resident((S, half), lambda b, i: (0, 0)),
                  _resident((S, half), lambda b, i: (0, 0)),
                  per_head((C, C)), per_head((C, 1)), per_head((C, 1)), per_head((1, 1))],
        out_specs=pl.BlockSpec((1, ts, wv), lambda b, i: (b, i, 0)),
        out_shape=jax.ShapeDtypeStruct((B, S, H * dv), BF16),
        scratch_shapes=[pltpu.VMEM((H, dk, dv), F32)],
        compiler_params=_params("parallel", "arbitrary"),
        name="retention",
    )(proj, proj, proj, proj, cos, sin, d_in, xi, zeta, chunk_decay)
    return out.reshape(B * S, H * dv)


ROUTE_E1, ROUTE_E2, ROUTE_G1, ROUTE_G2, ROUTE_R1, ROUTE_R2 = range(6)
ROUTE_ROWS = 8
ROUTER_EXP_LANE0 = MOE_GROUPS
HI16 = 0xFFFF0000


def _pack_bf16_pair(a, b):
    ua = lax.bitcast_convert_type(a.astype(BF16).astype(F32), jnp.uint32)
    ub = lax.bitcast_convert_type(b.astype(BF16).astype(F32), jnp.uint32)
    return ua | (ub >> 16)


def _unpack_bf16_pair(w):
    a = lax.bitcast_convert_type(w & jnp.uint32(HI16), F32)
    b = lax.bitcast_convert_type(w << 16, F32)
    return a, b


ROW_PARTS = 2


def _part_cols(D, part):
    w = D // 2 // ROW_PARTS
    return slice(part * w, (part + 1) * w), slice(D // 2 + part * w, D // 2 + (part + 1) * w)


def _pack_part(x, part):
    hi, lo = _part_cols(x.shape[1], part)
    return _pack_bf16_pair(x[:, hi], x[:, lo])


def _post_kernel(a_ref, w_ref, h_ref, g_ref, wr_ref, br_ref, tri_ref,
                 h1_ref, hn0_ref, hn1_ref, route_ref, route_t_ref, cnt_ref, carry_ref):
    hn_refs = (hn0_ref, hn1_ref)
    @pl.when(pl.program_id(0) == 0)
    def _():
        carry_ref[...] = jnp.zeros_like(carry_ref)

    half = h_ref.shape[0] // 2
    D = h_ref.shape[1]
    lane = lax.broadcasted_iota(jnp.int32, (half, LANES), 1).astype(F32)
    ninf = -jnp.inf

    def first_argmax(vals):
        top = jnp.max(vals, axis=1, keepdims=True)
        where = jnp.min(jnp.where(vals == top, lane, float(LANES)), axis=1, keepdims=True)
        return top, where

    h1_ref[...] = h_ref[...] + jnp.dot(a_ref[...], w_ref[...], preferred_element_type=F32)

    def route_half(rows):
        hn = _rms(h1_ref[rows, :]) * g_ref[...]
        for part, ref in enumerate(hn_refs):
            ref[rows, :] = _pack_part(hn, part)
        hi = hn.astype(BF16)
        lo = (hn - hi.astype(F32)).astype(BF16)
        logits = jnp.dot(jnp.concatenate([hi, hi, lo], axis=1), wr_ref[...],
                         preferred_element_type=F32) + br_ref[...]

        is_grp = lane < MOE_GROUPS
        lg = jnp.where(is_grp, logits, ninf)
        mg, grp = first_argmax(lg)
        p_grp = 1.0 / jnp.sum(jnp.where(is_grp, jnp.exp(lg - mg), 0.0), axis=1, keepdims=True)

        e_lane = lane - ROUTER_EXP_LANE0
        in_grp = (e_lane < N_EXPERTS) & (jnp.floor(e_lane * (1.0 / EXPERTS_PER_GROUP)) == grp)
        le = jnp.where(in_grp, logits, ninf)
        v1, i1 = first_argmax(le)
        le2 = jnp.where(lane == i1, ninf, le)
        v2, i2 = first_argmax(le2)
        e = jnp.exp(v2 - v1)
        hit1 = lane == (i1 - ROUTER_EXP_LANE0)
        hit2 = lane == (i2 - ROUTER_EXP_LANE0)
        onehot = jnp.where(hit1 | hit2, 1.0, 0.0)
        earlier = jnp.dot(tri_ref[...], onehot.astype(BF16), preferred_element_type=F32)
        return dict(e1=i1 - ROUTER_EXP_LANE0, e2=i2 - ROUTER_EXP_LANE0, g1=p_grp / (1.0 + e),
                    g2=p_grp * e / (1.0 + e), hit1=hit1, hit2=hit2, earlier=earlier,
                    count=jnp.sum(onehot, axis=0, keepdims=True))

    halves = [route_half(slice(0, half)), route_half(slice(half, 2 * half))]
    carry = carry_ref[...]
    for j, r in enumerate(halves):
        before = carry + r["earlier"]
        r1 = jnp.sum(jnp.where(r["hit1"], before, 0.0), axis=1, keepdims=True)
        r2 = jnp.sum(jnp.where(r["hit2"], before, 0.0), axis=1, keepdims=True)
        carry = carry + r["count"]
        route = jnp.zeros((half, LANES), F32)
        for slot, val in ((ROUTE_E1, r["e1"]), (ROUTE_E2, r["e2"]), (ROUTE_G1, r["g1"]),
                          (ROUTE_G2, r["g2"]), (ROUTE_R1, r1), (ROUTE_R2, r2)):
            route = jnp.where(lane == slot, val, route)
        route_ref[j * half:(j + 1) * half, :] = route
        route_t_ref[:, j * half:(j + 1) * half] = route.T[:ROUTE_ROWS]
    carry_ref[...] = carry
    cnt_ref[...] = carry


def _post(a, w, h, gain, w_grp, b_grp, w_exp, b_exp):
    T, D = h.shape
    K = a.shape[1]
    tm = min(TOKEN_TILE, T)
    n_r = MOE_GROUPS + N_EXPERTS
    wr = jnp.zeros((D, LANES), F32).at[:, :n_r].set(jnp.concatenate([w_grp, w_exp], axis=1))
    wr_hi = wr.astype(BF16)
    wr_lo = (wr - wr_hi.astype(F32)).astype(BF16)
    wr3 = jnp.concatenate([wr_hi, wr_lo, wr_hi], axis=0)
    br = jnp.zeros((1, LANES), F32).at[0, :n_r].set(jnp.concatenate([b_grp, b_exp]))
    tri = jnp.tril(jnp.ones((tm // 2, tm // 2), BF16), k=-1)
    return pl.pallas_call(
        _post_kernel,
        grid=(T // tm,),
        in_specs=[pl.BlockSpec((tm, K), lambda i: (i, 0)),
                  _resident((K, D), lambda i: (0, 0)),
                  pl.BlockSpec((tm, D), lambda i: (i, 0)),
                  _resident((1, D), lambda i: (0, 0)),
                  _resident((3 * D, LANES), lambda i: (0, 0)),
                  _resident((1, LANES), lambda i: (0, 0)),
                  _resident((tm // 2, tm // 2), lambda i: (0, 0))],
        out_specs=[pl.BlockSpec((tm, D), lambda i: (i, 0))]
                  + [pl.BlockSpec((tm, D // 2 // ROW_PARTS), lambda i: (i, 0))] * ROW_PARTS
                  + [pl.BlockSpec((tm, LANES), lambda i: (i, 0)),
                   pl.BlockSpec((ROUTE_ROWS, tm), lambda i: (0, i)),
                   pl.BlockSpec((1, LANES), lambda i: (0, 0))],
        out_shape=[jax.ShapeDtypeStruct((T, D), F32)]
                  + [jax.ShapeDtypeStruct((T, D // 2 // ROW_PARTS), jnp.uint32)] * ROW_PARTS
                  + [jax.ShapeDtypeStruct((T, LANES), F32),
                   jax.ShapeDtypeStruct((ROUTE_ROWS, T), F32),
                   jax.ShapeDtypeStruct((1, LANES), F32)],
        scratch_shapes=[pltpu.VMEM((1, LANES), F32)],
        compiler_params=_params("arbitrary"),
        name="post_mixer",
    )(a, w, h, gain.reshape(1, D), wr3, br, tri)


SC_WINDOW = 128


def _sc_mesh():
    return plsc.VectorSubcoreMesh(core_axis_name="core", subcore_axis_name="subcore")


def _sc_window_specs(W):
    rows = pl.BlockSpec((SC_WINDOW, W), lambda i: (i, 0))
    idx = pl.BlockSpec((1, SC_WINDOW), lambda i: (0, i))
    return rows, idx


def _sc_pipeline(body, n_rows, in_specs, out_specs):
    return pltpu.emit_pipeline(body, grid=(n_rows // SC_WINDOW,), in_specs=in_specs, out_specs=out_specs,
                               core_axis_name=("core", "subcore"), dimension_semantics=(pltpu.PARALLEL,))


def _dispatch(xs, dest, pad_idx, P):
    T, W = xs[0].shape
    n_pad = pad_idx.shape[1]
    n = len(xs)
    rows, idx = _sc_window_specs(W)
    zero_rows = pl.BlockSpec((SC_WINDOW, W), lambda i: (0, 0))
    out = jax.ShapeDtypeStruct((P, W), xs[0].dtype)

    @functools.partial(pl.kernel, out_type=(out,) * n, mesh=_sc_mesh(), scratch_types=[], name="dispatch")
    def scatter(*refs):
        x_hbm, (d0_hbm, d1_hbm, z_hbm, p_hbm), o_hbm = refs[:n], refs[n:n + 4], refs[n + 4:]
        for x, o in zip(x_hbm, o_hbm):
            def put_pair(x_vmem, i0_vmem, i1_vmem, o=o):
                pltpu.sync_copy(x_vmem, o.at[i0_vmem.at[0]])
                pltpu.sync_copy(x_vmem, o.at[i1_vmem.at[0]])

            def put(x_vmem, i_vmem, o=o):
                pltpu.sync_copy(x_vmem, o.at[i_vmem.at[0]])

            _sc_pipeline(put_pair, T, [rows, idx, idx], [])(x, d0_hbm, d1_hbm)
            _sc_pipeline(put, n_pad, [zero_rows, idx], [])(z_hbm, p_hbm)

    return scatter(*xs, dest[0:1], dest[1:2], jnp.zeros((SC_WINDOW, W), xs[0].dtype), pad_idx)


WEIGHT_LEADS = (3, 2, 1)
EXPERT_SLOTS = max(WEIGHT_LEADS) + 1


def _expert_kernel(blk_exp_ref, slot_ref, n_used_ref, *refs):
    x_refs = refs[:ROW_PARTS]
    w_refs = refs[ROW_PARTS:ROW_PARTS + 3]
    o_refs = refs[ROW_PARTS + 3:2 * ROW_PARTS + 3]
    w_slots = refs[2 * ROW_PARTS + 3:]
    lead = max(WEIGHT_LEADS)
    n_used = n_used_ref[0]
    i = pl.program_id(0) - lead
    D = w_slots[0].shape[1]

    for w_ref, w_s, ahead in zip(w_refs, w_slots, WEIGHT_LEADS):
        j = i + ahead
        jc = jnp.clip(j, 0, n_used - 1)
        arrived = (j >= 0) & (j < n_used) & ((j == 0) | (blk_exp_ref[jc] != blk_exp_ref[jnp.maximum(jc - 1, 0)]))

        @pl.when(arrived)
        def _(w_ref=w_ref, w_s=w_s, jc=jc):
            w_s[slot_ref[jc]] = w_ref[0, 0].astype(BF16)

    @pl.when((i >= 0) & (i < n_used))
    def _():
        slot = slot_ref[jnp.clip(i, 0, n_used - 1)]
        wg_s, wu_s, wd_s = w_slots
        pieces = []
        for part, x_ref in enumerate(x_refs):
            for cols, val in zip(_part_cols(D, part), _unpack_bf16_pair(x_ref[...])):
                pieces.append((cols, val.astype(BF16)))

        def up(w_s):
            return sum(jnp.dot(val, w_s[slot, cols, :], preferred_element_type=F32) for cols, val in pieces)

        g = up(wg_s)
        hid = (g * jax.nn.sigmoid(g) * up(wu_s)).astype(BF16)
        y = jnp.dot(hid, wd_s[slot], preferred_element_type=F32)
        for part, o_ref in enumerate(o_refs):
            o_ref[...] = _pack_part(y, part)

    @pl.when(i >= n_used)
    def _():
        for o_ref in o_refs:
            o_ref[...] = jnp.zeros_like(o_ref)


def _experts(xs, blk_exp, n_used, layer, w_gate, w_up, w_down):
    P, W = xs[0].shape
    D = 2 * W * ROW_PARTS
    FF = w_gate.shape[3]
    nblk = P // MOE_BLOCK
    lead = max(WEIGHT_LEADS)
    changes = jnp.concatenate([jnp.zeros((1,), jnp.int32), (blk_exp[1:] != blk_exp[:-1]).astype(jnp.int32)])
    slot = jnp.cumsum(changes) % EXPERT_SLOTS

    def x_map(g, be, sl, nu):
        return (jnp.clip(g - lead, 0, nu[0] - 1), 0)

    def o_map(g, be, sl, nu):
        return (jnp.maximum(g - lead, 0), 0)

    def w_map(ahead):
        return lambda g, be, sl, nu: (layer, be[jnp.clip(g - lead + ahead, 0, nu[0] - 1)], 0, 0)

    wg_spec, wu_spec, wd_spec = (pl.BlockSpec((1, 1) + shape, w_map(ahead))
                                 for shape, ahead in zip(((D, FF), (D, FF), (FF, D)), WEIGHT_LEADS))
    return pl.pallas_call(
        _expert_kernel,
        grid_spec=pltpu.PrefetchScalarGridSpec(
            num_scalar_prefetch=3,
            grid=(nblk + lead,),
            in_specs=[pl.BlockSpec((MOE_BLOCK, W), x_map)] * ROW_PARTS + [wg_spec, wu_spec, wd_spec],
            out_specs=[pl.BlockSpec((MOE_BLOCK, W), o_map)] * ROW_PARTS,
            scratch_shapes=[pltpu.VMEM((EXPERT_SLOTS, D, FF), BF16), pltpu.VMEM((EXPERT_SLOTS, D, FF), BF16),
                            pltpu.VMEM((EXPERT_SLOTS, FF, D), BF16)]),
        out_shape=[jax.ShapeDtypeStruct((P, W), jnp.uint32)] * ROW_PARTS,
        compiler_params=_params("arbitrary"),
        name="experts",
    )(blk_exp, slot, n_used, *xs, w_gate, w_up, w_down)


def _gather_pairs(ys, dest):
    W = ys[0].shape[1]
    T = dest.shape[1]
    n = len(ys)
    rows, idx = _sc_window_specs(W)
    out = jax.ShapeDtypeStruct((T, W), ys[0].dtype)

    @functools.partial(pl.kernel, out_type=(out,) * (2 * n), mesh=_sc_mesh(), scratch_types=[],
                       name="gather_pairs")
    def gather(*refs):
        y_hbm, d_hbm, o_hbm = refs[:n], refs[n:n + 2], refs[n + 2:]
        for slot, d in enumerate(d_hbm):
            for y, o in zip(y_hbm, o_hbm[slot * n:(slot + 1) * n]):
                def get(i_vmem, o_vmem, y=y):
                    pltpu.sync_copy(y.at[i_vmem.at[0]], o_vmem)

                _sc_pipeline(get, T, [idx], [rows])(d, o)

    return gather(*ys, dest[0:1], dest[1:2])


def _combine_kernel(h_ref, route_ref, *refs):
    y_refs, o_ref = refs[:-1], refs[-1]
    D = h_ref.shape[1]
    route = route_ref[...]
    gates = (route[:, ROUTE_G1:ROUTE_G1 + 1], route[:, ROUTE_G2:ROUTE_G2 + 1])
    for part in range(ROW_PARTS):
        slots = [_unpack_bf16_pair(y_refs[slot * ROW_PARTS + part][...]) for slot in range(2)]
        for half, cols in enumerate(_part_cols(D, part)):
            o_ref[:, cols] = h_ref[:, cols] + (gates[0] * slots[0][half] + gates[1] * slots[1][half])


def _combine(h, route, pairs):
    T, D = h.shape
    W = pairs[0].shape[1]
    tm = min(TOKEN_TILE, T)
    return pl.pallas_call(
        _combine_kernel,
        grid=(T // tm,),
        in_specs=[pl.BlockSpec((tm, D), lambda i: (i, 0)), pl.BlockSpec((tm, LANES), lambda i: (i, 0))]
                 + [pl.BlockSpec((tm, W), lambda i: (i, 0))] * len(pairs),
        out_specs=pl.BlockSpec((tm, D), lambda i: (i, 0)),
        out_shape=jax.ShapeDtypeStruct((T, D), F32),
        compiler_params=_params("parallel"),
        name="combine",
    )(h, route, *pairs)


def _moe(h1, hn, route, route_t, counts, layer, w_gate, w_up, w_down):
    T, D = h1.shape
    A = 2 * T
    nblk = -(-A // MOE_BLOCK) + N_EXPERTS
    P = nblk * MOE_BLOCK
    eid = route_t[ROUTE_E1:ROUTE_E2 + 1].astype(jnp.int32)
    rank = route_t[ROUTE_R1:ROUTE_R2 + 1].astype(jnp.int32)
    cnt = counts[0, :N_EXPERTS].astype(jnp.int32)
    padded = (cnt + MOE_BLOCK - 1) // MOE_BLOCK * MOE_BLOCK
    pends = jnp.cumsum(padded)
    pstarts = pends - padded
    experts = jnp.arange(N_EXPERTS, dtype=jnp.int32)
    start_of = jnp.sum(jnp.where(eid[:, None, :] == experts[None, :, None], pstarts[None, :, None], 0), axis=1)
    dest = start_of + rank
    blk_start = jnp.arange(nblk, dtype=jnp.int32) * MOE_BLOCK
    blk_exp = jnp.minimum(jnp.sum((pends[None, :] <= blk_start[:, None]).astype(jnp.int32), axis=1),
                          N_EXPERTS - 1)
    n_used = pends[-1:] // MOE_BLOCK

    gap_start = jnp.concatenate([pstarts + cnt, pends[-1:]])
    gap_len = jnp.concatenate([padded - cnt, P - pends[-1:]])
    gap_end = jnp.cumsum(gap_len)
    j = jnp.arange(P - A, dtype=jnp.int32)
    gap_of = jnp.sum((gap_end[None, :] <= j[:, None]).astype(jnp.int32), axis=1)
    sel = gap_of[:, None] == jnp.arange(N_EXPERTS + 1, dtype=jnp.int32)[None, :]
    pad_idx = jnp.sum(jnp.where(sel, (gap_start - (gap_end - gap_len))[None, :] + j[:, None], 0), axis=1)

    xs = _dispatch(hn, dest, pad_idx.reshape(1, P - A), P)
    ys = _experts(xs, blk_exp, n_used, layer, w_gate, w_up, w_down)
    return _combine(h1, route, _gather_pairs(ys, dest))


def _qkv_kernel(x_ref, gq_ref, gkv_ref, wq_ref, wk_ref, wvt_ref, qn_ref, kn_ref, seg_ref,
                q_ref, k_ref, vt_ref, xs_ref, *, rate, n, n_chunks):
    seg = seg_ref[...]
    width = seg.shape[0]
    res_per_chunk = q_ref.shape[1]
    chunk = pl.program_id(2)

    def head_norm(t, gain):
        cols = []
        for j in range(t.shape[1] // width):
            tj = t[:, j * width:(j + 1) * width]
            ms = jnp.dot((tj * tj).astype(BF16), seg, preferred_element_type=F32)
            cols.append(tj * lax.rsqrt(ms + EPS))
        return jnp.concatenate(cols, axis=1) * gain

    def project(x):
        y = _rms(x)
        xq = (y * gq_ref[...]).astype(BF16)
        xkv = (y * gkv_ref[...]).astype(BF16)
        q = head_norm(jnp.dot(xq, wq_ref[...], preferred_element_type=F32), qn_ref[...]).astype(q_ref.dtype)
        k = head_norm(jnp.dot(xkv, wk_ref[...], preferred_element_type=F32), kn_ref[...]).astype(k_ref.dtype)
        vt = lax.dot_general(wvt_ref[...], xkv, NT_DIMS, preferred_element_type=F32).astype(vt_ref.dtype)
        for j in range(res_per_chunk):
            q_ref[0, j] = q[j * n:(j + 1) * n]
            k_ref[0, j] = k[j * n:(j + 1) * n]
            vt_ref[0, j] = vt[:, j * n:(j + 1) * n]

    if rate == 1:
        project(x_ref[...])
        return

    @pl.when(chunk == 0)
    def _():
        for j in range(xs_ref.shape[0]):
            xs_ref[j] = x_ref[:, j * LANES:(j + 1) * LANES]

    for ch in range(n_chunks):
        @pl.when(chunk == ch)
        def _(ch=ch):
            residues = range(ch * res_per_chunk, (ch + 1) * res_per_chunk)
            project(jnp.concatenate(
                [jnp.concatenate([xs_ref[j, pl.ds(c, n, stride=rate), :] for j in range(xs_ref.shape[0])], axis=1)
                 for c in residues], axis=0))


def _qkv(h, B, S, rate, gq, gkv, wq, wk, wv, qn, kn):
    D = h.shape[1]
    L = S // rate
    hd = DIL_HEAD_DIM
    n = max(SUB_BLOCK, TOKEN_TILE // rate)
    tm = n * rate
    res_per_chunk = max(1, TOKEN_TILE // n)
    n_chunks = rate // res_per_chunk
    width = 2 * LANES
    ii = jnp.arange(width)
    seg = jnp.where((ii[:, None] // hd) == (ii[None, :] // hd), 1.0 / hd, 0.0).astype(BF16)
    row = lambda g: jnp.tile(g, D // hd).reshape(1, D)
    const = lambda shape: _resident(shape, lambda b, i, c: (0,) * len(shape))
    qk_spec = pl.BlockSpec((1, res_per_chunk, n, D), lambda b, i, c: (b, c, i, 0))
    qk_shape = jax.ShapeDtypeStruct((B, rate, L, D), BF16)
    return pl.pallas_call(
        functools.partial(_qkv_kernel, rate=rate, n=n, n_chunks=n_chunks),
        grid=(B, S // tm, n_chunks),
        in_specs=[pl.BlockSpec((tm, D), lambda b, i, c: (b * (S // tm) + i, 0)),
                  const((1, D)), const((1, D)), const((D, D)), const((D, D)), const((D, D)),
                  const((1, D)), const((1, D)), const((width, width))],
        out_specs=[qk_spec, qk_spec,
                   pl.BlockSpec((1, res_per_chunk, D, n), lambda b, i, c: (b, c, 0, i))],
        out_shape=[qk_shape, qk_shape, jax.ShapeDtypeStruct((B, rate, D, L), BF16)],
        scratch_shapes=[pltpu.VMEM((D // LANES, tm if rate > 1 else 8, LANES), F32)],
        compiler_params=_params("parallel", "parallel", "arbitrary"),
        name="qkv_rate%d" % rate,
    )(h, gq.reshape(1, D), gkv.reshape(1, D), wq, wk, wv.T, row(qn) * (hd ** -0.5 * LOG2E), row(kn), seg)


def _attn_kernel(*refs, rate, with_prev):
    if rate == 1:
        q_ref, kc_ref, kp_ref, vc_ref, vp_ref, edge_ref, bias_ref, o_ref, lse_ref, os_ref = refs
    elif with_prev:
        q_ref, kc_ref, kp_ref, vc_ref, vp_ref, edge_ref, o_ref, lse_ref, os_ref = refs
    else:
        q_ref, kc_ref, vc_ref, edge_ref, o_ref, lse_ref, os_ref = refs
    Bk = SUB_BLOCK
    hd = DIL_HEAD_DIM
    n_pairs = DIL_HEADS // 2
    lane = lax.broadcasted_iota(jnp.int32, (Bk, LANES), 1)

    def block(load_q, load_k, load_vt, load_bias, store):
        out_t, lse_t = [], []
        for p in range(n_pairs):
            qp = load_q(p)
            zero = jnp.zeros_like(qp)
            q2 = jnp.concatenate([jnp.where(lane < hd, qp, zero), jnp.where(lane >= hd, qp, zero)], axis=0)
            s = lax.dot_general(load_k(p), q2, NT_DIMS, preferred_element_type=F32) + load_bias(p)
            m = jnp.max(s, axis=0, keepdims=True)
            pr = jnp.exp2(s - m)
            l = jnp.sum(pr, axis=0, keepdims=True)
            pb = pr.astype(BF16)
            vt = load_vt(p)
            out_t.append(jnp.dot(vt[:hd], pb[:, :Bk], preferred_element_type=F32) / l[:, :Bk])
            out_t.append(jnp.dot(vt[hd:], pb[:, Bk:], preferred_element_type=F32) / l[:, Bk:])
            lse = (m + jnp.log2(l)) * LN2
            lse_t += [lse[:, :Bk], lse[:, Bk:]]
        lse_t.append(jnp.zeros((LANES - DIL_HEADS, Bk), F32))
        store(jnp.concatenate(out_t, axis=0).T, jnp.concatenate(lse_t, axis=0).T)

    def pair_cols(p):
        return slice(p * LANES, (p + 1) * LANES)

    edge_bias = lambda p: edge_ref[0, p]
    if rate == 1:
        for j in range(q_ref.shape[2] // Bk):
            rows = slice(j * Bk, (j + 1) * Bk)
            if j == 0:
                load_k = lambda p: jnp.concatenate([kp_ref[0, 0, :, pair_cols(p)],
                                                    kc_ref[0, 0, :Bk, pair_cols(p)]], axis=0)
                load_vt = lambda p: jnp.concatenate([vp_ref[0, 0, pair_cols(p), :],
                                                     vc_ref[0, 0, pair_cols(p), :Bk]], axis=1)
                load_bias = edge_bias
            else:
                krows = slice((j - 1) * Bk, (j + 1) * Bk)
                load_k = lambda p, krows=krows: kc_ref[0, 0, krows, pair_cols(p)]
                load_vt = lambda p, krows=krows: vc_ref[0, 0, pair_cols(p), krows]
                load_bias = lambda p: bias_ref[p]

            def store(o, lse, rows=rows):
                o_ref[rows, :] = o.astype(o_ref.dtype)
                lse_ref[rows, :] = lse

            block(lambda p, rows=rows: q_ref[0, 0, rows, pair_cols(p)], load_k, load_vt, load_bias, store)
    else:
        for c in range(q_ref.shape[1]):
            if with_prev:
                load_k = lambda p, c=c: jnp.concatenate([kp_ref[0, c, :, pair_cols(p)],
                                                         kc_ref[0, c, :, pair_cols(p)]], axis=0)
                load_vt = lambda p, c=c: jnp.concatenate([vp_ref[0, c, pair_cols(p), :],
                                                          vc_ref[0, c, pair_cols(p), :]], axis=1)
            else:
                load_k = lambda p, c=c: kc_ref[0, c, :, pair_cols(p)]
                load_vt = lambda p, c=c: vc_ref[0, c, pair_cols(p), :]

            def store(o, lse, c=c):
                for j in range(os_ref.shape[0]):
                    os_ref[j, pl.ds(c, Bk, stride=rate), :] = o[:, pair_cols(j)]
                lse_ref[pl.ds(c, Bk, stride=rate), :] = lse

            block(lambda p, c=c: q_ref[0, c, :, pair_cols(p)], load_k, load_vt, edge_bias, store)
        for j in range(os_ref.shape[0]):
            o_ref[:, pair_cols(j)] = os_ref[j].astype(o_ref.dtype)


def _t5_bucket(n):
    max_exact = NUM_BUCKETS // 2
    nf = jnp.maximum(n, max_exact).astype(F32)
    large = max_exact + (jnp.log(nf / max_exact) / math.log(MAX_DISTANCE / max_exact)
                         * (NUM_BUCKETS - max_exact)).astype(jnp.int32)
    large = jnp.minimum(large, NUM_BUCKETS - 1)
    return jnp.where(n < max_exact, n, large)


def _group_attention(q, k, vt, bias_table, rate, n_steps):
    B, _, L, D = q.shape
    S = L * rate
    Bk = SUB_BLOCK
    with_prev = L > Bk
    nk = 2 * Bk if with_prev else Bk
    n = max(Bk, TOKEN_TILE // rate)
    tm = n * rate
    nt = S // tm
    n_pairs = DIL_HEADS // 2
    ql = jnp.arange(Bk, dtype=jnp.int32)[:, None]
    kl = jnp.arange(2 * Bk, dtype=jnp.int32)[None, :]
    steps = ql + Bk - kl
    bucket = _t5_bucket(jnp.maximum(steps, 0) * rate)
    buckets = jnp.arange(NUM_BUCKETS, dtype=jnp.int32)
    bias = jnp.sum(jnp.where(bucket[None, :, :, None] == buckets[:, None, None, None],
                             bias_table.astype(F32)[:, None, None, :], 0.0), axis=0)
    band = ((steps >= 0) & (steps <= n_steps))[:, :, None]
    first = (kl < Bk)[:, :, None]

    def layout(t):
        t = t[:, 2 * Bk - nk:, :].transpose(1, 2, 0)
        return t.reshape(nk, n_pairs, 2 * Bk).transpose(1, 0, 2)

    bias_in = layout(jnp.where(band, bias * LOG2E, NEG_INF))
    bias_first = layout(jnp.where(band & ~first, bias * LOG2E, NEG_INF))
    edge = jnp.stack([bias_first, bias_in])

    cur_qk = pl.BlockSpec((1, rate, n, D), lambda b, i: (b, 0, i, 0))
    cur_vt = pl.BlockSpec((1, rate, D, n), lambda b, i: (b, 0, 0, i))
    per_n = n // Bk
    prev_qk = pl.BlockSpec((1, rate, Bk, D), lambda b, i: (b, 0, jnp.maximum(i * per_n - 1, 0), 0))
    prev_vt = pl.BlockSpec((1, rate, D, Bk), lambda b, i: (b, 0, 0, jnp.maximum(i * per_n - 1, 0)))
    edge_spec = pl.BlockSpec((1, n_pairs, nk, 2 * Bk), lambda b, i: (jnp.minimum(i, 1), 0, 0, 0))
    if with_prev:
        in_specs = [cur_qk, cur_qk, prev_qk, cur_vt, prev_vt, edge_spec]
        args = (q, k, k, vt, vt, edge)
    else:
        in_specs = [cur_qk, cur_qk, cur_vt, edge_spec]
        args = (q, k, vt, edge)
    if rate == 1:
        in_specs.append(_resident((n_pairs, nk, 2 * Bk), lambda b, i: (0, 0, 0)))
        args += (bias_in,)
    return pl.pallas_call(
        functools.partial(_attn_kernel, rate=rate, with_prev=with_prev),
        grid=(B, nt),
        in_specs=in_specs,
        out_specs=[pl.BlockSpec((tm, D), lambda b, i: (b * nt + i, 0)),
                   pl.BlockSpec((tm, LANES), lambda b, i: (b * nt + i, 0))],
        out_shape=[jax.ShapeDtypeStruct((B * S, D), BF16),
                   jax.ShapeDtypeStruct((B * S, LANES), F32)],
        scratch_shapes=[pltpu.VMEM((D // LANES, tm if rate > 1 else 8, LANES), F32)],
        compiler_params=_params("parallel", "parallel"),
        name="attn_rate%d" % rate,
    )(*args)


def _merge_kernel(o0_ref, o1_ref, o2_ref, l0_ref, l1_ref, l2_ref, ex_ref, out_ref):
    ex = ex_ref[...]

    def expand(w):
        hi = w.astype(BF16)
        lo = (w - hi.astype(F32)).astype(BF16)
        return jnp.dot(hi, ex, preferred_element_type=F32) + jnp.dot(lo, ex, preferred_element_type=F32)

    l0, l1, l2 = l0_ref[...], l1_ref[...], l2_ref[...]
    top = jnp.maximum(jnp.maximum(l0, l1), l2)
    w0, w1, w2 = jnp.exp(l0 - top), jnp.exp(l1 - top), jnp.exp(l2 - top)
    den = w0 + w1 + w2
    acc = expand(w0 / den) * o0_ref[...].astype(F32)
    acc = acc + expand(w1 / den) * o1_ref[...].astype(F32)
    acc = acc + expand(w2 / den) * o2_ref[...].astype(F32)
    out_ref[...] = acc.astype(out_ref.dtype)


def _merge(outs, lses):
    T, D = outs[0].shape
    tm = min(TOKEN_TILE, T)
    ex = jnp.where(jnp.arange(LANES)[:, None] == (jnp.arange(D)[None, :] // DIL_HEAD_DIM), 1.0, 0.0).astype(BF16)
    o_spec = pl.BlockSpec((tm, D), lambda i: (i, 0))
    l_spec = pl.BlockSpec((tm, LANES), lambda i: (i, 0))
    return pl.pallas_call(
        _merge_kernel,
        grid=(T // tm,),
        in_specs=[o_spec] * 3 + [l_spec] * 3 + [_resident((LANES, D), lambda i: (0, 0))],
        out_specs=o_spec,
        out_shape=jax.ShapeDtypeStruct((T, D), BF16),
        compiler_params=_params("parallel"),
        name="merge_groups",
    )(*outs, *lses, ex)


def kernel(x, ret_w_in, ret_w_out, kv_norm, w_kv, k_norm, dil_wq, q_norm, dil_wo, rel_bias,
           mixer_norm, ffn_norm, router_grp, router_grp_b, router_exp, router_exp_b,
           exp_gate, exp_up, exp_down):
    B, S, D = x.shape
    h = x.reshape(B * S, D)

    def moe_layer(layer, a, w_out, h):
        h1, *hn, route, route_t, counts = _post(
            a, w_out.astype(BF16), h, ffn_norm[layer], router_grp[layer], router_grp_b[layer],
            router_exp[layer], router_exp_b[layer])
        return _moe(h1, hn, route, route_t, counts, layer, exp_gate, exp_up, exp_down)

    proj = _norm_proj(h, mixer_norm[0], ret_w_in[0].astype(BF16))
    y = _retention(proj, B, S)
    h = moe_layer(0, y, ret_w_out[0], h)

    G = len(DIL_RATES)
    gd = DIL_HEADS * DIL_HEAD_DIM
    outs, lses = [], []
    for g in range(G):
        cq = slice(g * gd, (g + 1) * gd)
        cv = slice(G * gd + g * gd, G * gd + (g + 1) * gd)
        q, k, vt = _qkv(h, B, S, DIL_RATES[g], mixer_norm[1], kv_norm,
                        dil_wq[0][:, cq].astype(BF16), w_kv[:, cq].astype(BF16), w_kv[:, cv].astype(BF16),
                        q_norm[0][g], k_norm[g])
        o, lse = _group_attention(q, k, vt, rel_bias[:, g * DIL_HEADS:(g + 1) * DIL_HEADS],
                                  DIL_RATES[g], DIL_WINDOWS[g] // DIL_RATES[g])
        outs.append(o)
        lses.append(lse)
    att = _merge(outs, lses)
    h = moe_layer(1, att, dil_wo[0], h)
    return h.reshape(B, S, D)
```

```python
import functools
import math

import jax
import jax.numpy as jnp
from jax import lax
from jax.experimental import pallas as pl
from jax.experimental.pallas import tpu as pltpu
from jax.experimental.pallas import tpu_sc as plsc

F32 = jnp.float32
BF16 = jnp.bfloat16

EPS = 1e-6
NEG_INF = -1e30

RET_HEADS = 4
RET_CHUNK = 128
ROPE_BASE = 10000.0

DIL_WINDOWS = (128, 512, 2048)
DIL_RATES = (1, 4, 16)
DIL_HEADS = 16
DIL_HEAD_DIM = 64
SUB_BLOCK = 128
NUM_BUCKETS = 32
MAX_DISTANCE = 2048

MOE_GROUPS = 4
EXPERTS_PER_GROUP = 8
N_EXPERTS = MOE_GROUPS * EXPERTS_PER_GROUP
MOE_BLOCK = 1024

LANES = 128
ROW_TILE = 1024
TOKEN_TILE = 512
VMEM_LIMIT = 56 * 1024 * 1024

NT_DIMS = (((1,), (1,)), ((), ()))
LOG2E = math.log2(math.e)
LN2 = math.log(2.0)


def _params(*sem):
    return pltpu.CompilerParams(dimension_semantics=sem, vmem_limit_bytes=VMEM_LIMIT)


def _resident(shape, index_map):
    return pl.BlockSpec(shape, index_map, pipeline_mode=pl.Buffered(1))


def _rms(x):
    return x * lax.rsqrt(jnp.mean(x * x, axis=-1, keepdims=True) + EPS)


def _norm_proj_kernel(x_ref, g_ref, w_ref, o_ref, *, col_tile):
    xn = (_rms(x_ref[...]) * g_ref[...]).astype(BF16)
    for j in range(w_ref.shape[1] // col_tile):
        cols = slice(j * col_tile, (j + 1) * col_tile)
        o_ref[:, cols] = jnp.dot(xn, w_ref[:, cols], preferred_element_type=F32).astype(o_ref.dtype)


def _norm_proj(x, gain, w):
    T, D = x.shape
    N = w.shape[1]
    tm = min(ROW_TILE, T)
    return pl.pallas_call(
        functools.partial(_norm_proj_kernel, col_tile=512),
        grid=(T // tm,),
        in_specs=[pl.BlockSpec((tm, D), lambda i: (i, 0)),
                  _resident((1, D), lambda i: (0, 0)),
                  _resident((D, N), lambda i: (0, 0))],
        out_specs=pl.BlockSpec((tm, N), lambda i: (i, 0)),
        out_shape=jax.ShapeDtypeStruct((T, N), BF16),
        compiler_params=_params("parallel"),
        name="norm_proj",
    )(x, gain.reshape(1, D), w)


def _retention_kernel(q_ref, k_ref, v_ref, gate_ref, cos_ref, sin_ref, din_ref, xi_ref, zeta_ref,
                      cd_ref, o_ref, state_ref, *, k_scale):
    C = RET_CHUNK
    heads = state_ref.shape[0]
    dk = q_ref.shape[2] // heads
    dv = v_ref.shape[2] // heads
    half = dk // 2
    ts = q_ref.shape[1]
    tile = pl.program_id(1)

    @pl.when(tile == 0)
    def _():
        state_ref[...] = jnp.zeros_like(state_ref)

    def rot(t, cos, sin):
        t1, t2 = t[:, :half], t[:, half:]
        return jnp.concatenate([t1 * cos - t2 * sin, t1 * sin + t2 * cos], axis=-1)

    for c in range(ts // C):
        rows = slice(c * C, (c + 1) * C)
        pos = pl.ds(pl.multiple_of(tile * ts + c * C, C), C)
        cos = cos_ref[pos, :]
        sin = sin_ref[pos, :]
        for hh in range(heads):
            qk_cols = slice(hh * dk, (hh + 1) * dk)
            v_cols = slice(hh * dv, (hh + 1) * dv)
            q = rot(q_ref[0, rows, qk_cols].astype(F32), cos, sin)
            k = rot(k_ref[0, rows, qk_cols].astype(F32), cos, sin) * k_scale
            v = v_ref[0, rows, v_cols]
            qb = q.astype(BF16)
            kb = k.astype(BF16)
            s = lax.dot_general(qb, kb, NT_DIMS, preferred_element_type=F32) * din_ref[hh]
            inner = jnp.dot(s.astype(BF16), v, preferred_element_type=F32)
            state = state_ref[hh]
            cross = jnp.dot(qb, state.astype(BF16), preferred_element_type=F32) * xi_ref[hh]
            kz_t = (k * zeta_ref[hh]).T.astype(BF16)
            state_ref[hh] = state * cd_ref[hh] + jnp.dot(kz_t, v, preferred_element_type=F32)
            o = _rms(inner + cross)
            g = gate_ref[0, rows, v_cols].astype(F32)
            o_ref[0, rows, v_cols] = (g * jax.nn.sigmoid(g) * o).astype(o_ref.dtype)


def _retention(proj, B, S):
    H, C = RET_HEADS, RET_CHUNK
    D = proj.shape[1] // 6
    dk, dv = D // H, 2 * D // H
    half = dk // 2
    ts = min(ROW_TILE, S)
    proj = proj.reshape(B, S, 6 * D)

    pos = jnp.arange(S, dtype=F32)
    inv = 1.0 / (ROPE_BASE ** jnp.linspace(0.0, 1.0, half, dtype=F32))
    ang = pos[:, None] * inv[None, :]
    cos, sin = jnp.cos(ang), jnp.sin(ang)
    log_g = jnp.log(1.0 - 2.0 ** (-5.0 - jnp.arange(H, dtype=F32)))
    idx = jnp.arange(C, dtype=F32)
    diff = idx[:, None] - idx[None, :]
    d_in = jnp.where(diff >= 0, jnp.exp(log_g[:, None, None] * jnp.maximum(diff, 0.0)), 0.0)
    xi = jnp.exp(log_g[:, None] * (idx + 1.0))[:, :, None]
    zeta = jnp.exp(log_g[:, None] * (C - 1.0 - idx))[:, :, None]
    chunk_decay = jnp.exp(log_g * C)[:, None, None]

    wq, wv = H * dk, H * dv
    per_head = lambda shape: _resident((H,) + shape, lambda b, i: (0, 0, 0))
    out = pl.pallas_call(
        functools.partial(_retention_kernel, k_scale=dk ** -0.5),
        grid=(B, S // ts),
        in_specs=[pl.BlockSpec((1, ts, wq), lambda b, i: (b, i, 0)),
                  pl.BlockSpec((1, ts, wq), lambda b, i: (b, i, 1)),
                  pl.BlockSpec((1, ts, wv), lambda b, i: (b, i, 1)),
                  pl.BlockSpec((1, ts, wv), lambda b, i: (b, i, 2)),
                  _resident((S, half), lambda b, i: (0, 0)),
                  _resident((S, half), lambda b, i: (0, 0)),
                  per_head((C, C)), per_head((C, 1)), per_head((C, 1)), per_head((1, 1))],
        out_specs=pl.BlockSpec((1, ts, wv), lambda b, i: (b, i, 0)),
        out_shape=jax.ShapeDtypeStruct((B, S, H * dv), BF16),
        scratch_shapes=[pltpu.VMEM((H, dk, dv), F32)],
        compiler_params=_params("parallel", "arbitrary"),
        name="retention",
    )(proj, proj, proj, proj, cos, sin, d_in, xi, zeta, chunk_decay)
    return out.reshape(B * S, H * dv)


ROUTE_E1, ROUTE_E2, ROUTE_G1, ROUTE_G2, ROUTE_R1, ROUTE_R2 = range(6)
ROUTE_ROWS = 8
ROUTER_EXP_LANE0 = MOE_GROUPS
HI16 = 0xFFFF0000


def _pack_bf16_pair(a, b):
    ua = lax.bitcast_convert_type(a.astype(BF16).astype(F32), jnp.uint32)
    ub = lax.bitcast_convert_type(b.astype(BF16).astype(F32), jnp.uint32)
    return ua | (ub >> 16)


def _unpack_bf16_pair(w):
    a = lax.bitcast_convert_type(w & jnp.uint32(HI16), F32)
    b = lax.bitcast_convert_type(w << 16, F32)
    return a, b


ROW_PARTS = 2


def _part_cols(D, part):
    w = D // 2 // ROW_PARTS
    return slice(part * w, (part + 1) * w), slice(D // 2 + part * w, D // 2 + (part + 1) * w)


def _pack_part(x, part):
    hi, lo = _part_cols(x.shape[1], part)
    return _pack_bf16_pair(x[:, hi], x[:, lo])


def _post_kernel(a_ref, w_ref, h_ref, g_ref, wr_ref, br_ref, tri_ref,
                 h1_ref, hn0_ref, hn1_ref, route_ref, route_t_ref, cnt_ref, carry_ref):
    hn_refs = (hn0_ref, hn1_ref)
    @pl.when(pl.program_id(0) == 0)
    def _():
        carry_ref[...] = jnp.zeros_like(carry_ref)

    half = h_ref.shape[0] // 2
    D = h_ref.shape[1]
    lane = lax.broadcasted_iota(jnp.int32, (half, LANES), 1).astype(F32)
    ninf = -jnp.inf

    def first_argmax(vals):
        top = jnp.max(vals, axis=1, keepdims=True)
        where = jnp.min(jnp.where(vals == top, lane, float(LANES)), axis=1, keepdims=True)
        return top, where

    h1_ref[...] = h_ref[...] + jnp.dot(a_ref[...], w_ref[...], preferred_element_type=F32)

    def route_half(rows):
        hn = _rms(h1_ref[rows, :]) * g_ref[...]
        for part, ref in enumerate(hn_refs):
            ref[rows, :] = _pack_part(hn, part)
        hi = hn.astype(BF16)
        lo = (hn - hi.astype(F32)).astype(BF16)
        logits = jnp.dot(jnp.concatenate([hi, hi, lo], axis=1), wr_ref[...],
                         preferred_element_type=F32) + br_ref[...]

        is_grp = lane < MOE_GROUPS
        lg = jnp.where(is_grp, logits, ninf)
        mg, grp = first_argmax(lg)
        p_grp = 1.0 / jnp.sum(jnp.where(is_grp, jnp.exp(lg - mg), 0.0), axis=1, keepdims=True)

        e_lane = lane - ROUTER_EXP_LANE0
        in_grp = (e_lane < N_EXPERTS) & (jnp.floor(e_lane * (1.0 / EXPERTS_PER_GROUP)) == grp)
        le = jnp.where(in_grp, logits, ninf)
        v1, i1 = first_argmax(le)
        le2 = jnp.where(lane == i1, ninf, le)
        v2, i2 = first_argmax(le2)
        e = jnp.exp(v2 - v1)
        hit1 = lane == (i1 - ROUTER_EXP_LANE0)
        hit2 = lane == (i2 - ROUTER_EXP_LANE0)
        onehot = jnp.where(hit1 | hit2, 1.0, 0.0)
        earlier = jnp.dot(tri_ref[...], onehot.astype(BF16), preferred_element_type=F32)
        return dict(e1=i1 - ROUTER_EXP_LANE0, e2=i2 - ROUTER_EXP_LANE0, g1=p_grp / (1.0 + e),
                    g2=p_grp * e / (1.0 + e), hit1=hit1, hit2=hit2, earlier=earlier,
                    count=jnp.sum(onehot, axis=0, keepdims=True))

    halves = [route_half(slice(0, half)), route_half(slice(half, 2 * half))]
    carry = carry_ref[...]
    for j, r in enumerate(halves):
        before = carry + r["earlier"]
        r1 = jnp.sum(jnp.where(r["hit1"], before, 0.0), axis=1, keepdims=True)
        r2 = jnp.sum(jnp.where(r["hit2"], before, 0.0), axis=1, keepdims=True)
        carry = carry + r["count"]
        route = jnp.zeros((half, LANES), F32)
        for slot, val in ((ROUTE_E1, r["e1"]), (ROUTE_E2, r["e2"]), (ROUTE_G1, r["g1"]),
                          (ROUTE_G2, r["g2"]), (ROUTE_R1, r1), (ROUTE_R2, r2)):
            route = jnp.where(lane == slot, val, route)
        route_ref[j * half:(j + 1) * half, :] = route
        route_t_ref[:, j * half:(j + 1) * half] = route.T[:ROUTE_ROWS]
    carry_ref[...] = carry
    cnt_ref[...] = carry


def _post(a, w, h, gain, w_grp, b_grp, w_exp, b_exp):
    T, D = h.shape
    K = a.shape[1]
    tm = min(ROW_TILE, T)
    n_r = MOE_GROUPS + N_EXPERTS
    wr = jnp.zeros((D, LANES), F32).at[:, :n_r].set(jnp.concatenate([w_grp, w_exp], axis=1))
    wr_hi = wr.astype(BF16)
    wr_lo = (wr - wr_hi.astype(F32)).astype(BF16)
    wr3 = jnp.concatenate([wr_hi, wr_lo, wr_hi], axis=0)
    br = jnp.zeros((1, LANES), F32).at[0, :n_r].set(jnp.concatenate([b_grp, b_exp]))
    tri = jnp.tril(jnp.ones((tm // 2, tm // 2), BF16), k=-1)
    return pl.pallas_call(
        _post_kernel,
        grid=(T // tm,),
        in_specs=[pl.BlockSpec((tm, K), lambda i: (i, 0)),
                  _resident((K, D), lambda i: (0, 0)),
                  pl.BlockSpec((tm, D), lambda i: (i, 0)),
                  _resident((1, D), lambda i: (0, 0)),
                  _resident((3 * D, LANES), lambda i: (0, 0)),
                  _resident((1, LANES), lambda i: (0, 0)),
                  _resident((tm // 2, tm // 2), lambda i: (0, 0))],
        out_specs=[pl.BlockSpec((tm, D), lambda i: (i, 0))]
                  + [pl.BlockSpec((tm, D // 2 // ROW_PARTS), lambda i: (i, 0))] * ROW_PARTS
                  + [pl.BlockSpec((tm, LANES), lambda i: (i, 0)),
                   pl.BlockSpec((ROUTE_ROWS, tm), lambda i: (0, i)),
                   pl.BlockSpec((1, LANES), lambda i: (0, 0))],
        out_shape=[jax.ShapeDtypeStruct((T, D), F32)]
                  + [jax.ShapeDtypeStruct((T, D // 2 // ROW_PARTS), jnp.uint32)] * ROW_PARTS
                  + [jax.ShapeDtypeStruct((T, LANES), F32),
                   jax.ShapeDtypeStruct((ROUTE_ROWS, T), F32),
                   jax.ShapeDtypeStruct((1, LANES), F32)],
        scratch_shapes=[pltpu.VMEM((1, LANES), F32)],
        compiler_params=_params("arbitrary"),
        name="post_mixer",
    )(a, w, h, gain.reshape(1, D), wr3, br, tri)


SC_WINDOW = 128


def _sc_mesh():
    return plsc.VectorSubcoreMesh(core_axis_name="core", subcore_axis_name="subcore")


def _sc_window_specs(W):
    rows = pl.BlockSpec((SC_WINDOW, W), lambda i: (i, 0))
    idx = pl.BlockSpec((1, SC_WINDOW), lambda i: (0, i))
    return rows, idx


def _sc_pipeline(body, n_rows, in_specs, out_specs):
    return pltpu.emit_pipeline(body, grid=(n_rows // SC_WINDOW,), in_specs=in_specs, out_specs=out_specs,
                               core_axis_name=("core", "subcore"), dimension_semantics=(pltpu.PARALLEL,))


def _dispatch(xs, dest, pad_idx, P):
    T, W = xs[0].shape
    n_pad = pad_idx.shape[1]
    n = len(xs)
    rows, idx = _sc_window_specs(W)
    zero_rows = pl.BlockSpec((SC_WINDOW, W), lambda i: (0, 0))
    out = jax.ShapeDtypeStruct((P, W), xs[0].dtype)

    @functools.partial(pl.kernel, out_type=(out,) * n, mesh=_sc_mesh(), scratch_types=[], name="dispatch")
    def scatter(*refs):
        x_hbm, (d0_hbm, d1_hbm, z_hbm, p_hbm), o_hbm = refs[:n], refs[n:n + 4], refs[n + 4:]
        for x, o in zip(x_hbm, o_hbm):
            def put_pair(x_vmem, i0_vmem, i1_vmem, o=o):
                pltpu.sync_copy(x_vmem, o.at[i0_vmem.at[0]])
                pltpu.sync_copy(x_vmem, o.at[i1_vmem.at[0]])

            def put(x_vmem, i_vmem, o=o):
                pltpu.sync_copy(x_vmem, o.at[i_vmem.at[0]])

            _sc_pipeline(put_pair, T, [rows, idx, idx], [])(x, d0_hbm, d1_hbm)
            _sc_pipeline(put, n_pad, [zero_rows, idx], [])(z_hbm, p_hbm)

    return scatter(*xs, dest[0:1], dest[1:2], jnp.zeros((SC_WINDOW, W), xs[0].dtype), pad_idx)


WEIGHT_LEADS = (3, 2, 1)
EXPERT_SLOTS = max(WEIGHT_LEADS) + 1


def _expert_kernel(blk_exp_ref, slot_ref, n_used_ref, *refs):
    x_refs = refs[:ROW_PARTS]
    w_refs = refs[ROW_PARTS:ROW_PARTS + 3]
    o_refs = refs[ROW_PARTS + 3:2 * ROW_PARTS + 3]
    w_slots = refs[2 * ROW_PARTS + 3:]
    lead = max(WEIGHT_LEADS)
    n_used = n_used_ref[0]
    i = pl.program_id(0) - lead
    D = w_slots[0].shape[1]

    for w_ref, w_s, ahead in zip(w_refs, w_slots, WEIGHT_LEADS):
        j = i + ahead
        jc = jnp.clip(j, 0, n_used - 1)
        arrived = (j >= 0) & (j < n_used) & ((j == 0) | (blk_exp_ref[jc] != blk_exp_ref[jnp.maximum(jc - 1, 0)]))

        @pl.when(arrived)
        def _(w_ref=w_ref, w_s=w_s, jc=jc):
            w_s[slot_ref[jc]] = w_ref[0, 0].astype(BF16)

    @pl.when((i >= 0) & (i < n_used))
    def _():
        slot = slot_ref[jnp.clip(i, 0, n_used - 1)]
        wg_s, wu_s, wd_s = w_slots
        pieces = []
        for part, x_ref in enumerate(x_refs):
            for cols, val in zip(_part_cols(D, part), _unpack_bf16_pair(x_ref[...])):
                pieces.append((cols, val.astype(BF16)))

        def up(w_s):
            return sum(jnp.dot(val, w_s[slot, cols, :], preferred_element_type=F32) for cols, val in pieces)

        g = up(wg_s)
        hid = (g * jax.nn.sigmoid(g) * up(wu_s)).astype(BF16)
        y = jnp.dot(hid, wd_s[slot], preferred_element_type=F32)
        for part, o_ref in enumerate(o_refs):
            o_ref[...] = _pack_part(y, part)

    @pl.when(i >= n_used)
    def _():
        for o_ref in o_refs:
            o_ref[...] = jnp.zeros_like(o_ref)


def _experts(xs, blk_exp, n_used, layer, w_gate, w_up, w_down):
    P, W = xs[0].shape
    D = 2 * W * ROW_PARTS
    FF = w_gate.shape[3]
    nblk = P // MOE_BLOCK
    lead = max(WEIGHT_LEADS)
    changes = jnp.concatenate([jnp.zeros((1,), jnp.int32), (blk_exp[1:] != blk_exp[:-1]).astype(jnp.int32)])
    slot = jnp.cumsum(changes) % EXPERT_SLOTS

    def x_map(g, be, sl, nu):
        return (jnp.clip(g - lead, 0, nu[0] - 1), 0)

    def o_map(g, be, sl, nu):
        return (jnp.maximum(g - lead, 0), 0)

    def w_map(ahead):
        return lambda g, be, sl, nu: (layer, be[jnp.clip(g - lead + ahead, 0, nu[0] - 1)], 0, 0)

    wg_spec, wu_spec, wd_spec = (pl.BlockSpec((1, 1) + shape, w_map(ahead))
                                 for shape, ahead in zip(((D, FF), (D, FF), (FF, D)), WEIGHT_LEADS))
    return pl.pallas_call(
        _expert_kernel,
        grid_spec=pltpu.PrefetchScalarGridSpec(
            num_scalar_prefetch=3,
            grid=(nblk + lead,),
            in_specs=[pl.BlockSpec((MOE_BLOCK, W), x_map)] * ROW_PARTS + [wg_spec, wu_spec, wd_spec],
            out_specs=[pl.BlockSpec((MOE_BLOCK, W), o_map)] * ROW_PARTS,
            scratch_shapes=[pltpu.VMEM((EXPERT_SLOTS, D, FF), BF16), pltpu.VMEM((EXPERT_SLOTS, D, FF), BF16),
                            pltpu.VMEM((EXPERT_SLOTS, FF, D), BF16)]),
        out_shape=[jax.ShapeDtypeStruct((P, W), jnp.uint32)] * ROW_PARTS,
        compiler_params=_params("arbitrary"),
        name="experts",
    )(blk_exp, slot, n_used, *xs, w_gate, w_up, w_down)


def _gather_pairs(ys, dest):
    W = ys[0].shape[1]
    T = dest.shape[1]
    n = len(ys)
    rows, idx = _sc_window_specs(W)
    out = jax.ShapeDtypeStruct((T, W), ys[0].dtype)

    @functools.partial(pl.kernel, out_type=(out,) * (2 * n), mesh=_sc_mesh(), scratch_types=[],
                       name="gather_pairs")
    def gather(*refs):
        y_hbm, d_hbm, o_hbm = refs[:n], refs[n:n + 2], refs[n + 2:]
        for slot, d in enumerate(d_hbm):
            for y, o in zip(y_hbm, o_hbm[slot * n:(slot + 1) * n]):
                def get(i_vmem, o_vmem, y=y):
                    pltpu.sync_copy(y.at[i_vmem.at[0]], o_vmem)

                _sc_pipeline(get, T, [idx], [rows])(d, o)

    return gather(*ys, dest[0:1], dest[1:2])


def _combine_kernel(h_ref, route_ref, *refs):
    y_refs, o_ref = refs[:-1], refs[-1]
    D = h_ref.shape[1]
    route = route_ref[...]
    gates = (route[:, ROUTE_G1:ROUTE_G1 + 1], route[:, ROUTE_G2:ROUTE_G2 + 1])
    for part in range(ROW_PARTS):
        slots = [_unpack_bf16_pair(y_refs[slot * ROW_PARTS + part][...]) for slot in range(2)]
        for half, cols in enumerate(_part_cols(D, part)):
            o_ref[:, cols] = h_ref[:, cols] + (gates[0] * slots[0][half] + gates[1] * slots[1][half])


def _combine(h, route, pairs):
    T, D = h.shape
    W = pairs[0].shape[1]
    tm = min(ROW_TILE, T)
    return pl.pallas_call(
        _combine_kernel,
        grid=(T // tm,),
        in_specs=[pl.BlockSpec((tm, D), lambda i: (i, 0)), pl.BlockSpec((tm, LANES), lambda i: (i, 0))]
                 + [pl.BlockSpec((tm, W), lambda i: (i, 0))] * len(pairs),
        out_specs=pl.BlockSpec((tm, D), lambda i: (i, 0)),
        out_shape=jax.ShapeDtypeStruct((T, D), F32),
        compiler_params=_params("parallel"),
        name="combine",
    )(h, route, *pairs)


def _moe(h1, hn, route, route_t, counts, layer, w_gate, w_up, w_down):
    T, D = h1.shape
    A = 2 * T
    nblk = -(-A // MOE_BLOCK) + N_EXPERTS
    P = nblk * MOE_BLOCK
    eid = route_t[ROUTE_E1:ROUTE_E2 + 1].astype(jnp.int32)
    rank = route_t[ROUTE_R1:ROUTE_R2 + 1].astype(jnp.int32)
    cnt = counts[0, :N_EXPERTS].astype(jnp.int32)
    padded = (cnt + MOE_BLOCK - 1) // MOE_BLOCK * MOE_BLOCK
    pends = jnp.cumsum(padded)
    pstarts = pends - padded
    experts = jnp.arange(N_EXPERTS, dtype=jnp.int32)
    start_of = jnp.sum(jnp.where(eid[:, None, :] == experts[None, :, None], pstarts[None, :, None], 0), axis=1)
    dest = start_of + rank
    blk_start = jnp.arange(nblk, dtype=jnp.int32) * MOE_BLOCK
    blk_exp = jnp.minimum(jnp.sum((pends[None, :] <= blk_start[:, None]).astype(jnp.int32), axis=1),
                          N_EXPERTS - 1)
    n_used = pends[-1:] // MOE_BLOCK

    gap_start = jnp.concatenate([pstarts + cnt, pends[-1:]])
    gap_len = jnp.concatenate([padded - cnt, P - pends[-1:]])
    gap_end = jnp.cumsum(gap_len)
    j = jnp.arange(P - A, dtype=jnp.int32)
    gap_of = jnp.sum((gap_end[None, :] <= j[:, None]).astype(jnp.int32), axis=1)
    sel = gap_of[:, None] == jnp.arange(N_EXPERTS + 1, dtype=jnp.int32)[None, :]
    pad_idx = jnp.sum(jnp.where(sel, (gap_start - (gap_end - gap_len))[None, :] + j[:, None], 0), axis=1)

    xs = _dispatch(hn, dest, pad_idx.reshape(1, P - A), P)
    ys = _experts(xs, blk_exp, n_used, layer, w_gate, w_up, w_down)
    return _combine(h1, route, _gather_pairs(ys, dest))


def _qkv_kernel(x_ref, gq_ref, gkv_ref, wq_ref, wk_ref, wvt_ref, qn_ref, kn_ref, seg_ref,
                q_ref, k_ref, vt_ref, xs_ref, *, rate, n, n_chunks):
    seg = seg_ref[...]
    width = seg.shape[0]
    res_per_chunk = q_ref.shape[1]
    chunk = pl.program_id(2)

    def head_norm(t, gain):
        cols = []
        for j in range(t.shape[1] // width):
            tj = t[:, j * width:(j + 1) * width]
            ms = jnp.dot((tj * tj).astype(BF16), seg, preferred_element_type=F32)
            cols.append(tj * lax.rsqrt(ms + EPS))
        return jnp.concatenate(cols, axis=1) * gain

    def project(x):
        y = _rms(x)
        xq = (y * gq_ref[...]).astype(BF16)
        xkv = (y * gkv_ref[...]).astype(BF16)
        q = head_norm(jnp.dot(xq, wq_ref[...], preferred_element_type=F32), qn_ref[...]).astype(q_ref.dtype)
        k = head_norm(jnp.dot(xkv, wk_ref[...], preferred_element_type=F32), kn_ref[...]).astype(k_ref.dtype)
        vt = lax.dot_general(wvt_ref[...], xkv, NT_DIMS, preferred_element_type=F32).astype(vt_ref.dtype)
        for j in range(res_per_chunk):
            q_ref[0, j] = q[j * n:(j + 1) * n]
            k_ref[0, j] = k[j * n:(j + 1) * n]
            vt_ref[0, j] = vt[:, j * n:(j + 1) * n]

    if rate == 1:
        project(x_ref[...])
        return

    @pl.when(chunk == 0)
    def _():
        for j in range(xs_ref.shape[0]):
            xs_ref[j] = x_ref[:, j * LANES:(j + 1) * LANES]

    for ch in range(n_chunks):
        @pl.when(chunk == ch)
        def _(ch=ch):
            residues = range(ch * res_per_chunk, (ch + 1) * res_per_chunk)
            project(jnp.concatenate(
                [jnp.concatenate([xs_ref[j, pl.ds(c, n, stride=rate), :] for j in range(xs_ref.shape[0])], axis=1)
                 for c in residues], axis=0))


def _qkv(h, B, S, rate, gq, gkv, wq, wk, wv, qn, kn):
    D = h.shape[1]
    L = S // rate
    hd = DIL_HEAD_DIM
    n = max(SUB_BLOCK, TOKEN_TILE // rate)
    tm = n * rate
    res_per_chunk = max(1, TOKEN_TILE // n)
    n_chunks = rate // res_per_chunk
    width = 2 * LANES
    ii = jnp.arange(width)
    seg = jnp.where((ii[:, None] // hd) == (ii[None, :] // hd), 1.0 / hd, 0.0).astype(BF16)
    row = lambda g: jnp.tile(g, D // hd).reshape(1, D)
    const = lambda shape: _resident(shape, lambda b, i, c: (0,) * len(shape))
    qk_spec = pl.BlockSpec((1, res_per_chunk, n, D), lambda b, i, c: (b, c, i, 0))
    qk_shape = jax.ShapeDtypeStruct((B, rate, L, D), BF16)
    return pl.pallas_call(
        functools.partial(_qkv_kernel, rate=rate, n=n, n_chunks=n_chunks),
        grid=(B, S // tm, n_chunks),
        in_specs=[pl.BlockSpec((tm, D), lambda b, i, c: (b * (S // tm) + i, 0)),
                  const((1, D)), const((1, D)), const((D, D)), const((D, D)), const((D, D)),
                  const((1, D)), const((1, D)), const((width, width))],
        out_specs=[qk_spec, qk_spec,
                   pl.BlockSpec((1, res_per_chunk, D, n), lambda b, i, c: (b, c, 0, i))],
        out_shape=[qk_shape, qk_shape, jax.ShapeDtypeStruct((B, rate, D, L), BF16)],
        scratch_shapes=[pltpu.VMEM((D // LANES, tm if rate > 1 else 8, LANES), F32)],
        compiler_params=_params("parallel", "parallel", "arbitrary"),
        name="qkv_rate%d" % rate,
    )(h, gq.reshape(1, D), gkv.reshape(1, D), wq, wk, wv.T, row(qn) * (hd ** -0.5 * LOG2E), row(kn), seg)


def _attn_kernel(*refs, rate, with_prev):
    if rate == 1:
        q_ref, kc_ref, kp_ref, vc_ref, vp_ref, edge_ref, bias_ref, o_ref, lse_ref, os_ref = refs
    elif with_prev:
        q_ref, kc_ref, kp_ref, vc_ref, vp_ref, edge_ref, o_ref, lse_ref, os_ref = refs
    else:
        q_ref, kc_ref, vc_ref, edge_ref, o_ref, lse_ref, os_ref = refs
    Bk = SUB_BLOCK
    hd = DIL_HEAD_DIM
    n_pairs = DIL_HEADS // 2
    lane = lax.broadcasted_iota(jnp.int32, (Bk, LANES), 1)

    def block(load_q, load_k, load_vt, load_bias, store):
        out_t, lse_t = [], []
        for p in range(n_pairs):
            qp = load_q(p)
            zero = jnp.zeros_like(qp)
            q2 = jnp.concatenate([jnp.where(lane < hd, qp, zero), jnp.where(lane >= hd, qp, zero)], axis=0)
            s = lax.dot_general(load_k(p), q2, NT_DIMS, preferred_element_type=F32) + load_bias(p)
            m = jnp.max(s, axis=0, keepdims=True)
            pr = jnp.exp2(s - m)
            l = jnp.sum(pr, axis=0, keepdims=True)
            pb = pr.astype(BF16)
            vt = load_vt(p)
            out_t.append(jnp.dot(vt[:hd], pb[:, :Bk], preferred_element_type=F32) / l[:, :Bk])
            out_t.append(jnp.dot(vt[hd:], pb[:, Bk:], preferred_element_type=F32) / l[:, Bk:])
            lse = (m + jnp.log2(l)) * LN2
            lse_t += [lse[:, :Bk], lse[:, Bk:]]
        lse_t.append(jnp.zeros((LANES - DIL_HEADS, Bk), F32))
        store(jnp.concatenate(out_t, axis=0).T, jnp.concatenate(lse_t, axis=0).T)

    def pair_cols(p):
        return slice(p * LANES, (p + 1) * LANES)

    edge_bias = lambda p: edge_ref[0, p]
    if rate == 1:
        for j in range(q_ref.shape[2] // Bk):
            rows = slice(j * Bk, (j + 1) * Bk)
            if j == 0:
                load_k = lambda p: jnp.concatenate([kp_ref[0, 0, :, pair_cols(p)],
                                                    kc_ref[0, 0, :Bk, pair_cols(p)]], axis=0)
                load_vt = lambda p: jnp.concatenate([vp_ref[0, 0, pair_cols(p), :],
                                                     vc_ref[0, 0, pair_cols(p), :Bk]], axis=1)
                load_bias = edge_bias
            else:
                krows = slice((j - 1) * Bk, (j + 1) * Bk)
                load_k = lambda p, krows=krows: kc_ref[0, 0, krows, pair_cols(p)]
                load_vt = lambda p, krows=krows: vc_ref[0, 0, pair_cols(p), krows]
                load_bias = lambda p: bias_ref[p]

            def store(o, lse, rows=rows):
                o_ref[rows, :] = o.astype(o_ref.dtype)
                lse_ref[rows, :] = lse

            block(lambda p, rows=rows: q_ref[0, 0, rows, pair_cols(p)], load_k, load_vt, load_bias, store)
    else:
        for c in range(q_ref.shape[1]):
            if with_prev:
                load_k = lambda p, c=c: jnp.concatenate([kp_ref[0, c, :, pair_cols(p)],
                                                         kc_ref[0, c, :, pair_cols(p)]], axis=0)
                load_vt = lambda p, c=c: jnp.concatenate([vp_ref[0, c, pair_cols(p), :],
                                                          vc_ref[0, c, pair_cols(p), :]], axis=1)
            else:
                load_k = lambda p, c=c: kc_ref[0, c, :, pair_cols(p)]
                load_vt = lambda p, c=c: vc_ref[0, c, pair_cols(p), :]

            def store(o, lse, c=c):
                for j in range(os_ref.shape[0]):
                    os_ref[j, pl.ds(c, Bk, stride=rate), :] = o[:, pair_cols(j)]
                lse_ref[pl.ds(c, Bk, stride=rate), :] = lse

            block(lambda p, c=c: q_ref[0, c, :, pair_cols(p)], load_k, load_vt, edge_bias, store)
        for j in range(os_ref.shape[0]):
            o_ref[:, pair_cols(j)] = os_ref[j].astype(o_ref.dtype)


def _t5_bucket(n):
    max_exact = NUM_BUCKETS // 2
    nf = jnp.maximum(n, max_exact).astype(F32)
    large = max_exact + (jnp.log(nf / max_exact) / math.log(MAX_DISTANCE / max_exact)
                         * (NUM_BUCKETS - max_exact)).astype(jnp.int32)
    large = jnp.minimum(large, NUM_BUCKETS - 1)
    return jnp.where(n < max_exact, n, large)


def _group_attention(q, k, vt, bias_table, rate, n_steps):
    B, _, L, D = q.shape
    S = L * rate
    Bk = SUB_BLOCK
    with_prev = L > Bk
    nk = 2 * Bk if with_prev else Bk
    n = max(Bk, TOKEN_TILE // rate)
    tm = n * rate
    nt = S // tm
    n_pairs = DIL_HEADS // 2
    ql = jnp.arange(Bk, dtype=jnp.int32)[:, None]
    kl = jnp.arange(2 * Bk, dtype=jnp.int32)[None, :]
    steps = ql + Bk - kl
    bucket = _t5_bucket(jnp.maximum(steps, 0) * rate)
    buckets = jnp.arange(NUM_BUCKETS, dtype=jnp.int32)
    bias = jnp.sum(jnp.where(bucket[None, :, :, None] == buckets[:, None, None, None],
                             bias_table.astype(F32)[:, None, None, :], 0.0), axis=0)
    band = ((steps >= 0) & (steps <= n_steps))[:, :, None]
    first = (kl < Bk)[:, :, None]

    def layout(t):
        t = t[:, 2 * Bk - nk:, :].transpose(1, 2, 0)
        return t.reshape(nk, n_pairs, 2 * Bk).transpose(1, 0, 2)

    bias_in = layout(jnp.where(band, bias * LOG2E, NEG_INF))
    bias_first = layout(jnp.where(band & ~first, bias * LOG2E, NEG_INF))
    edge = jnp.stack([bias_first, bias_in])

    cur_qk = pl.BlockSpec((1, rate, n, D), lambda b, i: (b, 0, i, 0))
    cur_vt = pl.BlockSpec((1, rate, D, n), lambda b, i: (b, 0, 0, i))
    per_n = n // Bk
    prev_qk = pl.BlockSpec((1, rate, Bk, D), lambda b, i: (b, 0, jnp.maximum(i * per_n - 1, 0), 0))
    prev_vt = pl.BlockSpec((1, rate, D, Bk), lambda b, i: (b, 0, 0, jnp.maximum(i * per_n - 1, 0)))
    edge_spec = pl.BlockSpec((1, n_pairs, nk, 2 * Bk), lambda b, i: (jnp.minimum(i, 1), 0, 0, 0))
    if with_prev:
        in_specs = [cur_qk, cur_qk, prev_qk, cur_vt, prev_vt, edge_spec]
        args = (q, k, k, vt, vt, edge)
    else:
        in_specs = [cur_qk, cur_qk, cur_vt, edge_spec]
        args = (q, k, vt, edge)
    if rate == 1:
        in_specs.append(_resident((n_pairs, nk, 2 * Bk), lambda b, i: (0, 0, 0)))
        args += (bias_in,)
    return pl.pallas_call(
        functools.partial(_attn_kernel, rate=rate, with_prev=with_prev),
        grid=(B, nt),
        in_specs=in_specs,
        out_specs=[pl.BlockSpec((tm, D), lambda b, i: (b * nt + i, 0)),
                   pl.BlockSpec((tm, LANES), lambda b, i: (b * nt + i, 0))],
        out_shape=[jax.ShapeDtypeStruct((B * S, D), BF16),
                   jax.ShapeDtypeStruct((B * S, LANES), F32)],
        scratch_shapes=[pltpu.VMEM((D // LANES, tm if rate > 1 else 8, LANES), F32)],
        compiler_params=_params("parallel", "parallel"),
        name="attn_rate%d" % rate,
    )(*args)


def _merge_kernel(o0_ref, o1_ref, o2_ref, l0_ref, l1_ref, l2_ref, ex_ref, out_ref):
    ex = ex_ref[...]

    def expand(w):
        hi = w.astype(BF16)
        lo = (w - hi.astype(F32)).astype(BF16)
        return jnp.dot(hi, ex, preferred_element_type=F32) + jnp.dot(lo, ex, preferred_element_type=F32)

    l0, l1, l2 = l0_ref[...], l1_ref[...], l2_ref[...]
    top = jnp.maximum(jnp.maximum(l0, l1), l2)
    w0, w1, w2 = jnp.exp(l0 - top), jnp.exp(l1 - top), jnp.exp(l2 - top)
    den = w0 + w1 + w2
    acc = expand(w0 / den) * o0_ref[...].astype(F32)
    acc = acc + expand(w1 / den) * o1_ref[...].astype(F32)
    acc = acc + expand(w2 / den) * o2_ref[...].astype(F32)
    out_ref[...] = acc.astype(out_ref.dtype)


def _merge(outs, lses):
    T, D = outs[0].shape
    tm = min(ROW_TILE, T)
    ex = jnp.where(jnp.arange(LANES)[:, None] == (jnp.arange(D)[None, :] // DIL_HEAD_DIM), 1.0, 0.0).astype(BF16)
    o_spec = pl.BlockSpec((tm, D), lambda i: (i, 0))
    l_spec = pl.BlockSpec((tm, LANES), lambda i: (i, 0))
    return pl.pallas_call(
        _merge_kernel,
        grid=(T // tm,),
        in_specs=[o_spec] * 3 + [l_spec] * 3 + [_resident((LANES, D), lambda i: (0, 0))],
        out_specs=o_spec,
        out_shape=jax.ShapeDtypeStruct((T, D), BF16),
        compiler_params=_params("parallel"),
        name="merge_groups",
    )(*outs, *lses, ex)


def kernel(x, ret_w_in, ret_w_out, kv_norm, w_kv, k_norm, dil_wq, q_norm, dil_wo, rel_bias,
           mixer_norm, ffn_norm, router_grp, router_grp_b, router_exp, router_exp_b,
           exp_gate, exp_up, exp_down):
    B, S, D = x.shape
    h = x.reshape(B * S, D)

    def moe_layer(layer, a, w_out, h):
        h1, *hn, route, route_t, counts = _post(
            a, w_out.astype(BF16), h, ffn_norm[layer], router_grp[layer], router_grp_b[layer],
            router_exp[layer], router_exp_b[layer])
        return _moe(h1, hn, route, route_t, counts, layer, exp_gate, exp_up, exp_down)

    proj = _norm_proj(h, mixer_norm[0], ret_w_in[0].astype(BF16))
    y = _retention(proj, B, S)
    h = moe_layer(0, y, ret_w_out[0], h)

    G = len(DIL_RATES)
    gd = DIL_HEADS * DIL_HEAD_DIM
    outs, lses = [], []
    for g in range(G):
        cq = slice(g * gd, (g + 1) * gd)
        cv = slice(G * gd + g * gd, G * gd + (g + 1) * gd)
        q, k, vt = _qkv(h, B, S, DIL_RATES[g], mixer_norm[1], kv_norm,
                        dil_wq[0][:, cq].astype(BF16), w_kv[:, cq].astype(BF16), w_kv[:, cv].astype(BF16),
                        q_norm[0][g], k_norm[g])
        o, lse = _group_attention(q, k, vt, rel_bias[:, g * DIL_HEADS:(g + 1) * DIL_HEADS],
                                  DIL_RATES[g], DIL_WINDOWS[g] // DIL_RATES[g])
        outs.append(o)
        lses.append(lse)
    att = _merge(outs, lses)
    h = moe_layer(1, att, dil_wo[0], h)
    return h.reshape(B, S, D)
```

```python
import functools
import math

import jax
import jax.numpy as jnp
from jax import lax
from jax.experimental import pallas as pl
from jax.experimental.pallas import tpu as pltpu
from jax.experimental.pallas import tpu_sc as plsc

F32 = jnp.float32
BF16 = jnp.bfloat16

EPS = 1e-6
NEG_INF = -1e30

RET_HEADS = 4
RET_CHUNK = 128
ROPE_BASE = 10000.0

DIL_WINDOWS = (128, 512, 2048)
DIL_RATES = (1, 4, 16)
DIL_HEADS = 16
DIL_HEAD_DIM = 64
SUB_BLOCK = 128
NUM_BUCKETS = 32
MAX_DISTANCE = 2048

MOE_GROUPS = 4
EXPERTS_PER_GROUP = 8
N_EXPERTS = MOE_GROUPS * EXPERTS_PER_GROUP
MOE_BLOCK = 512

LANES = 128
ROW_TILE = 1024
TOKEN_TILE = 512
VMEM_LIMIT = 56 * 1024 * 1024

NT_DIMS = (((1,), (1,)), ((), ()))
LOG2E = math.log2(math.e)
LN2 = math.log(2.0)


def _params(*sem):
    return pltpu.CompilerParams(dimension_semantics=sem, vmem_limit_bytes=VMEM_LIMIT)


def _resident(shape, index_map):
    return pl.BlockSpec(shape, index_map, pipeline_mode=pl.Buffered(1))


def _rms(x):
    return x * lax.rsqrt(jnp.mean(x * x, axis=-1, keepdims=True) + EPS)


def _ret_proj_kernel(x_ref, g_ref, w_ref, cos_ref, sin_ref, o_ref, *, col_tile, dk, n_qk, n_gate0, k_scale):
    xn = (_rms(x_ref[...]) * g_ref[...]).astype(BF16)
    half = dk // 2
    cos, sin = cos_ref[...], sin_ref[...]
    for j in range(w_ref.shape[1] // col_tile):
        c0 = j * col_tile
        t = jnp.dot(xn, w_ref[:, c0:c0 + col_tile], preferred_element_type=F32)
        if c0 < 2 * n_qk:
            heads = []
            for h0 in range(0, col_tile, dk):
                t1, t2 = t[:, h0:h0 + half], t[:, h0 + half:h0 + dk]
                heads += [t1 * cos - t2 * sin, t1 * sin + t2 * cos]
            t = jnp.concatenate(heads, axis=1)
            if c0 >= n_qk:
                t = t * k_scale
        elif c0 >= n_gate0:
            t = t * jax.nn.sigmoid(t)
        o_ref[:, c0:c0 + col_tile] = t.astype(o_ref.dtype)


def _ret_proj(x, gain, w, S):
    T, D = x.shape
    N = w.shape[1]
    H = RET_HEADS
    dk = D // H
    half = dk // 2
    tm = min(ROW_TILE, S)
    pos = jnp.arange(S, dtype=F32)
    inv = 1.0 / (ROPE_BASE ** jnp.linspace(0.0, 1.0, half, dtype=F32))
    ang = pos[:, None] * inv[None, :]
    table = pl.BlockSpec((tm, half), lambda i: (i % (S // tm), 0))
    return pl.pallas_call(
        functools.partial(_ret_proj_kernel, col_tile=512, dk=dk, n_qk=H * dk, n_gate0=N - 2 * D,
                          k_scale=dk ** -0.5),
        grid=(T // tm,),
        in_specs=[pl.BlockSpec((tm, D), lambda i: (i, 0)),
                  _resident((1, D), lambda i: (0, 0)),
                  _resident((D, N), lambda i: (0, 0)),
                  table, table],
        out_specs=pl.BlockSpec((tm, N), lambda i: (i, 0)),
        out_shape=jax.ShapeDtypeStruct((T, N), BF16),
        compiler_params=_params("parallel"),
        name="norm_proj",
    )(x, gain.reshape(1, D), w, jnp.cos(ang), jnp.sin(ang))


def _retention_kernel(q_ref, k_ref, v_ref, gate_ref, din_ref, xi_ref, zeta_ref, cd_ref, o_ref, state_ref):
    C = RET_CHUNK
    heads = state_ref.shape[0]
    dk = q_ref.shape[2] // heads
    dv = v_ref.shape[2] // heads
    ts = q_ref.shape[1]

    @pl.when(pl.program_id(1) == 0)
    def _():
        state_ref[...] = jnp.zeros_like(state_ref)

    for c in range(ts // C):
        rows = slice(c * C, (c + 1) * C)
        for hh in range(heads):
            qk_cols = slice(hh * dk, (hh + 1) * dk)
            v_cols = slice(hh * dv, (hh + 1) * dv)
            q = q_ref[0, rows, qk_cols]
            k = k_ref[0, rows, qk_cols]
            v = v_ref[0, rows, v_cols]
            s = lax.dot_general(q, k, NT_DIMS, preferred_element_type=F32) * din_ref[hh]
            inner = jnp.dot(s.astype(BF16), v, preferred_element_type=F32)
            state = state_ref[hh]
            cross = jnp.dot(q, state.astype(BF16), preferred_element_type=F32) * xi_ref[hh]
            kz_t = (k.astype(F32) * zeta_ref[hh]).T.astype(BF16)
            state_ref[hh] = state * cd_ref[hh] + jnp.dot(kz_t, v, preferred_element_type=F32)
            o = _rms(inner + cross)
            o_ref[0, rows, v_cols] = (gate_ref[0, rows, v_cols].astype(F32) * o).astype(o_ref.dtype)


def _retention(proj, B, S):
    H, C = RET_HEADS, RET_CHUNK
    D = proj.shape[1] // 6
    dk, dv = D // H, 2 * D // H
    ts = min(ROW_TILE, S)
    proj = proj.reshape(B, S, 6 * D)

    log_g = jnp.log(1.0 - 2.0 ** (-5.0 - jnp.arange(H, dtype=F32)))
    idx = jnp.arange(C, dtype=F32)
    diff = idx[:, None] - idx[None, :]
    d_in = jnp.where(diff >= 0, jnp.exp(log_g[:, None, None] * jnp.maximum(diff, 0.0)), 0.0)
    xi = jnp.exp(log_g[:, None] * (idx + 1.0))[:, :, None]
    zeta = jnp.exp(log_g[:, None] * (C - 1.0 - idx))[:, :, None]
    chunk_decay = jnp.exp(log_g * C)[:, None, None]

    wq, wv = H * dk, H * dv
    per_head = lambda shape: _resident((H,) + shape, lambda b, i: (0, 0, 0))
    out = pl.pallas_call(
        _retention_kernel,
        grid=(B, S // ts),
        in_specs=[pl.BlockSpec((1, ts, wq), lambda b, i: (b, i, 0)),
                  pl.BlockSpec((1, ts, wq), lambda b, i: (b, i, 1)),
                  pl.BlockSpec((1, ts, wv), lambda b, i: (b, i, 1)),
                  pl.BlockSpec((1, ts, wv), lambda b, i: (b, i, 2)),
                  per_head((C, C)), per_head((C, 1)), per_head((C, 1)), per_head((1, 1))],
        out_specs=pl.BlockSpec((1, ts, wv), lambda b, i: (b, i, 0)),
        out_shape=jax.ShapeDtypeStruct((B, S, H * dv), BF16),
        scratch_shapes=[pltpu.VMEM((H, dk, dv), F32)],
        compiler_params=_params("parallel", "arbitrary"),
        name="retention",
    )(proj, proj, proj, proj, d_in, xi, zeta, chunk_decay)
    return out.reshape(B * S, H * dv)


ROUTE_E1, ROUTE_E2, ROUTE_G1, ROUTE_G2, ROUTE_R1, ROUTE_R2 = range(6)
ROUTE_ROWS = 8
ROUTER_EXP_LANE0 = MOE_GROUPS
HI16 = 0xFFFF0000


def _pack_bf16_pair(a, b):
    ua = lax.bitcast_convert_type(a.astype(BF16).astype(F32), jnp.uint32)
    ub = lax.bitcast_convert_type(b.astype(BF16).astype(F32), jnp.uint32)
    return ua | (ub >> 16)


def _unpack_bf16_pair(w):
    a = lax.bitcast_convert_type(w & jnp.uint32(HI16), F32)
    b = lax.bitcast_convert_type(w << 16, F32)
    return a, b


ROW_PARTS = 2


def _part_cols(D, part):
    w = D // 2 // ROW_PARTS
    return slice(part * w, (part + 1) * w), slice(D // 2 + part * w, D // 2 + (part + 1) * w)


def _pack_part(x, part):
    hi, lo = _part_cols(x.shape[1], part)
    return _pack_bf16_pair(x[:, hi], x[:, lo])


def _post_kernel(a_ref, w_ref, h_ref, g_ref, wr_ref, br_ref, tri_ref,
                 h1_ref, hn0_ref, hn1_ref, route_ref, route_t_ref, cnt_ref, carry_ref):
    hn_refs = (hn0_ref, hn1_ref)
    @pl.when(pl.program_id(0) == 0)
    def _():
        carry_ref[...] = jnp.zeros_like(carry_ref)

    half = h_ref.shape[0] // 2
    D = h_ref.shape[1]
    lane = lax.broadcasted_iota(jnp.int32, (half, LANES), 1).astype(F32)
    ninf = -jnp.inf

    def first_argmax(vals):
        top = jnp.max(vals, axis=1, keepdims=True)
        where = jnp.min(jnp.where(vals == top, lane, float(LANES)), axis=1, keepdims=True)
        return top, where

    h1_ref[...] = h_ref[...] + jnp.dot(a_ref[...], w_ref[...], preferred_element_type=F32)

    def route_half(rows):
        hn = _rms(h1_ref[rows, :]) * g_ref[...]
        for part, ref in enumerate(hn_refs):
            ref[rows, :] = _pack_part(hn, part)
        hi = hn.astype(BF16)
        lo = (hn - hi.astype(F32)).astype(BF16)
        logits = jnp.dot(jnp.concatenate([hi, hi, lo], axis=1), wr_ref[...],
                         preferred_element_type=F32) + br_ref[...]

        is_grp = lane < MOE_GROUPS
        lg = jnp.where(is_grp, logits, ninf)
        mg, grp = first_argmax(lg)
        p_grp = 1.0 / jnp.sum(jnp.where(is_grp, jnp.exp(lg - mg), 0.0), axis=1, keepdims=True)

        e_lane = lane - ROUTER_EXP_LANE0
        in_grp = (e_lane < N_EXPERTS) & (jnp.floor(e_lane * (1.0 / EXPERTS_PER_GROUP)) == grp)
        le = jnp.where(in_grp, logits, ninf)
        v1, i1 = first_argmax(le)
        le2 = jnp.where(lane == i1, ninf, le)
        v2, i2 = first_argmax(le2)
        e = jnp.exp(v2 - v1)
        hit1 = lane == (i1 - ROUTER_EXP_LANE0)
        hit2 = lane == (i2 - ROUTER_EXP_LANE0)
        onehot = jnp.where(hit1 | hit2, 1.0, 0.0)
        earlier = jnp.dot(tri_ref[...], onehot.astype(BF16), preferred_element_type=F32)
        return dict(e1=i1 - ROUTER_EXP_LANE0, e2=i2 - ROUTER_EXP_LANE0, g1=p_grp / (1.0 + e),
                    g2=p_grp * e / (1.0 + e), hit1=hit1, hit2=hit2, earlier=earlier,
                    count=jnp.sum(onehot, axis=0, keepdims=True))

    halves = [route_half(slice(0, half)), route_half(slice(half, 2 * half))]
    carry = carry_ref[...]
    for j, r in enumerate(halves):
        before = carry + r["earlier"]
        r1 = jnp.sum(jnp.where(r["hit1"], before, 0.0), axis=1, keepdims=True)
        r2 = jnp.sum(jnp.where(r["hit2"], before, 0.0), axis=1, keepdims=True)
        carry = carry + r["count"]
        route = jnp.zeros((half, LANES), F32)
        for slot, val in ((ROUTE_E1, r["e1"]), (ROUTE_E2, r["e2"]), (ROUTE_G1, r["g1"]),
                          (ROUTE_G2, r["g2"]), (ROUTE_R1, r1), (ROUTE_R2, r2)):
            route = jnp.where(lane == slot, val, route)
        route_ref[j * half:(j + 1) * half, :] = route
        route_t_ref[:, j * half:(j + 1) * half] = route.T[:ROUTE_ROWS]
    carry_ref[...] = carry
    cnt_ref[...] = carry


def _post(a, w, h, gain, w_grp, b_grp, w_exp, b_exp):
    T, D = h.shape
    K = a.shape[1]
    tm = min(TOKEN_TILE, T)
    n_r = MOE_GROUPS + N_EXPERTS
    wr = jnp.zeros((D, LANES), F32).at[:, :n_r].set(jnp.concatenate([w_grp, w_exp], axis=1))
    wr_hi = wr.astype(BF16)
    wr_lo = (wr - wr_hi.astype(F32)).astype(BF16)
    wr3 = jnp.concatenate([wr_hi, wr_lo, wr_hi], axis=0)
    br = jnp.zeros((1, LANES), F32).at[0, :n_r].set(jnp.concatenate([b_grp, b_exp]))
    tri = jnp.tril(jnp.ones((tm // 2, tm // 2), BF16), k=-1)
    return pl.pallas_call(
        _post_kernel,
        grid=(T // tm,),
        in_specs=[pl.BlockSpec((tm, K), lambda i: (i, 0)),
                  _resident((K, D), lambda i: (0, 0)),
                  pl.BlockSpec((tm, D), lambda i: (i, 0)),
                  _resident((1, D), lambda i: (0, 0)),
                  _resident((3 * D, LANES), lambda i: (0, 0)),
                  _resident((1, LANES), lambda i: (0, 0)),
                  _resident((tm // 2, tm // 2), lambda i: (0, 0))],
        out_specs=[pl.BlockSpec((tm, D), lambda i: (i, 0))]
                  + [pl.BlockSpec((tm, D // 2 // ROW_PARTS), lambda i: (i, 0))] * ROW_PARTS
                  + [pl.BlockSpec((tm, LANES), lambda i: (i, 0)),
                   pl.BlockSpec((ROUTE_ROWS, tm), lambda i: (0, i)),
                   pl.BlockSpec((1, LANES), lambda i: (0, 0))],
        out_shape=[jax.ShapeDtypeStruct((T, D), F32)]
                  + [jax.ShapeDtypeStruct((T, D // 2 // ROW_PARTS), jnp.uint32)] * ROW_PARTS
                  + [jax.ShapeDtypeStruct((T, LANES), F32),
                   jax.ShapeDtypeStruct((ROUTE_ROWS, T), F32),
                   jax.ShapeDtypeStruct((1, LANES), F32)],
        scratch_shapes=[pltpu.VMEM((1, LANES), F32)],
        compiler_params=_params("arbitrary"),
        name="post_mixer",
    )(a, w, h, gain.reshape(1, D), wr3, br, tri)


SC_WINDOW = 128


def _sc_mesh():
    return plsc.VectorSubcoreMesh(core_axis_name="core", subcore_axis_name="subcore")


def _sc_window_specs(W):
    rows = pl.BlockSpec((SC_WINDOW, W), lambda i: (i, 0))
    idx = pl.BlockSpec((1, SC_WINDOW), lambda i: (0, i))
    return rows, idx


def _sc_pipeline(body, n_rows, in_specs, out_specs):
    return pltpu.emit_pipeline(body, grid=(n_rows // SC_WINDOW,), in_specs=in_specs, out_specs=out_specs,
                               core_axis_name=("core", "subcore"), dimension_semantics=(pltpu.PARALLEL,))


def _dispatch(xs, dest, pad_idx, P):
    T, W = xs[0].shape
    n_pad = pad_idx.shape[1]
    n = len(xs)
    rows, idx = _sc_window_specs(W)
    zero_rows = pl.BlockSpec((SC_WINDOW, W), lambda i: (0, 0))
    out = jax.ShapeDtypeStruct((P, W), xs[0].dtype)

    @functools.partial(pl.kernel, out_type=(out,) * n, mesh=_sc_mesh(), scratch_types=[], name="dispatch")
    def scatter(*refs):
        x_hbm, (d0_hbm, d1_hbm, z_hbm, p_hbm), o_hbm = refs[:n], refs[n:n + 4], refs[n + 4:]
        for x, o in zip(x_hbm, o_hbm):
            def put_pair(x_vmem, i0_vmem, i1_vmem, o=o):
                pltpu.sync_copy(x_vmem, o.at[i0_vmem.at[0]])
                pltpu.sync_copy(x_vmem, o.at[i1_vmem.at[0]])

            def put(x_vmem, i_vmem, o=o):
                pltpu.sync_copy(x_vmem, o.at[i_vmem.at[0]])

            _sc_pipeline(put_pair, T, [rows, idx, idx], [])(x, d0_hbm, d1_hbm)
            _sc_pipeline(put, n_pad, [zero_rows, idx], [])(z_hbm, p_hbm)

    return scatter(*xs, dest[0:1], dest[1:2], jnp.zeros((SC_WINDOW, W), xs[0].dtype), pad_idx)


WEIGHT_LEADS = (3, 2, 1)
EXPERT_SLOTS = max(WEIGHT_LEADS) + 1


def _expert_kernel(blk_exp_ref, slot_ref, n_used_ref, *refs):
    x_refs = refs[:ROW_PARTS]
    w_refs = refs[ROW_PARTS:ROW_PARTS + 3]
    o_refs = refs[ROW_PARTS + 3:2 * ROW_PARTS + 3]
    w_slots = refs[2 * ROW_PARTS + 3:]
    lead = max(WEIGHT_LEADS)
    n_used = n_used_ref[0]
    i = pl.program_id(0) - lead
    D = w_slots[0].shape[1]

    for w_ref, w_s, ahead in zip(w_refs, w_slots, WEIGHT_LEADS):
        j = i + ahead
        jc = jnp.clip(j, 0, n_used - 1)
        arrived = (j >= 0) & (j < n_used) & ((j == 0) | (blk_exp_ref[jc] != blk_exp_ref[jnp.maximum(jc - 1, 0)]))

        @pl.when(arrived)
        def _(w_ref=w_ref, w_s=w_s, jc=jc):
            w_s[slot_ref[jc]] = w_ref[0, 0].astype(BF16)

    @pl.when((i >= 0) & (i < n_used))
    def _():
        slot = slot_ref[jnp.clip(i, 0, n_used - 1)]
        wg_s, wu_s, wd_s = w_slots
        pieces = []
        for part, x_ref in enumerate(x_refs):
            for cols, val in zip(_part_cols(D, part), _unpack_bf16_pair(x_ref[...])):
                pieces.append((cols, val.astype(BF16)))

        def up(w_s):
            return sum(jnp.dot(val, w_s[slot, cols, :], preferred_element_type=F32) for cols, val in pieces)

        g = up(wg_s)
        hid = (g * jax.nn.sigmoid(g) * up(wu_s)).astype(BF16)
        y = jnp.dot(hid, wd_s[slot], preferred_element_type=F32)
        for part, o_ref in enumerate(o_refs):
            o_ref[...] = _pack_part(y, part)

    @pl.when(i >= n_used)
    def _():
        for o_ref in o_refs:
            o_ref[...] = jnp.zeros_like(o_ref)


def _experts(xs, blk_exp, n_used, layer, w_gate, w_up, w_down):
    P, W = xs[0].shape
    D = 2 * W * ROW_PARTS
    FF = w_gate.shape[3]
    nblk = P // MOE_BLOCK
    lead = max(WEIGHT_LEADS)
    changes = jnp.concatenate([jnp.zeros((1,), jnp.int32), (blk_exp[1:] != blk_exp[:-1]).astype(jnp.int32)])
    slot = jnp.cumsum(changes) % EXPERT_SLOTS

    def x_map(g, be, sl, nu):
        return (jnp.clip(g - lead, 0, nu[0] - 1), 0)

    def o_map(g, be, sl, nu):
        return (jnp.maximum(g - lead, 0), 0)

    def w_map(ahead):
        return lambda g, be, sl, nu: (layer, be[jnp.clip(g - lead + ahead, 0, nu[0] - 1)], 0, 0)

    wg_spec, wu_spec, wd_spec = (pl.BlockSpec((1, 1) + shape, w_map(ahead))
                                 for shape, ahead in zip(((D, FF), (D, FF), (FF, D)), WEIGHT_LEADS))
    return pl.pallas_call(
        _expert_kernel,
        grid_spec=pltpu.PrefetchScalarGridSpec(
            num_scalar_prefetch=3,
            grid=(nblk + lead,),
            in_specs=[pl.BlockSpec((MOE_BLOCK, W), x_map)] * ROW_PARTS + [wg_spec, wu_spec, wd_spec],
            out_specs=[pl.BlockSpec((MOE_BLOCK, W), o_map)] * ROW_PARTS,
            scratch_shapes=[pltpu.VMEM((EXPERT_SLOTS, D, FF), BF16), pltpu.VMEM((EXPERT_SLOTS, D, FF), BF16),
                            pltpu.VMEM((EXPERT_SLOTS, FF, D), BF16)]),
        out_shape=[jax.ShapeDtypeStruct((P, W), jnp.uint32)] * ROW_PARTS,
        compiler_params=_params("arbitrary"),
        name="experts",
    )(blk_exp, slot, n_used, *xs, w_gate, w_up, w_down)


def _gather_pairs(ys, dest):
    W = ys[0].shape[1]
    T = dest.shape[1]
    n = len(ys)
    rows, idx = _sc_window_specs(W)
    out = jax.ShapeDtypeStruct((T, W), ys[0].dtype)

    @functools.partial(pl.kernel, out_type=(out,) * (2 * n), mesh=_sc_mesh(), scratch_types=[],
                       name="gather_pairs")
    def gather(*refs):
        y_hbm, d_hbm, o_hbm = refs[:n], refs[n:n + 2], refs[n + 2:]
        for slot, d in enumerate(d_hbm):
            for y, o in zip(y_hbm, o_hbm[slot * n:(slot + 1) * n]):
                def get(i_vmem, o_vmem, y=y):
                    pltpu.sync_copy(y.at[i_vmem.at[0]], o_vmem)

                _sc_pipeline(get, T, [idx], [rows])(d, o)

    return gather(*ys, dest[0:1], dest[1:2])


def _combine_kernel(h_ref, route_ref, *refs):
    y_refs, o_ref = refs[:-1], refs[-1]
    D = h_ref.shape[1]
    route = route_ref[...]
    gates = (route[:, ROUTE_G1:ROUTE_G1 + 1], route[:, ROUTE_G2:ROUTE_G2 + 1])
    for part in range(ROW_PARTS):
        slots = [_unpack_bf16_pair(y_refs[slot * ROW_PARTS + part][...]) for slot in range(2)]
        for half, cols in enumerate(_part_cols(D, part)):
            o_ref[:, cols] = h_ref[:, cols] + (gates[0] * slots[0][half] + gates[1] * slots[1][half])


def _combine(h, route, pairs):
    T, D = h.shape
    W = pairs[0].shape[1]
    tm = min(ROW_TILE, T)
    return pl.pallas_call(
        _combine_kernel,
        grid=(T // tm,),
        in_specs=[pl.BlockSpec((tm, D), lambda i: (i, 0)), pl.BlockSpec((tm, LANES), lambda i: (i, 0))]
                 + [pl.BlockSpec((tm, W), lambda i: (i, 0))] * len(pairs),
        out_specs=pl.BlockSpec((tm, D), lambda i: (i, 0)),
        out_shape=jax.ShapeDtypeStruct((T, D), F32),
        compiler_params=_params("parallel"),
        name="combine",
    )(h, route, *pairs)


def _moe(h1, hn, route, route_t, counts, layer, w_gate, w_up, w_down):
    T, D = h1.shape
    A = 2 * T
    nblk = -(-A // MOE_BLOCK) + N_EXPERTS
    P = nblk * MOE_BLOCK
    eid = route_t[ROUTE_E1:ROUTE_E2 + 1].astype(jnp.int32)
    rank = route_t[ROUTE_R1:ROUTE_R2 + 1].astype(jnp.int32)
    cnt = counts[0, :N_EXPERTS].astype(jnp.int32)
    padded = (cnt + MOE_BLOCK - 1) // MOE_BLOCK * MOE_BLOCK
    pends = jnp.cumsum(padded)
    pstarts = pends - padded
    experts = jnp.arange(N_EXPERTS, dtype=jnp.int32)
    start_of = jnp.sum(jnp.where(eid[:, None, :] == experts[None, :, None], pstarts[None, :, None], 0), axis=1)
    dest = start_of + rank
    blk_start = jnp.arange(nblk, dtype=jnp.int32) * MOE_BLOCK
    blk_exp = jnp.minimum(jnp.sum((pends[None, :] <= blk_start[:, None]).astype(jnp.int32), axis=1),
                          N_EXPERTS - 1)
    n_used = pends[-1:] // MOE_BLOCK

    gap_start = jnp.concatenate([pstarts + cnt, pends[-1:]])
    gap_len = jnp.concatenate([padded - cnt, P - pends[-1:]])
    gap_end = jnp.cumsum(gap_len)
    j = jnp.arange(P - A, dtype=jnp.int32)
    gap_of = jnp.sum((gap_end[None, :] <= j[:, None]).astype(jnp.int32), axis=1)
    sel = gap_of[:, None] == jnp.arange(N_EXPERTS + 1, dtype=jnp.int32)[None, :]
    pad_idx = jnp.sum(jnp.where(sel, (gap_start - (gap_end - gap_len))[None, :] + j[:, None], 0), axis=1)

    xs = _dispatch(hn, dest, pad_idx.reshape(1, P - A), P)
    ys = _experts(xs, blk_exp, n_used, layer, w_gate, w_up, w_down)
    return _combine(h1, route, _gather_pairs(ys, dest))


def _qkv_kernel(x_ref, gq_ref, gkv_ref, wq_ref, wk_ref, wvt_ref, qn_ref, kn_ref, seg_ref,
                q_ref, k_ref, vt_ref, xs_ref, *, rate, n, n_chunks):
    seg = seg_ref[...]
    width = seg.shape[0]
    res_per_chunk = q_ref.shape[1]
    chunk = pl.program_id(2)

    def head_norm(t, gain):
        cols = []
        for j in range(t.shape[1] // width):
            tj = t[:, j * width:(j + 1) * width]
            ms = jnp.dot((tj * tj).astype(BF16), seg, preferred_element_type=F32)
            cols.append(tj * lax.rsqrt(ms + EPS))
        return jnp.concatenate(cols, axis=1) * gain

    def project(x):
        y = _rms(x)
        xq = (y * gq_ref[...]).astype(BF16)
        xkv = (y * gkv_ref[...]).astype(BF16)
        q = head_norm(jnp.dot(xq, wq_ref[...], preferred_element_type=F32), qn_ref[...]).astype(q_ref.dtype)
        k = head_norm(jnp.dot(xkv, wk_ref[...], preferred_element_type=F32), kn_ref[...]).astype(k_ref.dtype)
        vt = lax.dot_general(wvt_ref[...], xkv, NT_DIMS, preferred_element_type=F32).astype(vt_ref.dtype)
        for j in range(res_per_chunk):
            q_ref[0, j] = q[j * n:(j + 1) * n]
            k_ref[0, j] = k[j * n:(j + 1) * n]
            vt_ref[0, j] = vt[:, j * n:(j + 1) * n]

    if rate == 1:
        project(x_ref[...])
        return

    @pl.when(chunk == 0)
    def _():
        for j in range(xs_ref.shape[0]):
            xs_ref[j] = x_ref[:, j * LANES:(j + 1) * LANES]

    for ch in range(n_chunks):
        @pl.when(chunk == ch)
        def _(ch=ch):
            residues = range(ch * res_per_chunk, (ch + 1) * res_per_chunk)
            project(jnp.concatenate(
                [jnp.concatenate([xs_ref[j, pl.ds(c, n, stride=rate), :] for j in range(xs_ref.shape[0])], axis=1)
                 for c in residues], axis=0))


def _qkv(h, B, S, rate, gq, gkv, wq, wk, wv, qn, kn):
    D = h.shape[1]
    L = S // rate
    hd = DIL_HEAD_DIM
    n = max(SUB_BLOCK, TOKEN_TILE // rate)
    tm = n * rate
    res_per_chunk = max(1, TOKEN_TILE // n)
    n_chunks = rate // res_per_chunk
    width = 2 * LANES
    ii = jnp.arange(width)
    seg = jnp.where((ii[:, None] // hd) == (ii[None, :] // hd), 1.0 / hd, 0.0).astype(BF16)
    row = lambda g: jnp.tile(g, D // hd).reshape(1, D)
    const = lambda shape: _resident(shape, lambda b, i, c: (0,) * len(shape))
    qk_spec = pl.BlockSpec((1, res_per_chunk, n, D), lambda b, i, c: (b, c, i, 0))
    qk_shape = jax.ShapeDtypeStruct((B, rate, L, D), BF16)
    return pl.pallas_call(
        functools.partial(_qkv_kernel, rate=rate, n=n, n_chunks=n_chunks),
        grid=(B, S // tm, n_chunks),
        in_specs=[pl.BlockSpec((tm, D), lambda b, i, c: (b * (S // tm) + i, 0)),
                  const((1, D)), const((1, D)), const((D, D)), const((D, D)), const((D, D)),
                  const((1, D)), const((1, D)), const((width, width))],
        out_specs=[qk_spec, qk_spec,
                   pl.BlockSpec((1, res_per_chunk, D, n), lambda b, i, c: (b, c, 0, i))],
        out_shape=[qk_shape, qk_shape, jax.ShapeDtypeStruct((B, rate, D, L), BF16)],
        scratch_shapes=[pltpu.VMEM((D // LANES, tm if rate > 1 else 8, LANES), F32)],
        compiler_params=_params("parallel", "parallel", "arbitrary"),
        name="qkv_rate%d" % rate,
    )(h, gq.reshape(1, D), gkv.reshape(1, D), wq, wk, wv.T, row(qn) * (hd ** -0.5 * LOG2E), row(kn), seg)


def _attn_kernel(*refs, rate, with_prev):
    if rate == 1:
        q_ref, kc_ref, kp_ref, vc_ref, vp_ref, edge_ref, bias_ref, o_ref, lse_ref, os_ref = refs
    elif with_prev:
        q_ref, kc_ref, kp_ref, vc_ref, vp_ref, edge_ref, o_ref, lse_ref, os_ref = refs
    else:
        q_ref, kc_ref, vc_ref, edge_ref, o_ref, lse_ref, os_ref = refs
    Bk = SUB_BLOCK
    hd = DIL_HEAD_DIM
    n_pairs = DIL_HEADS // 2
    lane = lax.broadcasted_iota(jnp.int32, (Bk, LANES), 1)

    def block(load_q, load_k, load_vt, load_bias, store):
        out_t, lse_t = [], []
        for p in range(n_pairs):
            qp = load_q(p)
            zero = jnp.zeros_like(qp)
            q2 = jnp.concatenate([jnp.where(lane < hd, qp, zero), jnp.where(lane >= hd, qp, zero)], axis=0)
            s = lax.dot_general(load_k(p), q2, NT_DIMS, preferred_element_type=F32) + load_bias(p)
            m = jnp.max(s, axis=0, keepdims=True)
            pr = jnp.exp2(s - m)
            l = jnp.sum(pr, axis=0, keepdims=True)
            pb = pr.astype(BF16)
            vt = load_vt(p)
            out_t.append(jnp.dot(vt[:hd], pb[:, :Bk], preferred_element_type=F32) / l[:, :Bk])
            out_t.append(jnp.dot(vt[hd:], pb[:, Bk:], preferred_element_type=F32) / l[:, Bk:])
            lse = (m + jnp.log2(l)) * LN2
            lse_t += [lse[:, :Bk], lse[:, Bk:]]
        lse_t.append(jnp.zeros((LANES - DIL_HEADS, Bk), F32))
        store(jnp.concatenate(out_t, axis=0).T, jnp.concatenate(lse_t, axis=0).T)

    def pair_cols(p):
        return slice(p * LANES, (p + 1) * LANES)

    edge_bias = lambda p: edge_ref[0, p]
    if rate == 1:
        for j in range(q_ref.shape[2] // Bk):
            rows = slice(j * Bk, (j + 1) * Bk)
            if j == 0:
                load_k = lambda p: jnp.concatenate([kp_ref[0, 0, :, pair_cols(p)],
                                                    kc_ref[0, 0, :Bk, pair_cols(p)]], axis=0)
                load_vt = lambda p: jnp.concatenate([vp_ref[0, 0, pair_cols(p), :],
                                                     vc_ref[0, 0, pair_cols(p), :Bk]], axis=1)
                load_bias = edge_bias
            else:
                krows = slice((j - 1) * Bk, (j + 1) * Bk)
                load_k = lambda p, krows=krows: kc_ref[0, 0, krows, pair_cols(p)]
                load_vt = lambda p, krows=krows: vc_ref[0, 0, pair_cols(p), krows]
                load_bias = lambda p: bias_ref[p]

            def store(o, lse, rows=rows):
                o_ref[rows, :] = o.astype(o_ref.dtype)
                lse_ref[rows, :] = lse

            block(lambda p, rows=rows: q_ref[0, 0, rows, pair_cols(p)], load_k, load_vt, load_bias, store)
    else:
        for c in range(q_ref.shape[1]):
            if with_prev:
                load_k = lambda p, c=c: jnp.concatenate([kp_ref[0, c, :, pair_cols(p)],
                                                         kc_ref[0, c, :, pair_cols(p)]], axis=0)
                load_vt = lambda p, c=c: jnp.concatenate([vp_ref[0, c, pair_cols(p), :],
                                                          vc_ref[0, c, pair_cols(p), :]], axis=1)
            else:
                load_k = lambda p, c=c: kc_ref[0, c, :, pair_cols(p)]
                load_vt = lambda p, c=c: vc_ref[0, c, pair_cols(p), :]

            def store(o, lse, c=c):
                for j in range(os_ref.shape[0]):
                    os_ref[j, pl.ds(c, Bk, stride=rate), :] = o[:, pair_cols(j)]
                lse_ref[pl.ds(c, Bk, stride=rate), :] = lse

            block(lambda p, c=c: q_ref[0, c, :, pair_cols(p)], load_k, load_vt, edge_bias, store)
        for j in range(os_ref.shape[0]):
            o_ref[:, pair_cols(j)] = os_ref[j].astype(o_ref.dtype)


def _t5_bucket(n):
    max_exact = NUM_BUCKETS // 2
    nf = jnp.maximum(n, max_exact).astype(F32)
    large = max_exact + (jnp.log(nf / max_exact) / math.log(MAX_DISTANCE / max_exact)
                         * (NUM_BUCKETS - max_exact)).astype(jnp.int32)
    large = jnp.minimum(large, NUM_BUCKETS - 1)
    return jnp.where(n < max_exact, n, large)


def _group_attention(q, k, vt, bias_table, rate, n_steps):
    B, _, L, D = q.shape
    S = L * rate
    Bk = SUB_BLOCK
    with_prev = L > Bk
    nk = 2 * Bk if with_prev else Bk
    n = max(Bk, TOKEN_TILE // rate)
    tm = n * rate
    nt = S // tm
    n_pairs = DIL_HEADS // 2
    ql = jnp.arange(Bk, dtype=jnp.int32)[:, None]
    kl = jnp.arange(2 * Bk, dtype=jnp.int32)[None, :]
    steps = ql + Bk - kl
    bucket = _t5_bucket(jnp.maximum(steps, 0) * rate)
    buckets = jnp.arange(NUM_BUCKETS, dtype=jnp.int32)
    bias = jnp.sum(jnp.where(bucket[None, :, :, None] == buckets[:, None, None, None],
                             bias_table.astype(F32)[:, None, None, :], 0.0), axis=0)
    band = ((steps >= 0) & (steps <= n_steps))[:, :, None]
    first = (kl < Bk)[:, :, None]

    def layout(t):
        t = t[:, 2 * Bk - nk:, :].transpose(1, 2, 0)
        return t.reshape(nk, n_pairs, 2 * Bk).transpose(1, 0, 2)

    bias_in = layout(jnp.where(band, bias * LOG2E, NEG_INF))
    bias_first = layout(jnp.where(band & ~first, bias * LOG2E, NEG_INF))
    edge = jnp.stack([bias_first, bias_in])

    cur_qk = pl.BlockSpec((1, rate, n, D), lambda b, i: (b, 0, i, 0))
    cur_vt = pl.BlockSpec((1, rate, D, n), lambda b, i: (b, 0, 0, i))
    per_n = n // Bk
    prev_qk = pl.BlockSpec((1, rate, Bk, D), lambda b, i: (b, 0, jnp.maximum(i * per_n - 1, 0), 0))
    prev_vt = pl.BlockSpec((1, rate, D, Bk), lambda b, i: (b, 0, 0, jnp.maximum(i * per_n - 1, 0)))
    edge_spec = pl.BlockSpec((1, n_pairs, nk, 2 * Bk), lambda b, i: (jnp.minimum(i, 1), 0, 0, 0))
    if with_prev:
        in_specs = [cur_qk, cur_qk, prev_qk, cur_vt, prev_vt, edge_spec]
        args = (q, k, k, vt, vt, edge)
    else:
        in_specs = [cur_qk, cur_qk, cur_vt, edge_spec]
        args = (q, k, vt, edge)
    if rate == 1:
        in_specs.append(_resident((n_pairs, nk, 2 * Bk), lambda b, i: (0, 0, 0)))
        args += (bias_in,)
    return pl.pallas_call(
        functools.partial(_attn_kernel, rate=rate, with_prev=with_prev),
        grid=(B, nt),
        in_specs=in_specs,
        out_specs=[pl.BlockSpec((tm, D), lambda b, i: (b * nt + i, 0)),
                   pl.BlockSpec((tm, LANES), lambda b, i: (b * nt + i, 0))],
        out_shape=[jax.ShapeDtypeStruct((B * S, D), BF16),
                   jax.ShapeDtypeStruct((B * S, LANES), F32)],
        scratch_shapes=[pltpu.VMEM((D // LANES, tm if rate > 1 else 8, LANES), F32)],
        compiler_params=_params("parallel", "parallel"),
        name="attn_rate%d" % rate,
    )(*args)


def _merge_kernel(o0_ref, o1_ref, o2_ref, l0_ref, l1_ref, l2_ref, ex_ref, out_ref):
    ex = ex_ref[...]

    def expand(w):
        hi = w.astype(BF16)
        lo = (w - hi.astype(F32)).astype(BF16)
        return jnp.dot(hi, ex, preferred_element_type=F32) + jnp.dot(lo, ex, preferred_element_type=F32)

    l0, l1, l2 = l0_ref[...], l1_ref[...], l2_ref[...]
    top = jnp.maximum(jnp.maximum(l0, l1), l2)
    w0, w1, w2 = jnp.exp(l0 - top), jnp.exp(l1 - top), jnp.exp(l2 - top)
    den = w0 + w1 + w2
    acc = expand(w0 / den) * o0_ref[...].astype(F32)
    acc = acc + expand(w1 / den) * o1_ref[...].astype(F32)
    acc = acc + expand(w2 / den) * o2_ref[...].astype(F32)
    out_ref[...] = acc.astype(out_ref.dtype)


def _merge(outs, lses):
    T, D = outs[0].shape
    tm = min(ROW_TILE, T)
    ex = jnp.where(jnp.arange(LANES)[:, None] == (jnp.arange(D)[None, :] // DIL_HEAD_DIM), 1.0, 0.0).astype(BF16)
    o_spec = pl.BlockSpec((tm, D), lambda i: (i, 0))
    l_spec = pl.BlockSpec((tm, LANES), lambda i: (i, 0))
    return pl.pallas_call(
        _merge_kernel,
        grid=(T // tm,),
        in_specs=[o_spec] * 3 + [l_spec] * 3 + [_resident((LANES, D), lambda i: (0, 0))],
        out_specs=o_spec,
        out_shape=jax.ShapeDtypeStruct((T, D), BF16),
        compiler_params=_params("parallel"),
        name="merge_groups",
    )(*outs, *lses, ex)


def kernel(x, ret_w_in, ret_w_out, kv_norm, w_kv, k_norm, dil_wq, q_norm, dil_wo, rel_bias,
           mixer_norm, ffn_norm, router_grp, router_grp_b, router_exp, router_exp_b,
           exp_gate, exp_up, exp_down):
    B, S, D = x.shape
    h = x.reshape(B * S, D)

    def moe_layer(layer, a, w_out, h):
        h1, *hn, route, route_t, counts = _post(
            a, w_out.astype(BF16), h, ffn_norm[layer], router_grp[layer], router_grp_b[layer],
            router_exp[layer], router_exp_b[layer])
        return _moe(h1, hn, route, route_t, counts, layer, exp_gate, exp_up, exp_down)

    proj = _ret_proj(h, mixer_norm[0], ret_w_in[0].astype(BF16), S)
    y = _retention(proj, B, S)
    h = moe_layer(0, y, ret_w_out[0], h)

    G = len(DIL_RATES)
    gd = DIL_HEADS * DIL_HEAD_DIM
    outs, lses = [], []
    for g in range(G):
        cq = slice(g * gd, (g + 1) * gd)
        cv = slice(G * gd + g * gd, G * gd + (g + 1) * gd)
        q, k, vt = _qkv(h, B, S, DIL_RATES[g], mixer_norm[1], kv_norm,
                        dil_wq[0][:, cq].astype(BF16), w_kv[:, cq].astype(BF16), w_kv[:, cv].astype(BF16),
                        q_norm[0][g], k_norm[g])
        o, lse = _group_attention(q, k, vt, rel_bias[:, g * DIL_HEADS:(g + 1) * DIL_HEADS],
                                  DIL_RATES[g], DIL_WINDOWS[g] // DIL_RATES[g])
        outs.append(o)
        lses.append(lse)
    att = _merge(outs, lses)
    h = moe_layer(1, att, dil_wo[0], h)
    return h.reshape(B, S, D)
```

```python
import functools
import math

import jax
import jax.numpy as jnp
from jax import lax
from jax.experimental import pallas as pl
from jax.experimental.pallas import tpu as pltpu
from jax.experimental.pallas import tpu_sc as plsc

F32 = jnp.float32
BF16 = jnp.bfloat16

EPS = 1e-6
NEG_INF = -1e30

RET_HEADS = 4
RET_CHUNK = 256
ROPE_BASE = 10000.0

DIL_WINDOWS = (128, 512, 2048)
DIL_RATES = (1, 4, 16)
DIL_HEADS = 16
DIL_HEAD_DIM = 64
SUB_BLOCK = 128
NUM_BUCKETS = 32
MAX_DISTANCE = 2048

MOE_GROUPS = 4
EXPERTS_PER_GROUP = 8
N_EXPERTS = MOE_GROUPS * EXPERTS_PER_GROUP
MOE_BLOCK = 512

LANES = 128
ROW_TILE = 1024
TOKEN_TILE = 512
VMEM_LIMIT = 56 * 1024 * 1024

NT_DIMS = (((1,), (1,)), ((), ()))
LOG2E = math.log2(math.e)
LN2 = math.log(2.0)


def _params(*sem):
    return pltpu.CompilerParams(dimension_semantics=sem, vmem_limit_bytes=VMEM_LIMIT)


def _resident(shape, index_map):
    return pl.BlockSpec(shape, index_map, pipeline_mode=pl.Buffered(1))


def _rms(x):
    return x * lax.rsqrt(jnp.mean(x * x, axis=-1, keepdims=True) + EPS)


def _ret_proj_kernel(x_ref, g_ref, w_ref, cos_ref, sin_ref, o_ref, *, col_tile, dk, n_qk, n_gate0, k_scale):
    xn = (_rms(x_ref[...]) * g_ref[...]).astype(BF16)
    half = dk // 2
    cos, sin = cos_ref[...], sin_ref[...]
    for j in range(w_ref.shape[1] // col_tile):
        c0 = j * col_tile
        t = jnp.dot(xn, w_ref[:, c0:c0 + col_tile], preferred_element_type=F32)
        if c0 < 2 * n_qk:
            heads = []
            for h0 in range(0, col_tile, dk):
                t1, t2 = t[:, h0:h0 + half], t[:, h0 + half:h0 + dk]
                heads += [t1 * cos - t2 * sin, t1 * sin + t2 * cos]
            t = jnp.concatenate(heads, axis=1)
            if c0 >= n_qk:
                t = t * k_scale
        elif c0 >= n_gate0:
            t = t * jax.nn.sigmoid(t)
        o_ref[:, c0:c0 + col_tile] = t.astype(o_ref.dtype)


def _ret_proj(x, gain, w, S):
    T, D = x.shape
    N = w.shape[1]
    H = RET_HEADS
    dk = D // H
    half = dk // 2
    tm = min(ROW_TILE, S)
    pos = jnp.arange(S, dtype=F32)
    inv = 1.0 / (ROPE_BASE ** jnp.linspace(0.0, 1.0, half, dtype=F32))
    ang = pos[:, None] * inv[None, :]
    table = pl.BlockSpec((tm, half), lambda i: (i % (S // tm), 0))
    return pl.pallas_call(
        functools.partial(_ret_proj_kernel, col_tile=512, dk=dk, n_qk=H * dk, n_gate0=N - 2 * D,
                          k_scale=dk ** -0.5),
        grid=(T // tm,),
        in_specs=[pl.BlockSpec((tm, D), lambda i: (i, 0)),
                  _resident((1, D), lambda i: (0, 0)),
                  _resident((D, N), lambda i: (0, 0)),
                  table, table],
        out_specs=pl.BlockSpec((tm, N), lambda i: (i, 0)),
        out_shape=jax.ShapeDtypeStruct((T, N), BF16),
        compiler_params=_params("parallel"),
        name="norm_proj",
    )(x, gain.reshape(1, D), w, jnp.cos(ang), jnp.sin(ang))


def _retention_kernel(q_ref, k_ref, v_ref, gate_ref, din_ref, xi_ref, zeta_ref, cd_ref, o_ref, state_ref):
    C = RET_CHUNK
    heads = state_ref.shape[0]
    dk = q_ref.shape[2] // heads
    dv = v_ref.shape[2] // heads
    ts = q_ref.shape[1]

    @pl.when(pl.program_id(1) == 0)
    def _():
        state_ref[...] = jnp.zeros_like(state_ref)

    for c in range(ts // C):
        rows = slice(c * C, (c + 1) * C)
        for hh in range(heads):
            qk_cols = slice(hh * dk, (hh + 1) * dk)
            v_cols = slice(hh * dv, (hh + 1) * dv)
            q = q_ref[0, rows, qk_cols]
            k = k_ref[0, rows, qk_cols]
            v = v_ref[0, rows, v_cols]
            s = lax.dot_general(q, k, NT_DIMS, preferred_element_type=F32) * din_ref[hh]
            inner = jnp.dot(s.astype(BF16), v, preferred_element_type=F32)
            state = state_ref[hh]
            cross = jnp.dot(q, state.astype(BF16), preferred_element_type=F32) * xi_ref[hh]
            kz_t = (k.astype(F32) * zeta_ref[hh]).T.astype(BF16)
            state_ref[hh] = state * cd_ref[hh] + jnp.dot(kz_t, v, preferred_element_type=F32)
            o = _rms(inner + cross)
            o_ref[0, rows, v_cols] = (gate_ref[0, rows, v_cols].astype(F32) * o).astype(o_ref.dtype)


def _retention(proj, B, S):
    H, C = RET_HEADS, RET_CHUNK
    D = proj.shape[1] // 6
    dk, dv = D // H, 2 * D // H
    ts = min(ROW_TILE, S)
    proj = proj.reshape(B, S, 6 * D)

    log_g = jnp.log(1.0 - 2.0 ** (-5.0 - jnp.arange(H, dtype=F32)))
    idx = jnp.arange(C, dtype=F32)
    diff = idx[:, None] - idx[None, :]
    d_in = jnp.where(diff >= 0, jnp.exp(log_g[:, None, None] * jnp.maximum(diff, 0.0)), 0.0)
    xi = jnp.exp(log_g[:, None] * (idx + 1.0))[:, :, None]
    zeta = jnp.exp(log_g[:, None] * (C - 1.0 - idx))[:, :, None]
    chunk_decay = jnp.exp(log_g * C)[:, None, None]

    wq, wv = H * dk, H * dv
    per_head = lambda shape: _resident((H,) + shape, lambda b, i: (0, 0, 0))
    out = pl.pallas_call(
        _retention_kernel,
        grid=(B, S // ts),
        in_specs=[pl.BlockSpec((1, ts, wq), lambda b, i: (b, i, 0)),
                  pl.BlockSpec((1, ts, wq), lambda b, i: (b, i, 1)),
                  pl.BlockSpec((1, ts, wv), lambda b, i: (b, i, 1)),
                  pl.BlockSpec((1, ts, wv), lambda b, i: (b, i, 2)),
                  per_head((C, C)), per_head((C, 1)), per_head((C, 1)), per_head((1, 1))],
        out_specs=pl.BlockSpec((1, ts, wv), lambda b, i: (b, i, 0)),
        out_shape=jax.ShapeDtypeStruct((B, S, H * dv), BF16),
        scratch_shapes=[pltpu.VMEM((H, dk, dv), F32)],
        compiler_params=_params("parallel", "arbitrary"),
        name="retention",
    )(proj, proj, proj, proj, d_in, xi, zeta, chunk_decay)
    return out.reshape(B * S, H * dv)


ROUTE_E1, ROUTE_E2, ROUTE_G1, ROUTE_G2, ROUTE_R1, ROUTE_R2 = range(6)
ROUTE_ROWS = 8
ROUTER_EXP_LANE0 = MOE_GROUPS
HI16 = 0xFFFF0000


def _pack_bf16_pair(a, b):
    ua = lax.bitcast_convert_type(a.astype(BF16).astype(F32), jnp.uint32)
    ub = lax.bitcast_convert_type(b.astype(BF16).astype(F32), jnp.uint32)
    return ua | (ub >> 16)


def _unpack_bf16_pair(w):
    a = lax.bitcast_convert_type(w & jnp.uint32(HI16), F32)
    b = lax.bitcast_convert_type(w << 16, F32)
    return a, b


ROW_PARTS = 2


def _part_cols(D, part):
    w = D // 2 // ROW_PARTS
    return slice(part * w, (part + 1) * w), slice(D // 2 + part * w, D // 2 + (part + 1) * w)


def _pack_part(x, part):
    hi, lo = _part_cols(x.shape[1], part)
    return _pack_bf16_pair(x[:, hi], x[:, lo])


def _merge_groups(o_refs, l_refs, ex_ref):
    ex = ex_ref[...]

    def expand(w):
        hi = w.astype(BF16)
        lo = (w - hi.astype(F32)).astype(BF16)
        return jnp.dot(hi, ex, preferred_element_type=F32) + jnp.dot(lo, ex, preferred_element_type=F32)

    lses = [l_ref[...] for l_ref in l_refs]
    top = functools.reduce(jnp.maximum, lses)
    ws = [jnp.exp(l - top) for l in lses]
    den = sum(ws)
    return sum(expand(w / den) * o_ref[...].astype(F32) for w, o_ref in zip(ws, o_refs)).astype(BF16)


def _post_kernel(*refs, n_groups):
    n_a = 2 * n_groups + 1 if n_groups else 1
    a_refs = refs[:n_a]
    (w_ref, h_ref, g_ref, wr_ref, br_ref, tri_ref,
     h1_ref, hn0_ref, hn1_ref, route_ref, route_t_ref, cnt_ref, carry_ref) = refs[n_a:]
    hn_refs = (hn0_ref, hn1_ref)
    if n_groups:
        a = _merge_groups(a_refs[:n_groups], a_refs[n_groups:2 * n_groups], a_refs[-1])
    else:
        a = a_refs[0][...]
    @pl.when(pl.program_id(0) == 0)
    def _():
        carry_ref[...] = jnp.zeros_like(carry_ref)

    half = h_ref.shape[0] // 2
    D = h_ref.shape[1]
    lane = lax.broadcasted_iota(jnp.int32, (half, LANES), 1).astype(F32)
    ninf = -jnp.inf

    def first_argmax(vals):
        top = jnp.max(vals, axis=1, keepdims=True)
        where = jnp.min(jnp.where(vals == top, lane, float(LANES)), axis=1, keepdims=True)
        return top, where

    h1_ref[...] = h_ref[...] + jnp.dot(a, w_ref[...], preferred_element_type=F32)

    def route_half(rows):
        hn = _rms(h1_ref[rows, :]) * g_ref[...]
        for part, ref in enumerate(hn_refs):
            ref[rows, :] = _pack_part(hn, part)
        hi = hn.astype(BF16)
        lo = (hn - hi.astype(F32)).astype(BF16)
        logits = jnp.dot(jnp.concatenate([hi, hi, lo], axis=1), wr_ref[...],
                         preferred_element_type=F32) + br_ref[...]

        is_grp = lane < MOE_GROUPS
        lg = jnp.where(is_grp, logits, ninf)
        mg, grp = first_argmax(lg)
        p_grp = 1.0 / jnp.sum(jnp.where(is_grp, jnp.exp(lg - mg), 0.0), axis=1, keepdims=True)

        e_lane = lane - ROUTER_EXP_LANE0
        in_grp = (e_lane < N_EXPERTS) & (jnp.floor(e_lane * (1.0 / EXPERTS_PER_GROUP)) == grp)
        le = jnp.where(in_grp, logits, ninf)
        v1, i1 = first_argmax(le)
        le2 = jnp.where(lane == i1, ninf, le)
        v2, i2 = first_argmax(le2)
        e = jnp.exp(v2 - v1)
        hit1 = lane == (i1 - ROUTER_EXP_LANE0)
        hit2 = lane == (i2 - ROUTER_EXP_LANE0)
        onehot = jnp.where(hit1 | hit2, 1.0, 0.0)
        earlier = jnp.dot(tri_ref[...], onehot.astype(BF16), preferred_element_type=F32)
        return dict(e1=i1 - ROUTER_EXP_LANE0, e2=i2 - ROUTER_EXP_LANE0, g1=p_grp / (1.0 + e),
                    g2=p_grp * e / (1.0 + e), hit1=hit1, hit2=hit2, earlier=earlier,
                    count=jnp.sum(onehot, axis=0, keepdims=True))

    halves = [route_half(slice(0, half)), route_half(slice(half, 2 * half))]
    carry = carry_ref[...]
    for j, r in enumerate(halves):
        before = carry + r["earlier"]
        r1 = jnp.sum(jnp.where(r["hit1"], before, 0.0), axis=1, keepdims=True)
        r2 = jnp.sum(jnp.where(r["hit2"], before, 0.0), axis=1, keepdims=True)
        carry = carry + r["count"]
        route = jnp.zeros((half, LANES), F32)
        for slot, val in ((ROUTE_E1, r["e1"]), (ROUTE_E2, r["e2"]), (ROUTE_G1, r["g1"]),
                          (ROUTE_G2, r["g2"]), (ROUTE_R1, r1), (ROUTE_R2, r2)):
            route = jnp.where(lane == slot, val, route)
        route_ref[j * half:(j + 1) * half, :] = route
        route_t_ref[:, j * half:(j + 1) * half] = route.T[:ROUTE_ROWS]
    carry_ref[...] = carry
    cnt_ref[...] = carry


def _post(a, w, h, gain, w_grp, b_grp, w_exp, b_exp):
    T, D = h.shape
    K = w.shape[0]
    tm = min(TOKEN_TILE, T)
    row_block = lambda width: pl.BlockSpec((tm, width), lambda i: (i, 0))
    if isinstance(a, tuple):
        outs, lses = a
        n_groups = len(outs)
        ex = jnp.where(jnp.arange(LANES)[:, None] == (jnp.arange(K)[None, :] // DIL_HEAD_DIM), 1.0, 0.0).astype(BF16)
        a_args = (*outs, *lses, ex)
        a_specs = [row_block(K)] * n_groups + [row_block(LANES)] * n_groups + [_resident((LANES, K), lambda i: (0, 0))]
    else:
        n_groups, a_args, a_specs = 0, (a,), [row_block(K)]
    n_r = MOE_GROUPS + N_EXPERTS
    wr = jnp.zeros((D, LANES), F32).at[:, :n_r].set(jnp.concatenate([w_grp, w_exp], axis=1))
    wr_hi = wr.astype(BF16)
    wr_lo = (wr - wr_hi.astype(F32)).astype(BF16)
    wr3 = jnp.concatenate([wr_hi, wr_lo, wr_hi], axis=0)
    br = jnp.zeros((1, LANES), F32).at[0, :n_r].set(jnp.concatenate([b_grp, b_exp]))
    tri = jnp.tril(jnp.ones((tm // 2, tm // 2), BF16), k=-1)
    return pl.pallas_call(
        functools.partial(_post_kernel, n_groups=n_groups),
        grid=(T // tm,),
        in_specs=a_specs + [
                  _resident((K, D), lambda i: (0, 0)),
                  pl.BlockSpec((tm, D), lambda i: (i, 0)),
                  _resident((1, D), lambda i: (0, 0)),
                  _resident((3 * D, LANES), lambda i: (0, 0)),
                  _resident((1, LANES), lambda i: (0, 0)),
                  _resident((tm // 2, tm // 2), lambda i: (0, 0))],
        out_specs=[pl.BlockSpec((tm, D), lambda i: (i, 0))]
                  + [pl.BlockSpec((tm, D // 2 // ROW_PARTS), lambda i: (i, 0))] * ROW_PARTS
                  + [pl.BlockSpec((tm, LANES), lambda i: (i, 0)),
                   pl.BlockSpec((ROUTE_ROWS, tm), lambda i: (0, i)),
                   pl.BlockSpec((1, LANES), lambda i: (0, 0))],
        out_shape=[jax.ShapeDtypeStruct((T, D), F32)]
                  + [jax.ShapeDtypeStruct((T, D // 2 // ROW_PARTS), jnp.uint32)] * ROW_PARTS
                  + [jax.ShapeDtypeStruct((T, LANES), F32),
                   jax.ShapeDtypeStruct((ROUTE_ROWS, T), F32),
                   jax.ShapeDtypeStruct((1, LANES), F32)],
        scratch_shapes=[pltpu.VMEM((1, LANES), F32)],
        compiler_params=_params("arbitrary"),
        name="post_mixer",
    )(*a_args, w, h, gain.reshape(1, D), wr3, br, tri)


SC_WINDOW = 128


def _sc_mesh():
    return plsc.VectorSubcoreMesh(core_axis_name="core", subcore_axis_name="subcore")


def _sc_window_specs(W):
    rows = pl.BlockSpec((SC_WINDOW, W), lambda i: (i, 0))
    idx = pl.BlockSpec((1, SC_WINDOW), lambda i: (0, i))
    return rows, idx


def _sc_pipeline(body, n_rows, in_specs, out_specs):
    return pltpu.emit_pipeline(body, grid=(n_rows // SC_WINDOW,), in_specs=in_specs, out_specs=out_specs,
                               core_axis_name=("core", "subcore"), dimension_semantics=(pltpu.PARALLEL,))


def _dispatch(xs, dest, pad_idx, P):
    T, W = xs[0].shape
    n_pad = pad_idx.shape[1]
    n = len(xs)
    rows, idx = _sc_window_specs(W)
    zero_rows = pl.BlockSpec((SC_WINDOW, W), lambda i: (0, 0))
    out = jax.ShapeDtypeStruct((P, W), xs[0].dtype)

    @functools.partial(pl.kernel, out_type=(out,) * n, mesh=_sc_mesh(), scratch_types=[], name="dispatch")
    def scatter(*refs):
        x_hbm, (d0_hbm, d1_hbm, z_hbm, p_hbm), o_hbm = refs[:n], refs[n:n + 4], refs[n + 4:]
        for x, o in zip(x_hbm, o_hbm):
            def put_pair(x_vmem, i0_vmem, i1_vmem, o=o):
                pltpu.sync_copy(x_vmem, o.at[i0_vmem.at[0]])
                pltpu.sync_copy(x_vmem, o.at[i1_vmem.at[0]])

            def put(x_vmem, i_vmem, o=o):
                pltpu.sync_copy(x_vmem, o.at[i_vmem.at[0]])

            _sc_pipeline(put_pair, T, [rows, idx, idx], [])(x, d0_hbm, d1_hbm)
            _sc_pipeline(put, n_pad, [zero_rows, idx], [])(z_hbm, p_hbm)

    return scatter(*xs, dest[0:1], dest[1:2], jnp.zeros((SC_WINDOW, W), xs[0].dtype), pad_idx)


WEIGHT_LEADS = (3, 2, 1)
EXPERT_SLOTS = max(WEIGHT_LEADS) + 1


def _expert_kernel(blk_exp_ref, slot_ref, n_used_ref, *refs):
    x_refs = refs[:ROW_PARTS]
    w_refs = refs[ROW_PARTS:ROW_PARTS + 3]
    o_refs = refs[ROW_PARTS + 3:2 * ROW_PARTS + 3]
    w_slots = refs[2 * ROW_PARTS + 3:]
    lead = max(WEIGHT_LEADS)
    n_used = n_used_ref[0]
    i = pl.program_id(0) - lead
    D = w_slots[0].shape[1]

    for w_ref, w_s, ahead in zip(w_refs, w_slots, WEIGHT_LEADS):
        j = i + ahead
        jc = jnp.clip(j, 0, n_used - 1)
        arrived = (j >= 0) & (j < n_used) & ((j == 0) | (blk_exp_ref[jc] != blk_exp_ref[jnp.maximum(jc - 1, 0)]))

        @pl.when(arrived)
        def _(w_ref=w_ref, w_s=w_s, jc=jc):
            w_s[slot_ref[jc]] = w_ref[0, 0].astype(BF16)

    @pl.when((i >= 0) & (i < n_used))
    def _():
        slot = slot_ref[jnp.clip(i, 0, n_used - 1)]
        wg_s, wu_s, wd_s = w_slots
        pieces = []
        for part, x_ref in enumerate(x_refs):
            for cols, val in zip(_part_cols(D, part), _unpack_bf16_pair(x_ref[...])):
                pieces.append((cols, val.astype(BF16)))

        def up(w_s):
            return sum(jnp.dot(val, w_s[slot, cols, :], preferred_element_type=F32) for cols, val in pieces)

        g = up(wg_s)
        hid = (g * jax.nn.sigmoid(g) * up(wu_s)).astype(BF16)
        y = jnp.dot(hid, wd_s[slot], preferred_element_type=F32)
        for part, o_ref in enumerate(o_refs):
            o_ref[...] = _pack_part(y, part)

    @pl.when(i >= n_used)
    def _():
        for o_ref in o_refs:
            o_ref[...] = jnp.zeros_like(o_ref)


def _experts(xs, blk_exp, n_used, layer, w_gate, w_up, w_down):
    P, W = xs[0].shape
    D = 2 * W * ROW_PARTS
    FF = w_gate.shape[3]
    nblk = P // MOE_BLOCK
    lead = max(WEIGHT_LEADS)
    changes = jnp.concatenate([jnp.zeros((1,), jnp.int32), (blk_exp[1:] != blk_exp[:-1]).astype(jnp.int32)])
    slot = jnp.cumsum(changes) % EXPERT_SLOTS

    def x_map(g, be, sl, nu):
        return (jnp.clip(g - lead, 0, nu[0] - 1), 0)

    def o_map(g, be, sl, nu):
        return (jnp.maximum(g - lead, 0), 0)

    def w_map(ahead):
        return lambda g, be, sl, nu: (layer, be[jnp.clip(g - lead + ahead, 0, nu[0] - 1)], 0, 0)

    wg_spec, wu_spec, wd_spec = (pl.BlockSpec((1, 1) + shape, w_map(ahead))
                                 for shape, ahead in zip(((D, FF), (D, FF), (FF, D)), WEIGHT_LEADS))
    return pl.pallas_call(
        _expert_kernel,
        grid_spec=pltpu.PrefetchScalarGridSpec(
            num_scalar_prefetch=3,
            grid=(nblk + lead,),
            in_specs=[pl.BlockSpec((MOE_BLOCK, W), x_map)] * ROW_PARTS + [wg_spec, wu_spec, wd_spec],
            out_specs=[pl.BlockSpec((MOE_BLOCK, W), o_map)] * ROW_PARTS,
            scratch_shapes=[pltpu.VMEM((EXPERT_SLOTS, D, FF), BF16), pltpu.VMEM((EXPERT_SLOTS, D, FF), BF16),
                            pltpu.VMEM((EXPERT_SLOTS, FF, D), BF16)]),
        out_shape=[jax.ShapeDtypeStruct((P, W), jnp.uint32)] * ROW_PARTS,
        compiler_params=_params("arbitrary"),
        name="experts",
    )(blk_exp, slot, n_used, *xs, w_gate, w_up, w_down)


def _gather_pairs(ys, dest):
    W = ys[0].shape[1]
    T = dest.shape[1]
    n = len(ys)
    rows, idx = _sc_window_specs(W)
    out = jax.ShapeDtypeStruct((T, W), ys[0].dtype)

    @functools.partial(pl.kernel, out_type=(out,) * (2 * n), mesh=_sc_mesh(), scratch_types=[],
                       name="gather_pairs")
    def gather(*refs):
        y_hbm, d_hbm, o_hbm = refs[:n], refs[n:n + 2], refs[n + 2:]
        for slot, d in enumerate(d_hbm):
            for y, o in zip(y_hbm, o_hbm[slot * n:(slot + 1) * n]):
                def get(i_vmem, o_vmem, y=y):
                    pltpu.sync_copy(y.at[i_vmem.at[0]], o_vmem)

                _sc_pipeline(get, T, [idx], [rows])(d, o)

    return gather(*ys, dest[0:1], dest[1:2])


def _combine_kernel(h_ref, route_ref, *refs):
    y_refs, o_ref = refs[:-1], refs[-1]
    D = h_ref.shape[1]
    route = route_ref[...]
    gates = (route[:, ROUTE_G1:ROUTE_G1 + 1], route[:, ROUTE_G2:ROUTE_G2 + 1])
    for part in range(ROW_PARTS):
        slots = [_unpack_bf16_pair(y_refs[slot * ROW_PARTS + part][...]) for slot in range(2)]
        for half, cols in enumerate(_part_cols(D, part)):
            o_ref[:, cols] = h_ref[:, cols] + (gates[0] * slots[0][half] + gates[1] * slots[1][half])


def _combine(h, route, pairs):
    T, D = h.shape
    W = pairs[0].shape[1]
    tm = min(ROW_TILE, T)
    return pl.pallas_call(
        _combine_kernel,
        grid=(T // tm,),
        in_specs=[pl.BlockSpec((tm, D), lambda i: (i, 0)), pl.BlockSpec((tm, LANES), lambda i: (i, 0))]
                 + [pl.BlockSpec((tm, W), lambda i: (i, 0))] * len(pairs),
        out_specs=pl.BlockSpec((tm, D), lambda i: (i, 0)),
        out_shape=jax.ShapeDtypeStruct((T, D), F32),
        compiler_params=_params("parallel"),
        name="combine",
    )(h, route, *pairs)


def _moe(h1, hn, route, route_t, counts, layer, w_gate, w_up, w_down):
    T, D = h1.shape
    A = 2 * T
    nblk = -(-A // MOE_BLOCK) + N_EXPERTS
    P = nblk * MOE_BLOCK
    eid = route_t[ROUTE_E1:ROUTE_E2 + 1].astype(jnp.int32)
    rank = route_t[ROUTE_R1:ROUTE_R2 + 1].astype(jnp.int32)
    cnt = counts[0, :N_EXPERTS].astype(jnp.int32)
    padded = (cnt + MOE_BLOCK - 1) // MOE_BLOCK * MOE_BLOCK
    pends = jnp.cumsum(padded)
    pstarts = pends - padded
    experts = jnp.arange(N_EXPERTS, dtype=jnp.int32)
    start_of = jnp.sum(jnp.where(eid[:, None, :] == experts[None, :, None], pstarts[None, :, None], 0), axis=1)
    dest = start_of + rank
    blk_start = jnp.arange(nblk, dtype=jnp.int32) * MOE_BLOCK
    blk_exp = jnp.minimum(jnp.sum((pends[None, :] <= blk_start[:, None]).astype(jnp.int32), axis=1),
                          N_EXPERTS - 1)
    n_used = pends[-1:] // MOE_BLOCK

    gap_start = jnp.concatenate([pstarts + cnt, pends[-1:]])
    gap_len = jnp.concatenate([padded - cnt, P - pends[-1:]])
    gap_end = jnp.cumsum(gap_len)
    j = jnp.arange(P - A, dtype=jnp.int32)
    gap_of = jnp.sum((gap_end[None, :] <= j[:, None]).astype(jnp.int32), axis=1)
    sel = gap_of[:, None] == jnp.arange(N_EXPERTS + 1, dtype=jnp.int32)[None, :]
    pad_idx = jnp.sum(jnp.where(sel, (gap_start - (gap_end - gap_len))[None, :] + j[:, None], 0), axis=1)

    xs = _dispatch(hn, dest, pad_idx.reshape(1, P - A), P)
    ys = _experts(xs, blk_exp, n_used, layer, w_gate, w_up, w_down)
    return _combine(h1, route, _gather_pairs(ys, dest))


def _qkv_kernel(x_ref, gq_ref, gkv_ref, wq_ref, wk_ref, wvt_ref, qn_ref, kn_ref, seg_ref,
                q_ref, k_ref, vt_ref, xs_ref, *, rate, n, n_chunks):
    seg = seg_ref[...]
    width = seg.shape[0]
    res_per_chunk = q_ref.shape[1]
    chunk = pl.program_id(2)

    def head_norm(t, gain):
        cols = []
        for j in range(t.shape[1] // width):
            tj = t[:, j * width:(j + 1) * width]
            ms = jnp.dot((tj * tj).astype(BF16), seg, preferred_element_type=F32)
            cols.append(tj * lax.rsqrt(ms + EPS))
        return jnp.concatenate(cols, axis=1) * gain

    def project(x):
        y = _rms(x)
        xq = (y * gq_ref[...]).astype(BF16)
        xkv = (y * gkv_ref[...]).astype(BF16)
        q = head_norm(jnp.dot(xq, wq_ref[...], preferred_element_type=F32), qn_ref[...]).astype(q_ref.dtype)
        k = head_norm(jnp.dot(xkv, wk_ref[...], preferred_element_type=F32), kn_ref[...]).astype(k_ref.dtype)
        vt = lax.dot_general(wvt_ref[...], xkv, NT_DIMS, preferred_element_type=F32).astype(vt_ref.dtype)
        for j in range(res_per_chunk):
            q_ref[0, j] = q[j * n:(j + 1) * n]
            k_ref[0, j] = k[j * n:(j + 1) * n]
            vt_ref[0, j] = vt[:, j * n:(j + 1) * n]

    if rate == 1:
        project(x_ref[...])
        return

    @pl.when(chunk == 0)
    def _():
        for j in range(xs_ref.shape[0]):
            xs_ref[j] = x_ref[:, j * LANES:(j + 1) * LANES]

    for ch in range(n_chunks):
        @pl.when(chunk == ch)
        def _(ch=ch):
            residues = range(ch * res_per_chunk, (ch + 1) * res_per_chunk)
            project(jnp.concatenate(
                [jnp.concatenate([xs_ref[j, pl.ds(c, n, stride=rate), :] for j in range(xs_ref.shape[0])], axis=1)
                 for c in residues], axis=0))


def _qkv(h, B, S, rate, gq, gkv, wq, wk, wv, qn, kn):
    D = h.shape[1]
    L = S // rate
    hd = DIL_HEAD_DIM
    n = max(SUB_BLOCK, TOKEN_TILE // rate)
    tm = n * rate
    res_per_chunk = max(1, TOKEN_TILE // n)
    n_chunks = rate // res_per_chunk
    width = 2 * LANES
    ii = jnp.arange(width)
    seg = jnp.where((ii[:, None] // hd) == (ii[None, :] // hd), 1.0 / hd, 0.0).astype(BF16)
    row = lambda g: jnp.tile(g, D // hd).reshape(1, D)
    const = lambda shape: _resident(shape, lambda b, i, c: (0,) * len(shape))
    qk_spec = pl.BlockSpec((1, res_per_chunk, n, D), lambda b, i, c: (b, c, i, 0))
    qk_shape = jax.ShapeDtypeStruct((B, rate, L, D), BF16)
    return pl.pallas_call(
        functools.partial(_qkv_kernel, rate=rate, n=n, n_chunks=n_chunks),
        grid=(B, S // tm, n_chunks),
        in_specs=[pl.BlockSpec((tm, D), lambda b, i, c: (b * (S // tm) + i, 0)),
                  const((1, D)), const((1, D)), const((D, D)), const((D, D)), const((D, D)),
                  const((1, D)), const((1, D)), const((width, width))],
        out_specs=[qk_spec, qk_spec,
                   pl.BlockSpec((1, res_per_chunk, D, n), lambda b, i, c: (b, c, 0, i))],
        out_shape=[qk_shape, qk_shape, jax.ShapeDtypeStruct((B, rate, D, L), BF16)],
        scratch_shapes=[pltpu.VMEM((D // LANES, tm if rate > 1 else 8, LANES), F32)],
        compiler_params=_params("parallel", "parallel", "arbitrary"),
        name="qkv_rate%d" % rate,
    )(h, gq.reshape(1, D), gkv.reshape(1, D), wq, wk, wv.T, row(qn) * (hd ** -0.5 * LOG2E), row(kn), seg)


def _attn_kernel(*refs, rate, with_prev):
    if rate == 1:
        q_ref, kc_ref, kp_ref, vc_ref, vp_ref, edge_ref, bias_ref, o_ref, lse_ref, os_ref = refs
    elif with_prev:
        q_ref, kc_ref, kp_ref, vc_ref, vp_ref, edge_ref, o_ref, lse_ref, os_ref = refs
    else:
        q_ref, kc_ref, vc_ref, edge_ref, o_ref, lse_ref, os_ref = refs
    Bk = SUB_BLOCK
    hd = DIL_HEAD_DIM
    n_pairs = DIL_HEADS // 2
    lane = lax.broadcasted_iota(jnp.int32, (Bk, LANES), 1)

    def block(load_q, load_k, load_vt, load_bias, store):
        out_t, lse_t = [], []
        for p in range(n_pairs):
            qp = load_q(p)
            zero = jnp.zeros_like(qp)
            q2 = jnp.concatenate([jnp.where(lane < hd, qp, zero), jnp.where(lane >= hd, qp, zero)], axis=0)
            s = lax.dot_general(load_k(p), q2, NT_DIMS, preferred_element_type=F32) + load_bias(p)
            m = jnp.max(s, axis=0, keepdims=True)
            pr = jnp.exp2(s - m)
            l = jnp.sum(pr, axis=0, keepdims=True)
            pb = pr.astype(BF16)
            vt = load_vt(p)
            out_t.append(jnp.dot(vt[:hd], pb[:, :Bk], preferred_element_type=F32) / l[:, :Bk])
            out_t.append(jnp.dot(vt[hd:], pb[:, Bk:], preferred_element_type=F32) / l[:, Bk:])
            lse = (m + jnp.log2(l)) * LN2
            lse_t += [lse[:, :Bk], lse[:, Bk:]]
        lse_t.append(jnp.zeros((LANES - DIL_HEADS, Bk), F32))
        store(jnp.concatenate(out_t, axis=0).T, jnp.concatenate(lse_t, axis=0).T)

    def pair_cols(p):
        return slice(p * LANES, (p + 1) * LANES)

    edge_bias = lambda p: edge_ref[0, p]
    if rate == 1:
        for j in range(q_ref.shape[2] // Bk):
            rows = slice(j * Bk, (j + 1) * Bk)
            if j == 0:
                load_k = lambda p: jnp.concatenate([kp_ref[0, 0, :, pair_cols(p)],
                                                    kc_ref[0, 0, :Bk, pair_cols(p)]], axis=0)
                load_vt = lambda p: jnp.concatenate([vp_ref[0, 0, pair_cols(p), :],
                                                     vc_ref[0, 0, pair_cols(p), :Bk]], axis=1)
                load_bias = edge_bias
            else:
                krows = slice((j - 1) * Bk, (j + 1) * Bk)
                load_k = lambda p, krows=krows: kc_ref[0, 0, krows, pair_cols(p)]
                load_vt = lambda p, krows=krows: vc_ref[0, 0, pair_cols(p), krows]
                load_bias = lambda p: bias_ref[p]

            def store(o, lse, rows=rows):
                o_ref[rows, :] = o.astype(o_ref.dtype)
                lse_ref[rows, :] = lse

            block(lambda p, rows=rows: q_ref[0, 0, rows, pair_cols(p)], load_k, load_vt, load_bias, store)
    else:
        for c in range(q_ref.shape[1]):
            if with_prev:
                load_k = lambda p, c=c: jnp.concatenate([kp_ref[0, c, :, pair_cols(p)],
                                                         kc_ref[0, c, :, pair_cols(p)]], axis=0)
                load_vt = lambda p, c=c: jnp.concatenate([vp_ref[0, c, pair_cols(p), :],
                                                          vc_ref[0, c, pair_cols(p), :]], axis=1)
            else:
                load_k = lambda p, c=c: kc_ref[0, c, :, pair_cols(p)]
                load_vt = lambda p, c=c: vc_ref[0, c, pair_cols(p), :]

            def store(o, lse, c=c):
                for j in range(os_ref.shape[0]):
                    os_ref[j, pl.ds(c, Bk, stride=rate), :] = o[:, pair_cols(j)]
                lse_ref[pl.ds(c, Bk, stride=rate), :] = lse

            block(lambda p, c=c: q_ref[0, c, :, pair_cols(p)], load_k, load_vt, edge_bias, store)
        for j in range(os_ref.shape[0]):
            o_ref[:, pair_cols(j)] = os_ref[j].astype(o_ref.dtype)


def _t5_bucket(n):
    max_exact = NUM_BUCKETS // 2
    nf = jnp.maximum(n, max_exact).astype(F32)
    large = max_exact + (jnp.log(nf / max_exact) / math.log(MAX_DISTANCE / max_exact)
                         * (NUM_BUCKETS - max_exact)).astype(jnp.int32)
    large = jnp.minimum(large, NUM_BUCKETS - 1)
    return jnp.where(n < max_exact, n, large)


def _group_attention(q, k, vt, bias_table, rate, n_steps):
    B, _, L, D = q.shape
    S = L * rate
    Bk = SUB_BLOCK
    with_prev = L > Bk
    nk = 2 * Bk if with_prev else Bk
    n = max(Bk, TOKEN_TILE // rate)
    tm = n * rate
    nt = S // tm
    n_pairs = DIL_HEADS // 2
    ql = jnp.arange(Bk, dtype=jnp.int32)[:, None]
    kl = jnp.arange(2 * Bk, dtype=jnp.int32)[None, :]
    steps = ql + Bk - kl
    bucket = _t5_bucket(jnp.maximum(steps, 0) * rate)
    buckets = jnp.arange(NUM_BUCKETS, dtype=jnp.int32)
    bias = jnp.sum(jnp.where(bucket[None, :, :, None] == buckets[:, None, None, None],
                             bias_table.astype(F32)[:, None, None, :], 0.0), axis=0)
    band = ((steps >= 0) & (steps <= n_steps))[:, :, None]
    first = (kl < Bk)[:, :, None]

    def layout(t):
        t = t[:, 2 * Bk - nk:, :].transpose(1, 2, 0)
        return t.reshape(nk, n_pairs, 2 * Bk).transpose(1, 0, 2)

    bias_in = layout(jnp.where(band, bias * LOG2E, NEG_INF))
    bias_first = layout(jnp.where(band & ~first, bias * LOG2E, NEG_INF))
    edge = jnp.stack([bias_first, bias_in])

    cur_qk = pl.BlockSpec((1, rate, n, D), lambda b, i: (b, 0, i, 0))
    cur_vt = pl.BlockSpec((1, rate, D, n), lambda b, i: (b, 0, 0, i))
    per_n = n // Bk
    prev_qk = pl.BlockSpec((1, rate, Bk, D), lambda b, i: (b, 0, jnp.maximum(i * per_n - 1, 0), 0))
    prev_vt = pl.BlockSpec((1, rate, D, Bk), lambda b, i: (b, 0, 0, jnp.maximum(i * per_n - 1, 0)))
    edge_spec = pl.BlockSpec((1, n_pairs, nk, 2 * Bk), lambda b, i: (jnp.minimum(i, 1), 0, 0, 0))
    if with_prev:
        in_specs = [cur_qk, cur_qk, prev_qk, cur_vt, prev_vt, edge_spec]
        args = (q, k, k, vt, vt, edge)
    else:
        in_specs = [cur_qk, cur_qk, cur_vt, edge_spec]
        args = (q, k, vt, edge)
    if rate == 1:
        in_specs.append(_resident((n_pairs, nk, 2 * Bk), lambda b, i: (0, 0, 0)))
        args += (bias_in,)
    return pl.pallas_call(
        functools.partial(_attn_kernel, rate=rate, with_prev=with_prev),
        grid=(B, nt),
        in_specs=in_specs,
        out_specs=[pl.BlockSpec((tm, D), lambda b, i: (b * nt + i, 0)),
                   pl.BlockSpec((tm, LANES), lambda b, i: (b * nt + i, 0))],
        out_shape=[jax.ShapeDtypeStruct((B * S, D), BF16),
                   jax.ShapeDtypeStruct((B * S, LANES), F32)],
        scratch_shapes=[pltpu.VMEM((D // LANES, tm if rate > 1 else 8, LANES), F32)],
        compiler_params=_params("parallel", "parallel"),
        name="attn_rate%d" % rate,
    )(*args)


def kernel(x, ret_w_in, ret_w_out, kv_norm, w_kv, k_norm, dil_wq, q_norm, dil_wo, rel_bias,
           mixer_norm, ffn_norm, router_grp, router_grp_b, router_exp, router_exp_b,
           exp_gate, exp_up, exp_down):
    B, S, D = x.shape
    h = x.reshape(B * S, D)

    def moe_layer(layer, a, w_out, h):
        h1, *hn, route, route_t, counts = _post(
            a, w_out.astype(BF16), h, ffn_norm[layer], router_grp[layer], router_grp_b[layer],
            router_exp[layer], router_exp_b[layer])
        return _moe(h1, hn, route, route_t, counts, layer, exp_gate, exp_up, exp_down)

    proj = _ret_proj(h, mixer_norm[0], ret_w_in[0].astype(BF16), S)
    y = _retention(proj, B, S)
    h = moe_layer(0, y, ret_w_out[0], h)

    G = len(DIL_RATES)
    gd = DIL_HEADS * DIL_HEAD_DIM
    outs, lses = [], []
    for g in range(G):
        cq = slice(g * gd, (g + 1) * gd)
        cv = slice(G * gd + g * gd, G * gd + (g + 1) * gd)
        q, k, vt = _qkv(h, B, S, DIL_RATES[g], mixer_norm[1], kv_norm,
                        dil_wq[0][:, cq].astype(BF16), w_kv[:, cq].astype(BF16), w_kv[:, cv].astype(BF16),
                        q_norm[0][g], k_norm[g])
        o, lse = _group_attention(q, k, vt, rel_bias[:, g * DIL_HEADS:(g + 1) * DIL_HEADS],
                                  DIL_RATES[g], DIL_WINDOWS[g] // DIL_RATES[g])
        outs.append(o)
        lses.append(lse)
    h = moe_layer(1, (tuple(outs), tuple(lses)), dil_wo[0], h)
    return h.reshape(B, S, D)
```

```python
import functools
import math

import jax
import jax.numpy as jnp
from jax import lax
from jax.experimental import pallas as pl
from jax.experimental.pallas import tpu as pltpu
from jax.experimental.pallas import tpu_sc as plsc

F32 = jnp.float32
BF16 = jnp.bfloat16

EPS = 1e-6
NEG_INF = -1e30

RET_HEADS = 4
RET_CHUNK = 256
ROPE_BASE = 10000.0

DIL_WINDOWS = (128, 512, 2048)
DIL_RATES = (1, 4, 16)
DIL_HEADS = 16
DIL_HEAD_DIM = 64
SUB_BLOCK = 128
NUM_BUCKETS = 32
MAX_DISTANCE = 2048

MOE_GROUPS = 4
EXPERTS_PER_GROUP = 8
N_EXPERTS = MOE_GROUPS * EXPERTS_PER_GROUP
MOE_BLOCK = 512

LANES = 128
ROW_TILE = 1024
TOKEN_TILE = 512
VMEM_LIMIT = 56 * 1024 * 1024

NT_DIMS = (((1,), (1,)), ((), ()))
LOG2E = math.log2(math.e)
LN2 = math.log(2.0)


def _params(*sem):
    return pltpu.CompilerParams(dimension_semantics=sem, vmem_limit_bytes=VMEM_LIMIT)


def _resident(shape, index_map):
    return pl.BlockSpec(shape, index_map, pipeline_mode=pl.Buffered(1))


def _rms(x):
    return x * lax.rsqrt(jnp.mean(x * x, axis=-1, keepdims=True) + EPS)


def _ret_proj_kernel(x_ref, g_ref, w_ref, cos_ref, sin_ref, o_ref, *, col_tile, dk, n_qk, n_gate0, k_scale):
    xn = (_rms(x_ref[...]) * g_ref[...]).astype(BF16)
    half = dk // 2
    cos, sin = cos_ref[...], sin_ref[...]
    for j in range(w_ref.shape[1] // col_tile):
        c0 = j * col_tile
        t = jnp.dot(xn, w_ref[:, c0:c0 + col_tile], preferred_element_type=F32)
        if c0 < 2 * n_qk:
            heads = []
            for h0 in range(0, col_tile, dk):
                t1, t2 = t[:, h0:h0 + half], t[:, h0 + half:h0 + dk]
                heads += [t1 * cos - t2 * sin, t1 * sin + t2 * cos]
            t = jnp.concatenate(heads, axis=1)
            if c0 >= n_qk:
                t = t * k_scale
        elif c0 >= n_gate0:
            t = t * jax.nn.sigmoid(t)
        o_ref[:, c0:c0 + col_tile] = t.astype(o_ref.dtype)


def _ret_proj(x, gain, w, S):
    T, D = x.shape
    N = w.shape[1]
    H = RET_HEADS
    dk = D // H
    half = dk // 2
    tm = min(ROW_TILE, S)
    pos = jnp.arange(S, dtype=F32)
    inv = 1.0 / (ROPE_BASE ** jnp.linspace(0.0, 1.0, half, dtype=F32))
    ang = pos[:, None] * inv[None, :]
    table = pl.BlockSpec((tm, half), lambda i: (i % (S // tm), 0))
    return pl.pallas_call(
        functools.partial(_ret_proj_kernel, col_tile=512, dk=dk, n_qk=H * dk, n_gate0=N - 2 * D,
                          k_scale=dk ** -0.5),
        grid=(T // tm,),
        in_specs=[pl.BlockSpec((tm, D), lambda i: (i, 0)),
                  _resident((1, D), lambda i: (0, 0)),
                  _resident((D, N), lambda i: (0, 0)),
                  table, table],
        out_specs=pl.BlockSpec((tm, N), lambda i: (i, 0)),
        out_shape=jax.ShapeDtypeStruct((T, N), BF16),
        compiler_params=_params("parallel"),
        name="norm_proj",
    )(x, gain.reshape(1, D), w, jnp.cos(ang), jnp.sin(ang))


def _retention_kernel(q_ref, k_ref, v_ref, gate_ref, din_ref, xi_ref, zeta_ref, cd_ref, o_ref, state_ref):
    C = RET_CHUNK
    heads = state_ref.shape[0]
    dk = q_ref.shape[2] // heads
    dv = v_ref.shape[2] // heads
    ts = q_ref.shape[1]

    @pl.when(pl.program_id(1) == 0)
    def _():
        state_ref[...] = jnp.zeros_like(state_ref)

    for c in range(ts // C):
        rows = slice(c * C, (c + 1) * C)
        for hh in range(heads):
            qk_cols = slice(hh * dk, (hh + 1) * dk)
            v_cols = slice(hh * dv, (hh + 1) * dv)
            q = q_ref[0, rows, qk_cols]
            k = k_ref[0, rows, qk_cols]
            v = v_ref[0, rows, v_cols]
            s = lax.dot_general(q, k, NT_DIMS, preferred_element_type=F32) * din_ref[hh]
            inner = jnp.dot(s.astype(BF16), v, preferred_element_type=F32)
            state = state_ref[hh]
            cross = jnp.dot(q, state.astype(BF16), preferred_element_type=F32) * xi_ref[hh]
            kz_t = (k.astype(F32) * zeta_ref[hh]).T.astype(BF16)
            state_ref[hh] = state * cd_ref[hh] + jnp.dot(kz_t, v, preferred_element_type=F32)
            o = _rms(inner + cross)
            o_ref[0, rows, v_cols] = (gate_ref[0, rows, v_cols].astype(F32) * o).astype(o_ref.dtype)


def _retention(proj, B, S):
    H, C = RET_HEADS, RET_CHUNK
    D = proj.shape[1] // 6
    dk, dv = D // H, 2 * D // H
    ts = min(ROW_TILE, S)
    proj = proj.reshape(B, S, 6 * D)

    log_g = jnp.log(1.0 - 2.0 ** (-5.0 - jnp.arange(H, dtype=F32)))
    idx = jnp.arange(C, dtype=F32)
    diff = idx[:, None] - idx[None, :]
    d_in = jnp.where(diff >= 0, jnp.exp(log_g[:, None, None] * jnp.maximum(diff, 0.0)), 0.0)
    xi = jnp.exp(log_g[:, None] * (idx + 1.0))[:, :, None]
    zeta = jnp.exp(log_g[:, None] * (C - 1.0 - idx))[:, :, None]
    chunk_decay = jnp.exp(log_g * C)[:, None, None]

    wq, wv = H * dk, H * dv
    per_head = lambda shape: _resident((H,) + shape, lambda b, i: (0, 0, 0))
    out = pl.pallas_call(
        _retention_kernel,
        grid=(B, S // ts),
        in_specs=[pl.BlockSpec((1, ts, wq), lambda b, i: (b, i, 0)),
                  pl.BlockSpec((1, ts, wq), lambda b, i: (b, i, 1)),
                  pl.BlockSpec((1, ts, wv), lambda b, i: (b, i, 1)),
                  pl.BlockSpec((1, ts, wv), lambda b, i: (b, i, 2)),
                  per_head((C, C)), per_head((C, 1)), per_head((C, 1)), per_head((1, 1))],
        out_specs=pl.BlockSpec((1, ts, wv), lambda b, i: (b, i, 0)),
        out_shape=jax.ShapeDtypeStruct((B, S, H * dv), BF16),
        scratch_shapes=[pltpu.VMEM((H, dk, dv), F32)],
        compiler_params=_params("parallel", "arbitrary"),
        name="retention",
    )(proj, proj, proj, proj, d_in, xi, zeta, chunk_decay)
    return out.reshape(B * S, H * dv)


ROUTE_E1, ROUTE_E2, ROUTE_G1, ROUTE_G2, ROUTE_R1, ROUTE_R2 = range(6)
ROUTE_ROWS = 8
ROUTER_EXP_LANE0 = MOE_GROUPS
HI16 = 0xFFFF0000


def _pack_bf16_pair(a, b):
    ua = lax.bitcast_convert_type(a.astype(BF16).astype(F32), jnp.uint32)
    ub = lax.bitcast_convert_type(b.astype(BF16).astype(F32), jnp.uint32)
    return ua | (ub >> 16)


def _unpack_bf16_pair(w):
    a = lax.bitcast_convert_type(w & jnp.uint32(HI16), F32)
    b = lax.bitcast_convert_type(w << 16, F32)
    return a, b


ROW_PARTS = 2


def _part_cols(D, part):
    w = D // 2 // ROW_PARTS
    return slice(part * w, (part + 1) * w), slice(D // 2 + part * w, D // 2 + (part + 1) * w)


def _pack_part(x, part):
    hi, lo = _part_cols(x.shape[1], part)
    return _pack_bf16_pair(x[:, hi], x[:, lo])


def _merge_groups(o_refs, l_refs, ex_ref):
    ex = ex_ref[...]

    def expand(w):
        hi = w.astype(BF16)
        lo = (w - hi.astype(F32)).astype(BF16)
        return jnp.dot(hi, ex, preferred_element_type=F32) + jnp.dot(lo, ex, preferred_element_type=F32)

    lses = [l_ref[...] for l_ref in l_refs]
    top = functools.reduce(jnp.maximum, lses)
    ws = [jnp.exp(l - top) for l in lses]
    den = sum(ws)
    return sum(expand(w / den) * o_ref[...].astype(F32) for w, o_ref in zip(ws, o_refs)).astype(BF16)


def _post_kernel(*refs, n_groups):
    n_a = 2 * n_groups + 1 if n_groups else 1
    a_refs = refs[:n_a]
    (w_ref, h_ref, g_ref, wr_ref, br_ref, tri_ref,
     h1_ref, hn0_ref, hn1_ref, route_ref, route_t_ref, cnt_ref, carry_ref) = refs[n_a:]
    hn_refs = (hn0_ref, hn1_ref)
    if n_groups:
        a = _merge_groups(a_refs[:n_groups], a_refs[n_groups:2 * n_groups], a_refs[-1])
    else:
        a = a_refs[0][...]
    @pl.when(pl.program_id(0) == 0)
    def _():
        carry_ref[...] = jnp.zeros_like(carry_ref)

    half = h_ref.shape[0] // 2
    D = h_ref.shape[1]
    lane = lax.broadcasted_iota(jnp.int32, (half, LANES), 1).astype(F32)
    ninf = -jnp.inf

    def first_argmax(vals):
        top = jnp.max(vals, axis=1, keepdims=True)
        where = jnp.min(jnp.where(vals == top, lane, float(LANES)), axis=1, keepdims=True)
        return top, where

    h1_ref[...] = h_ref[...] + jnp.dot(a, w_ref[...], preferred_element_type=F32)

    def route_half(rows):
        hn = _rms(h1_ref[rows, :]) * g_ref[...]
        for part, ref in enumerate(hn_refs):
            ref[rows, :] = _pack_part(hn, part)
        hi = hn.astype(BF16)
        lo = (hn - hi.astype(F32)).astype(BF16)
        logits = jnp.dot(jnp.concatenate([hi, hi, lo], axis=1), wr_ref[...],
                         preferred_element_type=F32) + br_ref[...]

        is_grp = lane < MOE_GROUPS
        lg = jnp.where(is_grp, logits, ninf)
        mg, grp = first_argmax(lg)
        p_grp = 1.0 / jnp.sum(jnp.where(is_grp, jnp.exp(lg - mg), 0.0), axis=1, keepdims=True)

        e_lane = lane - ROUTER_EXP_LANE0
        in_grp = (e_lane < N_EXPERTS) & (jnp.floor(e_lane * (1.0 / EXPERTS_PER_GROUP)) == grp)
        le = jnp.where(in_grp, logits, ninf)
        v1, i1 = first_argmax(le)
        le2 = jnp.where(lane == i1, ninf, le)
        v2, i2 = first_argmax(le2)
        e = jnp.exp(v2 - v1)
        hit1 = lane == (i1 - ROUTER_EXP_LANE0)
        hit2 = lane == (i2 - ROUTER_EXP_LANE0)
        onehot = jnp.where(hit1 | hit2, 1.0, 0.0)
        earlier = jnp.dot(tri_ref[...], onehot.astype(BF16), preferred_element_type=F32)
        return dict(e1=i1 - ROUTER_EXP_LANE0, e2=i2 - ROUTER_EXP_LANE0, g1=p_grp / (1.0 + e),
                    g2=p_grp * e / (1.0 + e), hit1=hit1, hit2=hit2, earlier=earlier,
                    count=jnp.sum(onehot, axis=0, keepdims=True))

    halves = [route_half(slice(0, half)), route_half(slice(half, 2 * half))]
    carry = carry_ref[...]
    for j, r in enumerate(halves):
        before = carry + r["earlier"]
        r1 = jnp.sum(jnp.where(r["hit1"], before, 0.0), axis=1, keepdims=True)
        r2 = jnp.sum(jnp.where(r["hit2"], before, 0.0), axis=1, keepdims=True)
        carry = carry + r["count"]
        route = jnp.zeros((half, LANES), F32)
        for slot, val in ((ROUTE_E1, r["e1"]), (ROUTE_E2, r["e2"]), (ROUTE_G1, r["g1"]),
                          (ROUTE_G2, r["g2"]), (ROUTE_R1, r1), (ROUTE_R2, r2)):
            route = jnp.where(lane == slot, val, route)
        route_ref[j * half:(j + 1) * half, :] = route
        route_t_ref[:, j * half:(j + 1) * half] = route.T[:ROUTE_ROWS]
    carry_ref[...] = carry
    cnt_ref[...] = carry


def _post(a, w, h, gain, w_grp, b_grp, w_exp, b_exp):
    T, D = h.shape
    K = w.shape[0]
    tm = min(TOKEN_TILE, T)
    row_block = lambda width: pl.BlockSpec((tm, width), lambda i: (i, 0))
    if isinstance(a, tuple):
        outs, lses = a
        n_groups = len(outs)
        ex = jnp.where(jnp.arange(LANES)[:, None] == (jnp.arange(K)[None, :] // DIL_HEAD_DIM), 1.0, 0.0).astype(BF16)
        a_args = (*outs, *lses, ex)
        a_specs = [row_block(K)] * n_groups + [row_block(LANES)] * n_groups + [_resident((LANES, K), lambda i: (0, 0))]
    else:
        n_groups, a_args, a_specs = 0, (a,), [row_block(K)]
    n_r = MOE_GROUPS + N_EXPERTS
    wr = jnp.zeros((D, LANES), F32).at[:, :n_r].set(jnp.concatenate([w_grp, w_exp], axis=1))
    wr_hi = wr.astype(BF16)
    wr_lo = (wr - wr_hi.astype(F32)).astype(BF16)
    wr3 = jnp.concatenate([wr_hi, wr_lo, wr_hi], axis=0)
    br = jnp.zeros((1, LANES), F32).at[0, :n_r].set(jnp.concatenate([b_grp, b_exp]))
    tri = jnp.tril(jnp.ones((tm // 2, tm // 2), BF16), k=-1)
    return pl.pallas_call(
        functools.partial(_post_kernel, n_groups=n_groups),
        grid=(T // tm,),
        in_specs=a_specs + [
                  _resident((K, D), lambda i: (0, 0)),
                  pl.BlockSpec((tm, D), lambda i: (i, 0)),
                  _resident((1, D), lambda i: (0, 0)),
                  _resident((3 * D, LANES), lambda i: (0, 0)),
                  _resident((1, LANES), lambda i: (0, 0)),
                  _resident((tm // 2, tm // 2), lambda i: (0, 0))],
        out_specs=[pl.BlockSpec((tm, D), lambda i: (i, 0))]
                  + [pl.BlockSpec((tm, D // 2 // ROW_PARTS), lambda i: (i, 0))] * ROW_PARTS
                  + [pl.BlockSpec((tm, LANES), lambda i: (i, 0)),
                   pl.BlockSpec((ROUTE_ROWS, tm), lambda i: (0, i)),
                   pl.BlockSpec((1, LANES), lambda i: (0, 0))],
        out_shape=[jax.ShapeDtypeStruct((T, D), F32)]
                  + [jax.ShapeDtypeStruct((T, D // 2 // ROW_PARTS), jnp.uint32)] * ROW_PARTS
                  + [jax.ShapeDtypeStruct((T, LANES), F32),
                   jax.ShapeDtypeStruct((ROUTE_ROWS, T), F32),
                   jax.ShapeDtypeStruct((1, LANES), F32)],
        scratch_shapes=[pltpu.VMEM((1, LANES), F32)],
        compiler_params=_params("arbitrary"),
        name="post_mixer",
    )(*a_args, w, h, gain.reshape(1, D), wr3, br, tri)


SC_WINDOW = 128


def _sc_mesh():
    return plsc.VectorSubcoreMesh(core_axis_name="core", subcore_axis_name="subcore")


def _sc_window_specs(W):
    rows = pl.BlockSpec((SC_WINDOW, W), lambda i: (i, 0))
    idx = pl.BlockSpec((1, SC_WINDOW), lambda i: (0, i))
    return rows, idx


def _sc_pipeline(body, n_rows, in_specs, out_specs):
    return pltpu.emit_pipeline(body, grid=(n_rows // SC_WINDOW,), in_specs=in_specs, out_specs=out_specs,
                               core_axis_name=("core", "subcore"), dimension_semantics=(pltpu.PARALLEL,))


def _dispatch(xs, dest, pad_idx, P):
    T, W = xs[0].shape
    n_pad = pad_idx.shape[1]
    n = len(xs)
    rows, idx = _sc_window_specs(W)
    zero_rows = pl.BlockSpec((SC_WINDOW, W), lambda i: (0, 0))
    out = jax.ShapeDtypeStruct((P, W), xs[0].dtype)

    @functools.partial(pl.kernel, out_type=(out,) * n, mesh=_sc_mesh(), scratch_types=[], name="dispatch")
    def scatter(*refs):
        x_hbm, (d0_hbm, d1_hbm, z_hbm, p_hbm), o_hbm = refs[:n], refs[n:n + 4], refs[n + 4:]
        for x, o in zip(x_hbm, o_hbm):
            def put_pair(x_vmem, i0_vmem, i1_vmem, o=o):
                def both_in_flight(sem0, sem1):
                    first = pltpu.async_copy(x_vmem, o.at[i0_vmem.at[0]], sem0)
                    second = pltpu.async_copy(x_vmem, o.at[i1_vmem.at[0]], sem1)
                    first.wait()
                    second.wait()

                pl.run_scoped(both_in_flight, pltpu.SemaphoreType.DMA, pltpu.SemaphoreType.DMA)

            def put(x_vmem, i_vmem, o=o):
                pltpu.sync_copy(x_vmem, o.at[i_vmem.at[0]])

            _sc_pipeline(put_pair, T, [rows, idx, idx], [])(x, d0_hbm, d1_hbm)
            _sc_pipeline(put, n_pad, [zero_rows, idx], [])(z_hbm, p_hbm)

    return scatter(*xs, dest[0:1], dest[1:2], jnp.zeros((SC_WINDOW, W), xs[0].dtype), pad_idx)


WEIGHT_LEADS = (3, 2, 1)
EXPERT_SLOTS = max(WEIGHT_LEADS) + 1


def _expert_kernel(blk_exp_ref, slot_ref, n_used_ref, *refs):
    x_refs = refs[:ROW_PARTS]
    w_refs = refs[ROW_PARTS:ROW_PARTS + 3]
    o_refs = refs[ROW_PARTS + 3:2 * ROW_PARTS + 3]
    w_slots = refs[2 * ROW_PARTS + 3:]
    lead = max(WEIGHT_LEADS)
    n_used = n_used_ref[0]
    i = pl.program_id(0) - lead
    D = w_slots[0].shape[1]

    for w_ref, w_s, ahead in zip(w_refs, w_slots, WEIGHT_LEADS):
        j = i + ahead
        jc = jnp.clip(j, 0, n_used - 1)
        arrived = (j >= 0) & (j < n_used) & ((j == 0) | (blk_exp_ref[jc] != blk_exp_ref[jnp.maximum(jc - 1, 0)]))

        @pl.when(arrived)
        def _(w_ref=w_ref, w_s=w_s, jc=jc):
            w_s[slot_ref[jc]] = w_ref[0, 0].astype(BF16)

    @pl.when((i >= 0) & (i < n_used))
    def _():
        slot = slot_ref[jnp.clip(i, 0, n_used - 1)]
        wg_s, wu_s, wd_s = w_slots
        pieces = []
        for part, x_ref in enumerate(x_refs):
            for cols, val in zip(_part_cols(D, part), _unpack_bf16_pair(x_ref[...])):
                pieces.append((cols, val.astype(BF16)))

        def up(w_s):
            return sum(jnp.dot(val, w_s[slot, cols, :], preferred_element_type=F32) for cols, val in pieces)

        g = up(wg_s)
        hid = (g * jax.nn.sigmoid(g) * up(wu_s)).astype(BF16)
        y = jnp.dot(hid, wd_s[slot], preferred_element_type=F32)
        for part, o_ref in enumerate(o_refs):
            o_ref[...] = _pack_part(y, part)

    @pl.when(i >= n_used)
    def _():
        for o_ref in o_refs:
            o_ref[...] = jnp.zeros_like(o_ref)


def _experts(xs, blk_exp, n_used, layer, w_gate, w_up, w_down):
    P, W = xs[0].shape
    D = 2 * W * ROW_PARTS
    FF = w_gate.shape[3]
    nblk = P // MOE_BLOCK
    lead = max(WEIGHT_LEADS)
    changes = jnp.concatenate([jnp.zeros((1,), jnp.int32), (blk_exp[1:] != blk_exp[:-1]).astype(jnp.int32)])
    slot = jnp.cumsum(changes) % EXPERT_SLOTS

    def x_map(g, be, sl, nu):
        return (jnp.clip(g - lead, 0, nu[0] - 1), 0)

    def o_map(g, be, sl, nu):
        return (jnp.maximum(g - lead, 0), 0)

    def w_map(ahead):
        return lambda g, be, sl, nu: (layer, be[jnp.clip(g - lead + ahead, 0, nu[0] - 1)], 0, 0)

    wg_spec, wu_spec, wd_spec = (pl.BlockSpec((1, 1) + shape, w_map(ahead))
                                 for shape, ahead in zip(((D, FF), (D, FF), (FF, D)), WEIGHT_LEADS))
    return pl.pallas_call(
        _expert_kernel,
        grid_spec=pltpu.PrefetchScalarGridSpec(
            num_scalar_prefetch=3,
            grid=(nblk + lead,),
            in_specs=[pl.BlockSpec((MOE_BLOCK, W), x_map)] * ROW_PARTS + [wg_spec, wu_spec, wd_spec],
            out_specs=[pl.BlockSpec((MOE_BLOCK, W), o_map)] * ROW_PARTS,
            scratch_shapes=[pltpu.VMEM((EXPERT_SLOTS, D, FF), BF16), pltpu.VMEM((EXPERT_SLOTS, D, FF), BF16),
                            pltpu.VMEM((EXPERT_SLOTS, FF, D), BF16)]),
        out_shape=[jax.ShapeDtypeStruct((P, W), jnp.uint32)] * ROW_PARTS,
        compiler_params=_params("arbitrary"),
        name="experts",
    )(blk_exp, slot, n_used, *xs, w_gate, w_up, w_down)


def _gather_pairs(ys, dest):
    W = ys[0].shape[1]
    T = dest.shape[1]
    n = len(ys)
    rows, idx = _sc_window_specs(W)
    out = jax.ShapeDtypeStruct((T, W), ys[0].dtype)

    @functools.partial(pl.kernel, out_type=(out,) * (2 * n), mesh=_sc_mesh(), scratch_types=[],
                       name="gather_pairs")
    def gather(*refs):
        y_hbm, d_hbm, o_hbm = refs[:n], refs[n:n + 2], refs[n + 2:]
        for slot, d in enumerate(d_hbm):
            for y, o in zip(y_hbm, o_hbm[slot * n:(slot + 1) * n]):
                def get(i_vmem, o_vmem, y=y):
                    pltpu.sync_copy(y.at[i_vmem.at[0]], o_vmem)

                _sc_pipeline(get, T, [idx], [rows])(d, o)

    return gather(*ys, dest[0:1], dest[1:2])


def _combine_kernel(h_ref, route_ref, *refs):
    y_refs, o_ref = refs[:-1], refs[-1]
    D = h_ref.shape[1]
    route = route_ref[...]
    gates = (route[:, ROUTE_G1:ROUTE_G1 + 1], route[:, ROUTE_G2:ROUTE_G2 + 1])
    for part in range(ROW_PARTS):
        slots = [_unpack_bf16_pair(y_refs[slot * ROW_PARTS + part][...]) for slot in range(2)]
        for half, cols in enumerate(_part_cols(D, part)):
            o_ref[:, cols] = h_ref[:, cols] + (gates[0] * slots[0][half] + gates[1] * slots[1][half])


def _combine(h, route, pairs):
    T, D = h.shape
    W = pairs[0].shape[1]
    tm = min(ROW_TILE, T)
    return pl.pallas_call(
        _combine_kernel,
        grid=(T // tm,),
        in_specs=[pl.BlockSpec((tm, D), lambda i: (i, 0)), pl.BlockSpec((tm, LANES), lambda i: (i, 0))]
                 + [pl.BlockSpec((tm, W), lambda i: (i, 0))] * len(pairs),
        out_specs=pl.BlockSpec((tm, D), lambda i: (i, 0)),
        out_shape=jax.ShapeDtypeStruct((T, D), F32),
        compiler_params=_params("parallel"),
        name="combine",
    )(h, route, *pairs)


def _moe(h1, hn, route, route_t, counts, layer, w_gate, w_up, w_down):
    T, D = h1.shape
    A = 2 * T
    nblk = -(-A // MOE_BLOCK) + N_EXPERTS
    P = nblk * MOE_BLOCK
    eid = route_t[ROUTE_E1:ROUTE_E2 + 1].astype(jnp.int32)
    rank = route_t[ROUTE_R1:ROUTE_R2 + 1].astype(jnp.int32)
    cnt = counts[0, :N_EXPERTS].astype(jnp.int32)
    padded = (cnt + MOE_BLOCK - 1) // MOE_BLOCK * MOE_BLOCK
    pends = jnp.cumsum(padded)
    pstarts = pends - padded
    experts = jnp.arange(N_EXPERTS, dtype=jnp.int32)
    start_of = jnp.sum(jnp.where(eid[:, None, :] == experts[None, :, None], pstarts[None, :, None], 0), axis=1)
    dest = start_of + rank
    blk_start = jnp.arange(nblk, dtype=jnp.int32) * MOE_BLOCK
    blk_exp = jnp.minimum(jnp.sum((pends[None, :] <= blk_start[:, None]).astype(jnp.int32), axis=1),
                          N_EXPERTS - 1)
    n_used = pends[-1:] // MOE_BLOCK

    gap_start = jnp.concatenate([pstarts + cnt, pends[-1:]])
    gap_len = jnp.concatenate([padded - cnt, P - pends[-1:]])
    gap_end = jnp.cumsum(gap_len)
    j = jnp.arange(P - A, dtype=jnp.int32)
    gap_of = jnp.sum((gap_end[None, :] <= j[:, None]).astype(jnp.int32), axis=1)
    sel = gap_of[:, None] == jnp.arange(N_EXPERTS + 1, dtype=jnp.int32)[None, :]
    pad_idx = jnp.sum(jnp.where(sel, (gap_start - (gap_end - gap_len))[None, :] + j[:, None], 0), axis=1)

    xs = _dispatch(hn, dest, pad_idx.reshape(1, P - A), P)
    ys = _experts(xs, blk_exp, n_used, layer, w_gate, w_up, w_down)
    return _combine(h1, route, _gather_pairs(ys, dest))


def _qkv_kernel(x_ref, gq_ref, gkv_ref, wq_ref, wk_ref, wvt_ref, qn_ref, kn_ref, seg_ref,
                q_ref, k_ref, vt_ref, xs_ref, *, rate, n, n_chunks):
    seg = seg_ref[...]
    width = seg.shape[0]
    res_per_chunk = q_ref.shape[1]
    chunk = pl.program_id(2)

    def head_norm(t, gain):
        cols = []
        for j in range(t.shape[1] // width):
            tj = t[:, j * width:(j + 1) * width]
            ms = jnp.dot((tj * tj).astype(BF16), seg, preferred_element_type=F32)
            cols.append(tj * lax.rsqrt(ms + EPS))
        return jnp.concatenate(cols, axis=1) * gain

    def project(x):
        y = _rms(x)
        xq = (y * gq_ref[...]).astype(BF16)
        xkv = (y * gkv_ref[...]).astype(BF16)
        q = head_norm(jnp.dot(xq, wq_ref[...], preferred_element_type=F32), qn_ref[...]).astype(q_ref.dtype)
        k = head_norm(jnp.dot(xkv, wk_ref[...], preferred_element_type=F32), kn_ref[...]).astype(k_ref.dtype)
        vt = lax.dot_general(wvt_ref[...], xkv, NT_DIMS, preferred_element_type=F32).astype(vt_ref.dtype)
        for j in range(res_per_chunk):
            q_ref[0, j] = q[j * n:(j + 1) * n]
            k_ref[0, j] = k[j * n:(j + 1) * n]
            vt_ref[0, j] = vt[:, j * n:(j + 1) * n]

    if rate == 1:
        project(x_ref[...])
        return

    @pl.when(chunk == 0)
    def _():
        for j in range(xs_ref.shape[0]):
            xs_ref[j] = x_ref[:, j * LANES:(j + 1) * LANES]

    for ch in range(n_chunks):
        @pl.when(chunk == ch)
        def _(ch=ch):
            residues = range(ch * res_per_chunk, (ch + 1) * res_per_chunk)
            project(jnp.concatenate(
                [jnp.concatenate([xs_ref[j, pl.ds(c, n, stride=rate), :] for j in range(xs_ref.shape[0])], axis=1)
                 for c in residues], axis=0))


def _qkv(h, B, S, rate, gq, gkv, wq, wk, wv, qn, kn):
    D = h.shape[1]
    L = S // rate
    hd = DIL_HEAD_DIM
    n = max(SUB_BLOCK, TOKEN_TILE // rate)
    tm = n * rate
    res_per_chunk = max(1, TOKEN_TILE // n)
    n_chunks = rate // res_per_chunk
    width = 2 * LANES
    ii = jnp.arange(width)
    seg = jnp.where((ii[:, None] // hd) == (ii[None, :] // hd), 1.0 / hd, 0.0).astype(BF16)
    row = lambda g: jnp.tile(g, D // hd).reshape(1, D)
    const = lambda shape: _resident(shape, lambda b, i, c: (0,) * len(shape))
    qk_spec = pl.BlockSpec((1, res_per_chunk, n, D), lambda b, i, c: (b, c, i, 0))
    qk_shape = jax.ShapeDtypeStruct((B, rate, L, D), BF16)
    return pl.pallas_call(
        functools.partial(_qkv_kernel, rate=rate, n=n, n_chunks=n_chunks),
        grid=(B, S // tm, n_chunks),
        in_specs=[pl.BlockSpec((tm, D), lambda b, i, c: (b * (S // tm) + i, 0)),
                  const((1, D)), const((1, D)), const((D, D)), const((D, D)), const((D, D)),
                  const((1, D)), const((1, D)), const((width, width))],
        out_specs=[qk_spec, qk_spec,
                   pl.BlockSpec((1, res_per_chunk, D, n), lambda b, i, c: (b, c, 0, i))],
        out_shape=[qk_shape, qk_shape, jax.ShapeDtypeStruct((B, rate, D, L), BF16)],
        scratch_shapes=[pltpu.VMEM((D // LANES, tm if rate > 1 else 8, LANES), F32)],
        compiler_params=_params("parallel", "parallel", "arbitrary"),
        name="qkv_rate%d" % rate,
    )(h, gq.reshape(1, D), gkv.reshape(1, D), wq, wk, wv.T, row(qn) * (hd ** -0.5 * LOG2E), row(kn), seg)


def _attn_kernel(*refs, rate, with_prev):
    if rate == 1:
        q_ref, kc_ref, kp_ref, vc_ref, vp_ref, edge_ref, bias_ref, o_ref, lse_ref, os_ref = refs
    elif with_prev:
        q_ref, kc_ref, kp_ref, vc_ref, vp_ref, edge_ref, o_ref, lse_ref, os_ref = refs
    else:
        q_ref, kc_ref, vc_ref, edge_ref, o_ref, lse_ref, os_ref = refs
    Bk = SUB_BLOCK
    hd = DIL_HEAD_DIM
    n_pairs = DIL_HEADS // 2
    lane = lax.broadcasted_iota(jnp.int32, (Bk, LANES), 1)

    def block(load_q, load_k, load_vt, load_bias, store):
        out_t, lse_t = [], []
        for p in range(n_pairs):
            qp = load_q(p)
            zero = jnp.zeros_like(qp)
            q2 = jnp.concatenate([jnp.where(lane < hd, qp, zero), jnp.where(lane >= hd, qp, zero)], axis=0)
            s = lax.dot_general(load_k(p), q2, NT_DIMS, preferred_element_type=F32) + load_bias(p)
            m = jnp.max(s, axis=0, keepdims=True)
            pr = jnp.exp2(s - m)
            l = jnp.sum(pr, axis=0, keepdims=True)
            pb = pr.astype(BF16)
            vt = load_vt(p)
            out_t.append(jnp.dot(vt[:hd], pb[:, :Bk], preferred_element_type=F32) / l[:, :Bk])
            out_t.append(jnp.dot(vt[hd:], pb[:, Bk:], preferred_element_type=F32) / l[:, Bk:])
            lse = (m + jnp.log2(l)) * LN2
            lse_t += [lse[:, :Bk], lse[:, Bk:]]
        lse_t.append(jnp.zeros((LANES - DIL_HEADS, Bk), F32))
        store(jnp.concatenate(out_t, axis=0).T, jnp.concatenate(lse_t, axis=0).T)

    def pair_cols(p):
        return slice(p * LANES, (p + 1) * LANES)

    edge_bias = lambda p: edge_ref[0, p]
    if rate == 1:
        for j in range(q_ref.shape[2] // Bk):
            rows = slice(j * Bk, (j + 1) * Bk)
            if j == 0:
                load_k = lambda p: jnp.concatenate([kp_ref[0, 0, :, pair_cols(p)],
                                                    kc_ref[0, 0, :Bk, pair_cols(p)]], axis=0)
                load_vt = lambda p: jnp.concatenate([vp_ref[0, 0, pair_cols(p), :],
                                                     vc_ref[0, 0, pair_cols(p), :Bk]], axis=1)
                load_bias = edge_bias
            else:
                krows = slice((j - 1) * Bk, (j + 1) * Bk)
                load_k = lambda p, krows=krows: kc_ref[0, 0, krows, pair_cols(p)]
                load_vt = lambda p, krows=krows: vc_ref[0, 0, pair_cols(p), krows]
                load_bias = lambda p: bias_ref[p]

            def store(o, lse, rows=rows):
                o_ref[rows, :] = o.astype(o_ref.dtype)
                lse_ref[rows, :] = lse

            block(lambda p, rows=rows: q_ref[0, 0, rows, pair_cols(p)], load_k, load_vt, load_bias, store)
    else:
        for c in range(q_ref.shape[1]):
            if with_prev:
                load_k = lambda p, c=c: jnp.concatenate([kp_ref[0, c, :, pair_cols(p)],
                                                         kc_ref[0, c, :, pair_cols(p)]], axis=0)
                load_vt = lambda p, c=c: jnp.concatenate([vp_ref[0, c, pair_cols(p), :],
                                                          vc_ref[0, c, pair_cols(p), :]], axis=1)
            else:
                load_k = lambda p, c=c: kc_ref[0, c, :, pair_cols(p)]
                load_vt = lambda p, c=c: vc_ref[0, c, pair_cols(p), :]

            def store(o, lse, c=c):
                for j in range(os_ref.shape[0]):
                    os_ref[j, pl.ds(c, Bk, stride=rate), :] = o[:, pair_cols(j)]
                lse_ref[pl.ds(c, Bk, stride=rate), :] = lse

            block(lambda p, c=c: q_ref[0, c, :, pair_cols(p)], load_k, load_vt, edge_bias, store)
        for j in range(os_ref.shape[0]):
            o_ref[:, pair_cols(j)] = os_ref[j].astype(o_ref.dtype)


def _t5_bucket(n):
    max_exact = NUM_BUCKETS // 2
    nf = jnp.maximum(n, max_exact).astype(F32)
    large = max_exact + (jnp.log(nf / max_exact) / math.log(MAX_DISTANCE / max_exact)
                         * (NUM_BUCKETS - max_exact)).astype(jnp.int32)
    large = jnp.minimum(large, NUM_BUCKETS - 1)
    return jnp.where(n < max_exact, n, large)


def _group_attention(q, k, vt, bias_table, rate, n_steps):
    B, _, L, D = q.shape
    S = L * rate
    Bk = SUB_BLOCK
    with_prev = L > Bk
    nk = 2 * Bk if with_prev else Bk
    n = max(Bk, TOKEN_TILE // rate)
    tm = n * rate
    nt = S // tm
    n_pairs = DIL_HEADS // 2
    ql = jnp.arange(Bk, dtype=jnp.int32)[:, None]
    kl = jnp.arange(2 * Bk, dtype=jnp.int32)[None, :]
    steps = ql + Bk - kl
    bucket = _t5_bucket(jnp.maximum(steps, 0) * rate)
    buckets = jnp.arange(NUM_BUCKETS, dtype=jnp.int32)
    bias = jnp.sum(jnp.where(bucket[None, :, :, None] == buckets[:, None, None, None],
                             bias_table.astype(F32)[:, None, None, :], 0.0), axis=0)
    band = ((steps >= 0) & (steps <= n_steps))[:, :, None]
    first = (kl < Bk)[:, :, None]

    def layout(t):
        t = t[:, 2 * Bk - nk:, :].transpose(1, 2, 0)
        return t.reshape(nk, n_pairs, 2 * Bk).transpose(1, 0, 2)

    bias_in = layout(jnp.where(band, bias * LOG2E, NEG_INF))
    bias_first = layout(jnp.where(band & ~first, bias * LOG2E, NEG_INF))
    edge = jnp.stack([bias_first, bias_in])

    cur_qk = pl.BlockSpec((1, rate, n, D), lambda b, i: (b, 0, i, 0))
    cur_vt = pl.BlockSpec((1, rate, D, n), lambda b, i: (b, 0, 0, i))
    per_n = n // Bk
    prev_qk = pl.BlockSpec((1, rate, Bk, D), lambda b, i: (b, 0, jnp.maximum(i * per_n - 1, 0), 0))
    prev_vt = pl.BlockSpec((1, rate, D, Bk), lambda b, i: (b, 0, 0, jnp.maximum(i * per_n - 1, 0)))
    edge_spec = pl.BlockSpec((1, n_pairs, nk, 2 * Bk), lambda b, i: (jnp.minimum(i, 1), 0, 0, 0))
    if with_prev:
        in_specs = [cur_qk, cur_qk, prev_qk, cur_vt, prev_vt, edge_spec]
        args = (q, k, k, vt, vt, edge)
    else:
        in_specs = [cur_qk, cur_qk, cur_vt, edge_spec]
        args = (q, k, vt, edge)
    if rate == 1:
        in_specs.append(_resident((n_pairs, nk, 2 * Bk), lambda b, i: (0, 0, 0)))
        args += (bias_in,)
    return pl.pallas_call(
        functools.partial(_attn_kernel, rate=rate, with_prev=with_prev),
        grid=(B, nt),
        in_specs=in_specs,
        out_specs=[pl.BlockSpec((tm, D), lambda b, i: (b * nt + i, 0)),
                   pl.BlockSpec((tm, LANES), lambda b, i: (b * nt + i, 0))],
        out_shape=[jax.ShapeDtypeStruct((B * S, D), BF16),
                   jax.ShapeDtypeStruct((B * S, LANES), F32)],
        scratch_shapes=[pltpu.VMEM((D // LANES, tm if rate > 1 else 8, LANES), F32)],
        compiler_params=_params("parallel", "parallel"),
        name="attn_rate%d" % rate,
    )(*args)


def kernel(x, ret_w_in, ret_w_out, kv_norm, w_kv, k_norm, dil_wq, q_norm, dil_wo, rel_bias,
           mixer_norm, ffn_norm, router_grp, router_grp_b, router_exp, router_exp_b,
           exp_gate, exp_up, exp_down):
    B, S, D = x.shape
    h = x.reshape(B * S, D)

    def moe_layer(layer, a, w_out, h):
        h1, *hn, route, route_t, counts = _post(
            a, w_out.astype(BF16), h, ffn_norm[layer], router_grp[layer], router_grp_b[layer],
            router_exp[layer], router_exp_b[layer])
        return _moe(h1, hn, route, route_t, counts, layer, exp_gate, exp_up, exp_down)

    proj = _ret_proj(h, mixer_norm[0], ret_w_in[0].astype(BF16), S)
    y = _retention(proj, B, S)
    h = moe_layer(0, y, ret_w_out[0], h)

    G = len(DIL_RATES)
    gd = DIL_HEADS * DIL_HEAD_DIM
    outs, lses = [], []
    for g in range(G):
        cq = slice(g * gd, (g + 1) * gd)
        cv = slice(G * gd + g * gd, G * gd + (g + 1) * gd)
        q, k, vt = _qkv(h, B, S, DIL_RATES[g], mixer_norm[1], kv_norm,
                        dil_wq[0][:, cq].astype(BF16), w_kv[:, cq].astype(BF16), w_kv[:, cv].astype(BF16),
                        q_norm[0][g], k_norm[g])
        o, lse = _group_attention(q, k, vt, rel_bias[:, g * DIL_HEADS:(g + 1) * DIL_HEADS],
                                  DIL_RATES[g], DIL_WINDOWS[g] // DIL_RATES[g])
        outs.append(o)
        lses.append(lse)
    h = moe_layer(1, (tuple(outs), tuple(lses)), dil_wo[0], h)
    return h.reshape(B, S, D)
```

```python
import functools
import math

import jax
import jax.numpy as jnp
from jax import lax
from jax.experimental import pallas as pl
from jax.experimental.pallas import tpu as pltpu
from jax.experimental.pallas import tpu_sc as plsc

F32 = jnp.float32
BF16 = jnp.bfloat16

EPS = 1e-6
NEG_INF = -1e30

RET_HEADS = 4
RET_CHUNK = 256
ROPE_BASE = 10000.0

DIL_WINDOWS = (128, 512, 2048)
DIL_RATES = (1, 4, 16)
DIL_HEADS = 16
DIL_HEAD_DIM = 64
SUB_BLOCK = 128
NUM_BUCKETS = 32
MAX_DISTANCE = 2048

MOE_GROUPS = 4
EXPERTS_PER_GROUP = 8
N_EXPERTS = MOE_GROUPS * EXPERTS_PER_GROUP
MOE_BLOCK = 512

LANES = 128
ROW_TILE = 1024
TOKEN_TILE = 512
VMEM_LIMIT = 56 * 1024 * 1024

NT_DIMS = (((1,), (1,)), ((), ()))
LOG2E = math.log2(math.e)
LN2 = math.log(2.0)


def _params(*sem):
    return pltpu.CompilerParams(dimension_semantics=sem, vmem_limit_bytes=VMEM_LIMIT)


def _resident(shape, index_map):
    return pl.BlockSpec(shape, index_map, pipeline_mode=pl.Buffered(1))


def _rms(x):
    return x * lax.rsqrt(jnp.mean(x * x, axis=-1, keepdims=True) + EPS)


def _retention_kernel(x_ref, g_ref, w_ref, cos_ref, sin_ref, din_ref, xi_ref, zeta_ref, cd_ref,
                      o_ref, state_ref, *, k_scale):
    C = RET_CHUNK
    H, dk, dv = state_ref.shape
    half = dk // 2
    ts = x_ref.shape[0]

    @pl.when(pl.program_id(1) == 0)
    def _():
        state_ref[...] = jnp.zeros_like(state_ref)

    xn = (_rms(x_ref[...]) * g_ref[...]).astype(BF16)
    cos, sin = cos_ref[...], sin_ref[...]

    def proj(c0, width):
        return jnp.dot(xn, w_ref[:, c0:c0 + width], preferred_element_type=F32)

    def rot(t):
        t1, t2 = t[:, :half], t[:, half:]
        return jnp.concatenate([t1 * cos - t2 * sin, t1 * sin + t2 * cos], axis=1)

    for hh in range(H):
        q = rot(proj(hh * dk, dk)).astype(BF16)
        k = rot(proj(H * dk + hh * dk, dk)) * k_scale
        kb = k.astype(BF16)
        v = proj(2 * H * dk + hh * dv, dv).astype(BF16)
        gate = proj(2 * H * dk + H * dv + hh * dv, dv)
        gate = gate * jax.nn.sigmoid(gate)
        for c in range(ts // C):
            rows = slice(c * C, (c + 1) * C)
            s = lax.dot_general(q[rows], kb[rows], NT_DIMS, preferred_element_type=F32) * din_ref[hh]
            inner = jnp.dot(s.astype(BF16), v[rows], preferred_element_type=F32)
            state = state_ref[hh]
            cross = jnp.dot(q[rows], state.astype(BF16), preferred_element_type=F32) * xi_ref[hh]
            kz_t = (k[rows] * zeta_ref[hh]).T.astype(BF16)
            state_ref[hh] = state * cd_ref[hh] + jnp.dot(kz_t, v[rows], preferred_element_type=F32)
            o_ref[0, rows, hh * dv:(hh + 1) * dv] = (gate[rows] * _rms(inner + cross)).astype(o_ref.dtype)


def _retention(x, gain, w, B, S):
    H, C = RET_HEADS, RET_CHUNK
    T, D = x.shape
    dk, dv = D // H, 2 * D // H
    half = dk // 2
    ts = min(TOKEN_TILE, S)
    nt = S // ts

    pos = jnp.arange(S, dtype=F32)
    inv = 1.0 / (ROPE_BASE ** jnp.linspace(0.0, 1.0, half, dtype=F32))
    ang = pos[:, None] * inv[None, :]
    log_g = jnp.log(1.0 - 2.0 ** (-5.0 - jnp.arange(H, dtype=F32)))
    idx = jnp.arange(C, dtype=F32)
    diff = idx[:, None] - idx[None, :]
    d_in = jnp.where(diff >= 0, jnp.exp(log_g[:, None, None] * jnp.maximum(diff, 0.0)), 0.0)
    xi = jnp.exp(log_g[:, None] * (idx + 1.0))[:, :, None]
    zeta = jnp.exp(log_g[:, None] * (C - 1.0 - idx))[:, :, None]
    chunk_decay = jnp.exp(log_g * C)[:, None, None]

    table = pl.BlockSpec((ts, half), lambda b, i: (i, 0))
    per_head = lambda shape: _resident((H,) + shape, lambda b, i: (0, 0, 0))
    out = pl.pallas_call(
        functools.partial(_retention_kernel, k_scale=dk ** -0.5),
        grid=(B, nt),
        in_specs=[pl.BlockSpec((ts, D), lambda b, i: (b * nt + i, 0)),
                  _resident((1, D), lambda b, i: (0, 0)),
                  _resident(w.shape, lambda b, i: (0, 0)),
                  table, table,
                  per_head((C, C)), per_head((C, 1)), per_head((C, 1)), per_head((1, 1))],
        out_specs=pl.BlockSpec((1, ts, H * dv), lambda b, i: (b, i, 0)),
        out_shape=jax.ShapeDtypeStruct((B, S, H * dv), BF16),
        scratch_shapes=[pltpu.VMEM((H, dk, dv), F32)],
        compiler_params=_params("parallel", "arbitrary"),
        name="retention",
    )(x, gain.reshape(1, D), w, jnp.cos(ang), jnp.sin(ang), d_in, xi, zeta, chunk_decay)
    return out.reshape(T, H * dv)


ROUTE_E1, ROUTE_E2, ROUTE_G1, ROUTE_G2, ROUTE_R1, ROUTE_R2 = range(6)
ROUTE_ROWS = 8
ROUTE_PIECES = 2
ROUTER_EXP_LANE0 = MOE_GROUPS
HI16 = 0xFFFF0000


def _pack_bf16_pair(a, b):
    ua = lax.bitcast_convert_type(a.astype(BF16).astype(F32), jnp.uint32)
    ub = lax.bitcast_convert_type(b.astype(BF16).astype(F32), jnp.uint32)
    return ua | (ub >> 16)


def _unpack_bf16_pair(w):
    a = lax.bitcast_convert_type(w & jnp.uint32(HI16), F32)
    b = lax.bitcast_convert_type(w << 16, F32)
    return a, b


ROW_PARTS = 2


def _part_cols(D, part):
    w = D // 2 // ROW_PARTS
    return slice(part * w, (part + 1) * w), slice(D // 2 + part * w, D // 2 + (part + 1) * w)


def _pack_part(x, part):
    hi, lo = _part_cols(x.shape[1], part)
    return _pack_bf16_pair(x[:, hi], x[:, lo])


def _merge_groups(o_refs, l_refs, ex_ref):
    ex = ex_ref[...]

    def expand(w):
        hi = w.astype(BF16)
        lo = (w - hi.astype(F32)).astype(BF16)
        return jnp.dot(hi, ex, preferred_element_type=F32) + jnp.dot(lo, ex, preferred_element_type=F32)

    lses = [l_ref[...] for l_ref in l_refs]
    top = functools.reduce(jnp.maximum, lses)
    ws = [jnp.exp(l - top) for l in lses]
    den = sum(ws)
    return sum(expand(w / den) * o_ref[...].astype(F32) for w, o_ref in zip(ws, o_refs)).astype(BF16)


def _post_kernel(*refs, n_groups):
    n_a = 2 * n_groups + 1 if n_groups else 1
    a_refs = refs[:n_a]
    (w_ref, h_ref, g_ref, wr_ref, br_ref, tri_ref,
     h1_ref, hn0_ref, hn1_ref, route_ref, route_t_ref, cnt_ref, carry_ref) = refs[n_a:]
    hn_refs = (hn0_ref, hn1_ref)
    if n_groups:
        a = _merge_groups(a_refs[:n_groups], a_refs[n_groups:2 * n_groups], a_refs[-1])
    else:
        a = a_refs[0][...]
    @pl.when(pl.program_id(0) == 0)
    def _():
        carry_ref[...] = jnp.zeros_like(carry_ref)

    half = tri_ref.shape[0]
    n_pieces = h_ref.shape[0] // half
    D = h_ref.shape[1]
    lane = lax.broadcasted_iota(jnp.int32, (half, LANES), 1).astype(F32)
    ninf = -jnp.inf

    def first_argmax(vals):
        top = jnp.max(vals, axis=1, keepdims=True)
        where = jnp.min(jnp.where(vals == top, lane, float(LANES)), axis=1, keepdims=True)
        return top, where

    h1_ref[...] = h_ref[...] + jnp.dot(a, w_ref[...], preferred_element_type=F32)

    def route_half(rows):
        hn = _rms(h1_ref[rows, :]) * g_ref[...]
        for part, ref in enumerate(hn_refs):
            ref[rows, :] = _pack_part(hn, part)
        hi = hn.astype(BF16)
        lo = (hn - hi.astype(F32)).astype(BF16)
        logits = jnp.dot(jnp.concatenate([hi, hi, lo], axis=1), wr_ref[...],
                         preferred_element_type=F32) + br_ref[...]

        is_grp = lane < MOE_GROUPS
        lg = jnp.where(is_grp, logits, ninf)
        mg, grp = first_argmax(lg)
        p_grp = 1.0 / jnp.sum(jnp.where(is_grp, jnp.exp(lg - mg), 0.0), axis=1, keepdims=True)

        e_lane = lane - ROUTER_EXP_LANE0
        in_grp = (e_lane < N_EXPERTS) & (jnp.floor(e_lane * (1.0 / EXPERTS_PER_GROUP)) == grp)
        le = jnp.where(in_grp, logits, ninf)
        v1, i1 = first_argmax(le)
        le2 = jnp.where(lane == i1, ninf, le)
        v2, i2 = first_argmax(le2)
        e = jnp.exp(v2 - v1)
        hit1 = lane == (i1 - ROUTER_EXP_LANE0)
        hit2 = lane == (i2 - ROUTER_EXP_LANE0)
        onehot = jnp.where(hit1 | hit2, 1.0, 0.0)
        earlier = jnp.dot(tri_ref[...], onehot.astype(BF16), preferred_element_type=F32)
        return dict(e1=i1 - ROUTER_EXP_LANE0, e2=i2 - ROUTER_EXP_LANE0, g1=p_grp / (1.0 + e),
                    g2=p_grp * e / (1.0 + e), hit1=hit1, hit2=hit2, earlier=earlier,
                    count=jnp.sum(onehot, axis=0, keepdims=True))

    halves = [route_half(slice(j * half, (j + 1) * half)) for j in range(n_pieces)]
    carry = carry_ref[...]
    for j, r in enumerate(halves):
        before = carry + r["earlier"]
        r1 = jnp.sum(jnp.where(r["hit1"], before, 0.0), axis=1, keepdims=True)
        r2 = jnp.sum(jnp.where(r["hit2"], before, 0.0), axis=1, keepdims=True)
        carry = carry + r["count"]
        route = jnp.zeros((half, LANES), F32)
        for slot, val in ((ROUTE_E1, r["e1"]), (ROUTE_E2, r["e2"]), (ROUTE_G1, r["g1"]),
                          (ROUTE_G2, r["g2"]), (ROUTE_R1, r1), (ROUTE_R2, r2)):
            route = jnp.where(lane == slot, val, route)
        route_ref[j * half:(j + 1) * half, :] = route
        route_t_ref[:, j * half:(j + 1) * half] = route.T[:ROUTE_ROWS]
    carry_ref[...] = carry
    cnt_ref[...] = carry


def _post(a, w, h, gain, w_grp, b_grp, w_exp, b_exp):
    T, D = h.shape
    K = w.shape[0]
    tm = min(TOKEN_TILE, T)
    row_block = lambda width: pl.BlockSpec((tm, width), lambda i: (i, 0))
    if isinstance(a, tuple):
        outs, lses = a
        n_groups = len(outs)
        ex = jnp.where(jnp.arange(LANES)[:, None] == (jnp.arange(K)[None, :] // DIL_HEAD_DIM), 1.0, 0.0).astype(BF16)
        a_args = (*outs, *lses, ex)
        a_specs = [row_block(K)] * n_groups + [row_block(LANES)] * n_groups + [_resident((LANES, K), lambda i: (0, 0))]
    else:
        n_groups, a_args, a_specs = 0, (a,), [row_block(K)]
    n_r = MOE_GROUPS + N_EXPERTS
    wr = jnp.zeros((D, LANES), F32).at[:, :n_r].set(jnp.concatenate([w_grp, w_exp], axis=1))
    wr_hi = wr.astype(BF16)
    wr_lo = (wr - wr_hi.astype(F32)).astype(BF16)
    wr3 = jnp.concatenate([wr_hi, wr_lo, wr_hi], axis=0)
    br = jnp.zeros((1, LANES), F32).at[0, :n_r].set(jnp.concatenate([b_grp, b_exp]))
    piece = tm // ROUTE_PIECES
    tri = jnp.tril(jnp.ones((piece, piece), BF16), k=-1)
    return pl.pallas_call(
        functools.partial(_post_kernel, n_groups=n_groups),
        grid=(T // tm,),
        in_specs=a_specs + [
                  _resident((K, D), lambda i: (0, 0)),
                  pl.BlockSpec((tm, D), lambda i: (i, 0)),
                  _resident((1, D), lambda i: (0, 0)),
                  _resident((3 * D, LANES), lambda i: (0, 0)),
                  _resident((1, LANES), lambda i: (0, 0)),
                  _resident((piece, piece), lambda i: (0, 0))],
        out_specs=[pl.BlockSpec((tm, D), lambda i: (i, 0))]
                  + [pl.BlockSpec((tm, D // 2 // ROW_PARTS), lambda i: (i, 0))] * ROW_PARTS
                  + [pl.BlockSpec((tm, LANES), lambda i: (i, 0)),
                   pl.BlockSpec((ROUTE_ROWS, tm), lambda i: (0, i)),
                   pl.BlockSpec((1, LANES), lambda i: (0, 0))],
        out_shape=[jax.ShapeDtypeStruct((T, D), F32)]
                  + [jax.ShapeDtypeStruct((T, D // 2 // ROW_PARTS), jnp.uint32)] * ROW_PARTS
                  + [jax.ShapeDtypeStruct((T, LANES), F32),
                   jax.ShapeDtypeStruct((ROUTE_ROWS, T), F32),
                   jax.ShapeDtypeStruct((1, LANES), F32)],
        scratch_shapes=[pltpu.VMEM((1, LANES), F32)],
        compiler_params=_params("arbitrary"),
        name="post_mixer",
    )(*a_args, w, h, gain.reshape(1, D), wr3, br, tri)


SC_WINDOW = 128


def _sc_mesh():
    return plsc.VectorSubcoreMesh(core_axis_name="core", subcore_axis_name="subcore")


def _sc_window_specs(W):
    rows = pl.BlockSpec((SC_WINDOW, W), lambda i: (i, 0))
    idx = pl.BlockSpec((1, SC_WINDOW), lambda i: (0, i))
    return rows, idx


def _sc_pipeline(body, n_rows, in_specs, out_specs):
    return pltpu.emit_pipeline(body, grid=(n_rows // SC_WINDOW,), in_specs=in_specs, out_specs=out_specs,
                               core_axis_name=("core", "subcore"), dimension_semantics=(pltpu.PARALLEL,))


def _dispatch(xs, dest, pad_idx, P):
    T, W = xs[0].shape
    n_pad = pad_idx.shape[1]
    n = len(xs)
    rows, idx = _sc_window_specs(W)
    zero_rows = pl.BlockSpec((SC_WINDOW, W), lambda i: (0, 0))
    out = jax.ShapeDtypeStruct((P, W), xs[0].dtype)

    @functools.partial(pl.kernel, out_type=(out,) * n, mesh=_sc_mesh(), scratch_types=[], name="dispatch")
    def scatter(*refs):
        x_hbm, (d0_hbm, d1_hbm, z_hbm, p_hbm), o_hbm = refs[:n], refs[n:n + 4], refs[n + 4:]
        for x, o in zip(x_hbm, o_hbm):
            def put_pair(x_vmem, i0_vmem, i1_vmem, o=o):
                pltpu.sync_copy(x_vmem, o.at[i0_vmem.at[0]])
                pltpu.sync_copy(x_vmem, o.at[i1_vmem.at[0]])

            def put(x_vmem, i_vmem, o=o):
                pltpu.sync_copy(x_vmem, o.at[i_vmem.at[0]])

            _sc_pipeline(put_pair, T, [rows, idx, idx], [])(x, d0_hbm, d1_hbm)
            _sc_pipeline(put, n_pad, [zero_rows, idx], [])(z_hbm, p_hbm)

    return scatter(*xs, dest[0:1], dest[1:2], jnp.zeros((SC_WINDOW, W), xs[0].dtype), pad_idx)


WEIGHT_LEADS = (3, 2, 1)
EXPERT_SLOTS = max(WEIGHT_LEADS) + 1


def _expert_kernel(blk_exp_ref, slot_ref, n_used_ref, *refs):
    x_refs = refs[:ROW_PARTS]
    w_refs = refs[ROW_PARTS:ROW_PARTS + 3]
    o_refs = refs[ROW_PARTS + 3:2 * ROW_PARTS + 3]
    w_slots = refs[2 * ROW_PARTS + 3:]
    lead = max(WEIGHT_LEADS)
    n_used = n_used_ref[0]
    i = pl.program_id(0) - lead
    D = w_slots[0].shape[1]

    for w_ref, w_s, ahead in zip(w_refs, w_slots, WEIGHT_LEADS):
        j = i + ahead
        jc = jnp.clip(j, 0, n_used - 1)
        arrived = (j >= 0) & (j < n_used) & ((j == 0) | (blk_exp_ref[jc] != blk_exp_ref[jnp.maximum(jc - 1, 0)]))

        @pl.when(arrived)
        def _(w_ref=w_ref, w_s=w_s, jc=jc):
            w_s[slot_ref[jc]] = w_ref[0, 0].astype(BF16)

    @pl.when((i >= 0) & (i < n_used))
    def _():
        slot = slot_ref[jnp.clip(i, 0, n_used - 1)]
        wg_s, wu_s, wd_s = w_slots
        pieces = []
        for part, x_ref in enumerate(x_refs):
            for cols, val in zip(_part_cols(D, part), _unpack_bf16_pair(x_ref[...])):
                pieces.append((cols, val.astype(BF16)))

        def up(w_s):
            return sum(jnp.dot(val, w_s[slot, cols, :], preferred_element_type=F32) for cols, val in pieces)

        g = up(wg_s)
        hid = (g * jax.nn.sigmoid(g) * up(wu_s)).astype(BF16)
        y = jnp.dot(hid, wd_s[slot], preferred_element_type=F32)
        for part, o_ref in enumerate(o_refs):
            o_ref[...] = _pack_part(y, part)

    @pl.when(i >= n_used)
    def _():
        for o_ref in o_refs:
            o_ref[...] = jnp.zeros_like(o_ref)


def _experts(xs, blk_exp, n_used, layer, w_gate, w_up, w_down):
    P, W = xs[0].shape
    D = 2 * W * ROW_PARTS
    FF = w_gate.shape[3]
    nblk = P // MOE_BLOCK
    lead = max(WEIGHT_LEADS)
    changes = jnp.concatenate([jnp.zeros((1,), jnp.int32), (blk_exp[1:] != blk_exp[:-1]).astype(jnp.int32)])
    slot = jnp.cumsum(changes) % EXPERT_SLOTS

    def x_map(g, be, sl, nu):
        return (jnp.clip(g - lead, 0, nu[0] - 1), 0)

    def o_map(g, be, sl, nu):
        return (jnp.maximum(g - lead, 0), 0)

    def w_map(ahead):
        return lambda g, be, sl, nu: (layer, be[jnp.clip(g - lead + ahead, 0, nu[0] - 1)], 0, 0)

    wg_spec, wu_spec, wd_spec = (pl.BlockSpec((1, 1) + shape, w_map(ahead))
                                 for shape, ahead in zip(((D, FF), (D, FF), (FF, D)), WEIGHT_LEADS))
    return pl.pallas_call(
        _expert_kernel,
        grid_spec=pltpu.PrefetchScalarGridSpec(
            num_scalar_prefetch=3,
            grid=(nblk + lead,),
            in_specs=[pl.BlockSpec((MOE_BLOCK, W), x_map)] * ROW_PARTS + [wg_spec, wu_spec, wd_spec],
            out_specs=[pl.BlockSpec((MOE_BLOCK, W), o_map)] * ROW_PARTS,
            scratch_shapes=[pltpu.VMEM((EXPERT_SLOTS, D, FF), BF16), pltpu.VMEM((EXPERT_SLOTS, D, FF), BF16),
                            pltpu.VMEM((EXPERT_SLOTS, FF, D), BF16)]),
        out_shape=[jax.ShapeDtypeStruct((P, W), jnp.uint32)] * ROW_PARTS,
        compiler_params=_params("arbitrary"),
        name="experts",
    )(blk_exp, slot, n_used, *xs, w_gate, w_up, w_down)


def _gather_pairs(ys, dest):
    W = ys[0].shape[1]
    T = dest.shape[1]
    n = len(ys)
    rows, idx = _sc_window_specs(W)
    out = jax.ShapeDtypeStruct((T, W), ys[0].dtype)

    @functools.partial(pl.kernel, out_type=(out,) * (2 * n), mesh=_sc_mesh(), scratch_types=[],
                       name="gather_pairs")
    def gather(*refs):
        y_hbm, d_hbm, o_hbm = refs[:n], refs[n:n + 2], refs[n + 2:]
        for slot, d in enumerate(d_hbm):
            for y, o in zip(y_hbm, o_hbm[slot * n:(slot + 1) * n]):
                def get(i_vmem, o_vmem, y=y):
                    pltpu.sync_copy(y.at[i_vmem.at[0]], o_vmem)

                _sc_pipeline(get, T, [idx], [rows])(d, o)

    return gather(*ys, dest[0:1], dest[1:2])


def _combine_kernel(h_ref, route_ref, *refs):
    y_refs, o_ref = refs[:-1], refs[-1]
    D = h_ref.shape[1]
    route = route_ref[...]
    gates = (route[:, ROUTE_G1:ROUTE_G1 + 1], route[:, ROUTE_G2:ROUTE_G2 + 1])
    for part in range(ROW_PARTS):
        slots = [_unpack_bf16_pair(y_refs[slot * ROW_PARTS + part][...]) for slot in range(2)]
        for half, cols in enumerate(_part_cols(D, part)):
            o_ref[:, cols] = h_ref[:, cols] + (gates[0] * slots[0][half] + gates[1] * slots[1][half])


def _combine(h, route, pairs):
    T, D = h.shape
    W = pairs[0].shape[1]
    tm = min(ROW_TILE, T)
    return pl.pallas_call(
        _combine_kernel,
        grid=(T // tm,),
        in_specs=[pl.BlockSpec((tm, D), lambda i: (i, 0)), pl.BlockSpec((tm, LANES), lambda i: (i, 0))]
                 + [pl.BlockSpec((tm, W), lambda i: (i, 0))] * len(pairs),
        out_specs=pl.BlockSpec((tm, D), lambda i: (i, 0)),
        out_shape=jax.ShapeDtypeStruct((T, D), F32),
        compiler_params=_params("parallel"),
        name="combine",
    )(h, route, *pairs)


def _moe(h1, hn, route, route_t, counts, layer, w_gate, w_up, w_down):
    T, D = h1.shape
    A = 2 * T
    nblk = -(-A // MOE_BLOCK) + N_EXPERTS
    P = nblk * MOE_BLOCK
    eid = route_t[ROUTE_E1:ROUTE_E2 + 1].astype(jnp.int32)
    rank = route_t[ROUTE_R1:ROUTE_R2 + 1].astype(jnp.int32)
    cnt = counts[0, :N_EXPERTS].astype(jnp.int32)
    padded = (cnt + MOE_BLOCK - 1) // MOE_BLOCK * MOE_BLOCK
    pends = jnp.cumsum(padded)
    pstarts = pends - padded
    experts = jnp.arange(N_EXPERTS, dtype=jnp.int32)
    start_of = jnp.sum(jnp.where(eid[:, None, :] == experts[None, :, None], pstarts[None, :, None], 0), axis=1)
    dest = start_of + rank
    blk_start = jnp.arange(nblk, dtype=jnp.int32) * MOE_BLOCK
    blk_exp = jnp.minimum(jnp.sum((pends[None, :] <= blk_start[:, None]).astype(jnp.int32), axis=1),
                          N_EXPERTS - 1)
    n_used = pends[-1:] // MOE_BLOCK

    gap_start = jnp.concatenate([pstarts + cnt, pends[-1:]])
    gap_len = jnp.concatenate([padded - cnt, P - pends[-1:]])
    gap_end = jnp.cumsum(gap_len)
    j = jnp.arange(P - A, dtype=jnp.int32)
    gap_of = jnp.sum((gap_end[None, :] <= j[:, None]).astype(jnp.int32), axis=1)
    sel = gap_of[:, None] == jnp.arange(N_EXPERTS + 1, dtype=jnp.int32)[None, :]
    pad_idx = jnp.sum(jnp.where(sel, (gap_start - (gap_end - gap_len))[None, :] + j[:, None], 0), axis=1)

    xs = _dispatch(hn, dest, pad_idx.reshape(1, P - A), P)
    ys = _experts(xs, blk_exp, n_used, layer, w_gate, w_up, w_down)
    return _combine(h1, route, _gather_pairs(ys, dest))


def _qkv_kernel(x_ref, gq_ref, gkv_ref, wq_ref, wk_ref, wvt_ref, qn_ref, kn_ref, seg_ref,
                q_ref, k_ref, vt_ref, xs_ref, *, rate, n, n_chunks):
    seg = seg_ref[...]
    width = seg.shape[0]
    res_per_chunk = q_ref.shape[1]
    chunk = pl.program_id(2)

    def head_norm(t, gain):
        cols = []
        for j in range(t.shape[1] // width):
            tj = t[:, j * width:(j + 1) * width]
            ms = jnp.dot((tj * tj).astype(BF16), seg, preferred_element_type=F32)
            cols.append(tj * lax.rsqrt(ms + EPS))
        return jnp.concatenate(cols, axis=1) * gain

    def project(x):
        y = _rms(x)
        xq = (y * gq_ref[...]).astype(BF16)
        xkv = (y * gkv_ref[...]).astype(BF16)
        q = head_norm(jnp.dot(xq, wq_ref[...], preferred_element_type=F32), qn_ref[...]).astype(q_ref.dtype)
        k = head_norm(jnp.dot(xkv, wk_ref[...], preferred_element_type=F32), kn_ref[...]).astype(k_ref.dtype)
        vt = lax.dot_general(wvt_ref[...], xkv, NT_DIMS, preferred_element_type=F32).astype(vt_ref.dtype)
        for j in range(res_per_chunk):
            q_ref[0, j] = q[j * n:(j + 1) * n]
            k_ref[0, j] = k[j * n:(j + 1) * n]
            vt_ref[0, j] = vt[:, j * n:(j + 1) * n]

    if rate == 1:
        project(x_ref[...])
        return

    @pl.when(chunk == 0)
    def _():
        for j in range(xs_ref.shape[0]):
            xs_ref[j] = x_ref[:, j * LANES:(j + 1) * LANES]

    for ch in range(n_chunks):
        @pl.when(chunk == ch)
        def _(ch=ch):
            residues = range(ch * res_per_chunk, (ch + 1) * res_per_chunk)
            project(jnp.concatenate(
                [jnp.concatenate([xs_ref[j, pl.ds(c, n, stride=rate), :] for j in range(xs_ref.shape[0])], axis=1)
                 for c in residues], axis=0))


def _qkv(h, B, S, rate, gq, gkv, wq, wk, wv, qn, kn):
    D = h.shape[1]
    L = S // rate
    hd = DIL_HEAD_DIM
    n = max(SUB_BLOCK, TOKEN_TILE // rate)
    tm = n * rate
    res_per_chunk = max(1, TOKEN_TILE // n)
    n_chunks = rate // res_per_chunk
    width = 2 * LANES
    ii = jnp.arange(width)
    seg = jnp.where((ii[:, None] // hd) == (ii[None, :] // hd), 1.0 / hd, 0.0).astype(BF16)
    row = lambda g: jnp.tile(g, D // hd).reshape(1, D)
    const = lambda shape: _resident(shape, lambda b, i, c: (0,) * len(shape))
    qk_spec = pl.BlockSpec((1, res_per_chunk, n, D), lambda b, i, c: (b, c, i, 0))
    qk_shape = jax.ShapeDtypeStruct((B, rate, L, D), BF16)
    return pl.pallas_call(
        functools.partial(_qkv_kernel, rate=rate, n=n, n_chunks=n_chunks),
        grid=(B, S // tm, n_chunks),
        in_specs=[pl.BlockSpec((tm, D), lambda b, i, c: (b * (S // tm) + i, 0)),
                  const((1, D)), const((1, D)), const((D, D)), const((D, D)), const((D, D)),
                  const((1, D)), const((1, D)), const((width, width))],
        out_specs=[qk_spec, qk_spec,
                   pl.BlockSpec((1, res_per_chunk, D, n), lambda b, i, c: (b, c, 0, i))],
        out_shape=[qk_shape, qk_shape, jax.ShapeDtypeStruct((B, rate, D, L), BF16)],
        scratch_shapes=[pltpu.VMEM((D // LANES, tm if rate > 1 else 8, LANES), F32)],
        compiler_params=_params("parallel", "parallel", "arbitrary"),
        name="qkv_rate%d" % rate,
    )(h, gq.reshape(1, D), gkv.reshape(1, D), wq, wk, wv.T, row(qn) * (hd ** -0.5 * LOG2E), row(kn), seg)


def _attn_kernel(*refs, rate, with_prev):
    if rate == 1:
        q_ref, kc_ref, kp_ref, vc_ref, vp_ref, edge_ref, bias_ref, o_ref, lse_ref, os_ref = refs
    elif with_prev:
        q_ref, kc_ref, kp_ref, vc_ref, vp_ref, edge_ref, o_ref, lse_ref, os_ref = refs
    else:
        q_ref, kc_ref, vc_ref, edge_ref, o_ref, lse_ref, os_ref = refs
    Bk = SUB_BLOCK
    hd = DIL_HEAD_DIM
    n_pairs = DIL_HEADS // 2
    lane = lax.broadcasted_iota(jnp.int32, (Bk, LANES), 1)

    def block(load_q, load_k, load_vt, load_bias, store):
        out_t, lse_t = [], []
        for p in range(n_pairs):
            qp = load_q(p)
            zero = jnp.zeros_like(qp)
            q2 = jnp.concatenate([jnp.where(lane < hd, qp, zero), jnp.where(lane >= hd, qp, zero)], axis=0)
            s = lax.dot_general(load_k(p), q2, NT_DIMS, preferred_element_type=F32) + load_bias(p)
            m = jnp.max(s, axis=0, keepdims=True)
            pr = jnp.exp2(s - m)
            l = jnp.sum(pr, axis=0, keepdims=True)
            pb = pr.astype(BF16)
            vt = load_vt(p)
            out_t.append(jnp.dot(vt[:hd], pb[:, :Bk], preferred_element_type=F32) / l[:, :Bk])
            out_t.append(jnp.dot(vt[hd:], pb[:, Bk:], preferred_element_type=F32) / l[:, Bk:])
            lse = (m + jnp.log2(l)) * LN2
            lse_t += [lse[:, :Bk], lse[:, Bk:]]
        lse_t.append(jnp.zeros((LANES - DIL_HEADS, Bk), F32))
        store(jnp.concatenate(out_t, axis=0).T, jnp.concatenate(lse_t, axis=0).T)

    def pair_cols(p):
        return slice(p * LANES, (p + 1) * LANES)

    edge_bias = lambda p: edge_ref[0, p]
    if rate == 1:
        for j in range(q_ref.shape[2] // Bk):
            rows = slice(j * Bk, (j + 1) * Bk)
            if j == 0:
                load_k = lambda p: jnp.concatenate([kp_ref[0, 0, :, pair_cols(p)],
                                                    kc_ref[0, 0, :Bk, pair_cols(p)]], axis=0)
                load_vt = lambda p: jnp.concatenate([vp_ref[0, 0, pair_cols(p), :],
                                                     vc_ref[0, 0, pair_cols(p), :Bk]], axis=1)
                load_bias = edge_bias
            else:
                krows = slice((j - 1) * Bk, (j + 1) * Bk)
                load_k = lambda p, krows=krows: kc_ref[0, 0, krows, pair_cols(p)]
                load_vt = lambda p, krows=krows: vc_ref[0, 0, pair_cols(p), krows]
                load_bias = lambda p: bias_ref[p]

            def store(o, lse, rows=rows):
                o_ref[rows, :] = o.astype(o_ref.dtype)
                lse_ref[rows, :] = lse

            block(lambda p, rows=rows: q_ref[0, 0, rows, pair_cols(p)], load_k, load_vt, load_bias, store)
    else:
        for c in range(q_ref.shape[1]):
            if with_prev:
                load_k = lambda p, c=c: jnp.concatenate([kp_ref[0, c, :, pair_cols(p)],
                                                         kc_ref[0, c, :, pair_cols(p)]], axis=0)
                load_vt = lambda p, c=c: jnp.concatenate([vp_ref[0, c, pair_cols(p), :],
                                                          vc_ref[0, c, pair_cols(p), :]], axis=1)
            else:
                load_k = lambda p, c=c: kc_ref[0, c, :, pair_cols(p)]
                load_vt = lambda p, c=c: vc_ref[0, c, pair_cols(p), :]

            def store(o, lse, c=c):
                for j in range(os_ref.shape[0]):
                    os_ref[j, pl.ds(c, Bk, stride=rate), :] = o[:, pair_cols(j)]
                lse_ref[pl.ds(c, Bk, stride=rate), :] = lse

            block(lambda p, c=c: q_ref[0, c, :, pair_cols(p)], load_k, load_vt, edge_bias, store)
        for j in range(os_ref.shape[0]):
            o_ref[:, pair_cols(j)] = os_ref[j].astype(o_ref.dtype)


def _t5_bucket(n):
    max_exact = NUM_BUCKETS // 2
    nf = jnp.maximum(n, max_exact).astype(F32)
    large = max_exact + (jnp.log(nf / max_exact) / math.log(MAX_DISTANCE / max_exact)
                         * (NUM_BUCKETS - max_exact)).astype(jnp.int32)
    large = jnp.minimum(large, NUM_BUCKETS - 1)
    return jnp.where(n < max_exact, n, large)


def _group_attention(q, k, vt, bias_table, rate, n_steps):
    B, _, L, D = q.shape
    S = L * rate
    Bk = SUB_BLOCK
    with_prev = L > Bk
    nk = 2 * Bk if with_prev else Bk
    n = max(Bk, TOKEN_TILE // rate)
    tm = n * rate
    nt = S // tm
    n_pairs = DIL_HEADS // 2
    ql = jnp.arange(Bk, dtype=jnp.int32)[:, None]
    kl = jnp.arange(2 * Bk, dtype=jnp.int32)[None, :]
    steps = ql + Bk - kl
    bucket = _t5_bucket(jnp.maximum(steps, 0) * rate)
    buckets = jnp.arange(NUM_BUCKETS, dtype=jnp.int32)
    bias = jnp.sum(jnp.where(bucket[None, :, :, None] == buckets[:, None, None, None],
                             bias_table.astype(F32)[:, None, None, :], 0.0), axis=0)
    band = ((steps >= 0) & (steps <= n_steps))[:, :, None]
    first = (kl < Bk)[:, :, None]

    def layout(t):
        t = t[:, 2 * Bk - nk:, :].transpose(1, 2, 0)
        return t.reshape(nk, n_pairs, 2 * Bk).transpose(1, 0, 2)

    bias_in = layout(jnp.where(band, bias * LOG2E, NEG_INF))
    bias_first = layout(jnp.where(band & ~first, bias * LOG2E, NEG_INF))
    edge = jnp.stack([bias_first, bias_in])

    cur_qk = pl.BlockSpec((1, rate, n, D), lambda b, i: (b, 0, i, 0))
    cur_vt = pl.BlockSpec((1, rate, D, n), lambda b, i: (b, 0, 0, i))
    per_n = n // Bk
    prev_qk = pl.BlockSpec((1, rate, Bk, D), lambda b, i: (b, 0, jnp.maximum(i * per_n - 1, 0), 0))
    prev_vt = pl.BlockSpec((1, rate, D, Bk), lambda b, i: (b, 0, 0, jnp.maximum(i * per_n - 1, 0)))
    edge_spec = pl.BlockSpec((1, n_pairs, nk, 2 * Bk), lambda b, i: (jnp.minimum(i, 1), 0, 0, 0))
    if with_prev:
        in_specs = [cur_qk, cur_qk, prev_qk, cur_vt, prev_vt, edge_spec]
        args = (q, k, k, vt, vt, edge)
    else:
        in_specs = [cur_qk, cur_qk, cur_vt, edge_spec]
        args = (q, k, vt, edge)
    if rate == 1:
        in_specs.append(_resident((n_pairs, nk, 2 * Bk), lambda b, i: (0, 0, 0)))
        args += (bias_in,)
    return pl.pallas_call(
        functools.partial(_attn_kernel, rate=rate, with_prev=with_prev),
        grid=(B, nt),
        in_specs=in_specs,
        out_specs=[pl.BlockSpec((tm, D), lambda b, i: (b * nt + i, 0)),
                   pl.BlockSpec((tm, LANES), lambda b, i: (b * nt + i, 0))],
        out_shape=[jax.ShapeDtypeStruct((B * S, D), BF16),
                   jax.ShapeDtypeStruct((B * S, LANES), F32)],
        scratch_shapes=[pltpu.VMEM((D // LANES, tm if rate > 1 else 8, LANES), F32)],
        compiler_params=_params("parallel", "parallel"),
        name="attn_rate%d" % rate,
    )(*args)


def kernel(x, ret_w_in, ret_w_out, kv_norm, w_kv, k_norm, dil_wq, q_norm, dil_wo, rel_bias,
           mixer_norm, ffn_norm, router_grp, router_grp_b, router_exp, router_exp_b,
           exp_gate, exp_up, exp_down):
    B, S, D = x.shape
    h = x.reshape(B * S, D)

    def moe_layer(layer, a, w_out, h):
        h1, *hn, route, route_t, counts = _post(
            a, w_out.astype(BF16), h, ffn_norm[layer], router_grp[layer], router_grp_b[layer],
            router_exp[layer], router_exp_b[layer])
        return _moe(h1, hn, route, route_t, counts, layer, exp_gate, exp_up, exp_down)

    y = _retention(h, mixer_norm[0], ret_w_in[0].astype(BF16), B, S)
    h = moe_layer(0, y, ret_w_out[0], h)

    G = len(DIL_RATES)
    gd = DIL_HEADS * DIL_HEAD_DIM
    outs, lses = [], []
    for g in range(G):
        cq = slice(g * gd, (g + 1) * gd)
        cv = slice(G * gd + g * gd, G * gd + (g + 1) * gd)
        q, k, vt = _qkv(h, B, S, DIL_RATES[g], mixer_norm[1], kv_norm,
                        dil_wq[0][:, cq].astype(BF16), w_kv[:, cq].astype(BF16), w_kv[:, cv].astype(BF16),
                        q_norm[0][g], k_norm[g])
        o, lse = _group_attention(q, k, vt, rel_bias[:, g * DIL_HEADS:(g + 1) * DIL_HEADS],
                                  DIL_RATES[g], DIL_WINDOWS[g] // DIL_RATES[g])
        outs.append(o)
        lses.append(lse)
    h = moe_layer(1, (tuple(outs), tuple(lses)), dil_wo[0], h)
    return h.reshape(B, S, D)
```

```python
import functools
import math

import jax
import jax.numpy as jnp
from jax import lax
from jax.experimental import pallas as pl
from jax.experimental.pallas import tpu as pltpu
from jax.experimental.pallas import tpu_sc as plsc

F32 = jnp.float32
BF16 = jnp.bfloat16

EPS = 1e-6
NEG_INF = -1e30

RET_HEADS = 4
RET_CHUNK = 256
ROPE_BASE = 10000.0

DIL_WINDOWS = (128, 512, 2048)
DIL_RATES = (1, 4, 16)
DIL_HEADS = 16
DIL_HEAD_DIM = 64
SUB_BLOCK = 128
NUM_BUCKETS = 32
MAX_DISTANCE = 2048

MOE_GROUPS = 4
EXPERTS_PER_GROUP = 8
N_EXPERTS = MOE_GROUPS * EXPERTS_PER_GROUP
MOE_BLOCK = 512

LANES = 128
ROW_TILE = 1024
TOKEN_TILE = 512
VMEM_LIMIT = 56 * 1024 * 1024

NT_DIMS = (((1,), (1,)), ((), ()))
LOG2E = math.log2(math.e)
LN2 = math.log(2.0)


def _params(*sem):
    return pltpu.CompilerParams(dimension_semantics=sem, vmem_limit_bytes=VMEM_LIMIT)


def _resident(shape, index_map):
    return pl.BlockSpec(shape, index_map, pipeline_mode=pl.Buffered(1))


def _rms(x):
    return x * lax.rsqrt(jnp.mean(x * x, axis=-1, keepdims=True) + EPS)


def _retention_tile(x_ref, g_ref, w_ref, cos_ref, sin_ref, din_ref, xi_ref, zeta_ref, cd_ref, state_ref,
                    first_tile):
    C = RET_CHUNK
    H, dk, dv = state_ref.shape
    half = dk // 2
    ts = x_ref.shape[0]
    k_scale = dk ** -0.5

    @pl.when(first_tile)
    def _():
        state_ref[...] = jnp.zeros_like(state_ref)

    xn = (_rms(x_ref[...]) * g_ref[...]).astype(BF16)
    cos, sin = cos_ref[...], sin_ref[...]

    def proj(c0, width):
        return jnp.dot(xn, w_ref[:, c0:c0 + width], preferred_element_type=F32)

    def rot(t):
        t1, t2 = t[:, :half], t[:, half:]
        return jnp.concatenate([t1 * cos - t2 * sin, t1 * sin + t2 * cos], axis=1)

    heads = []
    for hh in range(H):
        q = rot(proj(hh * dk, dk)).astype(BF16)
        k = rot(proj(H * dk + hh * dk, dk)) * k_scale
        kb = k.astype(BF16)
        v = proj(2 * H * dk + hh * dv, dv).astype(BF16)
        gate = proj(2 * H * dk + H * dv + hh * dv, dv)
        gate = gate * jax.nn.sigmoid(gate)
        chunks = []
        for c in range(ts // C):
            rows = slice(c * C, (c + 1) * C)
            s = lax.dot_general(q[rows], kb[rows], NT_DIMS, preferred_element_type=F32) * din_ref[hh]
            inner = jnp.dot(s.astype(BF16), v[rows], preferred_element_type=F32)
            state = state_ref[hh]
            cross = jnp.dot(q[rows], state.astype(BF16), preferred_element_type=F32) * xi_ref[hh]
            kz_t = (k[rows] * zeta_ref[hh]).T.astype(BF16)
            state_ref[hh] = state * cd_ref[hh] + jnp.dot(kz_t, v[rows], preferred_element_type=F32)
            chunks.append((gate[rows] * _rms(inner + cross)).astype(BF16))
        heads.append(jnp.concatenate(chunks, axis=0))
    return jnp.concatenate(heads, axis=1)


def _retention_operands(gain, w, S, ts):
    H, C = RET_HEADS, RET_CHUNK
    D = w.shape[0]
    dk = D // H
    half = dk // 2
    nt = S // ts
    pos = jnp.arange(S, dtype=F32)
    inv = 1.0 / (ROPE_BASE ** jnp.linspace(0.0, 1.0, half, dtype=F32))
    ang = pos[:, None] * inv[None, :]
    log_g = jnp.log(1.0 - 2.0 ** (-5.0 - jnp.arange(H, dtype=F32)))
    idx = jnp.arange(C, dtype=F32)
    diff = idx[:, None] - idx[None, :]
    d_in = jnp.where(diff >= 0, jnp.exp(log_g[:, None, None] * jnp.maximum(diff, 0.0)), 0.0)
    xi = jnp.exp(log_g[:, None] * (idx + 1.0))[:, :, None]
    zeta = jnp.exp(log_g[:, None] * (C - 1.0 - idx))[:, :, None]
    chunk_decay = jnp.exp(log_g * C)[:, None, None]
    table = pl.BlockSpec((ts, half), lambda i: (i % nt, 0))
    per_head = lambda shape: _resident((H,) + shape, lambda i: (0, 0, 0))
    args = (gain.reshape(1, D), w, jnp.cos(ang), jnp.sin(ang), d_in, xi, zeta, chunk_decay)
    specs = [_resident((1, D), lambda i: (0, 0)), _resident(w.shape, lambda i: (0, 0)), table, table,
             per_head((C, C)), per_head((C, 1)), per_head((C, 1)), per_head((1, 1))]
    state = pltpu.VMEM((H, dk, 2 * dk), F32)
    return args, specs, state


ROUTE_E1, ROUTE_E2, ROUTE_G1, ROUTE_G2, ROUTE_R1, ROUTE_R2 = range(6)
ROUTE_ROWS = 8
ROUTE_PIECES = 2
ROUTER_EXP_LANE0 = MOE_GROUPS
HI16 = 0xFFFF0000


def _pack_bf16_pair(a, b):
    ua = lax.bitcast_convert_type(a.astype(BF16).astype(F32), jnp.uint32)
    ub = lax.bitcast_convert_type(b.astype(BF16).astype(F32), jnp.uint32)
    return ua | (ub >> 16)


def _unpack_bf16_pair(w):
    a = lax.bitcast_convert_type(w & jnp.uint32(HI16), F32)
    b = lax.bitcast_convert_type(w << 16, F32)
    return a, b


ROW_PARTS = 2


def _part_cols(D, part):
    w = D // 2 // ROW_PARTS
    return slice(part * w, (part + 1) * w), slice(D // 2 + part * w, D // 2 + (part + 1) * w)


def _pack_part(x, part):
    hi, lo = _part_cols(x.shape[1], part)
    return _pack_bf16_pair(x[:, hi], x[:, lo])


def _merge_groups(o_refs, l_refs, ex_ref):
    ex = ex_ref[...]

    def expand(w):
        hi = w.astype(BF16)
        lo = (w - hi.astype(F32)).astype(BF16)
        return jnp.dot(hi, ex, preferred_element_type=F32) + jnp.dot(lo, ex, preferred_element_type=F32)

    lses = [l_ref[...] for l_ref in l_refs]
    top = functools.reduce(jnp.maximum, lses)
    ws = [jnp.exp(l - top) for l in lses]
    den = sum(ws)
    return sum(expand(w / den) * o_ref[...].astype(F32) for w, o_ref in zip(ws, o_refs)).astype(BF16)


def _post_kernel(*refs, n_groups, seq_tiles):
    n_a = 2 * n_groups + 1 if n_groups else 8
    a_refs = refs[:n_a]
    (w_ref, h_ref, g_ref, wr_ref, br_ref, tri_ref,
     h1_ref, hn0_ref, hn1_ref, route_ref, route_t_ref, cnt_ref, carry_ref) = refs[n_a:n_a + 13]
    hn_refs = (hn0_ref, hn1_ref)
    if n_groups:
        a = _merge_groups(a_refs[:n_groups], a_refs[n_groups:2 * n_groups], a_refs[-1])
    else:
        a = _retention_tile(h_ref, *a_refs, refs[-1], pl.program_id(0) % seq_tiles == 0)
    @pl.when(pl.program_id(0) == 0)
    def _():
        carry_ref[...] = jnp.zeros_like(carry_ref)

    half = tri_ref.shape[0]
    n_pieces = h_ref.shape[0] // half
    D = h_ref.shape[1]
    lane = lax.broadcasted_iota(jnp.int32, (half, LANES), 1).astype(F32)
    ninf = -jnp.inf

    def first_argmax(vals):
        top = jnp.max(vals, axis=1, keepdims=True)
        where = jnp.min(jnp.where(vals == top, lane, float(LANES)), axis=1, keepdims=True)
        return top, where

    h1_ref[...] = h_ref[...] + jnp.dot(a, w_ref[...], preferred_element_type=F32)

    def route_half(rows):
        hn = _rms(h1_ref[rows, :]) * g_ref[...]
        for part, ref in enumerate(hn_refs):
            ref[rows, :] = _pack_part(hn, part)
        hi = hn.astype(BF16)
        lo = (hn - hi.astype(F32)).astype(BF16)
        logits = jnp.dot(jnp.concatenate([hi, hi, lo], axis=1), wr_ref[...],
                         preferred_element_type=F32) + br_ref[...]

        is_grp = lane < MOE_GROUPS
        lg = jnp.where(is_grp, logits, ninf)
        mg, grp = first_argmax(lg)
        p_grp = 1.0 / jnp.sum(jnp.where(is_grp, jnp.exp(lg - mg), 0.0), axis=1, keepdims=True)

        e_lane = lane - ROUTER_EXP_LANE0
        in_grp = (e_lane < N_EXPERTS) & (jnp.floor(e_lane * (1.0 / EXPERTS_PER_GROUP)) == grp)
        le = jnp.where(in_grp, logits, ninf)
        v1, i1 = first_argmax(le)
        le2 = jnp.where(lane == i1, ninf, le)
        v2, i2 = first_argmax(le2)
        e = jnp.exp(v2 - v1)
        hit1 = lane == (i1 - ROUTER_EXP_LANE0)
        hit2 = lane == (i2 - ROUTER_EXP_LANE0)
        onehot = jnp.where(hit1 | hit2, 1.0, 0.0)
        earlier = jnp.dot(tri_ref[...], onehot.astype(BF16), preferred_element_type=F32)
        return dict(e1=i1 - ROUTER_EXP_LANE0, e2=i2 - ROUTER_EXP_LANE0, g1=p_grp / (1.0 + e),
                    g2=p_grp * e / (1.0 + e), hit1=hit1, hit2=hit2, earlier=earlier,
                    count=jnp.sum(onehot, axis=0, keepdims=True))

    halves = [route_half(slice(j * half, (j + 1) * half)) for j in range(n_pieces)]
    carry = carry_ref[...]
    for j, r in enumerate(halves):
        before = carry + r["earlier"]
        r1 = jnp.sum(jnp.where(r["hit1"], before, 0.0), axis=1, keepdims=True)
        r2 = jnp.sum(jnp.where(r["hit2"], before, 0.0), axis=1, keepdims=True)
        carry = carry + r["count"]
        route = jnp.zeros((half, LANES), F32)
        for slot, val in ((ROUTE_E1, r["e1"]), (ROUTE_E2, r["e2"]), (ROUTE_G1, r["g1"]),
                          (ROUTE_G2, r["g2"]), (ROUTE_R1, r1), (ROUTE_R2, r2)):
            route = jnp.where(lane == slot, val, route)
        route_ref[j * half:(j + 1) * half, :] = route
        route_t_ref[:, j * half:(j + 1) * half] = route.T[:ROUTE_ROWS]
    carry_ref[...] = carry
    cnt_ref[...] = carry


def _post(mixer, w, h, gain, w_grp, b_grp, w_exp, b_exp):
    T, D = h.shape
    K = w.shape[0]
    row_block = lambda width: pl.BlockSpec((tm, width), lambda i: (i, 0))
    scratch = [pltpu.VMEM((1, LANES), F32)]
    if len(mixer) == 2:
        outs, lses = mixer
        n_groups, seq_tiles = len(outs), 1
        tm = min(TOKEN_TILE, T)
        ex = jnp.where(jnp.arange(LANES)[:, None] == (jnp.arange(K)[None, :] // DIL_HEAD_DIM), 1.0, 0.0).astype(BF16)
        a_args = (*outs, *lses, ex)
        a_specs = [row_block(K)] * n_groups + [row_block(LANES)] * n_groups + [_resident((LANES, K), lambda i: (0, 0))]
    else:
        mix_gain, w_in, S = mixer
        tm = min(TOKEN_TILE, S)
        n_groups, seq_tiles = 0, S // tm
        a_args, a_specs, state = _retention_operands(mix_gain, w_in, S, tm)
        scratch.append(state)
    n_r = MOE_GROUPS + N_EXPERTS
    wr = jnp.zeros((D, LANES), F32).at[:, :n_r].set(jnp.concatenate([w_grp, w_exp], axis=1))
    wr_hi = wr.astype(BF16)
    wr_lo = (wr - wr_hi.astype(F32)).astype(BF16)
    wr3 = jnp.concatenate([wr_hi, wr_lo, wr_hi], axis=0)
    br = jnp.zeros((1, LANES), F32).at[0, :n_r].set(jnp.concatenate([b_grp, b_exp]))
    piece = tm // ROUTE_PIECES
    tri = jnp.tril(jnp.ones((piece, piece), BF16), k=-1)
    return pl.pallas_call(
        functools.partial(_post_kernel, n_groups=n_groups, seq_tiles=seq_tiles),
        grid=(T // tm,),
        in_specs=a_specs + [
                  _resident((K, D), lambda i: (0, 0)),
                  pl.BlockSpec((tm, D), lambda i: (i, 0)),
                  _resident((1, D), lambda i: (0, 0)),
                  _resident((3 * D, LANES), lambda i: (0, 0)),
                  _resident((1, LANES), lambda i: (0, 0)),
                  _resident((piece, piece), lambda i: (0, 0))],
        out_specs=[pl.BlockSpec((tm, D), lambda i: (i, 0))]
                  + [pl.BlockSpec((tm, D // 2 // ROW_PARTS), lambda i: (i, 0))] * ROW_PARTS
                  + [pl.BlockSpec((tm, LANES), lambda i: (i, 0)),
                   pl.BlockSpec((ROUTE_ROWS, tm), lambda i: (0, i)),
                   pl.BlockSpec((1, LANES), lambda i: (0, 0))],
        out_shape=[jax.ShapeDtypeStruct((T, D), F32)]
                  + [jax.ShapeDtypeStruct((T, D // 2 // ROW_PARTS), jnp.uint32)] * ROW_PARTS
                  + [jax.ShapeDtypeStruct((T, LANES), F32),
                   jax.ShapeDtypeStruct((ROUTE_ROWS, T), F32),
                   jax.ShapeDtypeStruct((1, LANES), F32)],
        scratch_shapes=scratch,
        compiler_params=_params("arbitrary"),
        name="post_mixer",
    )(*a_args, w, h, gain.reshape(1, D), wr3, br, tri)


SC_WINDOW = 128


def _sc_mesh():
    return plsc.VectorSubcoreMesh(core_axis_name="core", subcore_axis_name="subcore")


def _sc_window_specs(W):
    rows = pl.BlockSpec((SC_WINDOW, W), lambda i: (i, 0))
    idx = pl.BlockSpec((1, SC_WINDOW), lambda i: (0, i))
    return rows, idx


def _sc_pipeline(body, n_rows, in_specs, out_specs):
    return pltpu.emit_pipeline(body, grid=(n_rows // SC_WINDOW,), in_specs=in_specs, out_specs=out_specs,
                               core_axis_name=("core", "subcore"), dimension_semantics=(pltpu.PARALLEL,))


def _dispatch(xs, dest, pad_idx, P):
    T, W = xs[0].shape
    n_pad = pad_idx.shape[1]
    n = len(xs)
    rows, idx = _sc_window_specs(W)
    zero_rows = pl.BlockSpec((SC_WINDOW, W), lambda i: (0, 0))
    out = jax.ShapeDtypeStruct((P, W), xs[0].dtype)

    @functools.partial(pl.kernel, out_type=(out,) * n, mesh=_sc_mesh(), scratch_types=[], name="dispatch")
    def scatter(*refs):
        x_hbm, (d0_hbm, d1_hbm, z_hbm, p_hbm), o_hbm = refs[:n], refs[n:n + 4], refs[n + 4:]
        for x, o in zip(x_hbm, o_hbm):
            def put_pair(x_vmem, i0_vmem, i1_vmem, o=o):
                pltpu.sync_copy(x_vmem, o.at[i0_vmem.at[0]])
                pltpu.sync_copy(x_vmem, o.at[i1_vmem.at[0]])

            def put(x_vmem, i_vmem, o=o):
                pltpu.sync_copy(x_vmem, o.at[i_vmem.at[0]])

            _sc_pipeline(put_pair, T, [rows, idx, idx], [])(x, d0_hbm, d1_hbm)
            _sc_pipeline(put, n_pad, [zero_rows, idx], [])(z_hbm, p_hbm)

    return scatter(*xs, dest[0:1], dest[1:2], jnp.zeros((SC_WINDOW, W), xs[0].dtype), pad_idx)


WEIGHT_LEADS = (3, 2, 1)
EXPERT_SLOTS = max(WEIGHT_LEADS) + 1


def _expert_kernel(blk_exp_ref, slot_ref, n_used_ref, *refs):
    x_refs = refs[:ROW_PARTS]
    w_refs = refs[ROW_PARTS:ROW_PARTS + 3]
    o_refs = refs[ROW_PARTS + 3:2 * ROW_PARTS + 3]
    w_slots = refs[2 * ROW_PARTS + 3:]
    lead = max(WEIGHT_LEADS)
    n_used = n_used_ref[0]
    i = pl.program_id(0) - lead
    D = w_slots[0].shape[1]

    for w_ref, w_s, ahead in zip(w_refs, w_slots, WEIGHT_LEADS):
        j = i + ahead
        jc = jnp.clip(j, 0, n_used - 1)
        arrived = (j >= 0) & (j < n_used) & ((j == 0) | (blk_exp_ref[jc] != blk_exp_ref[jnp.maximum(jc - 1, 0)]))

        @pl.when(arrived)
        def _(w_ref=w_ref, w_s=w_s, jc=jc):
            w_s[slot_ref[jc]] = w_ref[0, 0].astype(BF16)

    @pl.when((i >= 0) & (i < n_used))
    def _():
        slot = slot_ref[jnp.clip(i, 0, n_used - 1)]
        wg_s, wu_s, wd_s = w_slots
        pieces = []
        for part, x_ref in enumerate(x_refs):
            for cols, val in zip(_part_cols(D, part), _unpack_bf16_pair(x_ref[...])):
                pieces.append((cols, val.astype(BF16)))

        def up(w_s):
            return sum(jnp.dot(val, w_s[slot, cols, :], preferred_element_type=F32) for cols, val in pieces)

        g = up(wg_s)
        hid = (g * jax.nn.sigmoid(g) * up(wu_s)).astype(BF16)
        y = jnp.dot(hid, wd_s[slot], preferred_element_type=F32)
        for part, o_ref in enumerate(o_refs):
            o_ref[...] = _pack_part(y, part)

    @pl.when(i >= n_used)
    def _():
        for o_ref in o_refs:
            o_ref[...] = jnp.zeros_like(o_ref)


def _experts(xs, blk_exp, n_used, layer, w_gate, w_up, w_down):
    P, W = xs[0].shape
    D = 2 * W * ROW_PARTS
    FF = w_gate.shape[3]
    nblk = P // MOE_BLOCK
    lead = max(WEIGHT_LEADS)
    changes = jnp.concatenate([jnp.zeros((1,), jnp.int32), (blk_exp[1:] != blk_exp[:-1]).astype(jnp.int32)])
    slot = jnp.cumsum(changes) % EXPERT_SLOTS

    def x_map(g, be, sl, nu):
        return (jnp.clip(g - lead, 0, nu[0] - 1), 0)

    def o_map(g, be, sl, nu):
        return (jnp.maximum(g - lead, 0), 0)

    def w_map(ahead):
        return lambda g, be, sl, nu: (layer, be[jnp.clip(g - lead + ahead, 0, nu[0] - 1)], 0, 0)

    wg_spec, wu_spec, wd_spec = (pl.BlockSpec((1, 1) + shape, w_map(ahead))
                                 for shape, ahead in zip(((D, FF), (D, FF), (FF, D)), WEIGHT_LEADS))
    return pl.pallas_call(
        _expert_kernel,
        grid_spec=pltpu.PrefetchScalarGridSpec(
            num_scalar_prefetch=3,
            grid=(nblk + lead,),
            in_specs=[pl.BlockSpec((MOE_BLOCK, W), x_map)] * ROW_PARTS + [wg_spec, wu_spec, wd_spec],
            out_specs=[pl.BlockSpec((MOE_BLOCK, W), o_map)] * ROW_PARTS,
            scratch_shapes=[pltpu.VMEM((EXPERT_SLOTS, D, FF), BF16), pltpu.VMEM((EXPERT_SLOTS, D, FF), BF16),
                            pltpu.VMEM((EXPERT_SLOTS, FF, D), BF16)]),
        out_shape=[jax.ShapeDtypeStruct((P, W), jnp.uint32)] * ROW_PARTS,
        compiler_params=_params("arbitrary"),
        name="experts",
    )(blk_exp, slot, n_used, *xs, w_gate, w_up, w_down)


def _gather_pairs(ys, dest):
    W = ys[0].shape[1]
    T = dest.shape[1]
    n = len(ys)
    rows, idx = _sc_window_specs(W)
    out = jax.ShapeDtypeStruct((T, W), ys[0].dtype)

    @functools.partial(pl.kernel, out_type=(out,) * (2 * n), mesh=_sc_mesh(), scratch_types=[],
                       name="gather_pairs")
    def gather(*refs):
        y_hbm, d_hbm, o_hbm = refs[:n], refs[n:n + 2], refs[n + 2:]
        for slot, d in enumerate(d_hbm):
            for y, o in zip(y_hbm, o_hbm[slot * n:(slot + 1) * n]):
                def get(i_vmem, o_vmem, y=y):
                    pltpu.sync_copy(y.at[i_vmem.at[0]], o_vmem)

                _sc_pipeline(get, T, [idx], [rows])(d, o)

    return gather(*ys, dest[0:1], dest[1:2])


def _combine_kernel(h_ref, route_ref, *refs):
    y_refs, o_ref = refs[:-1], refs[-1]
    D = h_ref.shape[1]
    route = route_ref[...]
    gates = (route[:, ROUTE_G1:ROUTE_G1 + 1], route[:, ROUTE_G2:ROUTE_G2 + 1])
    for part in range(ROW_PARTS):
        slots = [_unpack_bf16_pair(y_refs[slot * ROW_PARTS + part][...]) for slot in range(2)]
        for half, cols in enumerate(_part_cols(D, part)):
            o_ref[:, cols] = h_ref[:, cols] + (gates[0] * slots[0][half] + gates[1] * slots[1][half])


def _combine(h, route, pairs):
    T, D = h.shape
    W = pairs[0].shape[1]
    tm = min(ROW_TILE, T)
    return pl.pallas_call(
        _combine_kernel,
        grid=(T // tm,),
        in_specs=[pl.BlockSpec((tm, D), lambda i: (i, 0)), pl.BlockSpec((tm, LANES), lambda i: (i, 0))]
                 + [pl.BlockSpec((tm, W), lambda i: (i, 0))] * len(pairs),
        out_specs=pl.BlockSpec((tm, D), lambda i: (i, 0)),
        out_shape=jax.ShapeDtypeStruct((T, D), F32),
        compiler_params=_params("parallel"),
        name="combine",
    )(h, route, *pairs)


def _moe(h1, hn, route, route_t, counts, layer, w_gate, w_up, w_down):
    T, D = h1.shape
    A = 2 * T
    nblk = -(-A // MOE_BLOCK) + N_EXPERTS
    P = nblk * MOE_BLOCK
    eid = route_t[ROUTE_E1:ROUTE_E2 + 1].astype(jnp.int32)
    rank = route_t[ROUTE_R1:ROUTE_R2 + 1].astype(jnp.int32)
    cnt = counts[0, :N_EXPERTS].astype(jnp.int32)
    padded = (cnt + MOE_BLOCK - 1) // MOE_BLOCK * MOE_BLOCK
    pends = jnp.cumsum(padded)
    pstarts = pends - padded
    experts = jnp.arange(N_EXPERTS, dtype=jnp.int32)
    start_of = jnp.sum(jnp.where(eid[:, None, :] == experts[None, :, None], pstarts[None, :, None], 0), axis=1)
    dest = start_of + rank
    blk_start = jnp.arange(nblk, dtype=jnp.int32) * MOE_BLOCK
    blk_exp = jnp.minimum(jnp.sum((pends[None, :] <= blk_start[:, None]).astype(jnp.int32), axis=1),
                          N_EXPERTS - 1)
    n_used = pends[-1:] // MOE_BLOCK

    gap_start = jnp.concatenate([pstarts + cnt, pends[-1:]])
    gap_len = jnp.concatenate([padded - cnt, P - pends[-1:]])
    gap_end = jnp.cumsum(gap_len)
    j = jnp.arange(P - A, dtype=jnp.int32)
    gap_of = jnp.sum((gap_end[None, :] <= j[:, None]).astype(jnp.int32), axis=1)
    sel = gap_of[:, None] == jnp.arange(N_EXPERTS + 1, dtype=jnp.int32)[None, :]
    pad_idx = jnp.sum(jnp.where(sel, (gap_start - (gap_end - gap_len))[None, :] + j[:, None], 0), axis=1)

    xs = _dispatch(hn, dest, pad_idx.reshape(1, P - A), P)
    ys = _experts(xs, blk_exp, n_used, layer, w_gate, w_up, w_down)
    return _combine(h1, route, _gather_pairs(ys, dest))


def _qkv_kernel(x_ref, gq_ref, gkv_ref, wq_ref, wk_ref, wvt_ref, qn_ref, kn_ref, seg_ref,
                q_ref, k_ref, vt_ref, xs_ref, *, rate, n, n_chunks):
    seg = seg_ref[...]
    width = seg.shape[0]
    res_per_chunk = q_ref.shape[1]
    chunk = pl.program_id(2)

    def head_norm(t, gain):
        cols = []
        for j in range(t.shape[1] // width):
            tj = t[:, j * width:(j + 1) * width]
            ms = jnp.dot((tj * tj).astype(BF16), seg, preferred_element_type=F32)
            cols.append(tj * lax.rsqrt(ms + EPS))
        return jnp.concatenate(cols, axis=1) * gain

    def project(x):
        y = _rms(x)
        xq = (y * gq_ref[...]).astype(BF16)
        xkv = (y * gkv_ref[...]).astype(BF16)
        q = head_norm(jnp.dot(xq, wq_ref[...], preferred_element_type=F32), qn_ref[...]).astype(q_ref.dtype)
        k = head_norm(jnp.dot(xkv, wk_ref[...], preferred_element_type=F32), kn_ref[...]).astype(k_ref.dtype)
        vt = lax.dot_general(wvt_ref[...], xkv, NT_DIMS, preferred_element_type=F32).astype(vt_ref.dtype)
        for j in range(res_per_chunk):
            q_ref[0, j] = q[j * n:(j + 1) * n]
            k_ref[0, j] = k[j * n:(j + 1) * n]
            vt_ref[0, j] = vt[:, j * n:(j + 1) * n]

    if rate == 1:
        project(x_ref[...])
        return

    @pl.when(chunk == 0)
    def _():
        for j in range(xs_ref.shape[0]):
            xs_ref[j] = x_ref[:, j * LANES:(j + 1) * LANES]

    for ch in range(n_chunks):
        @pl.when(chunk == ch)
        def _(ch=ch):
            residues = range(ch * res_per_chunk, (ch + 1) * res_per_chunk)
            project(jnp.concatenate(
                [jnp.concatenate([xs_ref[j, pl.ds(c, n, stride=rate), :] for j in range(xs_ref.shape[0])], axis=1)
                 for c in residues], axis=0))


def _qkv(h, B, S, rate, gq, gkv, wq, wk, wv, qn, kn):
    D = h.shape[1]
    L = S // rate
    hd = DIL_HEAD_DIM
    n = max(SUB_BLOCK, TOKEN_TILE // rate)
    tm = n * rate
    res_per_chunk = max(1, TOKEN_TILE // n)
    n_chunks = rate // res_per_chunk
    width = 2 * LANES
    ii = jnp.arange(width)
    seg = jnp.where((ii[:, None] // hd) == (ii[None, :] // hd), 1.0 / hd, 0.0).astype(BF16)
    row = lambda g: jnp.tile(g, D // hd).reshape(1, D)
    const = lambda shape: _resident(shape, lambda b, i, c: (0,) * len(shape))
    qk_spec = pl.BlockSpec((1, res_per_chunk, n, D), lambda b, i, c: (b, c, i, 0))
    qk_shape = jax.ShapeDtypeStruct((B, rate, L, D), BF16)
    return pl.pallas_call(
        functools.partial(_qkv_kernel, rate=rate, n=n, n_chunks=n_chunks),
        grid=(B, S // tm, n_chunks),
        in_specs=[pl.BlockSpec((tm, D), lambda b, i, c: (b * (S // tm) + i, 0)),
                  const((1, D)), const((1, D)), const((D, D)), const((D, D)), const((D, D)),
                  const((1, D)), const((1, D)), const((width, width))],
        out_specs=[qk_spec, qk_spec,
                   pl.BlockSpec((1, res_per_chunk, D, n), lambda b, i, c: (b, c, 0, i))],
        out_shape=[qk_shape, qk_shape, jax.ShapeDtypeStruct((B, rate, D, L), BF16)],
        scratch_shapes=[pltpu.VMEM((D // LANES, tm if rate > 1 else 8, LANES), F32)],
        compiler_params=_params("parallel", "parallel", "arbitrary"),
        name="qkv_rate%d" % rate,
    )(h, gq.reshape(1, D), gkv.reshape(1, D), wq, wk, wv.T, row(qn) * (hd ** -0.5 * LOG2E), row(kn), seg)


def _attn_kernel(*refs, rate, with_prev):
    if rate == 1:
        q_ref, kc_ref, kp_ref, vc_ref, vp_ref, edge_ref, bias_ref, o_ref, lse_ref, os_ref = refs
    elif with_prev:
        q_ref, kc_ref, kp_ref, vc_ref, vp_ref, edge_ref, o_ref, lse_ref, os_ref = refs
    else:
        q_ref, kc_ref, vc_ref, edge_ref, o_ref, lse_ref, os_ref = refs
    Bk = SUB_BLOCK
    hd = DIL_HEAD_DIM
    n_pairs = DIL_HEADS // 2
    lane = lax.broadcasted_iota(jnp.int32, (Bk, LANES), 1)

    def block(load_q, load_k, load_vt, load_bias, store):
        out_t, lse_t = [], []
        for p in range(n_pairs):
            qp = load_q(p)
            zero = jnp.zeros_like(qp)
            q2 = jnp.concatenate([jnp.where(lane < hd, qp, zero), jnp.where(lane >= hd, qp, zero)], axis=0)
            s = lax.dot_general(load_k(p), q2, NT_DIMS, preferred_element_type=F32) + load_bias(p)
            m = jnp.max(s, axis=0, keepdims=True)
            pr = jnp.exp2(s - m)
            l = jnp.sum(pr, axis=0, keepdims=True)
            pb = pr.astype(BF16)
            vt = load_vt(p)
            out_t.append(jnp.dot(vt[:hd], pb[:, :Bk], preferred_element_type=F32) / l[:, :Bk])
            out_t.append(jnp.dot(vt[hd:], pb[:, Bk:], preferred_element_type=F32) / l[:, Bk:])
            lse = (m + jnp.log2(l)) * LN2
            lse_t += [lse[:, :Bk], lse[:, Bk:]]
        lse_t.append(jnp.zeros((LANES - DIL_HEADS, Bk), F32))
        store(jnp.concatenate(out_t, axis=0).T, jnp.concatenate(lse_t, axis=0).T)

    def pair_cols(p):
        return slice(p * LANES, (p + 1) * LANES)

    edge_bias = lambda p: edge_ref[0, p]
    if rate == 1:
        for j in range(q_ref.shape[2] // Bk):
            rows = slice(j * Bk, (j + 1) * Bk)
            if j == 0:
                load_k = lambda p: jnp.concatenate([kp_ref[0, 0, :, pair_cols(p)],
                                                    kc_ref[0, 0, :Bk, pair_cols(p)]], axis=0)
                load_vt = lambda p: jnp.concatenate([vp_ref[0, 0, pair_cols(p), :],
                                                     vc_ref[0, 0, pair_cols(p), :Bk]], axis=1)
                load_bias = edge_bias
            else:
                krows = slice((j - 1) * Bk, (j + 1) * Bk)
                load_k = lambda p, krows=krows: kc_ref[0, 0, krows, pair_cols(p)]
                load_vt = lambda p, krows=krows: vc_ref[0, 0, pair_cols(p), krows]
                load_bias = lambda p: bias_ref[p]

            def store(o, lse, rows=rows):
                o_ref[rows, :] = o.astype(o_ref.dtype)
                lse_ref[rows, :] = lse

            block(lambda p, rows=rows: q_ref[0, 0, rows, pair_cols(p)], load_k, load_vt, load_bias, store)
    else:
        for c in range(q_ref.shape[1]):
            if with_prev:
                load_k = lambda p, c=c: jnp.concatenate([kp_ref[0, c, :, pair_cols(p)],
                                                         kc_ref[0, c, :, pair_cols(p)]], axis=0)
                load_vt = lambda p, c=c: jnp.concatenate([vp_ref[0, c, pair_cols(p), :],
                                                          vc_ref[0, c, pair_cols(p), :]], axis=1)
            else:
                load_k = lambda p, c=c: kc_ref[0, c, :, pair_cols(p)]
                load_vt = lambda p, c=c: vc_ref[0, c, pair_cols(p), :]

            def store(o, lse, c=c):
                for j in range(os_ref.shape[0]):
                    os_ref[j, pl.ds(c, Bk, stride=rate), :] = o[:, pair_cols(j)]
                lse_ref[pl.ds(c, Bk, stride=rate), :] = lse

            block(lambda p, c=c: q_ref[0, c, :, pair_cols(p)], load_k, load_vt, edge_bias, store)
        for j in range(os_ref.shape[0]):
            o_ref[:, pair_cols(j)] = os_ref[j].astype(o_ref.dtype)


def _t5_bucket(n):
    max_exact = NUM_BUCKETS // 2
    nf = jnp.maximum(n, max_exact).astype(F32)
    large = max_exact + (jnp.log(nf / max_exact) / math.log(MAX_DISTANCE / max_exact)
                         * (NUM_BUCKETS - max_exact)).astype(jnp.int32)
    large = jnp.minimum(large, NUM_BUCKETS - 1)
    return jnp.where(n < max_exact, n, large)


def _group_attention(q, k, vt, bias_table, rate, n_steps):
    B, _, L, D = q.shape
    S = L * rate
    Bk = SUB_BLOCK
    with_prev = L > Bk
    nk = 2 * Bk if with_prev else Bk
    n = max(Bk, TOKEN_TILE // rate)
    tm = n * rate
    nt = S // tm
    n_pairs = DIL_HEADS // 2
    ql = jnp.arange(Bk, dtype=jnp.int32)[:, None]
    kl = jnp.arange(2 * Bk, dtype=jnp.int32)[None, :]
    steps = ql + Bk - kl
    bucket = _t5_bucket(jnp.maximum(steps, 0) * rate)
    buckets = jnp.arange(NUM_BUCKETS, dtype=jnp.int32)
    bias = jnp.sum(jnp.where(bucket[None, :, :, None] == buckets[:, None, None, None],
                             bias_table.astype(F32)[:, None, None, :], 0.0), axis=0)
    band = ((steps >= 0) & (steps <= n_steps))[:, :, None]
    first = (kl < Bk)[:, :, None]

    def layout(t):
        t = t[:, 2 * Bk - nk:, :].transpose(1, 2, 0)
        return t.reshape(nk, n_pairs, 2 * Bk).transpose(1, 0, 2)

    bias_in = layout(jnp.where(band, bias * LOG2E, NEG_INF))
    bias_first = layout(jnp.where(band & ~first, bias * LOG2E, NEG_INF))
    edge = jnp.stack([bias_first, bias_in])

    cur_qk = pl.BlockSpec((1, rate, n, D), lambda b, i: (b, 0, i, 0))
    cur_vt = pl.BlockSpec((1, rate, D, n), lambda b, i: (b, 0, 0, i))
    per_n = n // Bk
    prev_qk = pl.BlockSpec((1, rate, Bk, D), lambda b, i: (b, 0, jnp.maximum(i * per_n - 1, 0), 0))
    prev_vt = pl.BlockSpec((1, rate, D, Bk), lambda b, i: (b, 0, 0, jnp.maximum(i * per_n - 1, 0)))
    edge_spec = pl.BlockSpec((1, n_pairs, nk, 2 * Bk), lambda b, i: (jnp.minimum(i, 1), 0, 0, 0))
    if with_prev:
        in_specs = [cur_qk, cur_qk, prev_qk, cur_vt, prev_vt, edge_spec]
        args = (q, k, k, vt, vt, edge)
    else:
        in_specs = [cur_qk, cur_qk, cur_vt, edge_spec]
        args = (q, k, vt, edge)
    if rate == 1:
        in_specs.append(_resident((n_pairs, nk, 2 * Bk), lambda b, i: (0, 0, 0)))
        args += (bias_in,)
    return pl.pallas_call(
        functools.partial(_attn_kernel, rate=rate, with_prev=with_prev),
        grid=(B, nt),
        in_specs=in_specs,
        out_specs=[pl.BlockSpec((tm, D), lambda b, i: (b * nt + i, 0)),
                   pl.BlockSpec((tm, LANES), lambda b, i: (b * nt + i, 0))],
        out_shape=[jax.ShapeDtypeStruct((B * S, D), BF16),
                   jax.ShapeDtypeStruct((B * S, LANES), F32)],
        scratch_shapes=[pltpu.VMEM((D // LANES, tm if rate > 1 else 8, LANES), F32)],
        compiler_params=_params("parallel", "parallel"),
        name="attn_rate%d" % rate,
    )(*args)


def kernel(x, ret_w_in, ret_w_out, kv_norm, w_kv, k_norm, dil_wq, q_norm, dil_wo, rel_bias,
           mixer_norm, ffn_norm, router_grp, router_grp_b, router_exp, router_exp_b,
           exp_gate, exp_up, exp_down):
    B, S, D = x.shape
    h = x.reshape(B * S, D)

    def moe_layer(layer, mixer, w_out, h):
        h1, *hn, route, route_t, counts = _post(
            mixer, w_out.astype(BF16), h, ffn_norm[layer], router_grp[layer], router_grp_b[layer],
            router_exp[layer], router_exp_b[layer])
        return _moe(h1, hn, route, route_t, counts, layer, exp_gate, exp_up, exp_down)

    h = moe_layer(0, (mixer_norm[0], ret_w_in[0].astype(BF16), S), ret_w_out[0], h)

    G = len(DIL_RATES)
    gd = DIL_HEADS * DIL_HEAD_DIM
    outs, lses = [], []
    for g in range(G):
        cq = slice(g * gd, (g + 1) * gd)
        cv = slice(G * gd + g * gd, G * gd + (g + 1) * gd)
        q, k, vt = _qkv(h, B, S, DIL_RATES[g], mixer_norm[1], kv_norm,
                        dil_wq[0][:, cq].astype(BF16), w_kv[:, cq].astype(BF16), w_kv[:, cv].astype(BF16),
                        q_norm[0][g], k_norm[g])
        o, lse = _group_attention(q, k, vt, rel_bias[:, g * DIL_HEADS:(g + 1) * DIL_HEADS],
                                  DIL_RATES[g], DIL_WINDOWS[g] // DIL_RATES[g])
        outs.append(o)
        lses.append(lse)
    h = moe_layer(1, (tuple(outs), tuple(lses)), dil_wo[0], h)
    return h.reshape(B, S, D)
```

```python
import functools
import math

import jax
import jax.numpy as jnp
from jax import lax
from jax.experimental import pallas as pl
from jax.experimental.pallas import tpu as pltpu
from jax.experimental.pallas import tpu_sc as plsc

F32 = jnp.float32
BF16 = jnp.bfloat16

EPS = 1e-6
NEG_INF = -1e30

RET_HEADS = 4
RET_CHUNK = 256
ROPE_BASE = 10000.0

DIL_WINDOWS = (128, 512, 2048)
DIL_RATES = (1, 4, 16)
DIL_HEADS = 16
DIL_HEAD_DIM = 64
SUB_BLOCK = 128
NUM_BUCKETS = 32
MAX_DISTANCE = 2048

MOE_GROUPS = 4
EXPERTS_PER_GROUP = 8
N_EXPERTS = MOE_GROUPS * EXPERTS_PER_GROUP
MOE_BLOCK = 512

LANES = 128
ROW_TILE = 1024
TOKEN_TILE = 512
ATTN_TILE = 1024
VMEM_LIMIT = 56 * 1024 * 1024

NT_DIMS = (((1,), (1,)), ((), ()))
LOG2E = math.log2(math.e)
LN2 = math.log(2.0)


def _params(*sem):
    return pltpu.CompilerParams(dimension_semantics=sem, vmem_limit_bytes=VMEM_LIMIT)


def _resident(shape, index_map):
    return pl.BlockSpec(shape, index_map, pipeline_mode=pl.Buffered(1))


def _rms(x):
    return x * lax.rsqrt(jnp.mean(x * x, axis=-1, keepdims=True) + EPS)


def _retention_tile(x_ref, g_ref, w_ref, cos_ref, sin_ref, din_ref, xi_ref, zeta_ref, cd_ref, state_ref,
                    first_tile):
    C = RET_CHUNK
    H, dk, dv = state_ref.shape
    half = dk // 2
    ts = x_ref.shape[0]
    k_scale = dk ** -0.5

    @pl.when(first_tile)
    def _():
        state_ref[...] = jnp.zeros_like(state_ref)

    xn = (_rms(x_ref[...]) * g_ref[...]).astype(BF16)
    cos, sin = cos_ref[...], sin_ref[...]

    def proj(c0, width):
        return jnp.dot(xn, w_ref[:, c0:c0 + width], preferred_element_type=F32)

    def rot(t):
        t1, t2 = t[:, :half], t[:, half:]
        return jnp.concatenate([t1 * cos - t2 * sin, t1 * sin + t2 * cos], axis=1)

    heads = []
    for hh in range(H):
        q = rot(proj(hh * dk, dk)).astype(BF16)
        k = rot(proj(H * dk + hh * dk, dk)) * k_scale
        kb = k.astype(BF16)
        v = proj(2 * H * dk + hh * dv, dv).astype(BF16)
        gate = proj(2 * H * dk + H * dv + hh * dv, dv)
        gate = gate * jax.nn.sigmoid(gate)
        chunks = []
        for c in range(ts // C):
            rows = slice(c * C, (c + 1) * C)
            s = lax.dot_general(q[rows], kb[rows], NT_DIMS, preferred_element_type=F32) * din_ref[hh]
            inner = jnp.dot(s.astype(BF16), v[rows], preferred_element_type=F32)
            state = state_ref[hh]
            cross = jnp.dot(q[rows], state.astype(BF16), preferred_element_type=F32) * xi_ref[hh]
            kz_t = (k[rows] * zeta_ref[hh]).T.astype(BF16)
            state_ref[hh] = state * cd_ref[hh] + jnp.dot(kz_t, v[rows], preferred_element_type=F32)
            chunks.append((gate[rows] * _rms(inner + cross)).astype(BF16))
        heads.append(jnp.concatenate(chunks, axis=0))
    return jnp.concatenate(heads, axis=1)


def _retention_operands(gain, w, S, ts):
    H, C = RET_HEADS, RET_CHUNK
    D = w.shape[0]
    dk = D // H
    half = dk // 2
    nt = S // ts
    pos = jnp.arange(S, dtype=F32)
    inv = 1.0 / (ROPE_BASE ** jnp.linspace(0.0, 1.0, half, dtype=F32))
    ang = pos[:, None] * inv[None, :]
    log_g = jnp.log(1.0 - 2.0 ** (-5.0 - jnp.arange(H, dtype=F32)))
    idx = jnp.arange(C, dtype=F32)
    diff = idx[:, None] - idx[None, :]
    d_in = jnp.where(diff >= 0, jnp.exp(log_g[:, None, None] * jnp.maximum(diff, 0.0)), 0.0)
    xi = jnp.exp(log_g[:, None] * (idx + 1.0))[:, :, None]
    zeta = jnp.exp(log_g[:, None] * (C - 1.0 - idx))[:, :, None]
    chunk_decay = jnp.exp(log_g * C)[:, None, None]
    table = pl.BlockSpec((ts, half), lambda i: (i % nt, 0))
    per_head = lambda shape: _resident((H,) + shape, lambda i: (0, 0, 0))
    args = (gain.reshape(1, D), w, jnp.cos(ang), jnp.sin(ang), d_in, xi, zeta, chunk_decay)
    specs = [_resident((1, D), lambda i: (0, 0)), _resident(w.shape, lambda i: (0, 0)), table, table,
             per_head((C, C)), per_head((C, 1)), per_head((C, 1)), per_head((1, 1))]
    state = pltpu.VMEM((H, dk, 2 * dk), F32)
    return args, specs, state


ROUTE_E1, ROUTE_E2, ROUTE_G1, ROUTE_G2, ROUTE_R1, ROUTE_R2 = range(6)
ROUTE_ROWS = 8
ROUTE_PIECES = 2
ROUTER_EXP_LANE0 = MOE_GROUPS
HI16 = 0xFFFF0000


def _pack_bf16_pair(a, b):
    ua = lax.bitcast_convert_type(a.astype(BF16).astype(F32), jnp.uint32)
    ub = lax.bitcast_convert_type(b.astype(BF16).astype(F32), jnp.uint32)
    return ua | (ub >> 16)


def _unpack_bf16_pair(w):
    a = lax.bitcast_convert_type(w & jnp.uint32(HI16), F32)
    b = lax.bitcast_convert_type(w << 16, F32)
    return a, b


ROW_PARTS = 2


def _part_cols(D, part):
    w = D // 2 // ROW_PARTS
    return slice(part * w, (part + 1) * w), slice(D // 2 + part * w, D // 2 + (part + 1) * w)


def _pack_part(x, part):
    hi, lo = _part_cols(x.shape[1], part)
    return _pack_bf16_pair(x[:, hi], x[:, lo])


def _merge_groups(o_refs, l_refs, ex_ref):
    ex = ex_ref[...]

    def expand(w):
        hi = w.astype(BF16)
        lo = (w - hi.astype(F32)).astype(BF16)
        return jnp.dot(hi, ex, preferred_element_type=F32) + jnp.dot(lo, ex, preferred_element_type=F32)

    lses = [l_ref[...] for l_ref in l_refs]
    top = functools.reduce(jnp.maximum, lses)
    ws = [jnp.exp(l - top) for l in lses]
    den = sum(ws)
    return sum(expand(w / den) * o_ref[...].astype(F32) for w, o_ref in zip(ws, o_refs)).astype(BF16)


def _post_kernel(*refs, n_groups, seq_tiles):
    n_a = 2 * n_groups + 1 if n_groups else 8
    a_refs = refs[:n_a]
    (w_ref, h_ref, g_ref, wr_ref, br_ref, tri_ref,
     h1_ref, hn0_ref, hn1_ref, route_ref, route_t_ref, cnt_ref, carry_ref) = refs[n_a:n_a + 13]
    hn_refs = (hn0_ref, hn1_ref)
    if n_groups:
        a = _merge_groups(a_refs[:n_groups], a_refs[n_groups:2 * n_groups], a_refs[-1])
    else:
        a = _retention_tile(h_ref, *a_refs, refs[-1], pl.program_id(0) % seq_tiles == 0)
    @pl.when(pl.program_id(0) == 0)
    def _():
        carry_ref[...] = jnp.zeros_like(carry_ref)

    half = tri_ref.shape[0]
    n_pieces = h_ref.shape[0] // half
    D = h_ref.shape[1]
    lane = lax.broadcasted_iota(jnp.int32, (half, LANES), 1).astype(F32)
    ninf = -jnp.inf

    def first_argmax(vals):
        top = jnp.max(vals, axis=1, keepdims=True)
        where = jnp.min(jnp.where(vals == top, lane, float(LANES)), axis=1, keepdims=True)
        return top, where

    h1_ref[...] = h_ref[...] + jnp.dot(a, w_ref[...], preferred_element_type=F32)

    def route_half(rows):
        hn = _rms(h1_ref[rows, :]) * g_ref[...]
        for part, ref in enumerate(hn_refs):
            ref[rows, :] = _pack_part(hn, part)
        hi = hn.astype(BF16)
        lo = (hn - hi.astype(F32)).astype(BF16)
        logits = jnp.dot(jnp.concatenate([hi, hi, lo], axis=1), wr_ref[...],
                         preferred_element_type=F32) + br_ref[...]

        is_grp = lane < MOE_GROUPS
        lg = jnp.where(is_grp, logits, ninf)
        mg, grp = first_argmax(lg)
        p_grp = 1.0 / jnp.sum(jnp.where(is_grp, jnp.exp(lg - mg), 0.0), axis=1, keepdims=True)

        e_lane = lane - ROUTER_EXP_LANE0
        in_grp = (e_lane < N_EXPERTS) & (jnp.floor(e_lane * (1.0 / EXPERTS_PER_GROUP)) == grp)
        le = jnp.where(in_grp, logits, ninf)
        v1, i1 = first_argmax(le)
        le2 = jnp.where(lane == i1, ninf, le)
        v2, i2 = first_argmax(le2)
        e = jnp.exp(v2 - v1)
        hit1 = lane == (i1 - ROUTER_EXP_LANE0)
        hit2 = lane == (i2 - ROUTER_EXP_LANE0)
        onehot = jnp.where(hit1 | hit2, 1.0, 0.0)
        earlier = jnp.dot(tri_ref[...], onehot.astype(BF16), preferred_element_type=F32)
        return dict(e1=i1 - ROUTER_EXP_LANE0, e2=i2 - ROUTER_EXP_LANE0, g1=p_grp / (1.0 + e),
                    g2=p_grp * e / (1.0 + e), hit1=hit1, hit2=hit2, earlier=earlier,
                    count=jnp.sum(onehot, axis=0, keepdims=True))

    halves = [route_half(slice(j * half, (j + 1) * half)) for j in range(n_pieces)]
    carry = carry_ref[...]
    for j, r in enumerate(halves):
        before = carry + r["earlier"]
        r1 = jnp.sum(jnp.where(r["hit1"], before, 0.0), axis=1, keepdims=True)
        r2 = jnp.sum(jnp.where(r["hit2"], before, 0.0), axis=1, keepdims=True)
        carry = carry + r["count"]
        route = jnp.zeros((half, LANES), F32)
        for slot, val in ((ROUTE_E1, r["e1"]), (ROUTE_E2, r["e2"]), (ROUTE_G1, r["g1"]),
                          (ROUTE_G2, r["g2"]), (ROUTE_R1, r1), (ROUTE_R2, r2)):
            route = jnp.where(lane == slot, val, route)
        route_ref[j * half:(j + 1) * half, :] = route
        route_t_ref[:, j * half:(j + 1) * half] = route.T[:ROUTE_ROWS]
    carry_ref[...] = carry
    cnt_ref[...] = carry


def _post(mixer, w, h, gain, w_grp, b_grp, w_exp, b_exp):
    T, D = h.shape
    K = w.shape[0]
    row_block = lambda width: pl.BlockSpec((tm, width), lambda i: (i, 0))
    scratch = [pltpu.VMEM((1, LANES), F32)]
    if len(mixer) == 2:
        outs, lses = mixer
        n_groups, seq_tiles = len(outs), 1
        tm = min(TOKEN_TILE, T)
        ex = jnp.where(jnp.arange(LANES)[:, None] == (jnp.arange(K)[None, :] // DIL_HEAD_DIM), 1.0, 0.0).astype(BF16)
        a_args = (*outs, *lses, ex)
        a_specs = [row_block(K)] * n_groups + [row_block(LANES)] * n_groups + [_resident((LANES, K), lambda i: (0, 0))]
    else:
        mix_gain, w_in, S = mixer
        tm = min(TOKEN_TILE, S)
        n_groups, seq_tiles = 0, S // tm
        a_args, a_specs, state = _retention_operands(mix_gain, w_in, S, tm)
        scratch.append(state)
    n_r = MOE_GROUPS + N_EXPERTS
    wr = jnp.zeros((D, LANES), F32).at[:, :n_r].set(jnp.concatenate([w_grp, w_exp], axis=1))
    wr_hi = wr.astype(BF16)
    wr_lo = (wr - wr_hi.astype(F32)).astype(BF16)
    wr3 = jnp.concatenate([wr_hi, wr_lo, wr_hi], axis=0)
    br = jnp.zeros((1, LANES), F32).at[0, :n_r].set(jnp.concatenate([b_grp, b_exp]))
    piece = tm // ROUTE_PIECES
    tri = jnp.tril(jnp.ones((piece, piece), BF16), k=-1)
    return pl.pallas_call(
        functools.partial(_post_kernel, n_groups=n_groups, seq_tiles=seq_tiles),
        grid=(T // tm,),
        in_specs=a_specs + [
                  _resident((K, D), lambda i: (0, 0)),
                  pl.BlockSpec((tm, D), lambda i: (i, 0)),
                  _resident((1, D), lambda i: (0, 0)),
                  _resident((3 * D, LANES), lambda i: (0, 0)),
                  _resident((1, LANES), lambda i: (0, 0)),
                  _resident((piece, piece), lambda i: (0, 0))],
        out_specs=[pl.BlockSpec((tm, D), lambda i: (i, 0))]
                  + [pl.BlockSpec((tm, D // 2 // ROW_PARTS), lambda i: (i, 0))] * ROW_PARTS
                  + [pl.BlockSpec((tm, LANES), lambda i: (i, 0)),
                   pl.BlockSpec((ROUTE_ROWS, tm), lambda i: (0, i)),
                   pl.BlockSpec((1, LANES), lambda i: (0, 0))],
        out_shape=[jax.ShapeDtypeStruct((T, D), F32)]
                  + [jax.ShapeDtypeStruct((T, D // 2 // ROW_PARTS), jnp.uint32)] * ROW_PARTS
                  + [jax.ShapeDtypeStruct((T, LANES), F32),
                   jax.ShapeDtypeStruct((ROUTE_ROWS, T), F32),
                   jax.ShapeDtypeStruct((1, LANES), F32)],
        scratch_shapes=scratch,
        compiler_params=_params("arbitrary"),
        name="post_mixer",
    )(*a_args, w, h, gain.reshape(1, D), wr3, br, tri)


SC_WINDOW = 128


def _sc_mesh():
    return plsc.VectorSubcoreMesh(core_axis_name="core", subcore_axis_name="subcore")


def _sc_window_specs(W):
    rows = pl.BlockSpec((SC_WINDOW, W), lambda i: (i, 0))
    idx = pl.BlockSpec((1, SC_WINDOW), lambda i: (0, i))
    return rows, idx


def _sc_pipeline(body, n_rows, in_specs, out_specs):
    return pltpu.emit_pipeline(body, grid=(n_rows // SC_WINDOW,), in_specs=in_specs, out_specs=out_specs,
                               core_axis_name=("core", "subcore"), dimension_semantics=(pltpu.PARALLEL,))


def _dispatch(xs, dest, pad_idx, P):
    T, W = xs[0].shape
    n_pad = pad_idx.shape[1]
    n = len(xs)
    rows, idx = _sc_window_specs(W)
    zero_rows = pl.BlockSpec((SC_WINDOW, W), lambda i: (0, 0))
    out = jax.ShapeDtypeStruct((P, W), xs[0].dtype)

    @functools.partial(pl.kernel, out_type=(out,) * n, mesh=_sc_mesh(), scratch_types=[], name="dispatch")
    def scatter(*refs):
        x_hbm, (d0_hbm, d1_hbm, z_hbm, p_hbm), o_hbm = refs[:n], refs[n:n + 4], refs[n + 4:]
        for x, o in zip(x_hbm, o_hbm):
            def put_pair(x_vmem, i0_vmem, i1_vmem, o=o):
                pltpu.sync_copy(x_vmem, o.at[i0_vmem.at[0]])
                pltpu.sync_copy(x_vmem, o.at[i1_vmem.at[0]])

            def put(x_vmem, i_vmem, o=o):
                pltpu.sync_copy(x_vmem, o.at[i_vmem.at[0]])

            _sc_pipeline(put_pair, T, [rows, idx, idx], [])(x, d0_hbm, d1_hbm)
            _sc_pipeline(put, n_pad, [zero_rows, idx], [])(z_hbm, p_hbm)

    return scatter(*xs, dest[0:1], dest[1:2], jnp.zeros((SC_WINDOW, W), xs[0].dtype), pad_idx)


WEIGHT_LEADS = (3, 2, 1)
EXPERT_SLOTS = max(WEIGHT_LEADS) + 1


def _expert_kernel(blk_exp_ref, slot_ref, n_used_ref, *refs):
    x_refs = refs[:ROW_PARTS]
    w_refs = refs[ROW_PARTS:ROW_PARTS + 3]
    o_refs = refs[ROW_PARTS + 3:2 * ROW_PARTS + 3]
    w_slots = refs[2 * ROW_PARTS + 3:]
    lead = max(WEIGHT_LEADS)
    n_used = n_used_ref[0]
    i = pl.program_id(0) - lead
    D = w_slots[0].shape[1]

    for w_ref, w_s, ahead in zip(w_refs, w_slots, WEIGHT_LEADS):
        j = i + ahead
        jc = jnp.clip(j, 0, n_used - 1)
        arrived = (j >= 0) & (j < n_used) & ((j == 0) | (blk_exp_ref[jc] != blk_exp_ref[jnp.maximum(jc - 1, 0)]))

        @pl.when(arrived)
        def _(w_ref=w_ref, w_s=w_s, jc=jc):
            w_s[slot_ref[jc]] = w_ref[0, 0].astype(BF16)

    @pl.when((i >= 0) & (i < n_used))
    def _():
        slot = slot_ref[jnp.clip(i, 0, n_used - 1)]
        wg_s, wu_s, wd_s = w_slots
        pieces = []
        for part, x_ref in enumerate(x_refs):
            for cols, val in zip(_part_cols(D, part), _unpack_bf16_pair(x_ref[...])):
                pieces.append((cols, val.astype(BF16)))

        def up(w_s):
            return sum(jnp.dot(val, w_s[slot, cols, :], preferred_element_type=F32) for cols, val in pieces)

        g = up(wg_s)
        hid = (g * jax.nn.sigmoid(g) * up(wu_s)).astype(BF16)
        y = jnp.dot(hid, wd_s[slot], preferred_element_type=F32)
        for part, o_ref in enumerate(o_refs):
            o_ref[...] = _pack_part(y, part)

    @pl.when(i >= n_used)
    def _():
        for o_ref in o_refs:
            o_ref[...] = jnp.zeros_like(o_ref)


def _experts(xs, blk_exp, n_used, layer, w_gate, w_up, w_down):
    P, W = xs[0].shape
    D = 2 * W * ROW_PARTS
    FF = w_gate.shape[3]
    nblk = P // MOE_BLOCK
    lead = max(WEIGHT_LEADS)
    changes = jnp.concatenate([jnp.zeros((1,), jnp.int32), (blk_exp[1:] != blk_exp[:-1]).astype(jnp.int32)])
    slot = jnp.cumsum(changes) % EXPERT_SLOTS

    def x_map(g, be, sl, nu):
        return (jnp.clip(g - lead, 0, nu[0] - 1), 0)

    def o_map(g, be, sl, nu):
        return (jnp.maximum(g - lead, 0), 0)

    def w_map(ahead):
        return lambda g, be, sl, nu: (layer, be[jnp.clip(g - lead + ahead, 0, nu[0] - 1)], 0, 0)

    wg_spec, wu_spec, wd_spec = (pl.BlockSpec((1, 1) + shape, w_map(ahead))
                                 for shape, ahead in zip(((D, FF), (D, FF), (FF, D)), WEIGHT_LEADS))
    return pl.pallas_call(
        _expert_kernel,
        grid_spec=pltpu.PrefetchScalarGridSpec(
            num_scalar_prefetch=3,
            grid=(nblk + lead,),
            in_specs=[pl.BlockSpec((MOE_BLOCK, W), x_map)] * ROW_PARTS + [wg_spec, wu_spec, wd_spec],
            out_specs=[pl.BlockSpec((MOE_BLOCK, W), o_map)] * ROW_PARTS,
            scratch_shapes=[pltpu.VMEM((EXPERT_SLOTS, D, FF), BF16), pltpu.VMEM((EXPERT_SLOTS, D, FF), BF16),
                            pltpu.VMEM((EXPERT_SLOTS, FF, D), BF16)]),
        out_shape=[jax.ShapeDtypeStruct((P, W), jnp.uint32)] * ROW_PARTS,
        compiler_params=_params("arbitrary"),
        name="experts",
    )(blk_exp, slot, n_used, *xs, w_gate, w_up, w_down)


def _gather_pairs(ys, dest):
    W = ys[0].shape[1]
    T = dest.shape[1]
    n = len(ys)
    rows, idx = _sc_window_specs(W)
    out = jax.ShapeDtypeStruct((T, W), ys[0].dtype)

    @functools.partial(pl.kernel, out_type=(out,) * (2 * n), mesh=_sc_mesh(), scratch_types=[],
                       name="gather_pairs")
    def gather(*refs):
        y_hbm, d_hbm, o_hbm = refs[:n], refs[n:n + 2], refs[n + 2:]
        for slot, d in enumerate(d_hbm):
            for y, o in zip(y_hbm, o_hbm[slot * n:(slot + 1) * n]):
                def get(i_vmem, o_vmem, y=y):
                    pltpu.sync_copy(y.at[i_vmem.at[0]], o_vmem)

                _sc_pipeline(get, T, [idx], [rows])(d, o)

    return gather(*ys, dest[0:1], dest[1:2])


def _combine_kernel(h_ref, route_ref, *refs):
    y_refs, o_ref = refs[:-1], refs[-1]
    D = h_ref.shape[1]
    route = route_ref[...]
    gates = (route[:, ROUTE_G1:ROUTE_G1 + 1], route[:, ROUTE_G2:ROUTE_G2 + 1])
    for part in range(ROW_PARTS):
        slots = [_unpack_bf16_pair(y_refs[slot * ROW_PARTS + part][...]) for slot in range(2)]
        for half, cols in enumerate(_part_cols(D, part)):
            o_ref[:, cols] = h_ref[:, cols] + (gates[0] * slots[0][half] + gates[1] * slots[1][half])


def _combine(h, route, pairs):
    T, D = h.shape
    W = pairs[0].shape[1]
    tm = min(ROW_TILE, T)
    return pl.pallas_call(
        _combine_kernel,
        grid=(T // tm,),
        in_specs=[pl.BlockSpec((tm, D), lambda i: (i, 0)), pl.BlockSpec((tm, LANES), lambda i: (i, 0))]
                 + [pl.BlockSpec((tm, W), lambda i: (i, 0))] * len(pairs),
        out_specs=pl.BlockSpec((tm, D), lambda i: (i, 0)),
        out_shape=jax.ShapeDtypeStruct((T, D), F32),
        compiler_params=_params("parallel"),
        name="combine",
    )(h, route, *pairs)


def _moe(h1, hn, route, route_t, counts, layer, w_gate, w_up, w_down):
    T, D = h1.shape
    A = 2 * T
    nblk = -(-A // MOE_BLOCK) + N_EXPERTS
    P = nblk * MOE_BLOCK
    eid = route_t[ROUTE_E1:ROUTE_E2 + 1].astype(jnp.int32)
    rank = route_t[ROUTE_R1:ROUTE_R2 + 1].astype(jnp.int32)
    cnt = counts[0, :N_EXPERTS].astype(jnp.int32)
    padded = (cnt + MOE_BLOCK - 1) // MOE_BLOCK * MOE_BLOCK
    pends = jnp.cumsum(padded)
    pstarts = pends - padded
    experts = jnp.arange(N_EXPERTS, dtype=jnp.int32)
    start_of = jnp.sum(jnp.where(eid[:, None, :] == experts[None, :, None], pstarts[None, :, None], 0), axis=1)
    dest = start_of + rank
    blk_start = jnp.arange(nblk, dtype=jnp.int32) * MOE_BLOCK
    blk_exp = jnp.minimum(jnp.sum((pends[None, :] <= blk_start[:, None]).astype(jnp.int32), axis=1),
                          N_EXPERTS - 1)
    n_used = pends[-1:] // MOE_BLOCK

    gap_start = jnp.concatenate([pstarts + cnt, pends[-1:]])
    gap_len = jnp.concatenate([padded - cnt, P - pends[-1:]])
    gap_end = jnp.cumsum(gap_len)
    j = jnp.arange(P - A, dtype=jnp.int32)
    gap_of = jnp.sum((gap_end[None, :] <= j[:, None]).astype(jnp.int32), axis=1)
    sel = gap_of[:, None] == jnp.arange(N_EXPERTS + 1, dtype=jnp.int32)[None, :]
    pad_idx = jnp.sum(jnp.where(sel, (gap_start - (gap_end - gap_len))[None, :] + j[:, None], 0), axis=1)

    xs = _dispatch(hn, dest, pad_idx.reshape(1, P - A), P)
    ys = _experts(xs, blk_exp, n_used, layer, w_gate, w_up, w_down)
    return _combine(h1, route, _gather_pairs(ys, dest))


def _qkv_kernel(x_ref, gq_ref, gkv_ref, wq_ref, wk_ref, wvt_ref, qn_ref, kn_ref, seg_ref,
                q_ref, k_ref, vt_ref, xs_ref, *, rate, n, n_chunks):
    seg = seg_ref[...]
    width = seg.shape[0]
    res_per_chunk = q_ref.shape[1]
    chunk = pl.program_id(2)

    def head_norm(t, gain):
        cols = []
        for j in range(t.shape[1] // width):
            tj = t[:, j * width:(j + 1) * width]
            ms = jnp.dot((tj * tj).astype(BF16), seg, preferred_element_type=F32)
            cols.append(tj * lax.rsqrt(ms + EPS))
        return jnp.concatenate(cols, axis=1) * gain

    def project(x):
        y = _rms(x)
        xq = (y * gq_ref[...]).astype(BF16)
        xkv = (y * gkv_ref[...]).astype(BF16)
        q = head_norm(jnp.dot(xq, wq_ref[...], preferred_element_type=F32), qn_ref[...]).astype(q_ref.dtype)
        k = head_norm(jnp.dot(xkv, wk_ref[...], preferred_element_type=F32), kn_ref[...]).astype(k_ref.dtype)
        vt = lax.dot_general(wvt_ref[...], xkv, NT_DIMS, preferred_element_type=F32).astype(vt_ref.dtype)
        for j in range(res_per_chunk):
            q_ref[0, j] = q[j * n:(j + 1) * n]
            k_ref[0, j] = k[j * n:(j + 1) * n]
            vt_ref[0, j] = vt[:, j * n:(j + 1) * n]

    if rate == 1:
        project(x_ref[...])
        return

    @pl.when(chunk == 0)
    def _():
        for j in range(xs_ref.shape[0]):
            xs_ref[j] = x_ref[:, j * LANES:(j + 1) * LANES]

    for ch in range(n_chunks):
        @pl.when(chunk == ch)
        def _(ch=ch):
            residues = range(ch * res_per_chunk, (ch + 1) * res_per_chunk)
            project(jnp.concatenate(
                [jnp.concatenate([xs_ref[j, pl.ds(c, n, stride=rate), :] for j in range(xs_ref.shape[0])], axis=1)
                 for c in residues], axis=0))


def _qkv(h, B, S, rate, gq, gkv, wq, wk, wv, qn, kn):
    D = h.shape[1]
    L = S // rate
    hd = DIL_HEAD_DIM
    n = max(SUB_BLOCK, TOKEN_TILE // rate)
    tm = n * rate
    res_per_chunk = max(1, TOKEN_TILE // n)
    n_chunks = rate // res_per_chunk
    width = 2 * LANES
    ii = jnp.arange(width)
    seg = jnp.where((ii[:, None] // hd) == (ii[None, :] // hd), 1.0 / hd, 0.0).astype(BF16)
    row = lambda g: jnp.tile(g, D // hd).reshape(1, D)
    const = lambda shape: _resident(shape, lambda b, i, c: (0,) * len(shape))
    qk_spec = pl.BlockSpec((1, res_per_chunk, n, D), lambda b, i, c: (b, c, i, 0))
    qk_shape = jax.ShapeDtypeStruct((B, rate, L, D), BF16)
    return pl.pallas_call(
        functools.partial(_qkv_kernel, rate=rate, n=n, n_chunks=n_chunks),
        grid=(B, S // tm, n_chunks),
        in_specs=[pl.BlockSpec((tm, D), lambda b, i, c: (b * (S // tm) + i, 0)),
                  const((1, D)), const((1, D)), const((D, D)), const((D, D)), const((D, D)),
                  const((1, D)), const((1, D)), const((width, width))],
        out_specs=[qk_spec, qk_spec,
                   pl.BlockSpec((1, res_per_chunk, D, n), lambda b, i, c: (b, c, 0, i))],
        out_shape=[qk_shape, qk_shape, jax.ShapeDtypeStruct((B, rate, D, L), BF16)],
        scratch_shapes=[pltpu.VMEM((D // LANES, tm if rate > 1 else 8, LANES), F32)],
        compiler_params=_params("parallel", "parallel", "arbitrary"),
        name="qkv_rate%d" % rate,
    )(h, gq.reshape(1, D), gkv.reshape(1, D), wq, wk, wv.T, row(qn) * (hd ** -0.5 * LOG2E), row(kn), seg)


def _attn_kernel(*refs, rate, with_prev):
    if not with_prev:
        q_ref, kc_ref, vc_ref, edge_ref, o_ref, lse_ref, os_ref = refs
    elif len(refs) == 10:
        q_ref, kc_ref, kp_ref, vc_ref, vp_ref, edge_ref, bias_ref, o_ref, lse_ref, os_ref = refs
    else:
        q_ref, kc_ref, kp_ref, vc_ref, vp_ref, edge_ref, o_ref, lse_ref, os_ref = refs
    Bk = SUB_BLOCK
    hd = DIL_HEAD_DIM
    n_pairs = DIL_HEADS // 2
    lane = lax.broadcasted_iota(jnp.int32, (Bk, LANES), 1)

    def block(load_q, load_k, load_vt, load_bias, store):
        out_t, lse_t = [], []
        for p in range(n_pairs):
            qp = load_q(p)
            zero = jnp.zeros_like(qp)
            q2 = jnp.concatenate([jnp.where(lane < hd, qp, zero), jnp.where(lane >= hd, qp, zero)], axis=0)
            s = lax.dot_general(load_k(p), q2, NT_DIMS, preferred_element_type=F32) + load_bias(p)
            m = jnp.max(s, axis=0, keepdims=True)
            pr = jnp.exp2(s - m)
            l = jnp.sum(pr, axis=0, keepdims=True)
            pb = pr.astype(BF16)
            vt = load_vt(p)
            out_t.append(jnp.dot(vt[:hd], pb[:, :Bk], preferred_element_type=F32) / l[:, :Bk])
            out_t.append(jnp.dot(vt[hd:], pb[:, Bk:], preferred_element_type=F32) / l[:, Bk:])
            lse = (m + jnp.log2(l)) * LN2
            lse_t += [lse[:, :Bk], lse[:, Bk:]]
        lse_t.append(jnp.zeros((LANES - DIL_HEADS, Bk), F32))
        store(jnp.concatenate(out_t, axis=0).T, jnp.concatenate(lse_t, axis=0).T)

    def pair_cols(p):
        return slice(p * LANES, (p + 1) * LANES)

    n_res, n_blocks = kc_ref.shape[1], kc_ref.shape[2] // Bk
    for c in range(n_res):
        for j in range(n_blocks):
            rows = slice(j * Bk, (j + 1) * Bk)
            if not with_prev:
                load_k = lambda p, c=c: kc_ref[0, c, :, pair_cols(p)]
                load_vt = lambda p, c=c: vc_ref[0, c, pair_cols(p), :]
                load_bias = lambda p: edge_ref[0, p]
            elif j == 0:
                load_k = lambda p, c=c: jnp.concatenate([kp_ref[0, c, :, pair_cols(p)],
                                                         kc_ref[0, c, :Bk, pair_cols(p)]], axis=0)
                load_vt = lambda p, c=c: jnp.concatenate([vp_ref[0, c, pair_cols(p), :],
                                                          vc_ref[0, c, pair_cols(p), :Bk]], axis=1)
                load_bias = lambda p: edge_ref[0, p]
            else:
                krows = slice((j - 1) * Bk, (j + 1) * Bk)
                load_k = lambda p, c=c, krows=krows: kc_ref[0, c, krows, pair_cols(p)]
                load_vt = lambda p, c=c, krows=krows: vc_ref[0, c, pair_cols(p), krows]
                load_bias = lambda p: bias_ref[p]

            if rate == 1:
                def store(o, lse, rows=rows):
                    o_ref[rows, :] = o.astype(o_ref.dtype)
                    lse_ref[rows, :] = lse
            else:
                def store(o, lse, out_rows=pl.ds(c + j * Bk * rate, Bk, stride=rate)):
                    for g in range(os_ref.shape[0]):
                        os_ref[g, out_rows, :] = o[:, pair_cols(g)]
                    lse_ref[out_rows, :] = lse

            block(lambda p, c=c, rows=rows: q_ref[0, c, rows, pair_cols(p)], load_k, load_vt, load_bias, store)
    if rate > 1:
        for g in range(os_ref.shape[0]):
            o_ref[:, pair_cols(g)] = os_ref[g].astype(o_ref.dtype)


def _t5_bucket(n):
    max_exact = NUM_BUCKETS // 2
    nf = jnp.maximum(n, max_exact).astype(F32)
    large = max_exact + (jnp.log(nf / max_exact) / math.log(MAX_DISTANCE / max_exact)
                         * (NUM_BUCKETS - max_exact)).astype(jnp.int32)
    large = jnp.minimum(large, NUM_BUCKETS - 1)
    return jnp.where(n < max_exact, n, large)


def _group_attention(q, k, vt, bias_table, rate, n_steps):
    B, _, L, D = q.shape
    S = L * rate
    Bk = SUB_BLOCK
    with_prev = L > Bk
    nk = 2 * Bk if with_prev else Bk
    n = max(Bk, ATTN_TILE // rate)
    tm = n * rate
    nt = S // tm
    n_pairs = DIL_HEADS // 2
    ql = jnp.arange(Bk, dtype=jnp.int32)[:, None]
    kl = jnp.arange(2 * Bk, dtype=jnp.int32)[None, :]
    steps = ql + Bk - kl
    bucket = _t5_bucket(jnp.maximum(steps, 0) * rate)
    buckets = jnp.arange(NUM_BUCKETS, dtype=jnp.int32)
    bias = jnp.sum(jnp.where(bucket[None, :, :, None] == buckets[:, None, None, None],
                             bias_table.astype(F32)[:, None, None, :], 0.0), axis=0)
    band = ((steps >= 0) & (steps <= n_steps))[:, :, None]
    first = (kl < Bk)[:, :, None]

    def layout(t):
        t = t[:, 2 * Bk - nk:, :].transpose(1, 2, 0)
        return t.reshape(nk, n_pairs, 2 * Bk).transpose(1, 0, 2)

    bias_in = layout(jnp.where(band, bias * LOG2E, NEG_INF))
    bias_first = layout(jnp.where(band & ~first, bias * LOG2E, NEG_INF))
    edge = jnp.stack([bias_first, bias_in])

    cur_qk = pl.BlockSpec((1, rate, n, D), lambda b, i: (b, 0, i, 0))
    cur_vt = pl.BlockSpec((1, rate, D, n), lambda b, i: (b, 0, 0, i))
    per_n = n // Bk
    prev_qk = pl.BlockSpec((1, rate, Bk, D), lambda b, i: (b, 0, jnp.maximum(i * per_n - 1, 0), 0))
    prev_vt = pl.BlockSpec((1, rate, D, Bk), lambda b, i: (b, 0, 0, jnp.maximum(i * per_n - 1, 0)))
    edge_spec = pl.BlockSpec((1, n_pairs, nk, 2 * Bk), lambda b, i: (jnp.minimum(i, 1), 0, 0, 0))
    if with_prev:
        in_specs = [cur_qk, cur_qk, prev_qk, cur_vt, prev_vt, edge_spec]
        args = (q, k, k, vt, vt, edge)
    else:
        in_specs = [cur_qk, cur_qk, cur_vt, edge_spec]
        args = (q, k, vt, edge)
    if with_prev and n > Bk:
        in_specs.append(_resident((n_pairs, nk, 2 * Bk), lambda b, i: (0, 0, 0)))
        args += (bias_in,)
    return pl.pallas_call(
        functools.partial(_attn_kernel, rate=rate, with_prev=with_prev),
        grid=(B, nt),
        in_specs=in_specs,
        out_specs=[pl.BlockSpec((tm, D), lambda b, i: (b * nt + i, 0)),
                   pl.BlockSpec((tm, LANES), lambda b, i: (b * nt + i, 0))],
        out_shape=[jax.ShapeDtypeStruct((B * S, D), BF16),
                   jax.ShapeDtypeStruct((B * S, LANES), F32)],
        scratch_shapes=[pltpu.VMEM((D // LANES, tm if rate > 1 else 8, LANES), F32)],
        compiler_params=_params("parallel", "parallel"),
        name="attn_rate%d" % rate,
    )(*args)


def kernel(x, ret_w_in, ret_w_out, kv_norm, w_kv, k_norm, dil_wq, q_norm, dil_wo, rel_bias,
           mixer_norm, ffn_norm, router_grp, router_grp_b, router_exp, router_exp_b,
           exp_gate, exp_up, exp_down):
    B, S, D = x.shape
    h = x.reshape(B * S, D)

    def moe_layer(layer, mixer, w_out, h):
        h1, *hn, route, route_t, counts = _post(
            mixer, w_out.astype(BF16), h, ffn_norm[layer], router_grp[layer], router_grp_b[layer],
            router_exp[layer], router_exp_b[layer])
        return _moe(h1, hn, route, route_t, counts, layer, exp_gate, exp_up, exp_down)

    h = moe_layer(0, (mixer_norm[0], ret_w_in[0].astype(BF16), S), ret_w_out[0], h)

    G = len(DIL_RATES)
    gd = DIL_HEADS * DIL_HEAD_DIM
    outs, lses = [], []
    for g in range(G):
        cq = slice(g * gd, (g + 1) * gd)
        cv = slice(G * gd + g * gd, G * gd + (g + 1) * gd)
        q, k, vt = _qkv(h, B, S, DIL_RATES[g], mixer_norm[1], kv_norm,
                        dil_wq[0][:, cq].astype(BF16), w_kv[:, cq].astype(BF16), w_kv[:, cv].astype(BF16),
                        q_norm[0][g], k_norm[g])
        o, lse = _group_attention(q, k, vt, rel_bias[:, g * DIL_HEADS:(g + 1) * DIL_HEADS],
                                  DIL_RATES[g], DIL_WINDOWS[g] // DIL_RATES[g])
        outs.append(o)
        lses.append(lse)
    h = moe_layer(1, (tuple(outs), tuple(lses)), dil_wo[0], h)
    return h.reshape(B, S, D)
```

```python
import functools
import math

import jax
import jax.numpy as jnp
from jax import lax
from jax.experimental import pallas as pl
from jax.experimental.pallas import tpu as pltpu
from jax.experimental.pallas import tpu_sc as plsc

F32 = jnp.float32
BF16 = jnp.bfloat16

EPS = 1e-6
NEG_INF = -1e30

RET_HEADS = 4
RET_CHUNK = 256
ROPE_BASE = 10000.0

DIL_WINDOWS = (128, 512, 2048)
DIL_RATES = (1, 4, 16)
DIL_HEADS = 16
DIL_HEAD_DIM = 64
SUB_BLOCK = 128
NUM_BUCKETS = 32
MAX_DISTANCE = 2048

MOE_GROUPS = 4
EXPERTS_PER_GROUP = 8
N_EXPERTS = MOE_GROUPS * EXPERTS_PER_GROUP
MOE_BLOCK = 512

LANES = 128
ROW_TILE = 1024
TOKEN_TILE = 512
ATTN_TILE = 1024
VMEM_LIMIT = 56 * 1024 * 1024

NT_DIMS = (((1,), (1,)), ((), ()))
LOG2E = math.log2(math.e)
LN2 = math.log(2.0)


def _params(*sem):
    return pltpu.CompilerParams(dimension_semantics=sem, vmem_limit_bytes=VMEM_LIMIT)


def _resident(shape, index_map):
    return pl.BlockSpec(shape, index_map, pipeline_mode=pl.Buffered(1))


def _rms(x):
    return x * lax.rsqrt(jnp.mean(x * x, axis=-1, keepdims=True) + EPS)


def _retention_tile(x_ref, g_ref, w_ref, cos_ref, sin_ref, din_ref, xi_ref, zeta_ref, cd_ref, state_ref,
                    first_tile):
    C = RET_CHUNK
    H, dk, dv = state_ref.shape
    half = dk // 2
    ts = x_ref.shape[0]
    k_scale = dk ** -0.5

    @pl.when(first_tile)
    def _():
        state_ref[...] = jnp.zeros_like(state_ref)

    xn = (_rms(x_ref[...]) * g_ref[...]).astype(BF16)
    cos, sin = cos_ref[...], sin_ref[...]

    def proj(c0, width):
        return jnp.dot(xn, w_ref[:, c0:c0 + width], preferred_element_type=F32)

    def rot(t):
        t1, t2 = t[:, :half], t[:, half:]
        return jnp.concatenate([t1 * cos - t2 * sin, t1 * sin + t2 * cos], axis=1)

    heads = []
    for hh in range(H):
        q = rot(proj(hh * dk, dk)).astype(BF16)
        k = rot(proj(H * dk + hh * dk, dk)) * k_scale
        kb = k.astype(BF16)
        v = proj(2 * H * dk + hh * dv, dv).astype(BF16)
        gate = proj(2 * H * dk + H * dv + hh * dv, dv)
        gate = gate * jax.nn.sigmoid(gate)
        chunks = []
        for c in range(ts // C):
            rows = slice(c * C, (c + 1) * C)
            s = lax.dot_general(q[rows], kb[rows], NT_DIMS, preferred_element_type=F32) * din_ref[hh]
            inner = jnp.dot(s.astype(BF16), v[rows], preferred_element_type=F32)
            state = state_ref[hh]
            cross = jnp.dot(q[rows], state.astype(BF16), preferred_element_type=F32) * xi_ref[hh]
            kz_t = (k[rows] * zeta_ref[hh]).T.astype(BF16)
            state_ref[hh] = state * cd_ref[hh] + jnp.dot(kz_t, v[rows], preferred_element_type=F32)
            chunks.append((gate[rows] * _rms(inner + cross)).astype(BF16))
        heads.append(jnp.concatenate(chunks, axis=0))
    return jnp.concatenate(heads, axis=1)


def _retention_operands(gain, w, S, ts):
    H, C = RET_HEADS, RET_CHUNK
    D = w.shape[0]
    dk = D // H
    half = dk // 2
    nt = S // ts
    pos = jnp.arange(S, dtype=F32)
    inv = 1.0 / (ROPE_BASE ** jnp.linspace(0.0, 1.0, half, dtype=F32))
    ang = pos[:, None] * inv[None, :]
    log_g = jnp.log(1.0 - 2.0 ** (-5.0 - jnp.arange(H, dtype=F32)))
    idx = jnp.arange(C, dtype=F32)
    diff = idx[:, None] - idx[None, :]
    d_in = jnp.where(diff >= 0, jnp.exp(log_g[:, None, None] * jnp.maximum(diff, 0.0)), 0.0)
    xi = jnp.exp(log_g[:, None] * (idx + 1.0))[:, :, None]
    zeta = jnp.exp(log_g[:, None] * (C - 1.0 - idx))[:, :, None]
    chunk_decay = jnp.exp(log_g * C)[:, None, None]
    table = pl.BlockSpec((ts, half), lambda i: (i % nt, 0))
    per_head = lambda shape: _resident((H,) + shape, lambda i: (0, 0, 0))
    args = (gain.reshape(1, D), w, jnp.cos(ang), jnp.sin(ang), d_in, xi, zeta, chunk_decay)
    specs = [_resident((1, D), lambda i: (0, 0)), _resident(w.shape, lambda i: (0, 0)), table, table,
             per_head((C, C)), per_head((C, 1)), per_head((C, 1)), per_head((1, 1))]
    state = pltpu.VMEM((H, dk, 2 * dk), F32)
    return args, specs, state


ROUTE_E1, ROUTE_E2, ROUTE_G1, ROUTE_G2, ROUTE_R1, ROUTE_R2 = range(6)
ROUTE_ROWS = 8
ROUTE_PIECES = 2
ROUTER_EXP_LANE0 = MOE_GROUPS
HI16 = 0xFFFF0000


def _pack_bf16_pair(a, b):
    ua = lax.bitcast_convert_type(a.astype(BF16).astype(F32), jnp.uint32)
    ub = lax.bitcast_convert_type(b.astype(BF16).astype(F32), jnp.uint32)
    return ua | (ub >> 16)


def _unpack_bf16_pair(w):
    a = lax.bitcast_convert_type(w & jnp.uint32(HI16), F32)
    b = lax.bitcast_convert_type(w << 16, F32)
    return a, b


ROW_PARTS = 2


def _part_cols(D, part):
    w = D // 2 // ROW_PARTS
    return slice(part * w, (part + 1) * w), slice(D // 2 + part * w, D // 2 + (part + 1) * w)


def _pack_part(x, part):
    hi, lo = _part_cols(x.shape[1], part)
    return _pack_bf16_pair(x[:, hi], x[:, lo])


def _merge_groups(o_refs, l_refs, ex_ref):
    ex = ex_ref[...]

    def expand(w):
        hi = w.astype(BF16)
        lo = (w - hi.astype(F32)).astype(BF16)
        return jnp.dot(hi, ex, preferred_element_type=F32) + jnp.dot(lo, ex, preferred_element_type=F32)

    lses = [l_ref[...] for l_ref in l_refs]
    top = functools.reduce(jnp.maximum, lses)
    ws = [jnp.exp(l - top) for l in lses]
    den = sum(ws)
    return sum(expand(w / den) * o_ref[...].astype(F32) for w, o_ref in zip(ws, o_refs)).astype(BF16)


def _post_kernel(*refs, n_groups, seq_tiles):
    n_a = 2 * n_groups + 1 if n_groups else 8
    a_refs = refs[:n_a]
    (w_ref, h_ref, g_ref, wr_ref, br_ref, tri_ref,
     h1_ref, hn0_ref, hn1_ref, route_ref, route_t_ref, cnt_ref, carry_ref) = refs[n_a:n_a + 13]
    hn_refs = (hn0_ref, hn1_ref)
    if n_groups:
        a = _merge_groups(a_refs[:n_groups], a_refs[n_groups:2 * n_groups], a_refs[-1])
    else:
        a = _retention_tile(h_ref, *a_refs, refs[-1], pl.program_id(0) % seq_tiles == 0)
    @pl.when(pl.program_id(0) == 0)
    def _():
        carry_ref[...] = jnp.zeros_like(carry_ref)

    half = tri_ref.shape[0]
    n_pieces = h_ref.shape[0] // half
    D = h_ref.shape[1]
    lane = lax.broadcasted_iota(jnp.int32, (half, LANES), 1).astype(F32)
    ninf = -jnp.inf

    def first_argmax(vals):
        top = jnp.max(vals, axis=1, keepdims=True)
        where = jnp.min(jnp.where(vals == top, lane, float(LANES)), axis=1, keepdims=True)
        return top, where

    h1_ref[...] = h_ref[...] + jnp.dot(a, w_ref[...], preferred_element_type=F32)

    def route_half(rows):
        hn = _rms(h1_ref[rows, :]) * g_ref[...]
        for part, ref in enumerate(hn_refs):
            ref[rows, :] = _pack_part(hn, part)
        hi = hn.astype(BF16)
        lo = (hn - hi.astype(F32)).astype(BF16)
        logits = jnp.dot(jnp.concatenate([hi, hi, lo], axis=1), wr_ref[...],
                         preferred_element_type=F32) + br_ref[...]

        is_grp = lane < MOE_GROUPS
        lg = jnp.where(is_grp, logits, ninf)
        mg, grp = first_argmax(lg)
        p_grp = 1.0 / jnp.sum(jnp.where(is_grp, jnp.exp(lg - mg), 0.0), axis=1, keepdims=True)

        e_lane = lane - ROUTER_EXP_LANE0
        in_grp = (e_lane < N_EXPERTS) & (jnp.floor(e_lane * (1.0 / EXPERTS_PER_GROUP)) == grp)
        le = jnp.where(in_grp, logits, ninf)
        v1, i1 = first_argmax(le)
        le2 = jnp.where(lane == i1, ninf, le)
        v2, i2 = first_argmax(le2)
        e = jnp.exp(v2 - v1)
        hit1 = lane == (i1 - ROUTER_EXP_LANE0)
        hit2 = lane == (i2 - ROUTER_EXP_LANE0)
        onehot = jnp.where(hit1 | hit2, 1.0, 0.0)
        earlier = jnp.dot(tri_ref[...], onehot.astype(BF16), preferred_element_type=F32)
        return dict(e1=i1 - ROUTER_EXP_LANE0, e2=i2 - ROUTER_EXP_LANE0, g1=p_grp / (1.0 + e),
                    g2=p_grp * e / (1.0 + e), hit1=hit1, hit2=hit2, earlier=earlier,
                    count=jnp.sum(onehot, axis=0, keepdims=True))

    halves = [route_half(slice(j * half, (j + 1) * half)) for j in range(n_pieces)]
    carry = carry_ref[...]
    for j, r in enumerate(halves):
        before = carry + r["earlier"]
        r1 = jnp.sum(jnp.where(r["hit1"], before, 0.0), axis=1, keepdims=True)
        r2 = jnp.sum(jnp.where(r["hit2"], before, 0.0), axis=1, keepdims=True)
        carry = carry + r["count"]
        route = jnp.zeros((half, LANES), F32)
        for slot, val in ((ROUTE_E1, r["e1"]), (ROUTE_E2, r["e2"]), (ROUTE_G1, r["g1"]),
                          (ROUTE_G2, r["g2"]), (ROUTE_R1, r1), (ROUTE_R2, r2)):
            route = jnp.where(lane == slot, val, route)
        route_ref[j * half:(j + 1) * half, :] = route
        route_t_ref[:, j * half:(j + 1) * half] = route.T[:ROUTE_ROWS]
    carry_ref[...] = carry
    cnt_ref[...] = carry


def _post(mixer, w, h, gain, w_grp, b_grp, w_exp, b_exp):
    T, D = h.shape
    K = w.shape[0]
    row_block = lambda width: pl.BlockSpec((tm, width), lambda i: (i, 0))
    scratch = [pltpu.VMEM((1, LANES), F32)]
    if len(mixer) == 2:
        outs, lses = mixer
        n_groups, seq_tiles = len(outs), 1
        tm = min(TOKEN_TILE, T)
        ex = jnp.where(jnp.arange(LANES)[:, None] == (jnp.arange(K)[None, :] // DIL_HEAD_DIM), 1.0, 0.0).astype(BF16)
        a_args = (*outs, *lses, ex)
        a_specs = [row_block(K)] * n_groups + [row_block(LANES)] * n_groups + [_resident((LANES, K), lambda i: (0, 0))]
    else:
        mix_gain, w_in, S = mixer
        tm = min(TOKEN_TILE, S)
        n_groups, seq_tiles = 0, S // tm
        a_args, a_specs, state = _retention_operands(mix_gain, w_in, S, tm)
        scratch.append(state)
    n_r = MOE_GROUPS + N_EXPERTS
    wr = jnp.zeros((D, LANES), F32).at[:, :n_r].set(jnp.concatenate([w_grp, w_exp], axis=1))
    wr_hi = wr.astype(BF16)
    wr_lo = (wr - wr_hi.astype(F32)).astype(BF16)
    wr3 = jnp.concatenate([wr_hi, wr_lo, wr_hi], axis=0)
    br = jnp.zeros((1, LANES), F32).at[0, :n_r].set(jnp.concatenate([b_grp, b_exp]))
    piece = tm // ROUTE_PIECES
    tri = jnp.tril(jnp.ones((piece, piece), BF16), k=-1)
    return pl.pallas_call(
        functools.partial(_post_kernel, n_groups=n_groups, seq_tiles=seq_tiles),
        grid=(T // tm,),
        in_specs=a_specs + [
                  _resident((K, D), lambda i: (0, 0)),
                  pl.BlockSpec((tm, D), lambda i: (i, 0)),
                  _resident((1, D), lambda i: (0, 0)),
                  _resident((3 * D, LANES), lambda i: (0, 0)),
                  _resident((1, LANES), lambda i: (0, 0)),
                  _resident((piece, piece), lambda i: (0, 0))],
        out_specs=[pl.BlockSpec((tm, D), lambda i: (i, 0))]
                  + [pl.BlockSpec((tm, D // 2 // ROW_PARTS), lambda i: (i, 0))] * ROW_PARTS
                  + [pl.BlockSpec((tm, LANES), lambda i: (i, 0)),
                   pl.BlockSpec((ROUTE_ROWS, tm), lambda i: (0, i)),
                   pl.BlockSpec((1, LANES), lambda i: (0, 0))],
        out_shape=[jax.ShapeDtypeStruct((T, D), F32)]
                  + [jax.ShapeDtypeStruct((T, D // 2 // ROW_PARTS), jnp.uint32)] * ROW_PARTS
                  + [jax.ShapeDtypeStruct((T, LANES), F32),
                   jax.ShapeDtypeStruct((ROUTE_ROWS, T), F32),
                   jax.ShapeDtypeStruct((1, LANES), F32)],
        scratch_shapes=scratch,
        compiler_params=_params("arbitrary"),
        name="post_mixer",
    )(*a_args, w, h, gain.reshape(1, D), wr3, br, tri)


SC_WINDOW = 128


def _sc_mesh():
    return plsc.VectorSubcoreMesh(core_axis_name="core", subcore_axis_name="subcore")


def _sc_window_specs(W):
    rows = pl.BlockSpec((SC_WINDOW, W), lambda i: (i, 0))
    idx = pl.BlockSpec((1, SC_WINDOW), lambda i: (0, i))
    return rows, idx


def _sc_pipeline(body, n_rows, in_specs, out_specs):
    return pltpu.emit_pipeline(body, grid=(n_rows // SC_WINDOW,), in_specs=in_specs, out_specs=out_specs,
                               core_axis_name=("core", "subcore"), dimension_semantics=(pltpu.PARALLEL,))


def _dispatch(xs, dest, pad_idx, P):
    T, W = xs[0].shape
    n_pad = pad_idx.shape[1]
    n = len(xs)
    rows, idx = _sc_window_specs(W)
    zero_rows = pl.BlockSpec((SC_WINDOW, W), lambda i: (0, 0))
    out = jax.ShapeDtypeStruct((P, W), xs[0].dtype)

    @functools.partial(pl.kernel, out_type=(out,) * n, mesh=_sc_mesh(), scratch_types=[], name="dispatch")
    def scatter(*refs):
        x_hbm, (d0_hbm, d1_hbm, z_hbm, p_hbm), o_hbm = refs[:n], refs[n:n + 4], refs[n + 4:]
        for x, o in zip(x_hbm, o_hbm):
            def put_pair(x_vmem, i0_vmem, i1_vmem, o=o):
                pltpu.sync_copy(x_vmem, o.at[i0_vmem.at[0]])
                pltpu.sync_copy(x_vmem, o.at[i1_vmem.at[0]])

            def put(x_vmem, i_vmem, o=o):
                pltpu.sync_copy(x_vmem, o.at[i_vmem.at[0]])

            _sc_pipeline(put_pair, T, [rows, idx, idx], [])(x, d0_hbm, d1_hbm)
            _sc_pipeline(put, n_pad, [zero_rows, idx], [])(z_hbm, p_hbm)

    return scatter(*xs, dest[0:1], dest[1:2], jnp.zeros((SC_WINDOW, W), xs[0].dtype), pad_idx)


WEIGHT_LEADS = (3, 2, 1)
EXPERT_SLOTS = max(WEIGHT_LEADS) + 1


def _expert_kernel(blk_exp_ref, slot_ref, n_used_ref, *refs):
    x_refs = refs[:ROW_PARTS]
    w_refs = refs[ROW_PARTS:ROW_PARTS + 3]
    o_refs = refs[ROW_PARTS + 3:2 * ROW_PARTS + 3]
    w_slots = refs[2 * ROW_PARTS + 3:]
    lead = max(WEIGHT_LEADS)
    n_used = n_used_ref[0]
    i = pl.program_id(0) - lead
    D = w_slots[0].shape[1]

    for w_ref, w_s, ahead in zip(w_refs, w_slots, WEIGHT_LEADS):
        j = i + ahead
        jc = jnp.clip(j, 0, n_used - 1)
        arrived = (j >= 0) & (j < n_used) & ((j == 0) | (blk_exp_ref[jc] != blk_exp_ref[jnp.maximum(jc - 1, 0)]))

        @pl.when(arrived)
        def _(w_ref=w_ref, w_s=w_s, jc=jc):
            w_s[slot_ref[jc]] = w_ref[0, 0].astype(BF16)

    @pl.when((i >= 0) & (i < n_used))
    def _():
        slot = slot_ref[jnp.clip(i, 0, n_used - 1)]
        wg_s, wu_s, wd_s = w_slots
        pieces = []
        for part, x_ref in enumerate(x_refs):
            for cols, val in zip(_part_cols(D, part), _unpack_bf16_pair(x_ref[...])):
                pieces.append((cols, val.astype(BF16)))

        def up(w_s):
            return sum(jnp.dot(val, w_s[slot, cols, :], preferred_element_type=F32) for cols, val in pieces)

        g = up(wg_s)
        hid = (g * jax.nn.sigmoid(g) * up(wu_s)).astype(BF16)
        y = jnp.dot(hid, wd_s[slot], preferred_element_type=F32)
        for part, o_ref in enumerate(o_refs):
            o_ref[...] = _pack_part(y, part)

    @pl.when(i >= n_used)
    def _():
        for o_ref in o_refs:
            o_ref[...] = jnp.zeros_like(o_ref)


def _experts(xs, blk_exp, n_used, layer, w_gate, w_up, w_down):
    P, W = xs[0].shape
    D = 2 * W * ROW_PARTS
    FF = w_gate.shape[3]
    nblk = P // MOE_BLOCK
    lead = max(WEIGHT_LEADS)
    changes = jnp.concatenate([jnp.zeros((1,), jnp.int32), (blk_exp[1:] != blk_exp[:-1]).astype(jnp.int32)])
    slot = jnp.cumsum(changes) % EXPERT_SLOTS

    def x_map(g, be, sl, nu):
        return (jnp.clip(g - lead, 0, nu[0] - 1), 0)

    def o_map(g, be, sl, nu):
        return (jnp.maximum(g - lead, 0), 0)

    def w_map(ahead):
        return lambda g, be, sl, nu: (layer, be[jnp.clip(g - lead + ahead, 0, nu[0] - 1)], 0, 0)

    wg_spec, wu_spec, wd_spec = (pl.BlockSpec((1, 1) + shape, w_map(ahead))
                                 for shape, ahead in zip(((D, FF), (D, FF), (FF, D)), WEIGHT_LEADS))
    return pl.pallas_call(
        _expert_kernel,
        grid_spec=pltpu.PrefetchScalarGridSpec(
            num_scalar_prefetch=3,
            grid=(nblk + lead,),
            in_specs=[pl.BlockSpec((MOE_BLOCK, W), x_map)] * ROW_PARTS + [wg_spec, wu_spec, wd_spec],
            out_specs=[pl.BlockSpec((MOE_BLOCK, W), o_map)] * ROW_PARTS,
            scratch_shapes=[pltpu.VMEM((EXPERT_SLOTS, D, FF), BF16), pltpu.VMEM((EXPERT_SLOTS, D, FF), BF16),
                            pltpu.VMEM((EXPERT_SLOTS, FF, D), BF16)]),
        out_shape=[jax.ShapeDtypeStruct((P, W), jnp.uint32)] * ROW_PARTS,
        compiler_params=_params("arbitrary"),
        name="experts",
    )(blk_exp, slot, n_used, *xs, w_gate, w_up, w_down)


def _gather_pairs(ys, dest):
    W = ys[0].shape[1]
    T = dest.shape[1]
    n = len(ys)
    rows, idx = _sc_window_specs(W)
    out = jax.ShapeDtypeStruct((T, W), ys[0].dtype)

    @functools.partial(pl.kernel, out_type=(out,) * (2 * n), mesh=_sc_mesh(), scratch_types=[],
                       name="gather_pairs")
    def gather(*refs):
        y_hbm, d_hbm, o_hbm = refs[:n], refs[n:n + 2], refs[n + 2:]
        for slot, d in enumerate(d_hbm):
            for y, o in zip(y_hbm, o_hbm[slot * n:(slot + 1) * n]):
                def get(i_vmem, o_vmem, y=y):
                    pltpu.sync_copy(y.at[i_vmem.at[0]], o_vmem)

                _sc_pipeline(get, T, [idx], [rows])(d, o)

    return gather(*ys, dest[0:1], dest[1:2])


def _combined_rows(h_ref, route_ref, y_refs):
    D = h_ref.shape[1]
    route = route_ref[...]
    gates = (route[:, ROUTE_G1:ROUTE_G1 + 1], route[:, ROUTE_G2:ROUTE_G2 + 1])
    pieces = {}
    for part in range(ROW_PARTS):
        slots = [_unpack_bf16_pair(y_refs[slot * ROW_PARTS + part][...]) for slot in range(2)]
        for half, cols in enumerate(_part_cols(D, part)):
            pieces[cols.start] = h_ref[:, cols] + (gates[0] * slots[0][half] + gates[1] * slots[1][half])
    return jnp.concatenate([pieces[c] for c in sorted(pieces)], axis=1)


def _combine_kernel(h_ref, route_ref, *refs):
    refs[-1][...] = _combined_rows(h_ref, route_ref, refs[:-1])


def _combine(h, route, pairs):
    T, D = h.shape
    W = pairs[0].shape[1]
    tm = min(ROW_TILE, T)
    return pl.pallas_call(
        _combine_kernel,
        grid=(T // tm,),
        in_specs=[pl.BlockSpec((tm, D), lambda i: (i, 0)), pl.BlockSpec((tm, LANES), lambda i: (i, 0))]
                 + [pl.BlockSpec((tm, W), lambda i: (i, 0))] * len(pairs),
        out_specs=pl.BlockSpec((tm, D), lambda i: (i, 0)),
        out_shape=jax.ShapeDtypeStruct((T, D), F32),
        compiler_params=_params("parallel"),
        name="combine",
    )(h, route, *pairs)


def _moe(h1, hn, route, route_t, counts, layer, w_gate, w_up, w_down, combine):
    T, D = h1.shape
    A = 2 * T
    nblk = -(-A // MOE_BLOCK) + N_EXPERTS
    P = nblk * MOE_BLOCK
    eid = route_t[ROUTE_E1:ROUTE_E2 + 1].astype(jnp.int32)
    rank = route_t[ROUTE_R1:ROUTE_R2 + 1].astype(jnp.int32)
    cnt = counts[0, :N_EXPERTS].astype(jnp.int32)
    padded = (cnt + MOE_BLOCK - 1) // MOE_BLOCK * MOE_BLOCK
    pends = jnp.cumsum(padded)
    pstarts = pends - padded
    experts = jnp.arange(N_EXPERTS, dtype=jnp.int32)
    start_of = jnp.sum(jnp.where(eid[:, None, :] == experts[None, :, None], pstarts[None, :, None], 0), axis=1)
    dest = start_of + rank
    blk_start = jnp.arange(nblk, dtype=jnp.int32) * MOE_BLOCK
    blk_exp = jnp.minimum(jnp.sum((pends[None, :] <= blk_start[:, None]).astype(jnp.int32), axis=1),
                          N_EXPERTS - 1)
    n_used = pends[-1:] // MOE_BLOCK

    gap_start = jnp.concatenate([pstarts + cnt, pends[-1:]])
    gap_len = jnp.concatenate([padded - cnt, P - pends[-1:]])
    gap_end = jnp.cumsum(gap_len)
    j = jnp.arange(P - A, dtype=jnp.int32)
    gap_of = jnp.sum((gap_end[None, :] <= j[:, None]).astype(jnp.int32), axis=1)
    sel = gap_of[:, None] == jnp.arange(N_EXPERTS + 1, dtype=jnp.int32)[None, :]
    pad_idx = jnp.sum(jnp.where(sel, (gap_start - (gap_end - gap_len))[None, :] + j[:, None], 0), axis=1)

    xs = _dispatch(hn, dest, pad_idx.reshape(1, P - A), P)
    ys = _experts(xs, blk_exp, n_used, layer, w_gate, w_up, w_down)
    pairs = _gather_pairs(ys, dest)
    return _combine(h1, route, pairs) if combine else (h1, route, pairs)


def _qkv_kernel(*refs, rate, n, n_chunks, n_pending):
    n_x = 2 + n_pending if n_pending else 1
    x_refs = refs[:n_x]
    gq_ref, gkv_ref, wq_ref, wk_ref, wvt_ref, qn_ref, kn_ref, seg_ref, q_ref, k_ref, vt_ref = refs[n_x:n_x + 11]
    xs_ref = refs[-1]
    seg = seg_ref[...]
    width = seg.shape[0]
    res_per_chunk = q_ref.shape[1]
    chunk = pl.program_id(2)

    def head_norm(t, gain):
        cols = []
        for j in range(t.shape[1] // width):
            tj = t[:, j * width:(j + 1) * width]
            ms = jnp.dot((tj * tj).astype(BF16), seg, preferred_element_type=F32)
            cols.append(tj * lax.rsqrt(ms + EPS))
        return jnp.concatenate(cols, axis=1) * gain

    def project(x):
        y = _rms(x)
        xq = (y * gq_ref[...]).astype(BF16)
        xkv = (y * gkv_ref[...]).astype(BF16)
        q = head_norm(jnp.dot(xq, wq_ref[...], preferred_element_type=F32), qn_ref[...]).astype(q_ref.dtype)
        k = head_norm(jnp.dot(xkv, wk_ref[...], preferred_element_type=F32), kn_ref[...]).astype(k_ref.dtype)
        vt = lax.dot_general(wvt_ref[...], xkv, NT_DIMS, preferred_element_type=F32).astype(vt_ref.dtype)
        for j in range(res_per_chunk):
            q_ref[0, j] = q[j * n:(j + 1) * n]
            k_ref[0, j] = k[j * n:(j + 1) * n]
            vt_ref[0, j] = vt[:, j * n:(j + 1) * n]

    if n_pending:
        x = _combined_rows(x_refs[0], x_refs[1], x_refs[2:])
        refs[-2][...] = x
        project(x)
        return
    x_ref, = x_refs
    if rate == 1:
        project(x_ref[...])
        return

    @pl.when(chunk == 0)
    def _():
        for j in range(xs_ref.shape[0]):
            xs_ref[j] = x_ref[:, j * LANES:(j + 1) * LANES]

    for ch in range(n_chunks):
        @pl.when(chunk == ch)
        def _(ch=ch):
            residues = range(ch * res_per_chunk, (ch + 1) * res_per_chunk)
            project(jnp.concatenate(
                [jnp.concatenate([xs_ref[j, pl.ds(c, n, stride=rate), :] for j in range(xs_ref.shape[0])], axis=1)
                 for c in residues], axis=0))


def _qkv(h, B, S, rate, gq, gkv, wq, wk, wv, qn, kn):
    pending = h if isinstance(h, tuple) else None
    D = (pending[0] if pending else h).shape[1]
    L = S // rate
    hd = DIL_HEAD_DIM
    n = max(SUB_BLOCK, TOKEN_TILE // rate)
    tm = n * rate
    res_per_chunk = max(1, TOKEN_TILE // n)
    n_chunks = rate // res_per_chunk
    width = 2 * LANES
    ii = jnp.arange(width)
    seg = jnp.where((ii[:, None] // hd) == (ii[None, :] // hd), 1.0 / hd, 0.0).astype(BF16)
    row = lambda g: jnp.tile(g, D // hd).reshape(1, D)
    const = lambda shape: _resident(shape, lambda b, i, c: (0,) * len(shape))
    qk_spec = pl.BlockSpec((1, res_per_chunk, n, D), lambda b, i, c: (b, c, i, 0))
    qk_shape = jax.ShapeDtypeStruct((B, rate, L, D), BF16)
    rows = lambda width: pl.BlockSpec((tm, width), lambda b, i, c: (b * (S // tm) + i, 0))
    if pending:
        assert rate == 1
        h1, route, pairs = pending
        x_args = (h1, route, *pairs)
        x_specs = [rows(D), rows(LANES)] + [rows(pairs[0].shape[1])] * len(pairs)
        extra_specs, extra_shapes = [rows(D)], [jax.ShapeDtypeStruct((B * S, D), F32)]
    else:
        x_args, x_specs, extra_specs, extra_shapes = (h,), [rows(D)], [], []
    return pl.pallas_call(
        functools.partial(_qkv_kernel, rate=rate, n=n, n_chunks=n_chunks,
                          n_pending=len(pending[2]) if pending else 0),
        grid=(B, S // tm, n_chunks),
        in_specs=x_specs + [
                  const((1, D)), const((1, D)), const((D, D)), const((D, D)), const((D, D)),
                  const((1, D)), const((1, D)), const((width, width))],
        out_specs=[qk_spec, qk_spec,
                   pl.BlockSpec((1, res_per_chunk, D, n), lambda b, i, c: (b, c, 0, i))] + extra_specs,
        out_shape=[qk_shape, qk_shape, jax.ShapeDtypeStruct((B, rate, D, L), BF16)] + extra_shapes,
        scratch_shapes=[pltpu.VMEM((D // LANES, tm if rate > 1 else 8, LANES), F32)],
        compiler_params=_params("parallel", "parallel", "arbitrary"),
        name="qkv_rate%d" % rate,
    )(*x_args, gq.reshape(1, D), gkv.reshape(1, D), wq, wk, wv.T, row(qn) * (hd ** -0.5 * LOG2E), row(kn), seg)


def _attn_kernel(*refs, rate, with_prev):
    if not with_prev:
        q_ref, kc_ref, vc_ref, edge_ref, o_ref, lse_ref, os_ref = refs
    elif len(refs) == 10:
        q_ref, kc_ref, kp_ref, vc_ref, vp_ref, edge_ref, bias_ref, o_ref, lse_ref, os_ref = refs
    else:
        q_ref, kc_ref, kp_ref, vc_ref, vp_ref, edge_ref, o_ref, lse_ref, os_ref = refs
    Bk = SUB_BLOCK
    hd = DIL_HEAD_DIM
    n_pairs = DIL_HEADS // 2
    lane = lax.broadcasted_iota(jnp.int32, (Bk, LANES), 1)

    def block(load_q, load_k, load_vt, load_bias, store):
        out_t, lse_t = [], []
        for p in range(n_pairs):
            qp = load_q(p)
            zero = jnp.zeros_like(qp)
            q2 = jnp.concatenate([jnp.where(lane < hd, qp, zero), jnp.where(lane >= hd, qp, zero)], axis=0)
            s = lax.dot_general(load_k(p), q2, NT_DIMS, preferred_element_type=F32) + load_bias(p)
            m = jnp.max(s, axis=0, keepdims=True)
            pr = jnp.exp2(s - m)
            l = jnp.sum(pr, axis=0, keepdims=True)
            pb = pr.astype(BF16)
            vt = load_vt(p)
            out_t.append(jnp.dot(vt[:hd], pb[:, :Bk], preferred_element_type=F32) / l[:, :Bk])
            out_t.append(jnp.dot(vt[hd:], pb[:, Bk:], preferred_element_type=F32) / l[:, Bk:])
            lse = (m + jnp.log2(l)) * LN2
            lse_t += [lse[:, :Bk], lse[:, Bk:]]
        lse_t.append(jnp.zeros((LANES - DIL_HEADS, Bk), F32))
        store(jnp.concatenate(out_t, axis=0).T, jnp.concatenate(lse_t, axis=0).T)

    def pair_cols(p):
        return slice(p * LANES, (p + 1) * LANES)

    n_res, n_blocks = kc_ref.shape[1], kc_ref.shape[2] // Bk
    for c in range(n_res):
        for j in range(n_blocks):
            rows = slice(j * Bk, (j + 1) * Bk)
            if not with_prev:
                load_k = lambda p, c=c: kc_ref[0, c, :, pair_cols(p)]
                load_vt = lambda p, c=c: vc_ref[0, c, pair_cols(p), :]
                load_bias = lambda p: edge_ref[0, p]
            elif j == 0:
                load_k = lambda p, c=c: jnp.concatenate([kp_ref[0, c, :, pair_cols(p)],
                                                         kc_ref[0, c, :Bk, pair_cols(p)]], axis=0)
                load_vt = lambda p, c=c: jnp.concatenate([vp_ref[0, c, pair_cols(p), :],
                                                          vc_ref[0, c, pair_cols(p), :Bk]], axis=1)
                load_bias = lambda p: edge_ref[0, p]
            else:
                krows = slice((j - 1) * Bk, (j + 1) * Bk)
                load_k = lambda p, c=c, krows=krows: kc_ref[0, c, krows, pair_cols(p)]
                load_vt = lambda p, c=c, krows=krows: vc_ref[0, c, pair_cols(p), krows]
                load_bias = lambda p: bias_ref[p]

            if rate == 1:
                def store(o, lse, rows=rows):
                    o_ref[rows, :] = o.astype(o_ref.dtype)
                    lse_ref[rows, :] = lse
            else:
                def store(o, lse, out_rows=pl.ds(c + j * Bk * rate, Bk, stride=rate)):
                    for g in range(os_ref.shape[0]):
                        os_ref[g, out_rows, :] = o[:, pair_cols(g)]
                    lse_ref[out_rows, :] = lse

            block(lambda p, c=c, rows=rows: q_ref[0, c, rows, pair_cols(p)], load_k, load_vt, load_bias, store)
    if rate > 1:
        for g in range(os_ref.shape[0]):
            o_ref[:, pair_cols(g)] = os_ref[g].astype(o_ref.dtype)


def _t5_bucket(n):
    max_exact = NUM_BUCKETS // 2
    nf = jnp.maximum(n, max_exact).astype(F32)
    large = max_exact + (jnp.log(nf / max_exact) / math.log(MAX_DISTANCE / max_exact)
                         * (NUM_BUCKETS - max_exact)).astype(jnp.int32)
    large = jnp.minimum(large, NUM_BUCKETS - 1)
    return jnp.where(n < max_exact, n, large)


def _group_attention(q, k, vt, bias_table, rate, n_steps):
    B, _, L, D = q.shape
    S = L * rate
    Bk = SUB_BLOCK
    with_prev = L > Bk
    nk = 2 * Bk if with_prev else Bk
    n = max(Bk, ATTN_TILE // rate)
    tm = n * rate
    nt = S // tm
    n_pairs = DIL_HEADS // 2
    ql = jnp.arange(Bk, dtype=jnp.int32)[:, None]
    kl = jnp.arange(2 * Bk, dtype=jnp.int32)[None, :]
    steps = ql + Bk - kl
    bucket = _t5_bucket(jnp.maximum(steps, 0) * rate)
    buckets = jnp.arange(NUM_BUCKETS, dtype=jnp.int32)
    bias = jnp.sum(jnp.where(bucket[None, :, :, None] == buckets[:, None, None, None],
                             bias_table.astype(F32)[:, None, None, :], 0.0), axis=0)
    band = ((steps >= 0) & (steps <= n_steps))[:, :, None]
    first = (kl < Bk)[:, :, None]

    def layout(t):
        t = t[:, 2 * Bk - nk:, :].transpose(1, 2, 0)
        return t.reshape(nk, n_pairs, 2 * Bk).transpose(1, 0, 2)

    bias_in = layout(jnp.where(band, bias * LOG2E, NEG_INF))
    bias_first = layout(jnp.where(band & ~first, bias * LOG2E, NEG_INF))
    edge = jnp.stack([bias_first, bias_in])

    cur_qk = pl.BlockSpec((1, rate, n, D), lambda b, i: (b, 0, i, 0))
    cur_vt = pl.BlockSpec((1, rate, D, n), lambda b, i: (b, 0, 0, i))
    per_n = n // Bk
    prev_qk = pl.BlockSpec((1, rate, Bk, D), lambda b, i: (b, 0, jnp.maximum(i * per_n - 1, 0), 0))
    prev_vt = pl.BlockSpec((1, rate, D, Bk), lambda b, i: (b, 0, 0, jnp.maximum(i * per_n - 1, 0)))
    edge_spec = pl.BlockSpec((1, n_pairs, nk, 2 * Bk), lambda b, i: (jnp.minimum(i, 1), 0, 0, 0))
    if with_prev:
        in_specs = [cur_qk, cur_qk, prev_qk, cur_vt, prev_vt, edge_spec]
        args = (q, k, k, vt, vt, edge)
    else:
        in_specs = [cur_qk, cur_qk, cur_vt, edge_spec]
        args = (q, k, vt, edge)
    if with_prev and n > Bk:
        in_specs.append(_resident((n_pairs, nk, 2 * Bk), lambda b, i: (0, 0, 0)))
        args += (bias_in,)
    return pl.pallas_call(
        functools.partial(_attn_kernel, rate=rate, with_prev=with_prev),
        grid=(B, nt),
        in_specs=in_specs,
        out_specs=[pl.BlockSpec((tm, D), lambda b, i: (b * nt + i, 0)),
                   pl.BlockSpec((tm, LANES), lambda b, i: (b * nt + i, 0))],
        out_shape=[jax.ShapeDtypeStruct((B * S, D), BF16),
                   jax.ShapeDtypeStruct((B * S, LANES), F32)],
        scratch_shapes=[pltpu.VMEM((D // LANES, tm if rate > 1 else 8, LANES), F32)],
        compiler_params=_params("parallel", "parallel"),
        name="attn_rate%d" % rate,
    )(*args)


def kernel(x, ret_w_in, ret_w_out, kv_norm, w_kv, k_norm, dil_wq, q_norm, dil_wo, rel_bias,
           mixer_norm, ffn_norm, router_grp, router_grp_b, router_exp, router_exp_b,
           exp_gate, exp_up, exp_down):
    B, S, D = x.shape
    h = x.reshape(B * S, D)

    def moe_layer(layer, mixer, w_out, h, combine):
        h1, *hn, route, route_t, counts = _post(
            mixer, w_out.astype(BF16), h, ffn_norm[layer], router_grp[layer], router_grp_b[layer],
            router_exp[layer], router_exp_b[layer])
        return _moe(h1, hn, route, route_t, counts, layer, exp_gate, exp_up, exp_down, combine)

    h = moe_layer(0, (mixer_norm[0], ret_w_in[0].astype(BF16), S), ret_w_out[0], h, combine=False)

    G = len(DIL_RATES)
    gd = DIL_HEADS * DIL_HEAD_DIM
    outs, lses = [], []
    for g in range(G):
        cq = slice(g * gd, (g + 1) * gd)
        cv = slice(G * gd + g * gd, G * gd + (g + 1) * gd)
        q, k, vt, *combined = _qkv(h, B, S, DIL_RATES[g], mixer_norm[1], kv_norm,
                                   dil_wq[0][:, cq].astype(BF16), w_kv[:, cq].astype(BF16),
                                   w_kv[:, cv].astype(BF16), q_norm[0][g], k_norm[g])
        if combined:
            h, = combined
        o, lse = _group_attention(q, k, vt, rel_bias[:, g * DIL_HEADS:(g + 1) * DIL_HEADS],
                                  DIL_RATES[g], DIL_WINDOWS[g] // DIL_RATES[g])
        outs.append(o)
        lses.append(lse)
    h = moe_layer(1, (tuple(outs), tuple(lses)), dil_wo[0], h, combine=True)
    return h.reshape(B, S, D)
```

```python
import functools
import math

import jax
import jax.numpy as jnp
from jax import lax
from jax.experimental import pallas as pl
from jax.experimental.pallas import tpu as pltpu
from jax.experimental.pallas import tpu_sc as plsc

F32 = jnp.float32
BF16 = jnp.bfloat16

EPS = 1e-6
NEG_INF = -1e30

RET_HEADS = 4
RET_CHUNK = 256
ROPE_BASE = 10000.0

DIL_WINDOWS = (128, 512, 2048)
DIL_RATES = (1, 4, 16)
DIL_HEADS = 16
DIL_HEAD_DIM = 64
SUB_BLOCK = 128
NUM_BUCKETS = 32
MAX_DISTANCE = 2048

MOE_GROUPS = 4
EXPERTS_PER_GROUP = 8
N_EXPERTS = MOE_GROUPS * EXPERTS_PER_GROUP
MOE_BLOCK = 512

LANES = 128
ROW_TILE = 1024
TOKEN_TILE = 512
ATTN_TILE = 1024
VMEM_LIMIT = 56 * 1024 * 1024

NT_DIMS = (((1,), (1,)), ((), ()))
LOG2E = math.log2(math.e)
LN2 = math.log(2.0)


def _params(*sem):
    return pltpu.CompilerParams(dimension_semantics=sem, vmem_limit_bytes=VMEM_LIMIT)


def _resident(shape, index_map):
    return pl.BlockSpec(shape, index_map, pipeline_mode=pl.Buffered(1))


def _rms(x):
    return x * lax.rsqrt(jnp.mean(x * x, axis=-1, keepdims=True) + EPS)


def _retention_tile(x_ref, g_ref, w_ref, cos_ref, sin_ref, din_ref, xi_ref, zeta_ref, cd_ref, state_ref,
                    first_tile):
    C = RET_CHUNK
    H, dk, dv = state_ref.shape
    half = dk // 2
    ts = x_ref.shape[0]
    k_scale = dk ** -0.5

    @pl.when(first_tile)
    def _():
        state_ref[...] = jnp.zeros_like(state_ref)

    xn = (_rms(x_ref[...]) * g_ref[...]).astype(BF16)
    cos, sin = cos_ref[...], sin_ref[...]

    def proj(c0, width):
        return jnp.dot(xn, w_ref[:, c0:c0 + width], preferred_element_type=F32)

    def rot(t):
        t1, t2 = t[:, :half], t[:, half:]
        return jnp.concatenate([t1 * cos - t2 * sin, t1 * sin + t2 * cos], axis=1)

    heads = []
    for hh in range(H):
        q = rot(proj(hh * dk, dk)).astype(BF16)
        k = rot(proj(H * dk + hh * dk, dk)) * k_scale
        kb = k.astype(BF16)
        v = proj(2 * H * dk + hh * dv, dv).astype(BF16)
        gate = proj(2 * H * dk + H * dv + hh * dv, dv)
        gate = gate * jax.nn.sigmoid(gate)
        chunks = []
        for c in range(ts // C):
            rows = slice(c * C, (c + 1) * C)
            s = lax.dot_general(q[rows], kb[rows], NT_DIMS, preferred_element_type=F32) * din_ref[hh]
            inner = jnp.dot(s.astype(BF16), v[rows], preferred_element_type=F32)
            state = state_ref[hh]
            cross = jnp.dot(q[rows], state.astype(BF16), preferred_element_type=F32) * xi_ref[hh]
            kz_t = (k[rows] * zeta_ref[hh]).T.astype(BF16)
            state_ref[hh] = state * cd_ref[hh] + jnp.dot(kz_t, v[rows], preferred_element_type=F32)
            chunks.append((gate[rows] * _rms(inner + cross)).astype(BF16))
        heads.append(jnp.concatenate(chunks, axis=0))
    return jnp.concatenate(heads, axis=1)


def _retention_operands(gain, w, S, ts):
    H, C = RET_HEADS, RET_CHUNK
    D = w.shape[0]
    dk = D // H
    half = dk // 2
    nt = S // ts
    pos = jnp.arange(S, dtype=F32)
    inv = 1.0 / (ROPE_BASE ** jnp.linspace(0.0, 1.0, half, dtype=F32))
    ang = pos[:, None] * inv[None, :]
    log_g = jnp.log(1.0 - 2.0 ** (-5.0 - jnp.arange(H, dtype=F32)))
    idx = jnp.arange(C, dtype=F32)
    diff = idx[:, None] - idx[None, :]
    d_in = jnp.where(diff >= 0, jnp.exp(log_g[:, None, None] * jnp.maximum(diff, 0.0)), 0.0)
    xi = jnp.exp(log_g[:, None] * (idx + 1.0))[:, :, None]
    zeta = jnp.exp(log_g[:, None] * (C - 1.0 - idx))[:, :, None]
    chunk_decay = jnp.exp(log_g * C)[:, None, None]
    table = pl.BlockSpec((ts, half), lambda i: (i % nt, 0))
    per_head = lambda shape: _resident((H,) + shape, lambda i: (0, 0, 0))
    args = (gain.reshape(1, D), w, jnp.cos(ang), jnp.sin(ang), d_in, xi, zeta, chunk_decay)
    specs = [_resident((1, D), lambda i: (0, 0)), _resident(w.shape, lambda i: (0, 0)), table, table,
             per_head((C, C)), per_head((C, 1)), per_head((C, 1)), per_head((1, 1))]
    state = pltpu.VMEM((H, dk, 2 * dk), F32)
    return args, specs, state


ROUTE_E1, ROUTE_E2, ROUTE_G1, ROUTE_G2, ROUTE_R1, ROUTE_R2 = range(6)
ROUTE_ROWS = 8
ROUTE_PIECES = 2
ROUTER_EXP_LANE0 = MOE_GROUPS
HI16 = 0xFFFF0000


def _pack_bf16_pair(a, b):
    ua = lax.bitcast_convert_type(a.astype(BF16).astype(F32), jnp.uint32)
    ub = lax.bitcast_convert_type(b.astype(BF16).astype(F32), jnp.uint32)
    return ua | (ub >> 16)


def _unpack_bf16_pair(w):
    a = lax.bitcast_convert_type(w & jnp.uint32(HI16), F32)
    b = lax.bitcast_convert_type(w << 16, F32)
    return a, b


ROW_PARTS = 2


def _part_cols(D, part):
    w = D // 2 // ROW_PARTS
    return slice(part * w, (part + 1) * w), slice(D // 2 + part * w, D // 2 + (part + 1) * w)


def _pack_part(x, part):
    hi, lo = _part_cols(x.shape[1], part)
    return _pack_bf16_pair(x[:, hi], x[:, lo])


def _merge_groups(o_refs, l_refs, ex_ref):
    ex = ex_ref[...]

    def expand(w):
        return jnp.dot(w.astype(BF16), ex, preferred_element_type=F32)

    lses = [l_ref[...] for l_ref in l_refs]
    top = functools.reduce(jnp.maximum, lses)
    ws = [jnp.exp(l - top) for l in lses]
    den = sum(ws)
    return sum(expand(w / den) * o_ref[...].astype(F32) for w, o_ref in zip(ws, o_refs)).astype(BF16)


def _post_kernel(*refs, n_groups, seq_tiles):
    n_a = 2 * n_groups + 1 if n_groups else 8
    a_refs = refs[:n_a]
    (w_ref, h_ref, g_ref, wr_ref, br_ref, tri_ref,
     h1_ref, hn0_ref, hn1_ref, route_ref, route_t_ref, cnt_ref, carry_ref) = refs[n_a:n_a + 13]
    hn_refs = (hn0_ref, hn1_ref)
    if n_groups:
        a = _merge_groups(a_refs[:n_groups], a_refs[n_groups:2 * n_groups], a_refs[-1])
    else:
        a = _retention_tile(h_ref, *a_refs, refs[-1], pl.program_id(0) % seq_tiles == 0)
    @pl.when(pl.program_id(0) == 0)
    def _():
        carry_ref[...] = jnp.zeros_like(carry_ref)

    half = tri_ref.shape[0]
    n_pieces = h_ref.shape[0] // half
    D = h_ref.shape[1]
    lane = lax.broadcasted_iota(jnp.int32, (half, LANES), 1).astype(F32)
    ninf = -jnp.inf

    def first_argmax(vals):
        top = jnp.max(vals, axis=1, keepdims=True)
        where = jnp.min(jnp.where(vals == top, lane, float(LANES)), axis=1, keepdims=True)
        return top, where

    h1_ref[...] = h_ref[...] + jnp.dot(a, w_ref[...], preferred_element_type=F32)

    def route_half(rows):
        hn = _rms(h1_ref[rows, :]) * g_ref[...]
        for part, ref in enumerate(hn_refs):
            ref[rows, :] = _pack_part(hn, part)
        hi = hn.astype(BF16)
        lo = (hn - hi.astype(F32)).astype(BF16)
        both = jnp.dot(hi, wr_ref[...], preferred_element_type=F32)
        logits = (both[:, :LANES] + both[:, LANES:]
                  + jnp.dot(lo, wr_ref[:, :LANES], preferred_element_type=F32) + br_ref[...])

        is_grp = lane < MOE_GROUPS
        lg = jnp.where(is_grp, logits, ninf)
        mg, grp = first_argmax(lg)
        p_grp = 1.0 / jnp.sum(jnp.where(is_grp, jnp.exp(lg - mg), 0.0), axis=1, keepdims=True)

        e_lane = lane - ROUTER_EXP_LANE0
        in_grp = (e_lane < N_EXPERTS) & (jnp.floor(e_lane * (1.0 / EXPERTS_PER_GROUP)) == grp)
        le = jnp.where(in_grp, logits, ninf)
        v1, i1 = first_argmax(le)
        le2 = jnp.where(lane == i1, ninf, le)
        v2, i2 = first_argmax(le2)
        e = jnp.exp(v2 - v1)
        hit1 = lane == (i1 - ROUTER_EXP_LANE0)
        hit2 = lane == (i2 - ROUTER_EXP_LANE0)
        onehot = jnp.where(hit1 | hit2, 1.0, 0.0)
        earlier = jnp.dot(tri_ref[...], onehot.astype(BF16), preferred_element_type=F32)
        return dict(e1=i1 - ROUTER_EXP_LANE0, e2=i2 - ROUTER_EXP_LANE0, g1=p_grp / (1.0 + e),
                    g2=p_grp * e / (1.0 + e), hit1=hit1, hit2=hit2, earlier=earlier,
                    count=jnp.sum(onehot, axis=0, keepdims=True))

    halves = [route_half(slice(j * half, (j + 1) * half)) for j in range(n_pieces)]
    carry = carry_ref[...]
    for j, r in enumerate(halves):
        before = carry + r["earlier"]
        r1 = jnp.sum(jnp.where(r["hit1"], before, 0.0), axis=1, keepdims=True)
        r2 = jnp.sum(jnp.where(r["hit2"], before, 0.0), axis=1, keepdims=True)
        carry = carry + r["count"]
        route = jnp.zeros((half, LANES), F32)
        for slot, val in ((ROUTE_E1, r["e1"]), (ROUTE_E2, r["e2"]), (ROUTE_G1, r["g1"]),
                          (ROUTE_G2, r["g2"]), (ROUTE_R1, r1), (ROUTE_R2, r2)):
            route = jnp.where(lane == slot, val, route)
        route_ref[j * half:(j + 1) * half, :] = route
        route_t_ref[:, j * half:(j + 1) * half] = route.T[:ROUTE_ROWS]
    carry_ref[...] = carry
    cnt_ref[...] = carry


def _post(mixer, w, h, gain, w_grp, b_grp, w_exp, b_exp):
    T, D = h.shape
    K = w.shape[0]
    row_block = lambda width: pl.BlockSpec((tm, width), lambda i: (i, 0))
    scratch = [pltpu.VMEM((1, LANES), F32)]
    if len(mixer) == 2:
        outs, lses = mixer
        n_groups, seq_tiles = len(outs), 1
        tm = min(TOKEN_TILE, T)
        ex = jnp.where(jnp.arange(LANES)[:, None] == (jnp.arange(K)[None, :] // DIL_HEAD_DIM), 1.0, 0.0).astype(BF16)
        a_args = (*outs, *lses, ex)
        a_specs = [row_block(K)] * n_groups + [row_block(LANES)] * n_groups + [_resident((LANES, K), lambda i: (0, 0))]
    else:
        mix_gain, w_in, S = mixer
        tm = min(TOKEN_TILE, S)
        n_groups, seq_tiles = 0, S // tm
        a_args, a_specs, state = _retention_operands(mix_gain, w_in, S, tm)
        scratch.append(state)
    n_r = MOE_GROUPS + N_EXPERTS
    wr = jnp.zeros((D, LANES), F32).at[:, :n_r].set(jnp.concatenate([w_grp, w_exp], axis=1))
    wr_hi = wr.astype(BF16)
    wr_lo = (wr - wr_hi.astype(F32)).astype(BF16)
    wr2 = jnp.concatenate([wr_hi, wr_lo], axis=1)
    br = jnp.zeros((1, LANES), F32).at[0, :n_r].set(jnp.concatenate([b_grp, b_exp]))
    piece = tm // ROUTE_PIECES
    tri = jnp.tril(jnp.ones((piece, piece), BF16), k=-1)
    return pl.pallas_call(
        functools.partial(_post_kernel, n_groups=n_groups, seq_tiles=seq_tiles),
        grid=(T // tm,),
        in_specs=a_specs + [
                  _resident((K, D), lambda i: (0, 0)),
                  pl.BlockSpec((tm, D), lambda i: (i, 0)),
                  _resident((1, D), lambda i: (0, 0)),
                  _resident((D, 2 * LANES), lambda i: (0, 0)),
                  _resident((1, LANES), lambda i: (0, 0)),
                  _resident((piece, piece), lambda i: (0, 0))],
        out_specs=[pl.BlockSpec((tm, D), lambda i: (i, 0))]
                  + [pl.BlockSpec((tm, D // 2 // ROW_PARTS), lambda i: (i, 0))] * ROW_PARTS
                  + [pl.BlockSpec((tm, LANES), lambda i: (i, 0)),
                   pl.BlockSpec((ROUTE_ROWS, tm), lambda i: (0, i)),
                   pl.BlockSpec((1, LANES), lambda i: (0, 0))],
        out_shape=[jax.ShapeDtypeStruct((T, D), F32)]
                  + [jax.ShapeDtypeStruct((T, D // 2 // ROW_PARTS), jnp.uint32)] * ROW_PARTS
                  + [jax.ShapeDtypeStruct((T, LANES), F32),
                   jax.ShapeDtypeStruct((ROUTE_ROWS, T), F32),
                   jax.ShapeDtypeStruct((1, LANES), F32)],
        scratch_shapes=scratch,
        compiler_params=_params("arbitrary"),
        name="post_mixer",
    )(*a_args, w, h, gain.reshape(1, D), wr2, br, tri)


SC_WINDOW = 128


def _sc_mesh():
    return plsc.VectorSubcoreMesh(core_axis_name="core", subcore_axis_name="subcore")


def _sc_window_specs(W):
    rows = pl.BlockSpec((SC_WINDOW, W), lambda i: (i, 0))
    idx = pl.BlockSpec((1, SC_WINDOW), lambda i: (0, i))
    return rows, idx


def _sc_pipeline(body, n_rows, in_specs, out_specs):
    return pltpu.emit_pipeline(body, grid=(n_rows // SC_WINDOW,), in_specs=in_specs, out_specs=out_specs,
                               core_axis_name=("core", "subcore"), dimension_semantics=(pltpu.PARALLEL,))


def _dispatch(xs, dest, pad_idx, P):
    T, W = xs[0].shape
    n_pad = pad_idx.shape[1]
    n = len(xs)
    rows, idx = _sc_window_specs(W)
    zero_rows = pl.BlockSpec((SC_WINDOW, W), lambda i: (0, 0))
    out = jax.ShapeDtypeStruct((P, W), xs[0].dtype)

    @functools.partial(pl.kernel, out_type=(out,) * n, mesh=_sc_mesh(), scratch_types=[], name="dispatch")
    def scatter(*refs):
        x_hbm, (d0_hbm, d1_hbm, z_hbm, p_hbm), o_hbm = refs[:n], refs[n:n + 4], refs[n + 4:]
        for x, o in zip(x_hbm, o_hbm):
            def put_pair(x_vmem, i0_vmem, i1_vmem, o=o):
                pltpu.sync_copy(x_vmem, o.at[i0_vmem.at[0]])
                pltpu.sync_copy(x_vmem, o.at[i1_vmem.at[0]])

            def put(x_vmem, i_vmem, o=o):
                pltpu.sync_copy(x_vmem, o.at[i_vmem.at[0]])

            _sc_pipeline(put_pair, T, [rows, idx, idx], [])(x, d0_hbm, d1_hbm)
            _sc_pipeline(put, n_pad, [zero_rows, idx], [])(z_hbm, p_hbm)

    return scatter(*xs, dest[0:1], dest[1:2], jnp.zeros((SC_WINDOW, W), xs[0].dtype), pad_idx)


WEIGHT_LEADS = (3, 2, 1)
EXPERT_SLOTS = max(WEIGHT_LEADS) + 1


def _expert_kernel(blk_exp_ref, slot_ref, n_used_ref, *refs):
    x_refs = refs[:ROW_PARTS]
    w_refs = refs[ROW_PARTS:ROW_PARTS + 3]
    o_refs = refs[ROW_PARTS + 3:2 * ROW_PARTS + 3]
    w_slots = refs[2 * ROW_PARTS + 3:]
    lead = max(WEIGHT_LEADS)
    n_used = n_used_ref[0]
    i = pl.program_id(0) - lead
    D = w_slots[0].shape[1]

    for w_ref, w_s, ahead in zip(w_refs, w_slots, WEIGHT_LEADS):
        j = i + ahead
        jc = jnp.clip(j, 0, n_used - 1)
        arrived = (j >= 0) & (j < n_used) & ((j == 0) | (blk_exp_ref[jc] != blk_exp_ref[jnp.maximum(jc - 1, 0)]))

        @pl.when(arrived)
        def _(w_ref=w_ref, w_s=w_s, jc=jc):
            w_s[slot_ref[jc]] = w_ref[0, 0].astype(BF16)

    @pl.when((i >= 0) & (i < n_used))
    def _():
        slot = slot_ref[jnp.clip(i, 0, n_used - 1)]
        wg_s, wu_s, wd_s = w_slots
        pieces = []
        for part, x_ref in enumerate(x_refs):
            for cols, val in zip(_part_cols(D, part), _unpack_bf16_pair(x_ref[...])):
                pieces.append((cols, val.astype(BF16)))

        def up(w_s):
            return sum(jnp.dot(val, w_s[slot, cols, :], preferred_element_type=F32) for cols, val in pieces)

        g = up(wg_s)
        hid = (g * jax.nn.sigmoid(g) * up(wu_s)).astype(BF16)
        y = jnp.dot(hid, wd_s[slot], preferred_element_type=F32)
        for part, o_ref in enumerate(o_refs):
            o_ref[...] = _pack_part(y, part)

    @pl.when(i >= n_used)
    def _():
        for o_ref in o_refs:
            o_ref[...] = jnp.zeros_like(o_ref)


def _experts(xs, blk_exp, n_used, layer, w_gate, w_up, w_down):
    P, W = xs[0].shape
    D = 2 * W * ROW_PARTS
    FF = w_gate.shape[3]
    nblk = P // MOE_BLOCK
    lead = max(WEIGHT_LEADS)
    changes = jnp.concatenate([jnp.zeros((1,), jnp.int32), (blk_exp[1:] != blk_exp[:-1]).astype(jnp.int32)])
    slot = jnp.cumsum(changes) % EXPERT_SLOTS

    def x_map(g, be, sl, nu):
        return (jnp.clip(g - lead, 0, nu[0] - 1), 0)

    def o_map(g, be, sl, nu):
        return (jnp.maximum(g - lead, 0), 0)

    def w_map(ahead):
        return lambda g, be, sl, nu: (layer, be[jnp.clip(g - lead + ahead, 0, nu[0] - 1)], 0, 0)

    wg_spec, wu_spec, wd_spec = (pl.BlockSpec((1, 1) + shape, w_map(ahead))
                                 for shape, ahead in zip(((D, FF), (D, FF), (FF, D)), WEIGHT_LEADS))
    return pl.pallas_call(
        _expert_kernel,
        grid_spec=pltpu.PrefetchScalarGridSpec(
            num_scalar_prefetch=3,
            grid=(nblk + lead,),
            in_specs=[pl.BlockSpec((MOE_BLOCK, W), x_map)] * ROW_PARTS + [wg_spec, wu_spec, wd_spec],
            out_specs=[pl.BlockSpec((MOE_BLOCK, W), o_map)] * ROW_PARTS,
            scratch_shapes=[pltpu.VMEM((EXPERT_SLOTS, D, FF), BF16), pltpu.VMEM((EXPERT_SLOTS, D, FF), BF16),
                            pltpu.VMEM((EXPERT_SLOTS, FF, D), BF16)]),
        out_shape=[jax.ShapeDtypeStruct((P, W), jnp.uint32)] * ROW_PARTS,
        compiler_params=_params("arbitrary"),
        name="experts",
    )(blk_exp, slot, n_used, *xs, w_gate, w_up, w_down)


def _gather_pairs(ys, dest):
    W = ys[0].shape[1]
    T = dest.shape[1]
    n = len(ys)
    rows, idx = _sc_window_specs(W)
    out = jax.ShapeDtypeStruct((T, W), ys[0].dtype)

    @functools.partial(pl.kernel, out_type=(out,) * (2 * n), mesh=_sc_mesh(), scratch_types=[],
                       name="gather_pairs")
    def gather(*refs):
        y_hbm, d_hbm, o_hbm = refs[:n], refs[n:n + 2], refs[n + 2:]
        for slot, d in enumerate(d_hbm):
            for y, o in zip(y_hbm, o_hbm[slot * n:(slot + 1) * n]):
                def get(i_vmem, o_vmem, y=y):
                    pltpu.sync_copy(y.at[i_vmem.at[0]], o_vmem)

                _sc_pipeline(get, T, [idx], [rows])(d, o)

    return gather(*ys, dest[0:1], dest[1:2])


def _combined_rows(h_ref, route_ref, y_refs):
    D = h_ref.shape[1]
    route = route_ref[...]
    gates = (route[:, ROUTE_G1:ROUTE_G1 + 1], route[:, ROUTE_G2:ROUTE_G2 + 1])
    pieces = {}
    for part in range(ROW_PARTS):
        slots = [_unpack_bf16_pair(y_refs[slot * ROW_PARTS + part][...]) for slot in range(2)]
        for half, cols in enumerate(_part_cols(D, part)):
            pieces[cols.start] = h_ref[:, cols] + (gates[0] * slots[0][half] + gates[1] * slots[1][half])
    return jnp.concatenate([pieces[c] for c in sorted(pieces)], axis=1)


def _combine_kernel(h_ref, route_ref, *refs):
    refs[-1][...] = _combined_rows(h_ref, route_ref, refs[:-1])


def _combine(h, route, pairs):
    T, D = h.shape
    W = pairs[0].shape[1]
    tm = min(ROW_TILE, T)
    return pl.pallas_call(
        _combine_kernel,
        grid=(T // tm,),
        in_specs=[pl.BlockSpec((tm, D), lambda i: (i, 0)), pl.BlockSpec((tm, LANES), lambda i: (i, 0))]
                 + [pl.BlockSpec((tm, W), lambda i: (i, 0))] * len(pairs),
        out_specs=pl.BlockSpec((tm, D), lambda i: (i, 0)),
        out_shape=jax.ShapeDtypeStruct((T, D), F32),
        compiler_params=_params("parallel"),
        name="combine",
    )(h, route, *pairs)


def _moe(h1, hn, route, route_t, counts, layer, w_gate, w_up, w_down, combine):
    T, D = h1.shape
    A = 2 * T
    nblk = -(-A // MOE_BLOCK) + N_EXPERTS
    P = nblk * MOE_BLOCK
    eid = route_t[ROUTE_E1:ROUTE_E2 + 1].astype(jnp.int32)
    rank = route_t[ROUTE_R1:ROUTE_R2 + 1].astype(jnp.int32)
    cnt = counts[0, :N_EXPERTS].astype(jnp.int32)
    padded = (cnt + MOE_BLOCK - 1) // MOE_BLOCK * MOE_BLOCK
    pends = jnp.cumsum(padded)
    pstarts = pends - padded
    experts = jnp.arange(N_EXPERTS, dtype=jnp.int32)
    start_of = jnp.sum(jnp.where(eid[:, None, :] == experts[None, :, None], pstarts[None, :, None], 0), axis=1)
    dest = start_of + rank
    blk_start = jnp.arange(nblk, dtype=jnp.int32) * MOE_BLOCK
    blk_exp = jnp.minimum(jnp.sum((pends[None, :] <= blk_start[:, None]).astype(jnp.int32), axis=1),
                          N_EXPERTS - 1)
    n_used = pends[-1:] // MOE_BLOCK

    gap_start = jnp.concatenate([pstarts + cnt, pends[-1:]])
    gap_len = jnp.concatenate([padded - cnt, P - pends[-1:]])
    gap_end = jnp.cumsum(gap_len)
    j = jnp.arange(P - A, dtype=jnp.int32)
    gap_of = jnp.sum((gap_end[None, :] <= j[:, None]).astype(jnp.int32), axis=1)
    sel = gap_of[:, None] == jnp.arange(N_EXPERTS + 1, dtype=jnp.int32)[None, :]
    pad_idx = jnp.sum(jnp.where(sel, (gap_start - (gap_end - gap_len))[None, :] + j[:, None], 0), axis=1)

    xs = _dispatch(hn, dest, pad_idx.reshape(1, P - A), P)
    ys = _experts(xs, blk_exp, n_used, layer, w_gate, w_up, w_down)
    pairs = _gather_pairs(ys, dest)
    return _combine(h1, route, pairs) if combine else (h1, route, pairs)


def _qkv_kernel(*refs, rate, n, n_chunks, n_pending):
    n_x = 2 + n_pending if n_pending else 1
    x_refs = refs[:n_x]
    gq_ref, gkv_ref, wq_ref, wk_ref, wvt_ref, qn_ref, kn_ref, seg_ref, q_ref, k_ref, vt_ref = refs[n_x:n_x + 11]
    xs_ref = refs[-1]
    seg = seg_ref[...]
    width = seg.shape[0]
    res_per_chunk = q_ref.shape[1]
    chunk = pl.program_id(2)

    def head_norm(t, gain):
        cols = []
        for j in range(t.shape[1] // width):
            tj = t[:, j * width:(j + 1) * width]
            ms = jnp.dot((tj * tj).astype(BF16), seg, preferred_element_type=F32)
            cols.append(tj * lax.rsqrt(ms + EPS))
        return jnp.concatenate(cols, axis=1) * gain

    def project(x):
        y = _rms(x)
        xq = (y * gq_ref[...]).astype(BF16)
        xkv = (y * gkv_ref[...]).astype(BF16)
        q = head_norm(jnp.dot(xq, wq_ref[...], preferred_element_type=F32), qn_ref[...]).astype(q_ref.dtype)
        k = head_norm(jnp.dot(xkv, wk_ref[...], preferred_element_type=F32), kn_ref[...]).astype(k_ref.dtype)
        vt = lax.dot_general(wvt_ref[...], xkv, NT_DIMS, preferred_element_type=F32).astype(vt_ref.dtype)
        for j in range(res_per_chunk):
            q_ref[0, j] = q[j * n:(j + 1) * n]
            k_ref[0, j] = k[j * n:(j + 1) * n]
            vt_ref[0, j] = vt[:, j * n:(j + 1) * n]

    if n_pending:
        x = _combined_rows(x_refs[0], x_refs[1], x_refs[2:])
        refs[-2][...] = x
        project(x)
        return
    x_ref, = x_refs
    if rate == 1:
        project(x_ref[...])
        return

    @pl.when(chunk == 0)
    def _():
        for j in range(xs_ref.shape[0]):
            xs_ref[j] = x_ref[:, j * LANES:(j + 1) * LANES]

    for ch in range(n_chunks):
        @pl.when(chunk == ch)
        def _(ch=ch):
            residues = range(ch * res_per_chunk, (ch + 1) * res_per_chunk)
            project(jnp.concatenate(
                [jnp.concatenate([xs_ref[j, pl.ds(c, n, stride=rate), :] for j in range(xs_ref.shape[0])], axis=1)
                 for c in residues], axis=0))


def _qkv(h, B, S, rate, gq, gkv, wq, wk, wv, qn, kn):
    pending = h if isinstance(h, tuple) else None
    D = (pending[0] if pending else h).shape[1]
    L = S // rate
    hd = DIL_HEAD_DIM
    n = max(SUB_BLOCK, TOKEN_TILE // rate)
    tm = n * rate
    res_per_chunk = max(1, TOKEN_TILE // n)
    n_chunks = rate // res_per_chunk
    width = 2 * LANES
    ii = jnp.arange(width)
    seg = jnp.where((ii[:, None] // hd) == (ii[None, :] // hd), 1.0 / hd, 0.0).astype(BF16)
    row = lambda g: jnp.tile(g, D // hd).reshape(1, D)
    const = lambda shape: _resident(shape, lambda b, i, c: (0,) * len(shape))
    qk_spec = pl.BlockSpec((1, res_per_chunk, n, D), lambda b, i, c: (b, c, i, 0))
    qk_shape = jax.ShapeDtypeStruct((B, rate, L, D), BF16)
    rows = lambda width: pl.BlockSpec((tm, width), lambda b, i, c: (b * (S // tm) + i, 0))
    if pending:
        assert rate == 1
        h1, route, pairs = pending
        x_args = (h1, route, *pairs)
        x_specs = [rows(D), rows(LANES)] + [rows(pairs[0].shape[1])] * len(pairs)
        extra_specs, extra_shapes = [rows(D)], [jax.ShapeDtypeStruct((B * S, D), F32)]
    else:
        x_args, x_specs, extra_specs, extra_shapes = (h,), [rows(D)], [], []
    return pl.pallas_call(
        functools.partial(_qkv_kernel, rate=rate, n=n, n_chunks=n_chunks,
                          n_pending=len(pending[2]) if pending else 0),
        grid=(B, S // tm, n_chunks),
        in_specs=x_specs + [
                  const((1, D)), const((1, D)), const((D, D)), const((D, D)), const((D, D)),
                  const((1, D)), const((1, D)), const((width, width))],
        out_specs=[qk_spec, qk_spec,
                   pl.BlockSpec((1, res_per_chunk, D, n), lambda b, i, c: (b, c, 0, i))] + extra_specs,
        out_shape=[qk_shape, qk_shape, jax.ShapeDtypeStruct((B, rate, D, L), BF16)] + extra_shapes,
        scratch_shapes=[pltpu.VMEM((D // LANES, tm if rate > 1 else 8, LANES), F32)],
        compiler_params=_params("parallel", "parallel", "arbitrary"),
        name="qkv_rate%d" % rate,
    )(*x_args, gq.reshape(1, D), gkv.reshape(1, D), wq, wk, wv.T, row(qn) * (hd ** -0.5 * LOG2E), row(kn), seg)


def _attn_kernel(*refs, rate, with_prev):
    if not with_prev:
        q_ref, kc_ref, vc_ref, edge_ref, o_ref, lse_ref, os_ref = refs
    elif len(refs) == 10:
        q_ref, kc_ref, kp_ref, vc_ref, vp_ref, edge_ref, bias_ref, o_ref, lse_ref, os_ref = refs
    else:
        q_ref, kc_ref, kp_ref, vc_ref, vp_ref, edge_ref, o_ref, lse_ref, os_ref = refs
    Bk = SUB_BLOCK
    hd = DIL_HEAD_DIM
    n_pairs = DIL_HEADS // 2
    lane = lax.broadcasted_iota(jnp.int32, (Bk, LANES), 1)

    def block(load_q, load_k, load_vt, load_bias, store):
        out_t, lse_t = [], []
        for p in range(n_pairs):
            qp = load_q(p)
            zero = jnp.zeros_like(qp)
            q2 = jnp.concatenate([jnp.where(lane < hd, qp, zero), jnp.where(lane >= hd, qp, zero)], axis=0)
            s = lax.dot_general(load_k(p), q2, NT_DIMS, preferred_element_type=F32) + load_bias(p)
            m = jnp.max(s, axis=0, keepdims=True)
            pr = jnp.exp2(s - m)
            l = jnp.sum(pr, axis=0, keepdims=True)
            pb = pr.astype(BF16)
            vt = load_vt(p)
            out_t.append(jnp.dot(vt[:hd], pb[:, :Bk], preferred_element_type=F32) / l[:, :Bk])
            out_t.append(jnp.dot(vt[hd:], pb[:, Bk:], preferred_element_type=F32) / l[:, Bk:])
            lse = (m + jnp.log2(l)) * LN2
            lse_t += [lse[:, :Bk], lse[:, Bk:]]
        lse_t.append(jnp.zeros((LANES - DIL_HEADS, Bk), F32))
        store(jnp.concatenate(out_t, axis=0).T, jnp.concatenate(lse_t, axis=0).T)

    def pair_cols(p):
        return slice(p * LANES, (p + 1) * LANES)

    n_res, n_blocks = kc_ref.shape[1], kc_ref.shape[2] // Bk
    for c in range(n_res):
        for j in range(n_blocks):
            rows = slice(j * Bk, (j + 1) * Bk)
            if not with_prev:
                load_k = lambda p, c=c: kc_ref[0, c, :, pair_cols(p)]
                load_vt = lambda p, c=c: vc_ref[0, c, pair_cols(p), :]
                load_bias = lambda p: edge_ref[0, p]
            elif j == 0:
                load_k = lambda p, c=c: jnp.concatenate([kp_ref[0, c, :, pair_cols(p)],
                                                         kc_ref[0, c, :Bk, pair_cols(p)]], axis=0)
                load_vt = lambda p, c=c: jnp.concatenate([vp_ref[0, c, pair_cols(p), :],
                                                          vc_ref[0, c, pair_cols(p), :Bk]], axis=1)
                load_bias = lambda p: edge_ref[0, p]
            else:
                krows = slice((j - 1) * Bk, (j + 1) * Bk)
                load_k = lambda p, c=c, krows=krows: kc_ref[0, c, krows, pair_cols(p)]
                load_vt = lambda p, c=c, krows=krows: vc_ref[0, c, pair_cols(p), krows]
                load_bias = lambda p: bias_ref[p]

            if rate == 1:
                def store(o, lse, rows=rows):
                    o_ref[rows, :] = o.astype(o_ref.dtype)
                    lse_ref[rows, :] = lse
            else:
                def store(o, lse, out_rows=pl.ds(c + j * Bk * rate, Bk, stride=rate)):
                    for g in range(os_ref.shape[0]):
                        os_ref[g, out_rows, :] = o[:, pair_cols(g)]
                    lse_ref[out_rows, :] = lse

            block(lambda p, c=c, rows=rows: q_ref[0, c, rows, pair_cols(p)], load_k, load_vt, load_bias, store)
    if rate > 1:
        for g in range(os_ref.shape[0]):
            o_ref[:, pair_cols(g)] = os_ref[g].astype(o_ref.dtype)


def _t5_bucket(n):
    max_exact = NUM_BUCKETS // 2
    nf = jnp.maximum(n, max_exact).astype(F32)
    large = max_exact + (jnp.log(nf / max_exact) / math.log(MAX_DISTANCE / max_exact)
                         * (NUM_BUCKETS - max_exact)).astype(jnp.int32)
    large = jnp.minimum(large, NUM_BUCKETS - 1)
    return jnp.where(n < max_exact, n, large)


def _group_attention(q, k, vt, bias_table, rate, n_steps):
    B, _, L, D = q.shape
    S = L * rate
    Bk = SUB_BLOCK
    with_prev = L > Bk
    nk = 2 * Bk if with_prev else Bk
    n = max(Bk, ATTN_TILE // rate)
    tm = n * rate
    nt = S // tm
    n_pairs = DIL_HEADS // 2
    ql = jnp.arange(Bk, dtype=jnp.int32)[:, None]
    kl = jnp.arange(2 * Bk, dtype=jnp.int32)[None, :]
    steps = ql + Bk - kl
    bucket = _t5_bucket(jnp.maximum(steps, 0) * rate)
    buckets = jnp.arange(NUM_BUCKETS, dtype=jnp.int32)
    bias = jnp.sum(jnp.where(bucket[None, :, :, None] == buckets[:, None, None, None],
                             bias_table.astype(F32)[:, None, None, :], 0.0), axis=0)
    band = ((steps >= 0) & (steps <= n_steps))[:, :, None]
    first = (kl < Bk)[:, :, None]

    def layout(t):
        t = t[:, 2 * Bk - nk:, :].transpose(1, 2, 0)
        return t.reshape(nk, n_pairs, 2 * Bk).transpose(1, 0, 2)

    bias_in = layout(jnp.where(band, bias * LOG2E, NEG_INF))
    bias_first = layout(jnp.where(band & ~first, bias * LOG2E, NEG_INF))
    edge = jnp.stack([bias_first, bias_in])

    cur_qk = pl.BlockSpec((1, rate, n, D), lambda b, i: (b, 0, i, 0))
    cur_vt = pl.BlockSpec((1, rate, D, n), lambda b, i: (b, 0, 0, i))
    per_n = n // Bk
    prev_qk = pl.BlockSpec((1, rate, Bk, D), lambda b, i: (b, 0, jnp.maximum(i * per_n - 1, 0), 0))
    prev_vt = pl.BlockSpec((1, rate, D, Bk), lambda b, i: (b, 0, 0, jnp.maximum(i * per_n - 1, 0)))
    edge_spec = pl.BlockSpec((1, n_pairs, nk, 2 * Bk), lambda b, i: (jnp.minimum(i, 1), 0, 0, 0))
    if with_prev:
        in_specs = [cur_qk, cur_qk, prev_qk, cur_vt, prev_vt, edge_spec]
        args = (q, k, k, vt, vt, edge)
    else:
        in_specs = [cur_qk, cur_qk, cur_vt, edge_spec]
        args = (q, k, vt, edge)
    if with_prev and n > Bk:
        in_specs.append(_resident((n_pairs, nk, 2 * Bk), lambda b, i: (0, 0, 0)))
        args += (bias_in,)
    return pl.pallas_call(
        functools.partial(_attn_kernel, rate=rate, with_prev=with_prev),
        grid=(B, nt),
        in_specs=in_specs,
        out_specs=[pl.BlockSpec((tm, D), lambda b, i: (b * nt + i, 0)),
                   pl.BlockSpec((tm, LANES), lambda b, i: (b * nt + i, 0))],
        out_shape=[jax.ShapeDtypeStruct((B * S, D), BF16),
                   jax.ShapeDtypeStruct((B * S, LANES), F32)],
        scratch_shapes=[pltpu.VMEM((D // LANES, tm if rate > 1 else 8, LANES), F32)],
        compiler_params=_params("parallel", "parallel"),
        name="attn_rate%d" % rate,
    )(*args)


def kernel(x, ret_w_in, ret_w_out, kv_norm, w_kv, k_norm, dil_wq, q_norm, dil_wo, rel_bias,
           mixer_norm, ffn_norm, router_grp, router_grp_b, router_exp, router_exp_b,
           exp_gate, exp_up, exp_down):
    B, S, D = x.shape
    h = x.reshape(B * S, D)

    def moe_layer(layer, mixer, w_out, h, combine):
        h1, *hn, route, route_t, counts = _post(
            mixer, w_out.astype(BF16), h, ffn_norm[layer], router_grp[layer], router_grp_b[layer],
            router_exp[layer], router_exp_b[layer])
        return _moe(h1, hn, route, route_t, counts, layer, exp_gate, exp_up, exp_down, combine)

    h = moe_layer(0, (mixer_norm[0], ret_w_in[0].astype(BF16), S), ret_w_out[0], h, combine=False)

    G = len(DIL_RATES)
    gd = DIL_HEADS * DIL_HEAD_DIM
    outs, lses = [], []
    for g in range(G):
        cq = slice(g * gd, (g + 1) * gd)
        cv = slice(G * gd + g * gd, G * gd + (g + 1) * gd)
        q, k, vt, *combined = _qkv(h, B, S, DIL_RATES[g], mixer_norm[1], kv_norm,
                                   dil_wq[0][:, cq].astype(BF16), w_kv[:, cq].astype(BF16),
                                   w_kv[:, cv].astype(BF16), q_norm[0][g], k_norm[g])
        if combined:
            h, = combined
        o, lse = _group_attention(q, k, vt, rel_bias[:, g * DIL_HEADS:(g + 1) * DIL_HEADS],
                                  DIL_RATES[g], DIL_WINDOWS[g] // DIL_RATES[g])
        outs.append(o)
        lses.append(lse)
    h = moe_layer(1, (tuple(outs), tuple(lses)), dil_wo[0], h, combine=True)
    return h.reshape(B, S, D)
```

```python
import functools
import math

import jax
import jax.numpy as jnp
from jax import lax
from jax.experimental import pallas as pl
from jax.experimental.pallas import tpu as pltpu
from jax.experimental.pallas import tpu_sc as plsc

F32 = jnp.float32
BF16 = jnp.bfloat16

EPS = 1e-6
NEG_INF = -1e30

RET_HEADS = 4
RET_CHUNK = 256
ROPE_BASE = 10000.0

DIL_WINDOWS = (128, 512, 2048)
DIL_RATES = (1, 4, 16)
DIL_HEADS = 16
DIL_HEAD_DIM = 64
SUB_BLOCK = 128
NUM_BUCKETS = 32
MAX_DISTANCE = 2048

MOE_GROUPS = 4
EXPERTS_PER_GROUP = 8
N_EXPERTS = MOE_GROUPS * EXPERTS_PER_GROUP
MOE_BLOCK = 512

LANES = 128
ROW_TILE = 1024
TOKEN_TILE = 512
ATTN_TILE = 1024
VMEM_LIMIT = 56 * 1024 * 1024

NT_DIMS = (((1,), (1,)), ((), ()))
LOG2E = math.log2(math.e)
LN2 = math.log(2.0)


def _params(*sem):
    return pltpu.CompilerParams(dimension_semantics=sem, vmem_limit_bytes=VMEM_LIMIT)


def _resident(shape, index_map):
    return pl.BlockSpec(shape, index_map, pipeline_mode=pl.Buffered(1))


def _rms(x):
    return x * lax.rsqrt(jnp.mean(x * x, axis=-1, keepdims=True) + EPS)


def _retention_tile(x_ref, g_ref, w_ref, cos_ref, sin_ref, din_ref, xi_ref, zeta_ref, cd_ref, state_ref,
                    first_tile):
    C = RET_CHUNK
    H, dk, dv = state_ref.shape
    half = dk // 2
    ts = x_ref.shape[0]
    k_scale = dk ** -0.5

    @pl.when(first_tile)
    def _():
        state_ref[...] = jnp.zeros_like(state_ref)

    xn = (_rms(x_ref[...]) * g_ref[...]).astype(BF16)
    cos, sin = cos_ref[...], sin_ref[...]

    def proj(c0, width):
        return jnp.dot(xn, w_ref[:, c0:c0 + width], preferred_element_type=F32)

    def rot(t):
        t1, t2 = t[:, :half], t[:, half:]
        return jnp.concatenate([t1 * cos - t2 * sin, t1 * sin + t2 * cos], axis=1)

    heads = []
    for hh in range(H):
        q = rot(proj(hh * dk, dk)).astype(BF16)
        k = rot(proj(H * dk + hh * dk, dk)) * k_scale
        kb = k.astype(BF16)
        v = proj(2 * H * dk + hh * dv, dv).astype(BF16)
        gate = proj(2 * H * dk + H * dv + hh * dv, dv)
        gate = gate * jax.nn.sigmoid(gate)
        chunks = []
        for c in range(ts // C):
            rows = slice(c * C, (c + 1) * C)
            s = lax.dot_general(q[rows], kb[rows], NT_DIMS, preferred_element_type=F32) * din_ref[hh]
            inner = jnp.dot(s.astype(BF16), v[rows], preferred_element_type=F32)
            state = state_ref[hh]
            cross = jnp.dot(q[rows], state.astype(BF16), preferred_element_type=F32) * xi_ref[hh]
            kz_t = (k[rows] * zeta_ref[hh]).T.astype(BF16)
            state_ref[hh] = state * cd_ref[hh] + jnp.dot(kz_t, v[rows], preferred_element_type=F32)
            chunks.append((gate[rows] * _rms(inner + cross)).astype(BF16))
        heads.append(jnp.concatenate(chunks, axis=0))
    return jnp.concatenate(heads, axis=1)


def _retention_operands(gain, w, S, ts):
    H, C = RET_HEADS, RET_CHUNK
    D = w.shape[0]
    dk = D // H
    half = dk // 2
    nt = S // ts
    pos = jnp.arange(S, dtype=F32)
    inv = 1.0 / (ROPE_BASE ** jnp.linspace(0.0, 1.0, half, dtype=F32))
    ang = pos[:, None] * inv[None, :]
    log_g = jnp.log(1.0 - 2.0 ** (-5.0 - jnp.arange(H, dtype=F32)))
    idx = jnp.arange(C, dtype=F32)
    diff = idx[:, None] - idx[None, :]
    d_in = jnp.where(diff >= 0, jnp.exp(log_g[:, None, None] * jnp.maximum(diff, 0.0)), 0.0)
    xi = jnp.exp(log_g[:, None] * (idx + 1.0))[:, :, None]
    zeta = jnp.exp(log_g[:, None] * (C - 1.0 - idx))[:, :, None]
    chunk_decay = jnp.exp(log_g * C)[:, None, None]
    table = pl.BlockSpec((ts, half), lambda i: (i % nt, 0))
    per_head = lambda shape: _resident((H,) + shape, lambda i: (0, 0, 0))
    args = (gain.reshape(1, D), w, jnp.cos(ang), jnp.sin(ang), d_in, xi, zeta, chunk_decay)
    specs = [_resident((1, D), lambda i: (0, 0)), _resident(w.shape, lambda i: (0, 0)), table, table,
             per_head((C, C)), per_head((C, 1)), per_head((C, 1)), per_head((1, 1))]
    state = pltpu.VMEM((H, dk, 2 * dk), F32)
    return args, specs, state


ROUTE_E1, ROUTE_E2, ROUTE_G1, ROUTE_G2, ROUTE_R1, ROUTE_R2 = range(6)
ROUTE_ROWS = 8
ROUTE_PIECES = 2
ROUTER_EXP_LANE0 = MOE_GROUPS
HI16 = 0xFFFF0000


def _pack_bf16_pair(a, b):
    ua = lax.bitcast_convert_type(a.astype(BF16).astype(F32), jnp.uint32)
    ub = lax.bitcast_convert_type(b.astype(BF16).astype(F32), jnp.uint32)
    return ua | (ub >> 16)


def _unpack_bf16_pair(w):
    a = lax.bitcast_convert_type(w & jnp.uint32(HI16), F32)
    b = lax.bitcast_convert_type(w << 16, F32)
    return a, b


ROW_PARTS = 2


def _part_cols(D, part):
    w = D // 2 // ROW_PARTS
    return slice(part * w, (part + 1) * w), slice(D // 2 + part * w, D // 2 + (part + 1) * w)


def _pack_part(x, part):
    hi, lo = _part_cols(x.shape[1], part)
    return _pack_bf16_pair(x[:, hi], x[:, lo])


def _merge_groups(o_refs, l_refs, ex_ref):
    ex = ex_ref[...]

    def expand(w):
        return jnp.dot(w.astype(BF16), ex, preferred_element_type=F32)

    lses = [l_ref[...] for l_ref in l_refs]
    top = functools.reduce(jnp.maximum, lses)
    ws = [jnp.exp(l - top) for l in lses]
    den = sum(ws)
    return sum(expand(w / den) * o_ref[...].astype(F32) for w, o_ref in zip(ws, o_refs)).astype(BF16)


def _post_kernel(*refs, n_groups, seq_tiles):
    n_a = 2 * n_groups + 1 if n_groups else 8
    a_refs = refs[:n_a]
    (w_ref, h_ref, g_ref, wr_ref, br_ref, tri_ref,
     h1_ref, hn0_ref, hn1_ref, route_ref, route_t_ref, cnt_ref, carry_ref) = refs[n_a:n_a + 13]
    hn_refs = (hn0_ref, hn1_ref)
    if n_groups:
        a = _merge_groups(a_refs[:n_groups], a_refs[n_groups:2 * n_groups], a_refs[-1])
    else:
        a = _retention_tile(h_ref, *a_refs, refs[-1], pl.program_id(0) % seq_tiles == 0)
    @pl.when(pl.program_id(0) == 0)
    def _():
        carry_ref[...] = jnp.zeros_like(carry_ref)

    half = tri_ref.shape[0]
    n_pieces = h_ref.shape[0] // half
    D = h_ref.shape[1]
    lane = lax.broadcasted_iota(jnp.int32, (half, LANES), 1).astype(F32)
    ninf = -jnp.inf

    def first_argmax(vals):
        top = jnp.max(vals, axis=1, keepdims=True)
        where = jnp.min(jnp.where(vals == top, lane, float(LANES)), axis=1, keepdims=True)
        return top, where

    h1_ref[...] = h_ref[...] + jnp.dot(a, w_ref[...], preferred_element_type=F32)

    def route_half(rows):
        hn = _rms(h1_ref[rows, :]) * g_ref[...]
        for part, ref in enumerate(hn_refs):
            ref[rows, :] = _pack_part(hn, part)
        hi = hn.astype(BF16)
        lo = (hn - hi.astype(F32)).astype(BF16)
        both = jnp.dot(hi, wr_ref[...], preferred_element_type=F32)
        logits = (both[:, :LANES] + both[:, LANES:]
                  + jnp.dot(lo, wr_ref[:, :LANES], preferred_element_type=F32) + br_ref[...])

        is_grp = lane < MOE_GROUPS
        lg = jnp.where(is_grp, logits, ninf)
        mg, grp = first_argmax(lg)
        p_grp = 1.0 / jnp.sum(jnp.where(is_grp, jnp.exp(lg - mg), 0.0), axis=1, keepdims=True)

        e_lane = lane - ROUTER_EXP_LANE0
        in_grp = (e_lane < N_EXPERTS) & (jnp.floor(e_lane * (1.0 / EXPERTS_PER_GROUP)) == grp)
        le = jnp.where(in_grp, logits, ninf)
        v1, i1 = first_argmax(le)
        le2 = jnp.where(lane == i1, ninf, le)
        v2, i2 = first_argmax(le2)
        e = jnp.exp(v2 - v1)
        hit1 = lane == (i1 - ROUTER_EXP_LANE0)
        hit2 = lane == (i2 - ROUTER_EXP_LANE0)
        onehot = jnp.where(hit1 | hit2, 1.0, 0.0)
        earlier = jnp.dot(tri_ref[...], onehot.astype(BF16), preferred_element_type=F32)
        return dict(e1=i1 - ROUTER_EXP_LANE0, e2=i2 - ROUTER_EXP_LANE0, g1=p_grp / (1.0 + e),
                    g2=p_grp * e / (1.0 + e), hit1=hit1, hit2=hit2, earlier=earlier,
                    count=jnp.sum(onehot, axis=0, keepdims=True))

    halves = [route_half(slice(j * half, (j + 1) * half)) for j in range(n_pieces)]
    carry = carry_ref[...]
    for j, r in enumerate(halves):
        before = carry + r["earlier"]
        r1 = jnp.sum(jnp.where(r["hit1"], before, 0.0), axis=1, keepdims=True)
        r2 = jnp.sum(jnp.where(r["hit2"], before, 0.0), axis=1, keepdims=True)
        carry = carry + r["count"]
        route = jnp.zeros((half, LANES), F32)
        for slot, val in ((ROUTE_E1, r["e1"]), (ROUTE_E2, r["e2"]), (ROUTE_G1, r["g1"]),
                          (ROUTE_G2, r["g2"]), (ROUTE_R1, r1), (ROUTE_R2, r2)):
            route = jnp.where(lane == slot, val, route)
        route_ref[j * half:(j + 1) * half, :] = route
        route_t_ref[:, j * half:(j + 1) * half] = route.T[:ROUTE_ROWS]
    carry_ref[...] = carry
    cnt_ref[...] = carry


def _post(mixer, w, h, gain, w_grp, b_grp, w_exp, b_exp):
    T, D = h.shape
    K = w.shape[0]
    row_block = lambda width: pl.BlockSpec((tm, width), lambda i: (i, 0))
    scratch = [pltpu.VMEM((1, LANES), F32)]
    if len(mixer) == 2:
        outs, lses = mixer
        n_groups, seq_tiles = len(outs), 1
        tm = min(TOKEN_TILE, T)
        ex = jnp.where(jnp.arange(LANES)[:, None] == (jnp.arange(K)[None, :] // DIL_HEAD_DIM), 1.0, 0.0).astype(BF16)
        a_args = (*outs, *lses, ex)
        a_specs = [row_block(K)] * n_groups + [row_block(LANES)] * n_groups + [_resident((LANES, K), lambda i: (0, 0))]
    else:
        mix_gain, w_in, S = mixer
        tm = min(TOKEN_TILE, S)
        n_groups, seq_tiles = 0, S // tm
        a_args, a_specs, state = _retention_operands(mix_gain, w_in, S, tm)
        scratch.append(state)
    n_r = MOE_GROUPS + N_EXPERTS
    wr = jnp.zeros((D, LANES), F32).at[:, :n_r].set(jnp.concatenate([w_grp, w_exp], axis=1))
    wr_hi = wr.astype(BF16)
    wr_lo = (wr - wr_hi.astype(F32)).astype(BF16)
    wr2 = jnp.concatenate([wr_hi, wr_lo], axis=1)
    br = jnp.zeros((1, LANES), F32).at[0, :n_r].set(jnp.concatenate([b_grp, b_exp]))
    piece = tm // ROUTE_PIECES
    tri = jnp.tril(jnp.ones((piece, piece), BF16), k=-1)
    return pl.pallas_call(
        functools.partial(_post_kernel, n_groups=n_groups, seq_tiles=seq_tiles),
        grid=(T // tm,),
        in_specs=a_specs + [
                  _resident((K, D), lambda i: (0, 0)),
                  pl.BlockSpec((tm, D), lambda i: (i, 0)),
                  _resident((1, D), lambda i: (0, 0)),
                  _resident((D, 2 * LANES), lambda i: (0, 0)),
                  _resident((1, LANES), lambda i: (0, 0)),
                  _resident((piece, piece), lambda i: (0, 0))],
        out_specs=[pl.BlockSpec((tm, D), lambda i: (i, 0))]
                  + [pl.BlockSpec((tm, D // 2 // ROW_PARTS), lambda i: (i, 0))] * ROW_PARTS
                  + [pl.BlockSpec((tm, LANES), lambda i: (i, 0)),
                   pl.BlockSpec((ROUTE_ROWS, tm), lambda i: (0, i)),
                   pl.BlockSpec((1, LANES), lambda i: (0, 0))],
        out_shape=[jax.ShapeDtypeStruct((T, D), F32)]
                  + [jax.ShapeDtypeStruct((T, D // 2 // ROW_PARTS), jnp.uint32)] * ROW_PARTS
                  + [jax.ShapeDtypeStruct((T, LANES), F32),
                   jax.ShapeDtypeStruct((ROUTE_ROWS, T), F32),
                   jax.ShapeDtypeStruct((1, LANES), F32)],
        scratch_shapes=scratch,
        compiler_params=_params("arbitrary"),
        name="post_mixer",
    )(*a_args, w, h, gain.reshape(1, D), wr2, br, tri)


SC_WINDOW = 128


def _sc_mesh():
    return plsc.VectorSubcoreMesh(core_axis_name="core", subcore_axis_name="subcore")


def _sc_window_specs(W):
    rows = pl.BlockSpec((SC_WINDOW, W), lambda i: (i, 0))
    idx = pl.BlockSpec((1, SC_WINDOW), lambda i: (0, i))
    return rows, idx


def _sc_pipeline(body, n_rows, in_specs, out_specs):
    return pltpu.emit_pipeline(body, grid=(n_rows // SC_WINDOW,), in_specs=in_specs, out_specs=out_specs,
                               core_axis_name=("core", "subcore"), dimension_semantics=(pltpu.PARALLEL,))


def _dispatch(xs, dest, pad_idx, P):
    T, W = xs[0].shape
    n_pad = pad_idx.shape[1]
    n = len(xs)
    rows, idx = _sc_window_specs(W)
    zero_rows = pl.BlockSpec((SC_WINDOW, W), lambda i: (0, 0))
    out = jax.ShapeDtypeStruct((P, W), xs[0].dtype)

    @functools.partial(pl.kernel, out_type=(out,) * n, mesh=_sc_mesh(), scratch_types=[], name="dispatch")
    def scatter(*refs):
        x_hbm, (d0_hbm, d1_hbm, z_hbm, p_hbm), o_hbm = refs[:n], refs[n:n + 4], refs[n + 4:]
        for x, o in zip(x_hbm, o_hbm):
            def put_pair(x_vmem, i0_vmem, i1_vmem, o=o):
                pltpu.sync_copy(x_vmem, o.at[i0_vmem.at[0]])
                pltpu.sync_copy(x_vmem, o.at[i1_vmem.at[0]])

            def put(x_vmem, i_vmem, o=o):
                pltpu.sync_copy(x_vmem, o.at[i_vmem.at[0]])

            _sc_pipeline(put_pair, T, [rows, idx, idx], [])(x, d0_hbm, d1_hbm)
            _sc_pipeline(put, n_pad, [zero_rows, idx], [])(z_hbm, p_hbm)

    return scatter(*xs, dest[0:1], dest[1:2], jnp.zeros((SC_WINDOW, W), xs[0].dtype), pad_idx)


WEIGHT_LEADS = (3, 2, 1)
EXPERT_SLOTS = max(WEIGHT_LEADS) + 1


def _expert_kernel(blk_exp_ref, slot_ref, n_used_ref, *refs):
    x_refs = refs[:ROW_PARTS]
    w_refs = refs[ROW_PARTS:ROW_PARTS + 3]
    o_refs = refs[ROW_PARTS + 3:2 * ROW_PARTS + 3]
    w_slots = refs[2 * ROW_PARTS + 3:]
    lead = max(WEIGHT_LEADS)
    n_used = n_used_ref[0]
    i = pl.program_id(0) - lead
    D = w_slots[0].shape[1]

    for w_ref, w_s, ahead in zip(w_refs, w_slots, WEIGHT_LEADS):
        j = i + ahead
        jc = jnp.clip(j, 0, n_used - 1)
        arrived = (j >= 0) & (j < n_used) & ((j == 0) | (blk_exp_ref[jc] != blk_exp_ref[jnp.maximum(jc - 1, 0)]))

        @pl.when(arrived)
        def _(w_ref=w_ref, w_s=w_s, jc=jc):
            w_s[slot_ref[jc]] = w_ref[0, 0].astype(BF16)

    @pl.when((i >= 0) & (i < n_used))
    def _():
        slot = slot_ref[jnp.clip(i, 0, n_used - 1)]
        wg_s, wu_s, wd_s = w_slots
        pieces = []
        for part, x_ref in enumerate(x_refs):
            for cols, val in zip(_part_cols(D, part), _unpack_bf16_pair(x_ref[...])):
                pieces.append((cols, val.astype(BF16)))

        def up(w_s):
            return sum(jnp.dot(val, w_s[slot, cols, :], preferred_element_type=F32) for cols, val in pieces)

        g = up(wg_s)
        hid = (g * jax.nn.sigmoid(g) * up(wu_s)).astype(BF16)
        y = jnp.dot(hid, wd_s[slot], preferred_element_type=F32)
        for part, o_ref in enumerate(o_refs):
            o_ref[...] = _pack_part(y, part)

    @pl.when(i >= n_used)
    def _():
        for o_ref in o_refs:
            o_ref[...] = jnp.zeros_like(o_ref)


def _experts(xs, blk_exp, n_used, layer, w_gate, w_up, w_down):
    P, W = xs[0].shape
    D = 2 * W * ROW_PARTS
    FF = w_gate.shape[3]
    nblk = P // MOE_BLOCK
    lead = max(WEIGHT_LEADS)
    changes = jnp.concatenate([jnp.zeros((1,), jnp.int32), (blk_exp[1:] != blk_exp[:-1]).astype(jnp.int32)])
    slot = jnp.cumsum(changes) % EXPERT_SLOTS

    def x_map(g, be, sl, nu):
        return (jnp.clip(g - lead, 0, nu[0] - 1), 0)

    def o_map(g, be, sl, nu):
        return (jnp.maximum(g - lead, 0), 0)

    def w_map(ahead):
        return lambda g, be, sl, nu: (layer, be[jnp.clip(g - lead + ahead, 0, nu[0] - 1)], 0, 0)

    wg_spec, wu_spec, wd_spec = (pl.BlockSpec((1, 1) + shape, w_map(ahead))
                                 for shape, ahead in zip(((D, FF), (D, FF), (FF, D)), WEIGHT_LEADS))
    return pl.pallas_call(
        _expert_kernel,
        grid_spec=pltpu.PrefetchScalarGridSpec(
            num_scalar_prefetch=3,
            grid=(nblk + lead,),
            in_specs=[pl.BlockSpec((MOE_BLOCK, W), x_map)] * ROW_PARTS + [wg_spec, wu_spec, wd_spec],
            out_specs=[pl.BlockSpec((MOE_BLOCK, W), o_map)] * ROW_PARTS,
            scratch_shapes=[pltpu.VMEM((EXPERT_SLOTS, D, FF), BF16), pltpu.VMEM((EXPERT_SLOTS, D, FF), BF16),
                            pltpu.VMEM((EXPERT_SLOTS, FF, D), BF16)]),
        out_shape=[jax.ShapeDtypeStruct((P, W), jnp.uint32)] * ROW_PARTS,
        compiler_params=_params("arbitrary"),
        name="experts",
    )(blk_exp, slot, n_used, *xs, w_gate, w_up, w_down)


def _gather_pairs(ys, dest):
    W = ys[0].shape[1]
    T = dest.shape[1]
    n = len(ys)
    rows, idx = _sc_window_specs(W)
    out = jax.ShapeDtypeStruct((T, W), ys[0].dtype)

    @functools.partial(pl.kernel, out_type=(out,) * (2 * n), mesh=_sc_mesh(), scratch_types=[],
                       name="gather_pairs")
    def gather(*refs):
        y_hbm, d_hbm, o_hbm = refs[:n], refs[n:n + 2], refs[n + 2:]
        for slot, d in enumerate(d_hbm):
            for y, o in zip(y_hbm, o_hbm[slot * n:(slot + 1) * n]):
                def get(i_vmem, o_vmem, y=y):
                    pltpu.sync_copy(y.at[i_vmem.at[0]], o_vmem)

                _sc_pipeline(get, T, [idx], [rows])(d, o)

    return gather(*ys, dest[0:1], dest[1:2])


def _combined_rows(h_ref, route_ref, y_refs):
    D = h_ref.shape[1]
    route = route_ref[...]
    gates = (route[:, ROUTE_G1:ROUTE_G1 + 1], route[:, ROUTE_G2:ROUTE_G2 + 1])
    pieces = {}
    for part in range(ROW_PARTS):
        slots = [_unpack_bf16_pair(y_refs[slot * ROW_PARTS + part][...]) for slot in range(2)]
        for half, cols in enumerate(_part_cols(D, part)):
            pieces[cols.start] = h_ref[:, cols] + (gates[0] * slots[0][half] + gates[1] * slots[1][half])
    return jnp.concatenate([pieces[c] for c in sorted(pieces)], axis=1)


def _combine_kernel(h_ref, route_ref, *refs):
    refs[-1][...] = _combined_rows(h_ref, route_ref, refs[:-1])


def _combine_kernel_into(h_ref, route_ref, *refs):
    refs[-1][...] = _combined_rows(h_ref, route_ref, refs[:-2])


def _combine(h, route, pairs, first_row=0, partial=None):
    T, D = h.shape
    n, W = pairs[0].shape
    tm = min(ROW_TILE, n)
    t0 = first_row // tm
    rows = lambda width: pl.BlockSpec((tm, width), lambda i: (t0 + i, 0))
    in_specs = [rows(D), rows(LANES)] + [pl.BlockSpec((tm, W), lambda i: (i, 0))] * len(pairs)
    args = (h, route, *pairs)
    if partial is not None:
        in_specs.append(pl.BlockSpec(memory_space=pl.ANY))
        args += (partial,)
    return pl.pallas_call(
        _combine_kernel if partial is None else _combine_kernel_into,
        grid=(n // tm,),
        in_specs=in_specs,
        out_specs=rows(D),
        out_shape=jax.ShapeDtypeStruct((T, D), F32),
        input_output_aliases={} if partial is None else {len(args) - 1: 0},
        compiler_params=_params("parallel"),
        name="combine",
    )(*args)


def _moe(h1, hn, route, route_t, counts, layer, w_gate, w_up, w_down, combine):
    T, D = h1.shape
    A = 2 * T
    nblk = -(-A // MOE_BLOCK) + N_EXPERTS
    P = nblk * MOE_BLOCK
    eid = route_t[ROUTE_E1:ROUTE_E2 + 1].astype(jnp.int32)
    rank = route_t[ROUTE_R1:ROUTE_R2 + 1].astype(jnp.int32)
    cnt = counts[0, :N_EXPERTS].astype(jnp.int32)
    padded = (cnt + MOE_BLOCK - 1) // MOE_BLOCK * MOE_BLOCK
    pends = jnp.cumsum(padded)
    pstarts = pends - padded
    experts = jnp.arange(N_EXPERTS, dtype=jnp.int32)
    start_of = jnp.sum(jnp.where(eid[:, None, :] == experts[None, :, None], pstarts[None, :, None], 0), axis=1)
    dest = start_of + rank
    blk_start = jnp.arange(nblk, dtype=jnp.int32) * MOE_BLOCK
    blk_exp = jnp.minimum(jnp.sum((pends[None, :] <= blk_start[:, None]).astype(jnp.int32), axis=1),
                          N_EXPERTS - 1)
    n_used = pends[-1:] // MOE_BLOCK

    gap_start = jnp.concatenate([pstarts + cnt, pends[-1:]])
    gap_len = jnp.concatenate([padded - cnt, P - pends[-1:]])
    gap_end = jnp.cumsum(gap_len)
    j = jnp.arange(P - A, dtype=jnp.int32)
    gap_of = jnp.sum((gap_end[None, :] <= j[:, None]).astype(jnp.int32), axis=1)
    sel = gap_of[:, None] == jnp.arange(N_EXPERTS + 1, dtype=jnp.int32)[None, :]
    pad_idx = jnp.sum(jnp.where(sel, (gap_start - (gap_end - gap_len))[None, :] + j[:, None], 0), axis=1)

    xs = _dispatch(hn, dest, pad_idx.reshape(1, P - A), P)
    ys = _experts(xs, blk_exp, n_used, layer, w_gate, w_up, w_down)
    if not combine:
        return h1, route, _gather_pairs(ys, dest)
    out = None
    for first_row in range(0, T, T // 2):
        pairs = _gather_pairs(ys, dest[:, first_row:first_row + T // 2])
        out = _combine(h1, route, pairs, first_row, out)
    return out


def _qkv_kernel(*refs, rate, n, n_chunks, n_pending):
    n_x = 2 + n_pending if n_pending else 1
    x_refs = refs[:n_x]
    gq_ref, gkv_ref, wq_ref, wk_ref, wvt_ref, qn_ref, kn_ref, seg_ref, q_ref, k_ref, vt_ref = refs[n_x:n_x + 11]
    xs_ref = refs[-1]
    seg = seg_ref[...]
    width = seg.shape[0]
    res_per_chunk = q_ref.shape[1]
    chunk = pl.program_id(2)

    def head_norm(t, gain):
        cols = []
        for j in range(t.shape[1] // width):
            tj = t[:, j * width:(j + 1) * width]
            ms = jnp.dot((tj * tj).astype(BF16), seg, preferred_element_type=F32)
            cols.append(tj * lax.rsqrt(ms + EPS))
        return jnp.concatenate(cols, axis=1) * gain

    def project(x):
        y = _rms(x)
        xq = (y * gq_ref[...]).astype(BF16)
        xkv = (y * gkv_ref[...]).astype(BF16)
        q = head_norm(jnp.dot(xq, wq_ref[...], preferred_element_type=F32), qn_ref[...]).astype(q_ref.dtype)
        k = head_norm(jnp.dot(xkv, wk_ref[...], preferred_element_type=F32), kn_ref[...]).astype(k_ref.dtype)
        vt = lax.dot_general(wvt_ref[...], xkv, NT_DIMS, preferred_element_type=F32).astype(vt_ref.dtype)
        for j in range(res_per_chunk):
            q_ref[0, j] = q[j * n:(j + 1) * n]
            k_ref[0, j] = k[j * n:(j + 1) * n]
            vt_ref[0, j] = vt[:, j * n:(j + 1) * n]

    if n_pending:
        x = _combined_rows(x_refs[0], x_refs[1], x_refs[2:])
        refs[-2][...] = x
        project(x)
        return
    x_ref, = x_refs
    if rate == 1:
        project(x_ref[...])
        return

    @pl.when(chunk == 0)
    def _():
        for j in range(xs_ref.shape[0]):
            xs_ref[j] = x_ref[:, j * LANES:(j + 1) * LANES]

    for ch in range(n_chunks):
        @pl.when(chunk == ch)
        def _(ch=ch):
            residues = range(ch * res_per_chunk, (ch + 1) * res_per_chunk)
            project(jnp.concatenate(
                [jnp.concatenate([xs_ref[j, pl.ds(c, n, stride=rate), :] for j in range(xs_ref.shape[0])], axis=1)
                 for c in residues], axis=0))


def _qkv(h, B, S, rate, gq, gkv, wq, wk, wv, qn, kn):
    pending = h if isinstance(h, tuple) else None
    D = (pending[0] if pending else h).shape[1]
    L = S // rate
    hd = DIL_HEAD_DIM
    n = max(SUB_BLOCK, TOKEN_TILE // rate)
    tm = n * rate
    res_per_chunk = max(1, TOKEN_TILE // n)
    n_chunks = rate // res_per_chunk
    width = 2 * LANES
    ii = jnp.arange(width)
    seg = jnp.where((ii[:, None] // hd) == (ii[None, :] // hd), 1.0 / hd, 0.0).astype(BF16)
    row = lambda g: jnp.tile(g, D // hd).reshape(1, D)
    const = lambda shape: _resident(shape, lambda b, i, c: (0,) * len(shape))
    qk_spec = pl.BlockSpec((1, res_per_chunk, n, D), lambda b, i, c: (b, c, i, 0))
    qk_shape = jax.ShapeDtypeStruct((B, rate, L, D), BF16)
    rows = lambda width: pl.BlockSpec((tm, width), lambda b, i, c: (b * (S // tm) + i, 0))
    if pending:
        assert rate == 1
        h1, route, pairs = pending
        x_args = (h1, route, *pairs)
        x_specs = [rows(D), rows(LANES)] + [rows(pairs[0].shape[1])] * len(pairs)
        extra_specs, extra_shapes = [rows(D)], [jax.ShapeDtypeStruct((B * S, D), F32)]
    else:
        x_args, x_specs, extra_specs, extra_shapes = (h,), [rows(D)], [], []
    return pl.pallas_call(
        functools.partial(_qkv_kernel, rate=rate, n=n, n_chunks=n_chunks,
                          n_pending=len(pending[2]) if pending else 0),
        grid=(B, S // tm, n_chunks),
        in_specs=x_specs + [
                  const((1, D)), const((1, D)), const((D, D)), const((D, D)), const((D, D)),
                  const((1, D)), const((1, D)), const((width, width))],
        out_specs=[qk_spec, qk_spec,
                   pl.BlockSpec((1, res_per_chunk, D, n), lambda b, i, c: (b, c, 0, i))] + extra_specs,
        out_shape=[qk_shape, qk_shape, jax.ShapeDtypeStruct((B, rate, D, L), BF16)] + extra_shapes,
        scratch_shapes=[pltpu.VMEM((D // LANES, tm if rate > 1 else 8, LANES), F32)],
        compiler_params=_params("parallel", "parallel", "arbitrary"),
        name="qkv_rate%d" % rate,
    )(*x_args, gq.reshape(1, D), gkv.reshape(1, D), wq, wk, wv.T, row(qn) * (hd ** -0.5 * LOG2E), row(kn), seg)


def _attn_kernel(*refs, rate, with_prev):
    if not with_prev:
        q_ref, kc_ref, vc_ref, edge_ref, o_ref, lse_ref, os_ref = refs
    elif len(refs) == 10:
        q_ref, kc_ref, kp_ref, vc_ref, vp_ref, edge_ref, bias_ref, o_ref, lse_ref, os_ref = refs
    else:
        q_ref, kc_ref, kp_ref, vc_ref, vp_ref, edge_ref, o_ref, lse_ref, os_ref = refs
    Bk = SUB_BLOCK
    hd = DIL_HEAD_DIM
    n_pairs = DIL_HEADS // 2
    lane = lax.broadcasted_iota(jnp.int32, (Bk, LANES), 1)

    def block(load_q, load_k, load_vt, load_bias, store):
        out_t, lse_t = [], []
        for p in range(n_pairs):
            qp = load_q(p)
            zero = jnp.zeros_like(qp)
            q2 = jnp.concatenate([jnp.where(lane < hd, qp, zero), jnp.where(lane >= hd, qp, zero)], axis=0)
            s = lax.dot_general(load_k(p), q2, NT_DIMS, preferred_element_type=F32) + load_bias(p)
            m = jnp.max(s, axis=0, keepdims=True)
            pr = jnp.exp2(s - m)
            l = jnp.sum(pr, axis=0, keepdims=True)
            pb = pr.astype(BF16)
            vt = load_vt(p)
            out_t.append(jnp.dot(vt[:hd], pb[:, :Bk], preferred_element_type=F32) / l[:, :Bk])
            out_t.append(jnp.dot(vt[hd:], pb[:, Bk:], preferred_element_type=F32) / l[:, Bk:])
            lse = (m + jnp.log2(l)) * LN2
            lse_t += [lse[:, :Bk], lse[:, Bk:]]
        lse_t.append(jnp.zeros((LANES - DIL_HEADS, Bk), F32))
        store(jnp.concatenate(out_t, axis=0).T, jnp.concatenate(lse_t, axis=0).T)

    def pair_cols(p):
        return slice(p * LANES, (p + 1) * LANES)

    n_res, n_blocks = kc_ref.shape[1], kc_ref.shape[2] // Bk
    for c in range(n_res):
        for j in range(n_blocks):
            rows = slice(j * Bk, (j + 1) * Bk)
            if not with_prev:
                load_k = lambda p, c=c: kc_ref[0, c, :, pair_cols(p)]
                load_vt = lambda p, c=c: vc_ref[0, c, pair_cols(p), :]
                load_bias = lambda p: edge_ref[0, p]
            elif j == 0:
                load_k = lambda p, c=c: jnp.concatenate([kp_ref[0, c, :, pair_cols(p)],
                                                         kc_ref[0, c, :Bk, pair_cols(p)]], axis=0)
                load_vt = lambda p, c=c: jnp.concatenate([vp_ref[0, c, pair_cols(p), :],
                                                          vc_ref[0, c, pair_cols(p), :Bk]], axis=1)
                load_bias = lambda p: edge_ref[0, p]
            else:
                krows = slice((j - 1) * Bk, (j + 1) * Bk)
                load_k = lambda p, c=c, krows=krows: kc_ref[0, c, krows, pair_cols(p)]
                load_vt = lambda p, c=c, krows=krows: vc_ref[0, c, pair_cols(p), krows]
                load_bias = lambda p: bias_ref[p]

            if rate == 1:
                def store(o, lse, rows=rows):
                    o_ref[rows, :] = o.astype(o_ref.dtype)
                    lse_ref[rows, :] = lse
            else:
                def store(o, lse, out_rows=pl.ds(c + j * Bk * rate, Bk, stride=rate)):
                    for g in range(os_ref.shape[0]):
                        os_ref[g, out_rows, :] = o[:, pair_cols(g)]
                    lse_ref[out_rows, :] = lse

            block(lambda p, c=c, rows=rows: q_ref[0, c, rows, pair_cols(p)], load_k, load_vt, load_bias, store)
    if rate > 1:
        for g in range(os_ref.shape[0]):
            o_ref[:, pair_cols(g)] = os_ref[g].astype(o_ref.dtype)


def _t5_bucket(n):
    max_exact = NUM_BUCKETS // 2
    nf = jnp.maximum(n, max_exact).astype(F32)
    large = max_exact + (jnp.log(nf / max_exact) / math.log(MAX_DISTANCE / max_exact)
                         * (NUM_BUCKETS - max_exact)).astype(jnp.int32)
    large = jnp.minimum(large, NUM_BUCKETS - 1)
    return jnp.where(n < max_exact, n, large)


def _group_attention(q, k, vt, bias_table, rate, n_steps):
    B, _, L, D = q.shape
    S = L * rate
    Bk = SUB_BLOCK
    with_prev = L > Bk
    nk = 2 * Bk if with_prev else Bk
    n = max(Bk, ATTN_TILE // rate)
    tm = n * rate
    nt = S // tm
    n_pairs = DIL_HEADS // 2
    ql = jnp.arange(Bk, dtype=jnp.int32)[:, None]
    kl = jnp.arange(2 * Bk, dtype=jnp.int32)[None, :]
    steps = ql + Bk - kl
    bucket = _t5_bucket(jnp.maximum(steps, 0) * rate)
    buckets = jnp.arange(NUM_BUCKETS, dtype=jnp.int32)
    bias = jnp.sum(jnp.where(bucket[None, :, :, None] == buckets[:, None, None, None],
                             bias_table.astype(F32)[:, None, None, :], 0.0), axis=0)
    band = ((steps >= 0) & (steps <= n_steps))[:, :, None]
    first = (kl < Bk)[:, :, None]

    def layout(t):
        t = t[:, 2 * Bk - nk:, :].transpose(1, 2, 0)
        return t.reshape(nk, n_pairs, 2 * Bk).transpose(1, 0, 2)

    bias_in = layout(jnp.where(band, bias * LOG2E, NEG_INF))
    bias_first = layout(jnp.where(band & ~first, bias * LOG2E, NEG_INF))
    edge = jnp.stack([bias_first, bias_in])

    cur_qk = pl.BlockSpec((1, rate, n, D), lambda b, i: (b, 0, i, 0))
    cur_vt = pl.BlockSpec((1, rate, D, n), lambda b, i: (b, 0, 0, i))
    per_n = n // Bk
    prev_qk = pl.BlockSpec((1, rate, Bk, D), lambda b, i: (b, 0, jnp.maximum(i * per_n - 1, 0), 0))
    prev_vt = pl.BlockSpec((1, rate, D, Bk), lambda b, i: (b, 0, 0, jnp.maximum(i * per_n - 1, 0)))
    edge_spec = pl.BlockSpec((1, n_pairs, nk, 2 * Bk), lambda b, i: (jnp.minimum(i, 1), 0, 0, 0))
    if with_prev:
        in_specs = [cur_qk, cur_qk, prev_qk, cur_vt, prev_vt, edge_spec]
        args = (q, k, k, vt, vt, edge)
    else:
        in_specs = [cur_qk, cur_qk, cur_vt, edge_spec]
        args = (q, k, vt, edge)
    if with_prev and n > Bk:
        in_specs.append(_resident((n_pairs, nk, 2 * Bk), lambda b, i: (0, 0, 0)))
        args += (bias_in,)
    return pl.pallas_call(
        functools.partial(_attn_kernel, rate=rate, with_prev=with_prev),
        grid=(B, nt),
        in_specs=in_specs,
        out_specs=[pl.BlockSpec((tm, D), lambda b, i: (b * nt + i, 0)),
                   pl.BlockSpec((tm, LANES), lambda b, i: (b * nt + i, 0))],
        out_shape=[jax.ShapeDtypeStruct((B * S, D), BF16),
                   jax.ShapeDtypeStruct((B * S, LANES), F32)],
        scratch_shapes=[pltpu.VMEM((D // LANES, tm if rate > 1 else 8, LANES), F32)],
        compiler_params=_params("parallel", "parallel"),
        name="attn_rate%d" % rate,
    )(*args)


def kernel(x, ret_w_in, ret_w_out, kv_norm, w_kv, k_norm, dil_wq, q_norm, dil_wo, rel_bias,
           mixer_norm, ffn_norm, router_grp, router_grp_b, router_exp, router_exp_b,
           exp_gate, exp_up, exp_down):
    B, S, D = x.shape
    h = x.reshape(B * S, D)

    def moe_layer(layer, mixer, w_out, h, combine):
        h1, *hn, route, route_t, counts = _post(
            mixer, w_out.astype(BF16), h, ffn_norm[layer], router_grp[layer], router_grp_b[layer],
            router_exp[layer], router_exp_b[layer])
        return _moe(h1, hn, route, route_t, counts, layer, exp_gate, exp_up, exp_down, combine)

    h = moe_layer(0, (mixer_norm[0], ret_w_in[0].astype(BF16), S), ret_w_out[0], h, combine=False)

    G = len(DIL_RATES)
    gd = DIL_HEADS * DIL_HEAD_DIM
    outs, lses = [], []
    for g in range(G):
        cq = slice(g * gd, (g + 1) * gd)
        cv = slice(G * gd + g * gd, G * gd + (g + 1) * gd)
        q, k, vt, *combined = _qkv(h, B, S, DIL_RATES[g], mixer_norm[1], kv_norm,
                                   dil_wq[0][:, cq].astype(BF16), w_kv[:, cq].astype(BF16),
                                   w_kv[:, cv].astype(BF16), q_norm[0][g], k_norm[g])
        if combined:
            h, = combined
        o, lse = _group_attention(q, k, vt, rel_bias[:, g * DIL_HEADS:(g + 1) * DIL_HEADS],
                                  DIL_RATES[g], DIL_WINDOWS[g] // DIL_RATES[g])
        outs.append(o)
        lses.append(lse)
    h = moe_layer(1, (tuple(outs), tuple(lses)), dil_wo[0], h, combine=True)
    return h.reshape(B, S, D)
```

```python
import functools
import math

import jax
import jax.numpy as jnp
from jax import lax
from jax.experimental import pallas as pl
from jax.experimental.pallas import tpu as pltpu
from jax.experimental.pallas import tpu_sc as plsc

F32 = jnp.float32
BF16 = jnp.bfloat16

EPS = 1e-6
NEG_INF = -1e30

RET_HEADS = 4
RET_CHUNK = 256
ROPE_BASE = 10000.0

DIL_WINDOWS = (128, 512, 2048)
DIL_RATES = (1, 4, 16)
DIL_HEADS = 16
DIL_HEAD_DIM = 64
SUB_BLOCK = 128
NUM_BUCKETS = 32
MAX_DISTANCE = 2048

MOE_GROUPS = 4
EXPERTS_PER_GROUP = 8
N_EXPERTS = MOE_GROUPS * EXPERTS_PER_GROUP
MOE_BLOCK = 512

LANES = 128
ROW_TILE = 1024
TOKEN_TILE = 512
ATTN_TILE = 2048
VMEM_LIMIT = 56 * 1024 * 1024

NT_DIMS = (((1,), (1,)), ((), ()))
LOG2E = math.log2(math.e)
LN2 = math.log(2.0)


def _params(*sem):
    return pltpu.CompilerParams(dimension_semantics=sem, vmem_limit_bytes=VMEM_LIMIT)


def _resident(shape, index_map):
    return pl.BlockSpec(shape, index_map, pipeline_mode=pl.Buffered(1))


def _rms(x):
    return x * lax.rsqrt(jnp.mean(x * x, axis=-1, keepdims=True) + EPS)


def _retention_tile(x_ref, g_ref, w_ref, cos_ref, sin_ref, din_ref, xi_ref, zeta_ref, cd_ref, state_ref,
                    first_tile):
    C = RET_CHUNK
    H, dk, dv = state_ref.shape
    half = dk // 2
    ts = x_ref.shape[0]
    k_scale = dk ** -0.5

    @pl.when(first_tile)
    def _():
        state_ref[...] = jnp.zeros_like(state_ref)

    xn = (_rms(x_ref[...]) * g_ref[...]).astype(BF16)
    cos, sin = cos_ref[...], sin_ref[...]

    def proj(c0, width):
        return jnp.dot(xn, w_ref[:, c0:c0 + width], preferred_element_type=F32)

    def rot(t):
        t1, t2 = t[:, :half], t[:, half:]
        return jnp.concatenate([t1 * cos - t2 * sin, t1 * sin + t2 * cos], axis=1)

    heads = []
    for hh in range(H):
        q = rot(proj(hh * dk, dk)).astype(BF16)
        k = rot(proj(H * dk + hh * dk, dk)) * k_scale
        kb = k.astype(BF16)
        v = proj(2 * H * dk + hh * dv, dv).astype(BF16)
        gate = proj(2 * H * dk + H * dv + hh * dv, dv)
        gate = gate * jax.nn.sigmoid(gate)
        chunks = []
        for c in range(ts // C):
            rows = slice(c * C, (c + 1) * C)
            s = lax.dot_general(q[rows], kb[rows], NT_DIMS, preferred_element_type=F32) * din_ref[hh]
            inner = jnp.dot(s.astype(BF16), v[rows], preferred_element_type=F32)
            state = state_ref[hh]
            cross = jnp.dot(q[rows], state.astype(BF16), preferred_element_type=F32) * xi_ref[hh]
            kz_t = (k[rows] * zeta_ref[hh]).T.astype(BF16)
            state_ref[hh] = state * cd_ref[hh] + jnp.dot(kz_t, v[rows], preferred_element_type=F32)
            chunks.append((gate[rows] * _rms(inner + cross)).astype(BF16))
        heads.append(jnp.concatenate(chunks, axis=0))
    return jnp.concatenate(heads, axis=1)


def _retention_operands(gain, w, S, ts):
    H, C = RET_HEADS, RET_CHUNK
    D = w.shape[0]
    dk = D // H
    half = dk // 2
    nt = S // ts
    pos = jnp.arange(S, dtype=F32)
    inv = 1.0 / (ROPE_BASE ** jnp.linspace(0.0, 1.0, half, dtype=F32))
    ang = pos[:, None] * inv[None, :]
    log_g = jnp.log(1.0 - 2.0 ** (-5.0 - jnp.arange(H, dtype=F32)))
    idx = jnp.arange(C, dtype=F32)
    diff = idx[:, None] - idx[None, :]
    d_in = jnp.where(diff >= 0, jnp.exp(log_g[:, None, None] * jnp.maximum(diff, 0.0)), 0.0)
    xi = jnp.exp(log_g[:, None] * (idx + 1.0))[:, :, None]
    zeta = jnp.exp(log_g[:, None] * (C - 1.0 - idx))[:, :, None]
    chunk_decay = jnp.exp(log_g * C)[:, None, None]
    table = pl.BlockSpec((ts, half), lambda i: (i % nt, 0))
    per_head = lambda shape: _resident((H,) + shape, lambda i: (0, 0, 0))
    args = (gain.reshape(1, D), w, jnp.cos(ang), jnp.sin(ang), d_in, xi, zeta, chunk_decay)
    specs = [_resident((1, D), lambda i: (0, 0)), _resident(w.shape, lambda i: (0, 0)), table, table,
             per_head((C, C)), per_head((C, 1)), per_head((C, 1)), per_head((1, 1))]
    state = pltpu.VMEM((H, dk, 2 * dk), F32)
    return args, specs, state


ROUTE_E1, ROUTE_E2, ROUTE_G1, ROUTE_G2, ROUTE_R1, ROUTE_R2 = range(6)
ROUTE_ROWS = 8
ROUTE_PIECES = 2
ROUTER_EXP_LANE0 = MOE_GROUPS
HI16 = 0xFFFF0000


def _pack_bf16_pair(a, b):
    ua = lax.bitcast_convert_type(a.astype(BF16).astype(F32), jnp.uint32)
    ub = lax.bitcast_convert_type(b.astype(BF16).astype(F32), jnp.uint32)
    return ua | (ub >> 16)


def _unpack_bf16_pair(w):
    a = lax.bitcast_convert_type(w & jnp.uint32(HI16), F32)
    b = lax.bitcast_convert_type(w << 16, F32)
    return a, b


ROW_PARTS = 2


def _part_cols(D, part):
    w = D // 2 // ROW_PARTS
    return slice(part * w, (part + 1) * w), slice(D // 2 + part * w, D // 2 + (part + 1) * w)


def _pack_part(x, part):
    hi, lo = _part_cols(x.shape[1], part)
    return _pack_bf16_pair(x[:, hi], x[:, lo])


def _merge_groups(o_refs, l_refs, ex_ref):
    ex = ex_ref[...]

    def expand(w):
        return jnp.dot(w.astype(BF16), ex, preferred_element_type=F32)

    lses = [l_ref[...] for l_ref in l_refs]
    top = functools.reduce(jnp.maximum, lses)
    ws = [jnp.exp(l - top) for l in lses]
    den = sum(ws)
    last = o_refs[-1][...].astype(F32)
    return (last + sum(expand(w / den) * (o_ref[...].astype(F32) - last)
                       for w, o_ref in zip(ws[:-1], o_refs[:-1]))).astype(BF16)


def _post_kernel(*refs, n_groups, seq_tiles):
    n_a = 2 * n_groups + 1 if n_groups else 8
    a_refs = refs[:n_a]
    (w_ref, h_ref, g_ref, wr_ref, br_ref, tri_ref,
     h1_ref, hn0_ref, hn1_ref, route_ref, route_t_ref, cnt_ref, carry_ref) = refs[n_a:n_a + 13]
    hn_refs = (hn0_ref, hn1_ref)
    if n_groups:
        a = _merge_groups(a_refs[:n_groups], a_refs[n_groups:2 * n_groups], a_refs[-1])
    else:
        a = _retention_tile(h_ref, *a_refs, refs[-1], pl.program_id(0) % seq_tiles == 0)
    @pl.when(pl.program_id(0) == 0)
    def _():
        carry_ref[...] = jnp.zeros_like(carry_ref)

    half = tri_ref.shape[0]
    n_pieces = h_ref.shape[0] // half
    D = h_ref.shape[1]
    lane = lax.broadcasted_iota(jnp.int32, (half, LANES), 1).astype(F32)
    ninf = -jnp.inf

    def first_argmax(vals):
        top = jnp.max(vals, axis=1, keepdims=True)
        where = jnp.min(jnp.where(vals == top, lane, float(LANES)), axis=1, keepdims=True)
        return top, where

    h1_ref[...] = h_ref[...] + jnp.dot(a, w_ref[...], preferred_element_type=F32)

    def route_half(rows):
        hn = _rms(h1_ref[rows, :]) * g_ref[...]
        for part, ref in enumerate(hn_refs):
            ref[rows, :] = _pack_part(hn, part)
        hi = hn.astype(BF16)
        lo = (hn - hi.astype(F32)).astype(BF16)
        both = jnp.dot(hi, wr_ref[...], preferred_element_type=F32)
        logits = (both[:, :LANES] + both[:, LANES:]
                  + jnp.dot(lo, wr_ref[:, :LANES], preferred_element_type=F32) + br_ref[...])

        is_grp = lane < MOE_GROUPS
        lg = jnp.where(is_grp, logits, ninf)
        mg, grp = first_argmax(lg)
        p_grp = 1.0 / jnp.sum(jnp.where(is_grp, jnp.exp(lg - mg), 0.0), axis=1, keepdims=True)

        e_lane = lane - ROUTER_EXP_LANE0
        in_grp = (e_lane < N_EXPERTS) & (jnp.floor(e_lane * (1.0 / EXPERTS_PER_GROUP)) == grp)
        le = jnp.where(in_grp, logits, ninf)
        v1, i1 = first_argmax(le)
        le2 = jnp.where(lane == i1, ninf, le)
        v2, i2 = first_argmax(le2)
        e = jnp.exp(v2 - v1)
        hit1 = lane == (i1 - ROUTER_EXP_LANE0)
        hit2 = lane == (i2 - ROUTER_EXP_LANE0)
        onehot = jnp.where(hit1 | hit2, 1.0, 0.0)
        earlier = jnp.dot(tri_ref[...], onehot.astype(BF16), preferred_element_type=F32)
        return dict(e1=i1 - ROUTER_EXP_LANE0, e2=i2 - ROUTER_EXP_LANE0, g1=p_grp / (1.0 + e),
                    g2=p_grp * e / (1.0 + e), hit1=hit1, hit2=hit2, earlier=earlier,
                    count=jnp.sum(onehot, axis=0, keepdims=True))

    halves = [route_half(slice(j * half, (j + 1) * half)) for j in range(n_pieces)]
    carry = carry_ref[...]
    for j, r in enumerate(halves):
        before = carry + r["earlier"]
        r1 = jnp.sum(jnp.where(r["hit1"], before, 0.0), axis=1, keepdims=True)
        r2 = jnp.sum(jnp.where(r["hit2"], before, 0.0), axis=1, keepdims=True)
        carry = carry + r["count"]
        route = jnp.zeros((half, LANES), F32)
        for slot, val in ((ROUTE_E1, r["e1"]), (ROUTE_E2, r["e2"]), (ROUTE_G1, r["g1"]),
                          (ROUTE_G2, r["g2"]), (ROUTE_R1, r1), (ROUTE_R2, r2)):
            route = jnp.where(lane == slot, val, route)
        route_ref[j * half:(j + 1) * half, :] = route
        route_t_ref[:, j * half:(j + 1) * half] = route.T[:ROUTE_ROWS]
    carry_ref[...] = carry
    cnt_ref[...] = carry


def _post(mixer, w, h, gain, w_grp, b_grp, w_exp, b_exp):
    T, D = h.shape
    K = w.shape[0]
    row_block = lambda width: pl.BlockSpec((tm, width), lambda i: (i, 0))
    scratch = [pltpu.VMEM((1, LANES), F32)]
    if len(mixer) == 2:
        outs, lses = mixer
        n_groups, seq_tiles = len(outs), 1
        tm = min(TOKEN_TILE, T)
        ex = jnp.where(jnp.arange(LANES)[:, None] == (jnp.arange(K)[None, :] // DIL_HEAD_DIM), 1.0, 0.0).astype(BF16)
        a_args = (*outs, *lses, ex)
        a_specs = [row_block(K)] * n_groups + [row_block(LANES)] * n_groups + [_resident((LANES, K), lambda i: (0, 0))]
    else:
        mix_gain, w_in, S = mixer
        tm = min(TOKEN_TILE, S)
        n_groups, seq_tiles = 0, S // tm
        a_args, a_specs, state = _retention_operands(mix_gain, w_in, S, tm)
        scratch.append(state)
    n_r = MOE_GROUPS + N_EXPERTS
    wr = jnp.zeros((D, LANES), F32).at[:, :n_r].set(jnp.concatenate([w_grp, w_exp], axis=1))
    wr_hi = wr.astype(BF16)
    wr_lo = (wr - wr_hi.astype(F32)).astype(BF16)
    wr2 = jnp.concatenate([wr_hi, wr_lo], axis=1)
    br = jnp.zeros((1, LANES), F32).at[0, :n_r].set(jnp.concatenate([b_grp, b_exp]))
    piece = tm // ROUTE_PIECES
    tri = jnp.tril(jnp.ones((piece, piece), BF16), k=-1)
    return pl.pallas_call(
        functools.partial(_post_kernel, n_groups=n_groups, seq_tiles=seq_tiles),
        grid=(T // tm,),
        in_specs=a_specs + [
                  _resident((K, D), lambda i: (0, 0)),
                  pl.BlockSpec((tm, D), lambda i: (i, 0)),
                  _resident((1, D), lambda i: (0, 0)),
                  _resident((D, 2 * LANES), lambda i: (0, 0)),
                  _resident((1, LANES), lambda i: (0, 0)),
                  _resident((piece, piece), lambda i: (0, 0))],
        out_specs=[pl.BlockSpec((tm, D), lambda i: (i, 0))]
                  + [pl.BlockSpec((tm, D // 2 // ROW_PARTS), lambda i: (i, 0))] * ROW_PARTS
                  + [pl.BlockSpec((tm, LANES), lambda i: (i, 0)),
                   pl.BlockSpec((ROUTE_ROWS, tm), lambda i: (0, i)),
                   pl.BlockSpec((1, LANES), lambda i: (0, 0))],
        out_shape=[jax.ShapeDtypeStruct((T, D), F32)]
                  + [jax.ShapeDtypeStruct((T, D // 2 // ROW_PARTS), jnp.uint32)] * ROW_PARTS
                  + [jax.ShapeDtypeStruct((T, LANES), F32),
                   jax.ShapeDtypeStruct((ROUTE_ROWS, T), F32),
                   jax.ShapeDtypeStruct((1, LANES), F32)],
        scratch_shapes=scratch,
        compiler_params=_params("arbitrary"),
        name="post_mixer",
    )(*a_args, w, h, gain.reshape(1, D), wr2, br, tri)


SC_WINDOW = 128


def _sc_mesh():
    return plsc.VectorSubcoreMesh(core_axis_name="core", subcore_axis_name="subcore")


def _sc_window_specs(W):
    rows = pl.BlockSpec((SC_WINDOW, W), lambda i: (i, 0))
    idx = pl.BlockSpec((1, SC_WINDOW), lambda i: (0, i))
    return rows, idx


def _sc_pipeline(body, n_rows, in_specs, out_specs):
    return pltpu.emit_pipeline(body, grid=(n_rows // SC_WINDOW,), in_specs=in_specs, out_specs=out_specs,
                               core_axis_name=("core", "subcore"), dimension_semantics=(pltpu.PARALLEL,))


def _dispatch(xs, dest, pad_idx, P):
    T, W = xs[0].shape
    n_pad = pad_idx.shape[1]
    n = len(xs)
    rows, idx = _sc_window_specs(W)
    zero_rows = pl.BlockSpec((SC_WINDOW, W), lambda i: (0, 0))
    out = jax.ShapeDtypeStruct((P, W), xs[0].dtype)

    @functools.partial(pl.kernel, out_type=(out,) * n, mesh=_sc_mesh(), scratch_types=[], name="dispatch")
    def scatter(*refs):
        x_hbm, (d0_hbm, d1_hbm, z_hbm, p_hbm), o_hbm = refs[:n], refs[n:n + 4], refs[n + 4:]
        for x, o in zip(x_hbm, o_hbm):
            def put_pair(x_vmem, i0_vmem, i1_vmem, o=o):
                pltpu.sync_copy(x_vmem, o.at[i0_vmem.at[0]])
                pltpu.sync_copy(x_vmem, o.at[i1_vmem.at[0]])

            def put(x_vmem, i_vmem, o=o):
                pltpu.sync_copy(x_vmem, o.at[i_vmem.at[0]])

            _sc_pipeline(put_pair, T, [rows, idx, idx], [])(x, d0_hbm, d1_hbm)
            _sc_pipeline(put, n_pad, [zero_rows, idx], [])(z_hbm, p_hbm)

    return scatter(*xs, dest[0:1], dest[1:2], jnp.zeros((SC_WINDOW, W), xs[0].dtype), pad_idx)


WEIGHT_LEADS = (3, 2, 1)
EXPERT_SLOTS = max(WEIGHT_LEADS) + 1


def _expert_kernel(blk_exp_ref, slot_ref, n_used_ref, *refs):
    x_refs = refs[:ROW_PARTS]
    w_refs = refs[ROW_PARTS:ROW_PARTS + 3]
    o_refs = refs[ROW_PARTS + 3:2 * ROW_PARTS + 3]
    w_slots = refs[2 * ROW_PARTS + 3:]
    lead = max(WEIGHT_LEADS)
    n_used = n_used_ref[0]
    i = pl.program_id(0) - lead
    D = w_slots[0].shape[1]

    for w_ref, w_s, ahead in zip(w_refs, w_slots, WEIGHT_LEADS):
        j = i + ahead
        jc = jnp.clip(j, 0, n_used - 1)
        arrived = (j >= 0) & (j < n_used) & ((j == 0) | (blk_exp_ref[jc] != blk_exp_ref[jnp.maximum(jc - 1, 0)]))

        @pl.when(arrived)
        def _(w_ref=w_ref, w_s=w_s, jc=jc):
            w_s[slot_ref[jc]] = w_ref[0, 0].astype(BF16)

    @pl.when((i >= 0) & (i < n_used))
    def _():
        slot = slot_ref[jnp.clip(i, 0, n_used - 1)]
        wg_s, wu_s, wd_s = w_slots
        pieces = []
        for part, x_ref in enumerate(x_refs):
            for cols, val in zip(_part_cols(D, part), _unpack_bf16_pair(x_ref[...])):
                pieces.append((cols, val.astype(BF16)))

        def up(w_s):
            return sum(jnp.dot(val, w_s[slot, cols, :], preferred_element_type=F32) for cols, val in pieces)

        g = up(wg_s)
        hid = (g * jax.nn.sigmoid(g) * up(wu_s)).astype(BF16)
        y = jnp.dot(hid, wd_s[slot], preferred_element_type=F32)
        for part, o_ref in enumerate(o_refs):
            o_ref[...] = _pack_part(y, part)

    @pl.when(i >= n_used)
    def _():
        for o_ref in o_refs:
            o_ref[...] = jnp.zeros_like(o_ref)


def _experts(xs, blk_exp, n_used, layer, w_gate, w_up, w_down):
    P, W = xs[0].shape
    D = 2 * W * ROW_PARTS
    FF = w_gate.shape[3]
    nblk = P // MOE_BLOCK
    lead = max(WEIGHT_LEADS)
    changes = jnp.concatenate([jnp.zeros((1,), jnp.int32), (blk_exp[1:] != blk_exp[:-1]).astype(jnp.int32)])
    slot = jnp.cumsum(changes) % EXPERT_SLOTS

    def x_map(g, be, sl, nu):
        return (jnp.clip(g - lead, 0, nu[0] - 1), 0)

    def o_map(g, be, sl, nu):
        return (jnp.maximum(g - lead, 0), 0)

    def w_map(ahead):
        return lambda g, be, sl, nu: (layer, be[jnp.clip(g - lead + ahead, 0, nu[0] - 1)], 0, 0)

    wg_spec, wu_spec, wd_spec = (pl.BlockSpec((1, 1) + shape, w_map(ahead))
                                 for shape, ahead in zip(((D, FF), (D, FF), (FF, D)), WEIGHT_LEADS))
    return pl.pallas_call(
        _expert_kernel,
        grid_spec=pltpu.PrefetchScalarGridSpec(
            num_scalar_prefetch=3,
            grid=(nblk + lead,),
            in_specs=[pl.BlockSpec((MOE_BLOCK, W), x_map)] * ROW_PARTS + [wg_spec, wu_spec, wd_spec],
            out_specs=[pl.BlockSpec((MOE_BLOCK, W), o_map)] * ROW_PARTS,
            scratch_shapes=[pltpu.VMEM((EXPERT_SLOTS, D, FF), BF16), pltpu.VMEM((EXPERT_SLOTS, D, FF), BF16),
                            pltpu.VMEM((EXPERT_SLOTS, FF, D), BF16)]),
        out_shape=[jax.ShapeDtypeStruct((P, W), jnp.uint32)] * ROW_PARTS,
        compiler_params=_params("arbitrary"),
        name="experts",
    )(blk_exp, slot, n_used, *xs, w_gate, w_up, w_down)


def _gather_pairs(ys, dest):
    W = ys[0].shape[1]
    T = dest.shape[1]
    n = len(ys)
    rows, idx = _sc_window_specs(W)
    out = jax.ShapeDtypeStruct((T, W), ys[0].dtype)

    @functools.partial(pl.kernel, out_type=(out,) * (2 * n), mesh=_sc_mesh(), scratch_types=[],
                       name="gather_pairs")
    def gather(*refs):
        y_hbm, d_hbm, o_hbm = refs[:n], refs[n:n + 2], refs[n + 2:]
        for slot, d in enumerate(d_hbm):
            for y, o in zip(y_hbm, o_hbm[slot * n:(slot + 1) * n]):
                def get(i_vmem, o_vmem, y=y):
                    pltpu.sync_copy(y.at[i_vmem.at[0]], o_vmem)

                _sc_pipeline(get, T, [idx], [rows])(d, o)

    return gather(*ys, dest[0:1], dest[1:2])


def _combined_rows(h_ref, route_ref, y_refs):
    D = h_ref.shape[1]
    route = route_ref[...]
    gates = (route[:, ROUTE_G1:ROUTE_G1 + 1], route[:, ROUTE_G2:ROUTE_G2 + 1])
    pieces = {}
    for part in range(ROW_PARTS):
        slots = [_unpack_bf16_pair(y_refs[slot * ROW_PARTS + part][...]) for slot in range(2)]
        for half, cols in enumerate(_part_cols(D, part)):
            pieces[cols.start] = h_ref[:, cols] + (gates[0] * slots[0][half] + gates[1] * slots[1][half])
    return jnp.concatenate([pieces[c] for c in sorted(pieces)], axis=1)


def _combine_kernel(h_ref, route_ref, *refs):
    refs[-1][...] = _combined_rows(h_ref, route_ref, refs[:-1])


def _combine(h, route, pairs):
    T, D = h.shape
    W = pairs[0].shape[1]
    tm = min(ROW_TILE, T)
    return pl.pallas_call(
        _combine_kernel,
        grid=(T // tm,),
        in_specs=[pl.BlockSpec((tm, D), lambda i: (i, 0)), pl.BlockSpec((tm, LANES), lambda i: (i, 0))]
                 + [pl.BlockSpec((tm, W), lambda i: (i, 0))] * len(pairs),
        out_specs=pl.BlockSpec((tm, D), lambda i: (i, 0)),
        out_shape=jax.ShapeDtypeStruct((T, D), F32),
        compiler_params=_params("parallel"),
        name="combine",
    )(h, route, *pairs)


def _moe(h1, hn, route, route_t, counts, layer, w_gate, w_up, w_down, combine):
    T, D = h1.shape
    A = 2 * T
    nblk = -(-A // MOE_BLOCK) + N_EXPERTS
    P = nblk * MOE_BLOCK
    eid = route_t[ROUTE_E1:ROUTE_E2 + 1].astype(jnp.int32)
    rank = route_t[ROUTE_R1:ROUTE_R2 + 1].astype(jnp.int32)
    cnt = counts[0, :N_EXPERTS].astype(jnp.int32)
    padded = (cnt + MOE_BLOCK - 1) // MOE_BLOCK * MOE_BLOCK
    pends = jnp.cumsum(padded)
    pstarts = pends - padded
    experts = jnp.arange(N_EXPERTS, dtype=jnp.int32)
    start_of = jnp.sum(jnp.where(eid[:, None, :] == experts[None, :, None], pstarts[None, :, None], 0), axis=1)
    dest = start_of + rank
    blk_start = jnp.arange(nblk, dtype=jnp.int32) * MOE_BLOCK
    blk_exp = jnp.minimum(jnp.sum((pends[None, :] <= blk_start[:, None]).astype(jnp.int32), axis=1),
                          N_EXPERTS - 1)
    n_used = pends[-1:] // MOE_BLOCK

    gap_start = jnp.concatenate([pstarts + cnt, pends[-1:]])
    gap_len = jnp.concatenate([padded - cnt, P - pends[-1:]])
    gap_end = jnp.cumsum(gap_len)
    j = jnp.arange(P - A, dtype=jnp.int32)
    gap_of = jnp.sum((gap_end[None, :] <= j[:, None]).astype(jnp.int32), axis=1)
    sel = gap_of[:, None] == jnp.arange(N_EXPERTS + 1, dtype=jnp.int32)[None, :]
    pad_idx = jnp.sum(jnp.where(sel, (gap_start - (gap_end - gap_len))[None, :] + j[:, None], 0), axis=1)

    xs = _dispatch(hn, dest, pad_idx.reshape(1, P - A), P)
    ys = _experts(xs, blk_exp, n_used, layer, w_gate, w_up, w_down)
    pairs = _gather_pairs(ys, dest)
    return _combine(h1, route, pairs) if combine else (h1, route, pairs)


def _qkv_kernel(*refs, rate, n, n_chunks, n_pending):
    n_x = 2 + n_pending if n_pending else 1
    x_refs = refs[:n_x]
    gq_ref, gkv_ref, wq_ref, wk_ref, wvt_ref, qn_ref, kn_ref, seg_ref, q_ref, k_ref, vt_ref = refs[n_x:n_x + 11]
    xs_ref = refs[-1]
    seg = seg_ref[...]
    width = seg.shape[0]
    res_per_chunk = q_ref.shape[1]
    chunk = pl.program_id(2)

    def head_norm(t, gain):
        cols = []
        for j in range(t.shape[1] // width):
            tj = t[:, j * width:(j + 1) * width]
            ms = jnp.dot((tj * tj).astype(BF16), seg, preferred_element_type=F32)
            cols.append(tj * lax.rsqrt(ms + EPS))
        return jnp.concatenate(cols, axis=1) * gain

    def project(x):
        y = _rms(x)
        xq = (y * gq_ref[...]).astype(BF16)
        xkv = (y * gkv_ref[...]).astype(BF16)
        q = head_norm(jnp.dot(xq, wq_ref[...], preferred_element_type=F32), qn_ref[...]).astype(q_ref.dtype)
        k = head_norm(jnp.dot(xkv, wk_ref[...], preferred_element_type=F32), kn_ref[...]).astype(k_ref.dtype)
        vt = lax.dot_general(wvt_ref[...], xkv, NT_DIMS, preferred_element_type=F32).astype(vt_ref.dtype)
        for j in range(res_per_chunk):
            q_ref[0, j] = q[j * n:(j + 1) * n]
            k_ref[0, j] = k[j * n:(j + 1) * n]
            vt_ref[0, j] = vt[:, j * n:(j + 1) * n]

    if n_pending:
        x = _combined_rows(x_refs[0], x_refs[1], x_refs[2:])
        refs[-2][...] = x
        project(x)
        return
    x_ref, = x_refs
    if rate == 1:
        project(x_ref[...])
        return

    @pl.when(chunk == 0)
    def _():
        for j in range(xs_ref.shape[0]):
            xs_ref[j] = x_ref[:, j * LANES:(j + 1) * LANES]

    for ch in range(n_chunks):
        @pl.when(chunk == ch)
        def _(ch=ch):
            residues = range(ch * res_per_chunk, (ch + 1) * res_per_chunk)
            project(jnp.concatenate(
                [jnp.concatenate([xs_ref[j, pl.ds(c, n, stride=rate), :] for j in range(xs_ref.shape[0])], axis=1)
                 for c in residues], axis=0))


def _qkv(h, B, S, rate, gq, gkv, wq, wk, wv, qn, kn):
    pending = h if isinstance(h, tuple) else None
    D = (pending[0] if pending else h).shape[1]
    L = S // rate
    hd = DIL_HEAD_DIM
    n = max(SUB_BLOCK, TOKEN_TILE // rate)
    tm = n * rate
    res_per_chunk = max(1, TOKEN_TILE // n)
    n_chunks = rate // res_per_chunk
    width = 2 * LANES
    ii = jnp.arange(width)
    seg = jnp.where((ii[:, None] // hd) == (ii[None, :] // hd), 1.0 / hd, 0.0).astype(BF16)
    row = lambda g: jnp.tile(g, D // hd).reshape(1, D)
    const = lambda shape: _resident(shape, lambda b, i, c: (0,) * len(shape))
    qk_spec = pl.BlockSpec((1, res_per_chunk, n, D), lambda b, i, c: (b, c, i, 0))
    qk_shape = jax.ShapeDtypeStruct((B, rate, L, D), BF16)
    rows = lambda width: pl.BlockSpec((tm, width), lambda b, i, c: (b * (S // tm) + i, 0))
    if pending:
        assert rate == 1
        h1, route, pairs = pending
        x_args = (h1, route, *pairs)
        x_specs = [rows(D), rows(LANES)] + [rows(pairs[0].shape[1])] * len(pairs)
        extra_specs, extra_shapes = [rows(D)], [jax.ShapeDtypeStruct((B * S, D), F32)]
    else:
        x_args, x_specs, extra_specs, extra_shapes = (h,), [rows(D)], [], []
    return pl.pallas_call(
        functools.partial(_qkv_kernel, rate=rate, n=n, n_chunks=n_chunks,
                          n_pending=len(pending[2]) if pending else 0),
        grid=(B, S // tm, n_chunks),
        in_specs=x_specs + [
                  const((1, D)), const((1, D)), const((D, D)), const((D, D)), const((D, D)),
                  const((1, D)), const((1, D)), const((width, width))],
        out_specs=[qk_spec, qk_spec,
                   pl.BlockSpec((1, res_per_chunk, D, n), lambda b, i, c: (b, c, 0, i))] + extra_specs,
        out_shape=[qk_shape, qk_shape, jax.ShapeDtypeStruct((B, rate, D, L), BF16)] + extra_shapes,
        scratch_shapes=[pltpu.VMEM((D // LANES, tm if rate > 1 else 8, LANES), F32)],
        compiler_params=_params("parallel", "parallel", "arbitrary"),
        name="qkv_rate%d" % rate,
    )(*x_args, gq.reshape(1, D), gkv.reshape(1, D), wq, wk, wv.T, row(qn) * (hd ** -0.5 * LOG2E), row(kn), seg)


def _attn_kernel(*refs, rate, with_prev):
    if not with_prev:
        q_ref, kc_ref, vc_ref, edge_ref, o_ref, lse_ref, os_ref = refs
    elif len(refs) == 10:
        q_ref, kc_ref, kp_ref, vc_ref, vp_ref, edge_ref, bias_ref, o_ref, lse_ref, os_ref = refs
    else:
        q_ref, kc_ref, kp_ref, vc_ref, vp_ref, edge_ref, o_ref, lse_ref, os_ref = refs
    Bk = SUB_BLOCK
    hd = DIL_HEAD_DIM
    n_pairs = DIL_HEADS // 2
    lane = lax.broadcasted_iota(jnp.int32, (Bk, LANES), 1)

    def block(load_q, load_k, load_vt, load_bias, store):
        out_t, lse_t = [], []
        for p in range(n_pairs):
            qp = load_q(p)
            zero = jnp.zeros_like(qp)
            q2 = jnp.concatenate([jnp.where(lane < hd, qp, zero), jnp.where(lane >= hd, qp, zero)], axis=0)
            s = lax.dot_general(load_k(p), q2, NT_DIMS, preferred_element_type=F32) + load_bias(p)
            m = jnp.max(s, axis=0, keepdims=True)
            pr = jnp.exp2(s - m)
            l = jnp.sum(pr, axis=0, keepdims=True)
            pb = pr.astype(BF16)
            vt = load_vt(p)
            out_t.append(jnp.dot(vt[:hd], pb[:, :Bk], preferred_element_type=F32) / l[:, :Bk])
            out_t.append(jnp.dot(vt[hd:], pb[:, Bk:], preferred_element_type=F32) / l[:, Bk:])
            lse = (m + jnp.log2(l)) * LN2
            lse_t += [lse[:, :Bk], lse[:, Bk:]]
        lse_t.append(jnp.zeros((LANES - DIL_HEADS, Bk), F32))
        store(jnp.concatenate(out_t, axis=0).T, jnp.concatenate(lse_t, axis=0).T)

    def pair_cols(p):
        return slice(p * LANES, (p + 1) * LANES)

    n_res, n_blocks = kc_ref.shape[1], kc_ref.shape[2] // Bk
    for c in range(n_res):
        for j in range(n_blocks):
            rows = slice(j * Bk, (j + 1) * Bk)
            if not with_prev:
                load_k = lambda p, c=c: kc_ref[0, c, :, pair_cols(p)]
                load_vt = lambda p, c=c: vc_ref[0, c, pair_cols(p), :]
                load_bias = lambda p: edge_ref[0, p]
            elif j == 0:
                load_k = lambda p, c=c: jnp.concatenate([kp_ref[0, c, :, pair_cols(p)],
                                                         kc_ref[0, c, :Bk, pair_cols(p)]], axis=0)
                load_vt = lambda p, c=c: jnp.concatenate([vp_ref[0, c, pair_cols(p), :],
                                                          vc_ref[0, c, pair_cols(p), :Bk]], axis=1)
                load_bias = lambda p: edge_ref[0, p]
            else:
                krows = slice((j - 1) * Bk, (j + 1) * Bk)
                load_k = lambda p, c=c, krows=krows: kc_ref[0, c, krows, pair_cols(p)]
                load_vt = lambda p, c=c, krows=krows: vc_ref[0, c, pair_cols(p), krows]
                load_bias = lambda p: bias_ref[p]

            if rate == 1:
                def store(o, lse, rows=rows):
                    o_ref[rows, :] = o.astype(o_ref.dtype)
                    lse_ref[rows, :] = lse
            else:
                def store(o, lse, out_rows=pl.ds(c + j * Bk * rate, Bk, stride=rate)):
                    for g in range(os_ref.shape[0]):
                        os_ref[g, out_rows, :] = o[:, pair_cols(g)]
                    lse_ref[out_rows, :] = lse

            block(lambda p, c=c, rows=rows: q_ref[0, c, rows, pair_cols(p)], load_k, load_vt, load_bias, store)
    if rate > 1:
        for g in range(os_ref.shape[0]):
            o_ref[:, pair_cols(g)] = os_ref[g].astype(o_ref.dtype)


def _t5_bucket(n):
    max_exact = NUM_BUCKETS // 2
    nf = jnp.maximum(n, max_exact).astype(F32)
    large = max_exact + (jnp.log(nf / max_exact) / math.log(MAX_DISTANCE / max_exact)
                         * (NUM_BUCKETS - max_exact)).astype(jnp.int32)
    large = jnp.minimum(large, NUM_BUCKETS - 1)
    return jnp.where(n < max_exact, n, large)


def _group_attention(q, k, vt, bias_table, rate, n_steps):
    B, _, L, D = q.shape
    S = L * rate
    Bk = SUB_BLOCK
    with_prev = L > Bk
    nk = 2 * Bk if with_prev else Bk
    n = max(Bk, ATTN_TILE // rate)
    tm = n * rate
    nt = S // tm
    n_pairs = DIL_HEADS // 2
    ql = jnp.arange(Bk, dtype=jnp.int32)[:, None]
    kl = jnp.arange(2 * Bk, dtype=jnp.int32)[None, :]
    steps = ql + Bk - kl
    bucket = _t5_bucket(jnp.maximum(steps, 0) * rate)
    buckets = jnp.arange(NUM_BUCKETS, dtype=jnp.int32)
    bias = jnp.sum(jnp.where(bucket[None, :, :, None] == buckets[:, None, None, None],
                             bias_table.astype(F32)[:, None, None, :], 0.0), axis=0)
    band = ((steps >= 0) & (steps <= n_steps))[:, :, None]
    first = (kl < Bk)[:, :, None]

    def layout(t):
        t = t[:, 2 * Bk - nk:, :].transpose(1, 2, 0)
        return t.reshape(nk, n_pairs, 2 * Bk).transpose(1, 0, 2)

    bias_in = layout(jnp.where(band, bias * LOG2E, NEG_INF))
    bias_first = layout(jnp.where(band & ~first, bias * LOG2E, NEG_INF))
    edge = jnp.stack([bias_first, bias_in])

    cur_qk = pl.BlockSpec((1, rate, n, D), lambda b, i: (b, 0, i, 0))
    cur_vt = pl.BlockSpec((1, rate, D, n), lambda b, i: (b, 0, 0, i))
    per_n = n // Bk
    prev_qk = pl.BlockSpec((1, rate, Bk, D), lambda b, i: (b, 0, jnp.maximum(i * per_n - 1, 0), 0))
    prev_vt = pl.BlockSpec((1, rate, D, Bk), lambda b, i: (b, 0, 0, jnp.maximum(i * per_n - 1, 0)))
    edge_spec = pl.BlockSpec((1, n_pairs, nk, 2 * Bk), lambda b, i: (jnp.minimum(i, 1), 0, 0, 0))
    if with_prev:
        in_specs = [cur_qk, cur_qk, prev_qk, cur_vt, prev_vt, edge_spec]
        args = (q, k, k, vt, vt, edge)
    else:
        in_specs = [cur_qk, cur_qk, cur_vt, edge_spec]
        args = (q, k, vt, edge)
    if with_prev and n > Bk:
        in_specs.append(_resident((n_pairs, nk, 2 * Bk), lambda b, i: (0, 0, 0)))
        args += (bias_in,)
    return pl.pallas_call(
        functools.partial(_attn_kernel, rate=rate, with_prev=with_prev),
        grid=(B, nt),
        in_specs=in_specs,
        out_specs=[pl.BlockSpec((tm, D), lambda b, i: (b * nt + i, 0)),
                   pl.BlockSpec((tm, LANES), lambda b, i: (b * nt + i, 0))],
        out_shape=[jax.ShapeDtypeStruct((B * S, D), BF16),
                   jax.ShapeDtypeStruct((B * S, LANES), F32)],
        scratch_shapes=[pltpu.VMEM((D // LANES, tm if rate > 1 else 8, LANES), F32)],
        compiler_params=_params("parallel", "parallel"),
        name="attn_rate%d" % rate,
    )(*args)


def kernel(x, ret_w_in, ret_w_out, kv_norm, w_kv, k_norm, dil_wq, q_norm, dil_wo, rel_bias,
           mixer_norm, ffn_norm, router_grp, router_grp_b, router_exp, router_exp_b,
           exp_gate, exp_up, exp_down):
    B, S, D = x.shape
    h = x.reshape(B * S, D)

    def moe_layer(layer, mixer, w_out, h, combine):
        h1, *hn, route, route_t, counts = _post(
            mixer, w_out.astype(BF16), h, ffn_norm[layer], router_grp[layer], router_grp_b[layer],
            router_exp[layer], router_exp_b[layer])
        return _moe(h1, hn, route, route_t, counts, layer, exp_gate, exp_up, exp_down, combine)

    h = moe_layer(0, (mixer_norm[0], ret_w_in[0].astype(BF16), S), ret_w_out[0], h, combine=False)

    G = len(DIL_RATES)
    gd = DIL_HEADS * DIL_HEAD_DIM
    outs, lses = [], []
    for g in range(G):
        cq = slice(g * gd, (g + 1) * gd)
        cv = slice(G * gd + g * gd, G * gd + (g + 1) * gd)
        q, k, vt, *combined = _qkv(h, B, S, DIL_RATES[g], mixer_norm[1], kv_norm,
                                   dil_wq[0][:, cq].astype(BF16), w_kv[:, cq].astype(BF16),
                                   w_kv[:, cv].astype(BF16), q_norm[0][g], k_norm[g])
        if combined:
            h, = combined
        o, lse = _group_attention(q, k, vt, rel_bias[:, g * DIL_HEADS:(g + 1) * DIL_HEADS],
                                  DIL_RATES[g], DIL_WINDOWS[g] // DIL_RATES[g])
        outs.append(o)
        lses.append(lse)
    h = moe_layer(1, (tuple(outs), tuple(lses)), dil_wo[0], h, combine=True)
    return h.reshape(B, S, D)
```

```python
import functools
import math

import jax
import jax.numpy as jnp
from jax import lax
from jax.experimental import pallas as pl
from jax.experimental.pallas import tpu as pltpu
from jax.experimental.pallas import tpu_sc as plsc

F32 = jnp.float32
BF16 = jnp.bfloat16

EPS = 1e-6
NEG_INF = -1e30

RET_HEADS = 4
RET_CHUNK = 256
ROPE_BASE = 10000.0

DIL_WINDOWS = (128, 512, 2048)
DIL_RATES = (1, 4, 16)
DIL_HEADS = 16
DIL_HEAD_DIM = 64
SUB_BLOCK = 128
NUM_BUCKETS = 32
MAX_DISTANCE = 2048

MOE_GROUPS = 4
EXPERTS_PER_GROUP = 8
N_EXPERTS = MOE_GROUPS * EXPERTS_PER_GROUP
MOE_BLOCK = 512

LANES = 128
BF16_ROWS = 16
PERM_ROWS = 256
ROW_TILE = 1024
TOKEN_TILE = 512
ATTN_TILE = 2048
VMEM_LIMIT = 56 * 1024 * 1024

NT_DIMS = (((1,), (1,)), ((), ()))
LOG2E = math.log2(math.e)
LN2 = math.log(2.0)


def _params(*sem):
    return pltpu.CompilerParams(dimension_semantics=sem, vmem_limit_bytes=VMEM_LIMIT)


def _resident(shape, index_map):
    return pl.BlockSpec(shape, index_map, pipeline_mode=pl.Buffered(1))


def _rms(x):
    return x * lax.rsqrt(jnp.mean(x * x, axis=-1, keepdims=True) + EPS)


def _retention_tile(x_ref, g_ref, w_ref, cos_ref, sin_ref, din_ref, xi_ref, zeta_ref, cd_ref, state_ref,
                    first_tile):
    C = RET_CHUNK
    H, dk, dv = state_ref.shape
    half = dk // 2
    ts = x_ref.shape[0]
    k_scale = dk ** -0.5

    @pl.when(first_tile)
    def _():
        state_ref[...] = jnp.zeros_like(state_ref)

    xn = (_rms(x_ref[...]) * g_ref[...]).astype(BF16)
    cos, sin = cos_ref[...], sin_ref[...]

    def proj(c0, width):
        return jnp.dot(xn, w_ref[:, c0:c0 + width], preferred_element_type=F32)

    def rot(t):
        t1, t2 = t[:, :half], t[:, half:]
        return jnp.concatenate([t1 * cos - t2 * sin, t1 * sin + t2 * cos], axis=1)

    heads = []
    for hh in range(H):
        q = rot(proj(hh * dk, dk)).astype(BF16)
        k = rot(proj(H * dk + hh * dk, dk)) * k_scale
        kb = k.astype(BF16)
        v = proj(2 * H * dk + hh * dv, dv).astype(BF16)
        gate = proj(2 * H * dk + H * dv + hh * dv, dv)
        gate = gate * jax.nn.sigmoid(gate)
        chunks = []
        for c in range(ts // C):
            rows = slice(c * C, (c + 1) * C)
            s = lax.dot_general(q[rows], kb[rows], NT_DIMS, preferred_element_type=F32) * din_ref[hh]
            inner = jnp.dot(s.astype(BF16), v[rows], preferred_element_type=F32)
            state = state_ref[hh]
            cross = jnp.dot(q[rows], state.astype(BF16), preferred_element_type=F32) * xi_ref[hh]
            kz_t = (k[rows] * zeta_ref[hh]).T.astype(BF16)
            state_ref[hh] = state * cd_ref[hh] + jnp.dot(kz_t, v[rows], preferred_element_type=F32)
            chunks.append((gate[rows] * _rms(inner + cross)).astype(BF16))
        heads.append(jnp.concatenate(chunks, axis=0))
    return jnp.concatenate(heads, axis=1)


def _retention_operands(gain, w, S, ts):
    H, C = RET_HEADS, RET_CHUNK
    D = w.shape[0]
    dk = D // H
    half = dk // 2
    nt = S // ts
    pos = jnp.arange(S, dtype=F32)
    inv = 1.0 / (ROPE_BASE ** jnp.linspace(0.0, 1.0, half, dtype=F32))
    ang = pos[:, None] * inv[None, :]
    log_g = jnp.log(1.0 - 2.0 ** (-5.0 - jnp.arange(H, dtype=F32)))
    idx = jnp.arange(C, dtype=F32)
    diff = idx[:, None] - idx[None, :]
    d_in = jnp.where(diff >= 0, jnp.exp(log_g[:, None, None] * jnp.maximum(diff, 0.0)), 0.0)
    xi = jnp.exp(log_g[:, None] * (idx + 1.0))[:, :, None]
    zeta = jnp.exp(log_g[:, None] * (C - 1.0 - idx))[:, :, None]
    chunk_decay = jnp.exp(log_g * C)[:, None, None]
    table = pl.BlockSpec((ts, half), lambda i: (i % nt, 0))
    per_head = lambda shape: _resident((H,) + shape, lambda i: (0, 0, 0))
    args = (gain.reshape(1, D), w, jnp.cos(ang), jnp.sin(ang), d_in, xi, zeta, chunk_decay)
    specs = [_resident((1, D), lambda i: (0, 0)), _resident(w.shape, lambda i: (0, 0)), table, table,
             per_head((C, C)), per_head((C, 1)), per_head((C, 1)), per_head((1, 1))]
    state = pltpu.VMEM((H, dk, 2 * dk), F32)
    return args, specs, state


ROUTE_E1, ROUTE_E2, ROUTE_G1, ROUTE_G2, ROUTE_R1, ROUTE_R2 = range(6)
ROUTE_ROWS = 8
ROUTE_PIECES = 2
ROUTER_EXP_LANE0 = MOE_GROUPS
HI16 = 0xFFFF0000


def _pack_bf16_pair(a, b):
    ua = lax.bitcast_convert_type(a.astype(BF16).astype(F32), jnp.uint32)
    ub = lax.bitcast_convert_type(b.astype(BF16).astype(F32), jnp.uint32)
    return ua | (ub >> 16)


def _unpack_bf16_pair(w):
    a = lax.bitcast_convert_type(w & jnp.uint32(HI16), F32)
    b = lax.bitcast_convert_type(w << 16, F32)
    return a, b


ROW_PARTS = 2


def _part_cols(D, part):
    w = D // 2 // ROW_PARTS
    return slice(part * w, (part + 1) * w), slice(D // 2 + part * w, D // 2 + (part + 1) * w)


def _pack_part(x, part):
    hi, lo = _part_cols(x.shape[1], part)
    return _pack_bf16_pair(x[:, hi], x[:, lo])


def _merge_groups(o_refs, l_refs, ex_ref):
    ex = ex_ref[...]

    def expand(w):
        return jnp.dot(w.astype(BF16), ex, preferred_element_type=F32)

    lses = [l_ref[...] for l_ref in l_refs]
    top = functools.reduce(jnp.maximum, lses)
    ws = [jnp.exp(l - top) for l in lses]
    den = sum(ws)
    last = o_refs[-1][...].astype(F32)
    return (last + sum(expand(w / den) * (o_ref[...].astype(F32) - last)
                       for w, o_ref in zip(ws[:-1], o_refs[:-1]))).astype(BF16)


def _post_kernel(*refs, n_groups, seq_tiles):
    n_a = 2 * n_groups + 1 if n_groups else 8
    a_refs = refs[:n_a]
    (w_ref, h_ref, g_ref, wr_ref, br_ref, tri_ref,
     h1_ref, hn0_ref, hn1_ref, route_ref, route_t_ref, cnt_ref, carry_ref) = refs[n_a:n_a + 13]
    hn_refs = (hn0_ref, hn1_ref)
    if n_groups:
        a = _merge_groups(a_refs[:n_groups], a_refs[n_groups:2 * n_groups], a_refs[-1])
    else:
        a = _retention_tile(h_ref, *a_refs, refs[-1], pl.program_id(0) % seq_tiles == 0)
    @pl.when(pl.program_id(0) == 0)
    def _():
        carry_ref[...] = jnp.zeros_like(carry_ref)

    half = tri_ref.shape[0]
    n_pieces = h_ref.shape[0] // half
    D = h_ref.shape[1]
    lane = lax.broadcasted_iota(jnp.int32, (half, LANES), 1).astype(F32)
    ninf = -jnp.inf

    def first_argmax(vals):
        top = jnp.max(vals, axis=1, keepdims=True)
        where = jnp.min(jnp.where(vals == top, lane, float(LANES)), axis=1, keepdims=True)
        return top, where

    h1_ref[...] = h_ref[...] + jnp.dot(a, w_ref[...], preferred_element_type=F32)

    def route_half(rows):
        hn = _rms(h1_ref[rows, :]) * g_ref[...]
        for part, ref in enumerate(hn_refs):
            ref[rows, :] = _pack_part(hn, part)
        hi = hn.astype(BF16)
        lo = (hn - hi.astype(F32)).astype(BF16)
        both = jnp.dot(hi, wr_ref[...], preferred_element_type=F32)
        logits = (both[:, :LANES] + both[:, LANES:]
                  + jnp.dot(lo, wr_ref[:, :LANES], preferred_element_type=F32) + br_ref[...])

        is_grp = lane < MOE_GROUPS
        lg = jnp.where(is_grp, logits, ninf)
        mg, grp = first_argmax(lg)
        p_grp = 1.0 / jnp.sum(jnp.where(is_grp, jnp.exp(lg - mg), 0.0), axis=1, keepdims=True)

        e_lane = lane - ROUTER_EXP_LANE0
        in_grp = (e_lane < N_EXPERTS) & (jnp.floor(e_lane * (1.0 / EXPERTS_PER_GROUP)) == grp)
        le = jnp.where(in_grp, logits, ninf)
        v1, i1 = first_argmax(le)
        le2 = jnp.where(lane == i1, ninf, le)
        v2, i2 = first_argmax(le2)
        e = jnp.exp(v2 - v1)
        hit1 = lane == (i1 - ROUTER_EXP_LANE0)
        hit2 = lane == (i2 - ROUTER_EXP_LANE0)
        onehot = jnp.where(hit1 | hit2, 1.0, 0.0)
        earlier = jnp.dot(tri_ref[...], onehot.astype(BF16), preferred_element_type=F32)
        return dict(e1=i1 - ROUTER_EXP_LANE0, e2=i2 - ROUTER_EXP_LANE0, g1=p_grp / (1.0 + e),
                    g2=p_grp * e / (1.0 + e), hit1=hit1, hit2=hit2, earlier=earlier,
                    count=jnp.sum(onehot, axis=0, keepdims=True))

    halves = [route_half(slice(j * half, (j + 1) * half)) for j in range(n_pieces)]
    carry = carry_ref[...]
    for j, r in enumerate(halves):
        before = carry + r["earlier"]
        r1 = jnp.sum(jnp.where(r["hit1"], before, 0.0), axis=1, keepdims=True)
        r2 = jnp.sum(jnp.where(r["hit2"], before, 0.0), axis=1, keepdims=True)
        carry = carry + r["count"]
        route = jnp.zeros((half, LANES), F32)
        for slot, val in ((ROUTE_E1, r["e1"]), (ROUTE_E2, r["e2"]), (ROUTE_G1, r["g1"]),
                          (ROUTE_G2, r["g2"]), (ROUTE_R1, r1), (ROUTE_R2, r2)):
            route = jnp.where(lane == slot, val, route)
        route_ref[j * half:(j + 1) * half, :] = route
        route_t_ref[:, j * half:(j + 1) * half] = route.T[:ROUTE_ROWS]
    carry_ref[...] = carry
    cnt_ref[...] = carry


def _post(mixer, w, h, gain, w_grp, b_grp, w_exp, b_exp):
    T, D = h.shape
    K = w.shape[0]
    row_block = lambda width: pl.BlockSpec((tm, width), lambda i: (i, 0))
    scratch = [pltpu.VMEM((1, LANES), F32)]
    if len(mixer) == 2:
        outs, lses = mixer
        n_groups, seq_tiles = len(outs), 1
        tm = min(TOKEN_TILE, T)
        ex = jnp.where(jnp.arange(LANES)[:, None] == (jnp.arange(K)[None, :] // DIL_HEAD_DIM), 1.0, 0.0).astype(BF16)
        a_args = (*outs, *lses, ex)
        a_specs = [row_block(K)] * n_groups + [row_block(LANES)] * n_groups + [_resident((LANES, K), lambda i: (0, 0))]
    else:
        mix_gain, w_in, S = mixer
        tm = min(TOKEN_TILE, S)
        n_groups, seq_tiles = 0, S // tm
        a_args, a_specs, state = _retention_operands(mix_gain, w_in, S, tm)
        scratch.append(state)
    n_r = MOE_GROUPS + N_EXPERTS
    wr = jnp.zeros((D, LANES), F32).at[:, :n_r].set(jnp.concatenate([w_grp, w_exp], axis=1))
    wr_hi = wr.astype(BF16)
    wr_lo = (wr - wr_hi.astype(F32)).astype(BF16)
    wr2 = jnp.concatenate([wr_hi, wr_lo], axis=1)
    br = jnp.zeros((1, LANES), F32).at[0, :n_r].set(jnp.concatenate([b_grp, b_exp]))
    piece = tm // ROUTE_PIECES
    tri = jnp.tril(jnp.ones((piece, piece), BF16), k=-1)
    return pl.pallas_call(
        functools.partial(_post_kernel, n_groups=n_groups, seq_tiles=seq_tiles),
        grid=(T // tm,),
        in_specs=a_specs + [
                  _resident((K, D), lambda i: (0, 0)),
                  pl.BlockSpec((tm, D), lambda i: (i, 0)),
                  _resident((1, D), lambda i: (0, 0)),
                  _resident((D, 2 * LANES), lambda i: (0, 0)),
                  _resident((1, LANES), lambda i: (0, 0)),
                  _resident((piece, piece), lambda i: (0, 0))],
        out_specs=[pl.BlockSpec((tm, D), lambda i: (i, 0))]
                  + [pl.BlockSpec((tm, D // 2 // ROW_PARTS), lambda i: (i, 0))] * ROW_PARTS
                  + [pl.BlockSpec((tm, LANES), lambda i: (i, 0)),
                   pl.BlockSpec((ROUTE_ROWS, tm), lambda i: (0, i)),
                   pl.BlockSpec((1, LANES), lambda i: (0, 0))],
        out_shape=[jax.ShapeDtypeStruct((T, D), F32)]
                  + [jax.ShapeDtypeStruct((T, D // 2 // ROW_PARTS), jnp.uint32)] * ROW_PARTS
                  + [jax.ShapeDtypeStruct((T, LANES), F32),
                   jax.ShapeDtypeStruct((ROUTE_ROWS, T), F32),
                   jax.ShapeDtypeStruct((1, LANES), F32)],
        scratch_shapes=scratch,
        compiler_params=_params("arbitrary"),
        name="post_mixer",
    )(*a_args, w, h, gain.reshape(1, D), wr2, br, tri)


SC_WINDOW = 128


def _sc_mesh():
    return plsc.VectorSubcoreMesh(core_axis_name="core", subcore_axis_name="subcore")


def _sc_window_specs(W):
    rows = pl.BlockSpec((SC_WINDOW, W), lambda i: (i, 0))
    idx = pl.BlockSpec((1, SC_WINDOW), lambda i: (0, i))
    return rows, idx


def _sc_pipeline(body, n_rows, in_specs, out_specs):
    return pltpu.emit_pipeline(body, grid=(n_rows // SC_WINDOW,), in_specs=in_specs, out_specs=out_specs,
                               core_axis_name=("core", "subcore"), dimension_semantics=(pltpu.PARALLEL,))


def _dispatch(xs, dest, pad_idx, P):
    T, W = xs[0].shape
    n_pad = pad_idx.shape[1]
    n = len(xs)
    rows, idx = _sc_window_specs(W)
    zero_rows = pl.BlockSpec((SC_WINDOW, W), lambda i: (0, 0))
    out = jax.ShapeDtypeStruct((P, W), xs[0].dtype)

    @functools.partial(pl.kernel, out_type=(out,) * n, mesh=_sc_mesh(), scratch_types=[], name="dispatch")
    def scatter(*refs):
        x_hbm, (d0_hbm, d1_hbm, z_hbm, p_hbm), o_hbm = refs[:n], refs[n:n + 4], refs[n + 4:]
        for x, o in zip(x_hbm, o_hbm):
            def put_pair(x_vmem, i0_vmem, i1_vmem, o=o):
                pltpu.sync_copy(x_vmem, o.at[i0_vmem.at[0]])
                pltpu.sync_copy(x_vmem, o.at[i1_vmem.at[0]])

            def put(x_vmem, i_vmem, o=o):
                pltpu.sync_copy(x_vmem, o.at[i_vmem.at[0]])

            _sc_pipeline(put_pair, T, [rows, idx, idx], [])(x, d0_hbm, d1_hbm)
            _sc_pipeline(put, n_pad, [zero_rows, idx], [])(z_hbm, p_hbm)

    return scatter(*xs, dest[0:1], dest[1:2], jnp.zeros((SC_WINDOW, W), xs[0].dtype), pad_idx)


WEIGHT_LEADS = (3, 2, 1)
EXPERT_SLOTS = max(WEIGHT_LEADS) + 1


def _expert_kernel(blk_exp_ref, slot_ref, n_used_ref, *refs):
    x_refs = refs[:ROW_PARTS]
    w_refs = refs[ROW_PARTS:ROW_PARTS + 3]
    o_refs = refs[ROW_PARTS + 3:2 * ROW_PARTS + 3]
    w_slots = refs[2 * ROW_PARTS + 3:]
    lead = max(WEIGHT_LEADS)
    n_used = n_used_ref[0]
    i = pl.program_id(0) - lead
    D = w_slots[0].shape[1]

    for w_ref, w_s, ahead in zip(w_refs, w_slots, WEIGHT_LEADS):
        j = i + ahead
        jc = jnp.clip(j, 0, n_used - 1)
        arrived = (j >= 0) & (j < n_used) & ((j == 0) | (blk_exp_ref[jc] != blk_exp_ref[jnp.maximum(jc - 1, 0)]))

        @pl.when(arrived)
        def _(w_ref=w_ref, w_s=w_s, jc=jc):
            w_s[slot_ref[jc]] = w_ref[0, 0].astype(BF16)

    @pl.when((i >= 0) & (i < n_used))
    def _():
        slot = slot_ref[jnp.clip(i, 0, n_used - 1)]
        wg_s, wu_s, wd_s = w_slots
        pieces = []
        for part, x_ref in enumerate(x_refs):
            for cols, val in zip(_part_cols(D, part), _unpack_bf16_pair(x_ref[...])):
                pieces.append((cols, val.astype(BF16)))

        def up(w_s):
            return sum(jnp.dot(val, w_s[slot, cols, :], preferred_element_type=F32) for cols, val in pieces)

        g = up(wg_s)
        hid = (g * jax.nn.sigmoid(g) * up(wu_s)).astype(BF16)
        y = jnp.dot(hid, wd_s[slot], preferred_element_type=F32)
        for part, o_ref in enumerate(o_refs):
            o_ref[...] = _pack_part(y, part)

    @pl.when(i >= n_used)
    def _():
        for o_ref in o_refs:
            o_ref[...] = jnp.zeros_like(o_ref)


def _experts(xs, blk_exp, n_used, layer, w_gate, w_up, w_down):
    P, W = xs[0].shape
    D = 2 * W * ROW_PARTS
    FF = w_gate.shape[3]
    nblk = P // MOE_BLOCK
    lead = max(WEIGHT_LEADS)
    changes = jnp.concatenate([jnp.zeros((1,), jnp.int32), (blk_exp[1:] != blk_exp[:-1]).astype(jnp.int32)])
    slot = jnp.cumsum(changes) % EXPERT_SLOTS

    def x_map(g, be, sl, nu):
        return (jnp.clip(g - lead, 0, nu[0] - 1), 0)

    def o_map(g, be, sl, nu):
        return (jnp.maximum(g - lead, 0), 0)

    def w_map(ahead):
        return lambda g, be, sl, nu: (layer, be[jnp.clip(g - lead + ahead, 0, nu[0] - 1)], 0, 0)

    wg_spec, wu_spec, wd_spec = (pl.BlockSpec((1, 1) + shape, w_map(ahead))
                                 for shape, ahead in zip(((D, FF), (D, FF), (FF, D)), WEIGHT_LEADS))
    return pl.pallas_call(
        _expert_kernel,
        grid_spec=pltpu.PrefetchScalarGridSpec(
            num_scalar_prefetch=3,
            grid=(nblk + lead,),
            in_specs=[pl.BlockSpec((MOE_BLOCK, W), x_map)] * ROW_PARTS + [wg_spec, wu_spec, wd_spec],
            out_specs=[pl.BlockSpec((MOE_BLOCK, W), o_map)] * ROW_PARTS,
            scratch_shapes=[pltpu.VMEM((EXPERT_SLOTS, D, FF), BF16), pltpu.VMEM((EXPERT_SLOTS, D, FF), BF16),
                            pltpu.VMEM((EXPERT_SLOTS, FF, D), BF16)]),
        out_shape=[jax.ShapeDtypeStruct((P, W), jnp.uint32)] * ROW_PARTS,
        compiler_params=_params("arbitrary"),
        name="experts",
    )(blk_exp, slot, n_used, *xs, w_gate, w_up, w_down)


def _gather_pairs(ys, dest):
    W = ys[0].shape[1]
    T = dest.shape[1]
    n = len(ys)
    rows, idx = _sc_window_specs(W)
    out = jax.ShapeDtypeStruct((T, W), ys[0].dtype)

    @functools.partial(pl.kernel, out_type=(out,) * (2 * n), mesh=_sc_mesh(), scratch_types=[],
                       name="gather_pairs")
    def gather(*refs):
        y_hbm, d_hbm, o_hbm = refs[:n], refs[n:n + 2], refs[n + 2:]
        for slot, d in enumerate(d_hbm):
            for y, o in zip(y_hbm, o_hbm[slot * n:(slot + 1) * n]):
                def get(i_vmem, o_vmem, y=y):
                    pltpu.sync_copy(y.at[i_vmem.at[0]], o_vmem)

                _sc_pipeline(get, T, [idx], [rows])(d, o)

    return gather(*ys, dest[0:1], dest[1:2])


def _combined_rows(h_ref, route_ref, y_refs):
    D = h_ref.shape[1]
    route = route_ref[...]
    gates = (route[:, ROUTE_G1:ROUTE_G1 + 1], route[:, ROUTE_G2:ROUTE_G2 + 1])
    pieces = {}
    for part in range(ROW_PARTS):
        slots = [_unpack_bf16_pair(y_refs[slot * ROW_PARTS + part][...]) for slot in range(2)]
        for half, cols in enumerate(_part_cols(D, part)):
            pieces[cols.start] = h_ref[:, cols] + (gates[0] * slots[0][half] + gates[1] * slots[1][half])
    return jnp.concatenate([pieces[c] for c in sorted(pieces)], axis=1)


def _combine_kernel(h_ref, route_ref, *refs):
    refs[-1][...] = _combined_rows(h_ref, route_ref, refs[:-1])


def _combine(h, route, pairs):
    T, D = h.shape
    W = pairs[0].shape[1]
    tm = min(ROW_TILE, T)
    return pl.pallas_call(
        _combine_kernel,
        grid=(T // tm,),
        in_specs=[pl.BlockSpec((tm, D), lambda i: (i, 0)), pl.BlockSpec((tm, LANES), lambda i: (i, 0))]
                 + [pl.BlockSpec((tm, W), lambda i: (i, 0))] * len(pairs),
        out_specs=pl.BlockSpec((tm, D), lambda i: (i, 0)),
        out_shape=jax.ShapeDtypeStruct((T, D), F32),
        compiler_params=_params("parallel"),
        name="combine",
    )(h, route, *pairs)


def _moe(h1, hn, route, route_t, counts, layer, w_gate, w_up, w_down, combine):
    T, D = h1.shape
    A = 2 * T
    nblk = -(-A // MOE_BLOCK) + N_EXPERTS
    P = nblk * MOE_BLOCK
    eid = route_t[ROUTE_E1:ROUTE_E2 + 1].astype(jnp.int32)
    rank = route_t[ROUTE_R1:ROUTE_R2 + 1].astype(jnp.int32)
    cnt = counts[0, :N_EXPERTS].astype(jnp.int32)
    padded = (cnt + MOE_BLOCK - 1) // MOE_BLOCK * MOE_BLOCK
    pends = jnp.cumsum(padded)
    pstarts = pends - padded
    experts = jnp.arange(N_EXPERTS, dtype=jnp.int32)
    start_of = jnp.sum(jnp.where(eid[:, None, :] == experts[None, :, None], pstarts[None, :, None], 0), axis=1)
    dest = start_of + rank
    blk_start = jnp.arange(nblk, dtype=jnp.int32) * MOE_BLOCK
    blk_exp = jnp.minimum(jnp.sum((pends[None, :] <= blk_start[:, None]).astype(jnp.int32), axis=1),
                          N_EXPERTS - 1)
    n_used = pends[-1:] // MOE_BLOCK

    gap_start = jnp.concatenate([pstarts + cnt, pends[-1:]])
    gap_len = jnp.concatenate([padded - cnt, P - pends[-1:]])
    gap_end = jnp.cumsum(gap_len)
    j = jnp.arange(P - A, dtype=jnp.int32)
    gap_of = jnp.sum((gap_end[None, :] <= j[:, None]).astype(jnp.int32), axis=1)
    sel = gap_of[:, None] == jnp.arange(N_EXPERTS + 1, dtype=jnp.int32)[None, :]
    pad_idx = jnp.sum(jnp.where(sel, (gap_start - (gap_end - gap_len))[None, :] + j[:, None], 0), axis=1)

    xs = _dispatch(hn, dest, pad_idx.reshape(1, P - A), P)
    ys = _experts(xs, blk_exp, n_used, layer, w_gate, w_up, w_down)
    pairs = _gather_pairs(ys, dest)
    return _combine(h1, route, pairs) if combine else (h1, route, pairs)


def _qkv_kernel(*refs, rate, n, n_chunks, n_pending, permute):
    n_x = 2 + n_pending if n_pending else 1
    x_refs = refs[:n_x]
    gq_ref, gkv_ref, wq_ref, wk_ref, wvt_ref, qn_ref, kn_ref, seg_ref = refs[n_x:n_x + 8]
    perm_ref = refs[n_x + 8] if permute else None
    q_ref, k_ref, vt_ref = refs[n_x + 8 + permute:n_x + 11 + permute]
    xs_ref = refs[-1]
    seg = seg_ref[...]
    width = seg.shape[0]
    res_per_chunk = q_ref.shape[1]
    chunk = pl.program_id(2)

    def head_norm(t, gain):
        cols = []
        for j in range(t.shape[1] // width):
            tj = t[:, j * width:(j + 1) * width]
            ms = jnp.dot((tj * tj).astype(BF16), seg, preferred_element_type=F32)
            cols.append(tj * lax.rsqrt(ms + EPS))
        return jnp.concatenate(cols, axis=1) * gain

    def normed(x):
        y = _rms(x)
        return (y * gq_ref[...]).astype(BF16), (y * gkv_ref[...]).astype(BF16)

    def project(xq, xkv):
        q = head_norm(jnp.dot(xq, wq_ref[...], preferred_element_type=F32), qn_ref[...]).astype(q_ref.dtype)
        k = head_norm(jnp.dot(xkv, wk_ref[...], preferred_element_type=F32), kn_ref[...]).astype(k_ref.dtype)
        vt = lax.dot_general(wvt_ref[...], xkv, NT_DIMS, preferred_element_type=F32).astype(vt_ref.dtype)
        for j in range(res_per_chunk):
            q_ref[0, j] = q[j * n:(j + 1) * n]
            k_ref[0, j] = k[j * n:(j + 1) * n]
            vt_ref[0, j] = vt[:, j * n:(j + 1) * n]

    if n_pending:
        x = _combined_rows(x_refs[0], x_refs[1], x_refs[2:])
        refs[-2][...] = x
        project(*normed(x))
        return
    x_ref, = x_refs
    if rate == 1:
        project(*normed(x_ref[...]))
        return

    if permute:
        group = perm_ref.shape[0]
        n_groups, run = x_ref.shape[0] // group, group // rate

        @pl.when(chunk == 0)
        def _():
            for g in range(n_groups):
                rows = slice(g * group, (g + 1) * group)
                for t, xt in enumerate(normed(x_ref[rows, :])):
                    xs_ref[t, rows, :] = jnp.dot(perm_ref[...], xt, preferred_element_type=F32).astype(BF16)

        span = res_per_chunk * run
        start = pl.multiple_of(chunk * span, span)

        def residue_major(t):
            parts = [xs_ref[t, pl.ds(g * group + start, span), :] for g in range(n_groups)]
            return jnp.concatenate([p[j * run:(j + 1) * run] for j in range(res_per_chunk) for p in parts], axis=0)

        project(residue_major(0), residue_major(1))
        return

    @pl.when(chunk == 0)
    def _():
        for j in range(xs_ref.shape[0]):
            xs_ref[j] = x_ref[:, j * LANES:(j + 1) * LANES]

    for ch in range(n_chunks):
        @pl.when(chunk == ch)
        def _(ch=ch):
            residues = range(ch * res_per_chunk, (ch + 1) * res_per_chunk)
            project(*normed(jnp.concatenate(
                [jnp.concatenate([xs_ref[j, pl.ds(c, n, stride=rate), :] for j in range(xs_ref.shape[0])], axis=1)
                 for c in residues], axis=0)))


def _qkv(h, B, S, rate, gq, gkv, wq, wk, wv, qn, kn):
    pending = h if isinstance(h, tuple) else None
    D = (pending[0] if pending else h).shape[1]
    L = S // rate
    hd = DIL_HEAD_DIM
    n = max(SUB_BLOCK, TOKEN_TILE // rate)
    tm = n * rate
    res_per_chunk = max(1, TOKEN_TILE // n)
    n_chunks = rate // res_per_chunk
    width = 2 * LANES
    ii = jnp.arange(width)
    seg = jnp.where((ii[:, None] // hd) == (ii[None, :] // hd), 1.0 / hd, 0.0).astype(BF16)
    permute = rate * BF16_ROWS == PERM_ROWS
    if permute:
        pp = jnp.arange(PERM_ROWS)
        perm = ((pp[:, None] // BF16_ROWS == pp[None, :] % rate)
                & (pp[:, None] % BF16_ROWS == pp[None, :] // rate)).astype(BF16)
    row = lambda g: jnp.tile(g, D // hd).reshape(1, D)
    const = lambda shape: _resident(shape, lambda b, i, c: (0,) * len(shape))
    qk_spec = pl.BlockSpec((1, res_per_chunk, n, D), lambda b, i, c: (b, c, i, 0))
    qk_shape = jax.ShapeDtypeStruct((B, rate, L, D), BF16)
    rows = lambda width: pl.BlockSpec((tm, width), lambda b, i, c: (b * (S // tm) + i, 0))
    if pending:
        assert rate == 1
        h1, route, pairs = pending
        x_args = (h1, route, *pairs)
        x_specs = [rows(D), rows(LANES)] + [rows(pairs[0].shape[1])] * len(pairs)
        extra_specs, extra_shapes = [rows(D)], [jax.ShapeDtypeStruct((B * S, D), F32)]
    else:
        x_args, x_specs, extra_specs, extra_shapes = (h,), [rows(D)], [], []
    return pl.pallas_call(
        functools.partial(_qkv_kernel, rate=rate, n=n, n_chunks=n_chunks,
                          n_pending=len(pending[2]) if pending else 0, permute=permute),
        grid=(B, S // tm, n_chunks),
        in_specs=x_specs + [
                  const((1, D)), const((1, D)), const((D, D)), const((D, D)), const((D, D)),
                  const((1, D)), const((1, D)), const((width, width))]
                 + ([const((PERM_ROWS, PERM_ROWS))] if permute else []),
        out_specs=[qk_spec, qk_spec,
                   pl.BlockSpec((1, res_per_chunk, D, n), lambda b, i, c: (b, c, 0, i))] + extra_specs,
        out_shape=[qk_shape, qk_shape, jax.ShapeDtypeStruct((B, rate, D, L), BF16)] + extra_shapes,
        scratch_shapes=[pltpu.VMEM((2, tm, D), BF16) if permute else
                        pltpu.VMEM((D // LANES, tm if rate > 1 else 8, LANES), F32)],
        compiler_params=_params("parallel", "parallel", "arbitrary"),
        name="qkv_rate%d" % rate,
    )(*x_args, gq.reshape(1, D), gkv.reshape(1, D), wq, wk, wv.T, row(qn) * (hd ** -0.5 * LOG2E), row(kn), seg,
      *((perm,) if permute else ()))


def _attn_kernel(*refs, rate, with_prev):
    if not with_prev:
        q_ref, kc_ref, vc_ref, edge_ref, o_ref, lse_ref, os_ref = refs
    elif len(refs) == 10:
        q_ref, kc_ref, kp_ref, vc_ref, vp_ref, edge_ref, bias_ref, o_ref, lse_ref, os_ref = refs
    else:
        q_ref, kc_ref, kp_ref, vc_ref, vp_ref, edge_ref, o_ref, lse_ref, os_ref = refs
    Bk = SUB_BLOCK
    hd = DIL_HEAD_DIM
    n_pairs = DIL_HEADS // 2
    lane = lax.broadcasted_iota(jnp.int32, (Bk, LANES), 1)

    def block(load_q, load_k, load_vt, load_bias, store):
        out_t, lse_t = [], []
        for p in range(n_pairs):
            qp = load_q(p)
            zero = jnp.zeros_like(qp)
            q2 = jnp.concatenate([jnp.where(lane < hd, qp, zero), jnp.where(lane >= hd, qp, zero)], axis=0)
            s = lax.dot_general(load_k(p), q2, NT_DIMS, preferred_element_type=F32) + load_bias(p)
            m = jnp.max(s, axis=0, keepdims=True)
            pr = jnp.exp2(s - m)
            l = jnp.sum(pr, axis=0, keepdims=True)
            pb = pr.astype(BF16)
            vt = load_vt(p)
            out_t.append(jnp.dot(vt[:hd], pb[:, :Bk], preferred_element_type=F32) / l[:, :Bk])
            out_t.append(jnp.dot(vt[hd:], pb[:, Bk:], preferred_element_type=F32) / l[:, Bk:])
            lse = (m + jnp.log2(l)) * LN2
            lse_t += [lse[:, :Bk], lse[:, Bk:]]
        lse_t.append(jnp.zeros((LANES - DIL_HEADS, Bk), F32))
        store(jnp.concatenate(out_t, axis=0).T, jnp.concatenate(lse_t, axis=0).T)

    def pair_cols(p):
        return slice(p * LANES, (p + 1) * LANES)

    n_res, n_blocks = kc_ref.shape[1], kc_ref.shape[2] // Bk
    for c in range(n_res):
        for j in range(n_blocks):
            rows = slice(j * Bk, (j + 1) * Bk)
            if not with_prev:
                load_k = lambda p, c=c: kc_ref[0, c, :, pair_cols(p)]
                load_vt = lambda p, c=c: vc_ref[0, c, pair_cols(p), :]
                load_bias = lambda p: edge_ref[0, p]
            elif j == 0:
                load_k = lambda p, c=c: jnp.concatenate([kp_ref[0, c, :, pair_cols(p)],
                                                         kc_ref[0, c, :Bk, pair_cols(p)]], axis=0)
                load_vt = lambda p, c=c: jnp.concatenate([vp_ref[0, c, pair_cols(p), :],
                                                          vc_ref[0, c, pair_cols(p), :Bk]], axis=1)
                load_bias = lambda p: edge_ref[0, p]
            else:
                krows = slice((j - 1) * Bk, (j + 1) * Bk)
                load_k = lambda p, c=c, krows=krows: kc_ref[0, c, krows, pair_cols(p)]
                load_vt = lambda p, c=c, krows=krows: vc_ref[0, c, pair_cols(p), krows]
                load_bias = lambda p: bias_ref[p]

            if rate == 1:
                def store(o, lse, rows=rows):
                    o_ref[rows, :] = o.astype(o_ref.dtype)
                    lse_ref[rows, :] = lse
            else:
                def store(o, lse, out_rows=pl.ds(c + j * Bk * rate, Bk, stride=rate)):
                    for g in range(os_ref.shape[0]):
                        os_ref[g, out_rows, :] = o[:, pair_cols(g)]
                    lse_ref[out_rows, :] = lse

            block(lambda p, c=c, rows=rows: q_ref[0, c, rows, pair_cols(p)], load_k, load_vt, load_bias, store)
    if rate > 1:
        for g in range(os_ref.shape[0]):
            o_ref[:, pair_cols(g)] = os_ref[g].astype(o_ref.dtype)


def _t5_bucket(n):
    max_exact = NUM_BUCKETS // 2
    nf = jnp.maximum(n, max_exact).astype(F32)
    large = max_exact + (jnp.log(nf / max_exact) / math.log(MAX_DISTANCE / max_exact)
                         * (NUM_BUCKETS - max_exact)).astype(jnp.int32)
    large = jnp.minimum(large, NUM_BUCKETS - 1)
    return jnp.where(n < max_exact, n, large)


def _group_attention(q, k, vt, bias_table, rate, n_steps):
    B, _, L, D = q.shape
    S = L * rate
    Bk = SUB_BLOCK
    with_prev = L > Bk
    nk = 2 * Bk if with_prev else Bk
    n = max(Bk, ATTN_TILE // rate)
    tm = n * rate
    nt = S // tm
    n_pairs = DIL_HEADS // 2
    ql = jnp.arange(Bk, dtype=jnp.int32)[:, None]
    kl = jnp.arange(2 * Bk, dtype=jnp.int32)[None, :]
    steps = ql + Bk - kl
    bucket = _t5_bucket(jnp.maximum(steps, 0) * rate)
    buckets = jnp.arange(NUM_BUCKETS, dtype=jnp.int32)
    bias = jnp.sum(jnp.where(bucket[None, :, :, None] == buckets[:, None, None, None],
                             bias_table.astype(F32)[:, None, None, :], 0.0), axis=0)
    band = ((steps >= 0) & (steps <= n_steps))[:, :, None]
    first = (kl < Bk)[:, :, None]

    def layout(t):
        t = t[:, 2 * Bk - nk:, :].transpose(1, 2, 0)
        return t.reshape(nk, n_pairs, 2 * Bk).transpose(1, 0, 2)

    bias_in = layout(jnp.where(band, bias * LOG2E, NEG_INF))
    bias_first = layout(jnp.where(band & ~first, bias * LOG2E, NEG_INF))
    edge = jnp.stack([bias_first, bias_in])

    cur_qk = pl.BlockSpec((1, rate, n, D), lambda b, i: (b, 0, i, 0))
    cur_vt = pl.BlockSpec((1, rate, D, n), lambda b, i: (b, 0, 0, i))
    per_n = n // Bk
    prev_qk = pl.BlockSpec((1, rate, Bk, D), lambda b, i: (b, 0, jnp.maximum(i * per_n - 1, 0), 0))
    prev_vt = pl.BlockSpec((1, rate, D, Bk), lambda b, i: (b, 0, 0, jnp.maximum(i * per_n - 1, 0)))
    edge_spec = pl.BlockSpec((1, n_pairs, nk, 2 * Bk), lambda b, i: (jnp.minimum(i, 1), 0, 0, 0))
    if with_prev:
        in_specs = [cur_qk, cur_qk, prev_qk, cur_vt, prev_vt, edge_spec]
        args = (q, k, k, vt, vt, edge)
    else:
        in_specs = [cur_qk, cur_qk, cur_vt, edge_spec]
        args = (q, k, vt, edge)
    if with_prev and n > Bk:
        in_specs.append(_resident((n_pairs, nk, 2 * Bk), lambda b, i: (0, 0, 0)))
        args += (bias_in,)
    return pl.pallas_call(
        functools.partial(_attn_kernel, rate=rate, with_prev=with_prev),
        grid=(B, nt),
        in_specs=in_specs,
        out_specs=[pl.BlockSpec((tm, D), lambda b, i: (b * nt + i, 0)),
                   pl.BlockSpec((tm, LANES), lambda b, i: (b * nt + i, 0))],
        out_shape=[jax.ShapeDtypeStruct((B * S, D), BF16),
                   jax.ShapeDtypeStruct((B * S, LANES), F32)],
        scratch_shapes=[pltpu.VMEM((D // LANES, tm if rate > 1 else 8, LANES), F32)],
        compiler_params=_params("parallel", "parallel"),
        name="attn_rate%d" % rate,
    )(*args)


def kernel(x, ret_w_in, ret_w_out, kv_norm, w_kv, k_norm, dil_wq, q_norm, dil_wo, rel_bias,
           mixer_norm, ffn_norm, router_grp, router_grp_b, router_exp, router_exp_b,
           exp_gate, exp_up, exp_down):
    B, S, D = x.shape
    h = x.reshape(B * S, D)

    def moe_layer(layer, mixer, w_out, h, combine):
        h1, *hn, route, route_t, counts = _post(
            mixer, w_out.astype(BF16), h, ffn_norm[layer], router_grp[layer], router_grp_b[layer],
            router_exp[layer], router_exp_b[layer])
        return _moe(h1, hn, route, route_t, counts, layer, exp_gate, exp_up, exp_down, combine)

    h = moe_layer(0, (mixer_norm[0], ret_w_in[0].astype(BF16), S), ret_w_out[0], h, combine=False)

    G = len(DIL_RATES)
    gd = DIL_HEADS * DIL_HEAD_DIM
    outs, lses = [], []
    for g in range(G):
        cq = slice(g * gd, (g + 1) * gd)
        cv = slice(G * gd + g * gd, G * gd + (g + 1) * gd)
        q, k, vt, *combined = _qkv(h, B, S, DIL_RATES[g], mixer_norm[1], kv_norm,
                                   dil_wq[0][:, cq].astype(BF16), w_kv[:, cq].astype(BF16),
                                   w_kv[:, cv].astype(BF16), q_norm[0][g], k_norm[g])
        if combined:
            h, = combined
        o, lse = _group_attention(q, k, vt, rel_bias[:, g * DIL_HEADS:(g + 1) * DIL_HEADS],
                                  DIL_RATES[g], DIL_WINDOWS[g] // DIL_RATES[g])
        outs.append(o)
        lses.append(lse)
    h = moe_layer(1, (tuple(outs), tuple(lses)), dil_wo[0], h, combine=True)
    return h.reshape(B, S, D)
```

```python
import functools
import math

import jax
import jax.numpy as jnp
from jax import lax
from jax.experimental import pallas as pl
from jax.experimental.pallas import tpu as pltpu
from jax.experimental.pallas import tpu_sc as plsc

F32 = jnp.float32
BF16 = jnp.bfloat16

EPS = 1e-6
NEG_INF = -1e30

RET_HEADS = 4
RET_CHUNK = 256
ROPE_BASE = 10000.0

DIL_WINDOWS = (128, 512, 2048)
DIL_RATES = (1, 4, 16)
DIL_HEADS = 16
DIL_HEAD_DIM = 64
SUB_BLOCK = 128
NUM_BUCKETS = 32
MAX_DISTANCE = 2048

MOE_GROUPS = 4
EXPERTS_PER_GROUP = 8
N_EXPERTS = MOE_GROUPS * EXPERTS_PER_GROUP
MOE_BLOCK = 512

LANES = 128
BF16_ROWS = 16
PERM_ROWS = 256
ROW_TILE = 1024
TOKEN_TILE = 512
ATTN_TILE = 2048
VMEM_LIMIT = 56 * 1024 * 1024

NT_DIMS = (((1,), (1,)), ((), ()))
LOG2E = math.log2(math.e)
LN2 = math.log(2.0)


def _params(*sem):
    return pltpu.CompilerParams(dimension_semantics=sem, vmem_limit_bytes=VMEM_LIMIT)


def _resident(shape, index_map):
    return pl.BlockSpec(shape, index_map, pipeline_mode=pl.Buffered(1))


def _rms(x):
    return x * lax.rsqrt(jnp.mean(x * x, axis=-1, keepdims=True) + EPS)


def _retention_tile(x_ref, g_ref, w_ref, cos_ref, sin_ref, din_ref, xi_ref, zeta_ref, cd_ref, state_ref,
                    first_tile):
    C = RET_CHUNK
    H, dk, dv = state_ref.shape
    half = dk // 2
    ts = x_ref.shape[0]
    k_scale = dk ** -0.5

    @pl.when(first_tile)
    def _():
        state_ref[...] = jnp.zeros_like(state_ref)

    xn = (_rms(x_ref[...]) * g_ref[...]).astype(BF16)
    cos, sin = cos_ref[...], sin_ref[...]

    def proj(c0, width):
        return jnp.dot(xn, w_ref[:, c0:c0 + width], preferred_element_type=F32)

    def rot(t):
        t1, t2 = t[:, :half], t[:, half:]
        return jnp.concatenate([t1 * cos - t2 * sin, t1 * sin + t2 * cos], axis=1)

    heads = []
    for hh in range(H):
        q = rot(proj(hh * dk, dk)).astype(BF16)
        k = rot(proj(H * dk + hh * dk, dk)) * k_scale
        kb = k.astype(BF16)
        v = proj(2 * H * dk + hh * dv, dv).astype(BF16)
        gate = proj(2 * H * dk + H * dv + hh * dv, dv)
        gate = gate * jax.nn.sigmoid(gate)
        chunks = []
        for c in range(ts // C):
            rows = slice(c * C, (c + 1) * C)
            s = lax.dot_general(q[rows], kb[rows], NT_DIMS, preferred_element_type=F32) * din_ref[hh]
            inner = jnp.dot(s.astype(BF16), v[rows], preferred_element_type=F32)
            state = state_ref[hh]
            cross = jnp.dot(q[rows], state.astype(BF16), preferred_element_type=F32) * xi_ref[hh]
            kz_t = (k[rows] * zeta_ref[hh]).T.astype(BF16)
            state_ref[hh] = state * cd_ref[hh] + jnp.dot(kz_t, v[rows], preferred_element_type=F32)
            chunks.append((gate[rows] * _rms(inner + cross)).astype(BF16))
        heads.append(jnp.concatenate(chunks, axis=0))
    return jnp.concatenate(heads, axis=1)


def _retention_operands(gain, w, S, ts):
    H, C = RET_HEADS, RET_CHUNK
    D = w.shape[0]
    dk = D // H
    half = dk // 2
    nt = S // ts
    pos = jnp.arange(S, dtype=F32)
    inv = 1.0 / (ROPE_BASE ** jnp.linspace(0.0, 1.0, half, dtype=F32))
    ang = pos[:, None] * inv[None, :]
    log_g = jnp.log(1.0 - 2.0 ** (-5.0 - jnp.arange(H, dtype=F32)))
    idx = jnp.arange(C, dtype=F32)
    diff = idx[:, None] - idx[None, :]
    d_in = jnp.where(diff >= 0, jnp.exp(log_g[:, None, None] * jnp.maximum(diff, 0.0)), 0.0)
    xi = jnp.exp(log_g[:, None] * (idx + 1.0))[:, :, None]
    zeta = jnp.exp(log_g[:, None] * (C - 1.0 - idx))[:, :, None]
    chunk_decay = jnp.exp(log_g * C)[:, None, None]
    table = pl.BlockSpec((ts, half), lambda i: (i % nt, 0))
    per_head = lambda shape: _resident((H,) + shape, lambda i: (0, 0, 0))
    args = (gain.reshape(1, D), w, jnp.cos(ang), jnp.sin(ang), d_in, xi, zeta, chunk_decay)
    specs = [_resident((1, D), lambda i: (0, 0)), _resident(w.shape, lambda i: (0, 0)), table, table,
             per_head((C, C)), per_head((C, 1)), per_head((C, 1)), per_head((1, 1))]
    state = pltpu.VMEM((H, dk, 2 * dk), F32)
    return args, specs, state


ROUTE_E1, ROUTE_E2, ROUTE_G1, ROUTE_G2, ROUTE_R1, ROUTE_R2 = range(6)
ROUTE_ROWS = 8
ROUTE_PIECES = 2
ROUTER_EXP_LANE0 = MOE_GROUPS
HI16 = 0xFFFF0000


def _pack_bf16_pair(a, b):
    ua = lax.bitcast_convert_type(a.astype(BF16).astype(F32), jnp.uint32)
    ub = lax.bitcast_convert_type(b.astype(BF16).astype(F32), jnp.uint32)
    return ua | (ub >> 16)


def _unpack_bf16_pair(w):
    a = lax.bitcast_convert_type(w & jnp.uint32(HI16), F32)
    b = lax.bitcast_convert_type(w << 16, F32)
    return a, b


ROW_PARTS = 2


def _part_cols(D, part):
    w = D // 2 // ROW_PARTS
    return slice(part * w, (part + 1) * w), slice(D // 2 + part * w, D // 2 + (part + 1) * w)


def _pack_part(x, part):
    hi, lo = _part_cols(x.shape[1], part)
    return _pack_bf16_pair(x[:, hi], x[:, lo])


def _merge_groups(o_refs, l_refs, ex_ref):
    ex = ex_ref[...]

    def expand(w):
        return jnp.dot(w.astype(BF16), ex, preferred_element_type=F32)

    lses = [l_ref[...] for l_ref in l_refs]
    top = functools.reduce(jnp.maximum, lses)
    ws = [jnp.exp(l - top) for l in lses]
    den = sum(ws)
    last = o_refs[-1][...].astype(F32)
    return (last + sum(expand(w / den) * (o_ref[...].astype(F32) - last)
                       for w, o_ref in zip(ws[:-1], o_refs[:-1]))).astype(BF16)


def _post_kernel(*refs, n_groups, seq_tiles):
    n_a = 2 * n_groups + 1 if n_groups else 8
    a_refs = refs[:n_a]
    (w_ref, h_ref, g_ref, wr_ref, br_ref, tri_ref,
     h1_ref, hn0_ref, hn1_ref, route_ref, route_t_ref, cnt_ref, carry_ref) = refs[n_a:n_a + 13]
    hn_refs = (hn0_ref, hn1_ref)
    if n_groups:
        a = _merge_groups(a_refs[:n_groups], a_refs[n_groups:2 * n_groups], a_refs[-1])
    else:
        a = _retention_tile(h_ref, *a_refs, refs[-1], pl.program_id(0) % seq_tiles == 0)
    @pl.when(pl.program_id(0) == 0)
    def _():
        carry_ref[...] = jnp.zeros_like(carry_ref)

    half = tri_ref.shape[0]
    n_pieces = h_ref.shape[0] // half
    D = h_ref.shape[1]
    lane = lax.broadcasted_iota(jnp.int32, (half, LANES), 1).astype(F32)
    ninf = -jnp.inf

    def first_argmax(vals):
        top = jnp.max(vals, axis=1, keepdims=True)
        where = jnp.min(jnp.where(vals == top, lane, float(LANES)), axis=1, keepdims=True)
        return top, where

    h1_ref[...] = h_ref[...] + jnp.dot(a, w_ref[...], preferred_element_type=F32)

    def route_half(rows):
        hn = _rms(h1_ref[rows, :]) * g_ref[...]
        for part, ref in enumerate(hn_refs):
            ref[rows, :] = _pack_part(hn, part)
        hi = hn.astype(BF16)
        lo = (hn - hi.astype(F32)).astype(BF16)
        both = jnp.dot(hi, wr_ref[...], preferred_element_type=F32)
        logits = (both[:, :LANES] + both[:, LANES:]
                  + jnp.dot(lo, wr_ref[:, :LANES], preferred_element_type=F32) + br_ref[...])

        is_grp = lane < MOE_GROUPS
        lg = jnp.where(is_grp, logits, ninf)
        mg, grp = first_argmax(lg)
        p_grp = 1.0 / jnp.sum(jnp.where(is_grp, jnp.exp(lg - mg), 0.0), axis=1, keepdims=True)

        e_lane = lane - ROUTER_EXP_LANE0
        in_grp = (e_lane < N_EXPERTS) & (jnp.floor(e_lane * (1.0 / EXPERTS_PER_GROUP)) == grp)
        le = jnp.where(in_grp, logits, ninf)
        v1, i1 = first_argmax(le)
        le2 = jnp.where(lane == i1, ninf, le)
        v2, i2 = first_argmax(le2)
        e = jnp.exp(v2 - v1)
        hit1 = lane == (i1 - ROUTER_EXP_LANE0)
        hit2 = lane == (i2 - ROUTER_EXP_LANE0)
        onehot = jnp.where(hit1 | hit2, 1.0, 0.0)
        earlier = jnp.dot(tri_ref[...], onehot.astype(BF16), preferred_element_type=F32)
        return dict(e1=i1 - ROUTER_EXP_LANE0, e2=i2 - ROUTER_EXP_LANE0, g1=p_grp / (1.0 + e),
                    g2=p_grp * e / (1.0 + e), hit1=hit1, hit2=hit2, earlier=earlier,
                    count=jnp.sum(onehot, axis=0, keepdims=True))

    halves = [route_half(slice(j * half, (j + 1) * half)) for j in range(n_pieces)]
    carry = carry_ref[...]
    for j, r in enumerate(halves):
        before = carry + r["earlier"]
        r1 = jnp.sum(jnp.where(r["hit1"], before, 0.0), axis=1, keepdims=True)
        r2 = jnp.sum(jnp.where(r["hit2"], before, 0.0), axis=1, keepdims=True)
        carry = carry + r["count"]
        route = jnp.zeros((half, LANES), F32)
        for slot, val in ((ROUTE_E1, r["e1"]), (ROUTE_E2, r["e2"]), (ROUTE_G1, r["g1"]),
                          (ROUTE_G2, r["g2"]), (ROUTE_R1, r1), (ROUTE_R2, r2)):
            route = jnp.where(lane == slot, val, route)
        route_ref[j * half:(j + 1) * half, :] = route
        route_t_ref[:, j * half:(j + 1) * half] = route.T[:ROUTE_ROWS]
    carry_ref[...] = carry
    cnt_ref[...] = carry


def _post(mixer, w, h, gain, w_grp, b_grp, w_exp, b_exp):
    T, D = h.shape
    K = w.shape[0]
    row_block = lambda width: pl.BlockSpec((tm, width), lambda i: (i, 0))
    scratch = [pltpu.VMEM((1, LANES), F32)]
    if len(mixer) == 2:
        outs, lses = mixer
        n_groups, seq_tiles = len(outs), 1
        tm = min(TOKEN_TILE, T)
        ex = jnp.where(jnp.arange(LANES)[:, None] == (jnp.arange(K)[None, :] // DIL_HEAD_DIM), 1.0, 0.0).astype(BF16)
        a_args = (*outs, *lses, ex)
        a_specs = [row_block(K)] * n_groups + [row_block(LANES)] * n_groups + [_resident((LANES, K), lambda i: (0, 0))]
    else:
        mix_gain, w_in, S = mixer
        tm = min(TOKEN_TILE, S)
        n_groups, seq_tiles = 0, S // tm
        a_args, a_specs, state = _retention_operands(mix_gain, w_in, S, tm)
        scratch.append(state)
    n_r = MOE_GROUPS + N_EXPERTS
    wr = jnp.zeros((D, LANES), F32).at[:, :n_r].set(jnp.concatenate([w_grp, w_exp], axis=1))
    wr_hi = wr.astype(BF16)
    wr_lo = (wr - wr_hi.astype(F32)).astype(BF16)
    wr2 = jnp.concatenate([wr_hi, wr_lo], axis=1)
    br = jnp.zeros((1, LANES), F32).at[0, :n_r].set(jnp.concatenate([b_grp, b_exp]))
    piece = tm // ROUTE_PIECES
    tri = jnp.tril(jnp.ones((piece, piece), BF16), k=-1)
    return pl.pallas_call(
        functools.partial(_post_kernel, n_groups=n_groups, seq_tiles=seq_tiles),
        grid=(T // tm,),
        in_specs=a_specs + [
                  _resident((K, D), lambda i: (0, 0)),
                  pl.BlockSpec((tm, D), lambda i: (i, 0)),
                  _resident((1, D), lambda i: (0, 0)),
                  _resident((D, 2 * LANES), lambda i: (0, 0)),
                  _resident((1, LANES), lambda i: (0, 0)),
                  _resident((piece, piece), lambda i: (0, 0))],
        out_specs=[pl.BlockSpec((tm, D), lambda i: (i, 0))]
                  + [pl.BlockSpec((tm, D // 2 // ROW_PARTS), lambda i: (i, 0))] * ROW_PARTS
                  + [pl.BlockSpec((tm, LANES), lambda i: (i, 0)),
                   pl.BlockSpec((ROUTE_ROWS, tm), lambda i: (0, i)),
                   pl.BlockSpec((1, LANES), lambda i: (0, 0))],
        out_shape=[jax.ShapeDtypeStruct((T, D), F32)]
                  + [jax.ShapeDtypeStruct((T, D // 2 // ROW_PARTS), jnp.uint32)] * ROW_PARTS
                  + [jax.ShapeDtypeStruct((T, LANES), F32),
                   jax.ShapeDtypeStruct((ROUTE_ROWS, T), F32),
                   jax.ShapeDtypeStruct((1, LANES), F32)],
        scratch_shapes=scratch,
        compiler_params=_params("arbitrary"),
        name="post_mixer",
    )(*a_args, w, h, gain.reshape(1, D), wr2, br, tri)


SC_WINDOW = 128


def _sc_mesh():
    return plsc.VectorSubcoreMesh(core_axis_name="core", subcore_axis_name="subcore")


def _sc_window_specs(W):
    rows = pl.BlockSpec((SC_WINDOW, W), lambda i: (i, 0))
    idx = pl.BlockSpec((1, SC_WINDOW), lambda i: (0, i))
    return rows, idx


def _sc_pipeline(body, n_rows, in_specs, out_specs):
    return pltpu.emit_pipeline(body, grid=(n_rows // SC_WINDOW,), in_specs=in_specs, out_specs=out_specs,
                               core_axis_name=("core", "subcore"), dimension_semantics=(pltpu.PARALLEL,))


def _dispatch(xs, dest, pad_idx, P):
    T, W = xs[0].shape
    n_pad = pad_idx.shape[1]
    n = len(xs)
    rows, idx = _sc_window_specs(W)
    zero_rows = pl.BlockSpec((SC_WINDOW, W), lambda i: (0, 0))
    out = jax.ShapeDtypeStruct((P, W), xs[0].dtype)

    @functools.partial(pl.kernel, out_type=(out,) * n, mesh=_sc_mesh(), scratch_types=[], name="dispatch")
    def scatter(*refs):
        x_hbm, (d0_hbm, d1_hbm, z_hbm, p_hbm), o_hbm = refs[:n], refs[n:n + 4], refs[n + 4:]
        for x, o in zip(x_hbm, o_hbm):
            def put_pair(x_vmem, i0_vmem, i1_vmem, o=o):
                pltpu.sync_copy(x_vmem, o.at[i0_vmem.at[0]])
                pltpu.sync_copy(x_vmem, o.at[i1_vmem.at[0]])

            def put(x_vmem, i_vmem, o=o):
                pltpu.sync_copy(x_vmem, o.at[i_vmem.at[0]])

            _sc_pipeline(put_pair, T, [rows, idx, idx], [])(x, d0_hbm, d1_hbm)
            _sc_pipeline(put, n_pad, [zero_rows, idx], [])(z_hbm, p_hbm)

    return scatter(*xs, dest[0:1], dest[1:2], jnp.zeros((SC_WINDOW, W), xs[0].dtype), pad_idx)


WEIGHT_LEADS = (3, 2, 1)
EXPERT_SLOTS = max(WEIGHT_LEADS) + 1


def _expert_kernel(blk_exp_ref, slot_ref, n_used_ref, *refs):
    x_refs = refs[:ROW_PARTS]
    w_refs = refs[ROW_PARTS:ROW_PARTS + 3]
    o_refs = refs[ROW_PARTS + 3:2 * ROW_PARTS + 3]
    w_slots = refs[2 * ROW_PARTS + 3:]
    lead = max(WEIGHT_LEADS)
    n_used = n_used_ref[0]
    i = pl.program_id(0) - lead
    D = w_slots[0].shape[1]

    for w_ref, w_s, ahead in zip(w_refs, w_slots, WEIGHT_LEADS):
        j = i + ahead
        jc = jnp.clip(j, 0, n_used - 1)
        arrived = (j >= 0) & (j < n_used) & ((j == 0) | (blk_exp_ref[jc] != blk_exp_ref[jnp.maximum(jc - 1, 0)]))

        @pl.when(arrived)
        def _(w_ref=w_ref, w_s=w_s, jc=jc):
            w_s[slot_ref[jc]] = w_ref[0, 0].astype(BF16)

    @pl.when((i >= 0) & (i < n_used))
    def _():
        slot = slot_ref[jnp.clip(i, 0, n_used - 1)]
        wg_s, wu_s, wd_s = w_slots
        pieces = []
        for part, x_ref in enumerate(x_refs):
            for cols, val in zip(_part_cols(D, part), _unpack_bf16_pair(x_ref[...])):
                pieces.append((cols, val.astype(BF16)))

        def up(w_s):
            return sum(jnp.dot(val, w_s[slot, cols, :], preferred_element_type=F32) for cols, val in pieces)

        g = up(wg_s)
        hid = (g * jax.nn.sigmoid(g) * up(wu_s)).astype(BF16)
        y = jnp.dot(hid, wd_s[slot], preferred_element_type=F32)
        for part, o_ref in enumerate(o_refs):
            o_ref[...] = _pack_part(y, part)

    @pl.when(i >= n_used)
    def _():
        for o_ref in o_refs:
            o_ref[...] = jnp.zeros_like(o_ref)


def _experts(xs, blk_exp, n_used, layer, w_gate, w_up, w_down):
    P, W = xs[0].shape
    D = 2 * W * ROW_PARTS
    FF = w_gate.shape[3]
    nblk = P // MOE_BLOCK
    lead = max(WEIGHT_LEADS)
    changes = jnp.concatenate([jnp.zeros((1,), jnp.int32), (blk_exp[1:] != blk_exp[:-1]).astype(jnp.int32)])
    slot = jnp.cumsum(changes) % EXPERT_SLOTS

    def x_map(g, be, sl, nu):
        return (jnp.clip(g - lead, 0, nu[0] - 1), 0)

    def o_map(g, be, sl, nu):
        return (jnp.maximum(g - lead, 0), 0)

    def w_map(ahead):
        return lambda g, be, sl, nu: (layer, be[jnp.clip(g - lead + ahead, 0, nu[0] - 1)], 0, 0)

    wg_spec, wu_spec, wd_spec = (pl.BlockSpec((1, 1) + shape, w_map(ahead))
                                 for shape, ahead in zip(((D, FF), (D, FF), (FF, D)), WEIGHT_LEADS))
    return pl.pallas_call(
        _expert_kernel,
        grid_spec=pltpu.PrefetchScalarGridSpec(
            num_scalar_prefetch=3,
            grid=(nblk + lead,),
            in_specs=[pl.BlockSpec((MOE_BLOCK, W), x_map)] * ROW_PARTS + [wg_spec, wu_spec, wd_spec],
            out_specs=[pl.BlockSpec((MOE_BLOCK, W), o_map)] * ROW_PARTS,
            scratch_shapes=[pltpu.VMEM((EXPERT_SLOTS, D, FF), BF16), pltpu.VMEM((EXPERT_SLOTS, D, FF), BF16),
                            pltpu.VMEM((EXPERT_SLOTS, FF, D), BF16)]),
        out_shape=[jax.ShapeDtypeStruct((P, W), jnp.uint32)] * ROW_PARTS,
        compiler_params=_params("arbitrary"),
        name="experts",
    )(blk_exp, slot, n_used, *xs, w_gate, w_up, w_down)


def _gather_pairs(ys, dest):
    W = ys[0].shape[1]
    T = dest.shape[1]
    n = len(ys)
    rows, idx = _sc_window_specs(W)
    out = jax.ShapeDtypeStruct((T, W), ys[0].dtype)

    @functools.partial(pl.kernel, out_type=(out,) * (2 * n), mesh=_sc_mesh(), scratch_types=[],
                       name="gather_pairs")
    def gather(*refs):
        y_hbm, d_hbm, o_hbm = refs[:n], refs[n:n + 2], refs[n + 2:]
        for slot, d in enumerate(d_hbm):
            for y, o in zip(y_hbm, o_hbm[slot * n:(slot + 1) * n]):
                def get(i_vmem, o_vmem, y=y):
                    pltpu.sync_copy(y.at[i_vmem.at[0]], o_vmem)

                _sc_pipeline(get, T, [idx], [rows])(d, o)

    return gather(*ys, dest[0:1], dest[1:2])


def _combined_rows(h_ref, route_ref, y_refs):
    D = h_ref.shape[1]
    route = route_ref[...]
    gates = (route[:, ROUTE_G1:ROUTE_G1 + 1], route[:, ROUTE_G2:ROUTE_G2 + 1])
    pieces = {}
    for part in range(ROW_PARTS):
        slots = [_unpack_bf16_pair(y_refs[slot * ROW_PARTS + part][...]) for slot in range(2)]
        for half, cols in enumerate(_part_cols(D, part)):
            pieces[cols.start] = h_ref[:, cols] + (gates[0] * slots[0][half] + gates[1] * slots[1][half])
    return jnp.concatenate([pieces[c] for c in sorted(pieces)], axis=1)


def _combine_kernel(h_ref, route_ref, *refs):
    refs[-1][...] = _combined_rows(h_ref, route_ref, refs[:-1])


def _combine(h, route, pairs):
    T, D = h.shape
    W = pairs[0].shape[1]
    tm = min(ROW_TILE, T)
    return pl.pallas_call(
        _combine_kernel,
        grid=(T // tm,),
        in_specs=[pl.BlockSpec((tm, D), lambda i: (i, 0)), pl.BlockSpec((tm, LANES), lambda i: (i, 0))]
                 + [pl.BlockSpec((tm, W), lambda i: (i, 0))] * len(pairs),
        out_specs=pl.BlockSpec((tm, D), lambda i: (i, 0)),
        out_shape=jax.ShapeDtypeStruct((T, D), F32),
        compiler_params=_params("parallel"),
        name="combine",
    )(h, route, *pairs)


def _moe(h1, hn, route, route_t, counts, layer, w_gate, w_up, w_down, combine):
    T, D = h1.shape
    A = 2 * T
    nblk = -(-A // MOE_BLOCK) + N_EXPERTS
    P = nblk * MOE_BLOCK
    eid = route_t[ROUTE_E1:ROUTE_E2 + 1].astype(jnp.int32)
    rank = route_t[ROUTE_R1:ROUTE_R2 + 1].astype(jnp.int32)
    cnt = counts[0, :N_EXPERTS].astype(jnp.int32)
    padded = (cnt + MOE_BLOCK - 1) // MOE_BLOCK * MOE_BLOCK
    pends = jnp.cumsum(padded)
    pstarts = pends - padded
    experts = jnp.arange(N_EXPERTS, dtype=jnp.int32)
    start_of = jnp.sum(jnp.where(eid[:, None, :] == experts[None, :, None], pstarts[None, :, None], 0), axis=1)
    dest = start_of + rank
    blk_start = jnp.arange(nblk, dtype=jnp.int32) * MOE_BLOCK
    blk_exp = jnp.minimum(jnp.sum((pends[None, :] <= blk_start[:, None]).astype(jnp.int32), axis=1),
                          N_EXPERTS - 1)
    n_used = pends[-1:] // MOE_BLOCK

    gap_start = jnp.concatenate([pstarts + cnt, pends[-1:]])
    gap_len = jnp.concatenate([padded - cnt, P - pends[-1:]])
    gap_end = jnp.cumsum(gap_len)
    j = jnp.arange(P - A, dtype=jnp.int32)
    gap_of = jnp.sum((gap_end[None, :] <= j[:, None]).astype(jnp.int32), axis=1)
    sel = gap_of[:, None] == jnp.arange(N_EXPERTS + 1, dtype=jnp.int32)[None, :]
    pad_idx = jnp.sum(jnp.where(sel, (gap_start - (gap_end - gap_len))[None, :] + j[:, None], 0), axis=1)

    xs = _dispatch(hn, dest, pad_idx.reshape(1, P - A), P)
    ys = _experts(xs, blk_exp, n_used, layer, w_gate, w_up, w_down)
    pairs = _gather_pairs(ys, dest)
    return _combine(h1, route, pairs) if combine else (h1, route, pairs)


def _qkv_kernel(*refs, rate, n, n_chunks, n_pending, permute):
    n_x = 2 + n_pending if n_pending else 1
    x_refs = refs[:n_x]
    gq_ref, gkv_ref, wq_ref, wk_ref, wvt_ref, qn_ref, kn_ref, seg_ref = refs[n_x:n_x + 8]
    perm_ref = refs[n_x + 8] if permute else None
    q_ref, k_ref, vt_ref = refs[n_x + 8 + permute:n_x + 11 + permute]
    xs_ref = refs[-1]
    seg = seg_ref[...]
    width = seg.shape[0]
    res_per_chunk = q_ref.shape[1]
    chunk = pl.program_id(2)

    def head_norm(t, gain):
        cols = []
        for j in range(t.shape[1] // width):
            tj = t[:, j * width:(j + 1) * width]
            ms = jnp.dot((tj * tj).astype(BF16), seg, preferred_element_type=F32)
            cols.append(tj * lax.rsqrt(ms + EPS))
        return jnp.concatenate(cols, axis=1) * gain

    def normed(x):
        y = _rms(x)
        return (y * gq_ref[...]).astype(BF16), (y * gkv_ref[...]).astype(BF16)

    def project(xq, xkv):
        q = head_norm(jnp.dot(xq, wq_ref[...], preferred_element_type=F32), qn_ref[...]).astype(q_ref.dtype)
        k = head_norm(jnp.dot(xkv, wk_ref[...], preferred_element_type=F32), kn_ref[...]).astype(k_ref.dtype)
        vt = lax.dot_general(wvt_ref[...], xkv, NT_DIMS, preferred_element_type=F32).astype(vt_ref.dtype)
        for j in range(res_per_chunk):
            q_ref[0, j] = q[j * n:(j + 1) * n]
            k_ref[0, j] = k[j * n:(j + 1) * n]
            vt_ref[0, j] = vt[:, j * n:(j + 1) * n]

    if n_pending:
        x = _combined_rows(x_refs[0], x_refs[1], x_refs[2:])
        refs[-2][...] = x
        project(*normed(x))
        return
    x_ref, = x_refs
    if rate == 1:
        project(*normed(x_ref[...]))
        return

    if permute:
        group = perm_ref.shape[0]
        n_groups, run = x_ref.shape[0] // group, group // rate

        @pl.when(chunk == 0)
        def _():
            for g in range(n_groups):
                rows = slice(g * group, (g + 1) * group)
                for t, xt in enumerate(normed(x_ref[rows, :])):
                    xs_ref[t, rows, :] = jnp.dot(perm_ref[...], xt, preferred_element_type=F32).astype(BF16)

        span = res_per_chunk * run
        start = pl.multiple_of(chunk * span, span)

        def residue_major(t):
            parts = [xs_ref[t, pl.ds(g * group + start, span), :] for g in range(n_groups)]
            return jnp.concatenate([p[j * run:(j + 1) * run] for j in range(res_per_chunk) for p in parts], axis=0)

        project(residue_major(0), residue_major(1))
        return

    @pl.when(chunk == 0)
    def _():
        for j in range(xs_ref.shape[0]):
            xs_ref[j] = x_ref[:, j * LANES:(j + 1) * LANES]

    for ch in range(n_chunks):
        @pl.when(chunk == ch)
        def _(ch=ch):
            residues = range(ch * res_per_chunk, (ch + 1) * res_per_chunk)
            project(*normed(jnp.concatenate(
                [jnp.concatenate([xs_ref[j, pl.ds(c, n, stride=rate), :] for j in range(xs_ref.shape[0])], axis=1)
                 for c in residues], axis=0)))


def _qkv(h, B, S, rate, gq, gkv, wq, wk, wv, qn, kn):
    pending = h if isinstance(h, tuple) else None
    D = (pending[0] if pending else h).shape[1]
    L = S // rate
    hd = DIL_HEAD_DIM
    n = max(SUB_BLOCK, TOKEN_TILE // rate)
    tm = n * rate
    res_per_chunk = max(1, TOKEN_TILE // n)
    n_chunks = rate // res_per_chunk
    width = 2 * LANES
    ii = jnp.arange(width)
    seg = jnp.where((ii[:, None] // hd) == (ii[None, :] // hd), 1.0 / hd, 0.0).astype(BF16)
    permute = rate * BF16_ROWS == PERM_ROWS
    if permute:
        pp = jnp.arange(PERM_ROWS)
        perm = ((pp[:, None] // BF16_ROWS == pp[None, :] % rate)
                & (pp[:, None] % BF16_ROWS == pp[None, :] // rate)).astype(BF16)
    row = lambda g: jnp.tile(g, D // hd).reshape(1, D)
    const = lambda shape: _resident(shape, lambda b, i, c: (0,) * len(shape))
    qk_spec = pl.BlockSpec((1, res_per_chunk, n, D), lambda b, i, c: (b, c, i, 0))
    qk_shape = jax.ShapeDtypeStruct((B, rate, L, D), BF16)
    rows = lambda width: pl.BlockSpec((tm, width), lambda b, i, c: (b * (S // tm) + i, 0))
    if pending:
        assert rate == 1
        h1, route, pairs = pending
        x_args = (h1, route, *pairs)
        x_specs = [rows(D), rows(LANES)] + [rows(pairs[0].shape[1])] * len(pairs)
        extra_specs, extra_shapes = [rows(D)], [jax.ShapeDtypeStruct((B * S, D), F32)]
    else:
        x_args, x_specs, extra_specs, extra_shapes = (h,), [rows(D)], [], []
    return pl.pallas_call(
        functools.partial(_qkv_kernel, rate=rate, n=n, n_chunks=n_chunks,
                          n_pending=len(pending[2]) if pending else 0, permute=permute),
        grid=(B, S // tm, n_chunks),
        in_specs=x_specs + [
                  const((1, D)), const((1, D)), const((D, D)), const((D, D)), const((D, D)),
                  const((1, D)), const((1, D)), const((width, width))]
                 + ([const((PERM_ROWS, PERM_ROWS))] if permute else []),
        out_specs=[qk_spec, qk_spec,
                   pl.BlockSpec((1, res_per_chunk, D, n), lambda b, i, c: (b, c, 0, i))] + extra_specs,
        out_shape=[qk_shape, qk_shape, jax.ShapeDtypeStruct((B, rate, D, L), BF16)] + extra_shapes,
        scratch_shapes=[pltpu.VMEM((2, tm, D), BF16) if permute else
                        pltpu.VMEM((D // LANES, tm if rate > 1 else 8, LANES), F32)],
        compiler_params=_params("parallel", "parallel", "arbitrary"),
        name="qkv_rate%d" % rate,
    )(*x_args, gq.reshape(1, D), gkv.reshape(1, D), wq, wk, wv.T, row(qn) * (hd ** -0.5 * LOG2E), row(kn), seg,
      *((perm,) if permute else ()))


def _attn_kernel(*refs, rate, with_prev):
    if not with_prev:
        q_ref, kc_ref, vc_ref, edge_ref, o_ref, lse_ref, os_ref = refs
    elif len(refs) == 10:
        q_ref, kc_ref, kp_ref, vc_ref, vp_ref, edge_ref, bias_ref, o_ref, lse_ref, os_ref = refs
    else:
        q_ref, kc_ref, kp_ref, vc_ref, vp_ref, edge_ref, o_ref, lse_ref, os_ref = refs
    Bk = SUB_BLOCK
    hd = DIL_HEAD_DIM
    n_pairs = DIL_HEADS // 2
    lane = lax.broadcasted_iota(jnp.int32, (Bk, LANES), 1)

    def block(load_q, load_k, load_vt, load_bias, store):
        out_t, lse_t = [], []
        for p in range(n_pairs):
            qp = load_q(p)
            zero = jnp.zeros_like(qp)
            q2 = jnp.concatenate([jnp.where(lane < hd, qp, zero), jnp.where(lane >= hd, qp, zero)], axis=0)
            s = lax.dot_general(load_k(p), q2, NT_DIMS, preferred_element_type=F32) + load_bias(p)
            m = jnp.max(s, axis=0, keepdims=True)
            pr = jnp.exp2(s - m)
            l = jnp.sum(pr, axis=0, keepdims=True)
            pb = pr.astype(BF16)
            vt = load_vt(p)
            out_t.append(jnp.dot(vt[:hd], pb[:, :Bk], preferred_element_type=F32) / l[:, :Bk])
            out_t.append(jnp.dot(vt[hd:], pb[:, Bk:], preferred_element_type=F32) / l[:, Bk:])
            lse = (m + jnp.log2(l)) * LN2
            lse_t += [lse[:, :Bk], lse[:, Bk:]]
        lse_t.append(jnp.zeros((LANES - DIL_HEADS, Bk), F32))
        store(jnp.concatenate(out_t, axis=0).T, jnp.concatenate(lse_t, axis=0).T)

    def pair_cols(p):
        return slice(p * LANES, (p + 1) * LANES)

    n_res, n_blocks = kc_ref.shape[1], kc_ref.shape[2] // Bk
    for c in range(n_res):
        for j in range(n_blocks):
            rows = slice(j * Bk, (j + 1) * Bk)
            if not with_prev:
                load_k = lambda p, c=c: kc_ref[0, c, :, pair_cols(p)]
                load_vt = lambda p, c=c: vc_ref[0, c, pair_cols(p), :]
                load_bias = lambda p: edge_ref[0, p]
            elif j == 0:
                load_k = lambda p, c=c: jnp.concatenate([kp_ref[0, c, :, pair_cols(p)],
                                                         kc_ref[0, c, :Bk, pair_cols(p)]], axis=0)
                load_vt = lambda p, c=c: jnp.concatenate([vp_ref[0, c, pair_cols(p), :],
                                                          vc_ref[0, c, pair_cols(p), :Bk]], axis=1)
                load_bias = lambda p: edge_ref[0, p]
            else:
                krows = slice((j - 1) * Bk, (j + 1) * Bk)
                load_k = lambda p, c=c, krows=krows: kc_ref[0, c, krows, pair_cols(p)]
                load_vt = lambda p, c=c, krows=krows: vc_ref[0, c, pair_cols(p), krows]
                load_bias = lambda p: bias_ref[p]

            if rate == 1:
                def store(o, lse, rows=rows):
                    o_ref[rows, :] = o.astype(o_ref.dtype)
                    lse_ref[rows, :] = lse
            else:
                def store(o, lse, out_rows=pl.ds(c + j * Bk * rate, Bk, stride=rate)):
                    for g in range(os_ref.shape[0]):
                        os_ref[g, out_rows, :] = o[:, pair_cols(g)]
                    lse_ref[out_rows, :] = lse

            block(lambda p, c=c, rows=rows: q_ref[0, c, rows, pair_cols(p)], load_k, load_vt, load_bias, store)
    if rate > 1:
        for g in range(os_ref.shape[0]):
            o_ref[:, pair_cols(g)] = os_ref[g].astype(o_ref.dtype)


def _t5_bucket(n):
    max_exact = NUM_BUCKETS // 2
    nf = jnp.maximum(n, max_exact).astype(F32)
    large = max_exact + (jnp.log(nf / max_exact) / math.log(MAX_DISTANCE / max_exact)
                         * (NUM_BUCKETS - max_exact)).astype(jnp.int32)
    large = jnp.minimum(large, NUM_BUCKETS - 1)
    return jnp.where(n < max_exact, n, large)


def _group_attention(q, k, vt, bias_table, rate, n_steps):
    B, _, L, D = q.shape
    S = L * rate
    Bk = SUB_BLOCK
    with_prev = L > Bk
    nk = 2 * Bk if with_prev else Bk
    n = max(Bk, ATTN_TILE // rate)
    tm = n * rate
    nt = S // tm
    n_pairs = DIL_HEADS // 2
    ql = jnp.arange(Bk, dtype=jnp.int32)[:, None]
    kl = jnp.arange(2 * Bk, dtype=jnp.int32)[None, :]
    steps = ql + Bk - kl
    bucket = _t5_bucket(jnp.maximum(steps, 0) * rate)
    buckets = jnp.arange(NUM_BUCKETS, dtype=jnp.int32)
    bias = jnp.sum(jnp.where(bucket[None, :, :, None] == buckets[:, None, None, None],
                             bias_table.astype(F32)[:, None, None, :], 0.0), axis=0)
    band = ((steps >= 0) & (steps <= n_steps))[:, :, None]
    first = (kl < Bk)[:, :, None]

    def layout(t):
        t = t[:, 2 * Bk - nk:, :].transpose(1, 2, 0)
        return t.reshape(nk, n_pairs, 2 * Bk).transpose(1, 0, 2)

    bias_in = layout(jnp.where(band, bias * LOG2E, NEG_INF))
    bias_first = layout(jnp.where(band & ~first, bias * LOG2E, NEG_INF))
    edge = jnp.stack([bias_first, bias_in])

    cur_qk = pl.BlockSpec((1, rate, n, D), lambda b, i: (b, 0, i, 0))
    cur_vt = pl.BlockSpec((1, rate, D, n), lambda b, i: (b, 0, 0, i))
    per_n = n // Bk
    prev_qk = pl.BlockSpec((1, rate, Bk, D), lambda b, i: (b, 0, jnp.maximum(i * per_n - 1, 0), 0))
    prev_vt = pl.BlockSpec((1, rate, D, Bk), lambda b, i: (b, 0, 0, jnp.maximum(i * per_n - 1, 0)))
    edge_spec = pl.BlockSpec((1, n_pairs, nk, 2 * Bk), lambda b, i: (jnp.minimum(i, 1), 0, 0, 0))
    if with_prev:
        in_specs = [cur_qk, cur_qk, prev_qk, cur_vt, prev_vt, edge_spec]
        args = (q, k, k, vt, vt, edge)
    else:
        in_specs = [cur_qk, cur_qk, cur_vt, edge_spec]
        args = (q, k, vt, edge)
    if with_prev and n > Bk:
        in_specs.append(_resident((n_pairs, nk, 2 * Bk), lambda b, i: (0, 0, 0)))
        args += (bias_in,)
    return pl.pallas_call(
        functools.partial(_attn_kernel, rate=rate, with_prev=with_prev),
        grid=(B, nt),
        in_specs=in_specs,
        out_specs=[pl.BlockSpec((tm, D), lambda b, i: (b * nt + i, 0)),
                   pl.BlockSpec((tm, LANES), lambda b, i: (b * nt + i, 0))],
        out_shape=[jax.ShapeDtypeStruct((B * S, D), BF16),
                   jax.ShapeDtypeStruct((B * S, LANES), F32)],
        scratch_shapes=[pltpu.VMEM((D // LANES, tm if rate > 1 else 8, LANES), F32)],
        compiler_params=_params("parallel", "parallel"),
        name="attn_rate%d" % rate,
    )(*args)


def _cast_kernel(x_ref, o_ref):
    o_ref[...] = x_ref[...].astype(o_ref.dtype)


def _to_bf16(w):
    rows, cols = w.shape
    block = min(ROW_TILE // 4, rows)
    assert rows % block == 0
    spec = pl.BlockSpec((block, cols), lambda i: (i, 0))
    return pl.pallas_call(
        _cast_kernel, grid=(rows // block,), in_specs=[spec], out_specs=spec,
        out_shape=jax.ShapeDtypeStruct(w.shape, BF16),
        compiler_params=_params("parallel"), name="to_bf16",
    )(w)


def kernel(x, ret_w_in, ret_w_out, kv_norm, w_kv, k_norm, dil_wq, q_norm, dil_wo, rel_bias,
           mixer_norm, ffn_norm, router_grp, router_grp_b, router_exp, router_exp_b,
           exp_gate, exp_up, exp_down):
    B, S, D = x.shape
    h = x.reshape(B * S, D)

    def moe_layer(layer, mixer, w_out, h, combine):
        h1, *hn, route, route_t, counts = _post(
            mixer, w_out.astype(BF16), h, ffn_norm[layer], router_grp[layer], router_grp_b[layer],
            router_exp[layer], router_exp_b[layer])
        return _moe(h1, hn, route, route_t, counts, layer, exp_gate, exp_up, exp_down, combine)

    h = moe_layer(0, (mixer_norm[0], _to_bf16(ret_w_in[0]), S), ret_w_out[0], h, combine=False)

    G = len(DIL_RATES)
    gd = DIL_HEADS * DIL_HEAD_DIM
    outs, lses = [], []
    for g in range(G):
        cq = slice(g * gd, (g + 1) * gd)
        cv = slice(G * gd + g * gd, G * gd + (g + 1) * gd)
        q, k, vt, *combined = _qkv(h, B, S, DIL_RATES[g], mixer_norm[1], kv_norm,
                                   dil_wq[0][:, cq].astype(BF16), w_kv[:, cq].astype(BF16),
                                   w_kv[:, cv].astype(BF16), q_norm[0][g], k_norm[g])
        if combined:
            h, = combined
        o, lse = _group_attention(q, k, vt, rel_bias[:, g * DIL_HEADS:(g + 1) * DIL_HEADS],
                                  DIL_RATES[g], DIL_WINDOWS[g] // DIL_RATES[g])
        outs.append(o)
        lses.append(lse)
    h = moe_layer(1, (tuple(outs), tuple(lses)), dil_wo[0], h, combine=True)
    return h.reshape(B, S, D)
```

```python
import functools
import math

import jax
import jax.numpy as jnp
from jax import lax
from jax.experimental import pallas as pl
from jax.experimental.pallas import tpu as pltpu
from jax.experimental.pallas import tpu_sc as plsc

F32 = jnp.float32
BF16 = jnp.bfloat16

EPS = 1e-6
NEG_INF = -1e30

RET_HEADS = 4
RET_CHUNK = 256
ROPE_BASE = 10000.0

DIL_WINDOWS = (128, 512, 2048)
DIL_RATES = (1, 4, 16)
DIL_HEADS = 16
DIL_HEAD_DIM = 64
SUB_BLOCK = 128
NUM_BUCKETS = 32
MAX_DISTANCE = 2048

MOE_GROUPS = 4
EXPERTS_PER_GROUP = 8
N_EXPERTS = MOE_GROUPS * EXPERTS_PER_GROUP
MOE_BLOCK = 512

LANES = 128
BF16_ROWS = 16
PERM_ROWS = 256
ROW_TILE = 1024
TOKEN_TILE = 512
ATTN_TILE = 2048
VMEM_LIMIT = 56 * 1024 * 1024

NT_DIMS = (((1,), (1,)), ((), ()))
LOG2E = math.log2(math.e)
LN2 = math.log(2.0)


def _params(*sem):
    return pltpu.CompilerParams(dimension_semantics=sem, vmem_limit_bytes=VMEM_LIMIT)


def _resident(shape, index_map):
    return pl.BlockSpec(shape, index_map, pipeline_mode=pl.Buffered(1))


def _rms(x):
    return x * lax.rsqrt(jnp.mean(x * x, axis=-1, keepdims=True) + EPS)


def _retention_tile(x_ref, g_ref, w_ref, cos_ref, sin_ref, din_ref, xi_ref, zeta_ref, cd_ref, state_ref,
                    first_tile):
    C = RET_CHUNK
    H, dk, dv = state_ref.shape
    half = dk // 2
    ts = x_ref.shape[0]
    k_scale = dk ** -0.5

    @pl.when(first_tile)
    def _():
        state_ref[...] = jnp.zeros_like(state_ref)

    xn = (_rms(x_ref[...]) * g_ref[...]).astype(BF16)
    cos, sin = cos_ref[...], sin_ref[...]

    def proj(c0, width):
        return jnp.dot(xn, w_ref[:, c0:c0 + width], preferred_element_type=F32)

    def rot(t):
        t1, t2 = t[:, :half], t[:, half:]
        return jnp.concatenate([t1 * cos - t2 * sin, t1 * sin + t2 * cos], axis=1)

    heads = []
    for hh in range(H):
        q = rot(proj(hh * dk, dk)).astype(BF16)
        k = rot(proj(H * dk + hh * dk, dk)) * k_scale
        kb = k.astype(BF16)
        v = proj(2 * H * dk + hh * dv, dv).astype(BF16)
        gate = proj(2 * H * dk + H * dv + hh * dv, dv)
        gate = gate * jax.nn.sigmoid(gate)
        chunks = []
        for c in range(ts // C):
            rows = slice(c * C, (c + 1) * C)
            s = lax.dot_general(q[rows], kb[rows], NT_DIMS, preferred_element_type=F32) * din_ref[hh]
            inner = jnp.dot(s.astype(BF16), v[rows], preferred_element_type=F32)
            state = state_ref[hh]
            cross = jnp.dot(q[rows], state.astype(BF16), preferred_element_type=F32) * xi_ref[hh]
            kz_t = (k[rows] * zeta_ref[hh]).T.astype(BF16)
            state_ref[hh] = state * cd_ref[hh] + jnp.dot(kz_t, v[rows], preferred_element_type=F32)
            chunks.append((gate[rows] * _rms(inner + cross)).astype(BF16))
        heads.append(jnp.concatenate(chunks, axis=0))
    return jnp.concatenate(heads, axis=1)


def _retention_operands(gain, w, S, ts):
    H, C = RET_HEADS, RET_CHUNK
    D = w.shape[0]
    dk = D // H
    half = dk // 2
    nt = S // ts
    pos = jnp.arange(S, dtype=F32)
    inv = 1.0 / (ROPE_BASE ** jnp.linspace(0.0, 1.0, half, dtype=F32))
    ang = pos[:, None] * inv[None, :]
    log_g = jnp.log(1.0 - 2.0 ** (-5.0 - jnp.arange(H, dtype=F32)))
    idx = jnp.arange(C, dtype=F32)
    diff = idx[:, None] - idx[None, :]
    d_in = jnp.where(diff >= 0, jnp.exp(log_g[:, None, None] * jnp.maximum(diff, 0.0)), 0.0)
    xi = jnp.exp(log_g[:, None] * (idx + 1.0))[:, :, None]
    zeta = jnp.exp(log_g[:, None] * (C - 1.0 - idx))[:, :, None]
    chunk_decay = jnp.exp(log_g * C)[:, None, None]
    table = pl.BlockSpec((ts, half), lambda i: (i % nt, 0))
    per_head = lambda shape: _resident((H,) + shape, lambda i: (0, 0, 0))
    args = (gain.reshape(1, D), w, jnp.cos(ang), jnp.sin(ang), d_in, xi, zeta, chunk_decay)
    specs = [_resident((1, D), lambda i: (0, 0)), _resident(w.shape, lambda i: (0, 0)), table, table,
             per_head((C, C)), per_head((C, 1)), per_head((C, 1)), per_head((1, 1))]
    state = pltpu.VMEM((H, dk, 2 * dk), F32)
    return args, specs, state


ROUTE_E1, ROUTE_E2, ROUTE_G1, ROUTE_G2, ROUTE_R1, ROUTE_R2 = range(6)
ROUTE_ROWS = 8
ROUTE_PIECES = 2
ROUTER_EXP_LANE0 = MOE_GROUPS
HI16 = 0xFFFF0000


def _pack_bf16_pair(a, b):
    ua = lax.bitcast_convert_type(a.astype(BF16).astype(F32), jnp.uint32)
    ub = lax.bitcast_convert_type(b.astype(BF16).astype(F32), jnp.uint32)
    return ua | (ub >> 16)


def _unpack_bf16_pair(w):
    a = lax.bitcast_convert_type(w & jnp.uint32(HI16), F32)
    b = lax.bitcast_convert_type(w << 16, F32)
    return a, b


ROW_PARTS = 2


def _part_cols(D, part):
    w = D // 2 // ROW_PARTS
    return slice(part * w, (part + 1) * w), slice(D // 2 + part * w, D // 2 + (part + 1) * w)


def _pack_part(x, part):
    hi, lo = _part_cols(x.shape[1], part)
    return _pack_bf16_pair(x[:, hi], x[:, lo])


def _merge_groups(o_refs, l_refs, ex_ref):
    ex = ex_ref[...]

    def expand(w):
        return jnp.dot(w.astype(BF16), ex, preferred_element_type=F32)

    lses = [l_ref[...] for l_ref in l_refs]
    top = functools.reduce(jnp.maximum, lses)
    ws = [jnp.exp(l - top) for l in lses]
    den = sum(ws)
    last = o_refs[-1][...].astype(F32)
    return (last + sum(expand(w / den) * (o_ref[...].astype(F32) - last)
                       for w, o_ref in zip(ws[:-1], o_refs[:-1]))).astype(BF16)


def _post_kernel(*refs, n_groups, seq_tiles):
    n_a = 2 * n_groups + 1 if n_groups else 8
    a_refs = refs[:n_a]
    (w_ref, h_ref, g_ref, wr_ref, br_ref, tri_ref,
     h1_ref, hn0_ref, hn1_ref, route_ref, route_t_ref, cnt_ref, carry_ref) = refs[n_a:n_a + 13]
    hn_refs = (hn0_ref, hn1_ref)
    if n_groups:
        a = _merge_groups(a_refs[:n_groups], a_refs[n_groups:2 * n_groups], a_refs[-1])
    else:
        a = _retention_tile(h_ref, *a_refs, refs[-1], pl.program_id(0) % seq_tiles == 0)
    @pl.when(pl.program_id(0) == 0)
    def _():
        carry_ref[...] = jnp.zeros_like(carry_ref)

    half = tri_ref.shape[0]
    n_pieces = h_ref.shape[0] // half
    D = h_ref.shape[1]
    lane = lax.broadcasted_iota(jnp.int32, (half, LANES), 1).astype(F32)
    ninf = -jnp.inf

    def first_argmax(vals):
        top = jnp.max(vals, axis=1, keepdims=True)
        where = jnp.min(jnp.where(vals == top, lane, float(LANES)), axis=1, keepdims=True)
        return top, where

    h1_ref[...] = h_ref[...] + jnp.dot(a, w_ref[...], preferred_element_type=F32)

    def route_half(rows):
        hn = _rms(h1_ref[rows, :]) * g_ref[...]
        for part, ref in enumerate(hn_refs):
            ref[rows, :] = _pack_part(hn, part)
        hi = hn.astype(BF16)
        lo = (hn - hi.astype(F32)).astype(BF16)
        both = jnp.dot(hi, wr_ref[...], preferred_element_type=F32)
        logits = (both[:, :LANES] + both[:, LANES:]
                  + jnp.dot(lo, wr_ref[:, :LANES], preferred_element_type=F32) + br_ref[...])

        is_grp = lane < MOE_GROUPS
        lg = jnp.where(is_grp, logits, ninf)
        mg, grp = first_argmax(lg)
        p_grp = 1.0 / jnp.sum(jnp.where(is_grp, jnp.exp(lg - mg), 0.0), axis=1, keepdims=True)

        e_lane = lane - ROUTER_EXP_LANE0
        in_grp = (e_lane < N_EXPERTS) & (jnp.floor(e_lane * (1.0 / EXPERTS_PER_GROUP)) == grp)
        le = jnp.where(in_grp, logits, ninf)
        v1, i1 = first_argmax(le)
        le2 = jnp.where(lane == i1, ninf, le)
        v2, i2 = first_argmax(le2)
        e = jnp.exp(v2 - v1)
        hit1 = lane == (i1 - ROUTER_EXP_LANE0)
        hit2 = lane == (i2 - ROUTER_EXP_LANE0)
        onehot = jnp.where(hit1 | hit2, 1.0, 0.0)
        earlier = jnp.dot(tri_ref[...], onehot.astype(BF16), preferred_element_type=F32)
        return dict(e1=i1 - ROUTER_EXP_LANE0, e2=i2 - ROUTER_EXP_LANE0, g1=p_grp / (1.0 + e),
                    g2=p_grp * e / (1.0 + e), hit1=hit1, hit2=hit2, earlier=earlier,
                    count=jnp.sum(onehot, axis=0, keepdims=True))

    halves = [route_half(slice(j * half, (j + 1) * half)) for j in range(n_pieces)]
    carry = carry_ref[...]
    for j, r in enumerate(halves):
        before = carry + r["earlier"]
        r1 = jnp.sum(jnp.where(r["hit1"], before, 0.0), axis=1, keepdims=True)
        r2 = jnp.sum(jnp.where(r["hit2"], before, 0.0), axis=1, keepdims=True)
        carry = carry + r["count"]
        route = jnp.zeros((half, LANES), F32)
        for slot, val in ((ROUTE_E1, r["e1"]), (ROUTE_E2, r["e2"]), (ROUTE_G1, r["g1"]),
                          (ROUTE_G2, r["g2"]), (ROUTE_R1, r1), (ROUTE_R2, r2)):
            route = jnp.where(lane == slot, val, route)
        route_ref[j * half:(j + 1) * half, :] = route
        route_t_ref[:, j * half:(j + 1) * half] = route.T[:ROUTE_ROWS]
    carry_ref[...] = carry
    cnt_ref[...] = carry


def _post(mixer, w, h, gain, w_grp, b_grp, w_exp, b_exp):
    T, D = h.shape
    K = w.shape[0]
    row_block = lambda width: pl.BlockSpec((tm, width), lambda i: (i, 0))
    scratch = [pltpu.VMEM((1, LANES), F32)]
    if len(mixer) == 2:
        outs, lses = mixer
        n_groups, seq_tiles = len(outs), 1
        tm = min(TOKEN_TILE, T)
        ex = jnp.where(jnp.arange(LANES)[:, None] == (jnp.arange(K)[None, :] // DIL_HEAD_DIM), 1.0, 0.0).astype(BF16)
        a_args = (*outs, *lses, ex)
        a_specs = [row_block(K)] * n_groups + [row_block(LANES)] * n_groups + [_resident((LANES, K), lambda i: (0, 0))]
    else:
        mix_gain, w_in, S = mixer
        tm = min(TOKEN_TILE, S)
        n_groups, seq_tiles = 0, S // tm
        a_args, a_specs, state = _retention_operands(mix_gain, w_in, S, tm)
        scratch.append(state)
    n_r = MOE_GROUPS + N_EXPERTS
    wr = jnp.zeros((D, LANES), F32).at[:, :n_r].set(jnp.concatenate([w_grp, w_exp], axis=1))
    wr_hi = wr.astype(BF16)
    wr_lo = (wr - wr_hi.astype(F32)).astype(BF16)
    wr2 = jnp.concatenate([wr_hi, wr_lo], axis=1)
    br = jnp.zeros((1, LANES), F32).at[0, :n_r].set(jnp.concatenate([b_grp, b_exp]))
    piece = tm // ROUTE_PIECES
    tri = jnp.tril(jnp.ones((piece, piece), BF16), k=-1)
    return pl.pallas_call(
        functools.partial(_post_kernel, n_groups=n_groups, seq_tiles=seq_tiles),
        grid=(T // tm,),
        in_specs=a_specs + [
                  _resident((K, D), lambda i: (0, 0)),
                  pl.BlockSpec((tm, D), lambda i: (i, 0)),
                  _resident((1, D), lambda i: (0, 0)),
                  _resident((D, 2 * LANES), lambda i: (0, 0)),
                  _resident((1, LANES), lambda i: (0, 0)),
                  _resident((piece, piece), lambda i: (0, 0))],
        out_specs=[pl.BlockSpec((tm, D), lambda i: (i, 0))]
                  + [pl.BlockSpec((tm, D // 2 // ROW_PARTS), lambda i: (i, 0))] * ROW_PARTS
                  + [pl.BlockSpec((tm, LANES), lambda i: (i, 0)),
                   pl.BlockSpec((ROUTE_ROWS, tm), lambda i: (0, i)),
                   pl.BlockSpec((1, LANES), lambda i: (0, 0))],
        out_shape=[jax.ShapeDtypeStruct((T, D), F32)]
                  + [jax.ShapeDtypeStruct((T, D // 2 // ROW_PARTS), jnp.uint32)] * ROW_PARTS
                  + [jax.ShapeDtypeStruct((T, LANES), F32),
                   jax.ShapeDtypeStruct((ROUTE_ROWS, T), F32),
                   jax.ShapeDtypeStruct((1, LANES), F32)],
        scratch_shapes=scratch,
        compiler_params=_params("arbitrary"),
        name="post_mixer",
    )(*a_args, w, h, gain.reshape(1, D), wr2, br, tri)


SC_WINDOW = 128


def _sc_mesh():
    return plsc.VectorSubcoreMesh(core_axis_name="core", subcore_axis_name="subcore")


def _sc_window_specs(W):
    rows = pl.BlockSpec((SC_WINDOW, W), lambda i: (i, 0))
    idx = pl.BlockSpec((1, SC_WINDOW), lambda i: (0, i))
    return rows, idx


def _sc_pipeline(body, n_rows, in_specs, out_specs):
    return pltpu.emit_pipeline(body, grid=(n_rows // SC_WINDOW,), in_specs=in_specs, out_specs=out_specs,
                               core_axis_name=("core", "subcore"), dimension_semantics=(pltpu.PARALLEL,))


def _dispatch(xs, dest, pad_idx, P):
    T, W = xs[0].shape
    n_pad = pad_idx.shape[1]
    n = len(xs)
    rows, idx = _sc_window_specs(W)
    zero_rows = pl.BlockSpec((SC_WINDOW, W), lambda i: (0, 0))
    out = jax.ShapeDtypeStruct((P, W), xs[0].dtype)

    @functools.partial(pl.kernel, out_type=(out,) * n, mesh=_sc_mesh(), scratch_types=[], name="dispatch")
    def scatter(*refs):
        x_hbm, (d0_hbm, d1_hbm, z_hbm, p_hbm), o_hbm = refs[:n], refs[n:n + 4], refs[n + 4:]
        for x, o in zip(x_hbm, o_hbm):
            def put_pair(x_vmem, i0_vmem, i1_vmem, o=o):
                pltpu.sync_copy(x_vmem, o.at[i0_vmem.at[0]])
                pltpu.sync_copy(x_vmem, o.at[i1_vmem.at[0]])

            def put(x_vmem, i_vmem, o=o):
                pltpu.sync_copy(x_vmem, o.at[i_vmem.at[0]])

            _sc_pipeline(put_pair, T, [rows, idx, idx], [])(x, d0_hbm, d1_hbm)
            _sc_pipeline(put, n_pad, [zero_rows, idx], [])(z_hbm, p_hbm)

    return scatter(*xs, dest[0:1], dest[1:2], jnp.zeros((SC_WINDOW, W), xs[0].dtype), pad_idx)


WEIGHT_LEADS = (3, 2, 1)
EXPERT_SLOTS = max(WEIGHT_LEADS) + 1


def _expert_kernel(blk_exp_ref, slot_ref, n_used_ref, *refs):
    x_refs = refs[:ROW_PARTS]
    w_refs = refs[ROW_PARTS:ROW_PARTS + 3]
    o_refs = refs[ROW_PARTS + 3:2 * ROW_PARTS + 3]
    w_slots = refs[2 * ROW_PARTS + 3:]
    lead = max(WEIGHT_LEADS)
    n_used = n_used_ref[0]
    i = pl.program_id(0) - lead
    D = w_slots[0].shape[1]

    for w_ref, w_s, ahead in zip(w_refs, w_slots, WEIGHT_LEADS):
        j = i + ahead
        jc = jnp.clip(j, 0, n_used - 1)
        arrived = (j >= 0) & (j < n_used) & ((j == 0) | (blk_exp_ref[jc] != blk_exp_ref[jnp.maximum(jc - 1, 0)]))

        @pl.when(arrived)
        def _(w_ref=w_ref, w_s=w_s, jc=jc):
            w_s[slot_ref[jc]] = w_ref[0, 0].astype(BF16)

    @pl.when((i >= 0) & (i < n_used))
    def _():
        slot = slot_ref[jnp.clip(i, 0, n_used - 1)]
        wg_s, wu_s, wd_s = w_slots
        pieces = []
        for part, x_ref in enumerate(x_refs):
            for cols, val in zip(_part_cols(D, part), _unpack_bf16_pair(x_ref[...])):
                pieces.append((cols, val.astype(BF16)))

        def up(w_s):
            return sum(jnp.dot(val, w_s[slot, cols, :], preferred_element_type=F32) for cols, val in pieces)

        g = up(wg_s)
        hid = (g * jax.nn.sigmoid(g) * up(wu_s)).astype(BF16)
        y = jnp.dot(hid, wd_s[slot], preferred_element_type=F32)
        for part, o_ref in enumerate(o_refs):
            o_ref[...] = _pack_part(y, part)

    @pl.when(i >= n_used)
    def _():
        for o_ref in o_refs:
            o_ref[...] = jnp.zeros_like(o_ref)


def _experts(xs, blk_exp, n_used, layer, w_gate, w_up, w_down):
    P, W = xs[0].shape
    D = 2 * W * ROW_PARTS
    FF = w_gate.shape[3]
    nblk = P // MOE_BLOCK
    lead = max(WEIGHT_LEADS)
    changes = jnp.concatenate([jnp.zeros((1,), jnp.int32), (blk_exp[1:] != blk_exp[:-1]).astype(jnp.int32)])
    slot = jnp.cumsum(changes) % EXPERT_SLOTS

    def x_map(g, be, sl, nu):
        return (jnp.clip(g - lead, 0, nu[0] - 1), 0)

    def o_map(g, be, sl, nu):
        return (jnp.maximum(g - lead, 0), 0)

    def w_map(ahead):
        return lambda g, be, sl, nu: (layer, be[jnp.clip(g - lead + ahead, 0, nu[0] - 1)], 0, 0)

    wg_spec, wu_spec, wd_spec = (pl.BlockSpec((1, 1) + shape, w_map(ahead))
                                 for shape, ahead in zip(((D, FF), (D, FF), (FF, D)), WEIGHT_LEADS))
    return pl.pallas_call(
        _expert_kernel,
        grid_spec=pltpu.PrefetchScalarGridSpec(
            num_scalar_prefetch=3,
            grid=(nblk + lead,),
            in_specs=[pl.BlockSpec((MOE_BLOCK, W), x_map)] * ROW_PARTS + [wg_spec, wu_spec, wd_spec],
            out_specs=[pl.BlockSpec((MOE_BLOCK, W), o_map)] * ROW_PARTS,
            scratch_shapes=[pltpu.VMEM((EXPERT_SLOTS, D, FF), BF16), pltpu.VMEM((EXPERT_SLOTS, D, FF), BF16),
                            pltpu.VMEM((EXPERT_SLOTS, FF, D), BF16)]),
        out_shape=[jax.ShapeDtypeStruct((P, W), jnp.uint32)] * ROW_PARTS,
        compiler_params=_params("arbitrary"),
        name="experts",
    )(blk_exp, slot, n_used, *xs, w_gate, w_up, w_down)


def _gather_pairs(ys, dest):
    W = ys[0].shape[1]
    T = dest.shape[1]
    n = len(ys)
    rows, idx = _sc_window_specs(W)
    out = jax.ShapeDtypeStruct((T, W), ys[0].dtype)

    @functools.partial(pl.kernel, out_type=(out,) * (2 * n), mesh=_sc_mesh(), scratch_types=[],
                       name="gather_pairs")
    def gather(*refs):
        y_hbm, d_hbm, o_hbm = refs[:n], refs[n:n + 2], refs[n + 2:]
        for slot, d in enumerate(d_hbm):
            for y, o in zip(y_hbm, o_hbm[slot * n:(slot + 1) * n]):
                def get(i_vmem, o_vmem, y=y):
                    pltpu.sync_copy(y.at[i_vmem.at[0]], o_vmem)

                _sc_pipeline(get, T, [idx], [rows])(d, o)

    return gather(*ys, dest[0:1], dest[1:2])


def _combined_rows(h_ref, route_ref, y_refs):
    D = h_ref.shape[1]
    route = route_ref[...]
    gates = (route[:, ROUTE_G1:ROUTE_G1 + 1], route[:, ROUTE_G2:ROUTE_G2 + 1])
    pieces = {}
    for part in range(ROW_PARTS):
        slots = [_unpack_bf16_pair(y_refs[slot * ROW_PARTS + part][...]) for slot in range(2)]
        for half, cols in enumerate(_part_cols(D, part)):
            pieces[cols.start] = h_ref[:, cols] + (gates[0] * slots[0][half] + gates[1] * slots[1][half])
    return jnp.concatenate([pieces[c] for c in sorted(pieces)], axis=1)


def _combine_kernel(h_ref, route_ref, *refs):
    refs[-1][...] = _combined_rows(h_ref, route_ref, refs[:-1])


def _combine(h, route, pairs):
    T, D = h.shape
    W = pairs[0].shape[1]
    tm = min(ROW_TILE, T)
    return pl.pallas_call(
        _combine_kernel,
        grid=(T // tm,),
        in_specs=[pl.BlockSpec((tm, D), lambda i: (i, 0)), pl.BlockSpec((tm, LANES), lambda i: (i, 0))]
                 + [pl.BlockSpec((tm, W), lambda i: (i, 0))] * len(pairs),
        out_specs=pl.BlockSpec((tm, D), lambda i: (i, 0)),
        out_shape=jax.ShapeDtypeStruct((T, D), F32),
        compiler_params=_params("parallel"),
        name="combine",
    )(h, route, *pairs)


def _moe(h1, hn, route, route_t, counts, layer, w_gate, w_up, w_down, combine):
    T, D = h1.shape
    A = 2 * T
    nblk = -(-A // MOE_BLOCK) + N_EXPERTS
    P = nblk * MOE_BLOCK
    eid = route_t[ROUTE_E1:ROUTE_E2 + 1].astype(jnp.int32)
    rank = route_t[ROUTE_R1:ROUTE_R2 + 1].astype(jnp.int32)
    cnt = counts[0, :N_EXPERTS].astype(jnp.int32)
    padded = (cnt + MOE_BLOCK - 1) // MOE_BLOCK * MOE_BLOCK
    pends = jnp.cumsum(padded)
    pstarts = pends - padded
    experts = jnp.arange(N_EXPERTS, dtype=jnp.int32)
    start_of = jnp.sum(jnp.where(eid[:, None, :] == experts[None, :, None], pstarts[None, :, None], 0), axis=1)
    dest = start_of + rank
    blk_start = jnp.arange(nblk, dtype=jnp.int32) * MOE_BLOCK
    blk_exp = jnp.minimum(jnp.sum((pends[None, :] <= blk_start[:, None]).astype(jnp.int32), axis=1),
                          N_EXPERTS - 1)
    n_used = pends[-1:] // MOE_BLOCK

    gap_start = jnp.concatenate([pstarts + cnt, pends[-1:]])
    gap_len = jnp.concatenate([padded - cnt, P - pends[-1:]])
    gap_end = jnp.cumsum(gap_len)
    j = jnp.arange(P - A, dtype=jnp.int32)
    gap_of = jnp.sum((gap_end[None, :] <= j[:, None]).astype(jnp.int32), axis=1)
    sel = gap_of[:, None] == jnp.arange(N_EXPERTS + 1, dtype=jnp.int32)[None, :]
    pad_idx = jnp.sum(jnp.where(sel, (gap_start - (gap_end - gap_len))[None, :] + j[:, None], 0), axis=1)

    xs = _dispatch(hn, dest, pad_idx.reshape(1, P - A), P)
    ys = _experts(xs, blk_exp, n_used, layer, w_gate, w_up, w_down)
    pairs = _gather_pairs(ys, dest)
    return _combine(h1, route, pairs) if combine else (h1, route, pairs)


def _qkv_kernel(*refs, rate, n, n_chunks, n_pending, permute):
    n_x = 2 + n_pending if n_pending else 1
    x_refs = refs[:n_x]
    gq_ref, gkv_ref, wq_ref, wk_ref, wvt_ref, qn_ref, kn_ref, seg_ref = refs[n_x:n_x + 8]
    perm_ref = refs[n_x + 8] if permute else None
    q_ref, k_ref, vt_ref = refs[n_x + 8 + permute:n_x + 11 + permute]
    xs_ref = refs[-1]
    seg = seg_ref[...]
    width = seg.shape[0]
    res_per_chunk = q_ref.shape[1]
    chunk = pl.program_id(2)

    def head_norm(t, gain):
        cols = []
        for j in range(t.shape[1] // width):
            tj = t[:, j * width:(j + 1) * width]
            ms = jnp.dot((tj * tj).astype(BF16), seg, preferred_element_type=F32)
            cols.append(tj * lax.rsqrt(ms + EPS))
        return jnp.concatenate(cols, axis=1) * gain

    def normed(x):
        y = _rms(x)
        return (y * gq_ref[...]).astype(BF16), (y * gkv_ref[...]).astype(BF16)

    def project(xq, xkv):
        q = head_norm(jnp.dot(xq, wq_ref[...], preferred_element_type=F32), qn_ref[...]).astype(q_ref.dtype)
        k = head_norm(jnp.dot(xkv, wk_ref[...], preferred_element_type=F32), kn_ref[...]).astype(k_ref.dtype)
        vt = lax.dot_general(wvt_ref[...], xkv, NT_DIMS, preferred_element_type=F32).astype(vt_ref.dtype)
        for j in range(res_per_chunk):
            q_ref[0, j] = q[j * n:(j + 1) * n]
            k_ref[0, j] = k[j * n:(j + 1) * n]
            vt_ref[0, j] = vt[:, j * n:(j + 1) * n]

    if n_pending:
        x = _combined_rows(x_refs[0], x_refs[1], x_refs[2:])
        refs[-2][...] = x
        project(*normed(x))
        return
    x_ref, = x_refs
    if rate == 1:
        project(*normed(x_ref[...]))
        return

    if permute:
        group = perm_ref.shape[0]
        n_groups, run = x_ref.shape[0] // group, group // rate

        @pl.when(chunk == 0)
        def _():
            for g in range(n_groups):
                rows = slice(g * group, (g + 1) * group)
                for t, xt in enumerate(normed(x_ref[rows, :])):
                    xs_ref[t, rows, :] = jnp.dot(perm_ref[...], xt, preferred_element_type=F32).astype(BF16)

        span = res_per_chunk * run
        start = pl.multiple_of(chunk * span, span)

        def residue_major(t):
            parts = [xs_ref[t, pl.ds(g * group + start, span), :] for g in range(n_groups)]
            return jnp.concatenate([p[j * run:(j + 1) * run] for j in range(res_per_chunk) for p in parts], axis=0)

        project(residue_major(0), residue_major(1))
        return

    @pl.when(chunk == 0)
    def _():
        for j in range(xs_ref.shape[0]):
            xs_ref[j] = x_ref[:, j * LANES:(j + 1) * LANES]

    for ch in range(n_chunks):
        @pl.when(chunk == ch)
        def _(ch=ch):
            residues = range(ch * res_per_chunk, (ch + 1) * res_per_chunk)
            project(*normed(jnp.concatenate(
                [jnp.concatenate([xs_ref[j, pl.ds(c, n, stride=rate), :] for j in range(xs_ref.shape[0])], axis=1)
                 for c in residues], axis=0)))


def _qkv(h, B, S, rate, gq, gkv, wq, wk, wv, qn, kn):
    pending = h if isinstance(h, tuple) else None
    D = (pending[0] if pending else h).shape[1]
    L = S // rate
    hd = DIL_HEAD_DIM
    n = max(SUB_BLOCK, TOKEN_TILE // rate)
    tm = n * rate
    res_per_chunk = max(1, TOKEN_TILE // n)
    n_chunks = rate // res_per_chunk
    width = 2 * LANES
    ii = jnp.arange(width)
    seg = jnp.where((ii[:, None] // hd) == (ii[None, :] // hd), 1.0 / hd, 0.0).astype(BF16)
    permute = rate * BF16_ROWS == PERM_ROWS
    if permute:
        pp = jnp.arange(PERM_ROWS)
        perm = ((pp[:, None] // BF16_ROWS == pp[None, :] % rate)
                & (pp[:, None] % BF16_ROWS == pp[None, :] // rate)).astype(BF16)
    row = lambda g: jnp.tile(g, D // hd).reshape(1, D)
    const = lambda shape: _resident(shape, lambda b, i, c: (0,) * len(shape))
    qk_spec = pl.BlockSpec((1, res_per_chunk, n, D), lambda b, i, c: (b, c, i, 0))
    qk_shape = jax.ShapeDtypeStruct((B, rate, L, D), BF16)
    rows = lambda width: pl.BlockSpec((tm, width), lambda b, i, c: (b * (S // tm) + i, 0))
    if pending:
        assert rate == 1
        h1, route, pairs = pending
        x_args = (h1, route, *pairs)
        x_specs = [rows(D), rows(LANES)] + [rows(pairs[0].shape[1])] * len(pairs)
        extra_specs, extra_shapes = [rows(D)], [jax.ShapeDtypeStruct((B * S, D), F32)]
    else:
        x_args, x_specs, extra_specs, extra_shapes = (h,), [rows(D)], [], []
    return pl.pallas_call(
        functools.partial(_qkv_kernel, rate=rate, n=n, n_chunks=n_chunks,
                          n_pending=len(pending[2]) if pending else 0, permute=permute),
        grid=(B, S // tm, n_chunks),
        in_specs=x_specs + [
                  const((1, D)), const((1, D)), const((D, D)), const((D, D)), const((D, D)),
                  const((1, D)), const((1, D)), const((width, width))]
                 + ([const((PERM_ROWS, PERM_ROWS))] if permute else []),
        out_specs=[qk_spec, qk_spec,
                   pl.BlockSpec((1, res_per_chunk, D, n), lambda b, i, c: (b, c, 0, i))] + extra_specs,
        out_shape=[qk_shape, qk_shape, jax.ShapeDtypeStruct((B, rate, D, L), BF16)] + extra_shapes,
        scratch_shapes=[pltpu.VMEM((2, tm, D), BF16) if permute else
                        pltpu.VMEM((D // LANES, tm if rate > 1 else 8, LANES), F32)],
        compiler_params=_params("parallel", "parallel", "arbitrary"),
        name="qkv_rate%d" % rate,
    )(*x_args, gq.reshape(1, D), gkv.reshape(1, D), wq, wk, wv.T, row(qn) * (hd ** -0.5 * LOG2E), row(kn), seg,
      *((perm,) if permute else ()))


def _attn_kernel(*refs, rate, with_prev):
    if not with_prev:
        q_ref, kc_ref, vc_ref, edge_ref, o_ref, lse_ref, os_ref = refs
    elif len(refs) == 10:
        q_ref, kc_ref, kp_ref, vc_ref, vp_ref, edge_ref, bias_ref, o_ref, lse_ref, os_ref = refs
    else:
        q_ref, kc_ref, kp_ref, vc_ref, vp_ref, edge_ref, o_ref, lse_ref, os_ref = refs
    Bk = SUB_BLOCK
    hd = DIL_HEAD_DIM
    n_pairs = DIL_HEADS // 2
    lane = lax.broadcasted_iota(jnp.int32, (Bk, LANES), 1)

    def block(load_q, load_k, load_vt, load_bias, store):
        out_t, lse_t = [], []
        for p in range(n_pairs):
            qp = load_q(p)
            zero = jnp.zeros_like(qp)
            q2 = jnp.concatenate([jnp.where(lane < hd, qp, zero), jnp.where(lane >= hd, qp, zero)], axis=0)
            s = lax.dot_general(load_k(p), q2, NT_DIMS, preferred_element_type=F32) + load_bias(p)
            m = jnp.max(s, axis=0, keepdims=True)
            pr = jnp.exp2(s - m)
            l = jnp.sum(pr, axis=0, keepdims=True)
            pb = pr.astype(BF16)
            vt = load_vt(p)
            out_t.append(jnp.dot(vt[:hd], pb[:, :Bk], preferred_element_type=F32) / l[:, :Bk])
            out_t.append(jnp.dot(vt[hd:], pb[:, Bk:], preferred_element_type=F32) / l[:, Bk:])
            lse = (m + jnp.log2(l)) * LN2
            lse_t += [lse[:, :Bk], lse[:, Bk:]]
        lse_t.append(jnp.zeros((LANES - DIL_HEADS, Bk), F32))
        store(jnp.concatenate(out_t, axis=0).T, jnp.concatenate(lse_t, axis=0).T)

    def pair_cols(p):
        return slice(p * LANES, (p + 1) * LANES)

    n_res, n_blocks = kc_ref.shape[1], kc_ref.shape[2] // Bk
    for c in range(n_res):
        for j in range(n_blocks):
            rows = slice(j * Bk, (j + 1) * Bk)
            if not with_prev:
                load_k = lambda p, c=c: kc_ref[0, c, :, pair_cols(p)]
                load_vt = lambda p, c=c: vc_ref[0, c, pair_cols(p), :]
                load_bias = lambda p: edge_ref[0, p]
            elif j == 0:
                load_k = lambda p, c=c: jnp.concatenate([kp_ref[0, c, :, pair_cols(p)],
                                                         kc_ref[0, c, :Bk, pair_cols(p)]], axis=0)
                load_vt = lambda p, c=c: jnp.concatenate([vp_ref[0, c, pair_cols(p), :],
                                                          vc_ref[0, c, pair_cols(p), :Bk]], axis=1)
                load_bias = lambda p: edge_ref[0, p]
            else:
                krows = slice((j - 1) * Bk, (j + 1) * Bk)
                load_k = lambda p, c=c, krows=krows: kc_ref[0, c, krows, pair_cols(p)]
                load_vt = lambda p, c=c, krows=krows: vc_ref[0, c, pair_cols(p), krows]
                load_bias = lambda p: bias_ref[p]

            if rate == 1:
                def store(o, lse, rows=rows):
                    o_ref[rows, :] = o.astype(o_ref.dtype)
                    lse_ref[rows, :] = lse
            else:
                def store(o, lse, out_rows=pl.ds(c + j * Bk * rate, Bk, stride=rate)):
                    for g in range(os_ref.shape[0]):
                        os_ref[g, out_rows, :] = o[:, pair_cols(g)]
                    lse_ref[out_rows, :] = lse

            block(lambda p, c=c, rows=rows: q_ref[0, c, rows, pair_cols(p)], load_k, load_vt, load_bias, store)
    if rate > 1:
        for g in range(os_ref.shape[0]):
            o_ref[:, pair_cols(g)] = os_ref[g].astype(o_ref.dtype)


def _t5_bucket(n):
    max_exact = NUM_BUCKETS // 2
    nf = jnp.maximum(n, max_exact).astype(F32)
    large = max_exact + (jnp.log(nf / max_exact) / math.log(MAX_DISTANCE / max_exact)
                         * (NUM_BUCKETS - max_exact)).astype(jnp.int32)
    large = jnp.minimum(large, NUM_BUCKETS - 1)
    return jnp.where(n < max_exact, n, large)


def _group_attention(q, k, vt, bias_table, rate, n_steps):
    B, _, L, D = q.shape
    S = L * rate
    Bk = SUB_BLOCK
    with_prev = L > Bk
    nk = 2 * Bk if with_prev else Bk
    n = max(Bk, ATTN_TILE // rate)
    tm = n * rate
    nt = S // tm
    n_pairs = DIL_HEADS // 2
    ql = jnp.arange(Bk, dtype=jnp.int32)[:, None]
    kl = jnp.arange(2 * Bk, dtype=jnp.int32)[None, :]
    steps = ql + Bk - kl
    bucket = _t5_bucket(jnp.maximum(steps, 0) * rate)
    buckets = jnp.arange(NUM_BUCKETS, dtype=jnp.int32)
    bias = jnp.sum(jnp.where(bucket[None, :, :, None] == buckets[:, None, None, None],
                             bias_table.astype(F32)[:, None, None, :], 0.0), axis=0)
    band = ((steps >= 0) & (steps <= n_steps))[:, :, None]
    first = (kl < Bk)[:, :, None]

    def layout(t):
        t = t[:, 2 * Bk - nk:, :].transpose(1, 2, 0)
        return t.reshape(nk, n_pairs, 2 * Bk).transpose(1, 0, 2)

    bias_in = layout(jnp.where(band, bias * LOG2E, NEG_INF))
    bias_first = layout(jnp.where(band & ~first, bias * LOG2E, NEG_INF))
    edge = jnp.stack([bias_first, bias_in])

    cur_qk = pl.BlockSpec((1, rate, n, D), lambda b, i: (b, 0, i, 0))
    cur_vt = pl.BlockSpec((1, rate, D, n), lambda b, i: (b, 0, 0, i))
    per_n = n // Bk
    prev_qk = pl.BlockSpec((1, rate, Bk, D), lambda b, i: (b, 0, jnp.maximum(i * per_n - 1, 0), 0))
    prev_vt = pl.BlockSpec((1, rate, D, Bk), lambda b, i: (b, 0, 0, jnp.maximum(i * per_n - 1, 0)))
    edge_spec = pl.BlockSpec((1, n_pairs, nk, 2 * Bk), lambda b, i: (jnp.minimum(i, 1), 0, 0, 0))
    if with_prev:
        in_specs = [cur_qk, cur_qk, prev_qk, cur_vt, prev_vt, edge_spec]
        args = (q, k, k, vt, vt, edge)
    else:
        in_specs = [cur_qk, cur_qk, cur_vt, edge_spec]
        args = (q, k, vt, edge)
    if with_prev and n > Bk:
        in_specs.append(_resident((n_pairs, nk, 2 * Bk), lambda b, i: (0, 0, 0)))
        args += (bias_in,)
    return pl.pallas_call(
        functools.partial(_attn_kernel, rate=rate, with_prev=with_prev),
        grid=(B, nt),
        in_specs=in_specs,
        out_specs=[pl.BlockSpec((tm, D), lambda b, i: (b * nt + i, 0)),
                   pl.BlockSpec((tm, LANES), lambda b, i: (b * nt + i, 0))],
        out_shape=[jax.ShapeDtypeStruct((B * S, D), BF16),
                   jax.ShapeDtypeStruct((B * S, LANES), F32)],
        scratch_shapes=[pltpu.VMEM((D // LANES, tm if rate > 1 else 8, LANES), F32)],
        compiler_params=_params("parallel", "parallel"),
        name="attn_rate%d" % rate,
    )(*args)


def _cast_kernel(x_ref, o_ref):
    o_ref[...] = x_ref[...].astype(o_ref.dtype)


def _bf16_column_blocks(w, cols):
    rows, n = w.shape[0], w.shape[1] // cols
    assert w.shape[1] == n * cols
    return pl.pallas_call(
        _cast_kernel, grid=(n,),
        in_specs=[pl.BlockSpec((rows, cols), lambda j: (0, j))],
        out_specs=pl.BlockSpec((None, rows, cols), lambda j: (j, 0, 0)),
        out_shape=jax.ShapeDtypeStruct((n, rows, cols), BF16),
        compiler_params=_params("parallel"), name="bf16_column_blocks",
    )(w)


def kernel(x, ret_w_in, ret_w_out, kv_norm, w_kv, k_norm, dil_wq, q_norm, dil_wo, rel_bias,
           mixer_norm, ffn_norm, router_grp, router_grp_b, router_exp, router_exp_b,
           exp_gate, exp_up, exp_down):
    B, S, D = x.shape
    h = x.reshape(B * S, D)

    def moe_layer(layer, mixer, w_out, h, combine):
        h1, *hn, route, route_t, counts = _post(
            mixer, w_out.astype(BF16), h, ffn_norm[layer], router_grp[layer], router_grp_b[layer],
            router_exp[layer], router_exp_b[layer])
        return _moe(h1, hn, route, route_t, counts, layer, exp_gate, exp_up, exp_down, combine)

    h = moe_layer(0, (mixer_norm[0], ret_w_in[0].astype(BF16), S), ret_w_out[0], h, combine=False)

    G = len(DIL_RATES)
    gd = DIL_HEADS * DIL_HEAD_DIM
    outs, lses = [], []
    w_kv_blocks = _bf16_column_blocks(w_kv, gd)
    for g in range(G):
        cq = slice(g * gd, (g + 1) * gd)
        q, k, vt, *combined = _qkv(h, B, S, DIL_RATES[g], mixer_norm[1], kv_norm,
                                   dil_wq[0][:, cq].astype(BF16), w_kv_blocks[g],
                                   w_kv_blocks[G + g], q_norm[0][g], k_norm[g])
        if combined:
            h, = combined
        o, lse = _group_attention(q, k, vt, rel_bias[:, g * DIL_HEADS:(g + 1) * DIL_HEADS],
                                  DIL_RATES[g], DIL_WINDOWS[g] // DIL_RATES[g])
        outs.append(o)
        lses.append(lse)
    h = moe_layer(1, (tuple(outs), tuple(lses)), dil_wo[0], h, combine=True)
    return h.reshape(B, S, D)
```

```python
import functools
import math

import jax
import jax.numpy as jnp
from jax import lax
from jax.experimental import pallas as pl
from jax.experimental.pallas import tpu as pltpu
from jax.experimental.pallas import tpu_sc as plsc

F32 = jnp.float32
BF16 = jnp.bfloat16

EPS = 1e-6
NEG_INF = -1e30

RET_HEADS = 4
RET_CHUNK = 256
ROPE_BASE = 10000.0

DIL_WINDOWS = (128, 512, 2048)
DIL_RATES = (1, 4, 16)
DIL_HEADS = 16
DIL_HEAD_DIM = 64
SUB_BLOCK = 128
NUM_BUCKETS = 32
MAX_DISTANCE = 2048

MOE_GROUPS = 4
EXPERTS_PER_GROUP = 8
N_EXPERTS = MOE_GROUPS * EXPERTS_PER_GROUP
MOE_BLOCK = 512

LANES = 128
BF16_ROWS = 16
PERM_ROWS = 256
ROW_TILE = 1024
TOKEN_TILE = 512
ATTN_TILE = 2048
VMEM_LIMIT = 56 * 1024 * 1024

NT_DIMS = (((1,), (1,)), ((), ()))
LOG2E = math.log2(math.e)
LN2 = math.log(2.0)


def _params(*sem):
    return pltpu.CompilerParams(dimension_semantics=sem, vmem_limit_bytes=VMEM_LIMIT)


def _resident(shape, index_map):
    return pl.BlockSpec(shape, index_map, pipeline_mode=pl.Buffered(1))


def _rms(x):
    return x * lax.rsqrt(jnp.mean(x * x, axis=-1, keepdims=True) + EPS)


def _retention_tile(x_ref, g_ref, w_ref, cos_ref, sin_ref, din_ref, xi_ref, zeta_ref, cd_ref, state_ref,
                    first_tile):
    C = RET_CHUNK
    H, dk, dv = state_ref.shape
    half = dk // 2
    ts = x_ref.shape[0]
    k_scale = dk ** -0.5

    @pl.when(first_tile)
    def _():
        state_ref[...] = jnp.zeros_like(state_ref)

    xn = (_rms(x_ref[...]) * g_ref[...]).astype(BF16)
    cos, sin = cos_ref[...], sin_ref[...]

    def proj(c0, width):
        return jnp.dot(xn, w_ref[:, c0:c0 + width], preferred_element_type=F32)

    def rot(t):
        t1, t2 = t[:, :half], t[:, half:]
        return jnp.concatenate([t1 * cos - t2 * sin, t1 * sin + t2 * cos], axis=1)

    heads = []
    for hh in range(H):
        q = rot(proj(hh * dk, dk)).astype(BF16)
        k = rot(proj(H * dk + hh * dk, dk)) * k_scale
        kb = k.astype(BF16)
        v = proj(2 * H * dk + hh * dv, dv).astype(BF16)
        gate = proj(2 * H * dk + H * dv + hh * dv, dv)
        gate = gate * jax.nn.sigmoid(gate)
        chunks = []
        for c in range(ts // C):
            rows = slice(c * C, (c + 1) * C)
            s = lax.dot_general(q[rows], kb[rows], NT_DIMS, preferred_element_type=F32) * din_ref[hh]
            inner = jnp.dot(s.astype(BF16), v[rows], preferred_element_type=F32)
            state = state_ref[hh]
            cross = jnp.dot(q[rows], state.astype(BF16), preferred_element_type=F32) * xi_ref[hh]
            kz_t = (k[rows] * zeta_ref[hh]).T.astype(BF16)
            state_ref[hh] = state * cd_ref[hh] + jnp.dot(kz_t, v[rows], preferred_element_type=F32)
            chunks.append((gate[rows] * _rms(inner + cross)).astype(BF16))
        heads.append(jnp.concatenate(chunks, axis=0))
    return jnp.concatenate(heads, axis=1)


def _retention_operands(gain, w, S, ts):
    H, C = RET_HEADS, RET_CHUNK
    D = w.shape[0]
    dk = D // H
    half = dk // 2
    nt = S // ts
    pos = jnp.arange(S, dtype=F32)
    inv = 1.0 / (ROPE_BASE ** jnp.linspace(0.0, 1.0, half, dtype=F32))
    ang = pos[:, None] * inv[None, :]
    log_g = jnp.log(1.0 - 2.0 ** (-5.0 - jnp.arange(H, dtype=F32)))
    idx = jnp.arange(C, dtype=F32)
    diff = idx[:, None] - idx[None, :]
    d_in = jnp.where(diff >= 0, jnp.exp(log_g[:, None, None] * jnp.maximum(diff, 0.0)), 0.0)
    xi = jnp.exp(log_g[:, None] * (idx + 1.0))[:, :, None]
    zeta = jnp.exp(log_g[:, None] * (C - 1.0 - idx))[:, :, None]
    chunk_decay = jnp.exp(log_g * C)[:, None, None]
    table = pl.BlockSpec((ts, half), lambda i: (i % nt, 0))
    per_head = lambda shape: _resident((H,) + shape, lambda i: (0, 0, 0))
    args = (gain.reshape(1, D), w, jnp.cos(ang), jnp.sin(ang), d_in, xi, zeta, chunk_decay)
    specs = [_resident((1, D), lambda i: (0, 0)), _resident(w.shape, lambda i: (0, 0)), table, table,
             per_head((C, C)), per_head((C, 1)), per_head((C, 1)), per_head((1, 1))]
    state = pltpu.VMEM((H, dk, 2 * dk), F32)
    return args, specs, state


ROUTE_E1, ROUTE_E2, ROUTE_G1, ROUTE_G2, ROUTE_R1, ROUTE_R2 = range(6)
ROUTE_ROWS = 8
ROUTE_PIECES = 2
ROUTER_EXP_LANE0 = MOE_GROUPS
HI16 = 0xFFFF0000


def _pack_bf16_pair(a, b):
    ua = lax.bitcast_convert_type(a.astype(BF16).astype(F32), jnp.uint32)
    ub = lax.bitcast_convert_type(b.astype(BF16).astype(F32), jnp.uint32)
    return ua | (ub >> 16)


def _unpack_bf16_pair(w):
    a = lax.bitcast_convert_type(w & jnp.uint32(HI16), F32)
    b = lax.bitcast_convert_type(w << 16, F32)
    return a, b


ROW_PARTS = 2


def _part_cols(D, part):
    w = D // 2 // ROW_PARTS
    return slice(part * w, (part + 1) * w), slice(D // 2 + part * w, D // 2 + (part + 1) * w)


def _pack_part(x, part):
    hi, lo = _part_cols(x.shape[1], part)
    return _pack_bf16_pair(x[:, hi], x[:, lo])


def _merge_groups(o_refs, l_refs, ex_ref):
    ex = ex_ref[...]

    def expand(w):
        return jnp.dot(w.astype(BF16), ex, preferred_element_type=F32)

    lses = [l_ref[...] for l_ref in l_refs]
    top = functools.reduce(jnp.maximum, lses)
    ws = [jnp.exp(l - top) for l in lses]
    den = sum(ws)
    last = o_refs[-1][...].astype(F32)
    return (last + sum(expand(w / den) * (o_ref[...].astype(F32) - last)
                       for w, o_ref in zip(ws[:-1], o_refs[:-1]))).astype(BF16)


def _post_kernel(*refs, n_groups, seq_tiles):
    n_a = 2 * n_groups + 1 if n_groups else 8
    a_refs = refs[:n_a]
    (w_ref, h_ref, g_ref, wr_ref, br_ref, tri_ref,
     h1_ref, hn0_ref, hn1_ref, route_ref, route_t_ref, cnt_ref, carry_ref) = refs[n_a:n_a + 13]
    hn_refs = (hn0_ref, hn1_ref)
    if n_groups:
        a = _merge_groups(a_refs[:n_groups], a_refs[n_groups:2 * n_groups], a_refs[-1])
    else:
        a = _retention_tile(h_ref, *a_refs, refs[-1], pl.program_id(0) % seq_tiles == 0)
    @pl.when(pl.program_id(0) == 0)
    def _():
        carry_ref[...] = jnp.zeros_like(carry_ref)

    half = tri_ref.shape[0]
    n_pieces = h_ref.shape[0] // half
    D = h_ref.shape[1]
    lane = lax.broadcasted_iota(jnp.int32, (half, LANES), 1).astype(F32)
    ninf = -jnp.inf

    def first_argmax(vals):
        top = jnp.max(vals, axis=1, keepdims=True)
        where = jnp.min(jnp.where(vals == top, lane, float(LANES)), axis=1, keepdims=True)
        return top, where

    h1_ref[...] = h_ref[...] + jnp.dot(a, w_ref[...], preferred_element_type=F32)

    def route_half(rows):
        hn = _rms(h1_ref[rows, :]) * g_ref[...]
        for part, ref in enumerate(hn_refs):
            ref[rows, :] = _pack_part(hn, part)
        hi = hn.astype(BF16)
        lo = (hn - hi.astype(F32)).astype(BF16)
        both = jnp.dot(hi, wr_ref[...], preferred_element_type=F32)
        logits = (both[:, :LANES] + both[:, LANES:]
                  + jnp.dot(lo, wr_ref[:, :LANES], preferred_element_type=F32) + br_ref[...])

        is_grp = lane < MOE_GROUPS
        lg = jnp.where(is_grp, logits, ninf)
        mg, grp = first_argmax(lg)
        p_grp = 1.0 / jnp.sum(jnp.where(is_grp, jnp.exp(lg - mg), 0.0), axis=1, keepdims=True)

        e_lane = lane - ROUTER_EXP_LANE0
        in_grp = (e_lane < N_EXPERTS) & (jnp.floor(e_lane * (1.0 / EXPERTS_PER_GROUP)) == grp)
        le = jnp.where(in_grp, logits, ninf)
        v1, i1 = first_argmax(le)
        le2 = jnp.where(lane == i1, ninf, le)
        v2, i2 = first_argmax(le2)
        e = jnp.exp(v2 - v1)
        hit1 = lane == (i1 - ROUTER_EXP_LANE0)
        hit2 = lane == (i2 - ROUTER_EXP_LANE0)
        onehot = jnp.where(hit1 | hit2, 1.0, 0.0)
        earlier = jnp.dot(tri_ref[...], onehot.astype(BF16), preferred_element_type=F32)
        return dict(e1=i1 - ROUTER_EXP_LANE0, e2=i2 - ROUTER_EXP_LANE0, g1=p_grp / (1.0 + e),
                    g2=p_grp * e / (1.0 + e), hit1=hit1, hit2=hit2, earlier=earlier,
                    count=jnp.sum(onehot, axis=0, keepdims=True))

    halves = [route_half(slice(j * half, (j + 1) * half)) for j in range(n_pieces)]
    carry = carry_ref[...]
    for j, r in enumerate(halves):
        before = carry + r["earlier"]
        r1 = jnp.sum(jnp.where(r["hit1"], before, 0.0), axis=1, keepdims=True)
        r2 = jnp.sum(jnp.where(r["hit2"], before, 0.0), axis=1, keepdims=True)
        carry = carry + r["count"]
        route = jnp.zeros((half, LANES), F32)
        for slot, val in ((ROUTE_E1, r["e1"]), (ROUTE_E2, r["e2"]), (ROUTE_G1, r["g1"]),
                          (ROUTE_G2, r["g2"]), (ROUTE_R1, r1), (ROUTE_R2, r2)):
            route = jnp.where(lane == slot, val, route)
        route_ref[j * half:(j + 1) * half, :] = route
        route_t_ref[:, j * half:(j + 1) * half] = route.T[:ROUTE_ROWS]
    carry_ref[...] = carry
    cnt_ref[...] = carry


def _post(mixer, w, h, gain, w_grp, b_grp, w_exp, b_exp):
    T, D = h.shape
    K = w.shape[0]
    row_block = lambda width: pl.BlockSpec((tm, width), lambda i: (i, 0))
    scratch = [pltpu.VMEM((1, LANES), F32)]
    if len(mixer) == 2:
        outs, lses = mixer
        n_groups, seq_tiles = len(outs), 1
        tm = min(TOKEN_TILE, T)
        ex = jnp.where(jnp.arange(LANES)[:, None] == (jnp.arange(K)[None, :] // DIL_HEAD_DIM), 1.0, 0.0).astype(BF16)
        a_args = (*outs, *lses, ex)
        a_specs = [row_block(K)] * n_groups + [row_block(LANES)] * n_groups + [_resident((LANES, K), lambda i: (0, 0))]
    else:
        mix_gain, w_in, S = mixer
        tm = min(TOKEN_TILE, S)
        n_groups, seq_tiles = 0, S // tm
        a_args, a_specs, state = _retention_operands(mix_gain, w_in, S, tm)
        scratch.append(state)
    n_r = MOE_GROUPS + N_EXPERTS
    wr = jnp.zeros((D, LANES), F32).at[:, :n_r].set(jnp.concatenate([w_grp, w_exp], axis=1))
    wr_hi = wr.astype(BF16)
    wr_lo = (wr - wr_hi.astype(F32)).astype(BF16)
    wr2 = jnp.concatenate([wr_hi, wr_lo], axis=1)
    br = jnp.zeros((1, LANES), F32).at[0, :n_r].set(jnp.concatenate([b_grp, b_exp]))
    piece = tm // ROUTE_PIECES
    tri = jnp.tril(jnp.ones((piece, piece), BF16), k=-1)
    return pl.pallas_call(
        functools.partial(_post_kernel, n_groups=n_groups, seq_tiles=seq_tiles),
        grid=(T // tm,),
        in_specs=a_specs + [
                  _resident((K, D), lambda i: (0, 0)),
                  pl.BlockSpec((tm, D), lambda i: (i, 0)),
                  _resident((1, D), lambda i: (0, 0)),
                  _resident((D, 2 * LANES), lambda i: (0, 0)),
                  _resident((1, LANES), lambda i: (0, 0)),
                  _resident((piece, piece), lambda i: (0, 0))],
        out_specs=[pl.BlockSpec((tm, D), lambda i: (i, 0))]
                  + [pl.BlockSpec((tm, D // 2 // ROW_PARTS), lambda i: (i, 0))] * ROW_PARTS
                  + [pl.BlockSpec((tm, LANES), lambda i: (i, 0)),
                   pl.BlockSpec((ROUTE_ROWS, tm), lambda i: (0, i)),
                   pl.BlockSpec((1, LANES), lambda i: (0, 0))],
        out_shape=[jax.ShapeDtypeStruct((T, D), F32)]
                  + [jax.ShapeDtypeStruct((T, D // 2 // ROW_PARTS), jnp.uint32)] * ROW_PARTS
                  + [jax.ShapeDtypeStruct((T, LANES), F32),
                   jax.ShapeDtypeStruct((ROUTE_ROWS, T), F32),
                   jax.ShapeDtypeStruct((1, LANES), F32)],
        scratch_shapes=scratch,
        compiler_params=_params("arbitrary"),
        name="post_mixer",
    )(*a_args, w, h, gain.reshape(1, D), wr2, br, tri)


SC_WINDOW = 128


def _sc_mesh():
    return plsc.VectorSubcoreMesh(core_axis_name="core", subcore_axis_name="subcore")


def _sc_window_specs(W):
    rows = pl.BlockSpec((SC_WINDOW, W), lambda i: (i, 0))
    idx = pl.BlockSpec((1, SC_WINDOW), lambda i: (0, i))
    return rows, idx


def _sc_pipeline(body, n_rows, in_specs, out_specs):
    return pltpu.emit_pipeline(body, grid=(n_rows // SC_WINDOW,), in_specs=in_specs, out_specs=out_specs,
                               core_axis_name=("core", "subcore"), dimension_semantics=(pltpu.PARALLEL,))


def _dispatch(xs, dest, pad_idx, P):
    T, W = xs[0].shape
    n_pad = pad_idx.shape[1]
    n = len(xs)
    rows, idx = _sc_window_specs(W)
    zero_rows = pl.BlockSpec((SC_WINDOW, W), lambda i: (0, 0))
    out = jax.ShapeDtypeStruct((P, W), xs[0].dtype)

    @functools.partial(pl.kernel, out_type=(out,) * n, mesh=_sc_mesh(), scratch_types=[], name="dispatch")
    def scatter(*refs):
        x_hbm, (d0_hbm, d1_hbm, z_hbm, p_hbm), o_hbm = refs[:n], refs[n:n + 4], refs[n + 4:]
        for x, o in zip(x_hbm, o_hbm):
            def put_pair(x_vmem, i0_vmem, i1_vmem, o=o):
                pltpu.sync_copy(x_vmem, o.at[i0_vmem.at[0]])
                pltpu.sync_copy(x_vmem, o.at[i1_vmem.at[0]])

            def put(x_vmem, i_vmem, o=o):
                pltpu.sync_copy(x_vmem, o.at[i_vmem.at[0]])

            _sc_pipeline(put_pair, T, [rows, idx, idx], [])(x, d0_hbm, d1_hbm)
            _sc_pipeline(put, n_pad, [zero_rows, idx], [])(z_hbm, p_hbm)

    return scatter(*xs, dest[0:1], dest[1:2], jnp.zeros((SC_WINDOW, W), xs[0].dtype), pad_idx)


WEIGHT_LEADS = (3, 2, 1)
EXPERT_SLOTS = max(WEIGHT_LEADS) + 1


def _expert_kernel(blk_exp_ref, slot_ref, n_used_ref, *refs):
    x_refs = refs[:ROW_PARTS]
    w_refs = refs[ROW_PARTS:ROW_PARTS + 3]
    o_refs = refs[ROW_PARTS + 3:2 * ROW_PARTS + 3]
    w_slots = refs[2 * ROW_PARTS + 3:]
    lead = max(WEIGHT_LEADS)
    n_used = n_used_ref[0]
    i = pl.program_id(0) - lead
    D = w_slots[0].shape[1]

    for w_ref, w_s, ahead in zip(w_refs, w_slots, WEIGHT_LEADS):
        j = i + ahead
        jc = jnp.clip(j, 0, n_used - 1)
        arrived = (j >= 0) & (j < n_used) & ((j == 0) | (blk_exp_ref[jc] != blk_exp_ref[jnp.maximum(jc - 1, 0)]))

        @pl.when(arrived)
        def _(w_ref=w_ref, w_s=w_s, jc=jc):
            w_s[slot_ref[jc]] = w_ref[0, 0].astype(BF16)

    @pl.when((i >= 0) & (i < n_used))
    def _():
        slot = slot_ref[jnp.clip(i, 0, n_used - 1)]
        wg_s, wu_s, wd_s = w_slots
        pieces = []
        for part, x_ref in enumerate(x_refs):
            for cols, val in zip(_part_cols(D, part), _unpack_bf16_pair(x_ref[...])):
                pieces.append((cols, val.astype(BF16)))

        def up(w_s):
            return sum(jnp.dot(val, w_s[slot, cols, :], preferred_element_type=F32) for cols, val in pieces)

        g = up(wg_s)
        hid = (g * jax.nn.sigmoid(g) * up(wu_s)).astype(BF16)
        y = jnp.dot(hid, wd_s[slot], preferred_element_type=F32)
        for part, o_ref in enumerate(o_refs):
            o_ref[...] = _pack_part(y, part)

    @pl.when(i >= n_used)
    def _():
        for o_ref in o_refs:
            o_ref[...] = jnp.zeros_like(o_ref)


def _experts(xs, blk_exp, n_used, layer, w_gate, w_up, w_down):
    P, W = xs[0].shape
    D = 2 * W * ROW_PARTS
    FF = w_gate.shape[3]
    nblk = P // MOE_BLOCK
    lead = max(WEIGHT_LEADS)
    changes = jnp.concatenate([jnp.zeros((1,), jnp.int32), (blk_exp[1:] != blk_exp[:-1]).astype(jnp.int32)])
    slot = jnp.cumsum(changes) % EXPERT_SLOTS

    def x_map(g, be, sl, nu):
        return (jnp.clip(g - lead, 0, nu[0] - 1), 0)

    def o_map(g, be, sl, nu):
        return (jnp.maximum(g - lead, 0), 0)

    def w_map(ahead):
        return lambda g, be, sl, nu: (layer, be[jnp.clip(g - lead + ahead, 0, nu[0] - 1)], 0, 0)

    wg_spec, wu_spec, wd_spec = (pl.BlockSpec((1, 1) + shape, w_map(ahead))
                                 for shape, ahead in zip(((D, FF), (D, FF), (FF, D)), WEIGHT_LEADS))
    return pl.pallas_call(
        _expert_kernel,
        grid_spec=pltpu.PrefetchScalarGridSpec(
            num_scalar_prefetch=3,
            grid=(nblk + lead,),
            in_specs=[pl.BlockSpec((MOE_BLOCK, W), x_map)] * ROW_PARTS + [wg_spec, wu_spec, wd_spec],
            out_specs=[pl.BlockSpec((MOE_BLOCK, W), o_map)] * ROW_PARTS,
            scratch_shapes=[pltpu.VMEM((EXPERT_SLOTS, D, FF), BF16), pltpu.VMEM((EXPERT_SLOTS, D, FF), BF16),
                            pltpu.VMEM((EXPERT_SLOTS, FF, D), BF16)]),
        out_shape=[jax.ShapeDtypeStruct((P, W), jnp.uint32)] * ROW_PARTS,
        compiler_params=_params("arbitrary"),
        name="experts",
    )(blk_exp, slot, n_used, *xs, w_gate, w_up, w_down)


def _gather_pairs(ys, dest):
    W = ys[0].shape[1]
    T = dest.shape[1]
    n = len(ys)
    rows, idx = _sc_window_specs(W)
    out = jax.ShapeDtypeStruct((T, W), ys[0].dtype)

    @functools.partial(pl.kernel, out_type=(out,) * (2 * n), mesh=_sc_mesh(), scratch_types=[],
                       name="gather_pairs")
    def gather(*refs):
        y_hbm, d_hbm, o_hbm = refs[:n], refs[n:n + 2], refs[n + 2:]
        for slot, d in enumerate(d_hbm):
            for y, o in zip(y_hbm, o_hbm[slot * n:(slot + 1) * n]):
                def get(i_vmem, o_vmem, y=y):
                    pltpu.sync_copy(y.at[i_vmem.at[0]], o_vmem)

                _sc_pipeline(get, T, [idx], [rows])(d, o)

    return gather(*ys, dest[0:1], dest[1:2])


def _combined_rows(h_ref, route_ref, y_refs):
    D = h_ref.shape[1]
    route = route_ref[...]
    gates = (route[:, ROUTE_G1:ROUTE_G1 + 1], route[:, ROUTE_G2:ROUTE_G2 + 1])
    pieces = {}
    for part in range(ROW_PARTS):
        slots = [_unpack_bf16_pair(y_refs[slot * ROW_PARTS + part][...]) for slot in range(2)]
        for half, cols in enumerate(_part_cols(D, part)):
            pieces[cols.start] = h_ref[:, cols] + (gates[0] * slots[0][half] + gates[1] * slots[1][half])
    return jnp.concatenate([pieces[c] for c in sorted(pieces)], axis=1)


def _combine_kernel(h_ref, route_ref, *refs):
    refs[-1][...] = _combined_rows(h_ref, route_ref, refs[:-1])


def _combine(h, route, pairs):
    T, D = h.shape
    W = pairs[0].shape[1]
    tm = min(ROW_TILE, T)
    return pl.pallas_call(
        _combine_kernel,
        grid=(T // tm,),
        in_specs=[pl.BlockSpec((tm, D), lambda i: (i, 0)), pl.BlockSpec((tm, LANES), lambda i: (i, 0))]
                 + [pl.BlockSpec((tm, W), lambda i: (i, 0))] * len(pairs),
        out_specs=pl.BlockSpec((tm, D), lambda i: (i, 0)),
        out_shape=jax.ShapeDtypeStruct((T, D), F32),
        compiler_params=_params("parallel"),
        name="combine",
    )(h, route, *pairs)


def _moe(h1, hn, route, route_t, counts, layer, w_gate, w_up, w_down, combine):
    T, D = h1.shape
    A = 2 * T
    nblk = -(-A // MOE_BLOCK) + N_EXPERTS
    P = nblk * MOE_BLOCK
    eid = route_t[ROUTE_E1:ROUTE_E2 + 1].astype(jnp.int32)
    rank = route_t[ROUTE_R1:ROUTE_R2 + 1].astype(jnp.int32)
    cnt = counts[0, :N_EXPERTS].astype(jnp.int32)
    padded = (cnt + MOE_BLOCK - 1) // MOE_BLOCK * MOE_BLOCK
    pends = jnp.cumsum(padded)
    pstarts = pends - padded
    experts = jnp.arange(N_EXPERTS, dtype=jnp.int32)
    start_of = jnp.sum(jnp.where(eid[:, None, :] == experts[None, :, None], pstarts[None, :, None], 0), axis=1)
    dest = start_of + rank
    blk_start = jnp.arange(nblk, dtype=jnp.int32) * MOE_BLOCK
    blk_exp = jnp.minimum(jnp.sum((pends[None, :] <= blk_start[:, None]).astype(jnp.int32), axis=1),
                          N_EXPERTS - 1)
    n_used = pends[-1:] // MOE_BLOCK

    gap_start = jnp.concatenate([pstarts + cnt, pends[-1:]])
    gap_len = jnp.concatenate([padded - cnt, P - pends[-1:]])
    gap_end = jnp.cumsum(gap_len)
    j = jnp.arange(P - A, dtype=jnp.int32)
    gap_of = jnp.sum((gap_end[None, :] <= j[:, None]).astype(jnp.int32), axis=1)
    sel = gap_of[:, None] == jnp.arange(N_EXPERTS + 1, dtype=jnp.int32)[None, :]
    pad_idx = jnp.sum(jnp.where(sel, (gap_start - (gap_end - gap_len))[None, :] + j[:, None], 0), axis=1)

    xs = _dispatch(hn, dest, pad_idx.reshape(1, P - A), P)
    ys = _experts(xs, blk_exp, n_used, layer, w_gate, w_up, w_down)
    pairs = _gather_pairs(ys, dest)
    return _combine(h1, route, pairs) if combine else (h1, route, pairs)


def _qkv_kernel(*refs, rate, n, n_chunks, n_pending, permute):
    n_x = 2 + n_pending if n_pending else 1
    x_refs = refs[:n_x]
    gq_ref, gkv_ref, wq_ref, wk_ref, wvt_ref, qn_ref, kn_ref, seg_ref = refs[n_x:n_x + 8]
    perm_ref = refs[n_x + 8] if permute else None
    q_ref, k_ref, vt_ref = refs[n_x + 8 + permute:n_x + 11 + permute]
    xs_ref = refs[-1]
    seg = seg_ref[...]
    width = seg.shape[0]
    res_per_chunk = q_ref.shape[1]
    chunk = pl.program_id(2)

    def head_norm(t, gain):
        cols = []
        for j in range(t.shape[1] // width):
            tj = t[:, j * width:(j + 1) * width]
            ms = jnp.dot((tj * tj).astype(BF16), seg, preferred_element_type=F32)
            cols.append(tj * lax.rsqrt(ms + EPS))
        return jnp.concatenate(cols, axis=1) * gain

    def normed(x):
        y = _rms(x)
        return (y * gq_ref[...]).astype(BF16), (y * gkv_ref[...]).astype(BF16)

    def project(xq, xkv):
        q = head_norm(jnp.dot(xq, wq_ref[...], preferred_element_type=F32), qn_ref[...]).astype(q_ref.dtype)
        k = head_norm(jnp.dot(xkv, wk_ref[...], preferred_element_type=F32), kn_ref[...]).astype(k_ref.dtype)
        vt = lax.dot_general(wvt_ref[...], xkv, NT_DIMS, preferred_element_type=F32).astype(vt_ref.dtype)
        for j in range(res_per_chunk):
            q_ref[0, j] = q[j * n:(j + 1) * n]
            k_ref[0, j] = k[j * n:(j + 1) * n]
            vt_ref[0, j] = vt[:, j * n:(j + 1) * n]

    if n_pending:
        x = _combined_rows(x_refs[0], x_refs[1], x_refs[2:])
        refs[-2][...] = x
        project(*normed(x))
        return
    x_ref, = x_refs
    if rate == 1:
        project(*normed(x_ref[...]))
        return

    if permute:
        group = perm_ref.shape[0]
        n_groups, run = x_ref.shape[0] // group, group // rate

        @pl.when(chunk == 0)
        def _():
            for g in range(n_groups):
                rows = slice(g * group, (g + 1) * group)
                for t, xt in enumerate(normed(x_ref[rows, :])):
                    xs_ref[t, rows, :] = jnp.dot(perm_ref[...], xt, preferred_element_type=F32).astype(BF16)

        span = res_per_chunk * run
        start = pl.multiple_of(chunk * span, span)

        def residue_major(t):
            parts = [xs_ref[t, pl.ds(g * group + start, span), :] for g in range(n_groups)]
            return jnp.concatenate([p[j * run:(j + 1) * run] for j in range(res_per_chunk) for p in parts], axis=0)

        project(residue_major(0), residue_major(1))
        return

    @pl.when(chunk == 0)
    def _():
        for j in range(xs_ref.shape[0]):
            xs_ref[j] = x_ref[:, j * LANES:(j + 1) * LANES]

    for ch in range(n_chunks):
        @pl.when(chunk == ch)
        def _(ch=ch):
            residues = range(ch * res_per_chunk, (ch + 1) * res_per_chunk)
            project(*normed(jnp.concatenate(
                [jnp.concatenate([xs_ref[j, pl.ds(c, n, stride=rate), :] for j in range(xs_ref.shape[0])], axis=1)
                 for c in residues], axis=0)))


def _qkv(h, B, S, rate, gq, gkv, wq, wk, wv, qn, kn):
    pending = h if isinstance(h, tuple) else None
    D = (pending[0] if pending else h).shape[1]
    L = S // rate
    hd = DIL_HEAD_DIM
    n = max(SUB_BLOCK, TOKEN_TILE // rate)
    tm = n * rate
    res_per_chunk = max(1, TOKEN_TILE // n)
    n_chunks = rate // res_per_chunk
    width = 2 * LANES
    ii = jnp.arange(width)
    seg = jnp.where((ii[:, None] // hd) == (ii[None, :] // hd), 1.0 / hd, 0.0).astype(BF16)
    permute = rate * BF16_ROWS == PERM_ROWS
    if permute:
        pp = jnp.arange(PERM_ROWS)
        perm = ((pp[:, None] // BF16_ROWS == pp[None, :] % rate)
                & (pp[:, None] % BF16_ROWS == pp[None, :] // rate)).astype(BF16)
    row = lambda g: jnp.tile(g, D // hd).reshape(1, D)
    const = lambda shape: _resident(shape, lambda b, i, c: (0,) * len(shape))
    qk_spec = pl.BlockSpec((1, res_per_chunk, n, D), lambda b, i, c: (b, c, i, 0))
    qk_shape = jax.ShapeDtypeStruct((B, rate, L, D), BF16)
    rows = lambda width: pl.BlockSpec((tm, width), lambda b, i, c: (b * (S // tm) + i, 0))
    if pending:
        assert rate == 1
        h1, route, pairs = pending
        x_args = (h1, route, *pairs)
        x_specs = [rows(D), rows(LANES)] + [rows(pairs[0].shape[1])] * len(pairs)
        extra_specs, extra_shapes = [rows(D)], [jax.ShapeDtypeStruct((B * S, D), F32)]
    else:
        x_args, x_specs, extra_specs, extra_shapes = (h,), [rows(D)], [], []
    return pl.pallas_call(
        functools.partial(_qkv_kernel, rate=rate, n=n, n_chunks=n_chunks,
                          n_pending=len(pending[2]) if pending else 0, permute=permute),
        grid=(B, S // tm, n_chunks),
        in_specs=x_specs + [
                  const((1, D)), const((1, D)), const((D, D)), const((D, D)), const((D, D)),
                  const((1, D)), const((1, D)), const((width, width))]
                 + ([const((PERM_ROWS, PERM_ROWS))] if permute else []),
        out_specs=[qk_spec, qk_spec,
                   pl.BlockSpec((1, res_per_chunk, D, n), lambda b, i, c: (b, c, 0, i))] + extra_specs,
        out_shape=[qk_shape, qk_shape, jax.ShapeDtypeStruct((B, rate, D, L), BF16)] + extra_shapes,
        scratch_shapes=[pltpu.VMEM((2, tm, D), BF16) if permute else
                        pltpu.VMEM((D // LANES, tm if rate > 1 else 8, LANES), F32)],
        compiler_params=_params("parallel", "parallel", "arbitrary"),
        name="qkv_rate%d" % rate,
    )(*x_args, gq.reshape(1, D), gkv.reshape(1, D), wq, wk, wv.T, row(qn) * (hd ** -0.5 * LOG2E), row(kn), seg,
      *((perm,) if permute else ()))


def _attn_kernel(*refs, rate, with_prev):
    if not with_prev:
        q_ref, kc_ref, vc_ref, edge_ref, o_ref, lse_ref, os_ref = refs
    elif len(refs) == 10:
        q_ref, kc_ref, kp_ref, vc_ref, vp_ref, edge_ref, bias_ref, o_ref, lse_ref, os_ref = refs
    else:
        q_ref, kc_ref, kp_ref, vc_ref, vp_ref, edge_ref, o_ref, lse_ref, os_ref = refs
    Bk = SUB_BLOCK
    hd = DIL_HEAD_DIM
    n_pairs = DIL_HEADS // 2
    lane = lax.broadcasted_iota(jnp.int32, (Bk, LANES), 1)

    def block(load_q, load_k, load_vt, load_bias, store):
        out_t, lse_t = [], []
        for p in range(n_pairs):
            qp = load_q(p)
            zero = jnp.zeros_like(qp)
            q2 = jnp.concatenate([jnp.where(lane < hd, qp, zero), jnp.where(lane >= hd, qp, zero)], axis=0)
            s = lax.dot_general(load_k(p), q2, NT_DIMS, preferred_element_type=F32) + load_bias(p)
            m = jnp.max(s, axis=0, keepdims=True)
            pr = jnp.exp2(s - m)
            l = jnp.sum(pr, axis=0, keepdims=True)
            pb = pr.astype(BF16)
            vt = load_vt(p)
            out_t.append(jnp.dot(vt[:hd], pb[:, :Bk], preferred_element_type=F32) / l[:, :Bk])
            out_t.append(jnp.dot(vt[hd:], pb[:, Bk:], preferred_element_type=F32) / l[:, Bk:])
            lse = (m + jnp.log2(l)) * LN2
            lse_t += [lse[:, :Bk], lse[:, Bk:]]
        lse_t.append(jnp.zeros((LANES - DIL_HEADS, Bk), F32))
        store(jnp.concatenate(out_t, axis=0).T, jnp.concatenate(lse_t, axis=0).T)

    def pair_cols(p):
        return slice(p * LANES, (p + 1) * LANES)

    n_res, n_blocks = kc_ref.shape[1], kc_ref.shape[2] // Bk
    for c in range(n_res):
        for j in range(n_blocks):
            rows = slice(j * Bk, (j + 1) * Bk)
            if not with_prev:
                load_k = lambda p, c=c: kc_ref[0, c, :, pair_cols(p)]
                load_vt = lambda p, c=c: vc_ref[0, c, pair_cols(p), :]
                load_bias = lambda p: edge_ref[0, p]
            elif j == 0:
                load_k = lambda p, c=c: jnp.concatenate([kp_ref[0, c, :, pair_cols(p)],
                                                         kc_ref[0, c, :Bk, pair_cols(p)]], axis=0)
                load_vt = lambda p, c=c: jnp.concatenate([vp_ref[0, c, pair_cols(p), :],
                                                          vc_ref[0, c, pair_cols(p), :Bk]], axis=1)
                load_bias = lambda p: edge_ref[0, p]
            else:
                krows = slice((j - 1) * Bk, (j + 1) * Bk)
                load_k = lambda p, c=c, krows=krows: kc_ref[0, c, krows, pair_cols(p)]
                load_vt = lambda p, c=c, krows=krows: vc_ref[0, c, pair_cols(p), krows]
                load_bias = lambda p: bias_ref[p]

            if rate == 1:
                def store(o, lse, rows=rows):
                    o_ref[rows, :] = o.astype(o_ref.dtype)
                    lse_ref[rows, :] = lse
            else:
                def store(o, lse, out_rows=pl.ds(c + j * Bk * rate, Bk, stride=rate)):
                    for g in range(os_ref.shape[0]):
                        os_ref[g, out_rows, :] = o[:, pair_cols(g)]
                    lse_ref[out_rows, :] = lse

            block(lambda p, c=c, rows=rows: q_ref[0, c, rows, pair_cols(p)], load_k, load_vt, load_bias, store)
    if rate > 1:
        for g in range(os_ref.shape[0]):
            o_ref[:, pair_cols(g)] = os_ref[g].astype(o_ref.dtype)


def _t5_bucket(n):
    max_exact = NUM_BUCKETS // 2
    nf = jnp.maximum(n, max_exact).astype(F32)
    large = max_exact + (jnp.log(nf / max_exact) / math.log(MAX_DISTANCE / max_exact)
                         * (NUM_BUCKETS - max_exact)).astype(jnp.int32)
    large = jnp.minimum(large, NUM_BUCKETS - 1)
    return jnp.where(n < max_exact, n, large)


def _group_attention(q, k, vt, bias_table, rate, n_steps):
    B, _, L, D = q.shape
    S = L * rate
    Bk = SUB_BLOCK
    with_prev = L > Bk
    nk = 2 * Bk if with_prev else Bk
    n = max(Bk, ATTN_TILE // rate)
    tm = n * rate
    nt = S // tm
    n_pairs = DIL_HEADS // 2
    ql = jnp.arange(Bk, dtype=jnp.int32)[:, None]
    kl = jnp.arange(2 * Bk, dtype=jnp.int32)[None, :]
    steps = ql + Bk - kl
    bucket = _t5_bucket(jnp.maximum(steps, 0) * rate)
    buckets = jnp.arange(NUM_BUCKETS, dtype=jnp.int32)
    bias = jnp.sum(jnp.where(bucket[None, :, :, None] == buckets[:, None, None, None],
                             bias_table.astype(F32)[:, None, None, :], 0.0), axis=0)
    band = ((steps >= 0) & (steps <= n_steps))[:, :, None]
    first = (kl < Bk)[:, :, None]

    def layout(t):
        t = t[:, 2 * Bk - nk:, :].transpose(1, 2, 0)
        return t.reshape(nk, n_pairs, 2 * Bk).transpose(1, 0, 2)

    bias_in = layout(jnp.where(band, bias * LOG2E, NEG_INF))
    bias_first = layout(jnp.where(band & ~first, bias * LOG2E, NEG_INF))
    edge = jnp.stack([bias_first, bias_in])

    cur_qk = pl.BlockSpec((1, rate, n, D), lambda b, i: (b, 0, i, 0))
    cur_vt = pl.BlockSpec((1, rate, D, n), lambda b, i: (b, 0, 0, i))
    per_n = n // Bk
    prev_qk = pl.BlockSpec((1, rate, Bk, D), lambda b, i: (b, 0, jnp.maximum(i * per_n - 1, 0), 0))
    prev_vt = pl.BlockSpec((1, rate, D, Bk), lambda b, i: (b, 0, 0, jnp.maximum(i * per_n - 1, 0)))
    edge_spec = pl.BlockSpec((1, n_pairs, nk, 2 * Bk), lambda b, i: (jnp.minimum(i, 1), 0, 0, 0))
    if with_prev:
        in_specs = [cur_qk, cur_qk, prev_qk, cur_vt, prev_vt, edge_spec]
        args = (q, k, k, vt, vt, edge)
    else:
        in_specs = [cur_qk, cur_qk, cur_vt, edge_spec]
        args = (q, k, vt, edge)
    if with_prev and n > Bk:
        in_specs.append(_resident((n_pairs, nk, 2 * Bk), lambda b, i: (0, 0, 0)))
        args += (bias_in,)
    return pl.pallas_call(
        functools.partial(_attn_kernel, rate=rate, with_prev=with_prev),
        grid=(B, nt),
        in_specs=in_specs,
        out_specs=[pl.BlockSpec((tm, D), lambda b, i: (b * nt + i, 0)),
                   pl.BlockSpec((tm, LANES), lambda b, i: (b * nt + i, 0))],
        out_shape=[jax.ShapeDtypeStruct((B * S, D), BF16),
                   jax.ShapeDtypeStruct((B * S, LANES), F32)],
        scratch_shapes=[pltpu.VMEM((D // LANES, tm if rate > 1 else 8, LANES), F32)],
        compiler_params=_params("parallel", "parallel"),
        name="attn_rate%d" % rate,
    )(*args)


def _cast_kernel(x_ref, o_ref):
    cols = o_ref.shape[2]
    for j in range(o_ref.shape[0]):
        o_ref[j] = x_ref[:, j * cols:(j + 1) * cols].astype(o_ref.dtype)


def _bf16_column_blocks(w, cols):
    rows, n = w.shape[0], w.shape[1] // cols
    block = min(ROW_TILE // 4, rows)
    assert w.shape[1] == n * cols and rows % block == 0
    return pl.pallas_call(
        _cast_kernel, grid=(rows // block,),
        in_specs=[pl.BlockSpec((block, n * cols), lambda i: (i, 0))],
        out_specs=pl.BlockSpec((n, block, cols), lambda i: (0, i, 0)),
        out_shape=jax.ShapeDtypeStruct((n, rows, cols), BF16),
        compiler_params=_params("parallel"), name="bf16_column_blocks",
    )(w)


def kernel(x, ret_w_in, ret_w_out, kv_norm, w_kv, k_norm, dil_wq, q_norm, dil_wo, rel_bias,
           mixer_norm, ffn_norm, router_grp, router_grp_b, router_exp, router_exp_b,
           exp_gate, exp_up, exp_down):
    B, S, D = x.shape
    h = x.reshape(B * S, D)

    def moe_layer(layer, mixer, w_out, h, combine):
        h1, *hn, route, route_t, counts = _post(
            mixer, w_out.astype(BF16), h, ffn_norm[layer], router_grp[layer], router_grp_b[layer],
            router_exp[layer], router_exp_b[layer])
        return _moe(h1, hn, route, route_t, counts, layer, exp_gate, exp_up, exp_down, combine)

    h = moe_layer(0, (mixer_norm[0], ret_w_in[0].astype(BF16), S), ret_w_out[0], h, combine=False)

    G = len(DIL_RATES)
    gd = DIL_HEADS * DIL_HEAD_DIM
    outs, lses = [], []
    w_kv_blocks = _bf16_column_blocks(w_kv, gd)
    for g in range(G):
        cq = slice(g * gd, (g + 1) * gd)
        q, k, vt, *combined = _qkv(h, B, S, DIL_RATES[g], mixer_norm[1], kv_norm,
                                   dil_wq[0][:, cq].astype(BF16), w_kv_blocks[g],
                                   w_kv_blocks[G + g], q_norm[0][g], k_norm[g])
        if combined:
            h, = combined
        o, lse = _group_attention(q, k, vt, rel_bias[:, g * DIL_HEADS:(g + 1) * DIL_HEADS],
                                  DIL_RATES[g], DIL_WINDOWS[g] // DIL_RATES[g])
        outs.append(o)
        lses.append(lse)
    h = moe_layer(1, (tuple(outs), tuple(lses)), dil_wo[0], h, combine=True)
    return h.reshape(B, S, D)
```

```python
import functools
import math

import jax
import jax.numpy as jnp
from jax import lax
from jax.experimental import pallas as pl
from jax.experimental.pallas import tpu as pltpu
from jax.experimental.pallas import tpu_sc as plsc

F32 = jnp.float32
BF16 = jnp.bfloat16

EPS = 1e-6
NEG_INF = -1e30

RET_HEADS = 4
RET_CHUNK = 256
ROPE_BASE = 10000.0

DIL_WINDOWS = (128, 512, 2048)
DIL_RATES = (1, 4, 16)
DIL_HEADS = 16
DIL_HEAD_DIM = 64
SUB_BLOCK = 128
NUM_BUCKETS = 32
MAX_DISTANCE = 2048

MOE_GROUPS = 4
EXPERTS_PER_GROUP = 8
N_EXPERTS = MOE_GROUPS * EXPERTS_PER_GROUP
MOE_BLOCK = 512

LANES = 128
BF16_ROWS = 16
PERM_ROWS = 256
ROW_TILE = 1024
TOKEN_TILE = 512
ATTN_TILE = 2048
VMEM_LIMIT = 56 * 1024 * 1024

NT_DIMS = (((1,), (1,)), ((), ()))
LOG2E = math.log2(math.e)
LN2 = math.log(2.0)


def _params(*sem):
    return pltpu.CompilerParams(dimension_semantics=sem, vmem_limit_bytes=VMEM_LIMIT)


def _resident(shape, index_map):
    return pl.BlockSpec(shape, index_map, pipeline_mode=pl.Buffered(1))


def _rms(x):
    return x * lax.rsqrt(jnp.mean(x * x, axis=-1, keepdims=True) + EPS)


def _retention_tile(x_ref, g_ref, w_ref, cos_ref, sin_ref, din_ref, xi_ref, zeta_ref, cd_ref, state_ref,
                    first_tile):
    C = RET_CHUNK
    H, dk, dv = state_ref.shape
    half = dk // 2
    ts = x_ref.shape[0]
    k_scale = dk ** -0.5

    @pl.when(first_tile)
    def _():
        state_ref[...] = jnp.zeros_like(state_ref)

    xn = (_rms(x_ref[...]) * g_ref[...]).astype(BF16)
    cos, sin = cos_ref[...], sin_ref[...]

    def proj(c0, width):
        return jnp.dot(xn, w_ref[:, c0:c0 + width], preferred_element_type=F32)

    def rot(t):
        t1, t2 = t[:, :half], t[:, half:]
        return jnp.concatenate([t1 * cos - t2 * sin, t1 * sin + t2 * cos], axis=1)

    heads = []
    for hh in range(H):
        q = rot(proj(hh * dk, dk)).astype(BF16)
        k = rot(proj(H * dk + hh * dk, dk)) * k_scale
        kb = k.astype(BF16)
        v = proj(2 * H * dk + hh * dv, dv).astype(BF16)
        gate = proj(2 * H * dk + H * dv + hh * dv, dv)
        gate = gate * jax.nn.sigmoid(gate)
        chunks = []
        for c in range(ts // C):
            rows = slice(c * C, (c + 1) * C)
            s = lax.dot_general(q[rows], kb[rows], NT_DIMS, preferred_element_type=F32) * din_ref[hh]
            inner = jnp.dot(s.astype(BF16), v[rows], preferred_element_type=F32)
            state = state_ref[hh]
            cross = jnp.dot(q[rows], state.astype(BF16), preferred_element_type=F32) * xi_ref[hh]
            kz_t = (k[rows] * zeta_ref[hh]).T.astype(BF16)
            state_ref[hh] = state * cd_ref[hh] + jnp.dot(kz_t, v[rows], preferred_element_type=F32)
            chunks.append((gate[rows] * _rms(inner + cross)).astype(BF16))
        heads.append(jnp.concatenate(chunks, axis=0))
    return jnp.concatenate(heads, axis=1)


def _retention_operands(gain, w, S, ts):
    H, C = RET_HEADS, RET_CHUNK
    D = w.shape[0]
    dk = D // H
    half = dk // 2
    nt = S // ts
    pos = jnp.arange(S, dtype=F32)
    inv = 1.0 / (ROPE_BASE ** jnp.linspace(0.0, 1.0, half, dtype=F32))
    ang = pos[:, None] * inv[None, :]
    log_g = jnp.log(1.0 - 2.0 ** (-5.0 - jnp.arange(H, dtype=F32)))
    idx = jnp.arange(C, dtype=F32)
    diff = idx[:, None] - idx[None, :]
    d_in = jnp.where(diff >= 0, jnp.exp(log_g[:, None, None] * jnp.maximum(diff, 0.0)), 0.0)
    xi = jnp.exp(log_g[:, None] * (idx + 1.0))[:, :, None]
    zeta = jnp.exp(log_g[:, None] * (C - 1.0 - idx))[:, :, None]
    chunk_decay = jnp.exp(log_g * C)[:, None, None]
    table = pl.BlockSpec((ts, half), lambda i: (i % nt, 0))
    per_head = lambda shape: _resident((H,) + shape, lambda i: (0, 0, 0))
    args = (gain.reshape(1, D), w, jnp.cos(ang), jnp.sin(ang), d_in, xi, zeta, chunk_decay)
    specs = [_resident((1, D), lambda i: (0, 0)), _resident(w.shape, lambda i: (0, 0)), table, table,
             per_head((C, C)), per_head((C, 1)), per_head((C, 1)), per_head((1, 1))]
    state = pltpu.VMEM((H, dk, 2 * dk), F32)
    return args, specs, state


ROUTE_E1, ROUTE_E2, ROUTE_G1, ROUTE_G2, ROUTE_R1, ROUTE_R2 = range(6)
ROUTE_ROWS = 8
ROUTE_PIECES = 2
ROUTER_EXP_LANE0 = MOE_GROUPS
HI16 = 0xFFFF0000


def _pack_bf16_pair(a, b):
    ua = lax.bitcast_convert_type(a.astype(BF16).astype(F32), jnp.uint32)
    ub = lax.bitcast_convert_type(b.astype(BF16).astype(F32), jnp.uint32)
    return ua | (ub >> 16)


def _unpack_bf16_pair(w):
    a = lax.bitcast_convert_type(w & jnp.uint32(HI16), F32)
    b = lax.bitcast_convert_type(w << 16, F32)
    return a, b


ROW_PARTS = 2


def _part_cols(D, part):
    w = D // 2 // ROW_PARTS
    return slice(part * w, (part + 1) * w), slice(D // 2 + part * w, D // 2 + (part + 1) * w)


def _pack_part(x, part):
    hi, lo = _part_cols(x.shape[1], part)
    return _pack_bf16_pair(x[:, hi], x[:, lo])


def _merge_groups(o_refs, l_refs, ex_ref):
    ex = ex_ref[...]

    def expand(w):
        return jnp.dot(w.astype(BF16), ex, preferred_element_type=F32)

    lses = [l_ref[...] for l_ref in l_refs]
    top = functools.reduce(jnp.maximum, lses)
    ws = [jnp.exp(l - top) for l in lses]
    den = sum(ws)
    last = o_refs[-1][...].astype(F32)
    return (last + sum(expand(w / den) * (o_ref[...].astype(F32) - last)
                       for w, o_ref in zip(ws[:-1], o_refs[:-1]))).astype(BF16)


def _post_kernel(*refs, n_groups, seq_tiles):
    n_a = 2 * n_groups + 1 if n_groups else 8
    a_refs = refs[:n_a]
    (w_ref, h_ref, g_ref, wr_ref, br_ref, tri_ref,
     h1_ref, hn0_ref, hn1_ref, route_ref, route_t_ref, cnt_ref, carry_ref) = refs[n_a:n_a + 13]
    hn_refs = (hn0_ref, hn1_ref)
    if n_groups:
        a = _merge_groups(a_refs[:n_groups], a_refs[n_groups:2 * n_groups], a_refs[-1])
    else:
        a = _retention_tile(h_ref, *a_refs, refs[-1], pl.program_id(0) % seq_tiles == 0)
    @pl.when(pl.program_id(0) == 0)
    def _():
        carry_ref[...] = jnp.zeros_like(carry_ref)

    half = tri_ref.shape[0]
    n_pieces = h_ref.shape[0] // half
    D = h_ref.shape[1]
    lane = lax.broadcasted_iota(jnp.int32, (half, LANES), 1).astype(F32)
    ninf = -jnp.inf

    def first_argmax(vals):
        top = jnp.max(vals, axis=1, keepdims=True)
        where = jnp.min(jnp.where(vals == top, lane, float(LANES)), axis=1, keepdims=True)
        return top, where

    h1_ref[...] = h_ref[...] + jnp.dot(a, w_ref[...], preferred_element_type=F32)

    def route_half(rows):
        hn = _rms(h1_ref[rows, :]) * g_ref[...]
        for part, ref in enumerate(hn_refs):
            ref[rows, :] = _pack_part(hn, part)
        hi = hn.astype(BF16)
        lo = (hn - hi.astype(F32)).astype(BF16)
        both = jnp.dot(hi, wr_ref[...], preferred_element_type=F32)
        logits = (both[:, :LANES] + both[:, LANES:]
                  + jnp.dot(lo, wr_ref[:, :LANES], preferred_element_type=F32) + br_ref[...])

        is_grp = lane < MOE_GROUPS
        lg = jnp.where(is_grp, logits, ninf)
        mg, grp = first_argmax(lg)
        p_grp = 1.0 / jnp.sum(jnp.where(is_grp, jnp.exp(lg - mg), 0.0), axis=1, keepdims=True)

        e_lane = lane - ROUTER_EXP_LANE0
        in_grp = (e_lane < N_EXPERTS) & (jnp.floor(e_lane * (1.0 / EXPERTS_PER_GROUP)) == grp)
        le = jnp.where(in_grp, logits, ninf)
        v1, i1 = first_argmax(le)
        le2 = jnp.where(lane == i1, ninf, le)
        v2, i2 = first_argmax(le2)
        e = jnp.exp(v2 - v1)
        hit1 = lane == (i1 - ROUTER_EXP_LANE0)
        hit2 = lane == (i2 - ROUTER_EXP_LANE0)
        onehot = jnp.where(hit1 | hit2, 1.0, 0.0)
        earlier = jnp.dot(tri_ref[...], onehot.astype(BF16), preferred_element_type=F32)
        return dict(e1=i1 - ROUTER_EXP_LANE0, e2=i2 - ROUTER_EXP_LANE0, g1=p_grp / (1.0 + e),
                    g2=p_grp * e / (1.0 + e), hit1=hit1, hit2=hit2, earlier=earlier,
                    count=jnp.sum(onehot, axis=0, keepdims=True))

    halves = [route_half(slice(j * half, (j + 1) * half)) for j in range(n_pieces)]
    carry = carry_ref[...]
    for j, r in enumerate(halves):
        before = carry + r["earlier"]
        r1 = jnp.sum(jnp.where(r["hit1"], before, 0.0), axis=1, keepdims=True)
        r2 = jnp.sum(jnp.where(r["hit2"], before, 0.0), axis=1, keepdims=True)
        carry = carry + r["count"]
        route = jnp.zeros((half, LANES), F32)
        for slot, val in ((ROUTE_E1, r["e1"]), (ROUTE_E2, r["e2"]), (ROUTE_G1, r["g1"]),
                          (ROUTE_G2, r["g2"]), (ROUTE_R1, r1), (ROUTE_R2, r2)):
            route = jnp.where(lane == slot, val, route)
        route_ref[j * half:(j + 1) * half, :] = route
        route_t_ref[:, j * half:(j + 1) * half] = route.T[:ROUTE_ROWS]
    carry_ref[...] = carry
    cnt_ref[...] = carry


def _post(mixer, w, h, gain, w_grp, b_grp, w_exp, b_exp):
    T, D = h.shape
    K = w.shape[0]
    row_block = lambda width: pl.BlockSpec((tm, width), lambda i: (i, 0))
    scratch = [pltpu.VMEM((1, LANES), F32)]
    if len(mixer) == 2:
        outs, lses = mixer
        n_groups, seq_tiles = len(outs), 1
        tm = min(TOKEN_TILE, T)
        ex = jnp.where(jnp.arange(LANES)[:, None] == (jnp.arange(K)[None, :] // DIL_HEAD_DIM), 1.0, 0.0).astype(BF16)
        a_args = (*outs, *lses, ex)
        a_specs = [row_block(K)] * n_groups + [row_block(LANES)] * n_groups + [_resident((LANES, K), lambda i: (0, 0))]
    else:
        mix_gain, w_in, S = mixer
        tm = min(TOKEN_TILE, S)
        n_groups, seq_tiles = 0, S // tm
        a_args, a_specs, state = _retention_operands(mix_gain, w_in, S, tm)
        scratch.append(state)
    n_r = MOE_GROUPS + N_EXPERTS
    wr = jnp.zeros((D, LANES), F32).at[:, :n_r].set(jnp.concatenate([w_grp, w_exp], axis=1))
    wr_hi = wr.astype(BF16)
    wr_lo = (wr - wr_hi.astype(F32)).astype(BF16)
    wr2 = jnp.concatenate([wr_hi, wr_lo], axis=1)
    br = jnp.zeros((1, LANES), F32).at[0, :n_r].set(jnp.concatenate([b_grp, b_exp]))
    piece = tm // ROUTE_PIECES
    tri = jnp.tril(jnp.ones((piece, piece), BF16), k=-1)
    return pl.pallas_call(
        functools.partial(_post_kernel, n_groups=n_groups, seq_tiles=seq_tiles),
        grid=(T // tm,),
        in_specs=a_specs + [
                  _resident((K, D), lambda i: (0, 0)),
                  pl.BlockSpec((tm, D), lambda i: (i, 0)),
                  _resident((1, D), lambda i: (0, 0)),
                  _resident((D, 2 * LANES), lambda i: (0, 0)),
                  _resident((1, LANES), lambda i: (0, 0)),
                  _resident((piece, piece), lambda i: (0, 0))],
        out_specs=[pl.BlockSpec((tm, D), lambda i: (i, 0))]
                  + [pl.BlockSpec((tm, D // 2 // ROW_PARTS), lambda i: (i, 0))] * ROW_PARTS
                  + [pl.BlockSpec((tm, LANES), lambda i: (i, 0)),
                   pl.BlockSpec((ROUTE_ROWS, tm), lambda i: (0, i)),
                   pl.BlockSpec((1, LANES), lambda i: (0, 0))],
        out_shape=[jax.ShapeDtypeStruct((T, D), F32)]
                  + [jax.ShapeDtypeStruct((T, D // 2 // ROW_PARTS), jnp.uint32)] * ROW_PARTS
                  + [jax.ShapeDtypeStruct((T, LANES), F32),
                   jax.ShapeDtypeStruct((ROUTE_ROWS, T), F32),
                   jax.ShapeDtypeStruct((1, LANES), F32)],
        scratch_shapes=scratch,
        compiler_params=_params("arbitrary"),
        name="post_mixer",
    )(*a_args, w, h, gain.reshape(1, D), wr2, br, tri)


SC_WINDOW = 128


def _sc_mesh():
    return plsc.VectorSubcoreMesh(core_axis_name="core", subcore_axis_name="subcore")


def _sc_window_specs(W):
    rows = pl.BlockSpec((SC_WINDOW, W), lambda i: (i, 0))
    idx = pl.BlockSpec((1, SC_WINDOW), lambda i: (0, i))
    return rows, idx


def _sc_pipeline(body, n_rows, in_specs, out_specs):
    return pltpu.emit_pipeline(body, grid=(n_rows // SC_WINDOW,), in_specs=in_specs, out_specs=out_specs,
                               core_axis_name=("core", "subcore"), dimension_semantics=(pltpu.PARALLEL,))


def _dispatch(xs, dest, pad_idx, P):
    T, W = xs[0].shape
    n_pad = pad_idx.shape[1]
    n = len(xs)
    rows, idx = _sc_window_specs(W)
    zero_rows = pl.BlockSpec((SC_WINDOW, W), lambda i: (0, 0))
    out = jax.ShapeDtypeStruct((P, W), xs[0].dtype)

    @functools.partial(pl.kernel, out_type=(out,) * n, mesh=_sc_mesh(), scratch_types=[], name="dispatch")
    def scatter(*refs):
        x_hbm, (d0_hbm, d1_hbm, z_hbm, p_hbm), o_hbm = refs[:n], refs[n:n + 4], refs[n + 4:]
        for x, o in zip(x_hbm, o_hbm):
            def put_pair(x_vmem, i0_vmem, i1_vmem, o=o):
                pltpu.sync_copy(x_vmem, o.at[i0_vmem.at[0]])
                pltpu.sync_copy(x_vmem, o.at[i1_vmem.at[0]])

            def put(x_vmem, i_vmem, o=o):
                pltpu.sync_copy(x_vmem, o.at[i_vmem.at[0]])

            _sc_pipeline(put_pair, T, [rows, idx, idx], [])(x, d0_hbm, d1_hbm)
            _sc_pipeline(put, n_pad, [zero_rows, idx], [])(z_hbm, p_hbm)

    return scatter(*xs, dest[0:1], dest[1:2], jnp.zeros((SC_WINDOW, W), xs[0].dtype), pad_idx)


WEIGHT_LEADS = (3, 2, 1)
EXPERT_SLOTS = max(WEIGHT_LEADS) + 1


def _expert_kernel(blk_exp_ref, slot_ref, n_used_ref, *refs):
    x_refs = refs[:ROW_PARTS]
    w_refs = refs[ROW_PARTS:ROW_PARTS + 3]
    o_refs = refs[ROW_PARTS + 3:2 * ROW_PARTS + 3]
    w_slots = refs[2 * ROW_PARTS + 3:]
    lead = max(WEIGHT_LEADS)
    n_used = n_used_ref[0]
    i = pl.program_id(0) - lead
    D = w_slots[0].shape[1]

    for w_ref, w_s, ahead in zip(w_refs, w_slots, WEIGHT_LEADS):
        j = i + ahead
        jc = jnp.clip(j, 0, n_used - 1)
        arrived = (j >= 0) & (j < n_used) & ((j == 0) | (blk_exp_ref[jc] != blk_exp_ref[jnp.maximum(jc - 1, 0)]))

        @pl.when(arrived)
        def _(w_ref=w_ref, w_s=w_s, jc=jc):
            w_s[slot_ref[jc]] = w_ref[0, 0].astype(BF16)

    @pl.when((i >= 0) & (i < n_used))
    def _():
        slot = slot_ref[jnp.clip(i, 0, n_used - 1)]
        wg_s, wu_s, wd_s = w_slots
        pieces = []
        for part, x_ref in enumerate(x_refs):
            for cols, val in zip(_part_cols(D, part), _unpack_bf16_pair(x_ref[...])):
                pieces.append((cols, val.astype(BF16)))

        def up(w_s):
            return sum(jnp.dot(val, w_s[slot, cols, :], preferred_element_type=F32) for cols, val in pieces)

        g = up(wg_s)
        hid = (g * jax.nn.sigmoid(g) * up(wu_s)).astype(BF16)
        y = jnp.dot(hid, wd_s[slot], preferred_element_type=F32)
        for part, o_ref in enumerate(o_refs):
            o_ref[...] = _pack_part(y, part)

    @pl.when(i >= n_used)
    def _():
        for o_ref in o_refs:
            o_ref[...] = jnp.zeros_like(o_ref)


def _experts(xs, blk_exp, n_used, layer, w_gate, w_up, w_down):
    P, W = xs[0].shape
    D = 2 * W * ROW_PARTS
    FF = w_gate.shape[3]
    nblk = P // MOE_BLOCK
    lead = max(WEIGHT_LEADS)
    changes = jnp.concatenate([jnp.zeros((1,), jnp.int32), (blk_exp[1:] != blk_exp[:-1]).astype(jnp.int32)])
    slot = jnp.cumsum(changes) % EXPERT_SLOTS

    def x_map(g, be, sl, nu):
        return (jnp.clip(g - lead, 0, nu[0] - 1), 0)

    def o_map(g, be, sl, nu):
        return (jnp.maximum(g - lead, 0), 0)

    def w_map(ahead):
        return lambda g, be, sl, nu: (layer, be[jnp.clip(g - lead + ahead, 0, nu[0] - 1)], 0, 0)

    wg_spec, wu_spec, wd_spec = (pl.BlockSpec((1, 1) + shape, w_map(ahead))
                                 for shape, ahead in zip(((D, FF), (D, FF), (FF, D)), WEIGHT_LEADS))
    return pl.pallas_call(
        _expert_kernel,
        grid_spec=pltpu.PrefetchScalarGridSpec(
            num_scalar_prefetch=3,
            grid=(nblk + lead,),
            in_specs=[pl.BlockSpec((MOE_BLOCK, W), x_map)] * ROW_PARTS + [wg_spec, wu_spec, wd_spec],
            out_specs=[pl.BlockSpec((MOE_BLOCK, W), o_map)] * ROW_PARTS,
            scratch_shapes=[pltpu.VMEM((EXPERT_SLOTS, D, FF), BF16), pltpu.VMEM((EXPERT_SLOTS, D, FF), BF16),
                            pltpu.VMEM((EXPERT_SLOTS, FF, D), BF16)]),
        out_shape=[jax.ShapeDtypeStruct((P, W), jnp.uint32)] * ROW_PARTS,
        compiler_params=_params("arbitrary"),
        name="experts",
    )(blk_exp, slot, n_used, *xs, w_gate, w_up, w_down)


def _gather_pairs(ys, dest):
    W = ys[0].shape[1]
    T = dest.shape[1]
    n = len(ys)
    rows, idx = _sc_window_specs(W)
    out = jax.ShapeDtypeStruct((T, W), ys[0].dtype)

    @functools.partial(pl.kernel, out_type=(out,) * (2 * n), mesh=_sc_mesh(), scratch_types=[],
                       name="gather_pairs")
    def gather(*refs):
        y_hbm, d_hbm, o_hbm = refs[:n], refs[n:n + 2], refs[n + 2:]
        for slot, d in enumerate(d_hbm):
            for y, o in zip(y_hbm, o_hbm[slot * n:(slot + 1) * n]):
                def get(i_vmem, o_vmem, y=y):
                    pltpu.sync_copy(y.at[i_vmem.at[0]], o_vmem)

                _sc_pipeline(get, T, [idx], [rows])(d, o)

    return gather(*ys, dest[0:1], dest[1:2])


def _combined_rows(h_ref, route_ref, y_refs):
    D = h_ref.shape[1]
    route = route_ref[...]
    gates = (route[:, ROUTE_G1:ROUTE_G1 + 1], route[:, ROUTE_G2:ROUTE_G2 + 1])
    pieces = {}
    for part in range(ROW_PARTS):
        slots = [_unpack_bf16_pair(y_refs[slot * ROW_PARTS + part][...]) for slot in range(2)]
        for half, cols in enumerate(_part_cols(D, part)):
            pieces[cols.start] = h_ref[:, cols] + (gates[0] * slots[0][half] + gates[1] * slots[1][half])
    return jnp.concatenate([pieces[c] for c in sorted(pieces)], axis=1)


def _combine_kernel(h_ref, route_ref, *refs):
    refs[-1][...] = _combined_rows(h_ref, route_ref, refs[:-1])


def _combine(h, route, pairs):
    T, D = h.shape
    W = pairs[0].shape[1]
    tm = min(ROW_TILE, T)
    rows = lambda width: pl.BlockSpec((tm, width), lambda i: (i, 0), pipeline_mode=pl.Buffered(3))
    inner = pltpu.emit_pipeline(
        _combine_kernel, grid=(T // tm,),
        in_specs=[rows(D), rows(LANES)] + [rows(W)] * len(pairs),
        out_specs=[pl.BlockSpec((tm, D), lambda i: (i, 0))])
    anywhere = pl.BlockSpec(memory_space=pl.ANY)
    return pl.pallas_call(
        lambda *refs: inner(*refs),
        in_specs=[anywhere] * (2 + len(pairs)),
        out_specs=anywhere,
        out_shape=jax.ShapeDtypeStruct((T, D), F32),
        compiler_params=pltpu.CompilerParams(vmem_limit_bytes=VMEM_LIMIT),
        name="combine",
    )(h, route, *pairs)


def _moe(h1, hn, route, route_t, counts, layer, w_gate, w_up, w_down, combine):
    T, D = h1.shape
    A = 2 * T
    nblk = -(-A // MOE_BLOCK) + N_EXPERTS
    P = nblk * MOE_BLOCK
    eid = route_t[ROUTE_E1:ROUTE_E2 + 1].astype(jnp.int32)
    rank = route_t[ROUTE_R1:ROUTE_R2 + 1].astype(jnp.int32)
    cnt = counts[0, :N_EXPERTS].astype(jnp.int32)
    padded = (cnt + MOE_BLOCK - 1) // MOE_BLOCK * MOE_BLOCK
    pends = jnp.cumsum(padded)
    pstarts = pends - padded
    experts = jnp.arange(N_EXPERTS, dtype=jnp.int32)
    start_of = jnp.sum(jnp.where(eid[:, None, :] == experts[None, :, None], pstarts[None, :, None], 0), axis=1)
    dest = start_of + rank
    blk_start = jnp.arange(nblk, dtype=jnp.int32) * MOE_BLOCK
    blk_exp = jnp.minimum(jnp.sum((pends[None, :] <= blk_start[:, None]).astype(jnp.int32), axis=1),
                          N_EXPERTS - 1)
    n_used = pends[-1:] // MOE_BLOCK

    gap_start = jnp.concatenate([pstarts + cnt, pends[-1:]])
    gap_len = jnp.concatenate([padded - cnt, P - pends[-1:]])
    gap_end = jnp.cumsum(gap_len)
    j = jnp.arange(P - A, dtype=jnp.int32)
    gap_of = jnp.sum((gap_end[None, :] <= j[:, None]).astype(jnp.int32), axis=1)
    sel = gap_of[:, None] == jnp.arange(N_EXPERTS + 1, dtype=jnp.int32)[None, :]
    pad_idx = jnp.sum(jnp.where(sel, (gap_start - (gap_end - gap_len))[None, :] + j[:, None], 0), axis=1)

    xs = _dispatch(hn, dest, pad_idx.reshape(1, P - A), P)
    ys = _experts(xs, blk_exp, n_used, layer, w_gate, w_up, w_down)
    pairs = _gather_pairs(ys, dest)
    return _combine(h1, route, pairs) if combine else (h1, route, pairs)


def _qkv_kernel(*refs, rate, n, n_chunks, n_pending, permute):
    n_x = 2 + n_pending if n_pending else 1
    x_refs = refs[:n_x]
    gq_ref, gkv_ref, wq_ref, wk_ref, wvt_ref, qn_ref, kn_ref, seg_ref = refs[n_x:n_x + 8]
    perm_ref = refs[n_x + 8] if permute else None
    q_ref, k_ref, vt_ref = refs[n_x + 8 + permute:n_x + 11 + permute]
    xs_ref = refs[-1]
    seg = seg_ref[...]
    width = seg.shape[0]
    res_per_chunk = q_ref.shape[1]
    chunk = pl.program_id(2)

    def head_norm(t, gain):
        cols = []
        for j in range(t.shape[1] // width):
            tj = t[:, j * width:(j + 1) * width]
            ms = jnp.dot((tj * tj).astype(BF16), seg, preferred_element_type=F32)
            cols.append(tj * lax.rsqrt(ms + EPS))
        return jnp.concatenate(cols, axis=1) * gain

    def normed(x):
        y = _rms(x)
        return (y * gq_ref[...]).astype(BF16), (y * gkv_ref[...]).astype(BF16)

    def project(xq, xkv):
        q = head_norm(jnp.dot(xq, wq_ref[...], preferred_element_type=F32), qn_ref[...]).astype(q_ref.dtype)
        k = head_norm(jnp.dot(xkv, wk_ref[...], preferred_element_type=F32), kn_ref[...]).astype(k_ref.dtype)
        vt = lax.dot_general(wvt_ref[...], xkv, NT_DIMS, preferred_element_type=F32).astype(vt_ref.dtype)
        for j in range(res_per_chunk):
            q_ref[0, j] = q[j * n:(j + 1) * n]
            k_ref[0, j] = k[j * n:(j + 1) * n]
            vt_ref[0, j] = vt[:, j * n:(j + 1) * n]

    if n_pending:
        x = _combined_rows(x_refs[0], x_refs[1], x_refs[2:])
        refs[-2][...] = x
        project(*normed(x))
        return
    x_ref, = x_refs
    if rate == 1:
        project(*normed(x_ref[...]))
        return

    if permute:
        group = perm_ref.shape[0]
        n_groups, run = x_ref.shape[0] // group, group // rate

        @pl.when(chunk == 0)
        def _():
            for g in range(n_groups):
                rows = slice(g * group, (g + 1) * group)
                for t, xt in enumerate(normed(x_ref[rows, :])):
                    xs_ref[t, rows, :] = jnp.dot(perm_ref[...], xt, preferred_element_type=F32).astype(BF16)

        span = res_per_chunk * run
        start = pl.multiple_of(chunk * span, span)

        def residue_major(t):
            parts = [xs_ref[t, pl.ds(g * group + start, span), :] for g in range(n_groups)]
            return jnp.concatenate([p[j * run:(j + 1) * run] for j in range(res_per_chunk) for p in parts], axis=0)

        project(residue_major(0), residue_major(1))
        return

    @pl.when(chunk == 0)
    def _():
        for j in range(xs_ref.shape[0]):
            xs_ref[j] = x_ref[:, j * LANES:(j + 1) * LANES]

    for ch in range(n_chunks):
        @pl.when(chunk == ch)
        def _(ch=ch):
            residues = range(ch * res_per_chunk, (ch + 1) * res_per_chunk)
            project(*normed(jnp.concatenate(
                [jnp.concatenate([xs_ref[j, pl.ds(c, n, stride=rate), :] for j in range(xs_ref.shape[0])], axis=1)
                 for c in residues], axis=0)))


def _qkv(h, B, S, rate, gq, gkv, wq, wk, wv, qn, kn):
    pending = h if isinstance(h, tuple) else None
    D = (pending[0] if pending else h).shape[1]
    L = S // rate
    hd = DIL_HEAD_DIM
    n = max(SUB_BLOCK, TOKEN_TILE // rate)
    tm = n * rate
    res_per_chunk = max(1, TOKEN_TILE // n)
    n_chunks = rate // res_per_chunk
    width = 2 * LANES
    ii = jnp.arange(width)
    seg = jnp.where((ii[:, None] // hd) == (ii[None, :] // hd), 1.0 / hd, 0.0).astype(BF16)
    permute = rate * BF16_ROWS == PERM_ROWS
    if permute:
        pp = jnp.arange(PERM_ROWS)
        perm = ((pp[:, None] // BF16_ROWS == pp[None, :] % rate)
                & (pp[:, None] % BF16_ROWS == pp[None, :] // rate)).astype(BF16)
    row = lambda g: jnp.tile(g, D // hd).reshape(1, D)
    const = lambda shape: _resident(shape, lambda b, i, c: (0,) * len(shape))
    qk_spec = pl.BlockSpec((1, res_per_chunk, n, D), lambda b, i, c: (b, c, i, 0))
    qk_shape = jax.ShapeDtypeStruct((B, rate, L, D), BF16)
    rows = lambda width: pl.BlockSpec((tm, width), lambda b, i, c: (b * (S // tm) + i, 0))
    if pending:
        assert rate == 1
        h1, route, pairs = pending
        x_args = (h1, route, *pairs)
        x_specs = [rows(D), rows(LANES)] + [rows(pairs[0].shape[1])] * len(pairs)
        extra_specs, extra_shapes = [rows(D)], [jax.ShapeDtypeStruct((B * S, D), F32)]
    else:
        x_args, x_specs, extra_specs, extra_shapes = (h,), [rows(D)], [], []
    return pl.pallas_call(
        functools.partial(_qkv_kernel, rate=rate, n=n, n_chunks=n_chunks,
                          n_pending=len(pending[2]) if pending else 0, permute=permute),
        grid=(B, S // tm, n_chunks),
        in_specs=x_specs + [
                  const((1, D)), const((1, D)), const((D, D)), const((D, D)), const((D, D)),
                  const((1, D)), const((1, D)), const((width, width))]
                 + ([const((PERM_ROWS, PERM_ROWS))] if permute else []),
        out_specs=[qk_spec, qk_spec,
                   pl.BlockSpec((1, res_per_chunk, D, n), lambda b, i, c: (b, c, 0, i))] + extra_specs,
        out_shape=[qk_shape, qk_shape, jax.ShapeDtypeStruct((B, rate, D, L), BF16)] + extra_shapes,
        scratch_shapes=[pltpu.VMEM((2, tm, D), BF16) if permute else
                        pltpu.VMEM((D // LANES, tm if rate > 1 else 8, LANES), F32)],
        compiler_params=_params("parallel", "parallel", "arbitrary"),
        name="qkv_rate%d" % rate,
    )(*x_args, gq.reshape(1, D), gkv.reshape(1, D), wq, wk, wv.T, row(qn) * (hd ** -0.5 * LOG2E), row(kn), seg,
      *((perm,) if permute else ()))


def _attn_kernel(*refs, rate, with_prev):
    if not with_prev:
        q_ref, kc_ref, vc_ref, edge_ref, o_ref, lse_ref, os_ref = refs
    elif len(refs) == 10:
        q_ref, kc_ref, kp_ref, vc_ref, vp_ref, edge_ref, bias_ref, o_ref, lse_ref, os_ref = refs
    else:
        q_ref, kc_ref, kp_ref, vc_ref, vp_ref, edge_ref, o_ref, lse_ref, os_ref = refs
    Bk = SUB_BLOCK
    hd = DIL_HEAD_DIM
    n_pairs = DIL_HEADS // 2
    lane = lax.broadcasted_iota(jnp.int32, (Bk, LANES), 1)

    def block(load_q, load_k, load_vt, load_bias, store):
        out_t, lse_t = [], []
        for p in range(n_pairs):
            qp = load_q(p)
            zero = jnp.zeros_like(qp)
            q2 = jnp.concatenate([jnp.where(lane < hd, qp, zero), jnp.where(lane >= hd, qp, zero)], axis=0)
            s = lax.dot_general(load_k(p), q2, NT_DIMS, preferred_element_type=F32) + load_bias(p)
            m = jnp.max(s, axis=0, keepdims=True)
            pr = jnp.exp2(s - m)
            l = jnp.sum(pr, axis=0, keepdims=True)
            pb = pr.astype(BF16)
            vt = load_vt(p)
            out_t.append(jnp.dot(vt[:hd], pb[:, :Bk], preferred_element_type=F32) / l[:, :Bk])
            out_t.append(jnp.dot(vt[hd:], pb[:, Bk:], preferred_element_type=F32) / l[:, Bk:])
            lse = (m + jnp.log2(l)) * LN2
            lse_t += [lse[:, :Bk], lse[:, Bk:]]
        lse_t.append(jnp.zeros((LANES - DIL_HEADS, Bk), F32))
        store(jnp.concatenate(out_t, axis=0).T, jnp.concatenate(lse_t, axis=0).T)

    def pair_cols(p):
        return slice(p * LANES, (p + 1) * LANES)

    n_res, n_blocks = kc_ref.shape[1], kc_ref.shape[2] // Bk
    for c in range(n_res):
        for j in range(n_blocks):
            rows = slice(j * Bk, (j + 1) * Bk)
            if not with_prev:
                load_k = lambda p, c=c: kc_ref[0, c, :, pair_cols(p)]
                load_vt = lambda p, c=c: vc_ref[0, c, pair_cols(p), :]
                load_bias = lambda p: edge_ref[0, p]
            elif j == 0:
                load_k = lambda p, c=c: jnp.concatenate([kp_ref[0, c, :, pair_cols(p)],
                                                         kc_ref[0, c, :Bk, pair_cols(p)]], axis=0)
                load_vt = lambda p, c=c: jnp.concatenate([vp_ref[0, c, pair_cols(p), :],
                                                          vc_ref[0, c, pair_cols(p), :Bk]], axis=1)
                load_bias = lambda p: edge_ref[0, p]
            else:
                krows = slice((j - 1) * Bk, (j + 1) * Bk)
                load_k = lambda p, c=c, krows=krows: kc_ref[0, c, krows, pair_cols(p)]
                load_vt = lambda p, c=c, krows=krows: vc_ref[0, c, pair_cols(p), krows]
                load_bias = lambda p: bias_ref[p]

            if rate == 1:
                def store(o, lse, rows=rows):
                    o_ref[rows, :] = o.astype(o_ref.dtype)
                    lse_ref[rows, :] = lse
            else:
                def store(o, lse, out_rows=pl.ds(c + j * Bk * rate, Bk, stride=rate)):
                    for g in range(os_ref.shape[0]):
                        os_ref[g, out_rows, :] = o[:, pair_cols(g)]
                    lse_ref[out_rows, :] = lse

            block(lambda p, c=c, rows=rows: q_ref[0, c, rows, pair_cols(p)], load_k, load_vt, load_bias, store)
    if rate > 1:
        for g in range(os_ref.shape[0]):
            o_ref[:, pair_cols(g)] = os_ref[g].astype(o_ref.dtype)


def _t5_bucket(n):
    max_exact = NUM_BUCKETS // 2
    nf = jnp.maximum(n, max_exact).astype(F32)
    large = max_exact + (jnp.log(nf / max_exact) / math.log(MAX_DISTANCE / max_exact)
                         * (NUM_BUCKETS - max_exact)).astype(jnp.int32)
    large = jnp.minimum(large, NUM_BUCKETS - 1)
    return jnp.where(n < max_exact, n, large)


def _group_attention(q, k, vt, bias_table, rate, n_steps):
    B, _, L, D = q.shape
    S = L * rate
    Bk = SUB_BLOCK
    with_prev = L > Bk
    nk = 2 * Bk if with_prev else Bk
    n = max(Bk, ATTN_TILE // rate)
    tm = n * rate
    nt = S // tm
    n_pairs = DIL_HEADS // 2
    ql = jnp.arange(Bk, dtype=jnp.int32)[:, None]
    kl = jnp.arange(2 * Bk, dtype=jnp.int32)[None, :]
    steps = ql + Bk - kl
    bucket = _t5_bucket(jnp.maximum(steps, 0) * rate)
    buckets = jnp.arange(NUM_BUCKETS, dtype=jnp.int32)
    bias = jnp.sum(jnp.where(bucket[None, :, :, None] == buckets[:, None, None, None],
                             bias_table.astype(F32)[:, None, None, :], 0.0), axis=0)
    band = ((steps >= 0) & (steps <= n_steps))[:, :, None]
    first = (kl < Bk)[:, :, None]

    def layout(t):
        t = t[:, 2 * Bk - nk:, :].transpose(1, 2, 0)
        return t.reshape(nk, n_pairs, 2 * Bk).transpose(1, 0, 2)

    bias_in = layout(jnp.where(band, bias * LOG2E, NEG_INF))
    bias_first = layout(jnp.where(band & ~first, bias * LOG2E, NEG_INF))
    edge = jnp.stack([bias_first, bias_in])

    cur_qk = pl.BlockSpec((1, rate, n, D), lambda b, i: (b, 0, i, 0))
    cur_vt = pl.BlockSpec((1, rate, D, n), lambda b, i: (b, 0, 0, i))
    per_n = n // Bk
    prev_qk = pl.BlockSpec((1, rate, Bk, D), lambda b, i: (b, 0, jnp.maximum(i * per_n - 1, 0), 0))
    prev_vt = pl.BlockSpec((1, rate, D, Bk), lambda b, i: (b, 0, 0, jnp.maximum(i * per_n - 1, 0)))
    edge_spec = pl.BlockSpec((1, n_pairs, nk, 2 * Bk), lambda b, i: (jnp.minimum(i, 1), 0, 0, 0))
    if with_prev:
        in_specs = [cur_qk, cur_qk, prev_qk, cur_vt, prev_vt, edge_spec]
        args = (q, k, k, vt, vt, edge)
    else:
        in_specs = [cur_qk, cur_qk, cur_vt, edge_spec]
        args = (q, k, vt, edge)
    if with_prev and n > Bk:
        in_specs.append(_resident((n_pairs, nk, 2 * Bk), lambda b, i: (0, 0, 0)))
        args += (bias_in,)
    return pl.pallas_call(
        functools.partial(_attn_kernel, rate=rate, with_prev=with_prev),
        grid=(B, nt),
        in_specs=in_specs,
        out_specs=[pl.BlockSpec((tm, D), lambda b, i: (b * nt + i, 0)),
                   pl.BlockSpec((tm, LANES), lambda b, i: (b * nt + i, 0))],
        out_shape=[jax.ShapeDtypeStruct((B * S, D), BF16),
                   jax.ShapeDtypeStruct((B * S, LANES), F32)],
        scratch_shapes=[pltpu.VMEM((D // LANES, tm if rate > 1 else 8, LANES), F32)],
        compiler_params=_params("parallel", "parallel"),
        name="attn_rate%d" % rate,
    )(*args)


def _cast_kernel(x_ref, o_ref):
    cols = o_ref.shape[2]
    for j in range(o_ref.shape[0]):
        o_ref[j] = x_ref[:, j * cols:(j + 1) * cols].astype(o_ref.dtype)


def _bf16_column_blocks(w, cols):
    rows, n = w.shape[0], w.shape[1] // cols
    block = min(ROW_TILE // 4, rows)
    assert w.shape[1] == n * cols and rows % block == 0
    return pl.pallas_call(
        _cast_kernel, grid=(rows // block,),
        in_specs=[pl.BlockSpec((block, n * cols), lambda i: (i, 0))],
        out_specs=pl.BlockSpec((n, block, cols), lambda i: (0, i, 0)),
        out_shape=jax.ShapeDtypeStruct((n, rows, cols), BF16),
        compiler_params=_params("parallel"), name="bf16_column_blocks",
    )(w)


def kernel(x, ret_w_in, ret_w_out, kv_norm, w_kv, k_norm, dil_wq, q_norm, dil_wo, rel_bias,
           mixer_norm, ffn_norm, router_grp, router_grp_b, router_exp, router_exp_b,
           exp_gate, exp_up, exp_down):
    B, S, D = x.shape
    h = x.reshape(B * S, D)

    def moe_layer(layer, mixer, w_out, h, combine):
        h1, *hn, route, route_t, counts = _post(
            mixer, w_out.astype(BF16), h, ffn_norm[layer], router_grp[layer], router_grp_b[layer],
            router_exp[layer], router_exp_b[layer])
        return _moe(h1, hn, route, route_t, counts, layer, exp_gate, exp_up, exp_down, combine)

    h = moe_layer(0, (mixer_norm[0], ret_w_in[0].astype(BF16), S), ret_w_out[0], h, combine=False)

    G = len(DIL_RATES)
    gd = DIL_HEADS * DIL_HEAD_DIM
    outs, lses = [], []
    w_kv_blocks = _bf16_column_blocks(w_kv, gd)
    for g in range(G):
        cq = slice(g * gd, (g + 1) * gd)
        q, k, vt, *combined = _qkv(h, B, S, DIL_RATES[g], mixer_norm[1], kv_norm,
                                   dil_wq[0][:, cq].astype(BF16), w_kv_blocks[g],
                                   w_kv_blocks[G + g], q_norm[0][g], k_norm[g])
        if combined:
            h, = combined
        o, lse = _group_attention(q, k, vt, rel_bias[:, g * DIL_HEADS:(g + 1) * DIL_HEADS],
                                  DIL_RATES[g], DIL_WINDOWS[g] // DIL_RATES[g])
        outs.append(o)
        lses.append(lse)
    h = moe_layer(1, (tuple(outs), tuple(lses)), dil_wo[0], h, combine=True)
    return h.reshape(B, S, D)
```

```python
import functools
import math

import jax
import jax.numpy as jnp
from jax import lax
from jax.experimental import pallas as pl
from jax.experimental.pallas import tpu as pltpu
from jax.experimental.pallas import tpu_sc as plsc

F32 = jnp.float32
BF16 = jnp.bfloat16

EPS = 1e-6
NEG_INF = -1e30

RET_HEADS = 4
RET_CHUNK = 256
ROPE_BASE = 10000.0

DIL_WINDOWS = (128, 512, 2048)
DIL_RATES = (1, 4, 16)
DIL_HEADS = 16
DIL_HEAD_DIM = 64
SUB_BLOCK = 128
NUM_BUCKETS = 32
MAX_DISTANCE = 2048

MOE_GROUPS = 4
EXPERTS_PER_GROUP = 8
N_EXPERTS = MOE_GROUPS * EXPERTS_PER_GROUP
MOE_BLOCK = 512

LANES = 128
BF16_ROWS = 16
PERM_ROWS = 256
ROW_TILE = 1024
TOKEN_TILE = 512
QKV_TILE = 1024
ATTN_TILE = 2048
VMEM_LIMIT = 56 * 1024 * 1024

NT_DIMS = (((1,), (1,)), ((), ()))
LOG2E = math.log2(math.e)
LN2 = math.log(2.0)


def _params(*sem):
    return pltpu.CompilerParams(dimension_semantics=sem, vmem_limit_bytes=VMEM_LIMIT)


def _resident(shape, index_map):
    return pl.BlockSpec(shape, index_map, pipeline_mode=pl.Buffered(1))


def _rms(x):
    return x * lax.rsqrt(jnp.mean(x * x, axis=-1, keepdims=True) + EPS)


def _retention_tile(x_ref, g_ref, w_ref, cos_ref, sin_ref, din_ref, xi_ref, zeta_ref, cd_ref, state_ref,
                    first_tile):
    C = RET_CHUNK
    H, dk, dv = state_ref.shape
    half = dk // 2
    ts = x_ref.shape[0]
    k_scale = dk ** -0.5

    @pl.when(first_tile)
    def _():
        state_ref[...] = jnp.zeros_like(state_ref)

    xn = (_rms(x_ref[...]) * g_ref[...]).astype(BF16)
    cos, sin = cos_ref[...], sin_ref[...]

    def proj(c0, width):
        return jnp.dot(xn, w_ref[:, c0:c0 + width], preferred_element_type=F32)

    def rot(t):
        t1, t2 = t[:, :half], t[:, half:]
        return jnp.concatenate([t1 * cos - t2 * sin, t1 * sin + t2 * cos], axis=1)

    heads = []
    for hh in range(H):
        q = rot(proj(hh * dk, dk)).astype(BF16)
        k = rot(proj(H * dk + hh * dk, dk)) * k_scale
        kb = k.astype(BF16)
        v = proj(2 * H * dk + hh * dv, dv).astype(BF16)
        gate = proj(2 * H * dk + H * dv + hh * dv, dv)
        gate = gate * jax.nn.sigmoid(gate)
        chunks = []
        for c in range(ts // C):
            rows = slice(c * C, (c + 1) * C)
            s = lax.dot_general(q[rows], kb[rows], NT_DIMS, preferred_element_type=F32) * din_ref[hh]
            inner = jnp.dot(s.astype(BF16), v[rows], preferred_element_type=F32)
            state = state_ref[hh]
            cross = jnp.dot(q[rows], state.astype(BF16), preferred_element_type=F32) * xi_ref[hh]
            kz_t = (k[rows] * zeta_ref[hh]).T.astype(BF16)
            state_ref[hh] = state * cd_ref[hh] + jnp.dot(kz_t, v[rows], preferred_element_type=F32)
            chunks.append((gate[rows] * _rms(inner + cross)).astype(BF16))
        heads.append(jnp.concatenate(chunks, axis=0))
    return jnp.concatenate(heads, axis=1)


def _retention_operands(gain, w, S, ts):
    H, C = RET_HEADS, RET_CHUNK
    D = w.shape[0]
    dk = D // H
    half = dk // 2
    nt = S // ts
    pos = jnp.arange(S, dtype=F32)
    inv = 1.0 / (ROPE_BASE ** jnp.linspace(0.0, 1.0, half, dtype=F32))
    ang = pos[:, None] * inv[None, :]
    log_g = jnp.log(1.0 - 2.0 ** (-5.0 - jnp.arange(H, dtype=F32)))
    idx = jnp.arange(C, dtype=F32)
    diff = idx[:, None] - idx[None, :]
    d_in = jnp.where(diff >= 0, jnp.exp(log_g[:, None, None] * jnp.maximum(diff, 0.0)), 0.0)
    xi = jnp.exp(log_g[:, None] * (idx + 1.0))[:, :, None]
    zeta = jnp.exp(log_g[:, None] * (C - 1.0 - idx))[:, :, None]
    chunk_decay = jnp.exp(log_g * C)[:, None, None]
    table = pl.BlockSpec((ts, half), lambda i: (i % nt, 0))
    per_head = lambda shape: _resident((H,) + shape, lambda i: (0, 0, 0))
    args = (gain.reshape(1, D), w, jnp.cos(ang), jnp.sin(ang), d_in, xi, zeta, chunk_decay)
    specs = [_resident((1, D), lambda i: (0, 0)), _resident(w.shape, lambda i: (0, 0)), table, table,
             per_head((C, C)), per_head((C, 1)), per_head((C, 1)), per_head((1, 1))]
    state = pltpu.VMEM((H, dk, 2 * dk), F32)
    return args, specs, state


ROUTE_E1, ROUTE_E2, ROUTE_G1, ROUTE_G2, ROUTE_R1, ROUTE_R2 = range(6)
ROUTE_ROWS = 8
ROUTE_PIECES = 2
ROUTER_EXP_LANE0 = MOE_GROUPS
HI16 = 0xFFFF0000


def _pack_bf16_pair(a, b):
    ua = lax.bitcast_convert_type(a.astype(BF16).astype(F32), jnp.uint32)
    ub = lax.bitcast_convert_type(b.astype(BF16).astype(F32), jnp.uint32)
    return ua | (ub >> 16)


def _unpack_bf16_pair(w):
    a = lax.bitcast_convert_type(w & jnp.uint32(HI16), F32)
    b = lax.bitcast_convert_type(w << 16, F32)
    return a, b


ROW_PARTS = 2


def _part_cols(D, part):
    w = D // 2 // ROW_PARTS
    return slice(part * w, (part + 1) * w), slice(D // 2 + part * w, D // 2 + (part + 1) * w)


def _pack_part(x, part):
    hi, lo = _part_cols(x.shape[1], part)
    return _pack_bf16_pair(x[:, hi], x[:, lo])


def _merge_groups(o_refs, l_refs, ex_ref):
    ex = ex_ref[...]

    def expand(w):
        return jnp.dot(w.astype(BF16), ex, preferred_element_type=F32)

    lses = [l_ref[...] for l_ref in l_refs]
    top = functools.reduce(jnp.maximum, lses)
    ws = [jnp.exp(l - top) for l in lses]
    den = sum(ws)
    last = o_refs[-1][...].astype(F32)
    return (last + sum(expand(w / den) * (o_ref[...].astype(F32) - last)
                       for w, o_ref in zip(ws[:-1], o_refs[:-1]))).astype(BF16)


def _post_kernel(*refs, n_groups, seq_tiles):
    n_a = 2 * n_groups + 1 if n_groups else 8
    a_refs = refs[:n_a]
    (w_ref, h_ref, g_ref, wr_ref, br_ref, tri_ref,
     h1_ref, hn0_ref, hn1_ref, route_ref, route_t_ref, cnt_ref, carry_ref) = refs[n_a:n_a + 13]
    hn_refs = (hn0_ref, hn1_ref)
    if n_groups:
        a = _merge_groups(a_refs[:n_groups], a_refs[n_groups:2 * n_groups], a_refs[-1])
    else:
        a = _retention_tile(h_ref, *a_refs, refs[-1], pl.program_id(0) % seq_tiles == 0)
    @pl.when(pl.program_id(0) == 0)
    def _():
        carry_ref[...] = jnp.zeros_like(carry_ref)

    half = tri_ref.shape[0]
    n_pieces = h_ref.shape[0] // half
    D = h_ref.shape[1]
    lane = lax.broadcasted_iota(jnp.int32, (half, LANES), 1).astype(F32)
    ninf = -jnp.inf

    def first_argmax(vals):
        top = jnp.max(vals, axis=1, keepdims=True)
        where = jnp.min(jnp.where(vals == top, lane, float(LANES)), axis=1, keepdims=True)
        return top, where

    h1_ref[...] = h_ref[...] + jnp.dot(a, w_ref[...], preferred_element_type=F32)

    def route_half(rows):
        hn = _rms(h1_ref[rows, :]) * g_ref[...]
        for part, ref in enumerate(hn_refs):
            ref[rows, :] = _pack_part(hn, part)
        hi = hn.astype(BF16)
        lo = (hn - hi.astype(F32)).astype(BF16)
        both = jnp.dot(hi, wr_ref[...], preferred_element_type=F32)
        logits = (both[:, :LANES] + both[:, LANES:]
                  + jnp.dot(lo, wr_ref[:, :LANES], preferred_element_type=F32) + br_ref[...])

        is_grp = lane < MOE_GROUPS
        lg = jnp.where(is_grp, logits, ninf)
        mg, grp = first_argmax(lg)
        p_grp = 1.0 / jnp.sum(jnp.where(is_grp, jnp.exp(lg - mg), 0.0), axis=1, keepdims=True)

        e_lane = lane - ROUTER_EXP_LANE0
        in_grp = (e_lane < N_EXPERTS) & (jnp.floor(e_lane * (1.0 / EXPERTS_PER_GROUP)) == grp)
        le = jnp.where(in_grp, logits, ninf)
        v1, i1 = first_argmax(le)
        le2 = jnp.where(lane == i1, ninf, le)
        v2, i2 = first_argmax(le2)
        e = jnp.exp(v2 - v1)
        hit1 = lane == (i1 - ROUTER_EXP_LANE0)
        hit2 = lane == (i2 - ROUTER_EXP_LANE0)
        onehot = jnp.where(hit1 | hit2, 1.0, 0.0)
        earlier = jnp.dot(tri_ref[...], onehot.astype(BF16), preferred_element_type=F32)
        return dict(e1=i1 - ROUTER_EXP_LANE0, e2=i2 - ROUTER_EXP_LANE0, g1=p_grp / (1.0 + e),
                    g2=p_grp * e / (1.0 + e), hit1=hit1, hit2=hit2, earlier=earlier,
                    count=jnp.sum(onehot, axis=0, keepdims=True))

    halves = [route_half(slice(j * half, (j + 1) * half)) for j in range(n_pieces)]
    carry = carry_ref[...]
    for j, r in enumerate(halves):
        before = carry + r["earlier"]
        r1 = jnp.sum(jnp.where(r["hit1"], before, 0.0), axis=1, keepdims=True)
        r2 = jnp.sum(jnp.where(r["hit2"], before, 0.0), axis=1, keepdims=True)
        carry = carry + r["count"]
        route = jnp.zeros((half, LANES), F32)
        for slot, val in ((ROUTE_E1, r["e1"]), (ROUTE_E2, r["e2"]), (ROUTE_G1, r["g1"]),
                          (ROUTE_G2, r["g2"]), (ROUTE_R1, r1), (ROUTE_R2, r2)):
            route = jnp.where(lane == slot, val, route)
        route_ref[j * half:(j + 1) * half, :] = route
        route_t_ref[:, j * half:(j + 1) * half] = route.T[:ROUTE_ROWS]
    carry_ref[...] = carry
    cnt_ref[...] = carry


def _post(mixer, w, h, gain, w_grp, b_grp, w_exp, b_exp):
    T, D = h.shape
    K = w.shape[0]
    row_block = lambda width: pl.BlockSpec((tm, width), lambda i: (i, 0))
    scratch = [pltpu.VMEM((1, LANES), F32)]
    if len(mixer) == 2:
        outs, lses = mixer
        n_groups, seq_tiles = len(outs), 1
        tm = min(TOKEN_TILE, T)
        ex = jnp.where(jnp.arange(LANES)[:, None] == (jnp.arange(K)[None, :] // DIL_HEAD_DIM), 1.0, 0.0).astype(BF16)
        a_args = (*outs, *lses, ex)
        a_specs = [row_block(K)] * n_groups + [row_block(LANES)] * n_groups + [_resident((LANES, K), lambda i: (0, 0))]
    else:
        mix_gain, w_in, S = mixer
        tm = min(TOKEN_TILE, S)
        n_groups, seq_tiles = 0, S // tm
        a_args, a_specs, state = _retention_operands(mix_gain, w_in, S, tm)
        scratch.append(state)
    n_r = MOE_GROUPS + N_EXPERTS
    wr = jnp.zeros((D, LANES), F32).at[:, :n_r].set(jnp.concatenate([w_grp, w_exp], axis=1))
    wr_hi = wr.astype(BF16)
    wr_lo = (wr - wr_hi.astype(F32)).astype(BF16)
    wr2 = jnp.concatenate([wr_hi, wr_lo], axis=1)
    br = jnp.zeros((1, LANES), F32).at[0, :n_r].set(jnp.concatenate([b_grp, b_exp]))
    piece = tm // ROUTE_PIECES
    tri = jnp.tril(jnp.ones((piece, piece), BF16), k=-1)
    return pl.pallas_call(
        functools.partial(_post_kernel, n_groups=n_groups, seq_tiles=seq_tiles),
        grid=(T // tm,),
        in_specs=a_specs + [
                  _resident((K, D), lambda i: (0, 0)),
                  pl.BlockSpec((tm, D), lambda i: (i, 0)),
                  _resident((1, D), lambda i: (0, 0)),
                  _resident((D, 2 * LANES), lambda i: (0, 0)),
                  _resident((1, LANES), lambda i: (0, 0)),
                  _resident((piece, piece), lambda i: (0, 0))],
        out_specs=[pl.BlockSpec((tm, D), lambda i: (i, 0))]
                  + [pl.BlockSpec((tm, D // 2 // ROW_PARTS), lambda i: (i, 0))] * ROW_PARTS
                  + [pl.BlockSpec((tm, LANES), lambda i: (i, 0)),
                   pl.BlockSpec((ROUTE_ROWS, tm), lambda i: (0, i)),
                   pl.BlockSpec((1, LANES), lambda i: (0, 0))],
        out_shape=[jax.ShapeDtypeStruct((T, D), F32)]
                  + [jax.ShapeDtypeStruct((T, D // 2 // ROW_PARTS), jnp.uint32)] * ROW_PARTS
                  + [jax.ShapeDtypeStruct((T, LANES), F32),
                   jax.ShapeDtypeStruct((ROUTE_ROWS, T), F32),
                   jax.ShapeDtypeStruct((1, LANES), F32)],
        scratch_shapes=scratch,
        compiler_params=_params("arbitrary"),
        name="post_mixer",
    )(*a_args, w, h, gain.reshape(1, D), wr2, br, tri)


SC_WINDOW = 128


def _sc_mesh():
    return plsc.VectorSubcoreMesh(core_axis_name="core", subcore_axis_name="subcore")


def _sc_window_specs(W):
    rows = pl.BlockSpec((SC_WINDOW, W), lambda i: (i, 0))
    idx = pl.BlockSpec((1, SC_WINDOW), lambda i: (0, i))
    return rows, idx


def _sc_pipeline(body, n_rows, in_specs, out_specs):
    return pltpu.emit_pipeline(body, grid=(n_rows // SC_WINDOW,), in_specs=in_specs, out_specs=out_specs,
                               core_axis_name=("core", "subcore"), dimension_semantics=(pltpu.PARALLEL,))


def _dispatch(xs, dest, pad_idx, P):
    T, W = xs[0].shape
    n_pad = pad_idx.shape[1]
    n = len(xs)
    rows, idx = _sc_window_specs(W)
    zero_rows = pl.BlockSpec((SC_WINDOW, W), lambda i: (0, 0))
    out = jax.ShapeDtypeStruct((P, W), xs[0].dtype)

    @functools.partial(pl.kernel, out_type=(out,) * n, mesh=_sc_mesh(), scratch_types=[], name="dispatch")
    def scatter(*refs):
        x_hbm, (d0_hbm, d1_hbm, z_hbm, p_hbm), o_hbm = refs[:n], refs[n:n + 4], refs[n + 4:]
        for x, o in zip(x_hbm, o_hbm):
            def put_pair(x_vmem, i0_vmem, i1_vmem, o=o):
                pltpu.sync_copy(x_vmem, o.at[i0_vmem.at[0]])
                pltpu.sync_copy(x_vmem, o.at[i1_vmem.at[0]])

            def put(x_vmem, i_vmem, o=o):
                pltpu.sync_copy(x_vmem, o.at[i_vmem.at[0]])

            _sc_pipeline(put_pair, T, [rows, idx, idx], [])(x, d0_hbm, d1_hbm)
            _sc_pipeline(put, n_pad, [zero_rows, idx], [])(z_hbm, p_hbm)

    return scatter(*xs, dest[0:1], dest[1:2], jnp.zeros((SC_WINDOW, W), xs[0].dtype), pad_idx)


WEIGHT_LEADS = (3, 2, 1)
EXPERT_SLOTS = max(WEIGHT_LEADS) + 1


def _expert_kernel(blk_exp_ref, slot_ref, n_used_ref, *refs):
    x_refs = refs[:ROW_PARTS]
    w_refs = refs[ROW_PARTS:ROW_PARTS + 3]
    o_refs = refs[ROW_PARTS + 3:2 * ROW_PARTS + 3]
    w_slots = refs[2 * ROW_PARTS + 3:]
    lead = max(WEIGHT_LEADS)
    n_used = n_used_ref[0]
    i = pl.program_id(0) - lead
    D = w_slots[0].shape[1]

    for w_ref, w_s, ahead in zip(w_refs, w_slots, WEIGHT_LEADS):
        j = i + ahead
        jc = jnp.clip(j, 0, n_used - 1)
        arrived = (j >= 0) & (j < n_used) & ((j == 0) | (blk_exp_ref[jc] != blk_exp_ref[jnp.maximum(jc - 1, 0)]))

        @pl.when(arrived)
        def _(w_ref=w_ref, w_s=w_s, jc=jc):
            w_s[slot_ref[jc]] = w_ref[0, 0].astype(BF16)

    @pl.when((i >= 0) & (i < n_used))
    def _():
        slot = slot_ref[jnp.clip(i, 0, n_used - 1)]
        wg_s, wu_s, wd_s = w_slots
        pieces = []
        for part, x_ref in enumerate(x_refs):
            for cols, val in zip(_part_cols(D, part), _unpack_bf16_pair(x_ref[...])):
                pieces.append((cols, val.astype(BF16)))

        def up(w_s):
            return sum(jnp.dot(val, w_s[slot, cols, :], preferred_element_type=F32) for cols, val in pieces)

        g = up(wg_s)
        hid = (g * jax.nn.sigmoid(g) * up(wu_s)).astype(BF16)
        y = jnp.dot(hid, wd_s[slot], preferred_element_type=F32)
        for part, o_ref in enumerate(o_refs):
            o_ref[...] = _pack_part(y, part)

    @pl.when(i >= n_used)
    def _():
        for o_ref in o_refs:
            o_ref[...] = jnp.zeros_like(o_ref)


def _experts(xs, blk_exp, n_used, layer, w_gate, w_up, w_down):
    P, W = xs[0].shape
    D = 2 * W * ROW_PARTS
    FF = w_gate.shape[3]
    nblk = P // MOE_BLOCK
    lead = max(WEIGHT_LEADS)
    changes = jnp.concatenate([jnp.zeros((1,), jnp.int32), (blk_exp[1:] != blk_exp[:-1]).astype(jnp.int32)])
    slot = jnp.cumsum(changes) % EXPERT_SLOTS

    def x_map(g, be, sl, nu):
        return (jnp.clip(g - lead, 0, nu[0] - 1), 0)

    def o_map(g, be, sl, nu):
        return (jnp.maximum(g - lead, 0), 0)

    def w_map(ahead):
        return lambda g, be, sl, nu: (layer, be[jnp.clip(g - lead + ahead, 0, nu[0] - 1)], 0, 0)

    wg_spec, wu_spec, wd_spec = (pl.BlockSpec((1, 1) + shape, w_map(ahead))
                                 for shape, ahead in zip(((D, FF), (D, FF), (FF, D)), WEIGHT_LEADS))
    return pl.pallas_call(
        _expert_kernel,
        grid_spec=pltpu.PrefetchScalarGridSpec(
            num_scalar_prefetch=3,
            grid=(nblk + lead,),
            in_specs=[pl.BlockSpec((MOE_BLOCK, W), x_map)] * ROW_PARTS + [wg_spec, wu_spec, wd_spec],
            out_specs=[pl.BlockSpec((MOE_BLOCK, W), o_map)] * ROW_PARTS,
            scratch_shapes=[pltpu.VMEM((EXPERT_SLOTS, D, FF), BF16), pltpu.VMEM((EXPERT_SLOTS, D, FF), BF16),
                            pltpu.VMEM((EXPERT_SLOTS, FF, D), BF16)]),
        out_shape=[jax.ShapeDtypeStruct((P, W), jnp.uint32)] * ROW_PARTS,
        compiler_params=_params("arbitrary"),
        name="experts",
    )(blk_exp, slot, n_used, *xs, w_gate, w_up, w_down)


def _gather_pairs(ys, dest):
    W = ys[0].shape[1]
    T = dest.shape[1]
    n = len(ys)
    rows, idx = _sc_window_specs(W)
    out = jax.ShapeDtypeStruct((T, W), ys[0].dtype)

    @functools.partial(pl.kernel, out_type=(out,) * (2 * n), mesh=_sc_mesh(), scratch_types=[],
                       name="gather_pairs")
    def gather(*refs):
        y_hbm, d_hbm, o_hbm = refs[:n], refs[n:n + 2], refs[n + 2:]
        for slot, d in enumerate(d_hbm):
            for y, o in zip(y_hbm, o_hbm[slot * n:(slot + 1) * n]):
                def get(i_vmem, o_vmem, y=y):
                    pltpu.sync_copy(y.at[i_vmem.at[0]], o_vmem)

                _sc_pipeline(get, T, [idx], [rows])(d, o)

    return gather(*ys, dest[0:1], dest[1:2])


def _combined_rows(h_ref, route_ref, y_refs):
    D = h_ref.shape[1]
    route = route_ref[...]
    gates = (route[:, ROUTE_G1:ROUTE_G1 + 1], route[:, ROUTE_G2:ROUTE_G2 + 1])
    pieces = {}
    for part in range(ROW_PARTS):
        slots = [_unpack_bf16_pair(y_refs[slot * ROW_PARTS + part][...]) for slot in range(2)]
        for half, cols in enumerate(_part_cols(D, part)):
            pieces[cols.start] = h_ref[:, cols] + (gates[0] * slots[0][half] + gates[1] * slots[1][half])
    return jnp.concatenate([pieces[c] for c in sorted(pieces)], axis=1)


def _combine_kernel(h_ref, route_ref, *refs):
    refs[-1][...] = _combined_rows(h_ref, route_ref, refs[:-1])


def _combine(h, route, pairs):
    T, D = h.shape
    W = pairs[0].shape[1]
    tm = min(ROW_TILE, T)
    return pl.pallas_call(
        _combine_kernel,
        grid=(T // tm,),
        in_specs=[pl.BlockSpec((tm, D), lambda i: (i, 0)), pl.BlockSpec((tm, LANES), lambda i: (i, 0))]
                 + [pl.BlockSpec((tm, W), lambda i: (i, 0))] * len(pairs),
        out_specs=pl.BlockSpec((tm, D), lambda i: (i, 0)),
        out_shape=jax.ShapeDtypeStruct((T, D), F32),
        compiler_params=_params("parallel"),
        name="combine",
    )(h, route, *pairs)


def _moe(h1, hn, route, route_t, counts, layer, w_gate, w_up, w_down, combine):
    T, D = h1.shape
    A = 2 * T
    nblk = -(-A // MOE_BLOCK) + N_EXPERTS
    P = nblk * MOE_BLOCK
    eid = route_t[ROUTE_E1:ROUTE_E2 + 1].astype(jnp.int32)
    rank = route_t[ROUTE_R1:ROUTE_R2 + 1].astype(jnp.int32)
    cnt = counts[0, :N_EXPERTS].astype(jnp.int32)
    padded = (cnt + MOE_BLOCK - 1) // MOE_BLOCK * MOE_BLOCK
    pends = jnp.cumsum(padded)
    pstarts = pends - padded
    experts = jnp.arange(N_EXPERTS, dtype=jnp.int32)
    start_of = jnp.sum(jnp.where(eid[:, None, :] == experts[None, :, None], pstarts[None, :, None], 0), axis=1)
    dest = start_of + rank
    blk_start = jnp.arange(nblk, dtype=jnp.int32) * MOE_BLOCK
    blk_exp = jnp.minimum(jnp.sum((pends[None, :] <= blk_start[:, None]).astype(jnp.int32), axis=1),
                          N_EXPERTS - 1)
    n_used = pends[-1:] // MOE_BLOCK

    gap_start = jnp.concatenate([pstarts + cnt, pends[-1:]])
    gap_len = jnp.concatenate([padded - cnt, P - pends[-1:]])
    gap_end = jnp.cumsum(gap_len)
    j = jnp.arange(P - A, dtype=jnp.int32)
    gap_of = jnp.sum((gap_end[None, :] <= j[:, None]).astype(jnp.int32), axis=1)
    sel = gap_of[:, None] == jnp.arange(N_EXPERTS + 1, dtype=jnp.int32)[None, :]
    pad_idx = jnp.sum(jnp.where(sel, (gap_start - (gap_end - gap_len))[None, :] + j[:, None], 0), axis=1)

    xs = _dispatch(hn, dest, pad_idx.reshape(1, P - A), P)
    ys = _experts(xs, blk_exp, n_used, layer, w_gate, w_up, w_down)
    pairs = _gather_pairs(ys, dest)
    return _combine(h1, route, pairs) if combine else (h1, route, pairs)


def _qkv_kernel(*refs, rate, n, n_chunks, n_pending, permute):
    n_x = 2 + n_pending if n_pending else 1
    x_refs = refs[:n_x]
    gq_ref, gkv_ref, wq_ref, wk_ref, wvt_ref, qn_ref, kn_ref, seg_ref = refs[n_x:n_x + 8]
    perm_ref = refs[n_x + 8] if permute else None
    q_ref, k_ref, vt_ref = refs[n_x + 8 + permute:n_x + 11 + permute]
    xs_ref = refs[-1]
    seg = seg_ref[...]
    width = seg.shape[0]
    res_per_chunk = q_ref.shape[1]
    chunk = pl.program_id(2)

    def head_norm(t, gain):
        cols = []
        for j in range(t.shape[1] // width):
            tj = t[:, j * width:(j + 1) * width]
            ms = jnp.dot((tj * tj).astype(BF16), seg, preferred_element_type=F32)
            cols.append(tj * lax.rsqrt(ms + EPS))
        return jnp.concatenate(cols, axis=1) * gain

    def normed(x):
        y = _rms(x)
        return (y * gq_ref[...]).astype(BF16), (y * gkv_ref[...]).astype(BF16)

    def project(xq, xkv):
        q = head_norm(jnp.dot(xq, wq_ref[...], preferred_element_type=F32), qn_ref[...]).astype(q_ref.dtype)
        k = head_norm(jnp.dot(xkv, wk_ref[...], preferred_element_type=F32), kn_ref[...]).astype(k_ref.dtype)
        vt = lax.dot_general(wvt_ref[...], xkv, NT_DIMS, preferred_element_type=F32).astype(vt_ref.dtype)
        for j in range(res_per_chunk):
            q_ref[0, j] = q[j * n:(j + 1) * n]
            k_ref[0, j] = k[j * n:(j + 1) * n]
            vt_ref[0, j] = vt[:, j * n:(j + 1) * n]

    if n_pending:
        x = _combined_rows(x_refs[0], x_refs[1], x_refs[2:])
        refs[-2][...] = x
        project(*normed(x))
        return
    x_ref, = x_refs
    if rate == 1:
        project(*normed(x_ref[...]))
        return

    if permute:
        group = perm_ref.shape[0]
        n_groups, run = x_ref.shape[0] // group, group // rate

        @pl.when(chunk == 0)
        def _():
            for g in range(n_groups):
                rows = slice(g * group, (g + 1) * group)
                for t, xt in enumerate(normed(x_ref[rows, :])):
                    xs_ref[t, rows, :] = jnp.dot(perm_ref[...], xt, preferred_element_type=F32).astype(BF16)

        span = res_per_chunk * run
        start = pl.multiple_of(chunk * span, span)

        def residue_major(t):
            parts = [xs_ref[t, pl.ds(g * group + start, span), :] for g in range(n_groups)]
            return jnp.concatenate([p[j * run:(j + 1) * run] for j in range(res_per_chunk) for p in parts], axis=0)

        project(residue_major(0), residue_major(1))
        return

    @pl.when(chunk == 0)
    def _():
        for j in range(xs_ref.shape[0]):
            xs_ref[j] = x_ref[:, j * LANES:(j + 1) * LANES]

    for ch in range(n_chunks):
        @pl.when(chunk == ch)
        def _(ch=ch):
            residues = range(ch * res_per_chunk, (ch + 1) * res_per_chunk)
            project(*normed(jnp.concatenate(
                [jnp.concatenate([xs_ref[j, pl.ds(c, n, stride=rate), :] for j in range(xs_ref.shape[0])], axis=1)
                 for c in residues], axis=0)))


def _qkv(h, B, S, rate, gq, gkv, wq, wk, wv, qn, kn):
    pending = h if isinstance(h, tuple) else None
    D = (pending[0] if pending else h).shape[1]
    L = S // rate
    hd = DIL_HEAD_DIM
    n = max(SUB_BLOCK, QKV_TILE // rate)
    tm = n * rate
    res_per_chunk = max(1, QKV_TILE // n)
    n_chunks = rate // res_per_chunk
    width = 2 * LANES
    ii = jnp.arange(width)
    seg = jnp.where((ii[:, None] // hd) == (ii[None, :] // hd), 1.0 / hd, 0.0).astype(BF16)
    permute = rate * BF16_ROWS == PERM_ROWS
    if permute:
        pp = jnp.arange(PERM_ROWS)
        perm = ((pp[:, None] // BF16_ROWS == pp[None, :] % rate)
                & (pp[:, None] % BF16_ROWS == pp[None, :] // rate)).astype(BF16)
    row = lambda g: jnp.tile(g, D // hd).reshape(1, D)
    const = lambda shape: _resident(shape, lambda b, i, c: (0,) * len(shape))
    qk_spec = pl.BlockSpec((1, res_per_chunk, n, D), lambda b, i, c: (b, c, i, 0))
    qk_shape = jax.ShapeDtypeStruct((B, rate, L, D), BF16)
    rows = lambda width: pl.BlockSpec((tm, width), lambda b, i, c: (b * (S // tm) + i, 0))
    if pending:
        assert rate == 1
        h1, route, pairs = pending
        x_args = (h1, route, *pairs)
        x_specs = [rows(D), rows(LANES)] + [rows(pairs[0].shape[1])] * len(pairs)
        extra_specs, extra_shapes = [rows(D)], [jax.ShapeDtypeStruct((B * S, D), F32)]
    else:
        x_args, x_specs, extra_specs, extra_shapes = (h,), [rows(D)], [], []
    return pl.pallas_call(
        functools.partial(_qkv_kernel, rate=rate, n=n, n_chunks=n_chunks,
                          n_pending=len(pending[2]) if pending else 0, permute=permute),
        grid=(B, S // tm, n_chunks),
        in_specs=x_specs + [
                  const((1, D)), const((1, D)), const((D, D)), const((D, D)), const((D, D)),
                  const((1, D)), const((1, D)), const((width, width))]
                 + ([const((PERM_ROWS, PERM_ROWS))] if permute else []),
        out_specs=[qk_spec, qk_spec,
                   pl.BlockSpec((1, res_per_chunk, D, n), lambda b, i, c: (b, c, 0, i))] + extra_specs,
        out_shape=[qk_shape, qk_shape, jax.ShapeDtypeStruct((B, rate, D, L), BF16)] + extra_shapes,
        scratch_shapes=[pltpu.VMEM((2, tm, D), BF16) if permute else
                        pltpu.VMEM((D // LANES, tm if rate > 1 else 8, LANES), F32)],
        compiler_params=_params("parallel", "parallel", "arbitrary"),
        name="qkv_rate%d" % rate,
    )(*x_args, gq.reshape(1, D), gkv.reshape(1, D), wq, wk, wv.T, row(qn) * (hd ** -0.5 * LOG2E), row(kn), seg,
      *((perm,) if permute else ()))


def _attn_kernel(*refs, rate, with_prev):
    if not with_prev:
        q_ref, kc_ref, vc_ref, edge_ref, o_ref, lse_ref, os_ref = refs
    elif len(refs) == 10:
        q_ref, kc_ref, kp_ref, vc_ref, vp_ref, edge_ref, bias_ref, o_ref, lse_ref, os_ref = refs
    else:
        q_ref, kc_ref, kp_ref, vc_ref, vp_ref, edge_ref, o_ref, lse_ref, os_ref = refs
    Bk = SUB_BLOCK
    hd = DIL_HEAD_DIM
    n_pairs = DIL_HEADS // 2
    lane = lax.broadcasted_iota(jnp.int32, (Bk, LANES), 1)

    def block(load_q, load_k, load_vt, load_bias, store):
        out_t, lse_t = [], []
        for p in range(n_pairs):
            qp = load_q(p)
            zero = jnp.zeros_like(qp)
            q2 = jnp.concatenate([jnp.where(lane < hd, qp, zero), jnp.where(lane >= hd, qp, zero)], axis=0)
            s = lax.dot_general(load_k(p), q2, NT_DIMS, preferred_element_type=F32) + load_bias(p)
            m = jnp.max(s, axis=0, keepdims=True)
            pr = jnp.exp2(s - m)
            l = jnp.sum(pr, axis=0, keepdims=True)
            pb = pr.astype(BF16)
            vt = load_vt(p)
            out_t.append(jnp.dot(vt[:hd], pb[:, :Bk], preferred_element_type=F32) / l[:, :Bk])
            out_t.append(jnp.dot(vt[hd:], pb[:, Bk:], preferred_element_type=F32) / l[:, Bk:])
            lse = (m + jnp.log2(l)) * LN2
            lse_t += [lse[:, :Bk], lse[:, Bk:]]
        lse_t.append(jnp.zeros((LANES - DIL_HEADS, Bk), F32))
        store(jnp.concatenate(out_t, axis=0).T, jnp.concatenate(lse_t, axis=0).T)

    def pair_cols(p):
        return slice(p * LANES, (p + 1) * LANES)

    n_res, n_blocks = kc_ref.shape[1], kc_ref.shape[2] // Bk
    for c in range(n_res):
        for j in range(n_blocks):
            rows = slice(j * Bk, (j + 1) * Bk)
            if not with_prev:
                load_k = lambda p, c=c: kc_ref[0, c, :, pair_cols(p)]
                load_vt = lambda p, c=c: vc_ref[0, c, pair_cols(p), :]
                load_bias = lambda p: edge_ref[0, p]
            elif j == 0:
                load_k = lambda p, c=c: jnp.concatenate([kp_ref[0, c, :, pair_cols(p)],
                                                         kc_ref[0, c, :Bk, pair_cols(p)]], axis=0)
                load_vt = lambda p, c=c: jnp.concatenate([vp_ref[0, c, pair_cols(p), :],
                                                          vc_ref[0, c, pair_cols(p), :Bk]], axis=1)
                load_bias = lambda p: edge_ref[0, p]
            else:
                krows = slice((j - 1) * Bk, (j + 1) * Bk)
                load_k = lambda p, c=c, krows=krows: kc_ref[0, c, krows, pair_cols(p)]
                load_vt = lambda p, c=c, krows=krows: vc_ref[0, c, pair_cols(p), krows]
                load_bias = lambda p: bias_ref[p]

            if rate == 1:
                def store(o, lse, rows=rows):
                    o_ref[rows, :] = o.astype(o_ref.dtype)
                    lse_ref[rows, :] = lse
            else:
                def store(o, lse, out_rows=pl.ds(c + j * Bk * rate, Bk, stride=rate)):
                    for g in range(os_ref.shape[0]):
                        os_ref[g, out_rows, :] = o[:, pair_cols(g)]
                    lse_ref[out_rows, :] = lse

            block(lambda p, c=c, rows=rows: q_ref[0, c, rows, pair_cols(p)], load_k, load_vt, load_bias, store)
    if rate > 1:
        for g in range(os_ref.shape[0]):
            o_ref[:, pair_cols(g)] = os_ref[g].astype(o_ref.dtype)


def _t5_bucket(n):
    max_exact = NUM_BUCKETS // 2
    nf = jnp.maximum(n, max_exact).astype(F32)
    large = max_exact + (jnp.log(nf / max_exact) / math.log(MAX_DISTANCE / max_exact)
                         * (NUM_BUCKETS - max_exact)).astype(jnp.int32)
    large = jnp.minimum(large, NUM_BUCKETS - 1)
    return jnp.where(n < max_exact, n, large)


def _group_attention(q, k, vt, bias_table, rate, n_steps):
    B, _, L, D = q.shape
    S = L * rate
    Bk = SUB_BLOCK
    with_prev = L > Bk
    nk = 2 * Bk if with_prev else Bk
    n = max(Bk, ATTN_TILE // rate)
    tm = n * rate
    nt = S // tm
    n_pairs = DIL_HEADS // 2
    ql = jnp.arange(Bk, dtype=jnp.int32)[:, None]
    kl = jnp.arange(2 * Bk, dtype=jnp.int32)[None, :]
    steps = ql + Bk - kl
    bucket = _t5_bucket(jnp.maximum(steps, 0) * rate)
    buckets = jnp.arange(NUM_BUCKETS, dtype=jnp.int32)
    bias = jnp.sum(jnp.where(bucket[None, :, :, None] == buckets[:, None, None, None],
                             bias_table.astype(F32)[:, None, None, :], 0.0), axis=0)
    band = ((steps >= 0) & (steps <= n_steps))[:, :, None]
    first = (kl < Bk)[:, :, None]

    def layout(t):
        t = t[:, 2 * Bk - nk:, :].transpose(1, 2, 0)
        return t.reshape(nk, n_pairs, 2 * Bk).transpose(1, 0, 2)

    bias_in = layout(jnp.where(band, bias * LOG2E, NEG_INF))
    bias_first = layout(jnp.where(band & ~first, bias * LOG2E, NEG_INF))
    edge = jnp.stack([bias_first, bias_in])

    cur_qk = pl.BlockSpec((1, rate, n, D), lambda b, i: (b, 0, i, 0))
    cur_vt = pl.BlockSpec((1, rate, D, n), lambda b, i: (b, 0, 0, i))
    per_n = n // Bk
    prev_qk = pl.BlockSpec((1, rate, Bk, D), lambda b, i: (b, 0, jnp.maximum(i * per_n - 1, 0), 0))
    prev_vt = pl.BlockSpec((1, rate, D, Bk), lambda b, i: (b, 0, 0, jnp.maximum(i * per_n - 1, 0)))
    edge_spec = pl.BlockSpec((1, n_pairs, nk, 2 * Bk), lambda b, i: (jnp.minimum(i, 1), 0, 0, 0))
    if with_prev:
        in_specs = [cur_qk, cur_qk, prev_qk, cur_vt, prev_vt, edge_spec]
        args = (q, k, k, vt, vt, edge)
    else:
        in_specs = [cur_qk, cur_qk, cur_vt, edge_spec]
        args = (q, k, vt, edge)
    if with_prev and n > Bk:
        in_specs.append(_resident((n_pairs, nk, 2 * Bk), lambda b, i: (0, 0, 0)))
        args += (bias_in,)
    return pl.pallas_call(
        functools.partial(_attn_kernel, rate=rate, with_prev=with_prev),
        grid=(B, nt),
        in_specs=in_specs,
        out_specs=[pl.BlockSpec((tm, D), lambda b, i: (b * nt + i, 0)),
                   pl.BlockSpec((tm, LANES), lambda b, i: (b * nt + i, 0))],
        out_shape=[jax.ShapeDtypeStruct((B * S, D), BF16),
                   jax.ShapeDtypeStruct((B * S, LANES), F32)],
        scratch_shapes=[pltpu.VMEM((D // LANES, tm if rate > 1 else 8, LANES), F32)],
        compiler_params=_params("parallel", "parallel"),
        name="attn_rate%d" % rate,
    )(*args)


def _cast_kernel(x_ref, o_ref):
    cols = o_ref.shape[2]
    for j in range(o_ref.shape[0]):
        o_ref[j] = x_ref[:, j * cols:(j + 1) * cols].astype(o_ref.dtype)


def _bf16_column_blocks(w, cols):
    rows, n = w.shape[0], w.shape[1] // cols
    block = min(ROW_TILE // 4, rows)
    assert w.shape[1] == n * cols and rows % block == 0
    return pl.pallas_call(
        _cast_kernel, grid=(rows // block,),
        in_specs=[pl.BlockSpec((block, n * cols), lambda i: (i, 0))],
        out_specs=pl.BlockSpec((n, block, cols), lambda i: (0, i, 0)),
        out_shape=jax.ShapeDtypeStruct((n, rows, cols), BF16),
        compiler_params=_params("parallel"), name="bf16_column_blocks",
    )(w)


def kernel(x, ret_w_in, ret_w_out, kv_norm, w_kv, k_norm, dil_wq, q_norm, dil_wo, rel_bias,
           mixer_norm, ffn_norm, router_grp, router_grp_b, router_exp, router_exp_b,
           exp_gate, exp_up, exp_down):
    B, S, D = x.shape
    h = x.reshape(B * S, D)

    def moe_layer(layer, mixer, w_out, h, combine):
        h1, *hn, route, route_t, counts = _post(
            mixer, w_out.astype(BF16), h, ffn_norm[layer], router_grp[layer], router_grp_b[layer],
            router_exp[layer], router_exp_b[layer])
        return _moe(h1, hn, route, route_t, counts, layer, exp_gate, exp_up, exp_down, combine)

    h = moe_layer(0, (mixer_norm[0], ret_w_in[0].astype(BF16), S), ret_w_out[0], h, combine=False)

    G = len(DIL_RATES)
    gd = DIL_HEADS * DIL_HEAD_DIM
    outs, lses = [], []
    w_kv_blocks = _bf16_column_blocks(w_kv, gd)
    for g in range(G):
        cq = slice(g * gd, (g + 1) * gd)
        q, k, vt, *combined = _qkv(h, B, S, DIL_RATES[g], mixer_norm[1], kv_norm,
                                   dil_wq[0][:, cq].astype(BF16), w_kv_blocks[g],
                                   w_kv_blocks[G + g], q_norm[0][g], k_norm[g])
        if combined:
            h, = combined
        o, lse = _group_attention(q, k, vt, rel_bias[:, g * DIL_HEADS:(g + 1) * DIL_HEADS],
                                  DIL_RATES[g], DIL_WINDOWS[g] // DIL_RATES[g])
        outs.append(o)
        lses.append(lse)
    h = moe_layer(1, (tuple(outs), tuple(lses)), dil_wo[0], h, combine=True)
    return h.reshape(B, S, D)
```
